```python
import math
import jax, jax.numpy as jnp
from jax import lax
import numpy as np

D_MODEL = 1024
BATCH = 8
SEQ = 16384
DEPTH = 1

CHUNK = 64

D_MIX = D_MODEL
ATTN_HEADS = 8
HEAD_DIM = 64
D_ATTN = ATTN_HEADS * HEAD_DIM
D_CONV = D_MIX - D_ATTN
CONV_WIDTH = 3
Q_BLOCK = 128
D_IN_PROJ = 3 * D_ATTN + ATTN_HEADS + 3 * D_CONV

D_FF = 2816
PLE_DIM = 256
LN_EPS = 1e-5
RMS_EPS = 1e-6
NEG_INF = -1e30

DEEPNORM_ALPHA = (2.0 * DEPTH) ** 0.25
DEEPNORM_BETA = (8.0 * DEPTH) ** -0.25

kernel_name = "hybrid_fox_shortconv_macaron_deepnorm"


def _layer_norm(x, g, b):
    xf = x.astype(jnp.float32)
    mu = jnp.mean(xf, axis=-1, keepdims=True)
    xc = xf - mu
    var = jnp.mean(xc * xc, axis=-1, keepdims=True)
    y = xc * lax.rsqrt(var + LN_EPS) * g.astype(jnp.float32) + b.astype(jnp.float32)
    return y.astype(x.dtype)


def _rms_norm(x, g):
    xf = x.astype(jnp.float32)
    ms = jnp.mean(xf * xf, axis=-1, keepdims=True)
    return (xf * lax.rsqrt(ms + RMS_EPS) * g.astype(jnp.float32)).astype(x.dtype)


def _swiglu(x, w_in, w_out):
    gu = x @ w_in
    gate, up = jnp.split(gu, 2, axis=-1)
    return (jax.nn.silu(gate) * up) @ w_out


def _forgetting_attention(q, k, v, log_f):
    b, s, h, dh = q.shape
    nb = s // Q_BLOCK
    scale = 1.0 / math.sqrt(dh)
    c = jnp.cumsum(log_f, axis=1).transpose(0, 2, 1)
    qh = q.transpose(0, 2, 1, 3)
    kh = k.transpose(0, 2, 1, 3)
    vh = v.transpose(0, 2, 1, 3)
    q_blocks = qh.reshape(b, h, nb, Q_BLOCK, dh).transpose(2, 0, 1, 3, 4)
    c_blocks = c.reshape(b, h, nb, Q_BLOCK).transpose(2, 0, 1, 3)
    key_pos = jnp.arange(s)

    def one_block(args):
        qb, cb, blk = args
        q_pos = blk * Q_BLOCK + jnp.arange(Q_BLOCK)
        logits = jnp.einsum('bhqd,bhkd->bhqk', qb, kh,
                            preferred_element_type=jnp.float32) * scale
        logits = logits + cb[..., None] - c[:, :, None, :]
        logits = jnp.where(q_pos[:, None] >= key_pos[None, :], logits, NEG_INF)
        probs = jax.nn.softmax(logits, axis=-1)
        return jnp.einsum('bhqk,bhkd->bhqd', probs.astype(vh.dtype), vh)

    out = lax.map(one_block, (q_blocks, c_blocks, jnp.arange(nb)))
    return out.transpose(1, 0, 3, 2, 4).reshape(b, s, h * dh)


def _short_gated_conv(gate_b, gate_c, h_in, conv_w):
    u = gate_c * h_in
    y = lax.conv_general_dilated(
        u, conv_w[:, None, :].astype(u.dtype), window_strides=(1,),
        padding=[(CONV_WIDTH - 1, 0)],
        dimension_numbers=('NWC', 'WIO', 'NWC'),
        feature_group_count=D_CONV)
    return gate_b * y


def _hybrid_mixer(x, w_mix_in, b_forget, conv_w, g_attn, g_conv, w_mix_out):
    b, s, _ = x.shape
    proj = x @ w_mix_in
    o = 0
    q = proj[..., o:o + D_ATTN]; o += D_ATTN
    k = proj[..., o:o + D_ATTN]; o += D_ATTN
    v = proj[..., o:o + D_ATTN]; o += D_ATTN
    f_logit = proj[..., o:o + ATTN_HEADS]; o += ATTN_HEADS
    gate_b = proj[..., o:o + D_CONV]; o += D_CONV
    gate_c = proj[..., o:o + D_CONV]; o += D_CONV
    h_in = proj[..., o:o + D_CONV]

    log_f = jax.nn.log_sigmoid((f_logit + b_forget).astype(jnp.float32))
    attn = _forgetting_attention(
        q.reshape(b, s, ATTN_HEADS, HEAD_DIM),
        k.reshape(b, s, ATTN_HEADS, HEAD_DIM),
        v.reshape(b, s, ATTN_HEADS, HEAD_DIM),
        log_f)
    conv = _short_gated_conv(gate_b, gate_c, h_in, conv_w)

    merged = jnp.concatenate([_rms_norm(attn, g_attn), _rms_norm(conv, g_conv)], axis=-1)
    return merged @ w_mix_out


def _fwd_setup_inputs(seed: int = 0) -> dict:
    key = jax.random.key(seed)
    ks = jax.random.split(key, 26)
    L, D, F = DEPTH, D_MODEL, D_FF
    f32 = jnp.float32

    def nrm(k, shape, scale):
        return jax.random.normal(k, shape, f32) * scale

    def gain(k):
        return 1.0 + 0.02 * jax.random.normal(k, (L, D), f32)

    def bias(k, n):
        return 0.02 * jax.random.normal(k, (L, n), f32)

    b_forget = (jnp.linspace(1.0, 5.0, ATTN_HEADS, dtype=f32)[None, :]
                + 0.1 * jax.random.normal(ks[6], (L, ATTN_HEADS), f32))

    return {
        "x": nrm(ks[0], (BATCH, SEQ, D), 1.0),
        "p": nrm(ks[1], (DEPTH, BATCH, SEQ, PLE_DIM), 1.0),
        "ffn1_w_in": nrm(ks[2], (L, D, 2 * F), D ** -0.5),
        "ffn1_w_out": nrm(ks[3], (L, F, D), F ** -0.5 * DEEPNORM_BETA),
        "ln1_g": gain(ks[4]), "ln1_b": bias(ks[5], D),
        "w_mix_in": nrm(ks[7], (L, D, D_IN_PROJ), D ** -0.5),
        "b_forget": b_forget,
        "conv_w": nrm(ks[8], (L, CONV_WIDTH, D_CONV), CONV_WIDTH ** -0.5),
        "g_attn": 1.0 + 0.02 * jax.random.normal(ks[9], (L, D_ATTN), f32),
        "g_conv": 1.0 + 0.02 * jax.random.normal(ks[10], (L, D_CONV), f32),
        "w_mix_out": nrm(ks[11], (L, D_MIX, D), D_MIX ** -0.5 * DEEPNORM_BETA),
        "ln2_g": gain(ks[12]), "ln2_b": bias(ks[13], D),
        "ffn2_w_in": nrm(ks[14], (L, D, 2 * F), D ** -0.5),
        "ffn2_w_out": nrm(ks[15], (L, F, D), F ** -0.5 * DEEPNORM_BETA),
        "ln3_g": gain(ks[16]), "ln3_b": bias(ks[17], D),
        "w_ple": nrm(ks[18], (L, PLE_DIM, D), PLE_DIM ** -0.5 * DEEPNORM_BETA),
        "w_ple_gate": nrm(ks[19], (L, D, D), D ** -0.5),
        "b_ple_gate": bias(ks[20], D),
        "ln4_g": gain(ks[21]), "ln4_b": bias(ks[22], D),
    }


def _fwd_reference(x, p, ffn1_w_in, ffn1_w_out, ln1_g, ln1_b, w_mix_in, b_forget, conv_w,
              g_attn, g_conv, w_mix_out, ln2_g, ln2_b, ffn2_w_in, ffn2_w_out, ln3_g, ln3_b,
              w_ple, w_ple_gate, b_ple_gate, ln4_g, ln4_b):
    a = DEEPNORM_ALPHA
    for i in range(DEPTH):
        x = _layer_norm(a * x + 0.5 * _swiglu(x, ffn1_w_in[i], ffn1_w_out[i]), ln1_g[i], ln1_b[i])
        mix = _hybrid_mixer(x, w_mix_in[i], b_forget[i], conv_w[i], g_attn[i], g_conv[i], w_mix_out[i])
        x = _layer_norm(a * x + mix, ln2_g[i], ln2_b[i])
        x = _layer_norm(a * x + 0.5 * _swiglu(x, ffn2_w_in[i], ffn2_w_out[i]), ln3_g[i], ln3_b[i])
        gate = jax.nn.sigmoid(x @ w_ple_gate[i] + b_ple_gate[i])
        x = _layer_norm(a * x + gate * (p[i] @ w_ple[i]), ln4_g[i], ln4_b[i])
    return x


import jax as _jax
import jax.numpy as _jnp

TWIN_FORMAT = 'train_step'
FWD_PARAMS = ['x', 'p', 'ffn1_w_in', 'ffn1_w_out', 'ln1_g', 'ln1_b', 'w_mix_in', 'b_forget', 'conv_w', 'g_attn', 'g_conv', 'w_mix_out', 'ln2_g', 'ln2_b', 'ffn2_w_in', 'ffn2_w_out', 'ln3_g', 'ln3_b', 'w_ple', 'w_ple_gate', 'b_ple_gate', 'ln4_g', 'ln4_b']
TWIN_WEIGHTS = ['ffn1_w_in', 'ffn1_w_out', 'ln1_g', 'ln1_b', 'w_mix_in', 'b_forget', 'conv_w', 'g_attn', 'g_conv', 'w_mix_out', 'ln2_g', 'ln2_b', 'ffn2_w_in', 'ffn2_w_out', 'ln3_g', 'ln3_b', 'w_ple', 'w_ple_gate', 'b_ple_gate', 'ln4_g', 'ln4_b']
TWIN_DIFF_INPUT = 'x'
TWIN_INPUTS = ['x', 'p', 'ffn1_w_in', 'ffn1_w_out', 'ln1_g', 'ln1_b', 'w_mix_in', 'b_forget', 'conv_w', 'g_attn', 'g_conv', 'w_mix_out', 'ln2_g', 'ln2_b', 'ffn2_w_in', 'ffn2_w_out', 'ln3_g', 'ln3_b', 'w_ple', 'w_ple_gate', 'b_ple_gate', 'ln4_g', 'ln4_b', 'loss_target', 'm_ffn1_w_in', 'm_ffn1_w_out', 'm_ln1_g', 'm_ln1_b', 'm_w_mix_in', 'm_b_forget', 'm_conv_w', 'm_g_attn', 'm_g_conv', 'm_w_mix_out', 'm_ln2_g', 'm_ln2_b', 'm_ffn2_w_in', 'm_ffn2_w_out', 'm_ln3_g', 'm_ln3_b', 'm_w_ple', 'm_w_ple_gate', 'm_b_ple_gate', 'm_ln4_g', 'm_ln4_b', 'v_ffn1_w_in', 'v_ffn1_w_out', 'v_ln1_g', 'v_ln1_b', 'v_w_mix_in', 'v_b_forget', 'v_conv_w', 'v_g_attn', 'v_g_conv', 'v_w_mix_out', 'v_ln2_g', 'v_ln2_b', 'v_ffn2_w_in', 'v_ffn2_w_out', 'v_ln3_g', 'v_ln3_b', 'v_w_ple', 'v_w_ple_gate', 'v_b_ple_gate', 'v_ln4_g', 'v_ln4_b']
TWIN_OUTPUTS = ['loss', 'grad_x', 'grad_ffn1_w_in', 'grad_ffn1_w_out', 'grad_ln1_g', 'grad_ln1_b', 'grad_w_mix_in', 'grad_b_forget', 'grad_conv_w', 'grad_g_attn', 'grad_g_conv', 'grad_w_mix_out', 'grad_ln2_g', 'grad_ln2_b', 'grad_ffn2_w_in', 'grad_ffn2_w_out', 'grad_ln3_g', 'grad_ln3_b', 'grad_w_ple', 'grad_w_ple_gate', 'grad_b_ple_gate', 'grad_ln4_g', 'grad_ln4_b', 'delta_ffn1_w_in', 'delta_ffn1_w_out', 'delta_ln1_g', 'delta_ln1_b', 'delta_w_mix_in', 'delta_b_forget', 'delta_conv_w', 'delta_g_attn', 'delta_g_conv', 'delta_w_mix_out', 'delta_ln2_g', 'delta_ln2_b', 'delta_ffn2_w_in', 'delta_ffn2_w_out', 'delta_ln3_g', 'delta_ln3_b', 'delta_w_ple', 'delta_w_ple_gate', 'delta_b_ple_gate', 'delta_ln4_g', 'delta_ln4_b', 'new_m_ffn1_w_in', 'new_m_ffn1_w_out', 'new_m_ln1_g', 'new_m_ln1_b', 'new_m_w_mix_in', 'new_m_b_forget', 'new_m_conv_w', 'new_m_g_attn', 'new_m_g_conv', 'new_m_w_mix_out', 'new_m_ln2_g', 'new_m_ln2_b', 'new_m_ffn2_w_in', 'new_m_ffn2_w_out', 'new_m_ln3_g', 'new_m_ln3_b', 'new_m_w_ple', 'new_m_w_ple_gate', 'new_m_b_ple_gate', 'new_m_ln4_g', 'new_m_ln4_b', 'new_v_ffn1_w_in', 'new_v_ffn1_w_out', 'new_v_ln1_g', 'new_v_ln1_b', 'new_v_w_mix_in', 'new_v_b_forget', 'new_v_conv_w', 'new_v_g_attn', 'new_v_g_conv', 'new_v_w_mix_out', 'new_v_ln2_g', 'new_v_ln2_b', 'new_v_ffn2_w_in', 'new_v_ffn2_w_out', 'new_v_ln3_g', 'new_v_ln3_b', 'new_v_w_ple', 'new_v_w_ple_gate', 'new_v_b_ple_gate', 'new_v_ln4_g', 'new_v_ln4_b']
TWIN_LEAF_KINDS = {'loss': 'loss', 'grad_x': 'grad_x', 'grad_ffn1_w_in': 'grad_w', 'grad_ffn1_w_out': 'grad_w', 'grad_ln1_g': 'grad_w', 'grad_ln1_b': 'grad_w', 'grad_w_mix_in': 'grad_w', 'grad_b_forget': 'grad_w', 'grad_conv_w': 'grad_w', 'grad_g_attn': 'grad_w', 'grad_g_conv': 'grad_w', 'grad_w_mix_out': 'grad_w', 'grad_ln2_g': 'grad_w', 'grad_ln2_b': 'grad_w', 'grad_ffn2_w_in': 'grad_w', 'grad_ffn2_w_out': 'grad_w', 'grad_ln3_g': 'grad_w', 'grad_ln3_b': 'grad_w', 'grad_w_ple': 'grad_w', 'grad_w_ple_gate': 'grad_w', 'grad_b_ple_gate': 'grad_w', 'grad_ln4_g': 'grad_w', 'grad_ln4_b': 'grad_w', 'delta_ffn1_w_in': 'delta_w', 'delta_ffn1_w_out': 'delta_w', 'delta_ln1_g': 'delta_w', 'delta_ln1_b': 'delta_w', 'delta_w_mix_in': 'delta_w', 'delta_b_forget': 'delta_w', 'delta_conv_w': 'delta_w', 'delta_g_attn': 'delta_w', 'delta_g_conv': 'delta_w', 'delta_w_mix_out': 'delta_w', 'delta_ln2_g': 'delta_w', 'delta_ln2_b': 'delta_w', 'delta_ffn2_w_in': 'delta_w', 'delta_ffn2_w_out': 'delta_w', 'delta_ln3_g': 'delta_w', 'delta_ln3_b': 'delta_w', 'delta_w_ple': 'delta_w', 'delta_w_ple_gate': 'delta_w', 'delta_b_ple_gate': 'delta_w', 'delta_ln4_g': 'delta_w', 'delta_ln4_b': 'delta_w', 'new_m_ffn1_w_in': 'new_m', 'new_m_ffn1_w_out': 'new_m', 'new_m_ln1_g': 'new_m', 'new_m_ln1_b': 'new_m', 'new_m_w_mix_in': 'new_m', 'new_m_b_forget': 'new_m', 'new_m_conv_w': 'new_m', 'new_m_g_attn': 'new_m', 'new_m_g_conv': 'new_m', 'new_m_w_mix_out': 'new_m', 'new_m_ln2_g': 'new_m', 'new_m_ln2_b': 'new_m', 'new_m_ffn2_w_in': 'new_m', 'new_m_ffn2_w_out': 'new_m', 'new_m_ln3_g': 'new_m', 'new_m_ln3_b': 'new_m', 'new_m_w_ple': 'new_m', 'new_m_w_ple_gate': 'new_m', 'new_m_b_ple_gate': 'new_m', 'new_m_ln4_g': 'new_m', 'new_m_ln4_b': 'new_m', 'new_v_ffn1_w_in': 'new_v', 'new_v_ffn1_w_out': 'new_v', 'new_v_ln1_g': 'new_v', 'new_v_ln1_b': 'new_v', 'new_v_w_mix_in': 'new_v', 'new_v_b_forget': 'new_v', 'new_v_conv_w': 'new_v', 'new_v_g_attn': 'new_v', 'new_v_g_conv': 'new_v', 'new_v_w_mix_out': 'new_v', 'new_v_ln2_g': 'new_v', 'new_v_ln2_b': 'new_v', 'new_v_ffn2_w_in': 'new_v', 'new_v_ffn2_w_out': 'new_v', 'new_v_ln3_g': 'new_v', 'new_v_ln3_b': 'new_v', 'new_v_w_ple': 'new_v', 'new_v_w_ple_gate': 'new_v', 'new_v_b_ple_gate': 'new_v', 'new_v_ln4_g': 'new_v', 'new_v_ln4_b': 'new_v'}


def _forward(args):
    return _fwd_reference(*[args[k] for k in FWD_PARAMS])


def _output_shape():
    def fwd():
        inp = _fwd_setup_inputs(0)
        return _fwd_reference(*[inp[k] for k in FWD_PARAMS])
    out = _jax.eval_shape(fwd)
    return out.shape, out.dtype

N_MICROBATCH = 1
ADAM_LR = 0.001
ADAM_B1 = 0.9
ADAM_B2 = 0.999
ADAM_EPS = 1e-08
ADAM_WD = 0.01
ADAM_STEP = 10
PER_EXAMPLE_BATCH_AXIS = {'x': 0, 'p': 1, 'loss_target': 0}
SHARED_INPUTS = []
_WEIGHT_DTYPES = {'ffn1_w_in': _jnp.float32, 'ffn1_w_out': _jnp.float32, 'ln1_g': _jnp.float32, 'ln1_b': _jnp.float32, 'w_mix_in': _jnp.float32, 'b_forget': _jnp.float32, 'conv_w': _jnp.float32, 'g_attn': _jnp.float32, 'g_conv': _jnp.float32, 'w_mix_out': _jnp.float32, 'ln2_g': _jnp.float32, 'ln2_b': _jnp.float32, 'ffn2_w_in': _jnp.float32, 'ffn2_w_out': _jnp.float32, 'ln3_g': _jnp.float32, 'ln3_b': _jnp.float32, 'w_ple': _jnp.float32, 'w_ple_gate': _jnp.float32, 'b_ple_gate': _jnp.float32, 'ln4_g': _jnp.float32, 'ln4_b': _jnp.float32}
MOMENT_SCALE = {'ffn1_w_in': 3.645454e-02, 'ffn1_w_out': 9.995411e-02, 'ln1_g': 2.885860e+00, 'ln1_b': 2.354399e+00, 'w_mix_in': 1.475389e-01, 'b_forget': 1.286403e+00, 'conv_w': 1.669603e-01, 'g_attn': 1.538582e-01, 'g_conv': 1.541437e-01, 'w_mix_out': 2.657394e-01, 'ln2_g': 3.902594e+00, 'ln2_b': 1.754031e+00, 'ffn2_w_in': 3.140958e-02, 'ffn2_w_out': 8.615414e-02, 'ln3_g': 4.031785e+00, 'ln3_b': 1.787934e+00, 'w_ple': 1.562593e-01, 'w_ple_gate': 3.621818e-02, 'b_ple_gate': 7.103664e-02, 'ln4_g': 1.283402e+02, 'ln4_b': 4.023571e+00}


def _to_microbatches(a, axis):
    t = _jnp.moveaxis(a, axis, 0)
    t = t.reshape((N_MICROBATCH, t.shape[0] // N_MICROBATCH) + t.shape[1:])
    return _jnp.moveaxis(t, 1, axis + 1)


def setup_inputs(seed: int = 0) -> dict:
    inp = _fwd_setup_inputs(seed)
    key = _jax.random.fold_in(_jax.random.key(seed), 7919)
    shape, _ = _output_shape()
    out = dict(inp)
    out["loss_target"] = _jax.random.normal(_jax.random.fold_in(key, 0), shape, _jnp.float32)
    for i, name in enumerate(TWIN_WEIGHTS):
        w = inp[name].astype(_jnp.float32)
        if MOMENT_SCALE is None:
            s = _jnp.sqrt(_jnp.mean(_jnp.square(w)) + 1e-30)
        else:
            s = MOMENT_SCALE[name]
        km, kv = _jax.random.split(_jax.random.fold_in(key, i + 1))
        out[name] = w
        out["m_" + name] = s * _jax.random.normal(km, w.shape, _jnp.float32)
        out["v_" + name] = (s * s) * _jax.random.uniform(kv, w.shape, _jnp.float32, 0.5, 1.5)
    if N_MICROBATCH > 1:
        for name, axis in PER_EXAMPLE_BATCH_AXIS.items():
            out[name] = _to_microbatches(out[name], axis)
    return {'x': out['x'], 'p': out['p'], 'ffn1_w_in': out['ffn1_w_in'], 'ffn1_w_out': out['ffn1_w_out'], 'ln1_g': out['ln1_g'], 'ln1_b': out['ln1_b'], 'w_mix_in': out['w_mix_in'], 'b_forget': out['b_forget'], 'conv_w': out['conv_w'], 'g_attn': out['g_attn'], 'g_conv': out['g_conv'], 'w_mix_out': out['w_mix_out'], 'ln2_g': out['ln2_g'], 'ln2_b': out['ln2_b'], 'ffn2_w_in': out['ffn2_w_in'], 'ffn2_w_out': out['ffn2_w_out'], 'ln3_g': out['ln3_g'], 'ln3_b': out['ln3_b'], 'w_ple': out['w_ple'], 'w_ple_gate': out['w_ple_gate'], 'b_ple_gate': out['b_ple_gate'], 'ln4_g': out['ln4_g'], 'ln4_b': out['ln4_b'], 'loss_target': out['loss_target'], 'm_ffn1_w_in': out['m_ffn1_w_in'], 'm_ffn1_w_out': out['m_ffn1_w_out'], 'm_ln1_g': out['m_ln1_g'], 'm_ln1_b': out['m_ln1_b'], 'm_w_mix_in': out['m_w_mix_in'], 'm_b_forget': out['m_b_forget'], 'm_conv_w': out['m_conv_w'], 'm_g_attn': out['m_g_attn'], 'm_g_conv': out['m_g_conv'], 'm_w_mix_out': out['m_w_mix_out'], 'm_ln2_g': out['m_ln2_g'], 'm_ln2_b': out['m_ln2_b'], 'm_ffn2_w_in': out['m_ffn2_w_in'], 'm_ffn2_w_out': out['m_ffn2_w_out'], 'm_ln3_g': out['m_ln3_g'], 'm_ln3_b': out['m_ln3_b'], 'm_w_ple': out['m_w_ple'], 'm_w_ple_gate': out['m_w_ple_gate'], 'm_b_ple_gate': out['m_b_ple_gate'], 'm_ln4_g': out['m_ln4_g'], 'm_ln4_b': out['m_ln4_b'], 'v_ffn1_w_in': out['v_ffn1_w_in'], 'v_ffn1_w_out': out['v_ffn1_w_out'], 'v_ln1_g': out['v_ln1_g'], 'v_ln1_b': out['v_ln1_b'], 'v_w_mix_in': out['v_w_mix_in'], 'v_b_forget': out['v_b_forget'], 'v_conv_w': out['v_conv_w'], 'v_g_attn': out['v_g_attn'], 'v_g_conv': out['v_g_conv'], 'v_w_mix_out': out['v_w_mix_out'], 'v_ln2_g': out['v_ln2_g'], 'v_ln2_b': out['v_ln2_b'], 'v_ffn2_w_in': out['v_ffn2_w_in'], 'v_ffn2_w_out': out['v_ffn2_w_out'], 'v_ln3_g': out['v_ln3_g'], 'v_ln3_b': out['v_ln3_b'], 'v_w_ple': out['v_w_ple'], 'v_w_ple_gate': out['v_w_ple_gate'], 'v_b_ple_gate': out['v_b_ple_gate'], 'v_ln4_g': out['v_ln4_g'], 'v_ln4_b': out['v_ln4_b']}


def _loss(weights, diff, rest, loss_target):
    with _jax.named_scope("forward"):
        args = {**rest, TWIN_DIFF_INPUT: diff, **{k: w.astype(_WEIGHT_DTYPES[k]) for k, w in weights.items()}}
        y = _forward(args)
    with _jax.named_scope("loss_head"):
        err = _jnp.square(y.astype(_jnp.float32) - loss_target)
        return 0.5 * _jnp.sum(_jnp.mean(err, axis=-1)) if err.ndim else 0.5 * err


def _adamw(w, g, m, v):
    m = ADAM_B1 * m + (1.0 - ADAM_B1) * g
    v = ADAM_B2 * v + (1.0 - ADAM_B2) * _jnp.square(g)
    m_hat = m / (1.0 - ADAM_B1 ** ADAM_STEP)
    v_hat = v / (1.0 - ADAM_B2 ** ADAM_STEP)
    delta = -ADAM_LR * (m_hat / (_jnp.sqrt(v_hat) + ADAM_EPS) + ADAM_WD * w)
    return delta, m, v


def reference(x, p, ffn1_w_in, ffn1_w_out, ln1_g, ln1_b, w_mix_in, b_forget, conv_w, g_attn, g_conv, w_mix_out, ln2_g, ln2_b, ffn2_w_in, ffn2_w_out, ln3_g, ln3_b, w_ple, w_ple_gate, b_ple_gate, ln4_g, ln4_b, loss_target, m_ffn1_w_in, m_ffn1_w_out, m_ln1_g, m_ln1_b, m_w_mix_in, m_b_forget, m_conv_w, m_g_attn, m_g_conv, m_w_mix_out, m_ln2_g, m_ln2_b, m_ffn2_w_in, m_ffn2_w_out, m_ln3_g, m_ln3_b, m_w_ple, m_w_ple_gate, m_b_ple_gate, m_ln4_g, m_ln4_b, v_ffn1_w_in, v_ffn1_w_out, v_ln1_g, v_ln1_b, v_w_mix_in, v_b_forget, v_conv_w, v_g_attn, v_g_conv, v_w_mix_out, v_ln2_g, v_ln2_b, v_ffn2_w_in, v_ffn2_w_out, v_ln3_g, v_ln3_b, v_w_ple, v_w_ple_gate, v_b_ple_gate, v_ln4_g, v_ln4_b):
    given = dict(x=x, p=p, ffn1_w_in=ffn1_w_in, ffn1_w_out=ffn1_w_out, ln1_g=ln1_g, ln1_b=ln1_b, w_mix_in=w_mix_in, b_forget=b_forget, conv_w=conv_w, g_attn=g_attn, g_conv=g_conv, w_mix_out=w_mix_out, ln2_g=ln2_g, ln2_b=ln2_b, ffn2_w_in=ffn2_w_in, ffn2_w_out=ffn2_w_out, ln3_g=ln3_g, ln3_b=ln3_b, w_ple=w_ple, w_ple_gate=w_ple_gate, b_ple_gate=b_ple_gate, ln4_g=ln4_g, ln4_b=ln4_b, loss_target=loss_target, m_ffn1_w_in=m_ffn1_w_in, m_ffn1_w_out=m_ffn1_w_out, m_ln1_g=m_ln1_g, m_ln1_b=m_ln1_b, m_w_mix_in=m_w_mix_in, m_b_forget=m_b_forget, m_conv_w=m_conv_w, m_g_attn=m_g_attn, m_g_conv=m_g_conv, m_w_mix_out=m_w_mix_out, m_ln2_g=m_ln2_g, m_ln2_b=m_ln2_b, m_ffn2_w_in=m_ffn2_w_in, m_ffn2_w_out=m_ffn2_w_out, m_ln3_g=m_ln3_g, m_ln3_b=m_ln3_b, m_w_ple=m_w_ple, m_w_ple_gate=m_w_ple_gate, m_b_ple_gate=m_b_ple_gate, m_ln4_g=m_ln4_g, m_ln4_b=m_ln4_b, v_ffn1_w_in=v_ffn1_w_in, v_ffn1_w_out=v_ffn1_w_out, v_ln1_g=v_ln1_g, v_ln1_b=v_ln1_b, v_w_mix_in=v_w_mix_in, v_b_forget=v_b_forget, v_conv_w=v_conv_w, v_g_attn=v_g_attn, v_g_conv=v_g_conv, v_w_mix_out=v_w_mix_out, v_ln2_g=v_ln2_g, v_ln2_b=v_ln2_b, v_ffn2_w_in=v_ffn2_w_in, v_ffn2_w_out=v_ffn2_w_out, v_ln3_g=v_ln3_g, v_ln3_b=v_ln3_b, v_w_ple=v_w_ple, v_w_ple_gate=v_w_ple_gate, v_b_ple_gate=v_b_ple_gate, v_ln4_g=v_ln4_g, v_ln4_b=v_ln4_b)
    weights = {n: given[n] for n in TWIN_WEIGHTS}
    shared = {n: given[n] for n in SHARED_INPUTS}
    per_example = {n: given[n] for n in ['x', 'p']}
    grad_fn = _jax.value_and_grad(_loss, argnums=(0, 1))

    def one_microbatch(ex, loss_target):
        ex = dict(ex)
        diff = ex.pop(TWIN_DIFF_INPUT)
        return grad_fn(weights, diff, {**shared, **ex}, loss_target)

    if N_MICROBATCH == 1:
        loss, (grad_w, grad_x) = one_microbatch(per_example, given["loss_target"])
    else:
        def body(carry, xs):
            loss_sum, grad_sum = carry
            l_k, (gw_k, gx_k) = one_microbatch(xs[0], xs[1])
            with _jax.named_scope("update"):
                return (loss_sum + l_k, _jax.tree.map(_jnp.add, grad_sum, gw_k)), gx_k

        init = (_jnp.zeros((), _jnp.float32), _jax.tree.map(_jnp.zeros_like, weights))
        (loss, grad_w), grad_x = _jax.lax.scan(body, init, (per_example, given["loss_target"]))
    with _jax.named_scope("update"):
        delta_w, new_m, new_v = {}, {}, {}
        for n in TWIN_WEIGHTS:
            delta_w[n], new_m[n], new_v[n] = _adamw(weights[n], grad_w[n], given["m_" + n], given["v_" + n])
    return (loss, grad_x, *[grad_w[n] for n in TWIN_WEIGHTS], *[delta_w[n] for n in TWIN_WEIGHTS],
            *[new_m[n] for n in TWIN_WEIGHTS], *[new_v[n] for n in TWIN_WEIGHTS])
```

```python
import functools
import math

import jax
import jax.numpy as jnp
from jax import lax
from jax.experimental import pallas as pl
from jax.experimental.pallas import tpu as pltpu

D = 1024
F = 2816
NH = 8
DH = 64
DA = NH * DH
DCV = D - DA
PLE = 256
NPROJ = 3 * DA + NH + 3 * DCV
LN_EPS = 1e-5
RMS_EPS = 1e-6
NEG = -1e30
ALPHA = 2.0 ** 0.25
NDEV = 8
LANES = 128

ADAM_LR, ADAM_B1, ADAM_B2, ADAM_EPS, ADAM_WD, ADAM_STEP = 0.001, 0.9, 0.999, 1e-08, 0.01, 10

F32 = jnp.float32
MXU_DT = jnp.bfloat16
WIRE_DT = jnp.bfloat16

MESH_ID = pl.DeviceIdType.MESH
ANY = pl.BlockSpec(memory_space=pl.ANY)


def _params(vmem_mb, n_axes=1):
    return pltpu.CompilerParams(dimension_semantics=("arbitrary",) * n_axes,
                                vmem_limit_bytes=int(vmem_mb) << 20)


def _mm(a, b):
    return jnp.dot(a, b, preferred_element_type=F32)


def _mm_nt(a, b):
    return lax.dot_general(a, b, (((1,), (1,)), ((), ())), preferred_element_type=F32)


def _mm_tn(a, b):
    return lax.dot_general(a, b, (((0,), (0,)), ((), ())), preferred_element_type=F32)


def _split3(x):
    hi = x.astype(MXU_DT)
    r1 = x - hi.astype(F32)
    mid = r1.astype(MXU_DT)
    lo = (r1 - mid.astype(F32)).astype(MXU_DT)
    return hi, mid, lo


def _mm_sel(sel, x):
    hi, mid, lo = _split3(x)
    return _mm(sel, hi) + _mm(sel, mid) + _mm(sel, lo)


def _mm_xsel(x, sel):
    hi, mid, lo = _split3(x)
    return _mm(hi, sel) + _mm(mid, sel) + _mm(lo, sel)


def _sigmoid(x):
    return 1.0 / (1.0 + jnp.exp(-x))


def _ln_fwd(r):
    mu = jnp.mean(r, axis=-1, keepdims=True)
    xc = r - mu
    var = jnp.mean(xc * xc, axis=-1, keepdims=True)
    rstd = lax.rsqrt(var + LN_EPS)
    return xc * rstd, rstd


def _ln_bwd(dxhat, xhat, rstd):
    m1 = jnp.mean(dxhat, axis=-1, keepdims=True)
    m2 = jnp.mean(dxhat * xhat, axis=-1, keepdims=True)
    return rstd * (dxhat - m1 - xhat * m2)


def _rms_fwd(x):
    r = lax.rsqrt(jnp.mean(x * x, axis=-1, keepdims=True) + RMS_EPS)
    return x * r, r


def _rms_bwd(dyg, xn, r):
    return r * (dyg - xn * jnp.mean(dyg * xn, axis=-1, keepdims=True))


def _colsum(x):
    return jnp.sum(x, axis=0, keepdims=True)


def _f_chunks():
    out, c0 = [], 0
    while c0 < F:
        fc = min(512, F - c0)
        out.append((c0, fc))
        c0 += fc
    return out


def _tile(t, want):
    return want if t % want == 0 and t >= want else t


def _exchange(arrs, gather, name):
    n = len(arrs)
    out_shape = []
    for a in arrs:
        shp = ((NDEV,) + a.shape) if gather else a.shape
        out_shape.append(jax.ShapeDtypeStruct(shp, a.dtype))

    def body(*refs):
        ins, outs = refs[:n], refs[n:2 * n]
        send_sems, recv_sems, loc_sems = refs[2 * n:]
        x, y, c = lax.axis_index("x"), lax.axis_index("y"), lax.axis_index("c")
        me = 4 * x + 2 * y + c
        peers = []
        for k in range(1, NDEV):
            px = 1 - x if (k >> 2) & 1 else x
            py = 1 - y if (k >> 1) & 1 else y
            pc = 1 - c if k & 1 else c
            peers.append(((px, py, pc), 4 * px + 2 * py + pc))

        def remote(w, k, slot_src, slot_dst):
            src = ins[w] if gather else ins[w].at[slot_src]
            return pltpu.make_async_remote_copy(
                src_ref=src, dst_ref=outs[w].at[slot_dst],
                send_sem=send_sems.at[w * (NDEV - 1) + k], recv_sem=recv_sems.at[w * (NDEV - 1) + k],
                device_id=peers[k][0], device_id_type=MESH_ID)

        local = []
        for w in range(n):
            lc = pltpu.make_async_copy(ins[w] if gather else ins[w].at[me], outs[w].at[me], loc_sems.at[w])
            lc.start()
            local.append(lc)
        for k in range(NDEV - 1):
            for w in range(n):
                remote(w, k, peers[k][1], me).start()
        for k in range(NDEV - 1):
            for w in range(n):
                remote(w, k, me, peers[k][1]).wait_recv()
        for k in range(NDEV - 1):
            for w in range(n):
                remote(w, k, peers[k][1], me).wait_send()
        for lc in local:
            lc.wait()

    return pl.pallas_call(
        body, name=name, out_shape=tuple(out_shape),
        in_specs=[ANY] * n, out_specs=tuple([ANY] * n),
        scratch_shapes=[pltpu.SemaphoreType.DMA((n * (NDEV - 1),)), pltpu.SemaphoreType.DMA((n * (NDEV - 1),)),
                        pltpu.SemaphoreType.DMA((n,))],
    )(*arrs)


def _ffn_fwd(xin, gin, bin_, w_in, w_out, name):
    t = xin.shape[0]
    tm = _tile(t, 512)
    chunks = _f_chunks()

    def body(x_ref, gi_ref, bi_ref, win_hbm, wout_hbm, g_ref, u_ref, xh_ref, rs_ref, win_v, wout_v, acc_ref):
        @pl.when(pl.program_id(0) == 0)
        def _():
            pltpu.sync_copy(win_hbm, win_v)
            pltpu.sync_copy(wout_hbm, wout_v)

        x = x_ref[...] * gi_ref[...] + bi_ref[...]
        xb = x.astype(MXU_DT)
        for ci, (c0, fc) in enumerate(chunks):
            gc = _mm(xb, win_v[:, c0:c0 + fc])
            uc = _mm(xb, win_v[:, F + c0:F + c0 + fc])
            g_ref[:, c0:c0 + fc] = gc.astype(g_ref.dtype)
            u_ref[:, c0:c0 + fc] = uc.astype(u_ref.dtype)
            hc = (gc * _sigmoid(gc) * uc).astype(MXU_DT)
            part = _mm(hc, wout_v[c0:c0 + fc, :])
            if ci == 0:
                acc_ref[...] = part
            else:
                acc_ref[...] += part
        xh, rstd = _ln_fwd(ALPHA * x + 0.5 * acc_ref[...])
        xh_ref[...] = xh
        rs_ref[...] = rstd

    row = pl.BlockSpec((tm, D), lambda i: (i, 0))
    vec = pl.BlockSpec((1, D), lambda i: (0, 0))
    act = pl.BlockSpec((tm, F), lambda i: (i, 0))
    return pl.pallas_call(
        body, name=name, grid=(t // tm,),
        in_specs=[row, vec, vec, ANY, ANY],
        out_specs=(act, act, row, pl.BlockSpec((tm, 1), lambda i: (i, 0))),
        out_shape=(jax.ShapeDtypeStruct((t, F), MXU_DT), jax.ShapeDtypeStruct((t, F), MXU_DT),
                   jax.ShapeDtypeStruct((t, D), F32), jax.ShapeDtypeStruct((t, 1), F32)),
        scratch_shapes=[pltpu.VMEM((D, 2 * F), MXU_DT), pltpu.VMEM((F, D), MXU_DT), pltpu.VMEM((tm, D), F32)],
        compiler_params=_params(52),
    )(xin, gin, bin_, w_in, w_out)


def _mix_proj_fwd(xh1, g1, b1, w_qkv, w_bch, w_f, bf_pad, name):
    t = xh1.shape[0]
    tm = _tile(t, 512)

    def body(x_ref, g_ref, b_ref, wq_ref, wb_ref, wf_ref, bf_ref, q_ref, k_ref, v_ref, bch_ref, z_ref, c_ref, carry):
        @pl.when(pl.program_id(0) == 0)
        def _():
            carry[...] = jnp.zeros_like(carry)

        xb = (x_ref[...] * g_ref[...] + b_ref[...]).astype(MXU_DT)
        qkv = _mm(xb, wq_ref[...])
        q_ref[...] = qkv[:, 0:DA].astype(q_ref.dtype)
        k_ref[...] = qkv[:, DA:2 * DA].astype(k_ref.dtype)
        v_ref[...] = qkv[:, 2 * DA:3 * DA].astype(v_ref.dtype)
        bch_ref[...] = _mm(xb, wb_ref[...])
        z = _mm(xb, wf_ref[...]) + bf_ref[...]
        z_ref[...] = z
        logf = jnp.minimum(z, 0.0) - jnp.log(1.0 + jnp.exp(-jnp.abs(z)))
        row = lax.broadcasted_iota(jnp.int32, (tm, tm), 0)
        col = lax.broadcasted_iota(jnp.int32, (tm, tm), 1)
        tri = jnp.where(row >= col, 1.0, 0.0).astype(MXU_DT)
        c = carry[...] + _mm_sel(tri, logf)
        c_ref[...] = c
        carry[...] = c[tm - 1:tm, :]

    row = lambda w: pl.BlockSpec((tm, w), lambda i: (i, 0))
    full = lambda a: pl.BlockSpec(a.shape, lambda i: (0, 0))
    return pl.pallas_call(
        body, name=name, grid=(t // tm,),
        in_specs=[row(D), full(g1), full(b1), full(w_qkv), full(w_bch), full(w_f), full(bf_pad)],
        out_specs=(row(DA), row(DA), row(DA), row(3 * DCV), row(LANES), row(LANES)),
        out_shape=(jax.ShapeDtypeStruct((t, DA), MXU_DT),) * 3
        + (jax.ShapeDtypeStruct((t, 3 * DCV), F32), jax.ShapeDtypeStruct((t, LANES), F32),
           jax.ShapeDtypeStruct((t, LANES), F32)),
        scratch_shapes=[pltpu.VMEM((1, LANES), F32)],
        compiler_params=_params(48),
    )(xh1, g1, b1, w_qkv, w_bch, w_f, bf_pad)


def _attn_fwd(q, k, v, ccol, crow, name):
    t = q.shape[0]
    tq = _tile(t, 512)
    nq = t // tq

    def body(q_ref, k_ref, v_ref, cc_ref, cr_ref, o_ref, l_ref):
        i = pl.program_id(1)
        lane = lax.broadcasted_iota(jnp.int32, (1, LANES), 1)
        rowi = lax.broadcasted_iota(jnp.int32, (tq, tq), 0)
        coli = lax.broadcasted_iota(jnp.int32, (tq, tq), 1)
        outs = []
        for a in range(2):
            head = (lane // DH) == a
            qa = jnp.where(head, q_ref[...], jnp.zeros_like(q_ref[...])) * jnp.asarray(0.125, q_ref.dtype)
            cc = cc_ref[a]

            def step(j, carry, masked):
                m, l, acc = carry
                off = pl.multiple_of(j * tq, tq)
                kj = k_ref[pl.ds(off, tq), :]
                vj = v_ref[pl.ds(off, tq), :]
                s = _mm_nt(qa, kj) + (cc - cr_ref[a, :, pl.ds(off, tq)])
                if masked:
                    s = jnp.where(rowi >= coli, s, NEG)
                m_new = jnp.maximum(m, jnp.max(s, axis=-1, keepdims=True))
                p = jnp.exp(s - m_new)
                alpha = jnp.exp(m - m_new)
                l = alpha * l + jnp.sum(p, axis=-1, keepdims=True)
                acc = alpha * acc + _mm(p.astype(MXU_DT), vj)
                return m_new, l, acc

            init = (jnp.full((tq, 1), NEG, F32), jnp.zeros((tq, 1), F32), jnp.zeros((tq, LANES), F32))
            carry = lax.fori_loop(0, i, lambda j, cr: step(j, cr, False), init)
            m, l, acc = step(i, carry, True)
            outs.append(acc / l)
            l_ref[a] = m + jnp.log(l)
        o_ref[...] = jnp.where(lane < DH, outs[0], outs[1])

    return pl.pallas_call(
        body, name=name, grid=(NH // 2, nq),
        in_specs=[pl.BlockSpec((tq, LANES), lambda p, i: (i, p)),
                  pl.BlockSpec((t, LANES), lambda p, i: (0, p)),
                  pl.BlockSpec((t, LANES), lambda p, i: (0, p)),
                  pl.BlockSpec((2, tq, 1), lambda p, i: (p, i, 0)),
                  pl.BlockSpec((2, 1, t), lambda p, i: (p, 0, 0))],
        out_specs=(pl.BlockSpec((tq, LANES), lambda p, i: (i, p)),
                   pl.BlockSpec((2, tq, 1), lambda p, i: (p, i, 0))),
        out_shape=(jax.ShapeDtypeStruct((t, DA), F32), jax.ShapeDtypeStruct((NH, t, 1), F32)),
        compiler_params=_params(48, 2),
    )(q, k, v, ccol, crow)


def _conv_parts(bch):
    return bch[:, 0:DCV], bch[:, DCV:2 * DCV], bch[:, 2 * DCV:3 * DCV]


def _mix_post_fwd(o, bch, conv_w, g_attn, g_conv, xh1, g1, b1, w_mo, name):
    t = o.shape[0]
    tm = _tile(t, 512)
    hb = tm // 8

    def body(o_ref, bch_ref, halo_ref, cw_ref, ga_ref, gc_ref, x_ref, g_ref, b_ref, w_ref,
             mg_ref, xh_ref, rs_ref, ext):
        i = pl.program_id(0)
        an, _ = _rms_fwd(o_ref[...])
        mg_ref[:, 0:DA] = (an * ga_ref[...]).astype(mg_ref.dtype)
        bb, cc, hh = _conv_parts(bch_ref[...])
        _, hc, hh_h = _conv_parts(halo_ref[...])
        u = cc * hh
        ext[0:8, :] = jnp.where(i > 0, hc * hh_h, 0.0)
        ext[8:8 + tm, :] = u
        raw = cw_ref[0:1, :] * ext[6:6 + tm, :] + cw_ref[1:2, :] * ext[7:7 + tm, :] + cw_ref[2:3, :] * u
        cn, _ = _rms_fwd(bb * raw)
        mg_ref[:, DA:D] = (cn * gc_ref[...]).astype(mg_ref.dtype)
        x1 = x_ref[...] * g_ref[...] + b_ref[...]
        xh, rstd = _ln_fwd(ALPHA * x1 + _mm(mg_ref[...], w_ref[...]))
        xh_ref[...] = xh
        rs_ref[...] = rstd

    row = lambda w: pl.BlockSpec((tm, w), lambda i: (i, 0))
    full = lambda a: pl.BlockSpec(a.shape, lambda i: (0, 0))
    return pl.pallas_call(
        body, name=name, grid=(t // tm,),
        in_specs=[row(DA), row(3 * DCV),
                  pl.BlockSpec((8, 3 * DCV), lambda i: (jnp.maximum(i * hb - 1, 0), 0)),
                  full(conv_w), full(g_attn), full(g_conv), row(D), full(g1), full(b1), full(w_mo)],
        out_specs=(row(D), row(D), pl.BlockSpec((tm, 1), lambda i: (i, 0))),
        out_shape=(jax.ShapeDtypeStruct((t, D), MXU_DT), jax.ShapeDtypeStruct((t, D), F32),
                   jax.ShapeDtypeStruct((t, 1), F32)),
        scratch_shapes=[pltpu.VMEM((tm + 8, DCV), F32)],
        compiler_params=_params(48),
    )(o, bch, bch, conv_w, g_attn, g_conv, xh1, g1, b1, w_mo)


def _tail(xh3, rs3, g3, b3, p, w_g, w_ple, bg, g4, b4, target, name):
    t = xh3.shape[0]
    tm = _tile(t, 512)

    def body(x_ref, rs_ref, g3_ref, b3_ref, p_ref, wg_ref, wp_ref, bg_ref, g4_ref, b4_ref, t_ref,
             dr_ref, dz_ref, de_ref, st_ref):
        @pl.when(pl.program_id(0) == 0)
        def _():
            st_ref[...] = jnp.zeros_like(st_ref)

        xh3v = x_ref[...]
        x3 = xh3v * g3_ref[...] + b3_ref[...]
        gate = _sigmoid(_mm(x3.astype(MXU_DT), wg_ref[...]) + bg_ref[...])
        e = _mm(p_ref[...].astype(MXU_DT), wp_ref[...])
        xh4, rstd4 = _ln_fwd(ALPHA * x3 + gate * e)
        diff = xh4 * g4_ref[...] + b4_ref[...] - t_ref[...]
        dy = diff * (1.0 / D)
        st_ref[5:6, :] += _colsum(diff * diff)
        st_ref[0:1, :] += _colsum(dy * xh4)
        st_ref[1:2, :] += _colsum(dy)
        dr4 = _ln_bwd(dy * g4_ref[...], xh4, rstd4)
        de_ref[...] = (dr4 * gate).astype(de_ref.dtype)
        dz = dr4 * e * gate * (1.0 - gate)
        st_ref[2:3, :] += _colsum(dz)
        dzb = dz.astype(MXU_DT)
        dz_ref[...] = dzb
        dx3 = ALPHA * dr4 + _mm_nt(dzb, wg_ref[...])
        st_ref[3:4, :] += _colsum(dx3 * xh3v)
        st_ref[4:5, :] += _colsum(dx3)
        dr_ref[...] = _ln_bwd(dx3 * g3_ref[...], xh3v, rs_ref[...])

    row = lambda w: pl.BlockSpec((tm, w), lambda i: (i, 0))
    full = lambda a: pl.BlockSpec(a.shape, lambda i: (0, 0))
    return pl.pallas_call(
        body, name=name, grid=(t // tm,),
        in_specs=[row(D), row(1), full(g3), full(b3), row(PLE), full(w_g), full(w_ple), full(bg), full(g4),
                  full(b4), row(D)],
        out_specs=(row(D), row(D), row(D), pl.BlockSpec((8, D), lambda i: (0, 0))),
        out_shape=(jax.ShapeDtypeStruct((t, D), F32), jax.ShapeDtypeStruct((t, D), MXU_DT),
                   jax.ShapeDtypeStruct((t, D), MXU_DT), jax.ShapeDtypeStruct((8, D), F32)),
        compiler_params=_params(48),
    )(xh3, rs3, g3, b3, p, w_g, w_ple, bg, g4, b4, target)


def _ffn_bwd(dr, gact, uact, xin, rsin, gin, w_in, w_out, prev_ln, name):
    t = dr.shape[0]
    tm = _tile(t, 256)
    chunks = _f_chunks()

    def body(dr_ref, g_ref, u_ref, x_ref, rs_ref, gi_ref, win_hbm, wout_hbm,
             df_ref, dg_ref, du_ref, dx_ref, st_ref, win_v, wout_v, acc_ref):
        @pl.when(pl.program_id(0) == 0)
        def _():
            pltpu.sync_copy(win_hbm, win_v)
            pltpu.sync_copy(wout_hbm, wout_v)
            st_ref[...] = jnp.zeros_like(st_ref)

        drv = dr_ref[...]
        dfb = (0.5 * drv).astype(MXU_DT)
        df_ref[...] = dfb
        for ci, (c0, fc) in enumerate(chunks):
            dh = _mm_nt(dfb, wout_v[c0:c0 + fc, :])
            g = g_ref[:, c0:c0 + fc].astype(F32)
            u = u_ref[:, c0:c0 + fc].astype(F32)
            sg = _sigmoid(g)
            dgb = (dh * u * (sg * (1.0 + g * (1.0 - sg)))).astype(MXU_DT)
            dub = (dh * (g * sg)).astype(MXU_DT)
            dg_ref[:, c0:c0 + fc] = dgb
            du_ref[:, c0:c0 + fc] = dub
            part = _mm_nt(dgb, win_v[:, c0:c0 + fc]) + _mm_nt(dub, win_v[:, F + c0:F + c0 + fc])
            if ci == 0:
                acc_ref[...] = part
            else:
                acc_ref[...] += part
        dx = ALPHA * drv + acc_ref[...]
        if prev_ln:
            xh = x_ref[...]
            st_ref[0:1, :] += _colsum(dx * xh)
            st_ref[1:2, :] += _colsum(dx)
            dx_ref[...] = _ln_bwd(dx * gi_ref[...], xh, rs_ref[...])
        else:
            dx_ref[...] = dx

    row = pl.BlockSpec((tm, D), lambda i: (i, 0))
    vec = pl.BlockSpec((1, D), lambda i: (0, 0))
    act = pl.BlockSpec((tm, F), lambda i: (i, 0))
    return pl.pallas_call(
        body, name=name, grid=(t // tm,),
        in_specs=[row, act, act, row, pl.BlockSpec((tm, 1), lambda i: (i, 0)), vec, ANY, ANY],
        out_specs=(row, act, act, row, pl.BlockSpec((8, D), lambda i: (0, 0))),
        out_shape=(jax.ShapeDtypeStruct((t, D), MXU_DT), jax.ShapeDtypeStruct((t, F), MXU_DT),
                   jax.ShapeDtypeStruct((t, F), MXU_DT), jax.ShapeDtypeStruct((t, D), F32),
                   jax.ShapeDtypeStruct((8, D), F32)),
        scratch_shapes=[pltpu.VMEM((D, 2 * F), MXU_DT), pltpu.VMEM((F, D), MXU_DT), pltpu.VMEM((tm, D), F32)],
        compiler_params=_params(52),
    )(dr, gact, uact, xin, rsin, gin, w_in, w_out)


def _mix_post_bwd(dr2, o, bch, conv_w, g_attn, g_conv, w_mo, name):
    t = dr2.shape[0]
    tm = _tile(t, 512)
    hb = tm // 8

    def body(dr_ref, o_ref, bch_ref, halo_ref, cw_ref, ga_ref, gc_ref, w_ref,
             dm_ref, do_ref, dl_ref, dy_ref, st_ref, ext):
        i = pl.program_id(0)

        @pl.when(i == 0)
        def _():
            st_ref[...] = jnp.zeros_like(st_ref)

        dmb = dr_ref[...].astype(MXU_DT)
        dm_ref[...] = dmb
        dmg = _mm_nt(dmb, w_ref[...])
        ov = o_ref[...]
        an, ra = _rms_fwd(ov)
        da = dmg[:, 0:DA]
        st_ref[0:1, :] += _colsum(da * an)
        dxa = _rms_bwd(da * ga_ref[...], an, ra)
        dob = dxa.astype(do_ref.dtype)
        do_ref[...] = dob
        crow = lax.broadcasted_iota(jnp.int32, (DA, LANES), 0)
        ccol = lax.broadcasted_iota(jnp.int32, (DA, LANES), 1)
        sel = jnp.where((crow // DH) == ccol, 1.0, 0.0).astype(MXU_DT)
        dl_ref[...] = _mm_xsel(dob.astype(F32) * ov, sel)
        bb, cc, hh = _conv_parts(bch_ref[...])
        _, hc, hh_h = _conv_parts(halo_ref[...])
        u = cc * hh
        ext[0:8, :] = jnp.where(i > 0, hc * hh_h, 0.0)
        ext[8:8 + tm, :] = u
        raw = cw_ref[0:1, :] * ext[6:6 + tm, :] + cw_ref[1:2, :] * ext[7:7 + tm, :] + cw_ref[2:3, :] * u
        cn, rc = _rms_fwd(bb * raw)
        dcn = dmg[:, DA:D]
        st_ref[1:2, :] += _colsum(dcn * cn)
        dy_ref[...] = _rms_bwd(dcn * gc_ref[...], cn, rc)

    row = lambda w: pl.BlockSpec((tm, w), lambda i: (i, 0))
    full = lambda a: pl.BlockSpec(a.shape, lambda i: (0, 0))
    return pl.pallas_call(
        body, name=name, grid=(t // tm,),
        in_specs=[row(D), row(DA), row(3 * DCV),
                  pl.BlockSpec((8, 3 * DCV), lambda i: (jnp.maximum(i * hb - 1, 0), 0)),
                  full(conv_w), full(g_attn), full(g_conv), full(w_mo)],
        out_specs=(row(D), row(DA), row(LANES), row(DCV), pl.BlockSpec((8, DA), lambda i: (0, 0))),
        out_shape=(jax.ShapeDtypeStruct((t, D), MXU_DT), jax.ShapeDtypeStruct((t, DA), MXU_DT),
                   jax.ShapeDtypeStruct((t, LANES), F32), jax.ShapeDtypeStruct((t, DCV), F32),
                   jax.ShapeDtypeStruct((8, DA), F32)),
        scratch_shapes=[pltpu.VMEM((tm + 8, DCV), F32)],
        compiler_params=_params(48),
    )(dr2, o, bch, bch, conv_w, g_attn, g_conv, w_mo)


def _conv_bwd(dy, bch, conv_w, name):
    t = dy.shape[0]
    tm = _tile(t, 512)
    hb = tm // 8
    nt = t // tm

    def body(dy_ref, dyn_ref, bch_ref, prev_ref, next_ref, cw_ref, out_ref, st_ref, ext_u, ext_d):
        i = pl.program_id(0)

        @pl.when(i == 0)
        def _():
            st_ref[...] = jnp.zeros_like(st_ref)

        bb, cc, hh = _conv_parts(bch_ref[...])
        _, pc, ph = _conv_parts(prev_ref[...])
        nb, _, _ = _conv_parts(next_ref[...])
        u = cc * hh
        ext_u[0:8, :] = jnp.where(i > 0, pc * ph, 0.0)
        ext_u[8:8 + tm, :] = u
        u1 = ext_u[7:7 + tm, :]
        u2 = ext_u[6:6 + tm, :]
        w0, w1, w2 = cw_ref[0:1, :], cw_ref[1:2, :], cw_ref[2:3, :]
        dyv = dy_ref[...]
        out_ref[:, 0:DCV] = (dyv * (w0 * u2 + w1 * u1 + w2 * u)).astype(out_ref.dtype)
        dcr = dyv * bb
        ext_d[0:tm, :] = dcr
        ext_d[tm:tm + 8, :] = jnp.where(i < nt - 1, dyn_ref[...] * nb, 0.0)
        du = w2 * dcr + w1 * ext_d[1:1 + tm, :] + w0 * ext_d[2:2 + tm, :]
        out_ref[:, DCV:2 * DCV] = (du * hh).astype(out_ref.dtype)
        out_ref[:, 2 * DCV:3 * DCV] = (du * cc).astype(out_ref.dtype)
        st_ref[0:1, :] += _colsum(dcr * u2)
        st_ref[1:2, :] += _colsum(dcr * u1)
        st_ref[2:3, :] += _colsum(dcr * u)

    row = lambda w: pl.BlockSpec((tm, w), lambda i: (i, 0))
    prev = lambda w: pl.BlockSpec((8, w), lambda i: (jnp.maximum(i * hb - 1, 0), 0))
    nxt = lambda w: pl.BlockSpec((8, w), lambda i: (jnp.minimum((i + 1) * hb, nt * hb - 1), 0))
    return pl.pallas_call(
        body, name=name, grid=(nt,),
        in_specs=[row(DCV), nxt(DCV), row(3 * DCV), prev(3 * DCV), nxt(3 * DCV),
                  pl.BlockSpec(conv_w.shape, lambda i: (0, 0))],
        out_specs=(row(3 * DCV), pl.BlockSpec((8, DCV), lambda i: (0, 0))),
        out_shape=(jax.ShapeDtypeStruct((t, 3 * DCV), MXU_DT), jax.ShapeDtypeStruct((8, DCV), F32)),
        scratch_shapes=[pltpu.VMEM((tm + 8, DCV), F32), pltpu.VMEM((tm + 8, DCV), F32)],
        compiler_params=_params(48),
    )(dy, dy, bch, bch, bch, conv_w)


def _attn_bwd(q, k, v, do, lrow, drow, crow, ccol, name):
    t = q.shape[0]
    tq = _tile(t, 512)
    nq = t // tq

    def body(k_ref, v_ref, q_ref, do_ref, l_ref, dl_ref, cr_ref, cc_ref, dk_ref, dv_ref, dq_ref, dc_ref, dcq_ref):
        j = pl.program_id(1)

        @pl.when(j == 0)
        def _():
            dq_ref[...] = jnp.zeros_like(dq_ref)
            dcq_ref[...] = jnp.zeros_like(dcq_ref)

        lane = lax.broadcasted_iota(jnp.int32, (1, LANES), 1)
        rowi = lax.broadcasted_iota(jnp.int32, (tq, tq), 0)
        coli = lax.broadcasted_iota(jnp.int32, (tq, tq), 1)
        res = []
        for a in range(2):
            head = (lane // DH) == a
            ka = jnp.where(head, k_ref[...], jnp.zeros_like(k_ref[...])) * jnp.asarray(0.125, k_ref.dtype)
            va = jnp.where(head, v_ref[...], jnp.zeros_like(v_ref[...]))
            cck = cc_ref[a]

            def step(i, carry, masked):
                dk, dv, dc = carry
                off = pl.multiple_of(i * tq, tq)
                qi = q_ref[pl.ds(off, tq), :]
                doi = do_ref[pl.ds(off, tq), :]
                st = _mm_nt(ka, qi) + (cr_ref[a, :, pl.ds(off, tq)] - cck)
                if masked:
                    st = jnp.where(coli >= rowi, st, NEG)
                pt = jnp.exp(st - l_ref[a, :, pl.ds(off, tq)])
                dv = dv + _mm(pt.astype(MXU_DT), doi)
                dst = pt * (_mm_nt(va, doi) - dl_ref[a, :, pl.ds(off, tq)])
                dc = dc - jnp.sum(dst, axis=-1, keepdims=True)
                dcq_ref[a, :, pl.ds(off, tq)] += jnp.sum(dst, axis=0, keepdims=True)
                dsb = dst.astype(MXU_DT)
                dk = dk + _mm(dsb, qi)
                dq_ref[pl.ds(off, tq), :] += _mm_tn(dsb, ka)
                return dk, dv, dc

            init = (jnp.zeros((tq, LANES), F32), jnp.zeros((tq, LANES), F32), jnp.zeros((tq, 1), F32))
            carry = step(j, init, True)
            dk, dv, dc = lax.fori_loop(j + 1, nq, lambda i, cr: step(i, cr, False), carry)
            res.append((dk, dv))
            dc_ref[a] = dc
        dk_ref[...] = (jnp.where(lane < DH, res[0][0], res[1][0]) * 0.125).astype(dk_ref.dtype)
        dv_ref[...] = jnp.where(lane < DH, res[0][1], res[1][1]).astype(dv_ref.dtype)

    tile = pl.BlockSpec((tq, LANES), lambda p, j: (j, p))
    whole = pl.BlockSpec((t, LANES), lambda p, j: (0, p))
    rowv = pl.BlockSpec((2, 1, t), lambda p, j: (p, 0, 0))
    colv = pl.BlockSpec((2, tq, 1), lambda p, j: (p, j, 0))
    return pl.pallas_call(
        body, name=name, grid=(NH // 2, nq),
        in_specs=[tile, tile, whole, whole, rowv, rowv, rowv, colv],
        out_specs=(tile, tile, whole, colv, rowv),
        out_shape=(jax.ShapeDtypeStruct((t, DA), MXU_DT), jax.ShapeDtypeStruct((t, DA), MXU_DT),
                   jax.ShapeDtypeStruct((t, DA), F32), jax.ShapeDtypeStruct((NH, t, 1), F32),
                   jax.ShapeDtypeStruct((NH, 1, t), F32)),
        compiler_params=_params(52, 2),
    )(k, v, q, do, lrow, drow, crow, ccol)


def _mix_proj_bwd(dr2, dq, dk, dv, dbch, dc, z, xh1, rs1, g1, w_qkv, w_bch, w_f, name):
    t = dr2.shape[0]
    tm = _tile(t, 512)
    nt = t // tm

    def body(dr_ref, dq_ref, dk_ref, dv_ref, db_ref, dc_ref, z_ref, x_ref, rs_ref, g_ref,
             wq_ref, wb_ref, wf_ref, out_ref, df_ref, st_ref, carry):
        @pl.when(pl.program_id(0) == 0)
        def _():
            carry[...] = jnp.zeros_like(carry)
            st_ref[...] = jnp.zeros_like(st_ref)

        row = lax.broadcasted_iota(jnp.int32, (tm, tm), 0)
        col = lax.broadcasted_iota(jnp.int32, (tm, tm), 1)
        triu = jnp.where(col >= row, 1.0, 0.0).astype(MXU_DT)
        dlogf = carry[...] + _mm_sel(triu, dc_ref[...])
        carry[...] = dlogf[0:1, :]
        dz = dlogf / (1.0 + jnp.exp(z_ref[...]))
        st_ref[2:3, 0:LANES] += _colsum(dz)
        dfb = dz.astype(MXU_DT)
        df_ref[...] = dfb
        dx = (ALPHA * dr_ref[...]
              + _mm_nt(dq_ref[...].astype(MXU_DT), wq_ref[:, 0:DA])
              + _mm_nt(dk_ref[...], wq_ref[:, DA:2 * DA])
              + _mm_nt(dv_ref[...], wq_ref[:, 2 * DA:3 * DA])
              + _mm_nt(db_ref[...], wb_ref[...])
              + _mm_nt(dfb, wf_ref[...]))
        xh = x_ref[...]
        st_ref[0:1, :] += _colsum(dx * xh)
        st_ref[1:2, :] += _colsum(dx)
        out_ref[...] = _ln_bwd(dx * g_ref[...], xh, rs_ref[...])

    row = lambda w: pl.BlockSpec((tm, w), lambda i: (nt - 1 - i, 0))
    full = lambda a: pl.BlockSpec(a.shape, lambda i: (0, 0))
    return pl.pallas_call(
        body, name=name, grid=(nt,),
        in_specs=[row(D), row(DA), row(DA), row(DA), row(3 * DCV), row(LANES), row(LANES), row(D), row(1),
                  full(g1), full(w_qkv), full(w_bch), full(w_f)],
        out_specs=(row(D), row(LANES), pl.BlockSpec((8, D), lambda i: (0, 0))),
        out_shape=(jax.ShapeDtypeStruct((t, D), F32), jax.ShapeDtypeStruct((t, LANES), MXU_DT),
                   jax.ShapeDtypeStruct((8, D), F32)),
        scratch_shapes=[pltpu.VMEM((1, LANES), F32)],
        compiler_params=_params(48),
    )(dr2, dq, dk, dv, dbch, dc, z, xh1, rs1, g1, w_qkv, w_bch, w_f)


def _dw(mode, a_parts, b, m, n, name, tmm=None, tn=None):
    t = b.shape[0]
    tmm = tmm or m
    tn = tn or n
    tt = _tile(t, 1024)

    def body(*refs):
        a_refs, b_ref, o_ref = refs[:len(a_parts)], refs[len(a_parts)], refs[len(a_parts) + 1]

        @pl.when(pl.program_id(2) == 0)
        def _():
            o_ref[...] = jnp.zeros_like(o_ref)

        if mode == "plain":
            a = a_refs[0][...].astype(MXU_DT)
        elif mode == "affine":
            a = (a_refs[0][...] * a_refs[1][...] + a_refs[2][...]).astype(MXU_DT)
        else:
            g = a_refs[0][...].astype(F32)
            a = (g * _sigmoid(g) * a_refs[1][...].astype(F32)).astype(MXU_DT)
        o_ref[...] += _mm_tn(a, b_ref[...].astype(MXU_DT))

    a_tile = pl.BlockSpec((tt, tmm), lambda i, j, k: (k, i))
    a_vec = pl.BlockSpec((1, tmm), lambda i, j, k: (0, i))
    a_specs = {"plain": [a_tile], "affine": [a_tile, a_vec, a_vec], "swiglu": [a_tile, a_tile]}[mode]
    return pl.pallas_call(
        body, name=name, grid=(m // tmm, n // tn, t // tt),
        in_specs=a_specs + [pl.BlockSpec((tt, tn), lambda i, j, k: (k, j))],
        out_specs=pl.BlockSpec((tmm, tn), lambda i, j, k: (i, j)),
        out_shape=jax.ShapeDtypeStruct((m, n), F32),
        compiler_params=_params(48, 3),
    )(*a_parts, b)


def _adamw(w, g, m, v):
    m = ADAM_B1 * m + (1.0 - ADAM_B1) * g
    v = ADAM_B2 * v + (1.0 - ADAM_B2) * (g * g)
    m_hat = m / (1.0 - ADAM_B1 ** ADAM_STEP)
    v_hat = v / (1.0 - ADAM_B2 ** ADAM_STEP)
    delta = -ADAM_LR * (m_hat / (jnp.sqrt(v_hat) + ADAM_EPS) + ADAM_WD * w)
    return delta, m, v


def _reduce_adamw(landed, own, w, m, v, name):
    r, c = own.shape
    tr = _tile(r, 128)

    def body(l_ref, o_ref, w_ref, m_ref, v_ref, g_out, d_out, m_out, v_out):
        me = 4 * lax.axis_index("x") + 2 * lax.axis_index("y") + lax.axis_index("c")
        g = None
        for j in range(NDEV):
            term = jnp.where(me == j, o_ref[...], l_ref[j].astype(F32))
            g = term if g is None else g + term
        g_out[...] = g
        d_out[...], m_out[...], v_out[...] = _adamw(w_ref[...], g, m_ref[...], v_ref[...])

    blk = pl.BlockSpec((tr, c), lambda i: (i, 0))
    sds = jax.ShapeDtypeStruct((r, c), F32)
    return pl.pallas_call(
        body, name=name, grid=(r // tr,),
        in_specs=[pl.BlockSpec((NDEV, tr, c), lambda i: (0, i, 0)), blk, blk, blk, blk],
        out_specs=(blk, blk, blk, blk), out_shape=(sds, sds, sds, sds),
        compiler_params=_params(40),
    )(landed, own, w, m, v)


def _sum_small(gathered, name):
    _, r, c = gathered.shape

    def body(g_ref, o_ref):
        acc = g_ref[0]
        for j in range(1, NDEV):
            acc = acc + g_ref[j]
        o_ref[...] = acc

    return pl.pallas_call(body, name=name, out_shape=jax.ShapeDtypeStruct((r, c), F32))(gathered)


def _adamw_small(g, w, m, v, name):
    def body(g_ref, w_ref, m_ref, v_ref, d_out, m_out, v_out):
        d_out[...], m_out[...], v_out[...] = _adamw(w_ref[...], g_ref[...], m_ref[...], v_ref[...])

    sds = jax.ShapeDtypeStruct(g.shape, F32)
    return pl.pallas_call(body, name=name, out_shape=(sds, sds, sds))(g, w, m, v)


def _cols_from_stack(s):
    return jnp.transpose(s, (1, 0, 2)).reshape(s.shape[1], NDEV * s.shape[2])


def _cols_to_stack(w):
    r, c = w.shape
    return jnp.transpose(w.reshape(r, NDEV, c // NDEV), (1, 0, 2))


def _rows_from_stack(s):
    return s.reshape(NDEV * s.shape[1], s.shape[2])


def _rows_to_stack(w):
    r, c = w.shape
    return w.reshape(NDEV, r // NDEV, c)


SMALL_ROWS = 16
SMALL_SLOTS = {
    "ln1_g": (0, 0, D), "ln1_b": (1, 0, D), "ln2_g": (2, 0, D), "ln2_b": (3, 0, D), "ln3_g": (4, 0, D),
    "ln3_b": (5, 0, D), "b_ple_gate": (6, 0, D), "ln4_g": (7, 0, D), "ln4_b": (8, 0, D),
    "g_attn": (9, 0, DA), "g_conv": (9, DA, DCV), "b_forget": (10, 0, NH),
}
CONVW_ROW = 11
LOSS_SLOT = (10, LANES)


def _pack_small(vals, conv_rows, loss=None):
    out = jnp.zeros((SMALL_ROWS, D), F32)
    for nm, (r, off, wd) in SMALL_SLOTS.items():
        out = out.at[r:r + 1, off:off + wd].set(vals[nm].reshape(1, wd).astype(F32))
    out = out.at[CONVW_ROW:CONVW_ROW + 3, 0:conv_rows.shape[1]].set(conv_rows.astype(F32))
    if loss is not None:
        out = out.at[LOSS_SLOT[0], LOSS_SLOT[1]].set(loss)
    return out


def _unpack_small(packed, name):
    r, off, wd = SMALL_SLOTS[name]
    return packed[r:r + 1, off:off + wd]


def kernel(x, p, ffn1_w_in, ffn1_w_out, ln1_g, ln1_b, w_mix_in, b_forget, conv_w, g_attn, g_conv, w_mix_out, ln2_g, ln2_b, ffn2_w_in, ffn2_w_out, ln3_g, ln3_b, w_ple, w_ple_gate, b_ple_gate, ln4_g, ln4_b, loss_target, m_ffn1_w_in, m_ffn1_w_out, m_ln1_g, m_ln1_b, m_w_mix_in, m_b_forget, m_conv_w, m_g_attn, m_g_conv, m_w_mix_out, m_ln2_g, m_ln2_b, m_ffn2_w_in, m_ffn2_w_out, m_ln3_g, m_ln3_b, m_w_ple, m_w_ple_gate, m_b_ple_gate, m_ln4_g, m_ln4_b, v_ffn1_w_in, v_ffn1_w_out, v_ln1_g, v_ln1_b, v_w_mix_in, v_b_forget, v_conv_w, v_g_attn, v_g_conv, v_w_mix_out, v_ln2_g, v_ln2_b, v_ffn2_w_in, v_ffn2_w_out, v_ln3_g, v_ln3_b, v_w_ple, v_w_ple_gate, v_b_ple_gate, v_ln4_g, v_ln4_b):
    args = dict(locals())
    t = x.shape[1]
    me = 4 * lax.axis_index("x") + 2 * lax.axis_index("y") + lax.axis_index("c")
    x0 = x.reshape(t, D)
    p0 = p.reshape(t, PLE)
    tgt = loss_target.reshape(t, D)

    big = ["ffn1_w_in", "ffn1_w_out", "w_mix_in", "w_mix_out", "ffn2_w_in", "ffn2_w_out", "w_ple", "w_ple_gate"]
    col_sharded = {"ffn1_w_in", "w_mix_in", "ffn2_w_in", "w_ple"}
    shard = {nm: args[nm][0] for nm in big}

    gathered = _exchange([shard[nm].astype(WIRE_DT) for nm in big] + [conv_w[0]], True, "ag_weights")
    full = {nm: (_cols_from_stack(g) if nm in col_sharded else _rows_from_stack(g)).astype(MXU_DT)
            for nm, g in zip(big, gathered[:len(big)])}
    cw = _cols_from_stack(gathered[len(big)])
    gx, gw, small, loss_part = _local_step(x0, p0, tgt, full, cw, {nm: args[nm] for nm in SMALL_SLOTS})

    stacks = [(_cols_to_stack(gw[nm]) if nm in col_sharded else _rows_to_stack(gw[nm])) for nm in big]
    landed = _exchange([s.astype(WIRE_DT) for s in stacks], False, "rs_grads")
    small_part = _pack_small({nm: small[nm] for nm in SMALL_SLOTS},
                             jnp.pad(small["conv_w"], ((0, 0), (0, D - DCV))), loss_part)
    small_all = _exchange([small_part], True, "ag_small")[0]
    small_g = _sum_small(small_all, "sum_small")
    loss = small_g[LOSS_SLOT[0], LOSS_SLOT[1]]

    outs = {"loss": loss, "grad_x": gx.reshape(1, t, D)}
    for nm, st, ld in zip(big, stacks, landed):
        own = lax.dynamic_index_in_dim(st, me, axis=0, keepdims=False)
        g, dl, mn, vn = _reduce_adamw(ld, own, shard[nm], args["m_" + nm][0], args["v_" + nm][0], "adamw_" + nm)
        outs["grad_" + nm], outs["delta_" + nm], outs["new_m_" + nm], outs["new_v_" + nm] = (
            g[None], dl[None], mn[None], vn[None])
    small_names = list(SMALL_SLOTS)
    cshard = lax.dynamic_slice_in_dim(small_g[CONVW_ROW:CONVW_ROW + 3, 0:DCV], me * (DCV // NDEV), DCV // NDEV, axis=1)
    g_pack = _pack_small({nm: _unpack_small(small_g, nm) for nm in small_names}, cshard)
    packs = [_pack_small({nm: args[pre + nm] for nm in small_names}, args[pre + "conv_w"][0])
             for pre in ("", "m_", "v_")]
    d_pack, m_pack, v_pack = _adamw_small(g_pack, packs[0], packs[1], packs[2], "adamw_small")
    for key, pk in (("grad_", g_pack), ("delta_", d_pack), ("new_m_", m_pack), ("new_v_", v_pack)):
        for nm in small_names:
            outs[key + nm] = _unpack_small(pk, nm)
        outs[key + "conv_w"] = pk[CONVW_ROW:CONVW_ROW + 3, 0:DCV // NDEV][None]

    wnames = ["ffn1_w_in", "ffn1_w_out", "ln1_g", "ln1_b", "w_mix_in", "b_forget", "conv_w", "g_attn", "g_conv",
              "w_mix_out", "ln2_g", "ln2_b", "ffn2_w_in", "ffn2_w_out", "ln3_g", "ln3_b", "w_ple", "w_ple_gate",
              "b_ple_gate", "ln4_g", "ln4_b"]
    return (outs["loss"], outs["grad_x"], *[outs[pre + nm] for pre in ("grad_", "delta_", "new_m_", "new_v_")
                                            for nm in wnames])


def _local_step(x0, p0, tgt, full, cw, sp):
    t = x0.shape[0]
    ln1_g, ln1_b, ln2_g, ln2_b, ln3_g, ln3_b = (sp[k] for k in ("ln1_g", "ln1_b", "ln2_g", "ln2_b", "ln3_g", "ln3_b"))
    ln4_g, ln4_b, g_attn, g_conv, b_ple_gate = (sp[k] for k in ("ln4_g", "ln4_b", "g_attn", "g_conv", "b_ple_gate"))
    wmi = full["w_mix_in"]
    w_qkv = wmi[:, 0:3 * DA]
    w_f = jnp.pad(wmi[:, 3 * DA:3 * DA + NH], ((0, 0), (0, LANES - NH)))
    w_bch = wmi[:, 3 * DA + NH:]
    bf_pad = jnp.pad(sp["b_forget"], ((0, 0), (0, LANES - NH)))
    ones = jnp.ones((1, D), F32)
    zeros = jnp.zeros((1, D), F32)

    g1a, u1a, xh1, rs1 = _ffn_fwd(x0, ones, zeros, full["ffn1_w_in"], full["ffn1_w_out"], "ffn1_fwd")
    q, k, v, bch, z, c = _mix_proj_fwd(xh1, ln1_g, ln1_b, w_qkv, w_bch, w_f, bf_pad, "mix_proj_fwd")
    c8 = jnp.transpose(c[:, 0:NH])
    ccol, crow = c8.reshape(NH, t, 1), c8.reshape(NH, 1, t)
    o, lse = _attn_fwd(q, k, v, ccol, crow, "attn_fwd")
    merged, xh2, rs2 = _mix_post_fwd(o, bch, cw, g_attn, g_conv, xh1, ln1_g, ln1_b, full["w_mix_out"],
                                     "mix_post_fwd")
    g2a, u2a, xh3, rs3 = _ffn_fwd(xh2, ln2_g, ln2_b, full["ffn2_w_in"], full["ffn2_w_out"], "ffn2_fwd")

    dr3, dz, de, st_tail = _tail(xh3, rs3, ln3_g, ln3_b, p0, full["w_ple_gate"], full["w_ple"], b_ple_gate,
                                 ln4_g, ln4_b, tgt, "tail")
    df2, dg2, du2, dr2, st_f2 = _ffn_bwd(dr3, g2a, u2a, xh2, rs2, ln2_g, full["ffn2_w_in"], full["ffn2_w_out"],
                                         True, "ffn2_bwd")
    dmix, do, delta, dyc, st_post = _mix_post_bwd(dr2, o, bch, cw, g_attn, g_conv, full["w_mix_out"],
                                                  "mix_post_bwd")
    dbch, st_conv = _conv_bwd(dyc, bch, cw, "conv_bwd")
    drow = jnp.transpose(delta[:, 0:NH]).reshape(NH, 1, t)
    dk, dv, dq, dc, dcq = _attn_bwd(q, k, v, do, lse.reshape(NH, 1, t), drow, crow, ccol, "attn_bwd")
    dc_pad = jnp.pad(jnp.transpose(dc.reshape(NH, t) + dcq.reshape(NH, t)), ((0, 0), (0, LANES - NH)))
    dr1, dfl, st_proj = _mix_proj_bwd(dr2, dq, dk, dv, dbch, dc_pad, z, xh1, rs1, ln1_g, w_qkv, w_bch, w_f,
                                      "mix_proj_bwd")
    df1, dg1, du1, gx, _ = _ffn_bwd(dr1, g1a, u1a, x0, rs1, ones, full["ffn1_w_in"], full["ffn1_w_out"],
                                    False, "ffn1_bwd")

    x1p, x2p, x3p = (xh1, ln1_g, ln1_b), (xh2, ln2_g, ln2_b), (xh3, ln3_g, ln3_b)
    gw = {}
    gw["ffn1_w_in"] = jnp.concatenate(
        [_dw("affine", (x0, ones, zeros), dg1, D, F, "dw_ffn1_in_g", tn=F // 2),
         _dw("affine", (x0, ones, zeros), du1, D, F, "dw_ffn1_in_u", tn=F // 2)], axis=1)
    gw["ffn1_w_out"] = _dw("swiglu", (g1a, u1a), df1, F, D, "dw_ffn1_out", tmm=F // 2)
    gw["ffn2_w_in"] = jnp.concatenate(
        [_dw("affine", x2p, dg2, D, F, "dw_ffn2_in_g", tn=F // 2),
         _dw("affine", x2p, du2, D, F, "dw_ffn2_in_u", tn=F // 2)], axis=1)
    gw["ffn2_w_out"] = _dw("swiglu", (g2a, u2a), df2, F, D, "dw_ffn2_out", tmm=F // 2)
    gw["w_mix_out"] = _dw("plain", (merged,), dmix, D, D, "dw_mix_out")
    gw["w_mix_in"] = jnp.concatenate(
        [_dw("affine", x1p, dq, D, DA, "dw_mix_in_q"), _dw("affine", x1p, dk, D, DA, "dw_mix_in_k"),
         _dw("affine", x1p, dv, D, DA, "dw_mix_in_v"),
         _dw("affine", x1p, dfl, D, LANES, "dw_mix_in_f")[:, 0:NH],
         _dw("affine", x1p, dbch, D, 3 * DCV, "dw_mix_in_bch")], axis=1)
    gw["w_ple_gate"] = _dw("affine", x3p, dz, D, D, "dw_ple_gate")
    gw["w_ple"] = _dw("plain", (p0,), de, PLE, D, "dw_ple")

    loss_part = (0.5 / D) * jnp.sum(st_tail[5:6, :])
    small = {"ln1_g": st_proj[0:1], "ln1_b": st_proj[1:2], "ln2_g": st_f2[0:1], "ln2_b": st_f2[1:2],
             "ln3_g": st_tail[3:4], "ln3_b": st_tail[4:5], "b_ple_gate": st_tail[2:3], "ln4_g": st_tail[0:1],
             "ln4_b": st_tail[1:2], "g_attn": st_post[0:1], "g_conv": st_post[1:2],
             "b_forget": st_proj[2:3, 0:NH], "conv_w": st_conv[0:3]}
    return gx, gw, small, loss_part
```

```python
import functools
import math

import jax
import jax.numpy as jnp
from jax import lax
from jax.experimental import pallas as pl
from jax.experimental.pallas import tpu as pltpu

D = 1024
F = 2816
NH = 8
DH = 64
DA = NH * DH
DCV = D - DA
PLE = 256
NPROJ = 3 * DA + NH + 3 * DCV
LN_EPS = 1e-5
RMS_EPS = 1e-6
NEG = -1e30
ALPHA = 2.0 ** 0.25
NDEV = 8
LANES = 128

ADAM_LR, ADAM_B1, ADAM_B2, ADAM_EPS, ADAM_WD, ADAM_STEP = 0.001, 0.9, 0.999, 1e-08, 0.01, 10

F32 = jnp.float32
MXU_DT = jnp.bfloat16
WIRE_DT = jnp.bfloat16

MESH_ID = pl.DeviceIdType.MESH
ANY = pl.BlockSpec(memory_space=pl.ANY)


def _params(vmem_mb, n_axes=1):
    return pltpu.CompilerParams(dimension_semantics=("arbitrary",) * n_axes,
                                vmem_limit_bytes=int(vmem_mb) << 20)


def _mm(a, b):
    return jnp.dot(a, b, preferred_element_type=F32)


def _mm_nt(a, b):
    return lax.dot_general(a, b, (((1,), (1,)), ((), ())), preferred_element_type=F32)


def _mm_tn(a, b):
    return lax.dot_general(a, b, (((0,), (0,)), ((), ())), preferred_element_type=F32)


def _split3(x):
    hi = x.astype(MXU_DT)
    r1 = x - hi.astype(F32)
    mid = r1.astype(MXU_DT)
    lo = (r1 - mid.astype(F32)).astype(MXU_DT)
    return hi, mid, lo


def _mm_sel(sel, x):
    hi, mid, lo = _split3(x)
    return _mm(sel, hi) + _mm(sel, mid) + _mm(sel, lo)


def _mm_xsel(x, sel):
    hi, mid, lo = _split3(x)
    return _mm(hi, sel) + _mm(mid, sel) + _mm(lo, sel)


def _sigmoid(x):
    return 1.0 / (1.0 + jnp.exp(-x))


def _ln_fwd(r):
    mu = jnp.mean(r, axis=-1, keepdims=True)
    xc = r - mu
    var = jnp.mean(xc * xc, axis=-1, keepdims=True)
    rstd = lax.rsqrt(var + LN_EPS)
    return xc * rstd, rstd


def _ln_bwd(dxhat, xhat, rstd):
    m1 = jnp.mean(dxhat, axis=-1, keepdims=True)
    m2 = jnp.mean(dxhat * xhat, axis=-1, keepdims=True)
    return rstd * (dxhat - m1 - xhat * m2)


def _rms_fwd(x):
    r = lax.rsqrt(jnp.mean(x * x, axis=-1, keepdims=True) + RMS_EPS)
    return x * r, r


def _rms_bwd(dyg, xn, r):
    return r * (dyg - xn * jnp.mean(dyg * xn, axis=-1, keepdims=True))


def _colsum(x):
    return jnp.sum(x, axis=0, keepdims=True)


def _f_chunks():
    out, c0 = [], 0
    while c0 < F:
        fc = min(512, F - c0)
        out.append((c0, fc))
        c0 += fc
    return out


def _tile(t, want):
    return want if t % want == 0 and t >= want else t


def _exchange(arrs, gather, name):
    n = len(arrs)
    out_shape = []
    for a in arrs:
        shp = ((NDEV,) + a.shape) if gather else a.shape
        out_shape.append(jax.ShapeDtypeStruct(shp, a.dtype))

    def body(*refs):
        ins, outs = refs[:n], refs[n:2 * n]
        send_sems, recv_sems, loc_sems = refs[2 * n:]
        x, y, c = lax.axis_index("x"), lax.axis_index("y"), lax.axis_index("c")
        me = 4 * x + 2 * y + c
        peers = []
        for k in range(1, NDEV):
            px = 1 - x if (k >> 2) & 1 else x
            py = 1 - y if (k >> 1) & 1 else y
            pc = 1 - c if k & 1 else c
            peers.append(((px, py, pc), 4 * px + 2 * py + pc))

        def remote(w, k, slot_src, slot_dst):
            src = ins[w] if gather else ins[w].at[slot_src]
            return pltpu.make_async_remote_copy(
                src_ref=src, dst_ref=outs[w].at[slot_dst],
                send_sem=send_sems.at[w * (NDEV - 1) + k], recv_sem=recv_sems.at[w * (NDEV - 1) + k],
                device_id=peers[k][0], device_id_type=MESH_ID)

        local = []
        for w in range(n):
            lc = pltpu.make_async_copy(ins[w] if gather else ins[w].at[me], outs[w].at[me], loc_sems.at[w])
            lc.start()
            local.append(lc)
        for k in range(NDEV - 1):
            for w in range(n):
                remote(w, k, peers[k][1], me).start()
        for k in range(NDEV - 1):
            for w in range(n):
                remote(w, k, me, peers[k][1]).wait_recv()
        for k in range(NDEV - 1):
            for w in range(n):
                remote(w, k, peers[k][1], me).wait_send()
        for lc in local:
            lc.wait()

    return pl.pallas_call(
        body, name=name, out_shape=tuple(out_shape),
        in_specs=[ANY] * n, out_specs=tuple([ANY] * n),
        scratch_shapes=[pltpu.SemaphoreType.DMA((n * (NDEV - 1),)), pltpu.SemaphoreType.DMA((n * (NDEV - 1),)),
                        pltpu.SemaphoreType.DMA((n,))],
    )(*arrs)


def _ffn_fwd(xin, gin, bin_, w_in, w_out, name):
    t = xin.shape[0]
    tm = _tile(t, 512)
    chunks = _f_chunks()

    def body(x_ref, gi_ref, bi_ref, win_hbm, wout_hbm, g_ref, u_ref, xh_ref, rs_ref, win_v, wout_v, acc_ref):
        @pl.when(pl.program_id(0) == 0)
        def _():
            pltpu.sync_copy(win_hbm, win_v)
            pltpu.sync_copy(wout_hbm, wout_v)

        x = x_ref[...] * gi_ref[...] + bi_ref[...]
        xb = x.astype(MXU_DT)
        for ci, (c0, fc) in enumerate(chunks):
            gc = _mm(xb, win_v[:, c0:c0 + fc])
            uc = _mm(xb, win_v[:, F + c0:F + c0 + fc])
            g_ref[:, c0:c0 + fc] = gc.astype(g_ref.dtype)
            u_ref[:, c0:c0 + fc] = uc.astype(u_ref.dtype)
            hc = (gc * _sigmoid(gc) * uc).astype(MXU_DT)
            part = _mm(hc, wout_v[c0:c0 + fc, :])
            if ci == 0:
                acc_ref[...] = part
            else:
                acc_ref[...] += part
        xh, rstd = _ln_fwd(ALPHA * x + 0.5 * acc_ref[...])
        xh_ref[...] = xh
        rs_ref[...] = rstd

    row = pl.BlockSpec((tm, D), lambda i: (i, 0))
    vec = pl.BlockSpec((1, D), lambda i: (0, 0))
    act = pl.BlockSpec((tm, F), lambda i: (i, 0))
    return pl.pallas_call(
        body, name=name, grid=(t // tm,),
        in_specs=[row, vec, vec, ANY, ANY],
        out_specs=(act, act, row, pl.BlockSpec((tm, 1), lambda i: (i, 0))),
        out_shape=(jax.ShapeDtypeStruct((t, F), MXU_DT), jax.ShapeDtypeStruct((t, F), MXU_DT),
                   jax.ShapeDtypeStruct((t, D), F32), jax.ShapeDtypeStruct((t, 1), F32)),
        scratch_shapes=[pltpu.VMEM((D, 2 * F), MXU_DT), pltpu.VMEM((F, D), MXU_DT), pltpu.VMEM((tm, D), F32)],
        compiler_params=_params(52),
    )(xin, gin, bin_, w_in, w_out)


def _mix_proj_fwd(xh1, g1, b1, w_qkv, w_bch, w_f, bf_pad, name):
    t = xh1.shape[0]
    tm = _tile(t, 512)

    def body(x_ref, g_ref, b_ref, wq_ref, wb_ref, wf_ref, bf_ref, q_ref, k_ref, v_ref, bch_ref, z_ref, c_ref, carry):
        @pl.when(pl.program_id(0) == 0)
        def _():
            carry[...] = jnp.zeros_like(carry)

        xb = (x_ref[...] * g_ref[...] + b_ref[...]).astype(MXU_DT)
        qkv = _mm(xb, wq_ref[...])
        q_ref[...] = qkv[:, 0:DA].astype(q_ref.dtype)
        k_ref[...] = qkv[:, DA:2 * DA].astype(k_ref.dtype)
        v_ref[...] = qkv[:, 2 * DA:3 * DA].astype(v_ref.dtype)
        bch_ref[...] = _mm(xb, wb_ref[...])
        z = _mm(xb, wf_ref[...]) + bf_ref[...]
        z_ref[...] = z
        logf = jnp.minimum(z, 0.0) - jnp.log(1.0 + jnp.exp(-jnp.abs(z)))
        row = lax.broadcasted_iota(jnp.int32, (tm, tm), 0)
        col = lax.broadcasted_iota(jnp.int32, (tm, tm), 1)
        tri = jnp.where(row >= col, 1.0, 0.0).astype(MXU_DT)
        c = carry[...] + _mm_sel(tri, logf)
        c_ref[...] = c
        carry[...] = c[tm - 1:tm, :]

    row = lambda w: pl.BlockSpec((tm, w), lambda i: (i, 0))
    full = lambda a: pl.BlockSpec(a.shape, lambda i: (0, 0))
    return pl.pallas_call(
        body, name=name, grid=(t // tm,),
        in_specs=[row(D), full(g1), full(b1), full(w_qkv), full(w_bch), full(w_f), full(bf_pad)],
        out_specs=(row(DA), row(DA), row(DA), row(3 * DCV), row(LANES), row(LANES)),
        out_shape=(jax.ShapeDtypeStruct((t, DA), MXU_DT),) * 3
        + (jax.ShapeDtypeStruct((t, 3 * DCV), F32), jax.ShapeDtypeStruct((t, LANES), F32),
           jax.ShapeDtypeStruct((t, LANES), F32)),
        scratch_shapes=[pltpu.VMEM((1, LANES), F32)],
        compiler_params=_params(48),
    )(xh1, g1, b1, w_qkv, w_bch, w_f, bf_pad)


def _attn_operands(q, k, v, c8, tq):
    t = q.shape[0]
    nq = t // tq
    c3 = c8.reshape(NH, nq, tq)
    r = c3[:, :, 0]
    crel = (c3 - r[:, :, None]).reshape(NH, t)
    rnd = (lambda x: lax.reduce_precision(x, 8, 7)) if MXU_DT == jnp.bfloat16 else (lambda x: x)
    hi = rnd(crel)
    mid = rnd(crel - hi)
    lo = rnd(crel - hi - mid)
    cs = jnp.stack([hi, mid, lo], axis=-1).astype(MXU_DT)
    one3 = jnp.ones((NH, t, 3), MXU_DT)
    pad = jnp.zeros((NH, t, LANES - DH - 6), MXU_DT)
    heads = lambda x: jnp.transpose(x.reshape(t, NH, DH), (1, 0, 2))
    qa = jnp.concatenate([heads(q) * jnp.asarray(0.125, q.dtype), cs, one3, pad], axis=-1)
    ka = jnp.concatenate([heads(k), one3, -cs, pad], axis=-1)
    vt = jnp.transpose(v.reshape(t, NH, DH), (1, 2, 0))
    one_row = jnp.ones((NH, 1, t), MXU_DT)
    z_rows = jnp.zeros((NH, LANES - DH - 1, t), MXU_DT)
    odd = (jnp.arange(NH) % 2 == 1)[:, None, None]
    vta = jnp.where(odd, jnp.concatenate([one_row, z_rows, vt], axis=1),
                    jnp.concatenate([vt, one_row, z_rows], axis=1))
    return qa, ka, vta, r


def _attn_fwd(qa, ka, vta, r, name):
    t = qa.shape[1]
    tq = _tile(t, 512)
    nq = t // tq

    def body(r_ref, q_ref, k_ref, v_ref, o_ref, l_ref):
        hp, i = pl.program_id(0), pl.program_id(1)
        key = lax.broadcasted_iota(jnp.int32, (tq, tq), 0)
        qry = lax.broadcasted_iota(jnp.int32, (tq, tq), 1)

        def step(j, carry, masked):
            off = pl.multiple_of(j * tq, tq)
            out = []
            for a in range(2):
                m, acc = carry[a]
                st = _mm_nt(k_ref[a, pl.ds(off, tq), :], q_ref[a])
                if masked:
                    st = jnp.where(qry >= key, st, NEG)
                d = r_ref[2 * hp + a, i] - r_ref[2 * hp + a, j]
                m_new = jnp.maximum(m, jnp.max(st, axis=0, keepdims=True) + d)
                pt = jnp.exp(st - (m_new - d))
                acc = jnp.exp(m - m_new) * acc + _mm(v_ref[a, :, pl.ds(off, tq)], pt.astype(MXU_DT))
                out.append((m_new, acc))
            return tuple(out)

        init = tuple((jnp.full((1, tq), NEG, F32), jnp.zeros((LANES, tq), F32)) for _ in range(2))
        carry = lax.fori_loop(0, i, lambda j, cr: step(j, cr, False), init)
        (ma, acca), (mb, accb) = step(i, carry, True)
        la, lb = acca[DH:DH + 1, :], accb[0:1, :]
        l_ref[0] = ma + jnp.log(la)
        l_ref[1] = mb + jnp.log(lb)
        sub = lax.broadcasted_iota(jnp.int32, (LANES, tq), 0)
        o_ref[...] = jnp.where(sub < DH, acca / la, accb / lb).T

    return pl.pallas_call(
        body, name=name, grid=(NH // 2, nq),
        in_specs=[pl.BlockSpec(memory_space=pltpu.SMEM),
                  pl.BlockSpec((2, tq, LANES), lambda p, i: (p, i, 0)),
                  pl.BlockSpec((2, t, LANES), lambda p, i: (p, 0, 0)),
                  pl.BlockSpec((2, LANES, t), lambda p, i: (p, 0, 0))],
        out_specs=(pl.BlockSpec((tq, LANES), lambda p, i: (i, p)),
                   pl.BlockSpec((2, 1, tq), lambda p, i: (p, 0, i))),
        out_shape=(jax.ShapeDtypeStruct((t, DA), F32), jax.ShapeDtypeStruct((NH, 1, t), F32)),
        compiler_params=_params(48, 2),
    )(r, qa, ka, vta)


def _conv_parts(bch):
    return bch[:, 0:DCV], bch[:, DCV:2 * DCV], bch[:, 2 * DCV:3 * DCV]


def _mix_post_fwd(o, bch, conv_w, g_attn, g_conv, xh1, g1, b1, w_mo, name):
    t = o.shape[0]
    tm = _tile(t, 512)
    hb = tm // 8

    def body(o_ref, bch_ref, halo_ref, cw_ref, ga_ref, gc_ref, x_ref, g_ref, b_ref, w_ref,
             mg_ref, xh_ref, rs_ref, ext):
        i = pl.program_id(0)
        an, _ = _rms_fwd(o_ref[...])
        mg_ref[:, 0:DA] = (an * ga_ref[...]).astype(mg_ref.dtype)
        bb, cc, hh = _conv_parts(bch_ref[...])
        _, hc, hh_h = _conv_parts(halo_ref[...])
        u = cc * hh
        ext[0:8, :] = jnp.where(i > 0, hc * hh_h, 0.0)
        ext[8:8 + tm, :] = u
        raw = cw_ref[0:1, :] * ext[6:6 + tm, :] + cw_ref[1:2, :] * ext[7:7 + tm, :] + cw_ref[2:3, :] * u
        cn, _ = _rms_fwd(bb * raw)
        mg_ref[:, DA:D] = (cn * gc_ref[...]).astype(mg_ref.dtype)
        x1 = x_ref[...] * g_ref[...] + b_ref[...]
        xh, rstd = _ln_fwd(ALPHA * x1 + _mm(mg_ref[...], w_ref[...]))
        xh_ref[...] = xh
        rs_ref[...] = rstd

    row = lambda w: pl.BlockSpec((tm, w), lambda i: (i, 0))
    full = lambda a: pl.BlockSpec(a.shape, lambda i: (0, 0))
    return pl.pallas_call(
        body, name=name, grid=(t // tm,),
        in_specs=[row(DA), row(3 * DCV),
                  pl.BlockSpec((8, 3 * DCV), lambda i: (jnp.maximum(i * hb - 1, 0), 0)),
                  full(conv_w), full(g_attn), full(g_conv), row(D), full(g1), full(b1), full(w_mo)],
        out_specs=(row(D), row(D), pl.BlockSpec((tm, 1), lambda i: (i, 0))),
        out_shape=(jax.ShapeDtypeStruct((t, D), MXU_DT), jax.ShapeDtypeStruct((t, D), F32),
                   jax.ShapeDtypeStruct((t, 1), F32)),
        scratch_shapes=[pltpu.VMEM((tm + 8, DCV), F32)],
        compiler_params=_params(48),
    )(o, bch, bch, conv_w, g_attn, g_conv, xh1, g1, b1, w_mo)


def _tail(xh3, rs3, g3, b3, p, w_g, w_ple, bg, g4, b4, target, name):
    t = xh3.shape[0]
    tm = _tile(t, 512)

    def body(x_ref, rs_ref, g3_ref, b3_ref, p_ref, wg_ref, wp_ref, bg_ref, g4_ref, b4_ref, t_ref,
             dr_ref, dz_ref, de_ref, st_ref):
        @pl.when(pl.program_id(0) == 0)
        def _():
            st_ref[...] = jnp.zeros_like(st_ref)

        xh3v = x_ref[...]
        x3 = xh3v * g3_ref[...] + b3_ref[...]
        gate = _sigmoid(_mm(x3.astype(MXU_DT), wg_ref[...]) + bg_ref[...])
        e = _mm(p_ref[...].astype(MXU_DT), wp_ref[...])
        xh4, rstd4 = _ln_fwd(ALPHA * x3 + gate * e)
        diff = xh4 * g4_ref[...] + b4_ref[...] - t_ref[...]
        dy = diff * (1.0 / D)
        st_ref[5:6, :] += _colsum(diff * diff)
        st_ref[0:1, :] += _colsum(dy * xh4)
        st_ref[1:2, :] += _colsum(dy)
        dr4 = _ln_bwd(dy * g4_ref[...], xh4, rstd4)
        de_ref[...] = (dr4 * gate).astype(de_ref.dtype)
        dz = dr4 * e * gate * (1.0 - gate)
        st_ref[2:3, :] += _colsum(dz)
        dzb = dz.astype(MXU_DT)
        dz_ref[...] = dzb
        dx3 = ALPHA * dr4 + _mm_nt(dzb, wg_ref[...])
        st_ref[3:4, :] += _colsum(dx3 * xh3v)
        st_ref[4:5, :] += _colsum(dx3)
        dr_ref[...] = _ln_bwd(dx3 * g3_ref[...], xh3v, rs_ref[...])

    row = lambda w: pl.BlockSpec((tm, w), lambda i: (i, 0))
    full = lambda a: pl.BlockSpec(a.shape, lambda i: (0, 0))
    return pl.pallas_call(
        body, name=name, grid=(t // tm,),
        in_specs=[row(D), row(1), full(g3), full(b3), row(PLE), full(w_g), full(w_ple), full(bg), full(g4),
                  full(b4), row(D)],
        out_specs=(row(D), row(D), row(D), pl.BlockSpec((8, D), lambda i: (0, 0))),
        out_shape=(jax.ShapeDtypeStruct((t, D), F32), jax.ShapeDtypeStruct((t, D), MXU_DT),
                   jax.ShapeDtypeStruct((t, D), MXU_DT), jax.ShapeDtypeStruct((8, D), F32)),
        compiler_params=_params(48),
    )(xh3, rs3, g3, b3, p, w_g, w_ple, bg, g4, b4, target)


def _ffn_bwd(dr, gact, uact, xin, rsin, gin, w_in, w_out, prev_ln, name):
    t = dr.shape[0]
    tm = _tile(t, 256)
    chunks = _f_chunks()

    def body(dr_ref, g_ref, u_ref, x_ref, rs_ref, gi_ref, win_hbm, wout_hbm,
             df_ref, dg_ref, du_ref, dx_ref, st_ref, win_v, wout_v, acc_ref):
        @pl.when(pl.program_id(0) == 0)
        def _():
            pltpu.sync_copy(win_hbm, win_v)
            pltpu.sync_copy(wout_hbm, wout_v)
            st_ref[...] = jnp.zeros_like(st_ref)

        drv = dr_ref[...]
        dfb = (0.5 * drv).astype(MXU_DT)
        df_ref[...] = dfb
        for ci, (c0, fc) in enumerate(chunks):
            dh = _mm_nt(dfb, wout_v[c0:c0 + fc, :])
            g = g_ref[:, c0:c0 + fc].astype(F32)
            u = u_ref[:, c0:c0 + fc].astype(F32)
            sg = _sigmoid(g)
            dgb = (dh * u * (sg * (1.0 + g * (1.0 - sg)))).astype(MXU_DT)
            dub = (dh * (g * sg)).astype(MXU_DT)
            dg_ref[:, c0:c0 + fc] = dgb
            du_ref[:, c0:c0 + fc] = dub
            part = _mm_nt(dgb, win_v[:, c0:c0 + fc]) + _mm_nt(dub, win_v[:, F + c0:F + c0 + fc])
            if ci == 0:
                acc_ref[...] = part
            else:
                acc_ref[...] += part
        dx = ALPHA * drv + acc_ref[...]
        if prev_ln:
            xh = x_ref[...]
            st_ref[0:1, :] += _colsum(dx * xh)
            st_ref[1:2, :] += _colsum(dx)
            dx_ref[...] = _ln_bwd(dx * gi_ref[...], xh, rs_ref[...])
        else:
            dx_ref[...] = dx

    row = pl.BlockSpec((tm, D), lambda i: (i, 0))
    vec = pl.BlockSpec((1, D), lambda i: (0, 0))
    act = pl.BlockSpec((tm, F), lambda i: (i, 0))
    return pl.pallas_call(
        body, name=name, grid=(t // tm,),
        in_specs=[row, act, act, row, pl.BlockSpec((tm, 1), lambda i: (i, 0)), vec, ANY, ANY],
        out_specs=(row, act, act, row, pl.BlockSpec((8, D), lambda i: (0, 0))),
        out_shape=(jax.ShapeDtypeStruct((t, D), MXU_DT), jax.ShapeDtypeStruct((t, F), MXU_DT),
                   jax.ShapeDtypeStruct((t, F), MXU_DT), jax.ShapeDtypeStruct((t, D), F32),
                   jax.ShapeDtypeStruct((8, D), F32)),
        scratch_shapes=[pltpu.VMEM((D, 2 * F), MXU_DT), pltpu.VMEM((F, D), MXU_DT), pltpu.VMEM((tm, D), F32)],
        compiler_params=_params(52),
    )(dr, gact, uact, xin, rsin, gin, w_in, w_out)


def _mix_post_bwd(dr2, o, bch, conv_w, g_attn, g_conv, w_mo, name):
    t = dr2.shape[0]
    tm = _tile(t, 512)
    hb = tm // 8

    def body(dr_ref, o_ref, bch_ref, halo_ref, cw_ref, ga_ref, gc_ref, w_ref,
             dm_ref, do_ref, dl_ref, dy_ref, st_ref, ext):
        i = pl.program_id(0)

        @pl.when(i == 0)
        def _():
            st_ref[...] = jnp.zeros_like(st_ref)

        dmb = dr_ref[...].astype(MXU_DT)
        dm_ref[...] = dmb
        dmg = _mm_nt(dmb, w_ref[...])
        ov = o_ref[...]
        an, ra = _rms_fwd(ov)
        da = dmg[:, 0:DA]
        st_ref[0:1, :] += _colsum(da * an)
        dxa = _rms_bwd(da * ga_ref[...], an, ra)
        dob = dxa.astype(do_ref.dtype)
        do_ref[...] = dob
        crow = lax.broadcasted_iota(jnp.int32, (DA, LANES), 0)
        ccol = lax.broadcasted_iota(jnp.int32, (DA, LANES), 1)
        sel = jnp.where((crow // DH) == ccol, 1.0, 0.0).astype(MXU_DT)
        dl_ref[...] = _mm_xsel(dob.astype(F32) * ov, sel)
        bb, cc, hh = _conv_parts(bch_ref[...])
        _, hc, hh_h = _conv_parts(halo_ref[...])
        u = cc * hh
        ext[0:8, :] = jnp.where(i > 0, hc * hh_h, 0.0)
        ext[8:8 + tm, :] = u
        raw = cw_ref[0:1, :] * ext[6:6 + tm, :] + cw_ref[1:2, :] * ext[7:7 + tm, :] + cw_ref[2:3, :] * u
        cn, rc = _rms_fwd(bb * raw)
        dcn = dmg[:, DA:D]
        st_ref[1:2, :] += _colsum(dcn * cn)
        dy_ref[...] = _rms_bwd(dcn * gc_ref[...], cn, rc)

    row = lambda w: pl.BlockSpec((tm, w), lambda i: (i, 0))
    full = lambda a: pl.BlockSpec(a.shape, lambda i: (0, 0))
    return pl.pallas_call(
        body, name=name, grid=(t // tm,),
        in_specs=[row(D), row(DA), row(3 * DCV),
                  pl.BlockSpec((8, 3 * DCV), lambda i: (jnp.maximum(i * hb - 1, 0), 0)),
                  full(conv_w), full(g_attn), full(g_conv), full(w_mo)],
        out_specs=(row(D), row(DA), row(LANES), row(DCV), pl.BlockSpec((8, DA), lambda i: (0, 0))),
        out_shape=(jax.ShapeDtypeStruct((t, D), MXU_DT), jax.ShapeDtypeStruct((t, DA), MXU_DT),
                   jax.ShapeDtypeStruct((t, LANES), F32), jax.ShapeDtypeStruct((t, DCV), F32),
                   jax.ShapeDtypeStruct((8, DA), F32)),
        scratch_shapes=[pltpu.VMEM((tm + 8, DCV), F32)],
        compiler_params=_params(48),
    )(dr2, o, bch, bch, conv_w, g_attn, g_conv, w_mo)


def _conv_bwd(dy, bch, conv_w, name):
    t = dy.shape[0]
    tm = _tile(t, 512)
    hb = tm // 8
    nt = t // tm

    def body(dy_ref, dyn_ref, bch_ref, prev_ref, next_ref, cw_ref, out_ref, st_ref, ext_u, ext_d):
        i = pl.program_id(0)

        @pl.when(i == 0)
        def _():
            st_ref[...] = jnp.zeros_like(st_ref)

        bb, cc, hh = _conv_parts(bch_ref[...])
        _, pc, ph = _conv_parts(prev_ref[...])
        nb, _, _ = _conv_parts(next_ref[...])
        u = cc * hh
        ext_u[0:8, :] = jnp.where(i > 0, pc * ph, 0.0)
        ext_u[8:8 + tm, :] = u
        u1 = ext_u[7:7 + tm, :]
        u2 = ext_u[6:6 + tm, :]
        w0, w1, w2 = cw_ref[0:1, :], cw_ref[1:2, :], cw_ref[2:3, :]
        dyv = dy_ref[...]
        out_ref[:, 0:DCV] = (dyv * (w0 * u2 + w1 * u1 + w2 * u)).astype(out_ref.dtype)
        dcr = dyv * bb
        ext_d[0:tm, :] = dcr
        ext_d[tm:tm + 8, :] = jnp.where(i < nt - 1, dyn_ref[...] * nb, 0.0)
        du = w2 * dcr + w1 * ext_d[1:1 + tm, :] + w0 * ext_d[2:2 + tm, :]
        out_ref[:, DCV:2 * DCV] = (du * hh).astype(out_ref.dtype)
        out_ref[:, 2 * DCV:3 * DCV] = (du * cc).astype(out_ref.dtype)
        st_ref[0:1, :] += _colsum(dcr * u2)
        st_ref[1:2, :] += _colsum(dcr * u1)
        st_ref[2:3, :] += _colsum(dcr * u)

    row = lambda w: pl.BlockSpec((tm, w), lambda i: (i, 0))
    prev = lambda w: pl.BlockSpec((8, w), lambda i: (jnp.maximum(i * hb - 1, 0), 0))
    nxt = lambda w: pl.BlockSpec((8, w), lambda i: (jnp.minimum((i + 1) * hb, nt * hb - 1), 0))
    return pl.pallas_call(
        body, name=name, grid=(nt,),
        in_specs=[row(DCV), nxt(DCV), row(3 * DCV), prev(3 * DCV), nxt(3 * DCV),
                  pl.BlockSpec(conv_w.shape, lambda i: (0, 0))],
        out_specs=(row(3 * DCV), pl.BlockSpec((8, DCV), lambda i: (0, 0))),
        out_shape=(jax.ShapeDtypeStruct((t, 3 * DCV), MXU_DT), jax.ShapeDtypeStruct((8, DCV), F32)),
        scratch_shapes=[pltpu.VMEM((tm + 8, DCV), F32), pltpu.VMEM((tm + 8, DCV), F32)],
        compiler_params=_params(48),
    )(dy, dy, bch, bch, bch, conv_w)


def _attn_bwd(q, k, v, do, lrow, drow, crow, ccol, name):
    t = q.shape[0]
    tq = _tile(t, 512)
    nq = t // tq

    def body(k_ref, v_ref, q_ref, do_ref, l_ref, dl_ref, cr_ref, cc_ref, dk_ref, dv_ref, dq_ref, dc_ref, dcq_ref):
        j = pl.program_id(1)

        @pl.when(j == 0)
        def _():
            dq_ref[...] = jnp.zeros_like(dq_ref)
            dcq_ref[...] = jnp.zeros_like(dcq_ref)

        lane = lax.broadcasted_iota(jnp.int32, (1, LANES), 1)
        rowi = lax.broadcasted_iota(jnp.int32, (tq, tq), 0)
        coli = lax.broadcasted_iota(jnp.int32, (tq, tq), 1)
        res = []
        for a in range(2):
            head = (lane // DH) == a
            ka = jnp.where(head, k_ref[...], jnp.zeros_like(k_ref[...])) * jnp.asarray(0.125, k_ref.dtype)
            va = jnp.where(head, v_ref[...], jnp.zeros_like(v_ref[...]))
            cck = cc_ref[a]

            def step(i, carry, masked):
                dk, dv, dc = carry
                off = pl.multiple_of(i * tq, tq)
                qi = q_ref[pl.ds(off, tq), :]
                doi = do_ref[pl.ds(off, tq), :]
                st = _mm_nt(ka, qi) + (cr_ref[a, :, pl.ds(off, tq)] - cck)
                if masked:
                    st = jnp.where(coli >= rowi, st, NEG)
                pt = jnp.exp(st - l_ref[a, :, pl.ds(off, tq)])
                dv = dv + _mm(pt.astype(MXU_DT), doi)
                dst = pt * (_mm_nt(va, doi) - dl_ref[a, :, pl.ds(off, tq)])
                dc = dc - jnp.sum(dst, axis=-1, keepdims=True)
                dcq_ref[a, :, pl.ds(off, tq)] += jnp.sum(dst, axis=0, keepdims=True)
                dsb = dst.astype(MXU_DT)
                dk = dk + _mm(dsb, qi)
                dq_ref[pl.ds(off, tq), :] += _mm_tn(dsb, ka)
                return dk, dv, dc

            init = (jnp.zeros((tq, LANES), F32), jnp.zeros((tq, LANES), F32), jnp.zeros((tq, 1), F32))
            carry = step(j, init, True)
            dk, dv, dc = lax.fori_loop(j + 1, nq, lambda i, cr: step(i, cr, False), carry)
            res.append((dk, dv))
            dc_ref[a] = dc
        dk_ref[...] = (jnp.where(lane < DH, res[0][0], res[1][0]) * 0.125).astype(dk_ref.dtype)
        dv_ref[...] = jnp.where(lane < DH, res[0][1], res[1][1]).astype(dv_ref.dtype)

    tile = pl.BlockSpec((tq, LANES), lambda p, j: (j, p))
    whole = pl.BlockSpec((t, LANES), lambda p, j: (0, p))
    rowv = pl.BlockSpec((2, 1, t), lambda p, j: (p, 0, 0))
    colv = pl.BlockSpec((2, tq, 1), lambda p, j: (p, j, 0))
    return pl.pallas_call(
        body, name=name, grid=(NH // 2, nq),
        in_specs=[tile, tile, whole, whole, rowv, rowv, rowv, colv],
        out_specs=(tile, tile, whole, colv, rowv),
        out_shape=(jax.ShapeDtypeStruct((t, DA), MXU_DT), jax.ShapeDtypeStruct((t, DA), MXU_DT),
                   jax.ShapeDtypeStruct((t, DA), F32), jax.ShapeDtypeStruct((NH, t, 1), F32),
                   jax.ShapeDtypeStruct((NH, 1, t), F32)),
        compiler_params=_params(52, 2),
    )(k, v, q, do, lrow, drow, crow, ccol)


def _mix_proj_bwd(dr2, dq, dk, dv, dbch, dc, z, xh1, rs1, g1, w_qkv, w_bch, w_f, name):
    t = dr2.shape[0]
    tm = _tile(t, 512)
    nt = t // tm

    def body(dr_ref, dq_ref, dk_ref, dv_ref, db_ref, dc_ref, z_ref, x_ref, rs_ref, g_ref,
             wq_ref, wb_ref, wf_ref, out_ref, df_ref, st_ref, carry):
        @pl.when(pl.program_id(0) == 0)
        def _():
            carry[...] = jnp.zeros_like(carry)
            st_ref[...] = jnp.zeros_like(st_ref)

        row = lax.broadcasted_iota(jnp.int32, (tm, tm), 0)
        col = lax.broadcasted_iota(jnp.int32, (tm, tm), 1)
        triu = jnp.where(col >= row, 1.0, 0.0).astype(MXU_DT)
        dlogf = carry[...] + _mm_sel(triu, dc_ref[...])
        carry[...] = dlogf[0:1, :]
        dz = dlogf / (1.0 + jnp.exp(z_ref[...]))
        st_ref[2:3, 0:LANES] += _colsum(dz)
        dfb = dz.astype(MXU_DT)
        df_ref[...] = dfb
        dx = (ALPHA * dr_ref[...]
              + _mm_nt(dq_ref[...].astype(MXU_DT), wq_ref[:, 0:DA])
              + _mm_nt(dk_ref[...], wq_ref[:, DA:2 * DA])
              + _mm_nt(dv_ref[...], wq_ref[:, 2 * DA:3 * DA])
              + _mm_nt(db_ref[...], wb_ref[...])
              + _mm_nt(dfb, wf_ref[...]))
        xh = x_ref[...]
        st_ref[0:1, :] += _colsum(dx * xh)
        st_ref[1:2, :] += _colsum(dx)
        out_ref[...] = _ln_bwd(dx * g_ref[...], xh, rs_ref[...])

    row = lambda w: pl.BlockSpec((tm, w), lambda i: (nt - 1 - i, 0))
    full = lambda a: pl.BlockSpec(a.shape, lambda i: (0, 0))
    return pl.pallas_call(
        body, name=name, grid=(nt,),
        in_specs=[row(D), row(DA), row(DA), row(DA), row(3 * DCV), row(LANES), row(LANES), row(D), row(1),
                  full(g1), full(w_qkv), full(w_bch), full(w_f)],
        out_specs=(row(D), row(LANES), pl.BlockSpec((8, D), lambda i: (0, 0))),
        out_shape=(jax.ShapeDtypeStruct((t, D), F32), jax.ShapeDtypeStruct((t, LANES), MXU_DT),
                   jax.ShapeDtypeStruct((8, D), F32)),
        scratch_shapes=[pltpu.VMEM((1, LANES), F32)],
        compiler_params=_params(48),
    )(dr2, dq, dk, dv, dbch, dc, z, xh1, rs1, g1, w_qkv, w_bch, w_f)


def _dw(mode, a_parts, b, m, n, name, tmm=None, tn=None):
    t = b.shape[0]
    tmm = tmm or m
    tn = tn or n
    tt = _tile(t, 1024)

    def body(*refs):
        a_refs, b_ref, o_ref = refs[:len(a_parts)], refs[len(a_parts)], refs[len(a_parts) + 1]

        @pl.when(pl.program_id(2) == 0)
        def _():
            o_ref[...] = jnp.zeros_like(o_ref)

        if mode == "plain":
            a = a_refs[0][...].astype(MXU_DT)
        elif mode == "affine":
            a = (a_refs[0][...] * a_refs[1][...] + a_refs[2][...]).astype(MXU_DT)
        else:
            g = a_refs[0][...].astype(F32)
            a = (g * _sigmoid(g) * a_refs[1][...].astype(F32)).astype(MXU_DT)
        o_ref[...] += _mm_tn(a, b_ref[...].astype(MXU_DT))

    a_tile = pl.BlockSpec((tt, tmm), lambda i, j, k: (k, i))
    a_vec = pl.BlockSpec((1, tmm), lambda i, j, k: (0, i))
    a_specs = {"plain": [a_tile], "affine": [a_tile, a_vec, a_vec], "swiglu": [a_tile, a_tile]}[mode]
    return pl.pallas_call(
        body, name=name, grid=(m // tmm, n // tn, t // tt),
        in_specs=a_specs + [pl.BlockSpec((tt, tn), lambda i, j, k: (k, j))],
        out_specs=pl.BlockSpec((tmm, tn), lambda i, j, k: (i, j)),
        out_shape=jax.ShapeDtypeStruct((m, n), F32),
        compiler_params=_params(48, 3),
    )(*a_parts, b)


def _adamw(w, g, m, v):
    m = ADAM_B1 * m + (1.0 - ADAM_B1) * g
    v = ADAM_B2 * v + (1.0 - ADAM_B2) * (g * g)
    m_hat = m / (1.0 - ADAM_B1 ** ADAM_STEP)
    v_hat = v / (1.0 - ADAM_B2 ** ADAM_STEP)
    delta = -ADAM_LR * (m_hat / (jnp.sqrt(v_hat) + ADAM_EPS) + ADAM_WD * w)
    return delta, m, v


def _reduce_adamw(landed, own, w, m, v, name):
    r, c = own.shape
    tr = _tile(r, 128)

    def body(l_ref, o_ref, w_ref, m_ref, v_ref, g_out, d_out, m_out, v_out):
        me = 4 * lax.axis_index("x") + 2 * lax.axis_index("y") + lax.axis_index("c")
        g = None
        for j in range(NDEV):
            term = jnp.where(me == j, o_ref[...], l_ref[j].astype(F32))
            g = term if g is None else g + term
        g_out[...] = g
        d_out[...], m_out[...], v_out[...] = _adamw(w_ref[...], g, m_ref[...], v_ref[...])

    blk = pl.BlockSpec((tr, c), lambda i: (i, 0))
    sds = jax.ShapeDtypeStruct((r, c), F32)
    return pl.pallas_call(
        body, name=name, grid=(r // tr,),
        in_specs=[pl.BlockSpec((NDEV, tr, c), lambda i: (0, i, 0)), blk, blk, blk, blk],
        out_specs=(blk, blk, blk, blk), out_shape=(sds, sds, sds, sds),
        compiler_params=_params(40),
    )(landed, own, w, m, v)


def _sum_small(gathered, name):
    _, r, c = gathered.shape

    def body(g_ref, o_ref):
        acc = g_ref[0]
        for j in range(1, NDEV):
            acc = acc + g_ref[j]
        o_ref[...] = acc

    return pl.pallas_call(body, name=name, out_shape=jax.ShapeDtypeStruct((r, c), F32))(gathered)


def _adamw_small(g, w, m, v, name):
    def body(g_ref, w_ref, m_ref, v_ref, d_out, m_out, v_out):
        d_out[...], m_out[...], v_out[...] = _adamw(w_ref[...], g_ref[...], m_ref[...], v_ref[...])

    sds = jax.ShapeDtypeStruct(g.shape, F32)
    return pl.pallas_call(body, name=name, out_shape=(sds, sds, sds))(g, w, m, v)


def _cols_from_stack(s):
    return jnp.transpose(s, (1, 0, 2)).reshape(s.shape[1], NDEV * s.shape[2])


def _cols_to_stack(w):
    r, c = w.shape
    return jnp.transpose(w.reshape(r, NDEV, c // NDEV), (1, 0, 2))


def _rows_from_stack(s):
    return s.reshape(NDEV * s.shape[1], s.shape[2])


def _rows_to_stack(w):
    r, c = w.shape
    return w.reshape(NDEV, r // NDEV, c)


SMALL_ROWS = 16
SMALL_SLOTS = {
    "ln1_g": (0, 0, D), "ln1_b": (1, 0, D), "ln2_g": (2, 0, D), "ln2_b": (3, 0, D), "ln3_g": (4, 0, D),
    "ln3_b": (5, 0, D), "b_ple_gate": (6, 0, D), "ln4_g": (7, 0, D), "ln4_b": (8, 0, D),
    "g_attn": (9, 0, DA), "g_conv": (9, DA, DCV), "b_forget": (10, 0, NH),
}
CONVW_ROW = 11
LOSS_SLOT = (10, LANES)


def _pack_small(vals, conv_rows, loss=None):
    out = jnp.zeros((SMALL_ROWS, D), F32)
    for nm, (r, off, wd) in SMALL_SLOTS.items():
        out = out.at[r:r + 1, off:off + wd].set(vals[nm].reshape(1, wd).astype(F32))
    out = out.at[CONVW_ROW:CONVW_ROW + 3, 0:conv_rows.shape[1]].set(conv_rows.astype(F32))
    if loss is not None:
        out = out.at[LOSS_SLOT[0], LOSS_SLOT[1]].set(loss)
    return out


def _unpack_small(packed, name):
    r, off, wd = SMALL_SLOTS[name]
    return packed[r:r + 1, off:off + wd]


def kernel(x, p, ffn1_w_in, ffn1_w_out, ln1_g, ln1_b, w_mix_in, b_forget, conv_w, g_attn, g_conv, w_mix_out, ln2_g, ln2_b, ffn2_w_in, ffn2_w_out, ln3_g, ln3_b, w_ple, w_ple_gate, b_ple_gate, ln4_g, ln4_b, loss_target, m_ffn1_w_in, m_ffn1_w_out, m_ln1_g, m_ln1_b, m_w_mix_in, m_b_forget, m_conv_w, m_g_attn, m_g_conv, m_w_mix_out, m_ln2_g, m_ln2_b, m_ffn2_w_in, m_ffn2_w_out, m_ln3_g, m_ln3_b, m_w_ple, m_w_ple_gate, m_b_ple_gate, m_ln4_g, m_ln4_b, v_ffn1_w_in, v_ffn1_w_out, v_ln1_g, v_ln1_b, v_w_mix_in, v_b_forget, v_conv_w, v_g_attn, v_g_conv, v_w_mix_out, v_ln2_g, v_ln2_b, v_ffn2_w_in, v_ffn2_w_out, v_ln3_g, v_ln3_b, v_w_ple, v_w_ple_gate, v_b_ple_gate, v_ln4_g, v_ln4_b):
    args = dict(locals())
    t = x.shape[1]
    me = 4 * lax.axis_index("x") + 2 * lax.axis_index("y") + lax.axis_index("c")
    x0 = x.reshape(t, D)
    p0 = p.reshape(t, PLE)
    tgt = loss_target.reshape(t, D)

    big = ["ffn1_w_in", "ffn1_w_out", "w_mix_in", "w_mix_out", "ffn2_w_in", "ffn2_w_out", "w_ple", "w_ple_gate"]
    col_sharded = {"ffn1_w_in", "w_mix_in", "ffn2_w_in", "w_ple"}
    shard = {nm: args[nm][0] for nm in big}

    gathered = _exchange([shard[nm].astype(WIRE_DT) for nm in big] + [conv_w[0]], True, "ag_weights")
    full = {nm: (_cols_from_stack(g) if nm in col_sharded else _rows_from_stack(g)).astype(MXU_DT)
            for nm, g in zip(big, gathered[:len(big)])}
    cw = _cols_from_stack(gathered[len(big)])
    gx, gw, small, loss_part = _local_step(x0, p0, tgt, full, cw, {nm: args[nm] for nm in SMALL_SLOTS})

    stacks = [(_cols_to_stack(gw[nm]) if nm in col_sharded else _rows_to_stack(gw[nm])) for nm in big]
    landed = _exchange([s.astype(WIRE_DT) for s in stacks], False, "rs_grads")
    small_part = _pack_small({nm: small[nm] for nm in SMALL_SLOTS},
                             jnp.pad(small["conv_w"], ((0, 0), (0, D - DCV))), loss_part)
    small_all = _exchange([small_part], True, "ag_small")[0]
    small_g = _sum_small(small_all, "sum_small")
    loss = small_g[LOSS_SLOT[0], LOSS_SLOT[1]]

    outs = {"loss": loss, "grad_x": gx.reshape(1, t, D)}
    for nm, st, ld in zip(big, stacks, landed):
        own = lax.dynamic_index_in_dim(st, me, axis=0, keepdims=False)
        g, dl, mn, vn = _reduce_adamw(ld, own, shard[nm], args["m_" + nm][0], args["v_" + nm][0], "adamw_" + nm)
        outs["grad_" + nm], outs["delta_" + nm], outs["new_m_" + nm], outs["new_v_" + nm] = (
            g[None], dl[None], mn[None], vn[None])
    small_names = list(SMALL_SLOTS)
    cshard = lax.dynamic_slice_in_dim(small_g[CONVW_ROW:CONVW_ROW + 3, 0:DCV], me * (DCV // NDEV), DCV // NDEV, axis=1)
    g_pack = _pack_small({nm: _unpack_small(small_g, nm) for nm in small_names}, cshard)
    packs = [_pack_small({nm: args[pre + nm] for nm in small_names}, args[pre + "conv_w"][0])
             for pre in ("", "m_", "v_")]
    d_pack, m_pack, v_pack = _adamw_small(g_pack, packs[0], packs[1], packs[2], "adamw_small")
    for key, pk in (("grad_", g_pack), ("delta_", d_pack), ("new_m_", m_pack), ("new_v_", v_pack)):
        for nm in small_names:
            outs[key + nm] = _unpack_small(pk, nm)
        outs[key + "conv_w"] = pk[CONVW_ROW:CONVW_ROW + 3, 0:DCV // NDEV][None]

    wnames = ["ffn1_w_in", "ffn1_w_out", "ln1_g", "ln1_b", "w_mix_in", "b_forget", "conv_w", "g_attn", "g_conv",
              "w_mix_out", "ln2_g", "ln2_b", "ffn2_w_in", "ffn2_w_out", "ln3_g", "ln3_b", "w_ple", "w_ple_gate",
              "b_ple_gate", "ln4_g", "ln4_b"]
    return (outs["loss"], outs["grad_x"], *[outs[pre + nm] for pre in ("grad_", "delta_", "new_m_", "new_v_")
                                            for nm in wnames])


def _local_step(x0, p0, tgt, full, cw, sp):
    t = x0.shape[0]
    ln1_g, ln1_b, ln2_g, ln2_b, ln3_g, ln3_b = (sp[k] for k in ("ln1_g", "ln1_b", "ln2_g", "ln2_b", "ln3_g", "ln3_b"))
    ln4_g, ln4_b, g_attn, g_conv, b_ple_gate = (sp[k] for k in ("ln4_g", "ln4_b", "g_attn", "g_conv", "b_ple_gate"))
    wmi = full["w_mix_in"]
    w_qkv = wmi[:, 0:3 * DA]
    w_f = jnp.pad(wmi[:, 3 * DA:3 * DA + NH], ((0, 0), (0, LANES - NH)))
    w_bch = wmi[:, 3 * DA + NH:]
    bf_pad = jnp.pad(sp["b_forget"], ((0, 0), (0, LANES - NH)))
    ones = jnp.ones((1, D), F32)
    zeros = jnp.zeros((1, D), F32)

    g1a, u1a, xh1, rs1 = _ffn_fwd(x0, ones, zeros, full["ffn1_w_in"], full["ffn1_w_out"], "ffn1_fwd")
    q, k, v, bch, z, c = _mix_proj_fwd(xh1, ln1_g, ln1_b, w_qkv, w_bch, w_f, bf_pad, "mix_proj_fwd")
    c8 = jnp.transpose(c[:, 0:NH])
    ccol, crow = c8.reshape(NH, t, 1), c8.reshape(NH, 1, t)
    qa, ka, vta, rtile = _attn_operands(q, k, v, c8, _tile(t, 512))
    o, lse = _attn_fwd(qa, ka, vta, rtile, "attn_fwd")
    merged, xh2, rs2 = _mix_post_fwd(o, bch, cw, g_attn, g_conv, xh1, ln1_g, ln1_b, full["w_mix_out"],
                                     "mix_post_fwd")
    g2a, u2a, xh3, rs3 = _ffn_fwd(xh2, ln2_g, ln2_b, full["ffn2_w_in"], full["ffn2_w_out"], "ffn2_fwd")

    dr3, dz, de, st_tail = _tail(xh3, rs3, ln3_g, ln3_b, p0, full["w_ple_gate"], full["w_ple"], b_ple_gate,
                                 ln4_g, ln4_b, tgt, "tail")
    df2, dg2, du2, dr2, st_f2 = _ffn_bwd(dr3, g2a, u2a, xh2, rs2, ln2_g, full["ffn2_w_in"], full["ffn2_w_out"],
                                         True, "ffn2_bwd")
    dmix, do, delta, dyc, st_post = _mix_post_bwd(dr2, o, bch, cw, g_attn, g_conv, full["w_mix_out"],
                                                  "mix_post_bwd")
    dbch, st_conv = _conv_bwd(dyc, bch, cw, "conv_bwd")
    drow = jnp.transpose(delta[:, 0:NH]).reshape(NH, 1, t)
    dk, dv, dq, dc, dcq = _attn_bwd(q, k, v, do, lse, drow, crow, ccol, "attn_bwd")
    dc_pad = jnp.pad(jnp.transpose(dc.reshape(NH, t) + dcq.reshape(NH, t)), ((0, 0), (0, LANES - NH)))
    dr1, dfl, st_proj = _mix_proj_bwd(dr2, dq, dk, dv, dbch, dc_pad, z, xh1, rs1, ln1_g, w_qkv, w_bch, w_f,
                                      "mix_proj_bwd")
    df1, dg1, du1, gx, _ = _ffn_bwd(dr1, g1a, u1a, x0, rs1, ones, full["ffn1_w_in"], full["ffn1_w_out"],
                                    False, "ffn1_bwd")

    x1p, x2p, x3p = (xh1, ln1_g, ln1_b), (xh2, ln2_g, ln2_b), (xh3, ln3_g, ln3_b)
    gw = {}
    gw["ffn1_w_in"] = jnp.concatenate(
        [_dw("affine", (x0, ones, zeros), dg1, D, F, "dw_ffn1_in_g", tn=F // 2),
         _dw("affine", (x0, ones, zeros), du1, D, F, "dw_ffn1_in_u", tn=F // 2)], axis=1)
    gw["ffn1_w_out"] = _dw("swiglu", (g1a, u1a), df1, F, D, "dw_ffn1_out", tmm=F // 2)
    gw["ffn2_w_in"] = jnp.concatenate(
        [_dw("affine", x2p, dg2, D, F, "dw_ffn2_in_g", tn=F // 2),
         _dw("affine", x2p, du2, D, F, "dw_ffn2_in_u", tn=F // 2)], axis=1)
    gw["ffn2_w_out"] = _dw("swiglu", (g2a, u2a), df2, F, D, "dw_ffn2_out", tmm=F // 2)
    gw["w_mix_out"] = _dw("plain", (merged,), dmix, D, D, "dw_mix_out")
    gw["w_mix_in"] = jnp.concatenate(
        [_dw("affine", x1p, dq, D, DA, "dw_mix_in_q"), _dw("affine", x1p, dk, D, DA, "dw_mix_in_k"),
         _dw("affine", x1p, dv, D, DA, "dw_mix_in_v"),
         _dw("affine", x1p, dfl, D, LANES, "dw_mix_in_f")[:, 0:NH],
         _dw("affine", x1p, dbch, D, 3 * DCV, "dw_mix_in_bch")], axis=1)
    gw["w_ple_gate"] = _dw("affine", x3p, dz, D, D, "dw_ple_gate")
    gw["w_ple"] = _dw("plain", (p0,), de, PLE, D, "dw_ple")

    loss_part = (0.5 / D) * jnp.sum(st_tail[5:6, :])
    small = {"ln1_g": st_proj[0:1], "ln1_b": st_proj[1:2], "ln2_g": st_f2[0:1], "ln2_b": st_f2[1:2],
             "ln3_g": st_tail[3:4], "ln3_b": st_tail[4:5], "b_ple_gate": st_tail[2:3], "ln4_g": st_tail[0:1],
             "ln4_b": st_tail[1:2], "g_attn": st_post[0:1], "g_conv": st_post[1:2],
             "b_forget": st_proj[2:3, 0:NH], "conv_w": st_conv[0:3]}
    return gx, gw, small, loss_part
```

```python
import functools
import math

import jax
import jax.numpy as jnp
from jax import lax
from jax.experimental import pallas as pl
from jax.experimental.pallas import tpu as pltpu

D = 1024
F = 2816
NH = 8
DH = 64
DA = NH * DH
DCV = D - DA
PLE = 256
NPROJ = 3 * DA + NH + 3 * DCV
LN_EPS = 1e-5
RMS_EPS = 1e-6
NEG = -1e30
ALPHA = 2.0 ** 0.25
NDEV = 8
LANES = 128

ADAM_LR, ADAM_B1, ADAM_B2, ADAM_EPS, ADAM_WD, ADAM_STEP = 0.001, 0.9, 0.999, 1e-08, 0.01, 10

F32 = jnp.float32
MXU_DT = jnp.bfloat16
WIRE_DT = jnp.bfloat16

MESH_ID = pl.DeviceIdType.MESH
ANY = pl.BlockSpec(memory_space=pl.ANY)


def _params(vmem_mb, n_axes=1):
    return pltpu.CompilerParams(dimension_semantics=("arbitrary",) * n_axes,
                                vmem_limit_bytes=int(vmem_mb) << 20)


def _mm(a, b):
    return jnp.dot(a, b, preferred_element_type=F32)


def _mm_nt(a, b):
    return lax.dot_general(a, b, (((1,), (1,)), ((), ())), preferred_element_type=F32)


def _mm_tn(a, b):
    return lax.dot_general(a, b, (((0,), (0,)), ((), ())), preferred_element_type=F32)


def _split3(x):
    hi = x.astype(MXU_DT)
    r1 = x - hi.astype(F32)
    mid = r1.astype(MXU_DT)
    lo = (r1 - mid.astype(F32)).astype(MXU_DT)
    return hi, mid, lo


def _mm_sel(sel, x):
    hi, mid, lo = _split3(x)
    return _mm(sel, hi) + _mm(sel, mid) + _mm(sel, lo)


def _mm_xsel(x, sel):
    hi, mid, lo = _split3(x)
    return _mm(hi, sel) + _mm(mid, sel) + _mm(lo, sel)


def _sigmoid(x):
    return 1.0 / (1.0 + jnp.exp(-x))


def _ln_fwd(r):
    mu = jnp.mean(r, axis=-1, keepdims=True)
    xc = r - mu
    var = jnp.mean(xc * xc, axis=-1, keepdims=True)
    rstd = lax.rsqrt(var + LN_EPS)
    return xc * rstd, rstd


def _ln_bwd(dxhat, xhat, rstd):
    m1 = jnp.mean(dxhat, axis=-1, keepdims=True)
    m2 = jnp.mean(dxhat * xhat, axis=-1, keepdims=True)
    return rstd * (dxhat - m1 - xhat * m2)


def _rms_fwd(x):
    r = lax.rsqrt(jnp.mean(x * x, axis=-1, keepdims=True) + RMS_EPS)
    return x * r, r


def _rms_bwd(dyg, xn, r):
    return r * (dyg - xn * jnp.mean(dyg * xn, axis=-1, keepdims=True))


def _colsum(x):
    return jnp.sum(x, axis=0, keepdims=True)


def _f_chunks():
    out, c0 = [], 0
    while c0 < F:
        fc = min(512, F - c0)
        out.append((c0, fc))
        c0 += fc
    return out


def _tile(t, want):
    return want if t % want == 0 and t >= want else t


def _exchange(arrs, gather, name):
    n = len(arrs)
    out_shape = []
    for a in arrs:
        shp = ((NDEV,) + a.shape) if gather else a.shape
        out_shape.append(jax.ShapeDtypeStruct(shp, a.dtype))

    def body(*refs):
        ins, outs = refs[:n], refs[n:2 * n]
        send_sems, recv_sems, loc_sems = refs[2 * n:]
        x, y, c = lax.axis_index("x"), lax.axis_index("y"), lax.axis_index("c")
        me = 4 * x + 2 * y + c
        peers = []
        for k in range(1, NDEV):
            px = 1 - x if (k >> 2) & 1 else x
            py = 1 - y if (k >> 1) & 1 else y
            pc = 1 - c if k & 1 else c
            peers.append(((px, py, pc), 4 * px + 2 * py + pc))

        def remote(w, k, slot_src, slot_dst):
            src = ins[w] if gather else ins[w].at[slot_src]
            return pltpu.make_async_remote_copy(
                src_ref=src, dst_ref=outs[w].at[slot_dst],
                send_sem=send_sems.at[w * (NDEV - 1) + k], recv_sem=recv_sems.at[w * (NDEV - 1) + k],
                device_id=peers[k][0], device_id_type=MESH_ID)

        local = []
        for w in range(n):
            lc = pltpu.make_async_copy(ins[w] if gather else ins[w].at[me], outs[w].at[me], loc_sems.at[w])
            lc.start()
            local.append(lc)
        for k in range(NDEV - 1):
            for w in range(n):
                remote(w, k, peers[k][1], me).start()
        for k in range(NDEV - 1):
            for w in range(n):
                remote(w, k, me, peers[k][1]).wait_recv()
        for k in range(NDEV - 1):
            for w in range(n):
                remote(w, k, peers[k][1], me).wait_send()
        for lc in local:
            lc.wait()

    return pl.pallas_call(
        body, name=name, out_shape=tuple(out_shape),
        in_specs=[ANY] * n, out_specs=tuple([ANY] * n),
        scratch_shapes=[pltpu.SemaphoreType.DMA((n * (NDEV - 1),)), pltpu.SemaphoreType.DMA((n * (NDEV - 1),)),
                        pltpu.SemaphoreType.DMA((n,))],
    )(*arrs)


def _ffn_fwd(xin, gin, bin_, w_in, w_out, name):
    t = xin.shape[0]
    tm = _tile(t, 512)
    chunks = _f_chunks()

    def body(x_ref, gi_ref, bi_ref, win_hbm, wout_hbm, g_ref, u_ref, xh_ref, rs_ref, win_v, wout_v, acc_ref):
        @pl.when(pl.program_id(0) == 0)
        def _():
            pltpu.sync_copy(win_hbm, win_v)
            pltpu.sync_copy(wout_hbm, wout_v)

        x = x_ref[...] * gi_ref[...] + bi_ref[...]
        xb = x.astype(MXU_DT)
        for ci, (c0, fc) in enumerate(chunks):
            gc = _mm(xb, win_v[:, c0:c0 + fc])
            uc = _mm(xb, win_v[:, F + c0:F + c0 + fc])
            g_ref[:, c0:c0 + fc] = gc.astype(g_ref.dtype)
            u_ref[:, c0:c0 + fc] = uc.astype(u_ref.dtype)
            hc = (gc * _sigmoid(gc) * uc).astype(MXU_DT)
            part = _mm(hc, wout_v[c0:c0 + fc, :])
            if ci == 0:
                acc_ref[...] = part
            else:
                acc_ref[...] += part
        xh, rstd = _ln_fwd(ALPHA * x + 0.5 * acc_ref[...])
        xh_ref[...] = xh
        rs_ref[...] = rstd

    row = pl.BlockSpec((tm, D), lambda i: (i, 0))
    vec = pl.BlockSpec((1, D), lambda i: (0, 0))
    act = pl.BlockSpec((tm, F), lambda i: (i, 0))
    return pl.pallas_call(
        body, name=name, grid=(t // tm,),
        in_specs=[row, vec, vec, ANY, ANY],
        out_specs=(act, act, row, pl.BlockSpec((tm, 1), lambda i: (i, 0))),
        out_shape=(jax.ShapeDtypeStruct((t, F), MXU_DT), jax.ShapeDtypeStruct((t, F), MXU_DT),
                   jax.ShapeDtypeStruct((t, D), F32), jax.ShapeDtypeStruct((t, 1), F32)),
        scratch_shapes=[pltpu.VMEM((D, 2 * F), MXU_DT), pltpu.VMEM((F, D), MXU_DT), pltpu.VMEM((tm, D), F32)],
        compiler_params=_params(52),
    )(xin, gin, bin_, w_in, w_out)


def _mix_proj_fwd(xh1, g1, b1, w_qkv, w_bch, w_f, bf_pad, name):
    t = xh1.shape[0]
    tm = _tile(t, 512)

    def body(x_ref, g_ref, b_ref, wq_ref, wb_ref, wf_ref, bf_ref, q_ref, k_ref, v_ref, bch_ref, z_ref, c_ref, carry):
        @pl.when(pl.program_id(0) == 0)
        def _():
            carry[...] = jnp.zeros_like(carry)

        xb = (x_ref[...] * g_ref[...] + b_ref[...]).astype(MXU_DT)
        qkv = _mm(xb, wq_ref[...])
        q_ref[...] = qkv[:, 0:DA].astype(q_ref.dtype)
        k_ref[...] = qkv[:, DA:2 * DA].astype(k_ref.dtype)
        v_ref[...] = qkv[:, 2 * DA:3 * DA].astype(v_ref.dtype)
        bch_ref[...] = _mm(xb, wb_ref[...])
        z = _mm(xb, wf_ref[...]) + bf_ref[...]
        z_ref[...] = z
        logf = jnp.minimum(z, 0.0) - jnp.log(1.0 + jnp.exp(-jnp.abs(z)))
        row = lax.broadcasted_iota(jnp.int32, (tm, tm), 0)
        col = lax.broadcasted_iota(jnp.int32, (tm, tm), 1)
        tri = jnp.where(row >= col, 1.0, 0.0).astype(MXU_DT)
        c = carry[...] + _mm_sel(tri, logf)
        c_ref[...] = c
        carry[...] = c[tm - 1:tm, :]

    row = lambda w: pl.BlockSpec((tm, w), lambda i: (i, 0))
    full = lambda a: pl.BlockSpec(a.shape, lambda i: (0, 0))
    return pl.pallas_call(
        body, name=name, grid=(t // tm,),
        in_specs=[row(D), full(g1), full(b1), full(w_qkv), full(w_bch), full(w_f), full(bf_pad)],
        out_specs=(row(DA), row(DA), row(DA), row(3 * DCV), row(LANES), row(LANES)),
        out_shape=(jax.ShapeDtypeStruct((t, DA), MXU_DT),) * 3
        + (jax.ShapeDtypeStruct((t, 3 * DCV), F32), jax.ShapeDtypeStruct((t, LANES), F32),
           jax.ShapeDtypeStruct((t, LANES), F32)),
        scratch_shapes=[pltpu.VMEM((1, LANES), F32)],
        compiler_params=_params(48),
    )(xh1, g1, b1, w_qkv, w_bch, w_f, bf_pad)


def _attn_operands(q, k, v, c8, tq):
    t = q.shape[0]
    nq = t // tq
    c3 = c8.reshape(NH, nq, tq)
    r = c3[:, :, 0]
    crel = (c3 - r[:, :, None]).reshape(NH, t)
    rnd = (lambda x: lax.reduce_precision(x, 8, 7)) if MXU_DT == jnp.bfloat16 else (lambda x: x)
    hi = rnd(crel)
    mid = rnd(crel - hi)
    lo = rnd(crel - hi - mid)
    cs = jnp.stack([hi, mid, lo], axis=-1).astype(MXU_DT)
    one3 = jnp.ones((NH, t, 3), MXU_DT)
    pad = jnp.zeros((NH, t, LANES - DH - 6), MXU_DT)
    heads = lambda x: jnp.transpose(x.reshape(t, NH, DH), (1, 0, 2))
    qa = jnp.concatenate([heads(q) * jnp.asarray(0.125, q.dtype), cs, one3, pad], axis=-1)
    ka = jnp.concatenate([heads(k), one3, -cs, pad], axis=-1)
    vt = jnp.transpose(v.reshape(t, NH, DH), (1, 2, 0))
    one_row = jnp.ones((NH, 1, t), MXU_DT)
    z_rows = jnp.zeros((NH, LANES - DH - 1, t), MXU_DT)
    odd = (jnp.arange(NH) % 2 == 1)[:, None, None]
    vta = jnp.where(odd, jnp.concatenate([one_row, z_rows, vt], axis=1),
                    jnp.concatenate([vt, one_row, z_rows], axis=1))
    return qa, ka, vta, r


def _attn_fwd(qa, ka, vta, r, name):
    t = qa.shape[1]
    tq = _tile(t, 512)
    nq = t // tq

    def body(r_ref, q_ref, k_ref, v_ref, o_ref, l_ref):
        hp, i = pl.program_id(0), pl.program_id(1)
        key = lax.broadcasted_iota(jnp.int32, (tq, tq), 0)
        qry = lax.broadcasted_iota(jnp.int32, (tq, tq), 1)

        def step(j, carry, masked):
            off = pl.multiple_of(j * tq, tq)
            out = []
            for a in range(2):
                m, acc = carry[a]
                st = _mm_nt(k_ref[a, pl.ds(off, tq), :], q_ref[a])
                if masked:
                    st = jnp.where(qry >= key, st, NEG)
                d = r_ref[2 * hp + a, i] - r_ref[2 * hp + a, j]
                m_new = jnp.maximum(m, jnp.max(st, axis=0, keepdims=True) + d)
                pt = jnp.exp(st - (m_new - d))
                acc = jnp.exp(m - m_new) * acc + _mm(v_ref[a, :, pl.ds(off, tq)], pt.astype(MXU_DT))
                out.append((m_new, acc))
            return tuple(out)

        init = tuple((jnp.full((1, tq), NEG, F32), jnp.zeros((LANES, tq), F32)) for _ in range(2))
        carry = lax.fori_loop(0, i, lambda j, cr: step(j, cr, False), init)
        (ma, acca), (mb, accb) = step(i, carry, True)
        la, lb = acca[DH:DH + 1, :], accb[0:1, :]
        l_ref[0] = ma + jnp.log(la)
        l_ref[1] = mb + jnp.log(lb)
        sub = lax.broadcasted_iota(jnp.int32, (LANES, tq), 0)
        o_ref[...] = jnp.where(sub < DH, acca / la, accb / lb).T

    return pl.pallas_call(
        body, name=name, grid=(NH // 2, nq),
        in_specs=[pl.BlockSpec(memory_space=pltpu.SMEM),
                  pl.BlockSpec((2, tq, LANES), lambda p, i: (p, i, 0)),
                  pl.BlockSpec((2, t, LANES), lambda p, i: (p, 0, 0)),
                  pl.BlockSpec((2, LANES, t), lambda p, i: (p, 0, 0))],
        out_specs=(pl.BlockSpec((tq, LANES), lambda p, i: (i, p)),
                   pl.BlockSpec((2, 1, tq), lambda p, i: (p, 0, i))),
        out_shape=(jax.ShapeDtypeStruct((t, DA), F32), jax.ShapeDtypeStruct((NH, 1, t), F32)),
        compiler_params=_params(48, 2),
    )(r, qa, ka, vta)


def _conv_parts(bch):
    return bch[:, 0:DCV], bch[:, DCV:2 * DCV], bch[:, 2 * DCV:3 * DCV]


def _mix_post_fwd(o, bch, conv_w, g_attn, g_conv, xh1, g1, b1, w_mo, name):
    t = o.shape[0]
    tm = _tile(t, 512)
    hb = tm // 8

    def body(o_ref, bch_ref, halo_ref, cw_ref, ga_ref, gc_ref, x_ref, g_ref, b_ref, w_ref,
             mg_ref, xh_ref, rs_ref, ext):
        i = pl.program_id(0)
        an, _ = _rms_fwd(o_ref[...])
        mg_ref[:, 0:DA] = (an * ga_ref[...]).astype(mg_ref.dtype)
        bb, cc, hh = _conv_parts(bch_ref[...])
        _, hc, hh_h = _conv_parts(halo_ref[...])
        u = cc * hh
        ext[0:8, :] = jnp.where(i > 0, hc * hh_h, 0.0)
        ext[8:8 + tm, :] = u
        raw = cw_ref[0:1, :] * ext[6:6 + tm, :] + cw_ref[1:2, :] * ext[7:7 + tm, :] + cw_ref[2:3, :] * u
        cn, _ = _rms_fwd(bb * raw)
        mg_ref[:, DA:D] = (cn * gc_ref[...]).astype(mg_ref.dtype)
        x1 = x_ref[...] * g_ref[...] + b_ref[...]
        xh, rstd = _ln_fwd(ALPHA * x1 + _mm(mg_ref[...], w_ref[...]))
        xh_ref[...] = xh
        rs_ref[...] = rstd

    row = lambda w: pl.BlockSpec((tm, w), lambda i: (i, 0))
    full = lambda a: pl.BlockSpec(a.shape, lambda i: (0, 0))
    return pl.pallas_call(
        body, name=name, grid=(t // tm,),
        in_specs=[row(DA), row(3 * DCV),
                  pl.BlockSpec((8, 3 * DCV), lambda i: (jnp.maximum(i * hb - 1, 0), 0)),
                  full(conv_w), full(g_attn), full(g_conv), row(D), full(g1), full(b1), full(w_mo)],
        out_specs=(row(D), row(D), pl.BlockSpec((tm, 1), lambda i: (i, 0))),
        out_shape=(jax.ShapeDtypeStruct((t, D), MXU_DT), jax.ShapeDtypeStruct((t, D), F32),
                   jax.ShapeDtypeStruct((t, 1), F32)),
        scratch_shapes=[pltpu.VMEM((tm + 8, DCV), F32)],
        compiler_params=_params(48),
    )(o, bch, bch, conv_w, g_attn, g_conv, xh1, g1, b1, w_mo)


def _tail(xh3, rs3, g3, b3, p, w_g, w_ple, bg, g4, b4, target, name):
    t = xh3.shape[0]
    tm = _tile(t, 512)

    def body(x_ref, rs_ref, g3_ref, b3_ref, p_ref, wg_ref, wp_ref, bg_ref, g4_ref, b4_ref, t_ref,
             dr_ref, dz_ref, de_ref, st_ref):
        @pl.when(pl.program_id(0) == 0)
        def _():
            st_ref[...] = jnp.zeros_like(st_ref)

        xh3v = x_ref[...]
        x3 = xh3v * g3_ref[...] + b3_ref[...]
        gate = _sigmoid(_mm(x3.astype(MXU_DT), wg_ref[...]) + bg_ref[...])
        e = _mm(p_ref[...].astype(MXU_DT), wp_ref[...])
        xh4, rstd4 = _ln_fwd(ALPHA * x3 + gate * e)
        diff = xh4 * g4_ref[...] + b4_ref[...] - t_ref[...]
        dy = diff * (1.0 / D)
        st_ref[5:6, :] += _colsum(diff * diff)
        st_ref[0:1, :] += _colsum(dy * xh4)
        st_ref[1:2, :] += _colsum(dy)
        dr4 = _ln_bwd(dy * g4_ref[...], xh4, rstd4)
        de_ref[...] = (dr4 * gate).astype(de_ref.dtype)
        dz = dr4 * e * gate * (1.0 - gate)
        st_ref[2:3, :] += _colsum(dz)
        dzb = dz.astype(MXU_DT)
        dz_ref[...] = dzb
        dx3 = ALPHA * dr4 + _mm_nt(dzb, wg_ref[...])
        st_ref[3:4, :] += _colsum(dx3 * xh3v)
        st_ref[4:5, :] += _colsum(dx3)
        dr_ref[...] = _ln_bwd(dx3 * g3_ref[...], xh3v, rs_ref[...])

    row = lambda w: pl.BlockSpec((tm, w), lambda i: (i, 0))
    full = lambda a: pl.BlockSpec(a.shape, lambda i: (0, 0))
    return pl.pallas_call(
        body, name=name, grid=(t // tm,),
        in_specs=[row(D), row(1), full(g3), full(b3), row(PLE), full(w_g), full(w_ple), full(bg), full(g4),
                  full(b4), row(D)],
        out_specs=(row(D), row(D), row(D), pl.BlockSpec((8, D), lambda i: (0, 0))),
        out_shape=(jax.ShapeDtypeStruct((t, D), F32), jax.ShapeDtypeStruct((t, D), MXU_DT),
                   jax.ShapeDtypeStruct((t, D), MXU_DT), jax.ShapeDtypeStruct((8, D), F32)),
        compiler_params=_params(48),
    )(xh3, rs3, g3, b3, p, w_g, w_ple, bg, g4, b4, target)


def _ffn_bwd(dr, gact, uact, xin, rsin, gin, w_in, w_out, prev_ln, name):
    t = dr.shape[0]
    tm = _tile(t, 256)
    chunks = _f_chunks()

    def body(dr_ref, g_ref, u_ref, x_ref, rs_ref, gi_ref, win_hbm, wout_hbm,
             df_ref, dg_ref, du_ref, dx_ref, st_ref, win_v, wout_v, acc_ref):
        @pl.when(pl.program_id(0) == 0)
        def _():
            pltpu.sync_copy(win_hbm, win_v)
            pltpu.sync_copy(wout_hbm, wout_v)
            st_ref[...] = jnp.zeros_like(st_ref)

        drv = dr_ref[...]
        dfb = (0.5 * drv).astype(MXU_DT)
        df_ref[...] = dfb
        for ci, (c0, fc) in enumerate(chunks):
            dh = _mm_nt(dfb, wout_v[c0:c0 + fc, :])
            g = g_ref[:, c0:c0 + fc].astype(F32)
            u = u_ref[:, c0:c0 + fc].astype(F32)
            sg = _sigmoid(g)
            dgb = (dh * u * (sg * (1.0 + g * (1.0 - sg)))).astype(MXU_DT)
            dub = (dh * (g * sg)).astype(MXU_DT)
            dg_ref[:, c0:c0 + fc] = dgb
            du_ref[:, c0:c0 + fc] = dub
            part = _mm_nt(dgb, win_v[:, c0:c0 + fc]) + _mm_nt(dub, win_v[:, F + c0:F + c0 + fc])
            if ci == 0:
                acc_ref[...] = part
            else:
                acc_ref[...] += part
        dx = ALPHA * drv + acc_ref[...]
        if prev_ln:
            xh = x_ref[...]
            st_ref[0:1, :] += _colsum(dx * xh)
            st_ref[1:2, :] += _colsum(dx)
            dx_ref[...] = _ln_bwd(dx * gi_ref[...], xh, rs_ref[...])
        else:
            dx_ref[...] = dx

    row = pl.BlockSpec((tm, D), lambda i: (i, 0))
    vec = pl.BlockSpec((1, D), lambda i: (0, 0))
    act = pl.BlockSpec((tm, F), lambda i: (i, 0))
    return pl.pallas_call(
        body, name=name, grid=(t // tm,),
        in_specs=[row, act, act, row, pl.BlockSpec((tm, 1), lambda i: (i, 0)), vec, ANY, ANY],
        out_specs=(row, act, act, row, pl.BlockSpec((8, D), lambda i: (0, 0))),
        out_shape=(jax.ShapeDtypeStruct((t, D), MXU_DT), jax.ShapeDtypeStruct((t, F), MXU_DT),
                   jax.ShapeDtypeStruct((t, F), MXU_DT), jax.ShapeDtypeStruct((t, D), F32),
                   jax.ShapeDtypeStruct((8, D), F32)),
        scratch_shapes=[pltpu.VMEM((D, 2 * F), MXU_DT), pltpu.VMEM((F, D), MXU_DT), pltpu.VMEM((tm, D), F32)],
        compiler_params=_params(52),
    )(dr, gact, uact, xin, rsin, gin, w_in, w_out)


def _mix_post_bwd(dr2, o, bch, conv_w, g_attn, g_conv, w_mo, name):
    t = dr2.shape[0]
    tm = _tile(t, 512)
    hb = tm // 8

    def body(dr_ref, o_ref, bch_ref, halo_ref, cw_ref, ga_ref, gc_ref, w_ref,
             dm_ref, do_ref, dl_ref, dy_ref, st_ref, ext):
        i = pl.program_id(0)

        @pl.when(i == 0)
        def _():
            st_ref[...] = jnp.zeros_like(st_ref)

        dmb = dr_ref[...].astype(MXU_DT)
        dm_ref[...] = dmb
        dmg = _mm_nt(dmb, w_ref[...])
        ov = o_ref[...]
        an, ra = _rms_fwd(ov)
        da = dmg[:, 0:DA]
        st_ref[0:1, :] += _colsum(da * an)
        dxa = _rms_bwd(da * ga_ref[...], an, ra)
        dob = dxa.astype(do_ref.dtype)
        do_ref[...] = dob
        crow = lax.broadcasted_iota(jnp.int32, (DA, LANES), 0)
        ccol = lax.broadcasted_iota(jnp.int32, (DA, LANES), 1)
        sel = jnp.where((crow // DH) == ccol, 1.0, 0.0).astype(MXU_DT)
        dl_ref[...] = _mm_xsel(dob.astype(F32) * ov, sel)
        bb, cc, hh = _conv_parts(bch_ref[...])
        _, hc, hh_h = _conv_parts(halo_ref[...])
        u = cc * hh
        ext[0:8, :] = jnp.where(i > 0, hc * hh_h, 0.0)
        ext[8:8 + tm, :] = u
        raw = cw_ref[0:1, :] * ext[6:6 + tm, :] + cw_ref[1:2, :] * ext[7:7 + tm, :] + cw_ref[2:3, :] * u
        cn, rc = _rms_fwd(bb * raw)
        dcn = dmg[:, DA:D]
        st_ref[1:2, :] += _colsum(dcn * cn)
        dy_ref[...] = _rms_bwd(dcn * gc_ref[...], cn, rc)

    row = lambda w: pl.BlockSpec((tm, w), lambda i: (i, 0))
    full = lambda a: pl.BlockSpec(a.shape, lambda i: (0, 0))
    return pl.pallas_call(
        body, name=name, grid=(t // tm,),
        in_specs=[row(D), row(DA), row(3 * DCV),
                  pl.BlockSpec((8, 3 * DCV), lambda i: (jnp.maximum(i * hb - 1, 0), 0)),
                  full(conv_w), full(g_attn), full(g_conv), full(w_mo)],
        out_specs=(row(D), row(DA), row(LANES), row(DCV), pl.BlockSpec((8, DA), lambda i: (0, 0))),
        out_shape=(jax.ShapeDtypeStruct((t, D), MXU_DT), jax.ShapeDtypeStruct((t, DA), MXU_DT),
                   jax.ShapeDtypeStruct((t, LANES), F32), jax.ShapeDtypeStruct((t, DCV), F32),
                   jax.ShapeDtypeStruct((8, DA), F32)),
        scratch_shapes=[pltpu.VMEM((tm + 8, DCV), F32)],
        compiler_params=_params(48),
    )(dr2, o, bch, bch, conv_w, g_attn, g_conv, w_mo)


def _conv_bwd(dy, bch, conv_w, name):
    t = dy.shape[0]
    tm = _tile(t, 512)
    hb = tm // 8
    nt = t // tm

    def body(dy_ref, dyn_ref, bch_ref, prev_ref, next_ref, cw_ref, out_ref, st_ref, ext_u, ext_d):
        i = pl.program_id(0)

        @pl.when(i == 0)
        def _():
            st_ref[...] = jnp.zeros_like(st_ref)

        bb, cc, hh = _conv_parts(bch_ref[...])
        _, pc, ph = _conv_parts(prev_ref[...])
        nb, _, _ = _conv_parts(next_ref[...])
        u = cc * hh
        ext_u[0:8, :] = jnp.where(i > 0, pc * ph, 0.0)
        ext_u[8:8 + tm, :] = u
        u1 = ext_u[7:7 + tm, :]
        u2 = ext_u[6:6 + tm, :]
        w0, w1, w2 = cw_ref[0:1, :], cw_ref[1:2, :], cw_ref[2:3, :]
        dyv = dy_ref[...]
        out_ref[:, 0:DCV] = (dyv * (w0 * u2 + w1 * u1 + w2 * u)).astype(out_ref.dtype)
        dcr = dyv * bb
        ext_d[0:tm, :] = dcr
        ext_d[tm:tm + 8, :] = jnp.where(i < nt - 1, dyn_ref[...] * nb, 0.0)
        du = w2 * dcr + w1 * ext_d[1:1 + tm, :] + w0 * ext_d[2:2 + tm, :]
        out_ref[:, DCV:2 * DCV] = (du * hh).astype(out_ref.dtype)
        out_ref[:, 2 * DCV:3 * DCV] = (du * cc).astype(out_ref.dtype)
        st_ref[0:1, :] += _colsum(dcr * u2)
        st_ref[1:2, :] += _colsum(dcr * u1)
        st_ref[2:3, :] += _colsum(dcr * u)

    row = lambda w: pl.BlockSpec((tm, w), lambda i: (i, 0))
    prev = lambda w: pl.BlockSpec((8, w), lambda i: (jnp.maximum(i * hb - 1, 0), 0))
    nxt = lambda w: pl.BlockSpec((8, w), lambda i: (jnp.minimum((i + 1) * hb, nt * hb - 1), 0))
    return pl.pallas_call(
        body, name=name, grid=(nt,),
        in_specs=[row(DCV), nxt(DCV), row(3 * DCV), prev(3 * DCV), nxt(3 * DCV),
                  pl.BlockSpec(conv_w.shape, lambda i: (0, 0))],
        out_specs=(row(3 * DCV), pl.BlockSpec((8, DCV), lambda i: (0, 0))),
        out_shape=(jax.ShapeDtypeStruct((t, 3 * DCV), MXU_DT), jax.ShapeDtypeStruct((8, DCV), F32)),
        scratch_shapes=[pltpu.VMEM((tm + 8, DCV), F32), pltpu.VMEM((tm + 8, DCV), F32)],
        compiler_params=_params(48),
    )(dy, dy, bch, bch, bch, conv_w)


QROWS = 80


def _attn_bwd_operands(qa, ka, v, do):
    t = v.shape[0]
    heads = lambda x: jnp.transpose(x.reshape(t, NH, DH), (1, 0, 2))
    qat = jnp.transpose(qa, (0, 2, 1))
    kat = jnp.transpose(ka[:, :, 0:QROWS], (0, 2, 1))
    va = jnp.pad(heads(v), ((0, 0), (0, 0), (0, LANES - DH)))
    dot = jnp.pad(jnp.transpose(heads(do), (0, 2, 1)), ((0, 0), (0, LANES - DH), (0, 0)))
    return qat, kat, va, dot


def _attn_bwd(ka, kat, va, qat, dot, lrow, drow, r, name):
    t = ka.shape[1]
    tq = _tile(t, 512)
    nq = t // tq

    def body(r_ref, ka_ref, kat_ref, va_ref, l_ref, dl_ref, qat_hbm, dot_hbm,
             dk_ref, dv_ref, dck_ref, dqt_hbm, qat_v, dot_v, dq_acc):
        hp, j = pl.program_id(0), pl.program_id(1)

        @pl.when(j == 0)
        def _():
            pltpu.sync_copy(qat_hbm.at[pl.ds(2 * hp, 2)], qat_v)
            pltpu.sync_copy(dot_hbm.at[pl.ds(2 * hp, 2)], dot_v)
            dq_acc[...] = jnp.zeros_like(dq_acc)

        key = lax.broadcasted_iota(jnp.int32, (tq, tq), 0)
        qry = lax.broadcasted_iota(jnp.int32, (tq, tq), 1)

        def step(i, carry, masked):
            off = pl.multiple_of(i * tq, tq)
            out = []
            for a in range(2):
                dk, dv = carry[a]
                st = _mm(ka_ref[a], qat_v[a, :, pl.ds(off, tq)])
                dpt = _mm(va_ref[a], dot_v[a, :, pl.ds(off, tq)])
                if masked:
                    st = jnp.where(qry >= key, st, NEG)
                d = r_ref[2 * hp + a, i] - r_ref[2 * hp + a, j]
                pt = jnp.exp(st - (l_ref[a, :, pl.ds(off, tq)] - d))
                dsb = (pt * (dpt - dl_ref[a, :, pl.ds(off, tq)])).astype(MXU_DT)
                dv = dv + _mm_nt(dot_v[a, 0:DH, pl.ds(off, tq)], pt.astype(MXU_DT))
                dk = dk + _mm_nt(qat_v[a, 0:QROWS, pl.ds(off, tq)], dsb)
                dq_acc[a, :, pl.ds(off, tq)] += _mm(kat_ref[a], dsb)
                out.append((dk, dv))
            return tuple(out)

        init = tuple((jnp.zeros((QROWS, tq), F32), jnp.zeros((DH, tq), F32)) for _ in range(2))
        carry = step(j, init, True)
        (dka, dva), (dkb, dvb) = lax.fori_loop(j + 1, nq, lambda i, cr: step(i, cr, False), carry)
        dk_ref[...] = jnp.concatenate([dka[0:DH], dkb[0:DH]], axis=0).T.astype(dk_ref.dtype)
        dv_ref[...] = jnp.concatenate([dva, dvb], axis=0).T.astype(dv_ref.dtype)
        dck_ref[0] = -dka[DH + 3:DH + 4, :]
        dck_ref[1] = -dkb[DH + 3:DH + 4, :]

        @pl.when(j == nq - 1)
        def _():
            pltpu.sync_copy(dq_acc, dqt_hbm.at[pl.ds(2 * hp, 2)])

    pair = lambda rows, cols: pl.BlockSpec((2, rows, cols), lambda p, j: (p, 0, 0))
    return pl.pallas_call(
        body, name=name, grid=(NH // 2, nq),
        in_specs=[pl.BlockSpec(memory_space=pltpu.SMEM),
                  pl.BlockSpec((2, tq, LANES), lambda p, j: (p, j, 0)),
                  pl.BlockSpec((2, QROWS, tq), lambda p, j: (p, 0, j)),
                  pl.BlockSpec((2, tq, LANES), lambda p, j: (p, j, 0)),
                  pair(1, t), pair(1, t), ANY, ANY],
        out_specs=(pl.BlockSpec((tq, LANES), lambda p, j: (j, p)),
                   pl.BlockSpec((tq, LANES), lambda p, j: (j, p)),
                   pl.BlockSpec((2, 1, tq), lambda p, j: (p, 0, j)), ANY),
        out_shape=(jax.ShapeDtypeStruct((t, DA), MXU_DT), jax.ShapeDtypeStruct((t, DA), MXU_DT),
                   jax.ShapeDtypeStruct((NH, 1, t), F32), jax.ShapeDtypeStruct((NH, QROWS, t), F32)),
        scratch_shapes=[pltpu.VMEM((2, LANES, t), MXU_DT), pltpu.VMEM((2, LANES, t), MXU_DT),
                        pltpu.VMEM((2, QROWS, t), F32)],
        compiler_params=_params(52, 2),
    )(r, ka, kat, va, lrow, drow, qat, dot)


def _mix_proj_bwd(dr2, dq, dk, dv, dbch, dc, z, xh1, rs1, g1, w_qkv, w_bch, w_f, name):
    t = dr2.shape[0]
    tm = _tile(t, 512)
    nt = t // tm

    def body(dr_ref, dq_ref, dk_ref, dv_ref, db_ref, dc_ref, z_ref, x_ref, rs_ref, g_ref,
             wq_ref, wb_ref, wf_ref, out_ref, df_ref, st_ref, carry):
        @pl.when(pl.program_id(0) == 0)
        def _():
            carry[...] = jnp.zeros_like(carry)
            st_ref[...] = jnp.zeros_like(st_ref)

        row = lax.broadcasted_iota(jnp.int32, (tm, tm), 0)
        col = lax.broadcasted_iota(jnp.int32, (tm, tm), 1)
        triu = jnp.where(col >= row, 1.0, 0.0).astype(MXU_DT)
        dlogf = carry[...] + _mm_sel(triu, dc_ref[...])
        carry[...] = dlogf[0:1, :]
        dz = dlogf / (1.0 + jnp.exp(z_ref[...]))
        st_ref[2:3, 0:LANES] += _colsum(dz)
        dfb = dz.astype(MXU_DT)
        df_ref[...] = dfb
        dx = (ALPHA * dr_ref[...]
              + _mm_nt(dq_ref[...].astype(MXU_DT), wq_ref[:, 0:DA])
              + _mm_nt(dk_ref[...], wq_ref[:, DA:2 * DA])
              + _mm_nt(dv_ref[...], wq_ref[:, 2 * DA:3 * DA])
              + _mm_nt(db_ref[...], wb_ref[...])
              + _mm_nt(dfb, wf_ref[...]))
        xh = x_ref[...]
        st_ref[0:1, :] += _colsum(dx * xh)
        st_ref[1:2, :] += _colsum(dx)
        out_ref[...] = _ln_bwd(dx * g_ref[...], xh, rs_ref[...])

    row = lambda w: pl.BlockSpec((tm, w), lambda i: (nt - 1 - i, 0))
    full = lambda a: pl.BlockSpec(a.shape, lambda i: (0, 0))
    return pl.pallas_call(
        body, name=name, grid=(nt,),
        in_specs=[row(D), row(DA), row(DA), row(DA), row(3 * DCV), row(LANES), row(LANES), row(D), row(1),
                  full(g1), full(w_qkv), full(w_bch), full(w_f)],
        out_specs=(row(D), row(LANES), pl.BlockSpec((8, D), lambda i: (0, 0))),
        out_shape=(jax.ShapeDtypeStruct((t, D), F32), jax.ShapeDtypeStruct((t, LANES), MXU_DT),
                   jax.ShapeDtypeStruct((8, D), F32)),
        scratch_shapes=[pltpu.VMEM((1, LANES), F32)],
        compiler_params=_params(48),
    )(dr2, dq, dk, dv, dbch, dc, z, xh1, rs1, g1, w_qkv, w_bch, w_f)


def _dw(mode, a_parts, b, m, n, name, tmm=None, tn=None):
    t = b.shape[0]
    tmm = tmm or m
    tn = tn or n
    tt = _tile(t, 1024)

    def body(*refs):
        a_refs, b_ref, o_ref = refs[:len(a_parts)], refs[len(a_parts)], refs[len(a_parts) + 1]

        @pl.when(pl.program_id(2) == 0)
        def _():
            o_ref[...] = jnp.zeros_like(o_ref)

        if mode == "plain":
            a = a_refs[0][...].astype(MXU_DT)
        elif mode == "affine":
            a = (a_refs[0][...] * a_refs[1][...] + a_refs[2][...]).astype(MXU_DT)
        else:
            g = a_refs[0][...].astype(F32)
            a = (g * _sigmoid(g) * a_refs[1][...].astype(F32)).astype(MXU_DT)
        o_ref[...] += _mm_tn(a, b_ref[...].astype(MXU_DT))

    a_tile = pl.BlockSpec((tt, tmm), lambda i, j, k: (k, i))
    a_vec = pl.BlockSpec((1, tmm), lambda i, j, k: (0, i))
    a_specs = {"plain": [a_tile], "affine": [a_tile, a_vec, a_vec], "swiglu": [a_tile, a_tile]}[mode]
    return pl.pallas_call(
        body, name=name, grid=(m // tmm, n // tn, t // tt),
        in_specs=a_specs + [pl.BlockSpec((tt, tn), lambda i, j, k: (k, j))],
        out_specs=pl.BlockSpec((tmm, tn), lambda i, j, k: (i, j)),
        out_shape=jax.ShapeDtypeStruct((m, n), F32),
        compiler_params=_params(48, 3),
    )(*a_parts, b)


def _adamw(w, g, m, v):
    m = ADAM_B1 * m + (1.0 - ADAM_B1) * g
    v = ADAM_B2 * v + (1.0 - ADAM_B2) * (g * g)
    m_hat = m / (1.0 - ADAM_B1 ** ADAM_STEP)
    v_hat = v / (1.0 - ADAM_B2 ** ADAM_STEP)
    delta = -ADAM_LR * (m_hat / (jnp.sqrt(v_hat) + ADAM_EPS) + ADAM_WD * w)
    return delta, m, v


def _reduce_adamw(landed, own, w, m, v, name):
    r, c = own.shape
    tr = _tile(r, 128)

    def body(l_ref, o_ref, w_ref, m_ref, v_ref, g_out, d_out, m_out, v_out):
        me = 4 * lax.axis_index("x") + 2 * lax.axis_index("y") + lax.axis_index("c")
        g = None
        for j in range(NDEV):
            term = jnp.where(me == j, o_ref[...], l_ref[j].astype(F32))
            g = term if g is None else g + term
        g_out[...] = g
        d_out[...], m_out[...], v_out[...] = _adamw(w_ref[...], g, m_ref[...], v_ref[...])

    blk = pl.BlockSpec((tr, c), lambda i: (i, 0))
    sds = jax.ShapeDtypeStruct((r, c), F32)
    return pl.pallas_call(
        body, name=name, grid=(r // tr,),
        in_specs=[pl.BlockSpec((NDEV, tr, c), lambda i: (0, i, 0)), blk, blk, blk, blk],
        out_specs=(blk, blk, blk, blk), out_shape=(sds, sds, sds, sds),
        compiler_params=_params(40),
    )(landed, own, w, m, v)


def _sum_small(gathered, name):
    _, r, c = gathered.shape

    def body(g_ref, o_ref):
        acc = g_ref[0]
        for j in range(1, NDEV):
            acc = acc + g_ref[j]
        o_ref[...] = acc

    return pl.pallas_call(body, name=name, out_shape=jax.ShapeDtypeStruct((r, c), F32))(gathered)


def _adamw_small(g, w, m, v, name):
    def body(g_ref, w_ref, m_ref, v_ref, d_out, m_out, v_out):
        d_out[...], m_out[...], v_out[...] = _adamw(w_ref[...], g_ref[...], m_ref[...], v_ref[...])

    sds = jax.ShapeDtypeStruct(g.shape, F32)
    return pl.pallas_call(body, name=name, out_shape=(sds, sds, sds))(g, w, m, v)


def _cols_from_stack(s):
    return jnp.transpose(s, (1, 0, 2)).reshape(s.shape[1], NDEV * s.shape[2])


def _cols_to_stack(w):
    r, c = w.shape
    return jnp.transpose(w.reshape(r, NDEV, c // NDEV), (1, 0, 2))


def _rows_from_stack(s):
    return s.reshape(NDEV * s.shape[1], s.shape[2])


def _rows_to_stack(w):
    r, c = w.shape
    return w.reshape(NDEV, r // NDEV, c)


SMALL_ROWS = 16
SMALL_SLOTS = {
    "ln1_g": (0, 0, D), "ln1_b": (1, 0, D), "ln2_g": (2, 0, D), "ln2_b": (3, 0, D), "ln3_g": (4, 0, D),
    "ln3_b": (5, 0, D), "b_ple_gate": (6, 0, D), "ln4_g": (7, 0, D), "ln4_b": (8, 0, D),
    "g_attn": (9, 0, DA), "g_conv": (9, DA, DCV), "b_forget": (10, 0, NH),
}
CONVW_ROW = 11
LOSS_SLOT = (10, LANES)


def _pack_small(vals, conv_rows, loss=None):
    out = jnp.zeros((SMALL_ROWS, D), F32)
    for nm, (r, off, wd) in SMALL_SLOTS.items():
        out = out.at[r:r + 1, off:off + wd].set(vals[nm].reshape(1, wd).astype(F32))
    out = out.at[CONVW_ROW:CONVW_ROW + 3, 0:conv_rows.shape[1]].set(conv_rows.astype(F32))
    if loss is not None:
        out = out.at[LOSS_SLOT[0], LOSS_SLOT[1]].set(loss)
    return out


def _unpack_small(packed, name):
    r, off, wd = SMALL_SLOTS[name]
    return packed[r:r + 1, off:off + wd]


def kernel(x, p, ffn1_w_in, ffn1_w_out, ln1_g, ln1_b, w_mix_in, b_forget, conv_w, g_attn, g_conv, w_mix_out, ln2_g, ln2_b, ffn2_w_in, ffn2_w_out, ln3_g, ln3_b, w_ple, w_ple_gate, b_ple_gate, ln4_g, ln4_b, loss_target, m_ffn1_w_in, m_ffn1_w_out, m_ln1_g, m_ln1_b, m_w_mix_in, m_b_forget, m_conv_w, m_g_attn, m_g_conv, m_w_mix_out, m_ln2_g, m_ln2_b, m_ffn2_w_in, m_ffn2_w_out, m_ln3_g, m_ln3_b, m_w_ple, m_w_ple_gate, m_b_ple_gate, m_ln4_g, m_ln4_b, v_ffn1_w_in, v_ffn1_w_out, v_ln1_g, v_ln1_b, v_w_mix_in, v_b_forget, v_conv_w, v_g_attn, v_g_conv, v_w_mix_out, v_ln2_g, v_ln2_b, v_ffn2_w_in, v_ffn2_w_out, v_ln3_g, v_ln3_b, v_w_ple, v_w_ple_gate, v_b_ple_gate, v_ln4_g, v_ln4_b):
    args = dict(locals())
    t = x.shape[1]
    me = 4 * lax.axis_index("x") + 2 * lax.axis_index("y") + lax.axis_index("c")
    x0 = x.reshape(t, D)
    p0 = p.reshape(t, PLE)
    tgt = loss_target.reshape(t, D)

    big = ["ffn1_w_in", "ffn1_w_out", "w_mix_in", "w_mix_out", "ffn2_w_in", "ffn2_w_out", "w_ple", "w_ple_gate"]
    col_sharded = {"ffn1_w_in", "w_mix_in", "ffn2_w_in", "w_ple"}
    shard = {nm: args[nm][0] for nm in big}

    gathered = _exchange([shard[nm].astype(WIRE_DT) for nm in big] + [conv_w[0]], True, "ag_weights")
    full = {nm: (_cols_from_stack(g) if nm in col_sharded else _rows_from_stack(g)).astype(MXU_DT)
            for nm, g in zip(big, gathered[:len(big)])}
    cw = _cols_from_stack(gathered[len(big)])
    gx, gw, small, loss_part = _local_step(x0, p0, tgt, full, cw, {nm: args[nm] for nm in SMALL_SLOTS})

    stacks = [(_cols_to_stack(gw[nm]) if nm in col_sharded else _rows_to_stack(gw[nm])) for nm in big]
    landed = _exchange([s.astype(WIRE_DT) for s in stacks], False, "rs_grads")
    small_part = _pack_small({nm: small[nm] for nm in SMALL_SLOTS},
                             jnp.pad(small["conv_w"], ((0, 0), (0, D - DCV))), loss_part)
    small_all = _exchange([small_part], True, "ag_small")[0]
    small_g = _sum_small(small_all, "sum_small")
    loss = small_g[LOSS_SLOT[0], LOSS_SLOT[1]]

    outs = {"loss": loss, "grad_x": gx.reshape(1, t, D)}
    for nm, st, ld in zip(big, stacks, landed):
        own = lax.dynamic_index_in_dim(st, me, axis=0, keepdims=False)
        g, dl, mn, vn = _reduce_adamw(ld, own, shard[nm], args["m_" + nm][0], args["v_" + nm][0], "adamw_" + nm)
        outs["grad_" + nm], outs["delta_" + nm], outs["new_m_" + nm], outs["new_v_" + nm] = (
            g[None], dl[None], mn[None], vn[None])
    small_names = list(SMALL_SLOTS)
    cshard = lax.dynamic_slice_in_dim(small_g[CONVW_ROW:CONVW_ROW + 3, 0:DCV], me * (DCV // NDEV), DCV // NDEV, axis=1)
    g_pack = _pack_small({nm: _unpack_small(small_g, nm) for nm in small_names}, cshard)
    packs = [_pack_small({nm: args[pre + nm] for nm in small_names}, args[pre + "conv_w"][0])
             for pre in ("", "m_", "v_")]
    d_pack, m_pack, v_pack = _adamw_small(g_pack, packs[0], packs[1], packs[2], "adamw_small")
    for key, pk in (("grad_", g_pack), ("delta_", d_pack), ("new_m_", m_pack), ("new_v_", v_pack)):
        for nm in small_names:
            outs[key + nm] = _unpack_small(pk, nm)
        outs[key + "conv_w"] = pk[CONVW_ROW:CONVW_ROW + 3, 0:DCV // NDEV][None]

    wnames = ["ffn1_w_in", "ffn1_w_out", "ln1_g", "ln1_b", "w_mix_in", "b_forget", "conv_w", "g_attn", "g_conv",
              "w_mix_out", "ln2_g", "ln2_b", "ffn2_w_in", "ffn2_w_out", "ln3_g", "ln3_b", "w_ple", "w_ple_gate",
              "b_ple_gate", "ln4_g", "ln4_b"]
    return (outs["loss"], outs["grad_x"], *[outs[pre + nm] for pre in ("grad_", "delta_", "new_m_", "new_v_")
                                            for nm in wnames])


def _local_step(x0, p0, tgt, full, cw, sp):
    t = x0.shape[0]
    ln1_g, ln1_b, ln2_g, ln2_b, ln3_g, ln3_b = (sp[k] for k in ("ln1_g", "ln1_b", "ln2_g", "ln2_b", "ln3_g", "ln3_b"))
    ln4_g, ln4_b, g_attn, g_conv, b_ple_gate = (sp[k] for k in ("ln4_g", "ln4_b", "g_attn", "g_conv", "b_ple_gate"))
    wmi = full["w_mix_in"]
    w_qkv = wmi[:, 0:3 * DA]
    w_f = jnp.pad(wmi[:, 3 * DA:3 * DA + NH], ((0, 0), (0, LANES - NH)))
    w_bch = wmi[:, 3 * DA + NH:]
    bf_pad = jnp.pad(sp["b_forget"], ((0, 0), (0, LANES - NH)))
    ones = jnp.ones((1, D), F32)
    zeros = jnp.zeros((1, D), F32)

    g1a, u1a, xh1, rs1 = _ffn_fwd(x0, ones, zeros, full["ffn1_w_in"], full["ffn1_w_out"], "ffn1_fwd")
    q, k, v, bch, z, c = _mix_proj_fwd(xh1, ln1_g, ln1_b, w_qkv, w_bch, w_f, bf_pad, "mix_proj_fwd")
    c8 = jnp.transpose(c[:, 0:NH])
    qa, ka, vta, rtile = _attn_operands(q, k, v, c8, _tile(t, 512))
    o, lse = _attn_fwd(qa, ka, vta, rtile, "attn_fwd")
    merged, xh2, rs2 = _mix_post_fwd(o, bch, cw, g_attn, g_conv, xh1, ln1_g, ln1_b, full["w_mix_out"],
                                     "mix_post_fwd")
    g2a, u2a, xh3, rs3 = _ffn_fwd(xh2, ln2_g, ln2_b, full["ffn2_w_in"], full["ffn2_w_out"], "ffn2_fwd")

    dr3, dz, de, st_tail = _tail(xh3, rs3, ln3_g, ln3_b, p0, full["w_ple_gate"], full["w_ple"], b_ple_gate,
                                 ln4_g, ln4_b, tgt, "tail")
    df2, dg2, du2, dr2, st_f2 = _ffn_bwd(dr3, g2a, u2a, xh2, rs2, ln2_g, full["ffn2_w_in"], full["ffn2_w_out"],
                                         True, "ffn2_bwd")
    dmix, do, delta, dyc, st_post = _mix_post_bwd(dr2, o, bch, cw, g_attn, g_conv, full["w_mix_out"],
                                                  "mix_post_bwd")
    dbch, st_conv = _conv_bwd(dyc, bch, cw, "conv_bwd")
    drow = jnp.transpose(delta[:, 0:NH]).reshape(NH, 1, t)
    qat, kat, va, dot = _attn_bwd_operands(qa, ka, v, do)
    dk, dv, dck, dqt = _attn_bwd(ka, kat, va, qat, dot, lse, drow, rtile, "attn_bwd")
    dq = jnp.transpose(dqt[:, 0:DH, :], (2, 0, 1)).reshape(t, DA) * 0.125
    dc_pad = jnp.pad(jnp.transpose(dqt[:, DH, :] + dck.reshape(NH, t)), ((0, 0), (0, LANES - NH)))
    dr1, dfl, st_proj = _mix_proj_bwd(dr2, dq, dk, dv, dbch, dc_pad, z, xh1, rs1, ln1_g, w_qkv, w_bch, w_f,
                                      "mix_proj_bwd")
    df1, dg1, du1, gx, _ = _ffn_bwd(dr1, g1a, u1a, x0, rs1, ones, full["ffn1_w_in"], full["ffn1_w_out"],
                                    False, "ffn1_bwd")

    x1p, x2p, x3p = (xh1, ln1_g, ln1_b), (xh2, ln2_g, ln2_b), (xh3, ln3_g, ln3_b)
    gw = {}
    gw["ffn1_w_in"] = jnp.concatenate(
        [_dw("affine", (x0, ones, zeros), dg1, D, F, "dw_ffn1_in_g", tn=F // 2),
         _dw("affine", (x0, ones, zeros), du1, D, F, "dw_ffn1_in_u", tn=F // 2)], axis=1)
    gw["ffn1_w_out"] = _dw("swiglu", (g1a, u1a), df1, F, D, "dw_ffn1_out", tmm=F // 2)
    gw["ffn2_w_in"] = jnp.concatenate(
        [_dw("affine", x2p, dg2, D, F, "dw_ffn2_in_g", tn=F // 2),
         _dw("affine", x2p, du2, D, F, "dw_ffn2_in_u", tn=F // 2)], axis=1)
    gw["ffn2_w_out"] = _dw("swiglu", (g2a, u2a), df2, F, D, "dw_ffn2_out", tmm=F // 2)
    gw["w_mix_out"] = _dw("plain", (merged,), dmix, D, D, "dw_mix_out")
    gw["w_mix_in"] = jnp.concatenate(
        [_dw("affine", x1p, dq, D, DA, "dw_mix_in_q"), _dw("affine", x1p, dk, D, DA, "dw_mix_in_k"),
         _dw("affine", x1p, dv, D, DA, "dw_mix_in_v"),
         _dw("affine", x1p, dfl, D, LANES, "dw_mix_in_f")[:, 0:NH],
         _dw("affine", x1p, dbch, D, 3 * DCV, "dw_mix_in_bch")], axis=1)
    gw["w_ple_gate"] = _dw("affine", x3p, dz, D, D, "dw_ple_gate")
    gw["w_ple"] = _dw("plain", (p0,), de, PLE, D, "dw_ple")

    loss_part = (0.5 / D) * jnp.sum(st_tail[5:6, :])
    small = {"ln1_g": st_proj[0:1], "ln1_b": st_proj[1:2], "ln2_g": st_f2[0:1], "ln2_b": st_f2[1:2],
             "ln3_g": st_tail[3:4], "ln3_b": st_tail[4:5], "b_ple_gate": st_tail[2:3], "ln4_g": st_tail[0:1],
             "ln4_b": st_tail[1:2], "g_attn": st_post[0:1], "g_conv": st_post[1:2],
             "b_forget": st_proj[2:3, 0:NH], "conv_w": st_conv[0:3]}
    return gx, gw, small, loss_part
```

```python
import functools
import math

import jax
import jax.numpy as jnp
from jax import lax
from jax.experimental import pallas as pl
from jax.experimental.pallas import tpu as pltpu

D = 1024
F = 2816
NH = 8
DH = 64
DA = NH * DH
DCV = D - DA
PLE = 256
NPROJ = 3 * DA + NH + 3 * DCV
LN_EPS = 1e-5
RMS_EPS = 1e-6
NEG = -1e30
ALPHA = 2.0 ** 0.25
NDEV = 8
LANES = 128

ADAM_LR, ADAM_B1, ADAM_B2, ADAM_EPS, ADAM_WD, ADAM_STEP = 0.001, 0.9, 0.999, 1e-08, 0.01, 10

F32 = jnp.float32
MXU_DT = jnp.bfloat16
WIRE_DT = jnp.bfloat16

MESH_ID = pl.DeviceIdType.MESH
ANY = pl.BlockSpec(memory_space=pl.ANY)


def _params(vmem_mb, n_axes=1):
    return pltpu.CompilerParams(dimension_semantics=("arbitrary",) * n_axes,
                                vmem_limit_bytes=int(vmem_mb) << 20)


def _mm(a, b):
    return jnp.dot(a, b, preferred_element_type=F32)


def _mm_nt(a, b):
    return lax.dot_general(a, b, (((1,), (1,)), ((), ())), preferred_element_type=F32)


def _mm_tn(a, b):
    return lax.dot_general(a, b, (((0,), (0,)), ((), ())), preferred_element_type=F32)


def _split3(x):
    hi = x.astype(MXU_DT)
    r1 = x - hi.astype(F32)
    mid = r1.astype(MXU_DT)
    lo = (r1 - mid.astype(F32)).astype(MXU_DT)
    return hi, mid, lo


def _mm_sel(sel, x):
    hi, mid, lo = _split3(x)
    return _mm(sel, hi) + _mm(sel, mid) + _mm(sel, lo)


def _mm_xsel(x, sel):
    hi, mid, lo = _split3(x)
    return _mm(hi, sel) + _mm(mid, sel) + _mm(lo, sel)


def _sigmoid(x):
    return 1.0 / (1.0 + jnp.exp(-x))


def _ln_fwd(r):
    mu = jnp.mean(r, axis=-1, keepdims=True)
    xc = r - mu
    var = jnp.mean(xc * xc, axis=-1, keepdims=True)
    rstd = lax.rsqrt(var + LN_EPS)
    return xc * rstd, rstd


def _ln_bwd(dxhat, xhat, rstd):
    m1 = jnp.mean(dxhat, axis=-1, keepdims=True)
    m2 = jnp.mean(dxhat * xhat, axis=-1, keepdims=True)
    return rstd * (dxhat - m1 - xhat * m2)


def _rms_fwd(x):
    r = lax.rsqrt(jnp.mean(x * x, axis=-1, keepdims=True) + RMS_EPS)
    return x * r, r


def _rms_bwd(dyg, xn, r):
    return r * (dyg - xn * jnp.mean(dyg * xn, axis=-1, keepdims=True))


def _colsum(x):
    return jnp.sum(x, axis=0, keepdims=True)


def _f_chunks():
    out, c0 = [], 0
    while c0 < F:
        fc = min(512, F - c0)
        out.append((c0, fc))
        c0 += fc
    return out


def _tile(t, want):
    return want if t % want == 0 and t >= want else t


def _exchange(arrs, name):
    n = len(arrs)
    out_shape = [jax.ShapeDtypeStruct(a.shape, a.dtype) for a in arrs]

    def body(*refs):
        ins, outs = refs[:n], refs[n:2 * n]
        send_sems, recv_sems = refs[2 * n:]
        x, y, c = lax.axis_index("x"), lax.axis_index("y"), lax.axis_index("c")
        me = 4 * x + 2 * y + c
        peers = []
        for k in range(1, NDEV):
            px = 1 - x if (k >> 2) & 1 else x
            py = 1 - y if (k >> 1) & 1 else y
            pc = 1 - c if k & 1 else c
            peers.append(((px, py, pc), 4 * px + 2 * py + pc))

        def remote(w, k, slot_src, slot_dst):
            return pltpu.make_async_remote_copy(
                src_ref=ins[w].at[slot_src], dst_ref=outs[w].at[slot_dst],
                send_sem=send_sems.at[w * (NDEV - 1) + k], recv_sem=recv_sems.at[w * (NDEV - 1) + k],
                device_id=peers[k][0], device_id_type=MESH_ID)

        for k in range(NDEV - 1):
            for w in range(n):
                remote(w, k, peers[k][1], me).start()
        for k in range(NDEV - 1):
            for w in range(n):
                remote(w, k, me, peers[k][1]).wait_recv()
        for k in range(NDEV - 1):
            for w in range(n):
                remote(w, k, peers[k][1], me).wait_send()

    return pl.pallas_call(
        body, name=name, out_shape=tuple(out_shape),
        in_specs=[ANY] * n, out_specs=tuple([ANY] * n),
        scratch_shapes=[pltpu.SemaphoreType.DMA((n * (NDEV - 1),)), pltpu.SemaphoreType.DMA((n * (NDEV - 1),))],
    )(*arrs)


def _allgather(arrs, name):
    n = len(arrs)
    per = NDEV - 1

    def body(*refs):
        ins, outs = refs[:n], refs[n:2 * n]
        send_sems, recv_sems, loc_sems = refs[2 * n:]
        x, y, c = lax.axis_index("x"), lax.axis_index("y"), lax.axis_index("c")
        me, sib = (x, y, c), (x, y, 1 - c)
        chips = [(1 - x, y), (x, 1 - y), (1 - x, 1 - y)]

        def copy(w, k, block, to, src=None):
            dst = outs[w].at[4 * block[0] + 2 * block[1] + block[2]]
            return pltpu.make_async_remote_copy(
                src_ref=dst if src is None else src, dst_ref=dst,
                send_sem=send_sems.at[w * per + k], recv_sem=recv_sems.at[w * per + k],
                device_id=to, device_id_type=MESH_ID)

        local = [pltpu.make_async_copy(ins[w], outs[w].at[4 * x + 2 * y + c], loc_sems.at[w]) for w in range(n)]
        for lc in local:
            lc.start()
        sent = []
        for j, chip in enumerate(chips):
            sent += [copy(w, 1 + j, me, (*chip, c), src=ins[w]) for w in range(n)]
        sent += [copy(w, 0, me, sib, src=ins[w]) for w in range(n)]
        for cp in sent:
            cp.start()
        for j, chip in enumerate(chips):
            for w in range(n):
                copy(w, 1 + j, (*chip, c), me).wait_recv()
                fwd = copy(w, 4 + j, (*chip, c), sib)
                fwd.start()
                sent.append(fwd)
        for w in range(n):
            copy(w, 0, sib, me).wait_recv()
        for j, chip in enumerate(chips):
            for w in range(n):
                copy(w, 4 + j, (*chip, 1 - c), me).wait_recv()
        for cp in sent:
            cp.wait_send()
        for lc in local:
            lc.wait()

    return pl.pallas_call(
        body, name=name, out_shape=tuple(jax.ShapeDtypeStruct((NDEV,) + a.shape, a.dtype) for a in arrs),
        in_specs=[ANY] * n, out_specs=tuple([ANY] * n),
        scratch_shapes=[pltpu.SemaphoreType.DMA((n * per,)), pltpu.SemaphoreType.DMA((n * per,)),
                        pltpu.SemaphoreType.DMA((n,))],
    )(*arrs)


def _ffn_fwd(xin, gin, bin_, w_in, w_out, name):
    t = xin.shape[0]
    tm = _tile(t, 512)
    chunks = _f_chunks()

    def body(x_ref, gi_ref, bi_ref, win_hbm, wout_hbm, g_ref, u_ref, xh_ref, rs_ref, win_v, wout_v, acc_ref):
        @pl.when(pl.program_id(0) == 0)
        def _():
            pltpu.sync_copy(win_hbm, win_v)
            pltpu.sync_copy(wout_hbm, wout_v)

        x = x_ref[...] * gi_ref[...] + bi_ref[...]
        xb = x.astype(MXU_DT)
        for ci, (c0, fc) in enumerate(chunks):
            gc = _mm(xb, win_v[:, c0:c0 + fc])
            uc = _mm(xb, win_v[:, F + c0:F + c0 + fc])
            g_ref[:, c0:c0 + fc] = gc.astype(g_ref.dtype)
            u_ref[:, c0:c0 + fc] = uc.astype(u_ref.dtype)
            hc = (gc * _sigmoid(gc) * uc).astype(MXU_DT)
            part = _mm(hc, wout_v[c0:c0 + fc, :])
            if ci == 0:
                acc_ref[...] = part
            else:
                acc_ref[...] += part
        xh, rstd = _ln_fwd(ALPHA * x + 0.5 * acc_ref[...])
        xh_ref[...] = xh
        rs_ref[...] = rstd

    row = pl.BlockSpec((tm, D), lambda i: (i, 0))
    vec = pl.BlockSpec((1, D), lambda i: (0, 0))
    act = pl.BlockSpec((tm, F), lambda i: (i, 0))
    return pl.pallas_call(
        body, name=name, grid=(t // tm,),
        in_specs=[row, vec, vec, ANY, ANY],
        out_specs=(act, act, row, pl.BlockSpec((tm, 1), lambda i: (i, 0))),
        out_shape=(jax.ShapeDtypeStruct((t, F), MXU_DT), jax.ShapeDtypeStruct((t, F), MXU_DT),
                   jax.ShapeDtypeStruct((t, D), F32), jax.ShapeDtypeStruct((t, 1), F32)),
        scratch_shapes=[pltpu.VMEM((D, 2 * F), MXU_DT), pltpu.VMEM((F, D), MXU_DT), pltpu.VMEM((tm, D), F32)],
        compiler_params=_params(52),
    )(xin, gin, bin_, w_in, w_out)


QROWS = 80
BIAS_AT = DH


def _place_matrices():
    import numpy as np
    pk = np.zeros((NH, LANES, LANES), np.float32)
    pqt = np.zeros((NH, LANES, LANES), np.float32)
    for h in range(NH):
        for piece in range(3):
            pk[h, 8 * piece + h, BIAS_AT + 3 + piece] = -1.0
            pqt[h, BIAS_AT + piece, 8 * piece + h] = 1.0
    pkt = np.transpose(pk, (0, 2, 1))
    return tuple(jnp.asarray(m, MXU_DT) for m in (pk, pqt, pkt))


def _mix_proj_fwd(xh1, g1, b1, w_kv, w_qkv_t, w_bch, w_f, bf_pad, name):
    t = xh1.shape[0]
    tm = _tile(t, 512)
    pk, pqt, pkt = _place_matrices()

    def body(x_ref, g_ref, b_ref, wkv_ref, wt_ref, wb_ref, wf_ref, bf_ref, pk_ref, pqt_ref, pkt_ref,
             ka_ref, va_ref, qat_ref, kat_ref, vta_ref, bch_ref, z_ref, r_ref, carry):
        @pl.when(pl.program_id(0) == 0)
        def _():
            carry[...] = jnp.zeros_like(carry)

        xb = (x_ref[...] * g_ref[...] + b_ref[...]).astype(MXU_DT)
        kv = _mm(xb, wkv_ref[...])
        qkvt = _mm_nt(wt_ref[...], xb)
        bch_ref[...] = _mm(xb, wb_ref[...])
        z = _mm(xb, wf_ref[...]) + bf_ref[...]
        z_ref[...] = z
        logf = jnp.minimum(z, 0.0) - jnp.log(1.0 + jnp.exp(-jnp.abs(z)))
        row = lax.broadcasted_iota(jnp.int32, (tm, tm), 0)
        col = lax.broadcasted_iota(jnp.int32, (tm, tm), 1)
        tri = jnp.where(row >= col, 1.0, 0.0).astype(MXU_DT)
        c = carry[...] + _mm_sel(tri, logf)
        carry[...] = c[tm - 1:tm, :]
        r_ref[0] = c[0:1, :]
        lane = lax.broadcasted_iota(jnp.int32, (1, LANES), 1)
        hi, mid, lo = _split3(jnp.where(lane < NH, c - c[0:1, :], 0.0))
        pieces = (hi.astype(F32) + pltpu.roll(mid.astype(F32), 8, 1) + pltpu.roll(lo.astype(F32), 16, 1)
                  ).astype(MXU_DT)
        sub = lax.broadcasted_iota(jnp.int32, (DH, 1), 0)
        ones_k_lanes = jnp.where((lane >= BIAS_AT) & (lane < BIAS_AT + 3), 1.0, 0.0)
        ones_q_rows = jnp.where((sub >= 3) & (sub < 6), 1.0, 0.0)
        ones_k_rows = jnp.where(sub[0:QROWS - DH] < 3, 1.0, 0.0)
        first_row = jnp.where(sub == 0, 1.0, 0.0) + jnp.zeros((DH, tm), F32)
        for h in range(NH):
            pair, odd = divmod(h, 2)
            k2 = kv[:, LANES * pair:LANES * (pair + 1)]
            v2 = kv[:, DA + LANES * pair:DA + LANES * (pair + 1)]
            if odd:
                k2, v2 = pltpu.roll(k2, DH, 1), pltpu.roll(v2, DH, 1)
            ka_ref[h] = jnp.where(lane < DH, k2, _mm(pieces, pk_ref[h]) + ones_k_lanes).astype(ka_ref.dtype)
            va_ref[h] = jnp.where(lane < DH, v2, 0.0).astype(va_ref.dtype)
            qat_ref[h, 0:DH, :] = (qkvt[DH * h:DH * (h + 1)] * 0.125).astype(qat_ref.dtype)
            qat_ref[h, DH:LANES, :] = (_mm_nt(pqt_ref[h], pieces)[DH:LANES] + ones_q_rows).astype(qat_ref.dtype)
            kat_ref[h, 0:DH, :] = qkvt[DA + DH * h:DA + DH * (h + 1)].astype(kat_ref.dtype)
            kat_ref[h, DH:QROWS, :] = (_mm_nt(pkt_ref[h], pieces)[DH:QROWS] + ones_k_rows).astype(kat_ref.dtype)
            vt = qkvt[2 * DA + DH * h:2 * DA + DH * (h + 1)]
            vta_ref[h, 0:DH, :] = (first_row if odd else vt).astype(vta_ref.dtype)
            vta_ref[h, DH:LANES, :] = (vt if odd else first_row).astype(vta_ref.dtype)

    row = lambda w: pl.BlockSpec((tm, w), lambda i: (i, 0))
    full = lambda a: pl.BlockSpec(a.shape, lambda i: (0,) * a.ndim)
    nat = pl.BlockSpec((NH, tm, LANES), lambda i: (0, i, 0))
    fmaj = lambda rows: pl.BlockSpec((NH, rows, tm), lambda i: (0, 0, i))
    return pl.pallas_call(
        body, name=name, grid=(t // tm,),
        in_specs=[row(D), full(g1), full(b1), full(w_kv), full(w_qkv_t), full(w_bch), full(w_f), full(bf_pad),
                  full(pk), full(pqt), full(pkt)],
        out_specs=(nat, nat, fmaj(LANES), fmaj(QROWS), fmaj(LANES), row(3 * DCV), row(LANES),
                   pl.BlockSpec((1, 1, LANES), lambda i: (i, 0, 0))),
        out_shape=(jax.ShapeDtypeStruct((NH, t, LANES), MXU_DT), jax.ShapeDtypeStruct((NH, t, LANES), MXU_DT),
                   jax.ShapeDtypeStruct((NH, LANES, t), MXU_DT), jax.ShapeDtypeStruct((NH, QROWS, t), MXU_DT),
                   jax.ShapeDtypeStruct((NH, LANES, t), MXU_DT), jax.ShapeDtypeStruct((t, 3 * DCV), F32),
                   jax.ShapeDtypeStruct((t, LANES), F32), jax.ShapeDtypeStruct((t // tm, 1, LANES), F32)),
        scratch_shapes=[pltpu.VMEM((1, LANES), F32)],
        compiler_params=_params(56),
    )(xh1, g1, b1, w_kv, w_qkv_t, w_bch, w_f, bf_pad, pk, pqt, pkt)


def _attn_fwd(qat, ka, vta, r, name):
    t = ka.shape[1]
    tq = _tile(t, 512)
    nq = t // tq

    def body(r_ref, q_ref, k_ref, v_ref, o_ref, l_ref):
        hp, i = pl.program_id(0), pl.program_id(1)
        key = lax.broadcasted_iota(jnp.int32, (tq, tq), 0)
        qry = lax.broadcasted_iota(jnp.int32, (tq, tq), 1)

        def step(j, carry, masked):
            off = pl.multiple_of(j * tq, tq)
            out = []
            for a in range(2):
                m, acc = carry[a]
                st = _mm(k_ref[a, pl.ds(off, tq), :], q_ref[a])
                if masked:
                    st = jnp.where(qry >= key, st, NEG)
                d = r_ref[2 * hp + a, i] - r_ref[2 * hp + a, j]
                m_new = jnp.maximum(m, jnp.max(st, axis=0, keepdims=True) + d)
                pt = jnp.exp(st - (m_new - d))
                acc = jnp.exp(m - m_new) * acc + _mm(v_ref[a, :, pl.ds(off, tq)], pt.astype(MXU_DT))
                out.append((m_new, acc))
            return tuple(out)

        init = tuple((jnp.full((1, tq), NEG, F32), jnp.zeros((LANES, tq), F32)) for _ in range(2))
        carry = lax.fori_loop(0, i, lambda j, cr: step(j, cr, False), init)
        (ma, acca), (mb, accb) = step(i, carry, True)
        la, lb = acca[DH:DH + 1, :], accb[0:1, :]
        l_ref[0] = ma + jnp.log(la)
        l_ref[1] = mb + jnp.log(lb)
        sub = lax.broadcasted_iota(jnp.int32, (LANES, tq), 0)
        o_ref[...] = jnp.where(sub < DH, acca / la, accb / lb).T

    return pl.pallas_call(
        body, name=name, grid=(NH // 2, nq),
        in_specs=[pl.BlockSpec(memory_space=pltpu.SMEM),
                  pl.BlockSpec((2, LANES, tq), lambda p, i: (p, 0, i)),
                  pl.BlockSpec((2, t, LANES), lambda p, i: (p, 0, 0)),
                  pl.BlockSpec((2, LANES, t), lambda p, i: (p, 0, 0))],
        out_specs=(pl.BlockSpec((tq, LANES), lambda p, i: (i, p)),
                   pl.BlockSpec((2, 1, tq), lambda p, i: (p, 0, i))),
        out_shape=(jax.ShapeDtypeStruct((t, DA), F32), jax.ShapeDtypeStruct((NH, 1, t), F32)),
        compiler_params=_params(48, 2),
    )(r, qat, ka, vta)


def _conv_parts(bch):
    return bch[:, 0:DCV], bch[:, DCV:2 * DCV], bch[:, 2 * DCV:3 * DCV]


def _mix_post_fwd(o, bch, conv_w, g_attn, g_conv, xh1, g1, b1, w_mo, name):
    t = o.shape[0]
    tm = _tile(t, 512)
    hb = tm // 8

    def body(o_ref, bch_ref, halo_ref, cw_ref, ga_ref, gc_ref, x_ref, g_ref, b_ref, w_ref,
             mg_ref, xh_ref, rs_ref, ext):
        i = pl.program_id(0)
        an, _ = _rms_fwd(o_ref[...])
        mg_ref[:, 0:DA] = (an * ga_ref[...]).astype(mg_ref.dtype)
        bb, cc, hh = _conv_parts(bch_ref[...])
        _, hc, hh_h = _conv_parts(halo_ref[...])
        u = cc * hh
        ext[0:8, :] = jnp.where(i > 0, hc * hh_h, 0.0)
        ext[8:8 + tm, :] = u
        raw = cw_ref[0:1, :] * ext[6:6 + tm, :] + cw_ref[1:2, :] * ext[7:7 + tm, :] + cw_ref[2:3, :] * u
        cn, _ = _rms_fwd(bb * raw)
        mg_ref[:, DA:D] = (cn * gc_ref[...]).astype(mg_ref.dtype)
        x1 = x_ref[...] * g_ref[...] + b_ref[...]
        xh, rstd = _ln_fwd(ALPHA * x1 + _mm(mg_ref[...], w_ref[...]))
        xh_ref[...] = xh
        rs_ref[...] = rstd

    row = lambda w: pl.BlockSpec((tm, w), lambda i: (i, 0))
    full = lambda a: pl.BlockSpec(a.shape, lambda i: (0, 0))
    return pl.pallas_call(
        body, name=name, grid=(t // tm,),
        in_specs=[row(DA), row(3 * DCV),
                  pl.BlockSpec((8, 3 * DCV), lambda i: (jnp.maximum(i * hb - 1, 0), 0)),
                  full(conv_w), full(g_attn), full(g_conv), row(D), full(g1), full(b1), full(w_mo)],
        out_specs=(row(D), row(D), pl.BlockSpec((tm, 1), lambda i: (i, 0))),
        out_shape=(jax.ShapeDtypeStruct((t, D), MXU_DT), jax.ShapeDtypeStruct((t, D), F32),
                   jax.ShapeDtypeStruct((t, 1), F32)),
        scratch_shapes=[pltpu.VMEM((tm + 8, DCV), F32)],
        compiler_params=_params(48),
    )(o, bch, bch, conv_w, g_attn, g_conv, xh1, g1, b1, w_mo)


def _tail(xh3, rs3, g3, b3, p, w_g, w_ple, bg, g4, b4, target, name):
    t = xh3.shape[0]
    tm = _tile(t, 512)

    def body(x_ref, rs_ref, g3_ref, b3_ref, p_ref, wg_ref, wp_ref, bg_ref, g4_ref, b4_ref, t_ref,
             dr_ref, dz_ref, de_ref, st_ref):
        @pl.when(pl.program_id(0) == 0)
        def _():
            st_ref[...] = jnp.zeros_like(st_ref)

        xh3v = x_ref[...]
        x3 = xh3v * g3_ref[...] + b3_ref[...]
        gate = _sigmoid(_mm(x3.astype(MXU_DT), wg_ref[...]) + bg_ref[...])
        e = _mm(p_ref[...].astype(MXU_DT), wp_ref[...])
        xh4, rstd4 = _ln_fwd(ALPHA * x3 + gate * e)
        diff = xh4 * g4_ref[...] + b4_ref[...] - t_ref[...]
        dy = diff * (1.0 / D)
        st_ref[5:6, :] += _colsum(diff * diff)
        st_ref[0:1, :] += _colsum(dy * xh4)
        st_ref[1:2, :] += _colsum(dy)
        dr4 = _ln_bwd(dy * g4_ref[...], xh4, rstd4)
        de_ref[...] = (dr4 * gate).astype(de_ref.dtype)
        dz = dr4 * e * gate * (1.0 - gate)
        st_ref[2:3, :] += _colsum(dz)
        dzb = dz.astype(MXU_DT)
        dz_ref[...] = dzb
        dx3 = ALPHA * dr4 + _mm_nt(dzb, wg_ref[...])
        st_ref[3:4, :] += _colsum(dx3 * xh3v)
        st_ref[4:5, :] += _colsum(dx3)
        dr_ref[...] = _ln_bwd(dx3 * g3_ref[...], xh3v, rs_ref[...])

    row = lambda w: pl.BlockSpec((tm, w), lambda i: (i, 0))
    full = lambda a: pl.BlockSpec(a.shape, lambda i: (0, 0))
    return pl.pallas_call(
        body, name=name, grid=(t // tm,),
        in_specs=[row(D), row(1), full(g3), full(b3), row(PLE), full(w_g), full(w_ple), full(bg), full(g4),
                  full(b4), row(D)],
        out_specs=(row(D), row(D), row(D), pl.BlockSpec((8, D), lambda i: (0, 0))),
        out_shape=(jax.ShapeDtypeStruct((t, D), F32), jax.ShapeDtypeStruct((t, D), MXU_DT),
                   jax.ShapeDtypeStruct((t, D), MXU_DT), jax.ShapeDtypeStruct((8, D), F32)),
        compiler_params=_params(48),
    )(xh3, rs3, g3, b3, p, w_g, w_ple, bg, g4, b4, target)


def _ffn_bwd(dr, gact, uact, xin, rsin, gin, w_in, w_out, prev_ln, name):
    t = dr.shape[0]
    tm = _tile(t, 256)
    chunks = _f_chunks()

    def body(dr_ref, g_ref, u_ref, x_ref, rs_ref, gi_ref, win_hbm, wout_hbm,
             df_ref, dg_ref, du_ref, dx_ref, st_ref, win_v, wout_v, acc_ref):
        @pl.when(pl.program_id(0) == 0)
        def _():
            pltpu.sync_copy(win_hbm, win_v)
            pltpu.sync_copy(wout_hbm, wout_v)
            st_ref[...] = jnp.zeros_like(st_ref)

        drv = dr_ref[...]
        dfb = (0.5 * drv).astype(MXU_DT)
        df_ref[...] = dfb
        for ci, (c0, fc) in enumerate(chunks):
            dh = _mm_nt(dfb, wout_v[c0:c0 + fc, :])
            g = g_ref[:, c0:c0 + fc].astype(F32)
            u = u_ref[:, c0:c0 + fc].astype(F32)
            sg = _sigmoid(g)
            dgb = (dh * u * (sg * (1.0 + g * (1.0 - sg)))).astype(MXU_DT)
            dub = (dh * (g * sg)).astype(MXU_DT)
            dg_ref[:, c0:c0 + fc] = dgb
            du_ref[:, c0:c0 + fc] = dub
            part = _mm_nt(dgb, win_v[:, c0:c0 + fc]) + _mm_nt(dub, win_v[:, F + c0:F + c0 + fc])
            if ci == 0:
                acc_ref[...] = part
            else:
                acc_ref[...] += part
        dx = ALPHA * drv + acc_ref[...]
        if prev_ln:
            xh = x_ref[...]
            st_ref[0:1, :] += _colsum(dx * xh)
            st_ref[1:2, :] += _colsum(dx)
            dx_ref[...] = _ln_bwd(dx * gi_ref[...], xh, rs_ref[...])
        else:
            dx_ref[...] = dx

    row = pl.BlockSpec((tm, D), lambda i: (i, 0))
    vec = pl.BlockSpec((1, D), lambda i: (0, 0))
    act = pl.BlockSpec((tm, F), lambda i: (i, 0))
    return pl.pallas_call(
        body, name=name, grid=(t // tm,),
        in_specs=[row, act, act, row, pl.BlockSpec((tm, 1), lambda i: (i, 0)), vec, ANY, ANY],
        out_specs=(row, act, act, row, pl.BlockSpec((8, D), lambda i: (0, 0))),
        out_shape=(jax.ShapeDtypeStruct((t, D), MXU_DT), jax.ShapeDtypeStruct((t, F), MXU_DT),
                   jax.ShapeDtypeStruct((t, F), MXU_DT), jax.ShapeDtypeStruct((t, D), F32),
                   jax.ShapeDtypeStruct((8, D), F32)),
        scratch_shapes=[pltpu.VMEM((D, 2 * F), MXU_DT), pltpu.VMEM((F, D), MXU_DT), pltpu.VMEM((tm, D), F32)],
        compiler_params=_params(52),
    )(dr, gact, uact, xin, rsin, gin, w_in, w_out)


def _mix_post_bwd(dr2, o, bch, conv_w, g_attn, g_conv, w_mo, name):
    t = dr2.shape[0]
    tm = _tile(t, 512)
    hb = tm // 8

    def body(dr_ref, o_ref, bch_ref, halo_ref, cw_ref, ga_ref, gc_ref, w_ref,
             dm_ref, do_ref, dl_ref, dy_ref, st_ref, ext):
        i = pl.program_id(0)

        @pl.when(i == 0)
        def _():
            st_ref[...] = jnp.zeros_like(st_ref)

        dmb = dr_ref[...].astype(MXU_DT)
        dm_ref[...] = dmb
        dmg = _mm_nt(dmb, w_ref[...])
        ov = o_ref[...]
        an, ra = _rms_fwd(ov)
        da = dmg[:, 0:DA]
        st_ref[0:1, :] += _colsum(da * an)
        dxa = _rms_bwd(da * ga_ref[...], an, ra)
        dor = dxa.astype(MXU_DT).astype(F32)
        dot = dor.T
        for h in range(NH):
            do_ref[h, 0:DH, :] = dot[DH * h:DH * (h + 1)].astype(do_ref.dtype)
            do_ref[h, DH:LANES, :] = jnp.zeros((LANES - DH, tm), do_ref.dtype)
        srow = lax.broadcasted_iota(jnp.int32, (8, DA), 0)
        scol = lax.broadcasted_iota(jnp.int32, (8, DA), 1)
        sel = jnp.where((scol // DH) == srow, 1.0, 0.0).astype(MXU_DT)
        hi, mid, lo = _split3(dor * ov)
        delta = _mm_nt(sel, hi) + _mm_nt(sel, mid) + _mm_nt(sel, lo)
        for h in range(NH):
            dl_ref[h] = delta[h:h + 1, :]
        bb, cc, hh = _conv_parts(bch_ref[...])
        _, hc, hh_h = _conv_parts(halo_ref[...])
        u = cc * hh
        ext[0:8, :] = jnp.where(i > 0, hc * hh_h, 0.0)
        ext[8:8 + tm, :] = u
        raw = cw_ref[0:1, :] * ext[6:6 + tm, :] + cw_ref[1:2, :] * ext[7:7 + tm, :] + cw_ref[2:3, :] * u
        cn, rc = _rms_fwd(bb * raw)
        dcn = dmg[:, DA:D]
        st_ref[1:2, :] += _colsum(dcn * cn)
        dy_ref[...] = _rms_bwd(dcn * gc_ref[...], cn, rc)

    row = lambda w: pl.BlockSpec((tm, w), lambda i: (i, 0))
    full = lambda a: pl.BlockSpec(a.shape, lambda i: (0, 0))
    return pl.pallas_call(
        body, name=name, grid=(t // tm,),
        in_specs=[row(D), row(DA), row(3 * DCV),
                  pl.BlockSpec((8, 3 * DCV), lambda i: (jnp.maximum(i * hb - 1, 0), 0)),
                  full(conv_w), full(g_attn), full(g_conv), full(w_mo)],
        out_specs=(row(D), pl.BlockSpec((NH, LANES, tm), lambda i: (0, 0, i)),
                   pl.BlockSpec((NH, 1, tm), lambda i: (0, 0, i)), row(DCV),
                   pl.BlockSpec((8, DA), lambda i: (0, 0))),
        out_shape=(jax.ShapeDtypeStruct((t, D), MXU_DT), jax.ShapeDtypeStruct((NH, LANES, t), MXU_DT),
                   jax.ShapeDtypeStruct((NH, 1, t), F32), jax.ShapeDtypeStruct((t, DCV), F32),
                   jax.ShapeDtypeStruct((8, DA), F32)),
        scratch_shapes=[pltpu.VMEM((tm + 8, DCV), F32)],
        compiler_params=_params(48),
    )(dr2, o, bch, bch, conv_w, g_attn, g_conv, w_mo)


def _conv_bwd(dy, bch, conv_w, name):
    t = dy.shape[0]
    tm = _tile(t, 512)
    hb = tm // 8
    nt = t // tm

    def body(dy_ref, dyn_ref, bch_ref, prev_ref, next_ref, cw_ref, out_ref, st_ref, ext_u, ext_d):
        i = pl.program_id(0)

        @pl.when(i == 0)
        def _():
            st_ref[...] = jnp.zeros_like(st_ref)

        bb, cc, hh = _conv_parts(bch_ref[...])
        _, pc, ph = _conv_parts(prev_ref[...])
        nb, _, _ = _conv_parts(next_ref[...])
        u = cc * hh
        ext_u[0:8, :] = jnp.where(i > 0, pc * ph, 0.0)
        ext_u[8:8 + tm, :] = u
        u1 = ext_u[7:7 + tm, :]
        u2 = ext_u[6:6 + tm, :]
        w0, w1, w2 = cw_ref[0:1, :], cw_ref[1:2, :], cw_ref[2:3, :]
        dyv = dy_ref[...]
        out_ref[:, 0:DCV] = (dyv * (w0 * u2 + w1 * u1 + w2 * u)).astype(out_ref.dtype)
        dcr = dyv * bb
        ext_d[0:tm, :] = dcr
        ext_d[tm:tm + 8, :] = jnp.where(i < nt - 1, dyn_ref[...] * nb, 0.0)
        du = w2 * dcr + w1 * ext_d[1:1 + tm, :] + w0 * ext_d[2:2 + tm, :]
        out_ref[:, DCV:2 * DCV] = (du * hh).astype(out_ref.dtype)
        out_ref[:, 2 * DCV:3 * DCV] = (du * cc).astype(out_ref.dtype)
        st_ref[0:1, :] += _colsum(dcr * u2)
        st_ref[1:2, :] += _colsum(dcr * u1)
        st_ref[2:3, :] += _colsum(dcr * u)

    row = lambda w: pl.BlockSpec((tm, w), lambda i: (i, 0))
    prev = lambda w: pl.BlockSpec((8, w), lambda i: (jnp.maximum(i * hb - 1, 0), 0))
    nxt = lambda w: pl.BlockSpec((8, w), lambda i: (jnp.minimum((i + 1) * hb, nt * hb - 1), 0))
    return pl.pallas_call(
        body, name=name, grid=(nt,),
        in_specs=[row(DCV), nxt(DCV), row(3 * DCV), prev(3 * DCV), nxt(3 * DCV),
                  pl.BlockSpec(conv_w.shape, lambda i: (0, 0))],
        out_specs=(row(3 * DCV), pl.BlockSpec((8, DCV), lambda i: (0, 0))),
        out_shape=(jax.ShapeDtypeStruct((t, 3 * DCV), MXU_DT), jax.ShapeDtypeStruct((8, DCV), F32)),
        scratch_shapes=[pltpu.VMEM((tm + 8, DCV), F32), pltpu.VMEM((tm + 8, DCV), F32)],
        compiler_params=_params(48),
    )(dy, dy, bch, bch, bch, conv_w)


def _attn_bwd(ka, kat, va, qat, dot, lrow, drow, r, name):
    t = ka.shape[1]
    tq = _tile(t, 512)
    nq = t // tq

    def body(r_ref, ka_ref, kat_ref, va_ref, l_ref, dl_ref, qat_hbm, dot_hbm,
             dk_ref, dv_ref, dck_ref, dqt_hbm, qat_v, dot_v, dq_acc):
        hp, j = pl.program_id(0), pl.program_id(1)

        @pl.when(j == 0)
        def _():
            pltpu.sync_copy(qat_hbm.at[pl.ds(2 * hp, 2)], qat_v)
            pltpu.sync_copy(dot_hbm.at[pl.ds(2 * hp, 2)], dot_v)
            dq_acc[...] = jnp.zeros_like(dq_acc)

        key = lax.broadcasted_iota(jnp.int32, (tq, tq), 0)
        qry = lax.broadcasted_iota(jnp.int32, (tq, tq), 1)

        def step(i, carry, masked):
            off = pl.multiple_of(i * tq, tq)
            out = []
            for a in range(2):
                dk, dv = carry[a]
                st = _mm(ka_ref[a], qat_v[a, :, pl.ds(off, tq)])
                dpt = _mm(va_ref[a], dot_v[a, :, pl.ds(off, tq)])
                if masked:
                    st = jnp.where(qry >= key, st, NEG)
                d = r_ref[2 * hp + a, i] - r_ref[2 * hp + a, j]
                pt = jnp.exp(st - (l_ref[a, :, pl.ds(off, tq)] - d))
                dsb = (pt * (dpt - dl_ref[a, :, pl.ds(off, tq)])).astype(MXU_DT)
                dv = dv + _mm_nt(dot_v[a, 0:DH, pl.ds(off, tq)], pt.astype(MXU_DT))
                dk = dk + _mm_nt(qat_v[a, 0:QROWS, pl.ds(off, tq)], dsb)
                dq_acc[a, :, pl.ds(off, tq)] += _mm(kat_ref[a], dsb)
                out.append((dk, dv))
            return tuple(out)

        init = tuple((jnp.zeros((QROWS, tq), F32), jnp.zeros((DH, tq), F32)) for _ in range(2))
        carry = step(j, init, True)
        (dka, dva), (dkb, dvb) = lax.fori_loop(j + 1, nq, lambda i, cr: step(i, cr, False), carry)
        dk_ref[...] = jnp.concatenate([dka[0:DH], dkb[0:DH]], axis=0).T.astype(dk_ref.dtype)
        dv_ref[...] = jnp.concatenate([dva, dvb], axis=0).T.astype(dv_ref.dtype)
        dck_ref[0] = -dka[DH + 3:DH + 4, :]
        dck_ref[1] = -dkb[DH + 3:DH + 4, :]

        @pl.when(j == nq - 1)
        def _():
            pltpu.sync_copy(dq_acc, dqt_hbm.at[pl.ds(2 * hp, 2)])

    pair = lambda rows, cols: pl.BlockSpec((2, rows, cols), lambda p, j: (p, 0, 0))
    return pl.pallas_call(
        body, name=name, grid=(NH // 2, nq),
        in_specs=[pl.BlockSpec(memory_space=pltpu.SMEM),
                  pl.BlockSpec((2, tq, LANES), lambda p, j: (p, j, 0)),
                  pl.BlockSpec((2, QROWS, tq), lambda p, j: (p, 0, j)),
                  pl.BlockSpec((2, tq, LANES), lambda p, j: (p, j, 0)),
                  pair(1, t), pair(1, t), ANY, ANY],
        out_specs=(pl.BlockSpec((tq, LANES), lambda p, j: (j, p)),
                   pl.BlockSpec((tq, LANES), lambda p, j: (j, p)),
                   pl.BlockSpec((2, 1, tq), lambda p, j: (p, 0, j)), ANY),
        out_shape=(jax.ShapeDtypeStruct((t, DA), MXU_DT), jax.ShapeDtypeStruct((t, DA), MXU_DT),
                   jax.ShapeDtypeStruct((NH, 1, t), F32), jax.ShapeDtypeStruct((NH, QROWS, t), F32)),
        scratch_shapes=[pltpu.VMEM((2, LANES, t), MXU_DT), pltpu.VMEM((2, LANES, t), MXU_DT),
                        pltpu.VMEM((2, QROWS, t), F32)],
        compiler_params=_params(52, 2),
    )(r, ka, kat, va, lrow, drow, qat, dot)


def _mix_proj_bwd(dr2, dqt, dk, dv, dbch, dc, z, xh1, rs1, g1, w_qkv, w_bch, w_f, name):
    t = dr2.shape[0]
    tm = _tile(t, 512)
    nt = t // tm

    def body(dr_ref, dqt_ref, dk_ref, dv_ref, db_ref, dc_ref, z_ref, x_ref, rs_ref, g_ref,
             wq_ref, wb_ref, wf_ref, out_ref, df_ref, dq_ref, st_ref, carry):
        @pl.when(pl.program_id(0) == 0)
        def _():
            carry[...] = jnp.zeros_like(carry)
            st_ref[...] = jnp.zeros_like(st_ref)

        dq_ref[...] = (jnp.concatenate([dqt_ref[h, 0:DH, :] for h in range(NH)], axis=0).T * 0.125
                       ).astype(dq_ref.dtype)

        row = lax.broadcasted_iota(jnp.int32, (tm, tm), 0)
        col = lax.broadcasted_iota(jnp.int32, (tm, tm), 1)
        triu = jnp.where(col >= row, 1.0, 0.0).astype(MXU_DT)
        dlogf = carry[...] + _mm_sel(triu, dc_ref[...])
        carry[...] = dlogf[0:1, :]
        dz = dlogf / (1.0 + jnp.exp(z_ref[...]))
        st_ref[2:3, 0:LANES] += _colsum(dz)
        dfb = dz.astype(MXU_DT)
        df_ref[...] = dfb
        dx = (ALPHA * dr_ref[...]
              + _mm_nt(dq_ref[...], wq_ref[:, 0:DA])
              + _mm_nt(dk_ref[...], wq_ref[:, DA:2 * DA])
              + _mm_nt(dv_ref[...], wq_ref[:, 2 * DA:3 * DA])
              + _mm_nt(db_ref[...], wb_ref[...])
              + _mm_nt(dfb, wf_ref[...]))
        xh = x_ref[...]
        st_ref[0:1, :] += _colsum(dx * xh)
        st_ref[1:2, :] += _colsum(dx)
        out_ref[...] = _ln_bwd(dx * g_ref[...], xh, rs_ref[...])

    row = lambda w: pl.BlockSpec((tm, w), lambda i: (nt - 1 - i, 0))
    full = lambda a: pl.BlockSpec(a.shape, lambda i: (0, 0))
    return pl.pallas_call(
        body, name=name, grid=(nt,),
        in_specs=[row(D), pl.BlockSpec((NH, QROWS, tm), lambda i: (0, 0, nt - 1 - i)), row(DA), row(DA),
                  row(3 * DCV), row(LANES), row(LANES), row(D), row(1),
                  full(g1), full(w_qkv), full(w_bch), full(w_f)],
        out_specs=(row(D), row(LANES), row(DA), pl.BlockSpec((8, D), lambda i: (0, 0))),
        out_shape=(jax.ShapeDtypeStruct((t, D), F32), jax.ShapeDtypeStruct((t, LANES), MXU_DT),
                   jax.ShapeDtypeStruct((t, DA), MXU_DT), jax.ShapeDtypeStruct((8, D), F32)),
        scratch_shapes=[pltpu.VMEM((1, LANES), F32)],
        compiler_params=_params(48),
    )(dr2, dqt, dk, dv, dbch, dc, z, xh1, rs1, g1, w_qkv, w_bch, w_f)


def _dw(mode, a_parts, b, m, n, name, tmm=None, tn=None):
    t = b.shape[0]
    tmm = tmm or m
    tn = tn or n
    tt = _tile(t, 1024)

    def body(*refs):
        a_refs, b_ref, o_ref = refs[:len(a_parts)], refs[len(a_parts)], refs[len(a_parts) + 1]

        @pl.when(pl.program_id(2) == 0)
        def _():
            o_ref[...] = jnp.zeros_like(o_ref)

        if mode == "plain":
            a = a_refs[0][...].astype(MXU_DT)
        elif mode == "affine":
            a = (a_refs[0][...] * a_refs[1][...] + a_refs[2][...]).astype(MXU_DT)
        else:
            g = a_refs[0][...].astype(F32)
            a = (g * _sigmoid(g) * a_refs[1][...].astype(F32)).astype(MXU_DT)
        o_ref[...] += _mm_tn(a, b_ref[...].astype(MXU_DT))

    a_tile = pl.BlockSpec((tt, tmm), lambda i, j, k: (k, i))
    a_vec = pl.BlockSpec((1, tmm), lambda i, j, k: (0, i))
    a_specs = {"plain": [a_tile], "affine": [a_tile, a_vec, a_vec], "swiglu": [a_tile, a_tile]}[mode]
    return pl.pallas_call(
        body, name=name, grid=(m // tmm, n // tn, t // tt),
        in_specs=a_specs + [pl.BlockSpec((tt, tn), lambda i, j, k: (k, j))],
        out_specs=pl.BlockSpec((tmm, tn), lambda i, j, k: (i, j)),
        out_shape=jax.ShapeDtypeStruct((m, n), F32),
        compiler_params=_params(48, 3),
    )(*a_parts, b)


def _adamw(w, g, m, v):
    m = ADAM_B1 * m + (1.0 - ADAM_B1) * g
    v = ADAM_B2 * v + (1.0 - ADAM_B2) * (g * g)
    m_hat = m / (1.0 - ADAM_B1 ** ADAM_STEP)
    v_hat = v / (1.0 - ADAM_B2 ** ADAM_STEP)
    delta = -ADAM_LR * (m_hat / (jnp.sqrt(v_hat) + ADAM_EPS) + ADAM_WD * w)
    return delta, m, v


def _reduce_adamw(landed, own, w, m, v, name):
    r, c = own.shape
    tr = _tile(r, 128)

    def body(l_ref, o_ref, w_ref, m_ref, v_ref, g_out, d_out, m_out, v_out):
        me = 4 * lax.axis_index("x") + 2 * lax.axis_index("y") + lax.axis_index("c")
        g = None
        for j in range(NDEV):
            term = jnp.where(me == j, o_ref[...], l_ref[j].astype(F32))
            g = term if g is None else g + term
        g_out[...] = g
        d_out[...], m_out[...], v_out[...] = _adamw(w_ref[...], g, m_ref[...], v_ref[...])

    blk = pl.BlockSpec((tr, c), lambda i: (i, 0))
    sds = jax.ShapeDtypeStruct((r, c), F32)
    return pl.pallas_call(
        body, name=name, grid=(r // tr,),
        in_specs=[pl.BlockSpec((NDEV, tr, c), lambda i: (0, i, 0)), blk, blk, blk, blk],
        out_specs=(blk, blk, blk, blk), out_shape=(sds, sds, sds, sds),
        compiler_params=_params(40),
    )(landed, own, w, m, v)


def _sum_small(gathered, name):
    _, r, c = gathered.shape

    def body(g_ref, o_ref):
        acc = g_ref[0]
        for j in range(1, NDEV):
            acc = acc + g_ref[j]
        o_ref[...] = acc

    return pl.pallas_call(body, name=name, out_shape=jax.ShapeDtypeStruct((r, c), F32))(gathered)


def _adamw_small(g, w, m, v, name):
    def body(g_ref, w_ref, m_ref, v_ref, d_out, m_out, v_out):
        d_out[...], m_out[...], v_out[...] = _adamw(w_ref[...], g_ref[...], m_ref[...], v_ref[...])

    sds = jax.ShapeDtypeStruct(g.shape, F32)
    return pl.pallas_call(body, name=name, out_shape=(sds, sds, sds))(g, w, m, v)


def _cols_from_stack(s):
    return jnp.transpose(s, (1, 0, 2)).reshape(s.shape[1], NDEV * s.shape[2])


def _cols_to_stack(w):
    r, c = w.shape
    return jnp.transpose(w.reshape(r, NDEV, c // NDEV), (1, 0, 2))


def _rows_from_stack(s):
    return s.reshape(NDEV * s.shape[1], s.shape[2])


def _rows_to_stack(w):
    r, c = w.shape
    return w.reshape(NDEV, r // NDEV, c)


SMALL_ROWS = 16
SMALL_SLOTS = {
    "ln1_g": (0, 0, D), "ln1_b": (1, 0, D), "ln2_g": (2, 0, D), "ln2_b": (3, 0, D), "ln3_g": (4, 0, D),
    "ln3_b": (5, 0, D), "b_ple_gate": (6, 0, D), "ln4_g": (7, 0, D), "ln4_b": (8, 0, D),
    "g_attn": (9, 0, DA), "g_conv": (9, DA, DCV), "b_forget": (10, 0, NH),
}
CONVW_ROW = 11
LOSS_SLOT = (10, LANES)


def _pack_small(vals, conv_rows, loss=None):
    out = jnp.zeros((SMALL_ROWS, D), F32)
    for nm, (r, off, wd) in SMALL_SLOTS.items():
        out = out.at[r:r + 1, off:off + wd].set(vals[nm].reshape(1, wd).astype(F32))
    out = out.at[CONVW_ROW:CONVW_ROW + 3, 0:conv_rows.shape[1]].set(conv_rows.astype(F32))
    if loss is not None:
        out = out.at[LOSS_SLOT[0], LOSS_SLOT[1]].set(loss)
    return out


def _unpack_small(packed, name):
    r, off, wd = SMALL_SLOTS[name]
    return packed[r:r + 1, off:off + wd]


def kernel(x, p, ffn1_w_in, ffn1_w_out, ln1_g, ln1_b, w_mix_in, b_forget, conv_w, g_attn, g_conv, w_mix_out, ln2_g, ln2_b, ffn2_w_in, ffn2_w_out, ln3_g, ln3_b, w_ple, w_ple_gate, b_ple_gate, ln4_g, ln4_b, loss_target, m_ffn1_w_in, m_ffn1_w_out, m_ln1_g, m_ln1_b, m_w_mix_in, m_b_forget, m_conv_w, m_g_attn, m_g_conv, m_w_mix_out, m_ln2_g, m_ln2_b, m_ffn2_w_in, m_ffn2_w_out, m_ln3_g, m_ln3_b, m_w_ple, m_w_ple_gate, m_b_ple_gate, m_ln4_g, m_ln4_b, v_ffn1_w_in, v_ffn1_w_out, v_ln1_g, v_ln1_b, v_w_mix_in, v_b_forget, v_conv_w, v_g_attn, v_g_conv, v_w_mix_out, v_ln2_g, v_ln2_b, v_ffn2_w_in, v_ffn2_w_out, v_ln3_g, v_ln3_b, v_w_ple, v_w_ple_gate, v_b_ple_gate, v_ln4_g, v_ln4_b):
    args = dict(locals())
    t = x.shape[1]
    me = 4 * lax.axis_index("x") + 2 * lax.axis_index("y") + lax.axis_index("c")
    x0 = x.reshape(t, D)
    p0 = p.reshape(t, PLE)
    tgt = loss_target.reshape(t, D)

    big = ["ffn1_w_in", "ffn1_w_out", "w_mix_in", "w_mix_out", "ffn2_w_in", "ffn2_w_out", "w_ple", "w_ple_gate"]
    col_sharded = {"ffn1_w_in", "w_mix_in", "ffn2_w_in", "w_ple"}
    shard = {nm: args[nm][0] for nm in big}

    gathered = _allgather([shard[nm].astype(WIRE_DT) for nm in big] + [conv_w[0]], "ag_weights")
    full = {nm: (_cols_from_stack(g) if nm in col_sharded else _rows_from_stack(g)).astype(MXU_DT)
            for nm, g in zip(big, gathered[:len(big)])}
    cw = _cols_from_stack(gathered[len(big)])
    gx, gw, small, loss_part = _local_step(x0, p0, tgt, full, cw, {nm: args[nm] for nm in SMALL_SLOTS})

    stacks = [(_cols_to_stack(gw[nm]) if nm in col_sharded else _rows_to_stack(gw[nm])) for nm in big]
    landed = _exchange([s.astype(WIRE_DT) for s in stacks], "rs_grads")
    small_part = _pack_small({nm: small[nm] for nm in SMALL_SLOTS},
                             jnp.pad(small["conv_w"], ((0, 0), (0, D - DCV))), loss_part)
    small_all = _allgather([small_part], "ag_small")[0]
    small_g = _sum_small(small_all, "sum_small")
    loss = small_g[LOSS_SLOT[0], LOSS_SLOT[1]]

    outs = {"loss": loss, "grad_x": gx.reshape(1, t, D)}
    for nm, st, ld in zip(big, stacks, landed):
        own = lax.dynamic_index_in_dim(st, me, axis=0, keepdims=False)
        g, dl, mn, vn = _reduce_adamw(ld, own, shard[nm], args["m_" + nm][0], args["v_" + nm][0], "adamw_" + nm)
        outs["grad_" + nm], outs["delta_" + nm], outs["new_m_" + nm], outs["new_v_" + nm] = (
            g[None], dl[None], mn[None], vn[None])
    small_names = list(SMALL_SLOTS)
    cshard = lax.dynamic_slice_in_dim(small_g[CONVW_ROW:CONVW_ROW + 3, 0:DCV], me * (DCV // NDEV), DCV // NDEV, axis=1)
    g_pack = _pack_small({nm: _unpack_small(small_g, nm) for nm in small_names}, cshard)
    packs = [_pack_small({nm: args[pre + nm] for nm in small_names}, args[pre + "conv_w"][0])
             for pre in ("", "m_", "v_")]
    d_pack, m_pack, v_pack = _adamw_small(g_pack, packs[0], packs[1], packs[2], "adamw_small")
    for key, pk in (("grad_", g_pack), ("delta_", d_pack), ("new_m_", m_pack), ("new_v_", v_pack)):
        for nm in small_names:
            outs[key + nm] = _unpack_small(pk, nm)
        outs[key + "conv_w"] = pk[CONVW_ROW:CONVW_ROW + 3, 0:DCV // NDEV][None]

    wnames = ["ffn1_w_in", "ffn1_w_out", "ln1_g", "ln1_b", "w_mix_in", "b_forget", "conv_w", "g_attn", "g_conv",
              "w_mix_out", "ln2_g", "ln2_b", "ffn2_w_in", "ffn2_w_out", "ln3_g", "ln3_b", "w_ple", "w_ple_gate",
              "b_ple_gate", "ln4_g", "ln4_b"]
    return (outs["loss"], outs["grad_x"], *[outs[pre + nm] for pre in ("grad_", "delta_", "new_m_", "new_v_")
                                            for nm in wnames])


def _local_step(x0, p0, tgt, full, cw, sp):
    t = x0.shape[0]
    ln1_g, ln1_b, ln2_g, ln2_b, ln3_g, ln3_b = (sp[k] for k in ("ln1_g", "ln1_b", "ln2_g", "ln2_b", "ln3_g", "ln3_b"))
    ln4_g, ln4_b, g_attn, g_conv, b_ple_gate = (sp[k] for k in ("ln4_g", "ln4_b", "g_attn", "g_conv", "b_ple_gate"))
    wmi = full["w_mix_in"]
    w_qkv = wmi[:, 0:3 * DA]
    w_f = jnp.pad(wmi[:, 3 * DA:3 * DA + NH], ((0, 0), (0, LANES - NH)))
    w_bch = wmi[:, 3 * DA + NH:]
    bf_pad = jnp.pad(sp["b_forget"], ((0, 0), (0, LANES - NH)))
    ones = jnp.ones((1, D), F32)
    zeros = jnp.zeros((1, D), F32)

    g1a, u1a, xh1, rs1 = _ffn_fwd(x0, ones, zeros, full["ffn1_w_in"], full["ffn1_w_out"], "ffn1_fwd")
    ka, va, qat, kat, vta, bch, z, rt = _mix_proj_fwd(xh1, ln1_g, ln1_b, w_qkv[:, DA:], jnp.transpose(w_qkv),
                                                      w_bch, w_f, bf_pad, "mix_proj_fwd")
    rtile = jnp.transpose(rt[:, 0, 0:NH])
    o, lse = _attn_fwd(qat, ka, vta, rtile, "attn_fwd")
    merged, xh2, rs2 = _mix_post_fwd(o, bch, cw, g_attn, g_conv, xh1, ln1_g, ln1_b, full["w_mix_out"],
                                     "mix_post_fwd")
    g2a, u2a, xh3, rs3 = _ffn_fwd(xh2, ln2_g, ln2_b, full["ffn2_w_in"], full["ffn2_w_out"], "ffn2_fwd")

    dr3, dz, de, st_tail = _tail(xh3, rs3, ln3_g, ln3_b, p0, full["w_ple_gate"], full["w_ple"], b_ple_gate,
                                 ln4_g, ln4_b, tgt, "tail")
    df2, dg2, du2, dr2, st_f2 = _ffn_bwd(dr3, g2a, u2a, xh2, rs2, ln2_g, full["ffn2_w_in"], full["ffn2_w_out"],
                                         True, "ffn2_bwd")
    dmix, dot, drow, dyc, st_post = _mix_post_bwd(dr2, o, bch, cw, g_attn, g_conv, full["w_mix_out"],
                                                  "mix_post_bwd")
    dbch, st_conv = _conv_bwd(dyc, bch, cw, "conv_bwd")
    dk, dv, dck, dqt = _attn_bwd(ka, kat, va, qat, dot, lse, drow, rtile, "attn_bwd")
    dc_pad = jnp.pad(jnp.transpose(dqt[:, DH, :] + dck.reshape(NH, t)), ((0, 0), (0, LANES - NH)))
    dr1, dfl, dq, st_proj = _mix_proj_bwd(dr2, dqt, dk, dv, dbch, dc_pad, z, xh1, rs1, ln1_g, w_qkv, w_bch, w_f,
                                          "mix_proj_bwd")
    df1, dg1, du1, gx, _ = _ffn_bwd(dr1, g1a, u1a, x0, rs1, ones, full["ffn1_w_in"], full["ffn1_w_out"],
                                    False, "ffn1_bwd")

    x1p, x2p, x3p = (xh1, ln1_g, ln1_b), (xh2, ln2_g, ln2_b), (xh3, ln3_g, ln3_b)
    gw = {}
    gw["ffn1_w_in"] = jnp.concatenate(
        [_dw("affine", (x0, ones, zeros), dg1, D, F, "dw_ffn1_in_g", tn=F // 2),
         _dw("affine", (x0, ones, zeros), du1, D, F, "dw_ffn1_in_u", tn=F // 2)], axis=1)
    gw["ffn1_w_out"] = _dw("swiglu", (g1a, u1a), df1, F, D, "dw_ffn1_out", tmm=F // 2)
    gw["ffn2_w_in"] = jnp.concatenate(
        [_dw("affine", x2p, dg2, D, F, "dw_ffn2_in_g", tn=F // 2),
         _dw("affine", x2p, du2, D, F, "dw_ffn2_in_u", tn=F // 2)], axis=1)
    gw["ffn2_w_out"] = _dw("swiglu", (g2a, u2a), df2, F, D, "dw_ffn2_out", tmm=F // 2)
    gw["w_mix_out"] = _dw("plain", (merged,), dmix, D, D, "dw_mix_out")
    gw["w_mix_in"] = jnp.concatenate(
        [_dw("affine", x1p, dq, D, DA, "dw_mix_in_q"), _dw("affine", x1p, dk, D, DA, "dw_mix_in_k"),
         _dw("affine", x1p, dv, D, DA, "dw_mix_in_v"),
         _dw("affine", x1p, dfl, D, LANES, "dw_mix_in_f")[:, 0:NH],
         _dw("affine", x1p, dbch, D, 3 * DCV, "dw_mix_in_bch")], axis=1)
    gw["w_ple_gate"] = _dw("affine", x3p, dz, D, D, "dw_ple_gate")
    gw["w_ple"] = _dw("plain", (p0,), de, PLE, D, "dw_ple")

    loss_part = (0.5 / D) * jnp.sum(st_tail[5:6, :])
    small = {"ln1_g": st_proj[0:1], "ln1_b": st_proj[1:2], "ln2_g": st_f2[0:1], "ln2_b": st_f2[1:2],
             "ln3_g": st_tail[3:4], "ln3_b": st_tail[4:5], "b_ple_gate": st_tail[2:3], "ln4_g": st_tail[0:1],
             "ln4_b": st_tail[1:2], "g_attn": st_post[0:1], "g_conv": st_post[1:2],
             "b_forget": st_proj[2:3, 0:NH], "conv_w": st_conv[0:3]}
    return gx, gw, small, loss_part
```

```python
import functools
import math

import jax
import jax.numpy as jnp
from jax import lax
from jax.experimental import pallas as pl
from jax.experimental.pallas import tpu as pltpu

D = 1024
F = 2816
NH = 8
DH = 64
DA = NH * DH
DCV = D - DA
PLE = 256
NPROJ = 3 * DA + NH + 3 * DCV
LN_EPS = 1e-5
RMS_EPS = 1e-6
NEG = -1e30
ALPHA = 2.0 ** 0.25
NDEV = 8
LANES = 128

ADAM_LR, ADAM_B1, ADAM_B2, ADAM_EPS, ADAM_WD, ADAM_STEP = 0.001, 0.9, 0.999, 1e-08, 0.01, 10

F32 = jnp.float32
MXU_DT = jnp.bfloat16
WIRE_DT = jnp.bfloat16

MESH_ID = pl.DeviceIdType.MESH
ANY = pl.BlockSpec(memory_space=pl.ANY)


def _params(vmem_mb, n_axes=1):
    return pltpu.CompilerParams(dimension_semantics=("arbitrary",) * n_axes,
                                vmem_limit_bytes=int(vmem_mb) << 20)


def _mm(a, b):
    return jnp.dot(a, b, preferred_element_type=F32)


def _mm_nt(a, b):
    return lax.dot_general(a, b, (((1,), (1,)), ((), ())), preferred_element_type=F32)


def _mm_tn(a, b):
    return lax.dot_general(a, b, (((0,), (0,)), ((), ())), preferred_element_type=F32)


def _split3(x):
    hi = x.astype(MXU_DT)
    r1 = x - hi.astype(F32)
    mid = r1.astype(MXU_DT)
    lo = (r1 - mid.astype(F32)).astype(MXU_DT)
    return hi, mid, lo


def _mm_sel(sel, x):
    hi, mid, lo = _split3(x)
    return _mm(sel, hi) + _mm(sel, mid) + _mm(sel, lo)


def _mm_xsel(x, sel):
    hi, mid, lo = _split3(x)
    return _mm(hi, sel) + _mm(mid, sel) + _mm(lo, sel)


def _sigmoid(x):
    return 1.0 / (1.0 + jnp.exp(-x))


def _ln_fwd(r):
    mu = jnp.mean(r, axis=-1, keepdims=True)
    xc = r - mu
    var = jnp.mean(xc * xc, axis=-1, keepdims=True)
    rstd = lax.rsqrt(var + LN_EPS)
    return xc * rstd, rstd


def _ln_bwd(dxhat, xhat, rstd):
    m1 = jnp.mean(dxhat, axis=-1, keepdims=True)
    m2 = jnp.mean(dxhat * xhat, axis=-1, keepdims=True)
    return rstd * (dxhat - m1 - xhat * m2)


def _rms_fwd(x):
    r = lax.rsqrt(jnp.mean(x * x, axis=-1, keepdims=True) + RMS_EPS)
    return x * r, r


def _rms_bwd(dyg, xn, r):
    return r * (dyg - xn * jnp.mean(dyg * xn, axis=-1, keepdims=True))


def _colsum(x):
    return jnp.sum(x, axis=0, keepdims=True)


def _f_chunks():
    out, c0 = [], 0
    while c0 < F:
        fc = min(512, F - c0)
        out.append((c0, fc))
        c0 += fc
    return out


def _tile(t, want):
    return want if t % want == 0 and t >= want else t


def _exchange(arrs, name):
    n = len(arrs)
    out_shape = [jax.ShapeDtypeStruct(a.shape, a.dtype) for a in arrs]

    def body(*refs):
        ins, outs = refs[:n], refs[n:2 * n]
        send_sems, recv_sems = refs[2 * n:]
        x, y, c = lax.axis_index("x"), lax.axis_index("y"), lax.axis_index("c")
        me = 4 * x + 2 * y + c
        peers = []
        for k in range(1, NDEV):
            px = 1 - x if (k >> 2) & 1 else x
            py = 1 - y if (k >> 1) & 1 else y
            pc = 1 - c if k & 1 else c
            peers.append(((px, py, pc), 4 * px + 2 * py + pc))

        def remote(w, k, slot_src, slot_dst):
            return pltpu.make_async_remote_copy(
                src_ref=ins[w].at[slot_src], dst_ref=outs[w].at[slot_dst],
                send_sem=send_sems.at[w * (NDEV - 1) + k], recv_sem=recv_sems.at[w * (NDEV - 1) + k],
                device_id=peers[k][0], device_id_type=MESH_ID)

        for k in range(NDEV - 1):
            for w in range(n):
                remote(w, k, peers[k][1], me).start()
        for k in range(NDEV - 1):
            for w in range(n):
                remote(w, k, me, peers[k][1]).wait_recv()
        for k in range(NDEV - 1):
            for w in range(n):
                remote(w, k, peers[k][1], me).wait_send()

    return pl.pallas_call(
        body, name=name, out_shape=tuple(out_shape),
        in_specs=[ANY] * n, out_specs=tuple([ANY] * n),
        scratch_shapes=[pltpu.SemaphoreType.DMA((n * (NDEV - 1),)), pltpu.SemaphoreType.DMA((n * (NDEV - 1),))],
    )(*arrs)


def _allgather(arrs, name):
    n = len(arrs)
    per = NDEV - 1

    def body(*refs):
        ins, outs = refs[:n], refs[n:2 * n]
        send_sems, recv_sems, loc_sems = refs[2 * n:]
        x, y, c = lax.axis_index("x"), lax.axis_index("y"), lax.axis_index("c")
        me, sib = (x, y, c), (x, y, 1 - c)
        chips = [(1 - x, y), (x, 1 - y), (1 - x, 1 - y)]

        def copy(w, k, block, to, src=None):
            dst = outs[w].at[4 * block[0] + 2 * block[1] + block[2]]
            return pltpu.make_async_remote_copy(
                src_ref=dst if src is None else src, dst_ref=dst,
                send_sem=send_sems.at[w * per + k], recv_sem=recv_sems.at[w * per + k],
                device_id=to, device_id_type=MESH_ID)

        local = [pltpu.make_async_copy(ins[w], outs[w].at[4 * x + 2 * y + c], loc_sems.at[w]) for w in range(n)]
        for lc in local:
            lc.start()
        sent = []
        for j, chip in enumerate(chips):
            sent += [copy(w, 1 + j, me, (*chip, c), src=ins[w]) for w in range(n)]
        sent += [copy(w, 0, me, sib, src=ins[w]) for w in range(n)]
        for cp in sent:
            cp.start()
        for j, chip in enumerate(chips):
            for w in range(n):
                copy(w, 1 + j, (*chip, c), me).wait_recv()
                fwd = copy(w, 4 + j, (*chip, c), sib)
                fwd.start()
                sent.append(fwd)
        for w in range(n):
            copy(w, 0, sib, me).wait_recv()
        for j, chip in enumerate(chips):
            for w in range(n):
                copy(w, 4 + j, (*chip, 1 - c), me).wait_recv()
        for cp in sent:
            cp.wait_send()
        for lc in local:
            lc.wait()

    return pl.pallas_call(
        body, name=name, out_shape=tuple(jax.ShapeDtypeStruct((NDEV,) + a.shape, a.dtype) for a in arrs),
        in_specs=[ANY] * n, out_specs=tuple([ANY] * n),
        scratch_shapes=[pltpu.SemaphoreType.DMA((n * per,)), pltpu.SemaphoreType.DMA((n * per,)),
                        pltpu.SemaphoreType.DMA((n,))],
    )(*arrs)


def _ffn_fwd(xin, gin, bin_, w_in, w_out, name):
    t = xin.shape[0]
    tm = _tile(t, 512)
    chunks = _f_chunks()

    def body(x_ref, gi_ref, bi_ref, win_hbm, wout_hbm, g_ref, u_ref, xh_ref, rs_ref, win_v, wout_v, acc_ref):
        @pl.when(pl.program_id(0) == 0)
        def _():
            pltpu.sync_copy(win_hbm, win_v)
            pltpu.sync_copy(wout_hbm, wout_v)

        x = x_ref[...] * gi_ref[...] + bi_ref[...]
        xb = x.astype(MXU_DT)
        for ci, (c0, fc) in enumerate(chunks):
            gc = _mm(xb, win_v[:, c0:c0 + fc])
            uc = _mm(xb, win_v[:, F + c0:F + c0 + fc])
            g_ref[:, c0:c0 + fc] = gc.astype(g_ref.dtype)
            u_ref[:, c0:c0 + fc] = uc.astype(u_ref.dtype)
            hc = (gc * _sigmoid(gc) * uc).astype(MXU_DT)
            part = _mm(hc, wout_v[c0:c0 + fc, :])
            if ci == 0:
                acc_ref[...] = part
            else:
                acc_ref[...] += part
        xh, rstd = _ln_fwd(ALPHA * x + 0.5 * acc_ref[...])
        xh_ref[...] = xh
        rs_ref[...] = rstd

    row = pl.BlockSpec((tm, D), lambda i: (i, 0))
    vec = pl.BlockSpec((1, D), lambda i: (0, 0))
    act = pl.BlockSpec((tm, F), lambda i: (i, 0))
    return pl.pallas_call(
        body, name=name, grid=(t // tm,),
        in_specs=[row, vec, vec, ANY, ANY],
        out_specs=(act, act, row, pl.BlockSpec((tm, 1), lambda i: (i, 0))),
        out_shape=(jax.ShapeDtypeStruct((t, F), MXU_DT), jax.ShapeDtypeStruct((t, F), MXU_DT),
                   jax.ShapeDtypeStruct((t, D), F32), jax.ShapeDtypeStruct((t, 1), F32)),
        scratch_shapes=[pltpu.VMEM((D, 2 * F), MXU_DT), pltpu.VMEM((F, D), MXU_DT), pltpu.VMEM((tm, D), F32)],
        compiler_params=_params(52),
    )(xin, gin, bin_, w_in, w_out)


QROWS = 80
BIAS_AT = DH


def _place_matrices():
    import numpy as np
    pk = np.zeros((NH, LANES, LANES), np.float32)
    pqt = np.zeros((NH, LANES, LANES), np.float32)
    for h in range(NH):
        for piece in range(3):
            pk[h, 8 * piece + h, BIAS_AT + 3 + piece] = -1.0
            pqt[h, BIAS_AT + piece, 8 * piece + h] = 1.0
    pkt = np.transpose(pk, (0, 2, 1))
    return tuple(jnp.asarray(m, MXU_DT) for m in (pk, pqt, pkt))


def _mix_proj_fwd(xh1, g1, b1, w_kv, w_qkv_t, w_bch, w_f, bf_pad, name):
    t = xh1.shape[0]
    tm = _tile(t, 512)
    pk, pqt, pkt = _place_matrices()

    def body(x_ref, g_ref, b_ref, wkv_ref, wt_ref, wb_ref, wf_ref, bf_ref, pk_ref, pqt_ref, pkt_ref,
             ka_ref, va_ref, qat_ref, kat_ref, vta_ref, bch_ref, z_ref, r_ref, carry):
        @pl.when(pl.program_id(0) == 0)
        def _():
            carry[...] = jnp.zeros_like(carry)

        xb = (x_ref[...] * g_ref[...] + b_ref[...]).astype(MXU_DT)
        kv = _mm(xb, wkv_ref[...])
        qkvt = _mm_nt(wt_ref[...], xb)
        bch_ref[...] = _mm(xb, wb_ref[...])
        z = _mm(xb, wf_ref[...]) + bf_ref[...]
        z_ref[...] = z
        logf = jnp.minimum(z, 0.0) - jnp.log(1.0 + jnp.exp(-jnp.abs(z)))
        row = lax.broadcasted_iota(jnp.int32, (tm, tm), 0)
        col = lax.broadcasted_iota(jnp.int32, (tm, tm), 1)
        tri = jnp.where(row >= col, 1.0, 0.0).astype(MXU_DT)
        c = carry[...] + _mm_sel(tri, logf)
        carry[...] = c[tm - 1:tm, :]
        r_ref[0] = c[0:1, :]
        lane = lax.broadcasted_iota(jnp.int32, (1, LANES), 1)
        hi, mid, lo = _split3(jnp.where(lane < NH, c - c[0:1, :], 0.0))
        pieces = (hi.astype(F32) + pltpu.roll(mid.astype(F32), 8, 1) + pltpu.roll(lo.astype(F32), 16, 1)
                  ).astype(MXU_DT)
        sub = lax.broadcasted_iota(jnp.int32, (DH, 1), 0)
        ones_k_lanes = jnp.where((lane >= BIAS_AT) & (lane < BIAS_AT + 3), 1.0, 0.0)
        ones_q_rows = jnp.where((sub >= 3) & (sub < 6), 1.0, 0.0)
        ones_k_rows = jnp.where(sub[0:QROWS - DH] < 3, 1.0, 0.0)
        first_row = jnp.where(sub == 0, 1.0, 0.0) + jnp.zeros((DH, tm), F32)
        for h in range(NH):
            pair, odd = divmod(h, 2)
            k2 = kv[:, LANES * pair:LANES * (pair + 1)]
            v2 = kv[:, DA + LANES * pair:DA + LANES * (pair + 1)]
            if odd:
                k2, v2 = pltpu.roll(k2, DH, 1), pltpu.roll(v2, DH, 1)
            ka_ref[h] = jnp.where(lane < DH, k2, _mm(pieces, pk_ref[h]) + ones_k_lanes).astype(ka_ref.dtype)
            va_ref[h] = jnp.where(lane < DH, v2, 0.0).astype(va_ref.dtype)
            qat_ref[h, 0:DH, :] = (qkvt[DH * h:DH * (h + 1)] * 0.125).astype(qat_ref.dtype)
            qat_ref[h, DH:LANES, :] = (_mm_nt(pqt_ref[h], pieces)[DH:LANES] + ones_q_rows).astype(qat_ref.dtype)
            kat_ref[h, 0:DH, :] = qkvt[DA + DH * h:DA + DH * (h + 1)].astype(kat_ref.dtype)
            kat_ref[h, DH:QROWS, :] = (_mm_nt(pkt_ref[h], pieces)[DH:QROWS] + ones_k_rows).astype(kat_ref.dtype)
            vt = qkvt[2 * DA + DH * h:2 * DA + DH * (h + 1)]
            vta_ref[h, 0:DH, :] = (first_row if odd else vt).astype(vta_ref.dtype)
            vta_ref[h, DH:LANES, :] = (vt if odd else first_row).astype(vta_ref.dtype)

    row = lambda w: pl.BlockSpec((tm, w), lambda i: (i, 0))
    full = lambda a: pl.BlockSpec(a.shape, lambda i: (0,) * a.ndim)
    nat = pl.BlockSpec((NH, tm, LANES), lambda i: (0, i, 0))
    fmaj = lambda rows: pl.BlockSpec((NH, rows, tm), lambda i: (0, 0, i))
    return pl.pallas_call(
        body, name=name, grid=(t // tm,),
        in_specs=[row(D), full(g1), full(b1), full(w_kv), full(w_qkv_t), full(w_bch), full(w_f), full(bf_pad),
                  full(pk), full(pqt), full(pkt)],
        out_specs=(nat, nat, fmaj(LANES), fmaj(QROWS), fmaj(LANES), row(3 * DCV), row(LANES),
                   pl.BlockSpec((1, 1, LANES), lambda i: (i, 0, 0))),
        out_shape=(jax.ShapeDtypeStruct((NH, t, LANES), MXU_DT), jax.ShapeDtypeStruct((NH, t, LANES), MXU_DT),
                   jax.ShapeDtypeStruct((NH, LANES, t), MXU_DT), jax.ShapeDtypeStruct((NH, QROWS, t), MXU_DT),
                   jax.ShapeDtypeStruct((NH, LANES, t), MXU_DT), jax.ShapeDtypeStruct((t, 3 * DCV), F32),
                   jax.ShapeDtypeStruct((t, LANES), F32), jax.ShapeDtypeStruct((t // tm, 1, LANES), F32)),
        scratch_shapes=[pltpu.VMEM((1, LANES), F32)],
        compiler_params=_params(56),
    )(xh1, g1, b1, w_kv, w_qkv_t, w_bch, w_f, bf_pad, pk, pqt, pkt)


def _attn_fwd(qat, ka, vta, r, name):
    t = ka.shape[1]
    tq = _tile(t, 512)
    nq = t // tq

    def body(r_ref, q_ref, k_ref, v_ref, o_ref, l_ref, st0, st1):
        hp, i = pl.program_id(0), pl.program_id(1)
        key = lax.broadcasted_iota(jnp.int32, (tq, tq), 0)
        qry = lax.broadcasted_iota(jnp.int32, (tq, tq), 1)

        def tile_of(pos):
            return jnp.where(pos == 0, i, jnp.minimum(pos - 1, jnp.maximum(i - 1, 0))), pos <= i

        def scores(pos, buf, masked):
            j, _ = tile_of(pos)
            off = pl.multiple_of(j * tq, tq)
            for a in range(2):
                st = _mm(k_ref[a, pl.ds(off, tq), :], q_ref[a])
                buf[a] = jnp.where(qry >= key, st, NEG) if masked else st

        def consume(pos, buf, carry):
            j, real = tile_of(pos)
            off = pl.multiple_of(j * tq, tq)
            out = []
            for a in range(2):
                m, acc = carry[a]
                st = buf[a]
                d = jnp.where(real, r_ref[2 * hp + a, i] - r_ref[2 * hp + a, j], NEG)
                m_new = jnp.maximum(m, jnp.max(st, axis=0, keepdims=True) + d)
                pt = jnp.exp(st - (m_new - d))
                acc = jnp.exp(m - m_new) * acc + _mm(v_ref[a, :, pl.ds(off, tq)], pt.astype(MXU_DT))
                out.append((m_new, acc))
            return tuple(out)

        def trip(p, carry):
            scores(2 * p + 1, st1, False)
            carry = consume(2 * p, st0, carry)
            scores(2 * p + 2, st0, False)
            return consume(2 * p + 1, st1, carry)

        scores(0, st0, True)
        init = tuple((jnp.full((1, tq), NEG, F32), jnp.zeros((LANES, tq), F32)) for _ in range(2))
        trips = (i + 1) // 2
        carry = lax.fori_loop(0, trips, trip, init)
        (ma, acca), (mb, accb) = consume(2 * trips, st0, carry)
        la, lb = acca[DH:DH + 1, :], accb[0:1, :]
        l_ref[0] = ma + jnp.log(la)
        l_ref[1] = mb + jnp.log(lb)
        sub = lax.broadcasted_iota(jnp.int32, (LANES, tq), 0)
        o_ref[...] = jnp.where(sub < DH, acca / la, accb / lb).T

    return pl.pallas_call(
        body, name=name, grid=(NH // 2, nq),
        in_specs=[pl.BlockSpec(memory_space=pltpu.SMEM),
                  pl.BlockSpec((2, LANES, tq), lambda p, i: (p, 0, i)),
                  pl.BlockSpec((2, t, LANES), lambda p, i: (p, 0, 0)),
                  pl.BlockSpec((2, LANES, t), lambda p, i: (p, 0, 0))],
        out_specs=(pl.BlockSpec((tq, LANES), lambda p, i: (i, p)),
                   pl.BlockSpec((2, 1, tq), lambda p, i: (p, 0, i))),
        out_shape=(jax.ShapeDtypeStruct((t, DA), F32), jax.ShapeDtypeStruct((NH, 1, t), F32)),
        scratch_shapes=[pltpu.VMEM((2, tq, tq), F32), pltpu.VMEM((2, tq, tq), F32)],
        compiler_params=_params(48, 2),
    )(r, qat, ka, vta)


def _conv_parts(bch):
    return bch[:, 0:DCV], bch[:, DCV:2 * DCV], bch[:, 2 * DCV:3 * DCV]


def _mix_post_fwd(o, bch, conv_w, g_attn, g_conv, xh1, g1, b1, w_mo, name):
    t = o.shape[0]
    tm = _tile(t, 512)
    hb = tm // 8

    def body(o_ref, bch_ref, halo_ref, cw_ref, ga_ref, gc_ref, x_ref, g_ref, b_ref, w_ref,
             mg_ref, xh_ref, rs_ref, ext):
        i = pl.program_id(0)
        an, _ = _rms_fwd(o_ref[...])
        mg_ref[:, 0:DA] = (an * ga_ref[...]).astype(mg_ref.dtype)
        bb, cc, hh = _conv_parts(bch_ref[...])
        _, hc, hh_h = _conv_parts(halo_ref[...])
        u = cc * hh
        ext[0:8, :] = jnp.where(i > 0, hc * hh_h, 0.0)
        ext[8:8 + tm, :] = u
        raw = cw_ref[0:1, :] * ext[6:6 + tm, :] + cw_ref[1:2, :] * ext[7:7 + tm, :] + cw_ref[2:3, :] * u
        cn, _ = _rms_fwd(bb * raw)
        mg_ref[:, DA:D] = (cn * gc_ref[...]).astype(mg_ref.dtype)
        x1 = x_ref[...] * g_ref[...] + b_ref[...]
        xh, rstd = _ln_fwd(ALPHA * x1 + _mm(mg_ref[...], w_ref[...]))
        xh_ref[...] = xh
        rs_ref[...] = rstd

    row = lambda w: pl.BlockSpec((tm, w), lambda i: (i, 0))
    full = lambda a: pl.BlockSpec(a.shape, lambda i: (0, 0))
    return pl.pallas_call(
        body, name=name, grid=(t // tm,),
        in_specs=[row(DA), row(3 * DCV),
                  pl.BlockSpec((8, 3 * DCV), lambda i: (jnp.maximum(i * hb - 1, 0), 0)),
                  full(conv_w), full(g_attn), full(g_conv), row(D), full(g1), full(b1), full(w_mo)],
        out_specs=(row(D), row(D), pl.BlockSpec((tm, 1), lambda i: (i, 0))),
        out_shape=(jax.ShapeDtypeStruct((t, D), MXU_DT), jax.ShapeDtypeStruct((t, D), F32),
                   jax.ShapeDtypeStruct((t, 1), F32)),
        scratch_shapes=[pltpu.VMEM((tm + 8, DCV), F32)],
        compiler_params=_params(48),
    )(o, bch, bch, conv_w, g_attn, g_conv, xh1, g1, b1, w_mo)


def _tail(xh3, rs3, g3, b3, p, w_g, w_ple, bg, g4, b4, target, name):
    t = xh3.shape[0]
    tm = _tile(t, 512)

    def body(x_ref, rs_ref, g3_ref, b3_ref, p_ref, wg_ref, wp_ref, bg_ref, g4_ref, b4_ref, t_ref,
             dr_ref, dz_ref, de_ref, st_ref):
        @pl.when(pl.program_id(0) == 0)
        def _():
            st_ref[...] = jnp.zeros_like(st_ref)

        xh3v = x_ref[...]
        x3 = xh3v * g3_ref[...] + b3_ref[...]
        gate = _sigmoid(_mm(x3.astype(MXU_DT), wg_ref[...]) + bg_ref[...])
        e = _mm(p_ref[...].astype(MXU_DT), wp_ref[...])
        xh4, rstd4 = _ln_fwd(ALPHA * x3 + gate * e)
        diff = xh4 * g4_ref[...] + b4_ref[...] - t_ref[...]
        dy = diff * (1.0 / D)
        st_ref[5:6, :] += _colsum(diff * diff)
        st_ref[0:1, :] += _colsum(dy * xh4)
        st_ref[1:2, :] += _colsum(dy)
        dr4 = _ln_bwd(dy * g4_ref[...], xh4, rstd4)
        de_ref[...] = (dr4 * gate).astype(de_ref.dtype)
        dz = dr4 * e * gate * (1.0 - gate)
        st_ref[2:3, :] += _colsum(dz)
        dzb = dz.astype(MXU_DT)
        dz_ref[...] = dzb
        dx3 = ALPHA * dr4 + _mm_nt(dzb, wg_ref[...])
        st_ref[3:4, :] += _colsum(dx3 * xh3v)
        st_ref[4:5, :] += _colsum(dx3)
        dr_ref[...] = _ln_bwd(dx3 * g3_ref[...], xh3v, rs_ref[...])

    row = lambda w: pl.BlockSpec((tm, w), lambda i: (i, 0))
    full = lambda a: pl.BlockSpec(a.shape, lambda i: (0, 0))
    return pl.pallas_call(
        body, name=name, grid=(t // tm,),
        in_specs=[row(D), row(1), full(g3), full(b3), row(PLE), full(w_g), full(w_ple), full(bg), full(g4),
                  full(b4), row(D)],
        out_specs=(row(D), row(D), row(D), pl.BlockSpec((8, D), lambda i: (0, 0))),
        out_shape=(jax.ShapeDtypeStruct((t, D), F32), jax.ShapeDtypeStruct((t, D), MXU_DT),
                   jax.ShapeDtypeStruct((t, D), MXU_DT), jax.ShapeDtypeStruct((8, D), F32)),
        compiler_params=_params(48),
    )(xh3, rs3, g3, b3, p, w_g, w_ple, bg, g4, b4, target)


def _ffn_bwd(dr, gact, uact, xin, rsin, gin, w_in, w_out, prev_ln, name):
    t = dr.shape[0]
    tm = _tile(t, 256)
    chunks = _f_chunks()

    def body(dr_ref, g_ref, u_ref, x_ref, rs_ref, gi_ref, win_hbm, wout_hbm,
             df_ref, dg_ref, du_ref, dx_ref, st_ref, win_v, wout_v, acc_ref):
        @pl.when(pl.program_id(0) == 0)
        def _():
            pltpu.sync_copy(win_hbm, win_v)
            pltpu.sync_copy(wout_hbm, wout_v)
            st_ref[...] = jnp.zeros_like(st_ref)

        drv = dr_ref[...]
        dfb = (0.5 * drv).astype(MXU_DT)
        df_ref[...] = dfb
        for ci, (c0, fc) in enumerate(chunks):
            dh = _mm_nt(dfb, wout_v[c0:c0 + fc, :])
            g = g_ref[:, c0:c0 + fc].astype(F32)
            u = u_ref[:, c0:c0 + fc].astype(F32)
            sg = _sigmoid(g)
            dgb = (dh * u * (sg * (1.0 + g * (1.0 - sg)))).astype(MXU_DT)
            dub = (dh * (g * sg)).astype(MXU_DT)
            dg_ref[:, c0:c0 + fc] = dgb
            du_ref[:, c0:c0 + fc] = dub
            part = _mm_nt(dgb, win_v[:, c0:c0 + fc]) + _mm_nt(dub, win_v[:, F + c0:F + c0 + fc])
            if ci == 0:
                acc_ref[...] = part
            else:
                acc_ref[...] += part
        dx = ALPHA * drv + acc_ref[...]
        if prev_ln:
            xh = x_ref[...]
            st_ref[0:1, :] += _colsum(dx * xh)
            st_ref[1:2, :] += _colsum(dx)
            dx_ref[...] = _ln_bwd(dx * gi_ref[...], xh, rs_ref[...])
        else:
            dx_ref[...] = dx

    row = pl.BlockSpec((tm, D), lambda i: (i, 0))
    vec = pl.BlockSpec((1, D), lambda i: (0, 0))
    act = pl.BlockSpec((tm, F), lambda i: (i, 0))
    return pl.pallas_call(
        body, name=name, grid=(t // tm,),
        in_specs=[row, act, act, row, pl.BlockSpec((tm, 1), lambda i: (i, 0)), vec, ANY, ANY],
        out_specs=(row, act, act, row, pl.BlockSpec((8, D), lambda i: (0, 0))),
        out_shape=(jax.ShapeDtypeStruct((t, D), MXU_DT), jax.ShapeDtypeStruct((t, F), MXU_DT),
                   jax.ShapeDtypeStruct((t, F), MXU_DT), jax.ShapeDtypeStruct((t, D), F32),
                   jax.ShapeDtypeStruct((8, D), F32)),
        scratch_shapes=[pltpu.VMEM((D, 2 * F), MXU_DT), pltpu.VMEM((F, D), MXU_DT), pltpu.VMEM((tm, D), F32)],
        compiler_params=_params(52),
    )(dr, gact, uact, xin, rsin, gin, w_in, w_out)


def _mix_post_bwd(dr2, o, bch, conv_w, g_attn, g_conv, w_mo, name):
    t = dr2.shape[0]
    tm = _tile(t, 512)
    hb = tm // 8

    def body(dr_ref, o_ref, bch_ref, halo_ref, cw_ref, ga_ref, gc_ref, w_ref,
             dm_ref, do_ref, dl_ref, dy_ref, st_ref, ext):
        i = pl.program_id(0)

        @pl.when(i == 0)
        def _():
            st_ref[...] = jnp.zeros_like(st_ref)

        dmb = dr_ref[...].astype(MXU_DT)
        dm_ref[...] = dmb
        dmg = _mm_nt(dmb, w_ref[...])
        ov = o_ref[...]
        an, ra = _rms_fwd(ov)
        da = dmg[:, 0:DA]
        st_ref[0:1, :] += _colsum(da * an)
        dxa = _rms_bwd(da * ga_ref[...], an, ra)
        dor = dxa.astype(MXU_DT).astype(F32)
        dot = dor.T
        for h in range(NH):
            do_ref[h, 0:DH, :] = dot[DH * h:DH * (h + 1)].astype(do_ref.dtype)
            do_ref[h, DH:LANES, :] = jnp.zeros((LANES - DH, tm), do_ref.dtype)
        srow = lax.broadcasted_iota(jnp.int32, (8, DA), 0)
        scol = lax.broadcasted_iota(jnp.int32, (8, DA), 1)
        sel = jnp.where((scol // DH) == srow, 1.0, 0.0).astype(MXU_DT)
        hi, mid, lo = _split3(dor * ov)
        delta = _mm_nt(sel, hi) + _mm_nt(sel, mid) + _mm_nt(sel, lo)
        for h in range(NH):
            dl_ref[h] = delta[h:h + 1, :]
        bb, cc, hh = _conv_parts(bch_ref[...])
        _, hc, hh_h = _conv_parts(halo_ref[...])
        u = cc * hh
        ext[0:8, :] = jnp.where(i > 0, hc * hh_h, 0.0)
        ext[8:8 + tm, :] = u
        raw = cw_ref[0:1, :] * ext[6:6 + tm, :] + cw_ref[1:2, :] * ext[7:7 + tm, :] + cw_ref[2:3, :] * u
        cn, rc = _rms_fwd(bb * raw)
        dcn = dmg[:, DA:D]
        st_ref[1:2, :] += _colsum(dcn * cn)
        dy_ref[...] = _rms_bwd(dcn * gc_ref[...], cn, rc)

    row = lambda w: pl.BlockSpec((tm, w), lambda i: (i, 0))
    full = lambda a: pl.BlockSpec(a.shape, lambda i: (0, 0))
    return pl.pallas_call(
        body, name=name, grid=(t // tm,),
        in_specs=[row(D), row(DA), row(3 * DCV),
                  pl.BlockSpec((8, 3 * DCV), lambda i: (jnp.maximum(i * hb - 1, 0), 0)),
                  full(conv_w), full(g_attn), full(g_conv), full(w_mo)],
        out_specs=(row(D), pl.BlockSpec((NH, LANES, tm), lambda i: (0, 0, i)),
                   pl.BlockSpec((NH, 1, tm), lambda i: (0, 0, i)), row(DCV),
                   pl.BlockSpec((8, DA), lambda i: (0, 0))),
        out_shape=(jax.ShapeDtypeStruct((t, D), MXU_DT), jax.ShapeDtypeStruct((NH, LANES, t), MXU_DT),
                   jax.ShapeDtypeStruct((NH, 1, t), F32), jax.ShapeDtypeStruct((t, DCV), F32),
                   jax.ShapeDtypeStruct((8, DA), F32)),
        scratch_shapes=[pltpu.VMEM((tm + 8, DCV), F32)],
        compiler_params=_params(48),
    )(dr2, o, bch, bch, conv_w, g_attn, g_conv, w_mo)


def _conv_bwd(dy, bch, conv_w, name):
    t = dy.shape[0]
    tm = _tile(t, 512)
    hb = tm // 8
    nt = t // tm

    def body(dy_ref, dyn_ref, bch_ref, prev_ref, next_ref, cw_ref, out_ref, st_ref, ext_u, ext_d):
        i = pl.program_id(0)

        @pl.when(i == 0)
        def _():
            st_ref[...] = jnp.zeros_like(st_ref)

        bb, cc, hh = _conv_parts(bch_ref[...])
        _, pc, ph = _conv_parts(prev_ref[...])
        nb, _, _ = _conv_parts(next_ref[...])
        u = cc * hh
        ext_u[0:8, :] = jnp.where(i > 0, pc * ph, 0.0)
        ext_u[8:8 + tm, :] = u
        u1 = ext_u[7:7 + tm, :]
        u2 = ext_u[6:6 + tm, :]
        w0, w1, w2 = cw_ref[0:1, :], cw_ref[1:2, :], cw_ref[2:3, :]
        dyv = dy_ref[...]
        out_ref[:, 0:DCV] = (dyv * (w0 * u2 + w1 * u1 + w2 * u)).astype(out_ref.dtype)
        dcr = dyv * bb
        ext_d[0:tm, :] = dcr
        ext_d[tm:tm + 8, :] = jnp.where(i < nt - 1, dyn_ref[...] * nb, 0.0)
        du = w2 * dcr + w1 * ext_d[1:1 + tm, :] + w0 * ext_d[2:2 + tm, :]
        out_ref[:, DCV:2 * DCV] = (du * hh).astype(out_ref.dtype)
        out_ref[:, 2 * DCV:3 * DCV] = (du * cc).astype(out_ref.dtype)
        st_ref[0:1, :] += _colsum(dcr * u2)
        st_ref[1:2, :] += _colsum(dcr * u1)
        st_ref[2:3, :] += _colsum(dcr * u)

    row = lambda w: pl.BlockSpec((tm, w), lambda i: (i, 0))
    prev = lambda w: pl.BlockSpec((8, w), lambda i: (jnp.maximum(i * hb - 1, 0), 0))
    nxt = lambda w: pl.BlockSpec((8, w), lambda i: (jnp.minimum((i + 1) * hb, nt * hb - 1), 0))
    return pl.pallas_call(
        body, name=name, grid=(nt,),
        in_specs=[row(DCV), nxt(DCV), row(3 * DCV), prev(3 * DCV), nxt(3 * DCV),
                  pl.BlockSpec(conv_w.shape, lambda i: (0, 0))],
        out_specs=(row(3 * DCV), pl.BlockSpec((8, DCV), lambda i: (0, 0))),
        out_shape=(jax.ShapeDtypeStruct((t, 3 * DCV), MXU_DT), jax.ShapeDtypeStruct((8, DCV), F32)),
        scratch_shapes=[pltpu.VMEM((tm + 8, DCV), F32), pltpu.VMEM((tm + 8, DCV), F32)],
        compiler_params=_params(48),
    )(dy, dy, bch, bch, bch, conv_w)


def _attn_bwd(ka, kat, va, qat, dot, lrow, drow, r, name):
    t = ka.shape[1]
    tq = _tile(t, 512)
    nq = t // tq

    def body(r_ref, ka_ref, kat_ref, va_ref, l_ref, dl_ref, qat_hbm, dot_hbm,
             dk_ref, dv_ref, dck_ref, dqt_hbm, dcq_hbm, qat_v, dot_v, dq_acc):
        hp, j = pl.program_id(0), pl.program_id(1)

        @pl.when(j == 0)
        def _():
            pltpu.sync_copy(qat_hbm.at[pl.ds(2 * hp, 2)], qat_v)
            pltpu.sync_copy(dot_hbm.at[pl.ds(2 * hp, 2)], dot_v)
            dq_acc[...] = jnp.zeros_like(dq_acc)

        key = lax.broadcasted_iota(jnp.int32, (tq, tq), 0)
        qry = lax.broadcasted_iota(jnp.int32, (tq, tq), 1)

        def step(i, carry, masked):
            off = pl.multiple_of(i * tq, tq)
            out = []
            for a in range(2):
                dk, dv = carry[a]
                st = _mm(ka_ref[a], qat_v[a, :, pl.ds(off, tq)])
                dpt = _mm(va_ref[a], dot_v[a, :, pl.ds(off, tq)])
                if masked:
                    st = jnp.where(qry >= key, st, NEG)
                d = r_ref[2 * hp + a, i] - r_ref[2 * hp + a, j]
                pt = jnp.exp(st - (l_ref[a, :, pl.ds(off, tq)] - d))
                dsb = (pt * (dpt - dl_ref[a, :, pl.ds(off, tq)])).astype(MXU_DT)
                dv = dv + _mm_nt(dot_v[a, 0:DH, pl.ds(off, tq)], pt.astype(MXU_DT))
                dk = dk + _mm_nt(qat_v[a, 0:QROWS, pl.ds(off, tq)], dsb)
                dq_acc[a, :, pl.ds(off, tq)] += _mm(kat_ref[a], dsb)
                out.append((dk, dv))
            return tuple(out)

        init = tuple((jnp.zeros((QROWS, tq), F32), jnp.zeros((DH, tq), F32)) for _ in range(2))
        carry = step(j, init, True)
        (dka, dva), (dkb, dvb) = lax.fori_loop(j + 1, nq, lambda i, cr: step(i, cr, False), carry)
        dk_ref[...] = jnp.concatenate([dka[0:DH], dkb[0:DH]], axis=0).T.astype(dk_ref.dtype)
        dv_ref[...] = jnp.concatenate([dva, dvb], axis=0).T.astype(dv_ref.dtype)
        dck_ref[0] = -dka[DH + 3:DH + 4, :]
        dck_ref[1] = -dkb[DH + 3:DH + 4, :]

        @pl.when(j == nq - 1)
        def _():
            pltpu.sync_copy(dq_acc, dqt_hbm.at[pl.ds(2 * hp, 2)])
            pltpu.sync_copy(dq_acc.at[:, DH:DH + 1, :], dcq_hbm.at[pl.ds(2 * hp, 2)])

    pair = lambda rows, cols: pl.BlockSpec((2, rows, cols), lambda p, j: (p, 0, 0))
    return pl.pallas_call(
        body, name=name, grid=(NH // 2, nq),
        in_specs=[pl.BlockSpec(memory_space=pltpu.SMEM),
                  pl.BlockSpec((2, tq, LANES), lambda p, j: (p, j, 0)),
                  pl.BlockSpec((2, QROWS, tq), lambda p, j: (p, 0, j)),
                  pl.BlockSpec((2, tq, LANES), lambda p, j: (p, j, 0)),
                  pair(1, t), pair(1, t), ANY, ANY],
        out_specs=(pl.BlockSpec((tq, LANES), lambda p, j: (j, p)),
                   pl.BlockSpec((tq, LANES), lambda p, j: (j, p)),
                   pl.BlockSpec((2, 1, tq), lambda p, j: (p, 0, j)), ANY, ANY),
        out_shape=(jax.ShapeDtypeStruct((t, DA), MXU_DT), jax.ShapeDtypeStruct((t, DA), MXU_DT),
                   jax.ShapeDtypeStruct((NH, 1, t), F32), jax.ShapeDtypeStruct((NH, QROWS, t), F32),
                   jax.ShapeDtypeStruct((NH, 1, t), F32)),
        scratch_shapes=[pltpu.VMEM((2, LANES, t), MXU_DT), pltpu.VMEM((2, LANES, t), MXU_DT),
                        pltpu.VMEM((2, QROWS, t), F32)],
        compiler_params=_params(52, 2),
    )(r, ka, kat, va, lrow, drow, qat, dot)


def _mix_proj_bwd(dr2, dqt, dk, dv, dbch, dc, z, xh1, rs1, g1, w_qkv, w_bch, w_f, name):
    t = dr2.shape[0]
    tm = _tile(t, 512)
    nt = t // tm

    def body(dr_ref, dqt_ref, dk_ref, dv_ref, db_ref, dc_ref, z_ref, x_ref, rs_ref, g_ref,
             wq_ref, wb_ref, wf_ref, out_ref, df_ref, dq_ref, st_ref, carry):
        @pl.when(pl.program_id(0) == 0)
        def _():
            carry[...] = jnp.zeros_like(carry)
            st_ref[...] = jnp.zeros_like(st_ref)

        dq_ref[...] = (jnp.concatenate([dqt_ref[h, 0:DH, :] for h in range(NH)], axis=0).T * 0.125
                       ).astype(dq_ref.dtype)

        row = lax.broadcasted_iota(jnp.int32, (tm, tm), 0)
        col = lax.broadcasted_iota(jnp.int32, (tm, tm), 1)
        triu = jnp.where(col >= row, 1.0, 0.0).astype(MXU_DT)
        dlogf = carry[...] + _mm_sel(triu, dc_ref[...])
        carry[...] = dlogf[0:1, :]
        dz = dlogf / (1.0 + jnp.exp(z_ref[...]))
        st_ref[2:3, 0:LANES] += _colsum(dz)
        dfb = dz.astype(MXU_DT)
        df_ref[...] = dfb
        dx = (ALPHA * dr_ref[...]
              + _mm_nt(dq_ref[...], wq_ref[:, 0:DA])
              + _mm_nt(dk_ref[...], wq_ref[:, DA:2 * DA])
              + _mm_nt(dv_ref[...], wq_ref[:, 2 * DA:3 * DA])
              + _mm_nt(db_ref[...], wb_ref[...])
              + _mm_nt(dfb, wf_ref[...]))
        xh = x_ref[...]
        st_ref[0:1, :] += _colsum(dx * xh)
        st_ref[1:2, :] += _colsum(dx)
        out_ref[...] = _ln_bwd(dx * g_ref[...], xh, rs_ref[...])

    row = lambda w: pl.BlockSpec((tm, w), lambda i: (nt - 1 - i, 0))
    full = lambda a: pl.BlockSpec(a.shape, lambda i: (0, 0))
    return pl.pallas_call(
        body, name=name, grid=(nt,),
        in_specs=[row(D), pl.BlockSpec((NH, QROWS, tm), lambda i: (0, 0, nt - 1 - i)), row(DA), row(DA),
                  row(3 * DCV), row(LANES), row(LANES), row(D), row(1),
                  full(g1), full(w_qkv), full(w_bch), full(w_f)],
        out_specs=(row(D), row(LANES), row(DA), pl.BlockSpec((8, D), lambda i: (0, 0))),
        out_shape=(jax.ShapeDtypeStruct((t, D), F32), jax.ShapeDtypeStruct((t, LANES), MXU_DT),
                   jax.ShapeDtypeStruct((t, DA), MXU_DT), jax.ShapeDtypeStruct((8, D), F32)),
        scratch_shapes=[pltpu.VMEM((1, LANES), F32)],
        compiler_params=_params(48),
    )(dr2, dqt, dk, dv, dbch, dc, z, xh1, rs1, g1, w_qkv, w_bch, w_f)


def _dw(mode, a_parts, b, m, n, name, tmm=None, tn=None):
    t = b.shape[0]
    tmm = tmm or m
    tn = tn or n
    tt = _tile(t, 1024)

    def body(*refs):
        a_refs, b_ref, o_ref = refs[:len(a_parts)], refs[len(a_parts)], refs[len(a_parts) + 1]

        @pl.when(pl.program_id(2) == 0)
        def _():
            o_ref[...] = jnp.zeros_like(o_ref)

        if mode == "plain":
            a = a_refs[0][...].astype(MXU_DT)
        elif mode == "affine":
            a = (a_refs[0][...] * a_refs[1][...] + a_refs[2][...]).astype(MXU_DT)
        else:
            g = a_refs[0][...].astype(F32)
            a = (g * _sigmoid(g) * a_refs[1][...].astype(F32)).astype(MXU_DT)
        o_ref[...] += _mm_tn(a, b_ref[...].astype(MXU_DT))

    a_tile = pl.BlockSpec((tt, tmm), lambda i, j, k: (k, i))
    a_vec = pl.BlockSpec((1, tmm), lambda i, j, k: (0, i))
    a_specs = {"plain": [a_tile], "affine": [a_tile, a_vec, a_vec], "swiglu": [a_tile, a_tile]}[mode]
    return pl.pallas_call(
        body, name=name, grid=(m // tmm, n // tn, t // tt),
        in_specs=a_specs + [pl.BlockSpec((tt, tn), lambda i, j, k: (k, j))],
        out_specs=pl.BlockSpec((tmm, tn), lambda i, j, k: (i, j)),
        out_shape=jax.ShapeDtypeStruct((m, n), F32),
        compiler_params=_params(48, 3),
    )(*a_parts, b)


def _adamw(w, g, m, v):
    m = ADAM_B1 * m + (1.0 - ADAM_B1) * g
    v = ADAM_B2 * v + (1.0 - ADAM_B2) * (g * g)
    m_hat = m / (1.0 - ADAM_B1 ** ADAM_STEP)
    v_hat = v / (1.0 - ADAM_B2 ** ADAM_STEP)
    delta = -ADAM_LR * (m_hat / (jnp.sqrt(v_hat) + ADAM_EPS) + ADAM_WD * w)
    return delta, m, v


def _reduce_adamw(landed, own, w, m, v, name):
    r, c = own.shape
    tr = _tile(r, 128)

    def body(l_ref, o_ref, w_ref, m_ref, v_ref, g_out, d_out, m_out, v_out):
        me = 4 * lax.axis_index("x") + 2 * lax.axis_index("y") + lax.axis_index("c")
        g = None
        for j in range(NDEV):
            term = jnp.where(me == j, o_ref[...], l_ref[j].astype(F32))
            g = term if g is None else g + term
        g_out[...] = g
        d_out[...], m_out[...], v_out[...] = _adamw(w_ref[...], g, m_ref[...], v_ref[...])

    blk = pl.BlockSpec((tr, c), lambda i: (i, 0))
    sds = jax.ShapeDtypeStruct((r, c), F32)
    return pl.pallas_call(
        body, name=name, grid=(r // tr,),
        in_specs=[pl.BlockSpec((NDEV, tr, c), lambda i: (0, i, 0)), blk, blk, blk, blk],
        out_specs=(blk, blk, blk, blk), out_shape=(sds, sds, sds, sds),
        compiler_params=_params(40),
    )(landed, own, w, m, v)


def _sum_small(gathered, name):
    _, r, c = gathered.shape

    def body(g_ref, o_ref):
        acc = g_ref[0]
        for j in range(1, NDEV):
            acc = acc + g_ref[j]
        o_ref[...] = acc

    return pl.pallas_call(body, name=name, out_shape=jax.ShapeDtypeStruct((r, c), F32))(gathered)


def _adamw_small(g, w, m, v, name):
    def body(g_ref, w_ref, m_ref, v_ref, d_out, m_out, v_out):
        d_out[...], m_out[...], v_out[...] = _adamw(w_ref[...], g_ref[...], m_ref[...], v_ref[...])

    sds = jax.ShapeDtypeStruct(g.shape, F32)
    return pl.pallas_call(body, name=name, out_shape=(sds, sds, sds))(g, w, m, v)


def _cols_from_stack(s):
    return jnp.transpose(s, (1, 0, 2)).reshape(s.shape[1], NDEV * s.shape[2])


def _cols_to_stack(w):
    r, c = w.shape
    return jnp.transpose(w.reshape(r, NDEV, c // NDEV), (1, 0, 2))


def _rows_from_stack(s):
    return s.reshape(NDEV * s.shape[1], s.shape[2])


def _rows_to_stack(w):
    r, c = w.shape
    return w.reshape(NDEV, r // NDEV, c)


SMALL_ROWS = 16
SMALL_SLOTS = {
    "ln1_g": (0, 0, D), "ln1_b": (1, 0, D), "ln2_g": (2, 0, D), "ln2_b": (3, 0, D), "ln3_g": (4, 0, D),
    "ln3_b": (5, 0, D), "b_ple_gate": (6, 0, D), "ln4_g": (7, 0, D), "ln4_b": (8, 0, D),
    "g_attn": (9, 0, DA), "g_conv": (9, DA, DCV), "b_forget": (10, 0, NH),
}
CONVW_ROW = 11
LOSS_SLOT = (10, LANES)


def _pack_small(vals, conv_rows, loss=None):
    out = jnp.zeros((SMALL_ROWS, D), F32)
    for nm, (r, off, wd) in SMALL_SLOTS.items():
        out = out.at[r:r + 1, off:off + wd].set(vals[nm].reshape(1, wd).astype(F32))
    out = out.at[CONVW_ROW:CONVW_ROW + 3, 0:conv_rows.shape[1]].set(conv_rows.astype(F32))
    if loss is not None:
        out = out.at[LOSS_SLOT[0], LOSS_SLOT[1]].set(loss)
    return out


def _unpack_small(packed, name):
    r, off, wd = SMALL_SLOTS[name]
    return packed[r:r + 1, off:off + wd]


def kernel(x, p, ffn1_w_in, ffn1_w_out, ln1_g, ln1_b, w_mix_in, b_forget, conv_w, g_attn, g_conv, w_mix_out, ln2_g, ln2_b, ffn2_w_in, ffn2_w_out, ln3_g, ln3_b, w_ple, w_ple_gate, b_ple_gate, ln4_g, ln4_b, loss_target, m_ffn1_w_in, m_ffn1_w_out, m_ln1_g, m_ln1_b, m_w_mix_in, m_b_forget, m_conv_w, m_g_attn, m_g_conv, m_w_mix_out, m_ln2_g, m_ln2_b, m_ffn2_w_in, m_ffn2_w_out, m_ln3_g, m_ln3_b, m_w_ple, m_w_ple_gate, m_b_ple_gate, m_ln4_g, m_ln4_b, v_ffn1_w_in, v_ffn1_w_out, v_ln1_g, v_ln1_b, v_w_mix_in, v_b_forget, v_conv_w, v_g_attn, v_g_conv, v_w_mix_out, v_ln2_g, v_ln2_b, v_ffn2_w_in, v_ffn2_w_out, v_ln3_g, v_ln3_b, v_w_ple, v_w_ple_gate, v_b_ple_gate, v_ln4_g, v_ln4_b):
    args = dict(locals())
    t = x.shape[1]
    me = 4 * lax.axis_index("x") + 2 * lax.axis_index("y") + lax.axis_index("c")
    x0 = x.reshape(t, D)
    p0 = p.reshape(t, PLE)
    tgt = loss_target.reshape(t, D)

    big = ["ffn1_w_in", "ffn1_w_out", "w_mix_in", "w_mix_out", "ffn2_w_in", "ffn2_w_out", "w_ple", "w_ple_gate"]
    col_sharded = {"ffn1_w_in", "w_mix_in", "ffn2_w_in", "w_ple"}
    shard = {nm: args[nm][0] for nm in big}

    gathered = _allgather([shard[nm].astype(WIRE_DT) for nm in big] + [conv_w[0]], "ag_weights")
    full = {nm: (_cols_from_stack(g) if nm in col_sharded else _rows_from_stack(g)).astype(MXU_DT)
            for nm, g in zip(big, gathered[:len(big)])}
    cw = _cols_from_stack(gathered[len(big)])
    gx, gw, small, loss_part = _local_step(x0, p0, tgt, full, cw, {nm: args[nm] for nm in SMALL_SLOTS})

    stacks = [(_cols_to_stack(gw[nm]) if nm in col_sharded else _rows_to_stack(gw[nm])) for nm in big]
    landed = _exchange([s.astype(WIRE_DT) for s in stacks], "rs_grads")
    small_part = _pack_small({nm: small[nm] for nm in SMALL_SLOTS},
                             jnp.pad(small["conv_w"], ((0, 0), (0, D - DCV))), loss_part)
    small_all = _allgather([small_part], "ag_small")[0]
    small_g = _sum_small(small_all, "sum_small")
    loss = small_g[LOSS_SLOT[0], LOSS_SLOT[1]]

    outs = {"loss": loss, "grad_x": gx.reshape(1, t, D)}
    for nm, st, ld in zip(big, stacks, landed):
        own = lax.dynamic_index_in_dim(st, me, axis=0, keepdims=False)
        g, dl, mn, vn = _reduce_adamw(ld, own, shard[nm], args["m_" + nm][0], args["v_" + nm][0], "adamw_" + nm)
        outs["grad_" + nm], outs["delta_" + nm], outs["new_m_" + nm], outs["new_v_" + nm] = (
            g[None], dl[None], mn[None], vn[None])
    small_names = list(SMALL_SLOTS)
    cshard = lax.dynamic_slice_in_dim(small_g[CONVW_ROW:CONVW_ROW + 3, 0:DCV], me * (DCV // NDEV), DCV // NDEV, axis=1)
    g_pack = _pack_small({nm: _unpack_small(small_g, nm) for nm in small_names}, cshard)
    packs = [_pack_small({nm: args[pre + nm] for nm in small_names}, args[pre + "conv_w"][0])
             for pre in ("", "m_", "v_")]
    d_pack, m_pack, v_pack = _adamw_small(g_pack, packs[0], packs[1], packs[2], "adamw_small")
    for key, pk in (("grad_", g_pack), ("delta_", d_pack), ("new_m_", m_pack), ("new_v_", v_pack)):
        for nm in small_names:
            outs[key + nm] = _unpack_small(pk, nm)
        outs[key + "conv_w"] = pk[CONVW_ROW:CONVW_ROW + 3, 0:DCV // NDEV][None]

    wnames = ["ffn1_w_in", "ffn1_w_out", "ln1_g", "ln1_b", "w_mix_in", "b_forget", "conv_w", "g_attn", "g_conv",
              "w_mix_out", "ln2_g", "ln2_b", "ffn2_w_in", "ffn2_w_out", "ln3_g", "ln3_b", "w_ple", "w_ple_gate",
              "b_ple_gate", "ln4_g", "ln4_b"]
    return (outs["loss"], outs["grad_x"], *[outs[pre + nm] for pre in ("grad_", "delta_", "new_m_", "new_v_")
                                            for nm in wnames])


def _local_step(x0, p0, tgt, full, cw, sp):
    t = x0.shape[0]
    ln1_g, ln1_b, ln2_g, ln2_b, ln3_g, ln3_b = (sp[k] for k in ("ln1_g", "ln1_b", "ln2_g", "ln2_b", "ln3_g", "ln3_b"))
    ln4_g, ln4_b, g_attn, g_conv, b_ple_gate = (sp[k] for k in ("ln4_g", "ln4_b", "g_attn", "g_conv", "b_ple_gate"))
    wmi = full["w_mix_in"]
    w_qkv = wmi[:, 0:3 * DA]
    w_f = jnp.pad(wmi[:, 3 * DA:3 * DA + NH], ((0, 0), (0, LANES - NH)))
    w_bch = wmi[:, 3 * DA + NH:]
    bf_pad = jnp.pad(sp["b_forget"], ((0, 0), (0, LANES - NH)))
    ones = jnp.ones((1, D), F32)
    zeros = jnp.zeros((1, D), F32)

    g1a, u1a, xh1, rs1 = _ffn_fwd(x0, ones, zeros, full["ffn1_w_in"], full["ffn1_w_out"], "ffn1_fwd")
    ka, va, qat, kat, vta, bch, z, rt = _mix_proj_fwd(xh1, ln1_g, ln1_b, w_qkv[:, DA:], jnp.transpose(w_qkv),
                                                      w_bch, w_f, bf_pad, "mix_proj_fwd")
    rtile = jnp.transpose(rt[:, 0, 0:NH])
    o, lse = _attn_fwd(qat, ka, vta, rtile, "attn_fwd")
    merged, xh2, rs2 = _mix_post_fwd(o, bch, cw, g_attn, g_conv, xh1, ln1_g, ln1_b, full["w_mix_out"],
                                     "mix_post_fwd")
    g2a, u2a, xh3, rs3 = _ffn_fwd(xh2, ln2_g, ln2_b, full["ffn2_w_in"], full["ffn2_w_out"], "ffn2_fwd")

    dr3, dz, de, st_tail = _tail(xh3, rs3, ln3_g, ln3_b, p0, full["w_ple_gate"], full["w_ple"], b_ple_gate,
                                 ln4_g, ln4_b, tgt, "tail")
    df2, dg2, du2, dr2, st_f2 = _ffn_bwd(dr3, g2a, u2a, xh2, rs2, ln2_g, full["ffn2_w_in"], full["ffn2_w_out"],
                                         True, "ffn2_bwd")
    dmix, dot, drow, dyc, st_post = _mix_post_bwd(dr2, o, bch, cw, g_attn, g_conv, full["w_mix_out"],
                                                  "mix_post_bwd")
    dbch, st_conv = _conv_bwd(dyc, bch, cw, "conv_bwd")
    dk, dv, dck, dqt, dcq = _attn_bwd(ka, kat, va, qat, dot, lse, drow, rtile, "attn_bwd")
    dc_pad = jnp.pad(jnp.transpose((dcq + dck).reshape(NH, t)), ((0, 0), (0, LANES - NH)))
    dr1, dfl, dq, st_proj = _mix_proj_bwd(dr2, dqt, dk, dv, dbch, dc_pad, z, xh1, rs1, ln1_g, w_qkv, w_bch, w_f,
                                          "mix_proj_bwd")
    df1, dg1, du1, gx, _ = _ffn_bwd(dr1, g1a, u1a, x0, rs1, ones, full["ffn1_w_in"], full["ffn1_w_out"],
                                    False, "ffn1_bwd")

    x1p, x2p, x3p = (xh1, ln1_g, ln1_b), (xh2, ln2_g, ln2_b), (xh3, ln3_g, ln3_b)
    gw = {}
    gw["ffn1_w_in"] = jnp.concatenate(
        [_dw("affine", (x0, ones, zeros), dg1, D, F, "dw_ffn1_in_g", tn=F // 2),
         _dw("affine", (x0, ones, zeros), du1, D, F, "dw_ffn1_in_u", tn=F // 2)], axis=1)
    gw["ffn1_w_out"] = _dw("swiglu", (g1a, u1a), df1, F, D, "dw_ffn1_out", tmm=F // 2)
    gw["ffn2_w_in"] = jnp.concatenate(
        [_dw("affine", x2p, dg2, D, F, "dw_ffn2_in_g", tn=F // 2),
         _dw("affine", x2p, du2, D, F, "dw_ffn2_in_u", tn=F // 2)], axis=1)
    gw["ffn2_w_out"] = _dw("swiglu", (g2a, u2a), df2, F, D, "dw_ffn2_out", tmm=F // 2)
    gw["w_mix_out"] = _dw("plain", (merged,), dmix, D, D, "dw_mix_out")
    gw["w_mix_in"] = jnp.concatenate(
        [_dw("affine", x1p, dq, D, DA, "dw_mix_in_q"), _dw("affine", x1p, dk, D, DA, "dw_mix_in_k"),
         _dw("affine", x1p, dv, D, DA, "dw_mix_in_v"),
         _dw("affine", x1p, dfl, D, LANES, "dw_mix_in_f")[:, 0:NH],
         _dw("affine", x1p, dbch, D, 3 * DCV, "dw_mix_in_bch")], axis=1)
    gw["w_ple_gate"] = _dw("affine", x3p, dz, D, D, "dw_ple_gate")
    gw["w_ple"] = _dw("plain", (p0,), de, PLE, D, "dw_ple")

    loss_part = (0.5 / D) * jnp.sum(st_tail[5:6, :])
    small = {"ln1_g": st_proj[0:1], "ln1_b": st_proj[1:2], "ln2_g": st_f2[0:1], "ln2_b": st_f2[1:2],
             "ln3_g": st_tail[3:4], "ln3_b": st_tail[4:5], "b_ple_gate": st_tail[2:3], "ln4_g": st_tail[0:1],
             "ln4_b": st_tail[1:2], "g_attn": st_post[0:1], "g_conv": st_post[1:2],
             "b_forget": st_proj[2:3, 0:NH], "conv_w": st_conv[0:3]}
    return gx, gw, small, loss_part
```

```python
import functools
import math

import jax
import jax.numpy as jnp
from jax import lax
from jax.experimental import pallas as pl
from jax.experimental.pallas import tpu as pltpu

D = 1024
F = 2816
NH = 8
DH = 64
DA = NH * DH
DCV = D - DA
PLE = 256
NPROJ = 3 * DA + NH + 3 * DCV
LN_EPS = 1e-5
RMS_EPS = 1e-6
NEG = -1e30
ALPHA = 2.0 ** 0.25
NDEV = 8
LANES = 128

ADAM_LR, ADAM_B1, ADAM_B2, ADAM_EPS, ADAM_WD, ADAM_STEP = 0.001, 0.9, 0.999, 1e-08, 0.01, 10

F32 = jnp.float32
MXU_DT = jnp.bfloat16
WIRE_DT = jnp.bfloat16

MESH_ID = pl.DeviceIdType.MESH
ANY = pl.BlockSpec(memory_space=pl.ANY)


def _params(vmem_mb, n_axes=1):
    return pltpu.CompilerParams(dimension_semantics=("arbitrary",) * n_axes,
                                vmem_limit_bytes=int(vmem_mb) << 20)


def _mm(a, b):
    return jnp.dot(a, b, preferred_element_type=F32)


def _mm_nt(a, b):
    return lax.dot_general(a, b, (((1,), (1,)), ((), ())), preferred_element_type=F32)


def _mm_tn(a, b):
    return lax.dot_general(a, b, (((0,), (0,)), ((), ())), preferred_element_type=F32)


def _split3(x):
    hi = x.astype(MXU_DT)
    r1 = x - hi.astype(F32)
    mid = r1.astype(MXU_DT)
    lo = (r1 - mid.astype(F32)).astype(MXU_DT)
    return hi, mid, lo


def _mm_sel(sel, x):
    hi, mid, lo = _split3(x)
    return _mm(sel, hi) + _mm(sel, mid) + _mm(sel, lo)


def _mm_xsel(x, sel):
    hi, mid, lo = _split3(x)
    return _mm(hi, sel) + _mm(mid, sel) + _mm(lo, sel)


def _sigmoid(x):
    return 1.0 / (1.0 + jnp.exp(-x))


def _ln_fwd(r):
    mu = jnp.mean(r, axis=-1, keepdims=True)
    xc = r - mu
    var = jnp.mean(xc * xc, axis=-1, keepdims=True)
    rstd = lax.rsqrt(var + LN_EPS)
    return xc * rstd, rstd


def _ln_bwd(dxhat, xhat, rstd):
    m1 = jnp.mean(dxhat, axis=-1, keepdims=True)
    m2 = jnp.mean(dxhat * xhat, axis=-1, keepdims=True)
    return rstd * (dxhat - m1 - xhat * m2)


def _rms_fwd(x):
    r = lax.rsqrt(jnp.mean(x * x, axis=-1, keepdims=True) + RMS_EPS)
    return x * r, r


def _rms_bwd(dyg, xn, r):
    return r * (dyg - xn * jnp.mean(dyg * xn, axis=-1, keepdims=True))


def _colsum(x):
    return jnp.sum(x, axis=0, keepdims=True)


def _f_chunks():
    out, c0 = [], 0
    while c0 < F:
        fc = min(512, F - c0)
        out.append((c0, fc))
        c0 += fc
    return out


def _tile(t, want):
    return want if t % want == 0 and t >= want else t


def _exchange_sems(n):
    return [pltpu.SemaphoreType.DMA((n * (NDEV - 1),)), pltpu.SemaphoreType.DMA((n * (NDEV - 1),))]


def _exchange_phases(ins, outs, send_sems, recv_sems):
    n = len(ins)

    def peers():
        x, y, c = lax.axis_index("x"), lax.axis_index("y"), lax.axis_index("c")
        out = []
        for k in range(1, NDEV):
            px = 1 - x if (k >> 2) & 1 else x
            py = 1 - y if (k >> 1) & 1 else y
            pc = 1 - c if k & 1 else c
            out.append(((px, py, pc), 4 * px + 2 * py + pc))
        return 4 * x + 2 * y + c, out

    def remote(w, k, to, slot_src, slot_dst):
        return pltpu.make_async_remote_copy(
            src_ref=ins[w].at[slot_src], dst_ref=outs[w].at[slot_dst],
            send_sem=send_sems.at[w * (NDEV - 1) + k], recv_sem=recv_sems.at[w * (NDEV - 1) + k],
            device_id=to, device_id_type=MESH_ID)

    def start():
        me, prs = peers()
        for k, (to, pid) in enumerate(prs):
            for w in range(n):
                remote(w, k, to, pid, me).start()

    def finish():
        me, prs = peers()
        for k, (to, pid) in enumerate(prs):
            for w in range(n):
                remote(w, k, to, me, pid).wait_recv()
        for k, (to, pid) in enumerate(prs):
            for w in range(n):
                remote(w, k, to, pid, me).wait_send()

    return start, finish


def _exchange(arrs, name):
    n = len(arrs)

    def body(*refs):
        start, finish = _exchange_phases(refs[:n], refs[n:2 * n], *refs[2 * n:])
        start()
        finish()

    return pl.pallas_call(
        body, name=name, out_shape=tuple(jax.ShapeDtypeStruct(a.shape, a.dtype) for a in arrs),
        in_specs=[ANY] * n, out_specs=tuple([ANY] * n), scratch_shapes=_exchange_sems(n),
    )(*arrs)


def _gather_sems(n):
    return [pltpu.SemaphoreType.DMA((n * (NDEV - 1),)), pltpu.SemaphoreType.DMA((n * (NDEV - 1),)),
            pltpu.SemaphoreType.DMA((n,))]


def _gather_phases(ins, outs, send_sems, recv_sems, loc_sems):
    n = len(ins)
    per = NDEV - 1

    def place():
        x, y, c = lax.axis_index("x"), lax.axis_index("y"), lax.axis_index("c")
        return (x, y, c), (x, y, 1 - c), [(1 - x, y), (x, 1 - y), (1 - x, 1 - y)]

    def copy(w, k, block, to, src=None):
        dst = outs[w].at[4 * block[0] + 2 * block[1] + block[2]]
        return pltpu.make_async_remote_copy(
            src_ref=dst if src is None else src, dst_ref=dst,
            send_sem=send_sems.at[w * per + k], recv_sem=recv_sems.at[w * per + k],
            device_id=to, device_id_type=MESH_ID)

    def local(w, me):
        return pltpu.make_async_copy(ins[w], outs[w].at[4 * me[0] + 2 * me[1] + me[2]], loc_sems.at[w])

    def first(me, sib, chips):
        out = []
        for j, chip in enumerate(chips):
            out += [copy(w, 1 + j, me, (*chip, me[2]), src=ins[w]) for w in range(n)]
        return out + [copy(w, 0, me, sib, src=ins[w]) for w in range(n)]

    def start():
        me, sib, chips = place()
        for w in range(n):
            local(w, me).start()
        for cp in first(me, sib, chips):
            cp.start()

    def forward():
        me, sib, chips = place()
        for j, chip in enumerate(chips):
            for w in range(n):
                copy(w, 1 + j, (*chip, me[2]), me).wait_recv()
                copy(w, 4 + j, (*chip, me[2]), sib).start()

    def finish():
        me, sib, chips = place()
        for w in range(n):
            copy(w, 0, sib, me).wait_recv()
        for j, chip in enumerate(chips):
            for w in range(n):
                copy(w, 4 + j, (*chip, 1 - me[2]), me).wait_recv()
        for cp in first(me, sib, chips):
            cp.wait_send()
        for j, chip in enumerate(chips):
            for w in range(n):
                copy(w, 4 + j, (*chip, me[2]), sib).wait_send()
        for w in range(n):
            local(w, me).wait()

    return start, forward, finish


def _allgather(arrs, name):
    n = len(arrs)

    def body(*refs):
        start, forward, finish = _gather_phases(refs[:n], refs[n:2 * n], *refs[2 * n:])
        start()
        forward()
        finish()

    return pl.pallas_call(
        body, name=name, out_shape=tuple(jax.ShapeDtypeStruct((NDEV,) + a.shape, a.dtype) for a in arrs),
        in_specs=[ANY] * n, out_specs=tuple([ANY] * n), scratch_shapes=_gather_sems(n),
    )(*arrs)


def _ffn_fwd(xin, gin, bin_, w_in, w_out, name, gather=()):
    t = xin.shape[0]
    tm = _tile(t, 512)
    nt = t // tm
    chunks = _f_chunks()
    ng = len(gather)

    def body(*refs):
        x_ref, gi_ref, bi_ref, win_hbm, wout_hbm = refs[:5]
        g_ref, u_ref, xh_ref, rs_ref = refs[5 + ng:9 + ng]
        win_v, wout_v, acc_ref = refs[9 + 2 * ng:12 + 2 * ng]
        if ng:
            g_start, g_forward, g_finish = _gather_phases(refs[5:5 + ng], refs[9 + ng:9 + 2 * ng],
                                                          *refs[12 + 2 * ng:])

        @pl.when(pl.program_id(0) == 0)
        def _():
            if ng:
                g_start()
            pltpu.sync_copy(win_hbm, win_v)
            pltpu.sync_copy(wout_hbm, wout_v)

        if ng:
            pl.when(pl.program_id(0) == nt // 2)(g_forward)
            pl.when(pl.program_id(0) == nt - 1)(g_finish)

        x = x_ref[...] * gi_ref[...] + bi_ref[...]
        xb = x.astype(MXU_DT)
        for ci, (c0, fc) in enumerate(chunks):
            gc = _mm(xb, win_v[:, c0:c0 + fc])
            uc = _mm(xb, win_v[:, F + c0:F + c0 + fc])
            g_ref[:, c0:c0 + fc] = gc.astype(g_ref.dtype)
            u_ref[:, c0:c0 + fc] = uc.astype(u_ref.dtype)
            hc = (gc * _sigmoid(gc) * uc).astype(MXU_DT)
            part = _mm(hc, wout_v[c0:c0 + fc, :])
            if ci == 0:
                acc_ref[...] = part
            else:
                acc_ref[...] += part
        xh, rstd = _ln_fwd(ALPHA * x + 0.5 * acc_ref[...])
        xh_ref[...] = xh
        rs_ref[...] = rstd

    row = pl.BlockSpec((tm, D), lambda i: (i, 0))
    vec = pl.BlockSpec((1, D), lambda i: (0, 0))
    act = pl.BlockSpec((tm, F), lambda i: (i, 0))
    return pl.pallas_call(
        body, name=name, grid=(nt,),
        in_specs=[row, vec, vec, ANY, ANY] + [ANY] * ng,
        out_specs=(act, act, row, pl.BlockSpec((tm, 1), lambda i: (i, 0))) + (ANY,) * ng,
        out_shape=(jax.ShapeDtypeStruct((t, F), MXU_DT), jax.ShapeDtypeStruct((t, F), MXU_DT),
                   jax.ShapeDtypeStruct((t, D), F32), jax.ShapeDtypeStruct((t, 1), F32))
        + tuple(jax.ShapeDtypeStruct((NDEV,) + a.shape, a.dtype) for a in gather),
        scratch_shapes=[pltpu.VMEM((D, 2 * F), MXU_DT), pltpu.VMEM((F, D), MXU_DT), pltpu.VMEM((tm, D), F32)]
        + (_gather_sems(ng) if ng else []),
        compiler_params=_params(52),
    )(xin, gin, bin_, w_in, w_out, *gather)


QROWS = 80
BIAS_AT = DH


def _place_matrices():
    import numpy as np
    pk = np.zeros((NH, LANES, LANES), np.float32)
    pqt = np.zeros((NH, LANES, LANES), np.float32)
    for h in range(NH):
        for piece in range(3):
            pk[h, 8 * piece + h, BIAS_AT + 3 + piece] = -1.0
            pqt[h, BIAS_AT + piece, 8 * piece + h] = 1.0
    pkt = np.transpose(pk, (0, 2, 1))
    return tuple(jnp.asarray(m, MXU_DT) for m in (pk, pqt, pkt))


def _mix_proj_fwd(xh1, g1, b1, w_kv, w_qkv_t, w_bch, w_f, bf_pad, name):
    t = xh1.shape[0]
    tm = _tile(t, 512)
    pk, pqt, pkt = _place_matrices()

    def body(x_ref, g_ref, b_ref, wkv_ref, wt_ref, wb_ref, wf_ref, bf_ref, pk_ref, pqt_ref, pkt_ref,
             ka_ref, va_ref, qat_ref, kat_ref, vta_ref, bch_ref, z_ref, r_ref, carry):
        @pl.when(pl.program_id(0) == 0)
        def _():
            carry[...] = jnp.zeros_like(carry)

        xb = (x_ref[...] * g_ref[...] + b_ref[...]).astype(MXU_DT)
        kv = _mm(xb, wkv_ref[...])
        qkvt = _mm_nt(wt_ref[...], xb)
        bch_ref[...] = _mm(xb, wb_ref[...])
        z = _mm(xb, wf_ref[...]) + bf_ref[...]
        z_ref[...] = z
        logf = jnp.minimum(z, 0.0) - jnp.log(1.0 + jnp.exp(-jnp.abs(z)))
        row = lax.broadcasted_iota(jnp.int32, (tm, tm), 0)
        col = lax.broadcasted_iota(jnp.int32, (tm, tm), 1)
        tri = jnp.where(row >= col, 1.0, 0.0).astype(MXU_DT)
        c = carry[...] + _mm_sel(tri, logf)
        carry[...] = c[tm - 1:tm, :]
        r_ref[0] = c[0:1, :]
        lane = lax.broadcasted_iota(jnp.int32, (1, LANES), 1)
        hi, mid, lo = _split3(jnp.where(lane < NH, c - c[0:1, :], 0.0))
        pieces = (hi.astype(F32) + pltpu.roll(mid.astype(F32), 8, 1) + pltpu.roll(lo.astype(F32), 16, 1)
                  ).astype(MXU_DT)
        sub = lax.broadcasted_iota(jnp.int32, (DH, 1), 0)
        ones_k_lanes = jnp.where((lane >= BIAS_AT) & (lane < BIAS_AT + 3), 1.0, 0.0)
        ones_q_rows = jnp.where((sub >= 3) & (sub < 6), 1.0, 0.0)
        ones_k_rows = jnp.where(sub[0:QROWS - DH] < 3, 1.0, 0.0)
        first_row = jnp.where(sub == 0, 1.0, 0.0) + jnp.zeros((DH, tm), F32)
        for h in range(NH):
            pair, odd = divmod(h, 2)
            k2 = kv[:, LANES * pair:LANES * (pair + 1)]
            v2 = kv[:, DA + LANES * pair:DA + LANES * (pair + 1)]
            if odd:
                k2, v2 = pltpu.roll(k2, DH, 1), pltpu.roll(v2, DH, 1)
            ka_ref[h] = jnp.where(lane < DH, k2, _mm(pieces, pk_ref[h]) + ones_k_lanes).astype(ka_ref.dtype)
            va_ref[h] = jnp.where(lane < DH, v2, 0.0).astype(va_ref.dtype)
            qat_ref[h, 0:DH, :] = (qkvt[DH * h:DH * (h + 1)] * 0.125).astype(qat_ref.dtype)
            qat_ref[h, DH:LANES, :] = (_mm_nt(pqt_ref[h], pieces)[DH:LANES] + ones_q_rows).astype(qat_ref.dtype)
            kat_ref[h, 0:DH, :] = qkvt[DA + DH * h:DA + DH * (h + 1)].astype(kat_ref.dtype)
            kat_ref[h, DH:QROWS, :] = (_mm_nt(pkt_ref[h], pieces)[DH:QROWS] + ones_k_rows).astype(kat_ref.dtype)
            vt = qkvt[2 * DA + DH * h:2 * DA + DH * (h + 1)]
            vta_ref[h, 0:DH, :] = (first_row if odd else vt).astype(vta_ref.dtype)
            vta_ref[h, DH:LANES, :] = (vt if odd else first_row).astype(vta_ref.dtype)

    row = lambda w: pl.BlockSpec((tm, w), lambda i: (i, 0))
    full = lambda a: pl.BlockSpec(a.shape, lambda i: (0,) * a.ndim)
    nat = pl.BlockSpec((NH, tm, LANES), lambda i: (0, i, 0))
    fmaj = lambda rows: pl.BlockSpec((NH, rows, tm), lambda i: (0, 0, i))
    return pl.pallas_call(
        body, name=name, grid=(t // tm,),
        in_specs=[row(D), full(g1), full(b1), full(w_kv), full(w_qkv_t), full(w_bch), full(w_f), full(bf_pad),
                  full(pk), full(pqt), full(pkt)],
        out_specs=(nat, nat, fmaj(LANES), fmaj(QROWS), fmaj(LANES), row(3 * DCV), row(LANES),
                   pl.BlockSpec((1, 1, LANES), lambda i: (i, 0, 0))),
        out_shape=(jax.ShapeDtypeStruct((NH, t, LANES), MXU_DT), jax.ShapeDtypeStruct((NH, t, LANES), MXU_DT),
                   jax.ShapeDtypeStruct((NH, LANES, t), MXU_DT), jax.ShapeDtypeStruct((NH, QROWS, t), MXU_DT),
                   jax.ShapeDtypeStruct((NH, LANES, t), MXU_DT), jax.ShapeDtypeStruct((t, 3 * DCV), F32),
                   jax.ShapeDtypeStruct((t, LANES), F32), jax.ShapeDtypeStruct((t // tm, 1, LANES), F32)),
        scratch_shapes=[pltpu.VMEM((1, LANES), F32)],
        compiler_params=_params(56),
    )(xh1, g1, b1, w_kv, w_qkv_t, w_bch, w_f, bf_pad, pk, pqt, pkt)


def _attn_fwd(qat, ka, vta, r, name):
    t = ka.shape[1]
    tq = _tile(t, 512)
    nq = t // tq

    def body(r_ref, q_ref, k_ref, v_ref, o_ref, l_ref, st0, st1):
        hp, i = pl.program_id(0), pl.program_id(1)
        key = lax.broadcasted_iota(jnp.int32, (tq, tq), 0)
        qry = lax.broadcasted_iota(jnp.int32, (tq, tq), 1)

        def tile_of(pos):
            return jnp.where(pos == 0, i, jnp.minimum(pos - 1, jnp.maximum(i - 1, 0))), pos <= i

        def scores(pos, buf, masked):
            j, _ = tile_of(pos)
            off = pl.multiple_of(j * tq, tq)
            for a in range(2):
                st = _mm(k_ref[a, pl.ds(off, tq), :], q_ref[a])
                buf[a] = jnp.where(qry >= key, st, NEG) if masked else st

        def consume(pos, buf, carry):
            j, real = tile_of(pos)
            off = pl.multiple_of(j * tq, tq)
            out = []
            for a in range(2):
                m, acc = carry[a]
                st = buf[a]
                d = jnp.where(real, r_ref[2 * hp + a, i] - r_ref[2 * hp + a, j], NEG)
                m_new = jnp.maximum(m, jnp.max(st, axis=0, keepdims=True) + d)
                pt = jnp.exp(st - (m_new - d))
                acc = jnp.exp(m - m_new) * acc + _mm(v_ref[a, :, pl.ds(off, tq)], pt.astype(MXU_DT))
                out.append((m_new, acc))
            return tuple(out)

        def trip(p, carry):
            scores(2 * p + 1, st1, False)
            carry = consume(2 * p, st0, carry)
            scores(2 * p + 2, st0, False)
            return consume(2 * p + 1, st1, carry)

        scores(0, st0, True)
        init = tuple((jnp.full((1, tq), NEG, F32), jnp.zeros((LANES, tq), F32)) for _ in range(2))
        trips = (i + 1) // 2
        carry = lax.fori_loop(0, trips, trip, init)
        (ma, acca), (mb, accb) = consume(2 * trips, st0, carry)
        la, lb = acca[DH:DH + 1, :], accb[0:1, :]
        l_ref[0] = ma + jnp.log(la)
        l_ref[1] = mb + jnp.log(lb)
        sub = lax.broadcasted_iota(jnp.int32, (LANES, tq), 0)
        o_ref[...] = jnp.where(sub < DH, acca / la, accb / lb).T

    return pl.pallas_call(
        body, name=name, grid=(NH // 2, nq),
        in_specs=[pl.BlockSpec(memory_space=pltpu.SMEM),
                  pl.BlockSpec((2, LANES, tq), lambda p, i: (p, 0, i)),
                  pl.BlockSpec((2, t, LANES), lambda p, i: (p, 0, 0)),
                  pl.BlockSpec((2, LANES, t), lambda p, i: (p, 0, 0))],
        out_specs=(pl.BlockSpec((tq, LANES), lambda p, i: (i, p)),
                   pl.BlockSpec((2, 1, tq), lambda p, i: (p, 0, i))),
        out_shape=(jax.ShapeDtypeStruct((t, DA), F32), jax.ShapeDtypeStruct((NH, 1, t), F32)),
        scratch_shapes=[pltpu.VMEM((2, tq, tq), F32), pltpu.VMEM((2, tq, tq), F32)],
        compiler_params=_params(48, 2),
    )(r, qat, ka, vta)


def _conv_parts(bch):
    return bch[:, 0:DCV], bch[:, DCV:2 * DCV], bch[:, 2 * DCV:3 * DCV]


def _mix_post_fwd(o, bch, conv_w, g_attn, g_conv, xh1, g1, b1, w_mo, name):
    t = o.shape[0]
    tm = _tile(t, 512)
    hb = tm // 8

    def body(o_ref, bch_ref, halo_ref, cw_ref, ga_ref, gc_ref, x_ref, g_ref, b_ref, w_ref,
             mg_ref, xh_ref, rs_ref, ext):
        i = pl.program_id(0)
        an, _ = _rms_fwd(o_ref[...])
        mg_ref[:, 0:DA] = (an * ga_ref[...]).astype(mg_ref.dtype)
        bb, cc, hh = _conv_parts(bch_ref[...])
        _, hc, hh_h = _conv_parts(halo_ref[...])
        u = cc * hh
        ext[0:8, :] = jnp.where(i > 0, hc * hh_h, 0.0)
        ext[8:8 + tm, :] = u
        raw = cw_ref[0:1, :] * ext[6:6 + tm, :] + cw_ref[1:2, :] * ext[7:7 + tm, :] + cw_ref[2:3, :] * u
        cn, _ = _rms_fwd(bb * raw)
        mg_ref[:, DA:D] = (cn * gc_ref[...]).astype(mg_ref.dtype)
        x1 = x_ref[...] * g_ref[...] + b_ref[...]
        xh, rstd = _ln_fwd(ALPHA * x1 + _mm(mg_ref[...], w_ref[...]))
        xh_ref[...] = xh
        rs_ref[...] = rstd

    row = lambda w: pl.BlockSpec((tm, w), lambda i: (i, 0))
    full = lambda a: pl.BlockSpec(a.shape, lambda i: (0, 0))
    return pl.pallas_call(
        body, name=name, grid=(t // tm,),
        in_specs=[row(DA), row(3 * DCV),
                  pl.BlockSpec((8, 3 * DCV), lambda i: (jnp.maximum(i * hb - 1, 0), 0)),
                  full(conv_w), full(g_attn), full(g_conv), row(D), full(g1), full(b1), full(w_mo)],
        out_specs=(row(D), row(D), pl.BlockSpec((tm, 1), lambda i: (i, 0))),
        out_shape=(jax.ShapeDtypeStruct((t, D), MXU_DT), jax.ShapeDtypeStruct((t, D), F32),
                   jax.ShapeDtypeStruct((t, 1), F32)),
        scratch_shapes=[pltpu.VMEM((tm + 8, DCV), F32)],
        compiler_params=_params(48),
    )(o, bch, bch, conv_w, g_attn, g_conv, xh1, g1, b1, w_mo)


def _tail(xh3, rs3, g3, b3, p, w_g, w_ple, bg, g4, b4, target, name):
    t = xh3.shape[0]
    tm = _tile(t, 512)

    def body(x_ref, rs_ref, g3_ref, b3_ref, p_ref, wg_ref, wp_ref, bg_ref, g4_ref, b4_ref, t_ref,
             dr_ref, dz_ref, de_ref, st_ref):
        @pl.when(pl.program_id(0) == 0)
        def _():
            st_ref[...] = jnp.zeros_like(st_ref)

        xh3v = x_ref[...]
        x3 = xh3v * g3_ref[...] + b3_ref[...]
        gate = _sigmoid(_mm(x3.astype(MXU_DT), wg_ref[...]) + bg_ref[...])
        e = _mm(p_ref[...].astype(MXU_DT), wp_ref[...])
        xh4, rstd4 = _ln_fwd(ALPHA * x3 + gate * e)
        diff = xh4 * g4_ref[...] + b4_ref[...] - t_ref[...]
        dy = diff * (1.0 / D)
        st_ref[5:6, :] += _colsum(diff * diff)
        st_ref[0:1, :] += _colsum(dy * xh4)
        st_ref[1:2, :] += _colsum(dy)
        dr4 = _ln_bwd(dy * g4_ref[...], xh4, rstd4)
        de_ref[...] = (dr4 * gate).astype(de_ref.dtype)
        dz = dr4 * e * gate * (1.0 - gate)
        st_ref[2:3, :] += _colsum(dz)
        dzb = dz.astype(MXU_DT)
        dz_ref[...] = dzb
        dx3 = ALPHA * dr4 + _mm_nt(dzb, wg_ref[...])
        st_ref[3:4, :] += _colsum(dx3 * xh3v)
        st_ref[4:5, :] += _colsum(dx3)
        dr_ref[...] = _ln_bwd(dx3 * g3_ref[...], xh3v, rs_ref[...])

    row = lambda w: pl.BlockSpec((tm, w), lambda i: (i, 0))
    full = lambda a: pl.BlockSpec(a.shape, lambda i: (0, 0))
    return pl.pallas_call(
        body, name=name, grid=(t // tm,),
        in_specs=[row(D), row(1), full(g3), full(b3), row(PLE), full(w_g), full(w_ple), full(bg), full(g4),
                  full(b4), row(D)],
        out_specs=(row(D), row(D), row(D), pl.BlockSpec((8, D), lambda i: (0, 0))),
        out_shape=(jax.ShapeDtypeStruct((t, D), F32), jax.ShapeDtypeStruct((t, D), MXU_DT),
                   jax.ShapeDtypeStruct((t, D), MXU_DT), jax.ShapeDtypeStruct((8, D), F32)),
        compiler_params=_params(48),
    )(xh3, rs3, g3, b3, p, w_g, w_ple, bg, g4, b4, target)


def _ffn_bwd(dr, gact, uact, xin, rsin, gin, w_in, w_out, prev_ln, name, exchange=()):
    t = dr.shape[0]
    tm = _tile(t, 256)
    nt = t // tm
    chunks = _f_chunks()
    ne = len(exchange)

    def body(*refs):
        dr_ref, g_ref, u_ref, x_ref, rs_ref, gi_ref, win_hbm, wout_hbm = refs[:8]
        df_ref, dg_ref, du_ref, dx_ref, st_ref = refs[8 + ne:13 + ne]
        win_v, wout_v, acc_ref = refs[13 + 2 * ne:16 + 2 * ne]
        if ne:
            e_start, e_finish = _exchange_phases(refs[8:8 + ne], refs[13 + ne:13 + 2 * ne], *refs[16 + 2 * ne:])

        @pl.when(pl.program_id(0) == 0)
        def _():
            if ne:
                e_start()
            pltpu.sync_copy(win_hbm, win_v)
            pltpu.sync_copy(wout_hbm, wout_v)
            st_ref[...] = jnp.zeros_like(st_ref)

        if ne:
            pl.when(pl.program_id(0) == nt - 1)(e_finish)

        drv = dr_ref[...]
        dfb = (0.5 * drv).astype(MXU_DT)
        df_ref[...] = dfb
        for ci, (c0, fc) in enumerate(chunks):
            dh = _mm_nt(dfb, wout_v[c0:c0 + fc, :])
            g = g_ref[:, c0:c0 + fc].astype(F32)
            u = u_ref[:, c0:c0 + fc].astype(F32)
            sg = _sigmoid(g)
            dgb = (dh * u * (sg * (1.0 + g * (1.0 - sg)))).astype(MXU_DT)
            dub = (dh * (g * sg)).astype(MXU_DT)
            dg_ref[:, c0:c0 + fc] = dgb
            du_ref[:, c0:c0 + fc] = dub
            part = _mm_nt(dgb, win_v[:, c0:c0 + fc]) + _mm_nt(dub, win_v[:, F + c0:F + c0 + fc])
            if ci == 0:
                acc_ref[...] = part
            else:
                acc_ref[...] += part
        dx = ALPHA * drv + acc_ref[...]
        if prev_ln:
            xh = x_ref[...]
            st_ref[0:1, :] += _colsum(dx * xh)
            st_ref[1:2, :] += _colsum(dx)
            dx_ref[...] = _ln_bwd(dx * gi_ref[...], xh, rs_ref[...])
        else:
            dx_ref[...] = dx

    row = pl.BlockSpec((tm, D), lambda i: (i, 0))
    vec = pl.BlockSpec((1, D), lambda i: (0, 0))
    act = pl.BlockSpec((tm, F), lambda i: (i, 0))
    return pl.pallas_call(
        body, name=name, grid=(nt,),
        in_specs=[row, act, act, row, pl.BlockSpec((tm, 1), lambda i: (i, 0)), vec, ANY, ANY] + [ANY] * ne,
        out_specs=(row, act, act, row, pl.BlockSpec((8, D), lambda i: (0, 0))) + (ANY,) * ne,
        out_shape=(jax.ShapeDtypeStruct((t, D), MXU_DT), jax.ShapeDtypeStruct((t, F), MXU_DT),
                   jax.ShapeDtypeStruct((t, F), MXU_DT), jax.ShapeDtypeStruct((t, D), F32),
                   jax.ShapeDtypeStruct((8, D), F32))
        + tuple(jax.ShapeDtypeStruct(a.shape, a.dtype) for a in exchange),
        scratch_shapes=[pltpu.VMEM((D, 2 * F), MXU_DT), pltpu.VMEM((F, D), MXU_DT), pltpu.VMEM((tm, D), F32)]
        + (_exchange_sems(ne) if ne else []),
        compiler_params=_params(52),
    )(dr, gact, uact, xin, rsin, gin, w_in, w_out, *exchange)


def _mix_post_bwd(dr2, o, bch, conv_w, g_attn, g_conv, w_mo, name):
    t = dr2.shape[0]
    tm = _tile(t, 512)
    hb = tm // 8

    def body(dr_ref, o_ref, bch_ref, halo_ref, cw_ref, ga_ref, gc_ref, w_ref,
             dm_ref, do_ref, dl_ref, dy_ref, st_ref, ext):
        i = pl.program_id(0)

        @pl.when(i == 0)
        def _():
            st_ref[...] = jnp.zeros_like(st_ref)

        dmb = dr_ref[...].astype(MXU_DT)
        dm_ref[...] = dmb
        dmg = _mm_nt(dmb, w_ref[...])
        ov = o_ref[...]
        an, ra = _rms_fwd(ov)
        da = dmg[:, 0:DA]
        st_ref[0:1, :] += _colsum(da * an)
        dxa = _rms_bwd(da * ga_ref[...], an, ra)
        dor = dxa.astype(MXU_DT).astype(F32)
        dot = dor.T
        for h in range(NH):
            do_ref[h, 0:DH, :] = dot[DH * h:DH * (h + 1)].astype(do_ref.dtype)
            do_ref[h, DH:LANES, :] = jnp.zeros((LANES - DH, tm), do_ref.dtype)
        srow = lax.broadcasted_iota(jnp.int32, (8, DA), 0)
        scol = lax.broadcasted_iota(jnp.int32, (8, DA), 1)
        sel = jnp.where((scol // DH) == srow, 1.0, 0.0).astype(MXU_DT)
        hi, mid, lo = _split3(dor * ov)
        delta = _mm_nt(sel, hi) + _mm_nt(sel, mid) + _mm_nt(sel, lo)
        for h in range(NH):
            dl_ref[h] = delta[h:h + 1, :]
        bb, cc, hh = _conv_parts(bch_ref[...])
        _, hc, hh_h = _conv_parts(halo_ref[...])
        u = cc * hh
        ext[0:8, :] = jnp.where(i > 0, hc * hh_h, 0.0)
        ext[8:8 + tm, :] = u
        raw = cw_ref[0:1, :] * ext[6:6 + tm, :] + cw_ref[1:2, :] * ext[7:7 + tm, :] + cw_ref[2:3, :] * u
        cn, rc = _rms_fwd(bb * raw)
        dcn = dmg[:, DA:D]
        st_ref[1:2, :] += _colsum(dcn * cn)
        dy_ref[...] = _rms_bwd(dcn * gc_ref[...], cn, rc)

    row = lambda w: pl.BlockSpec((tm, w), lambda i: (i, 0))
    full = lambda a: pl.BlockSpec(a.shape, lambda i: (0, 0))
    return pl.pallas_call(
        body, name=name, grid=(t // tm,),
        in_specs=[row(D), row(DA), row(3 * DCV),
                  pl.BlockSpec((8, 3 * DCV), lambda i: (jnp.maximum(i * hb - 1, 0), 0)),
                  full(conv_w), full(g_attn), full(g_conv), full(w_mo)],
        out_specs=(row(D), pl.BlockSpec((NH, LANES, tm), lambda i: (0, 0, i)),
                   pl.BlockSpec((NH, 1, tm), lambda i: (0, 0, i)), row(DCV),
                   pl.BlockSpec((8, DA), lambda i: (0, 0))),
        out_shape=(jax.ShapeDtypeStruct((t, D), MXU_DT), jax.ShapeDtypeStruct((NH, LANES, t), MXU_DT),
                   jax.ShapeDtypeStruct((NH, 1, t), F32), jax.ShapeDtypeStruct((t, DCV), F32),
                   jax.ShapeDtypeStruct((8, DA), F32)),
        scratch_shapes=[pltpu.VMEM((tm + 8, DCV), F32)],
        compiler_params=_params(48),
    )(dr2, o, bch, bch, conv_w, g_attn, g_conv, w_mo)


def _conv_bwd(dy, bch, conv_w, name):
    t = dy.shape[0]
    tm = _tile(t, 512)
    hb = tm // 8
    nt = t // tm

    def body(dy_ref, dyn_ref, bch_ref, prev_ref, next_ref, cw_ref, out_ref, st_ref, ext_u, ext_d):
        i = pl.program_id(0)

        @pl.when(i == 0)
        def _():
            st_ref[...] = jnp.zeros_like(st_ref)

        bb, cc, hh = _conv_parts(bch_ref[...])
        _, pc, ph = _conv_parts(prev_ref[...])
        nb, _, _ = _conv_parts(next_ref[...])
        u = cc * hh
        ext_u[0:8, :] = jnp.where(i > 0, pc * ph, 0.0)
        ext_u[8:8 + tm, :] = u
        u1 = ext_u[7:7 + tm, :]
        u2 = ext_u[6:6 + tm, :]
        w0, w1, w2 = cw_ref[0:1, :], cw_ref[1:2, :], cw_ref[2:3, :]
        dyv = dy_ref[...]
        out_ref[:, 0:DCV] = (dyv * (w0 * u2 + w1 * u1 + w2 * u)).astype(out_ref.dtype)
        dcr = dyv * bb
        ext_d[0:tm, :] = dcr
        ext_d[tm:tm + 8, :] = jnp.where(i < nt - 1, dyn_ref[...] * nb, 0.0)
        du = w2 * dcr + w1 * ext_d[1:1 + tm, :] + w0 * ext_d[2:2 + tm, :]
        out_ref[:, DCV:2 * DCV] = (du * hh).astype(out_ref.dtype)
        out_ref[:, 2 * DCV:3 * DCV] = (du * cc).astype(out_ref.dtype)
        st_ref[0:1, :] += _colsum(dcr * u2)
        st_ref[1:2, :] += _colsum(dcr * u1)
        st_ref[2:3, :] += _colsum(dcr * u)

    row = lambda w: pl.BlockSpec((tm, w), lambda i: (i, 0))
    prev = lambda w: pl.BlockSpec((8, w), lambda i: (jnp.maximum(i * hb - 1, 0), 0))
    nxt = lambda w: pl.BlockSpec((8, w), lambda i: (jnp.minimum((i + 1) * hb, nt * hb - 1), 0))
    return pl.pallas_call(
        body, name=name, grid=(nt,),
        in_specs=[row(DCV), nxt(DCV), row(3 * DCV), prev(3 * DCV), nxt(3 * DCV),
                  pl.BlockSpec(conv_w.shape, lambda i: (0, 0))],
        out_specs=(row(3 * DCV), pl.BlockSpec((8, DCV), lambda i: (0, 0))),
        out_shape=(jax.ShapeDtypeStruct((t, 3 * DCV), MXU_DT), jax.ShapeDtypeStruct((8, DCV), F32)),
        scratch_shapes=[pltpu.VMEM((tm + 8, DCV), F32), pltpu.VMEM((tm + 8, DCV), F32)],
        compiler_params=_params(48),
    )(dy, dy, bch, bch, bch, conv_w)


def _attn_bwd(ka, kat, va, qat, dot, lrow, drow, r, name):
    t = ka.shape[1]
    tq = _tile(t, 512)
    nq = t // tq

    def body(r_ref, ka_ref, kat_ref, va_ref, l_ref, dl_ref, qat_hbm, dot_hbm,
             dk_ref, dv_ref, dck_ref, dqt_hbm, dcq_hbm, qat_v, dot_v, dq_acc):
        hp, j = pl.program_id(0), pl.program_id(1)

        @pl.when(j == 0)
        def _():
            pltpu.sync_copy(qat_hbm.at[pl.ds(2 * hp, 2)], qat_v)
            pltpu.sync_copy(dot_hbm.at[pl.ds(2 * hp, 2)], dot_v)
            dq_acc[...] = jnp.zeros_like(dq_acc)

        key = lax.broadcasted_iota(jnp.int32, (tq, tq), 0)
        qry = lax.broadcasted_iota(jnp.int32, (tq, tq), 1)

        def step(i, carry, masked):
            off = pl.multiple_of(i * tq, tq)
            out = []
            for a in range(2):
                dk, dv = carry[a]
                st = _mm(ka_ref[a], qat_v[a, :, pl.ds(off, tq)])
                dpt = _mm(va_ref[a], dot_v[a, :, pl.ds(off, tq)])
                if masked:
                    st = jnp.where(qry >= key, st, NEG)
                d = r_ref[2 * hp + a, i] - r_ref[2 * hp + a, j]
                pt = jnp.exp(st - (l_ref[a, :, pl.ds(off, tq)] - d))
                dsb = (pt * (dpt - dl_ref[a, :, pl.ds(off, tq)])).astype(MXU_DT)
                dv = dv + _mm_nt(dot_v[a, 0:DH, pl.ds(off, tq)], pt.astype(MXU_DT))
                dk = dk + _mm_nt(qat_v[a, 0:QROWS, pl.ds(off, tq)], dsb)
                dq_acc[a, :, pl.ds(off, tq)] += _mm(kat_ref[a], dsb)
                out.append((dk, dv))
            return tuple(out)

        init = tuple((jnp.zeros((QROWS, tq), F32), jnp.zeros((DH, tq), F32)) for _ in range(2))
        carry = step(j, init, True)
        (dka, dva), (dkb, dvb) = lax.fori_loop(j + 1, nq, lambda i, cr: step(i, cr, False), carry)
        dk_ref[...] = jnp.concatenate([dka[0:DH], dkb[0:DH]], axis=0).T.astype(dk_ref.dtype)
        dv_ref[...] = jnp.concatenate([dva, dvb], axis=0).T.astype(dv_ref.dtype)
        dck_ref[0] = -dka[DH + 3:DH + 4, :]
        dck_ref[1] = -dkb[DH + 3:DH + 4, :]

        @pl.when(j == nq - 1)
        def _():
            pltpu.sync_copy(dq_acc, dqt_hbm.at[pl.ds(2 * hp, 2)])
            pltpu.sync_copy(dq_acc.at[:, DH:DH + 1, :], dcq_hbm.at[pl.ds(2 * hp, 2)])

    pair = lambda rows, cols: pl.BlockSpec((2, rows, cols), lambda p, j: (p, 0, 0))
    return pl.pallas_call(
        body, name=name, grid=(NH // 2, nq),
        in_specs=[pl.BlockSpec(memory_space=pltpu.SMEM),
                  pl.BlockSpec((2, tq, LANES), lambda p, j: (p, j, 0)),
                  pl.BlockSpec((2, QROWS, tq), lambda p, j: (p, 0, j)),
                  pl.BlockSpec((2, tq, LANES), lambda p, j: (p, j, 0)),
                  pair(1, t), pair(1, t), ANY, ANY],
        out_specs=(pl.BlockSpec((tq, LANES), lambda p, j: (j, p)),
                   pl.BlockSpec((tq, LANES), lambda p, j: (j, p)),
                   pl.BlockSpec((2, 1, tq), lambda p, j: (p, 0, j)), ANY, ANY),
        out_shape=(jax.ShapeDtypeStruct((t, DA), MXU_DT), jax.ShapeDtypeStruct((t, DA), MXU_DT),
                   jax.ShapeDtypeStruct((NH, 1, t), F32), jax.ShapeDtypeStruct((NH, QROWS, t), F32),
                   jax.ShapeDtypeStruct((NH, 1, t), F32)),
        scratch_shapes=[pltpu.VMEM((2, LANES, t), MXU_DT), pltpu.VMEM((2, LANES, t), MXU_DT),
                        pltpu.VMEM((2, QROWS, t), F32)],
        compiler_params=_params(52, 2),
    )(r, ka, kat, va, lrow, drow, qat, dot)


def _mix_proj_bwd(dr2, dqt, dk, dv, dbch, dc, z, xh1, rs1, g1, w_qkv, w_bch, w_f, name):
    t = dr2.shape[0]
    tm = _tile(t, 512)
    nt = t // tm

    def body(dr_ref, dqt_ref, dk_ref, dv_ref, db_ref, dc_ref, z_ref, x_ref, rs_ref, g_ref,
             wq_ref, wb_ref, wf_ref, out_ref, df_ref, dq_ref, st_ref, carry):
        @pl.when(pl.program_id(0) == 0)
        def _():
            carry[...] = jnp.zeros_like(carry)
            st_ref[...] = jnp.zeros_like(st_ref)

        dq_ref[...] = (jnp.concatenate([dqt_ref[h, 0:DH, :] for h in range(NH)], axis=0).T * 0.125
                       ).astype(dq_ref.dtype)

        row = lax.broadcasted_iota(jnp.int32, (tm, tm), 0)
        col = lax.broadcasted_iota(jnp.int32, (tm, tm), 1)
        triu = jnp.where(col >= row, 1.0, 0.0).astype(MXU_DT)
        dlogf = carry[...] + _mm_sel(triu, dc_ref[...])
        carry[...] = dlogf[0:1, :]
        dz = dlogf / (1.0 + jnp.exp(z_ref[...]))
        st_ref[2:3, 0:LANES] += _colsum(dz)
        dfb = dz.astype(MXU_DT)
        df_ref[...] = dfb
        dx = (ALPHA * dr_ref[...]
              + _mm_nt(dq_ref[...], wq_ref[:, 0:DA])
              + _mm_nt(dk_ref[...], wq_ref[:, DA:2 * DA])
              + _mm_nt(dv_ref[...], wq_ref[:, 2 * DA:3 * DA])
              + _mm_nt(db_ref[...], wb_ref[...])
              + _mm_nt(dfb, wf_ref[...]))
        xh = x_ref[...]
        st_ref[0:1, :] += _colsum(dx * xh)
        st_ref[1:2, :] += _colsum(dx)
        out_ref[...] = _ln_bwd(dx * g_ref[...], xh, rs_ref[...])

    row = lambda w: pl.BlockSpec((tm, w), lambda i: (nt - 1 - i, 0))
    full = lambda a: pl.BlockSpec(a.shape, lambda i: (0, 0))
    return pl.pallas_call(
        body, name=name, grid=(nt,),
        in_specs=[row(D), pl.BlockSpec((NH, QROWS, tm), lambda i: (0, 0, nt - 1 - i)), row(DA), row(DA),
                  row(3 * DCV), row(LANES), row(LANES), row(D), row(1),
                  full(g1), full(w_qkv), full(w_bch), full(w_f)],
        out_specs=(row(D), row(LANES), row(DA), pl.BlockSpec((8, D), lambda i: (0, 0))),
        out_shape=(jax.ShapeDtypeStruct((t, D), F32), jax.ShapeDtypeStruct((t, LANES), MXU_DT),
                   jax.ShapeDtypeStruct((t, DA), MXU_DT), jax.ShapeDtypeStruct((8, D), F32)),
        scratch_shapes=[pltpu.VMEM((1, LANES), F32)],
        compiler_params=_params(48),
    )(dr2, dqt, dk, dv, dbch, dc, z, xh1, rs1, g1, w_qkv, w_bch, w_f)


def _dw(mode, a_parts, b, m, n, name, tmm=None, tn=None):
    t = b.shape[0]
    tmm = tmm or m
    tn = tn or n
    tt = _tile(t, 1024)

    def body(*refs):
        a_refs, b_ref, o_ref = refs[:len(a_parts)], refs[len(a_parts)], refs[len(a_parts) + 1]

        @pl.when(pl.program_id(2) == 0)
        def _():
            o_ref[...] = jnp.zeros_like(o_ref)

        if mode == "plain":
            a = a_refs[0][...].astype(MXU_DT)
        elif mode == "affine":
            a = (a_refs[0][...] * a_refs[1][...] + a_refs[2][...]).astype(MXU_DT)
        else:
            g = a_refs[0][...].astype(F32)
            a = (g * _sigmoid(g) * a_refs[1][...].astype(F32)).astype(MXU_DT)
        o_ref[...] += _mm_tn(a, b_ref[...].astype(MXU_DT))

    a_tile = pl.BlockSpec((tt, tmm), lambda i, j, k: (k, i))
    a_vec = pl.BlockSpec((1, tmm), lambda i, j, k: (0, i))
    a_specs = {"plain": [a_tile], "affine": [a_tile, a_vec, a_vec], "swiglu": [a_tile, a_tile]}[mode]
    return pl.pallas_call(
        body, name=name, grid=(m // tmm, n // tn, t // tt),
        in_specs=a_specs + [pl.BlockSpec((tt, tn), lambda i, j, k: (k, j))],
        out_specs=pl.BlockSpec((tmm, tn), lambda i, j, k: (i, j)),
        out_shape=jax.ShapeDtypeStruct((m, n), F32),
        compiler_params=_params(48, 3),
    )(*a_parts, b)


def _adamw(w, g, m, v):
    m = ADAM_B1 * m + (1.0 - ADAM_B1) * g
    v = ADAM_B2 * v + (1.0 - ADAM_B2) * (g * g)
    m_hat = m / (1.0 - ADAM_B1 ** ADAM_STEP)
    v_hat = v / (1.0 - ADAM_B2 ** ADAM_STEP)
    delta = -ADAM_LR * (m_hat / (jnp.sqrt(v_hat) + ADAM_EPS) + ADAM_WD * w)
    return delta, m, v


def _reduce_adamw(landed, own, w, m, v, name):
    r, c = own.shape
    tr = _tile(r, 128)

    def body(l_ref, o_ref, w_ref, m_ref, v_ref, g_out, d_out, m_out, v_out):
        me = 4 * lax.axis_index("x") + 2 * lax.axis_index("y") + lax.axis_index("c")
        g = None
        for j in range(NDEV):
            term = jnp.where(me == j, o_ref[...], l_ref[j].astype(F32))
            g = term if g is None else g + term
        g_out[...] = g
        d_out[...], m_out[...], v_out[...] = _adamw(w_ref[...], g, m_ref[...], v_ref[...])

    blk = pl.BlockSpec((tr, c), lambda i: (i, 0))
    sds = jax.ShapeDtypeStruct((r, c), F32)
    return pl.pallas_call(
        body, name=name, grid=(r // tr,),
        in_specs=[pl.BlockSpec((NDEV, tr, c), lambda i: (0, i, 0)), blk, blk, blk, blk],
        out_specs=(blk, blk, blk, blk), out_shape=(sds, sds, sds, sds),
        compiler_params=_params(40),
    )(landed, own, w, m, v)


def _sum_small(gathered, name):
    _, r, c = gathered.shape

    def body(g_ref, o_ref):
        acc = g_ref[0]
        for j in range(1, NDEV):
            acc = acc + g_ref[j]
        o_ref[...] = acc

    return pl.pallas_call(body, name=name, out_shape=jax.ShapeDtypeStruct((r, c), F32))(gathered)


def _adamw_small(g, w, m, v, name):
    def body(g_ref, w_ref, m_ref, v_ref, d_out, m_out, v_out):
        d_out[...], m_out[...], v_out[...] = _adamw(w_ref[...], g_ref[...], m_ref[...], v_ref[...])

    sds = jax.ShapeDtypeStruct(g.shape, F32)
    return pl.pallas_call(body, name=name, out_shape=(sds, sds, sds))(g, w, m, v)


def _cols_from_stack(s):
    return jnp.transpose(s, (1, 0, 2)).reshape(s.shape[1], NDEV * s.shape[2])


def _cols_to_stack(w):
    r, c = w.shape
    return jnp.transpose(w.reshape(r, NDEV, c // NDEV), (1, 0, 2))


def _rows_from_stack(s):
    return s.reshape(NDEV * s.shape[1], s.shape[2])


def _rows_to_stack(w):
    r, c = w.shape
    return w.reshape(NDEV, r // NDEV, c)


SMALL_ROWS = 16
SMALL_SLOTS = {
    "ln1_g": (0, 0, D), "ln1_b": (1, 0, D), "ln2_g": (2, 0, D), "ln2_b": (3, 0, D), "ln3_g": (4, 0, D),
    "ln3_b": (5, 0, D), "b_ple_gate": (6, 0, D), "ln4_g": (7, 0, D), "ln4_b": (8, 0, D),
    "g_attn": (9, 0, DA), "g_conv": (9, DA, DCV), "b_forget": (10, 0, NH),
}
CONVW_ROW = 11
LOSS_SLOT = (10, LANES)


def _pack_small(vals, conv_rows, loss=None):
    out = jnp.zeros((SMALL_ROWS, D), F32)
    for nm, (r, off, wd) in SMALL_SLOTS.items():
        out = out.at[r:r + 1, off:off + wd].set(vals[nm].reshape(1, wd).astype(F32))
    out = out.at[CONVW_ROW:CONVW_ROW + 3, 0:conv_rows.shape[1]].set(conv_rows.astype(F32))
    if loss is not None:
        out = out.at[LOSS_SLOT[0], LOSS_SLOT[1]].set(loss)
    return out


def _unpack_small(packed, name):
    r, off, wd = SMALL_SLOTS[name]
    return packed[r:r + 1, off:off + wd]


def kernel(x, p, ffn1_w_in, ffn1_w_out, ln1_g, ln1_b, w_mix_in, b_forget, conv_w, g_attn, g_conv, w_mix_out, ln2_g, ln2_b, ffn2_w_in, ffn2_w_out, ln3_g, ln3_b, w_ple, w_ple_gate, b_ple_gate, ln4_g, ln4_b, loss_target, m_ffn1_w_in, m_ffn1_w_out, m_ln1_g, m_ln1_b, m_w_mix_in, m_b_forget, m_conv_w, m_g_attn, m_g_conv, m_w_mix_out, m_ln2_g, m_ln2_b, m_ffn2_w_in, m_ffn2_w_out, m_ln3_g, m_ln3_b, m_w_ple, m_w_ple_gate, m_b_ple_gate, m_ln4_g, m_ln4_b, v_ffn1_w_in, v_ffn1_w_out, v_ln1_g, v_ln1_b, v_w_mix_in, v_b_forget, v_conv_w, v_g_attn, v_g_conv, v_w_mix_out, v_ln2_g, v_ln2_b, v_ffn2_w_in, v_ffn2_w_out, v_ln3_g, v_ln3_b, v_w_ple, v_w_ple_gate, v_b_ple_gate, v_ln4_g, v_ln4_b):
    args = dict(locals())
    t = x.shape[1]
    me = 4 * lax.axis_index("x") + 2 * lax.axis_index("y") + lax.axis_index("c")
    x0 = x.reshape(t, D)
    p0 = p.reshape(t, PLE)
    tgt = loss_target.reshape(t, D)

    big = ["ffn1_w_in", "ffn1_w_out", "w_mix_in", "w_mix_out", "ffn2_w_in", "ffn2_w_out", "w_ple", "w_ple_gate"]
    col_sharded = {"ffn1_w_in", "w_mix_in", "ffn2_w_in", "w_ple"}
    shard = {nm: args[nm][0] for nm in big}

    unstack = lambda nm, g: (_cols_from_stack(g) if nm in col_sharded else _rows_from_stack(g)).astype(MXU_DT)
    stack = lambda nm, g: _cols_to_stack(g) if nm in col_sharded else _rows_to_stack(g)
    wire = lambda names: [shard[nm].astype(WIRE_DT) for nm in names]
    first, later = big[:2], big[2:]

    full = {nm: unstack(nm, g) for nm, g in zip(first, _allgather(wire(first), "ag_ffn1"))}
    ffn1_out, gathered = _ffn1_fwd(x0, full, wire(later) + [conv_w[0]])
    full.update({nm: unstack(nm, g) for nm, g in zip(later, gathered)})
    cw = _cols_from_stack(gathered[len(later)])

    dr1, gw, small, loss_part = _mid_step(p0, tgt, full, cw, {nm: args[nm] for nm in SMALL_SLOTS}, ffn1_out)
    stacks = {nm: stack(nm, gw[nm]) for nm in later}
    gx, gw1, landed_later = _ffn1_bwd(x0, dr1, ffn1_out, full, [stacks[nm].astype(WIRE_DT) for nm in later])
    stacks.update({nm: stack(nm, gw1[nm]) for nm in first})
    landed_first = _exchange([stacks[nm].astype(WIRE_DT) for nm in first], "rs_ffn1")
    landed = dict(zip(first + later, list(landed_first) + list(landed_later)))
    small_part = _pack_small({nm: small[nm] for nm in SMALL_SLOTS},
                             jnp.pad(small["conv_w"], ((0, 0), (0, D - DCV))), loss_part)
    small_all = _allgather([small_part], "ag_small")[0]
    small_g = _sum_small(small_all, "sum_small")
    loss = small_g[LOSS_SLOT[0], LOSS_SLOT[1]]

    outs = {"loss": loss, "grad_x": gx.reshape(1, t, D)}
    for nm in big:
        own = lax.dynamic_index_in_dim(stacks[nm], me, axis=0, keepdims=False)
        g, dl, mn, vn = _reduce_adamw(landed[nm], own, shard[nm], args["m_" + nm][0], args["v_" + nm][0],
                                      "adamw_" + nm)
        outs["grad_" + nm], outs["delta_" + nm], outs["new_m_" + nm], outs["new_v_" + nm] = (
            g[None], dl[None], mn[None], vn[None])
    small_names = list(SMALL_SLOTS)
    cshard = lax.dynamic_slice_in_dim(small_g[CONVW_ROW:CONVW_ROW + 3, 0:DCV], me * (DCV // NDEV), DCV // NDEV, axis=1)
    g_pack = _pack_small({nm: _unpack_small(small_g, nm) for nm in small_names}, cshard)
    packs = [_pack_small({nm: args[pre + nm] for nm in small_names}, args[pre + "conv_w"][0])
             for pre in ("", "m_", "v_")]
    d_pack, m_pack, v_pack = _adamw_small(g_pack, packs[0], packs[1], packs[2], "adamw_small")
    for key, pk in (("grad_", g_pack), ("delta_", d_pack), ("new_m_", m_pack), ("new_v_", v_pack)):
        for nm in small_names:
            outs[key + nm] = _unpack_small(pk, nm)
        outs[key + "conv_w"] = pk[CONVW_ROW:CONVW_ROW + 3, 0:DCV // NDEV][None]

    wnames = ["ffn1_w_in", "ffn1_w_out", "ln1_g", "ln1_b", "w_mix_in", "b_forget", "conv_w", "g_attn", "g_conv",
              "w_mix_out", "ln2_g", "ln2_b", "ffn2_w_in", "ffn2_w_out", "ln3_g", "ln3_b", "w_ple", "w_ple_gate",
              "b_ple_gate", "ln4_g", "ln4_b"]
    return (outs["loss"], outs["grad_x"], *[outs[pre + nm] for pre in ("grad_", "delta_", "new_m_", "new_v_")
                                            for nm in wnames])


def _ffn1_fwd(x0, full, gather=()):
    res = _ffn_fwd(x0, jnp.ones((1, D), F32), jnp.zeros((1, D), F32), full["ffn1_w_in"], full["ffn1_w_out"],
                   "ffn1_fwd", gather)
    return res[:4], res[4:]


def _ffn1_bwd(x0, dr1, ffn1_out, full, exchange=()):
    g1a, u1a, _, rs1 = ffn1_out
    ones, zeros = jnp.ones((1, D), F32), jnp.zeros((1, D), F32)
    res = _ffn_bwd(dr1, g1a, u1a, x0, rs1, ones, full["ffn1_w_in"], full["ffn1_w_out"], False, "ffn1_bwd",
                   exchange)
    df1, dg1, du1, gx = res[:4]
    gw = {"ffn1_w_in": jnp.concatenate(
        [_dw("affine", (x0, ones, zeros), dg1, D, F, "dw_ffn1_in_g", tn=F // 2),
         _dw("affine", (x0, ones, zeros), du1, D, F, "dw_ffn1_in_u", tn=F // 2)], axis=1),
        "ffn1_w_out": _dw("swiglu", (g1a, u1a), df1, F, D, "dw_ffn1_out", tmm=F // 2)}
    return gx, gw, res[5:]


def _mid_step(p0, tgt, full, cw, sp, ffn1_out):
    g1a, u1a, xh1, rs1 = ffn1_out
    t = xh1.shape[0]
    ln1_g, ln1_b, ln2_g, ln2_b, ln3_g, ln3_b = (sp[k] for k in ("ln1_g", "ln1_b", "ln2_g", "ln2_b", "ln3_g", "ln3_b"))
    ln4_g, ln4_b, g_attn, g_conv, b_ple_gate = (sp[k] for k in ("ln4_g", "ln4_b", "g_attn", "g_conv", "b_ple_gate"))
    wmi = full["w_mix_in"]
    w_qkv = wmi[:, 0:3 * DA]
    w_f = jnp.pad(wmi[:, 3 * DA:3 * DA + NH], ((0, 0), (0, LANES - NH)))
    w_bch = wmi[:, 3 * DA + NH:]
    bf_pad = jnp.pad(sp["b_forget"], ((0, 0), (0, LANES - NH)))

    ka, va, qat, kat, vta, bch, z, rt = _mix_proj_fwd(xh1, ln1_g, ln1_b, w_qkv[:, DA:], jnp.transpose(w_qkv),
                                                      w_bch, w_f, bf_pad, "mix_proj_fwd")
    rtile = jnp.transpose(rt[:, 0, 0:NH])
    o, lse = _attn_fwd(qat, ka, vta, rtile, "attn_fwd")
    merged, xh2, rs2 = _mix_post_fwd(o, bch, cw, g_attn, g_conv, xh1, ln1_g, ln1_b, full["w_mix_out"],
                                     "mix_post_fwd")
    g2a, u2a, xh3, rs3 = _ffn_fwd(xh2, ln2_g, ln2_b, full["ffn2_w_in"], full["ffn2_w_out"], "ffn2_fwd")

    dr3, dz, de, st_tail = _tail(xh3, rs3, ln3_g, ln3_b, p0, full["w_ple_gate"], full["w_ple"], b_ple_gate,
                                 ln4_g, ln4_b, tgt, "tail")
    df2, dg2, du2, dr2, st_f2 = _ffn_bwd(dr3, g2a, u2a, xh2, rs2, ln2_g, full["ffn2_w_in"], full["ffn2_w_out"],
                                         True, "ffn2_bwd")
    dmix, dot, drow, dyc, st_post = _mix_post_bwd(dr2, o, bch, cw, g_attn, g_conv, full["w_mix_out"],
                                                  "mix_post_bwd")
    dbch, st_conv = _conv_bwd(dyc, bch, cw, "conv_bwd")
    dk, dv, dck, dqt, dcq = _attn_bwd(ka, kat, va, qat, dot, lse, drow, rtile, "attn_bwd")
    dc_pad = jnp.pad(jnp.transpose((dcq + dck).reshape(NH, t)), ((0, 0), (0, LANES - NH)))
    dr1, dfl, dq, st_proj = _mix_proj_bwd(dr2, dqt, dk, dv, dbch, dc_pad, z, xh1, rs1, ln1_g, w_qkv, w_bch, w_f,
                                          "mix_proj_bwd")

    x1p, x2p, x3p = (xh1, ln1_g, ln1_b), (xh2, ln2_g, ln2_b), (xh3, ln3_g, ln3_b)
    gw = {}
    gw["ffn2_w_in"] = jnp.concatenate(
        [_dw("affine", x2p, dg2, D, F, "dw_ffn2_in_g", tn=F // 2),
         _dw("affine", x2p, du2, D, F, "dw_ffn2_in_u", tn=F // 2)], axis=1)
    gw["ffn2_w_out"] = _dw("swiglu", (g2a, u2a), df2, F, D, "dw_ffn2_out", tmm=F // 2)
    gw["w_mix_out"] = _dw("plain", (merged,), dmix, D, D, "dw_mix_out")
    gw["w_mix_in"] = jnp.concatenate(
        [_dw("affine", x1p, dq, D, DA, "dw_mix_in_q"), _dw("affine", x1p, dk, D, DA, "dw_mix_in_k"),
         _dw("affine", x1p, dv, D, DA, "dw_mix_in_v"),
         _dw("affine", x1p, dfl, D, LANES, "dw_mix_in_f")[:, 0:NH],
         _dw("affine", x1p, dbch, D, 3 * DCV, "dw_mix_in_bch")], axis=1)
    gw["w_ple_gate"] = _dw("affine", x3p, dz, D, D, "dw_ple_gate")
    gw["w_ple"] = _dw("plain", (p0,), de, PLE, D, "dw_ple")

    loss_part = (0.5 / D) * jnp.sum(st_tail[5:6, :])
    small = {"ln1_g": st_proj[0:1], "ln1_b": st_proj[1:2], "ln2_g": st_f2[0:1], "ln2_b": st_f2[1:2],
             "ln3_g": st_tail[3:4], "ln3_b": st_tail[4:5], "b_ple_gate": st_tail[2:3], "ln4_g": st_tail[0:1],
             "ln4_b": st_tail[1:2], "g_attn": st_post[0:1], "g_conv": st_post[1:2],
             "b_forget": st_proj[2:3, 0:NH], "conv_w": st_conv[0:3]}
    return dr1, gw, small, loss_part
```

```python
import functools
import math

import jax
import jax.numpy as jnp
from jax import lax
from jax.experimental import pallas as pl
from jax.experimental.pallas import tpu as pltpu

D = 1024
F = 2816
NH = 8
DH = 64
DA = NH * DH
DCV = D - DA
PLE = 256
NPROJ = 3 * DA + NH + 3 * DCV
LN_EPS = 1e-5
RMS_EPS = 1e-6
NEG = -1e30
ALPHA = 2.0 ** 0.25
NDEV = 8
LANES = 128

ADAM_LR, ADAM_B1, ADAM_B2, ADAM_EPS, ADAM_WD, ADAM_STEP = 0.001, 0.9, 0.999, 1e-08, 0.01, 10

F32 = jnp.float32
MXU_DT = jnp.bfloat16
WIRE_DT = jnp.bfloat16

MESH_ID = pl.DeviceIdType.MESH
ANY = pl.BlockSpec(memory_space=pl.ANY)


def _params(vmem_mb, n_axes=1):
    return pltpu.CompilerParams(dimension_semantics=("arbitrary",) * n_axes,
                                vmem_limit_bytes=int(vmem_mb) << 20)


def _mm(a, b):
    return jnp.dot(a, b, preferred_element_type=F32)


def _mm_nt(a, b):
    return lax.dot_general(a, b, (((1,), (1,)), ((), ())), preferred_element_type=F32)


def _mm_tn(a, b):
    return lax.dot_general(a, b, (((0,), (0,)), ((), ())), preferred_element_type=F32)


def _split3(x):
    hi = x.astype(MXU_DT)
    r1 = x - hi.astype(F32)
    mid = r1.astype(MXU_DT)
    lo = (r1 - mid.astype(F32)).astype(MXU_DT)
    return hi, mid, lo


def _mm_sel(sel, x):
    hi, mid, lo = _split3(x)
    return _mm(sel, hi) + _mm(sel, mid) + _mm(sel, lo)


def _mm_xsel(x, sel):
    hi, mid, lo = _split3(x)
    return _mm(hi, sel) + _mm(mid, sel) + _mm(lo, sel)


def _sigmoid(x):
    return 1.0 / (1.0 + jnp.exp(-x))


def _ln_fwd(r):
    mu = jnp.mean(r, axis=-1, keepdims=True)
    xc = r - mu
    var = jnp.mean(xc * xc, axis=-1, keepdims=True)
    rstd = lax.rsqrt(var + LN_EPS)
    return xc * rstd, rstd


def _ln_bwd(dxhat, xhat, rstd):
    m1 = jnp.mean(dxhat, axis=-1, keepdims=True)
    m2 = jnp.mean(dxhat * xhat, axis=-1, keepdims=True)
    return rstd * (dxhat - m1 - xhat * m2)


def _rms_fwd(x):
    r = lax.rsqrt(jnp.mean(x * x, axis=-1, keepdims=True) + RMS_EPS)
    return x * r, r


def _rms_bwd(dyg, xn, r):
    return r * (dyg - xn * jnp.mean(dyg * xn, axis=-1, keepdims=True))


def _colsum(x):
    return jnp.sum(x, axis=0, keepdims=True)


def _f_chunks():
    out, c0 = [], 0
    while c0 < F:
        fc = min(512, F - c0)
        out.append((c0, fc))
        c0 += fc
    return out


def _tile(t, want):
    return want if t % want == 0 and t >= want else t


def _exchange_sems(n):
    return [pltpu.SemaphoreType.DMA((n * (NDEV - 1),)), pltpu.SemaphoreType.DMA((n * (NDEV - 1),))]


def _exchange_phases(ins, outs, send_sems, recv_sems):
    n = len(ins)

    def peers():
        x, y, c = lax.axis_index("x"), lax.axis_index("y"), lax.axis_index("c")
        out = []
        for k in range(1, NDEV):
            px = 1 - x if (k >> 2) & 1 else x
            py = 1 - y if (k >> 1) & 1 else y
            pc = 1 - c if k & 1 else c
            out.append(((px, py, pc), 4 * px + 2 * py + pc))
        return 4 * x + 2 * y + c, out

    def remote(w, k, to, slot_src, slot_dst):
        return pltpu.make_async_remote_copy(
            src_ref=ins[w].at[slot_src], dst_ref=outs[w].at[slot_dst],
            send_sem=send_sems.at[w * (NDEV - 1) + k], recv_sem=recv_sems.at[w * (NDEV - 1) + k],
            device_id=to, device_id_type=MESH_ID)

    def start():
        me, prs = peers()
        for k, (to, pid) in enumerate(prs):
            for w in range(n):
                remote(w, k, to, pid, me).start()

    def finish():
        me, prs = peers()
        for k, (to, pid) in enumerate(prs):
            for w in range(n):
                remote(w, k, to, me, pid).wait_recv()
        for k, (to, pid) in enumerate(prs):
            for w in range(n):
                remote(w, k, to, pid, me).wait_send()

    return start, finish


def _gather_sems(n):
    return [pltpu.SemaphoreType.DMA((n * (NDEV - 1),)), pltpu.SemaphoreType.DMA((n * (NDEV - 1),)),
            pltpu.SemaphoreType.DMA((n,))]


def _gather_phases(ins, outs, send_sems, recv_sems, loc_sems):
    n = len(ins)
    per = NDEV - 1

    def place():
        x, y, c = lax.axis_index("x"), lax.axis_index("y"), lax.axis_index("c")
        return (x, y, c), (x, y, 1 - c), [(1 - x, y), (x, 1 - y), (1 - x, 1 - y)]

    def copy(w, k, block, to, src=None):
        dst = outs[w].at[4 * block[0] + 2 * block[1] + block[2]]
        return pltpu.make_async_remote_copy(
            src_ref=dst if src is None else src, dst_ref=dst,
            send_sem=send_sems.at[w * per + k], recv_sem=recv_sems.at[w * per + k],
            device_id=to, device_id_type=MESH_ID)

    def local(w, me):
        return pltpu.make_async_copy(ins[w], outs[w].at[4 * me[0] + 2 * me[1] + me[2]], loc_sems.at[w])

    def first(me, sib, chips):
        out = []
        for j, chip in enumerate(chips):
            out += [copy(w, 1 + j, me, (*chip, me[2]), src=ins[w]) for w in range(n)]
        return out + [copy(w, 0, me, sib, src=ins[w]) for w in range(n)]

    def start():
        me, sib, chips = place()
        for w in range(n):
            local(w, me).start()
        for cp in first(me, sib, chips):
            cp.start()

    def forward():
        me, sib, chips = place()
        for j, chip in enumerate(chips):
            for w in range(n):
                copy(w, 1 + j, (*chip, me[2]), me).wait_recv()
                copy(w, 4 + j, (*chip, me[2]), sib).start()

    def finish():
        me, sib, chips = place()
        for w in range(n):
            copy(w, 0, sib, me).wait_recv()
        for j, chip in enumerate(chips):
            for w in range(n):
                copy(w, 4 + j, (*chip, 1 - me[2]), me).wait_recv()
        for cp in first(me, sib, chips):
            cp.wait_send()
        for j, chip in enumerate(chips):
            for w in range(n):
                copy(w, 4 + j, (*chip, me[2]), sib).wait_send()
        for w in range(n):
            local(w, me).wait()

    return start, forward, finish


def _allgather(arrs, name):
    n = len(arrs)

    def body(*refs):
        start, forward, finish = _gather_phases(refs[:n], refs[n:2 * n], *refs[2 * n:])
        start()
        forward()
        finish()

    return pl.pallas_call(
        body, name=name, out_shape=tuple(jax.ShapeDtypeStruct((NDEV,) + a.shape, a.dtype) for a in arrs),
        in_specs=[ANY] * n, out_specs=tuple([ANY] * n), scratch_shapes=_gather_sems(n),
    )(*arrs)


def _exchange_and_gather(ex, ga, name):
    ne, ng = len(ex), len(ga)

    def body(*refs):
        ins, outs, sems = refs[:ne + ng], refs[ne + ng:2 * (ne + ng)], refs[2 * (ne + ng):]
        e_start, e_finish = _exchange_phases(ins[:ne], outs[:ne], *sems[:2])
        g_start, g_forward, g_finish = _gather_phases(ins[ne:], outs[ne:], *sems[2:])
        e_start()
        g_start()
        g_forward()
        g_finish()
        e_finish()

    res = pl.pallas_call(
        body, name=name,
        out_shape=tuple(jax.ShapeDtypeStruct(a.shape, a.dtype) for a in ex)
        + tuple(jax.ShapeDtypeStruct((NDEV,) + a.shape, a.dtype) for a in ga),
        in_specs=[ANY] * (ne + ng), out_specs=tuple([ANY] * (ne + ng)),
        scratch_shapes=_exchange_sems(ne) + _gather_sems(ng),
    )(*ex, *ga)
    return res[:ne], res[ne:]


def _ffn_fwd(xin, gin, bin_, w_in, w_out, name, gather=()):
    t = xin.shape[0]
    tm = _tile(t, 512)
    nt = t // tm
    chunks = _f_chunks()
    ng = len(gather)

    def body(*refs):
        x_ref, gi_ref, bi_ref, win_hbm, wout_hbm = refs[:5]
        g_ref, u_ref, xh_ref, rs_ref = refs[5 + ng:9 + ng]
        win_v, wout_v, acc_ref = refs[9 + 2 * ng:12 + 2 * ng]
        if ng:
            g_start, g_forward, g_finish = _gather_phases(refs[5:5 + ng], refs[9 + ng:9 + 2 * ng],
                                                          *refs[12 + 2 * ng:])

        @pl.when(pl.program_id(0) == 0)
        def _():
            if ng:
                g_start()
            pltpu.sync_copy(win_hbm, win_v)
            pltpu.sync_copy(wout_hbm, wout_v)

        if ng:
            pl.when(pl.program_id(0) == nt // 2)(g_forward)
            pl.when(pl.program_id(0) == nt - 1)(g_finish)

        x = x_ref[...] * gi_ref[...] + bi_ref[...]
        xb = x.astype(MXU_DT)
        for ci, (c0, fc) in enumerate(chunks):
            gc = _mm(xb, win_v[:, c0:c0 + fc])
            uc = _mm(xb, win_v[:, F + c0:F + c0 + fc])
            g_ref[:, c0:c0 + fc] = gc.astype(g_ref.dtype)
            u_ref[:, c0:c0 + fc] = uc.astype(u_ref.dtype)
            hc = (gc * _sigmoid(gc) * uc).astype(MXU_DT)
            part = _mm(hc, wout_v[c0:c0 + fc, :])
            if ci == 0:
                acc_ref[...] = part
            else:
                acc_ref[...] += part
        xh, rstd = _ln_fwd(ALPHA * x + 0.5 * acc_ref[...])
        xh_ref[...] = xh
        rs_ref[...] = rstd

    row = pl.BlockSpec((tm, D), lambda i: (i, 0))
    vec = pl.BlockSpec((1, D), lambda i: (0, 0))
    act = pl.BlockSpec((tm, F), lambda i: (i, 0))
    return pl.pallas_call(
        body, name=name, grid=(nt,),
        in_specs=[row, vec, vec, ANY, ANY] + [ANY] * ng,
        out_specs=(act, act, row, pl.BlockSpec((tm, 1), lambda i: (i, 0))) + (ANY,) * ng,
        out_shape=(jax.ShapeDtypeStruct((t, F), MXU_DT), jax.ShapeDtypeStruct((t, F), MXU_DT),
                   jax.ShapeDtypeStruct((t, D), F32), jax.ShapeDtypeStruct((t, 1), F32))
        + tuple(jax.ShapeDtypeStruct((NDEV,) + a.shape, a.dtype) for a in gather),
        scratch_shapes=[pltpu.VMEM((D, 2 * F), MXU_DT), pltpu.VMEM((F, D), MXU_DT), pltpu.VMEM((tm, D), F32)]
        + (_gather_sems(ng) if ng else []),
        compiler_params=_params(52),
    )(xin, gin, bin_, w_in, w_out, *gather)


QROWS = 80
BIAS_AT = DH


def _place_matrices():
    import numpy as np
    pk = np.zeros((NH, LANES, LANES), np.float32)
    pqt = np.zeros((NH, LANES, LANES), np.float32)
    for h in range(NH):
        for piece in range(3):
            pk[h, 8 * piece + h, BIAS_AT + 3 + piece] = -1.0
            pqt[h, BIAS_AT + piece, 8 * piece + h] = 1.0
    pkt = np.transpose(pk, (0, 2, 1))
    return tuple(jnp.asarray(m, MXU_DT) for m in (pk, pqt, pkt))


def _mix_proj_fwd(xh1, g1, b1, w_kv, w_qkv_t, w_bch, w_f, bf_pad, name):
    t = xh1.shape[0]
    tm = _tile(t, 512)
    pk, pqt, pkt = _place_matrices()

    def body(x_ref, g_ref, b_ref, wkv_ref, wt_ref, wb_ref, wf_ref, bf_ref, pk_ref, pqt_ref, pkt_ref,
             ka_ref, va_ref, qat_ref, kat_ref, vta_ref, bch_ref, z_ref, r_ref, carry):
        @pl.when(pl.program_id(0) == 0)
        def _():
            carry[...] = jnp.zeros_like(carry)

        xb = (x_ref[...] * g_ref[...] + b_ref[...]).astype(MXU_DT)
        kv = _mm(xb, wkv_ref[...])
        qkvt = _mm_nt(wt_ref[...], xb)
        bch_ref[...] = _mm(xb, wb_ref[...])
        z = _mm(xb, wf_ref[...]) + bf_ref[...]
        z_ref[...] = z
        logf = jnp.minimum(z, 0.0) - jnp.log(1.0 + jnp.exp(-jnp.abs(z)))
        row = lax.broadcasted_iota(jnp.int32, (tm, tm), 0)
        col = lax.broadcasted_iota(jnp.int32, (tm, tm), 1)
        tri = jnp.where(row >= col, 1.0, 0.0).astype(MXU_DT)
        c = carry[...] + _mm_sel(tri, logf)
        carry[...] = c[tm - 1:tm, :]
        r_ref[0] = c[0:1, :]
        lane = lax.broadcasted_iota(jnp.int32, (1, LANES), 1)
        hi, mid, lo = _split3(jnp.where(lane < NH, c - c[0:1, :], 0.0))
        pieces = (hi.astype(F32) + pltpu.roll(mid.astype(F32), 8, 1) + pltpu.roll(lo.astype(F32), 16, 1)
                  ).astype(MXU_DT)
        sub = lax.broadcasted_iota(jnp.int32, (DH, 1), 0)
        ones_k_lanes = jnp.where((lane >= BIAS_AT) & (lane < BIAS_AT + 3), 1.0, 0.0)
        ones_q_rows = jnp.where((sub >= 3) & (sub < 6), 1.0, 0.0)
        ones_k_rows = jnp.where(sub[0:QROWS - DH] < 3, 1.0, 0.0)
        first_row = jnp.where(sub == 0, 1.0, 0.0) + jnp.zeros((DH, tm), F32)
        for h in range(NH):
            pair, odd = divmod(h, 2)
            k2 = kv[:, LANES * pair:LANES * (pair + 1)]
            v2 = kv[:, DA + LANES * pair:DA + LANES * (pair + 1)]
            if odd:
                k2, v2 = pltpu.roll(k2, DH, 1), pltpu.roll(v2, DH, 1)
            ka_ref[h] = jnp.where(lane < DH, k2, _mm(pieces, pk_ref[h]) + ones_k_lanes).astype(ka_ref.dtype)
            va_ref[h] = jnp.where(lane < DH, v2, 0.0).astype(va_ref.dtype)
            qat_ref[h, 0:DH, :] = (qkvt[DH * h:DH * (h + 1)] * 0.125).astype(qat_ref.dtype)
            qat_ref[h, DH:LANES, :] = (_mm_nt(pqt_ref[h], pieces)[DH:LANES] + ones_q_rows).astype(qat_ref.dtype)
            kat_ref[h, 0:DH, :] = qkvt[DA + DH * h:DA + DH * (h + 1)].astype(kat_ref.dtype)
            kat_ref[h, DH:QROWS, :] = (_mm_nt(pkt_ref[h], pieces)[DH:QROWS] + ones_k_rows).astype(kat_ref.dtype)
            vt = qkvt[2 * DA + DH * h:2 * DA + DH * (h + 1)]
            vta_ref[h, 0:DH, :] = (first_row if odd else vt).astype(vta_ref.dtype)
            vta_ref[h, DH:LANES, :] = (vt if odd else first_row).astype(vta_ref.dtype)

    row = lambda w: pl.BlockSpec((tm, w), lambda i: (i, 0))
    full = lambda a: pl.BlockSpec(a.shape, lambda i: (0,) * a.ndim)
    nat = pl.BlockSpec((NH, tm, LANES), lambda i: (0, i, 0))
    fmaj = lambda rows: pl.BlockSpec((NH, rows, tm), lambda i: (0, 0, i))
    return pl.pallas_call(
        body, name=name, grid=(t // tm,),
        in_specs=[row(D), full(g1), full(b1), full(w_kv), full(w_qkv_t), full(w_bch), full(w_f), full(bf_pad),
                  full(pk), full(pqt), full(pkt)],
        out_specs=(nat, nat, fmaj(LANES), fmaj(QROWS), fmaj(LANES), row(3 * DCV), row(LANES),
                   pl.BlockSpec((1, 1, LANES), lambda i: (i, 0, 0))),
        out_shape=(jax.ShapeDtypeStruct((NH, t, LANES), MXU_DT), jax.ShapeDtypeStruct((NH, t, LANES), MXU_DT),
                   jax.ShapeDtypeStruct((NH, LANES, t), MXU_DT), jax.ShapeDtypeStruct((NH, QROWS, t), MXU_DT),
                   jax.ShapeDtypeStruct((NH, LANES, t), MXU_DT), jax.ShapeDtypeStruct((t, 3 * DCV), F32),
                   jax.ShapeDtypeStruct((t, LANES), F32), jax.ShapeDtypeStruct((t // tm, 1, LANES), F32)),
        scratch_shapes=[pltpu.VMEM((1, LANES), F32)],
        compiler_params=_params(56),
    )(xh1, g1, b1, w_kv, w_qkv_t, w_bch, w_f, bf_pad, pk, pqt, pkt)


def _attn_fwd(qat, ka, vta, r, name):
    t = ka.shape[1]
    tq = _tile(t, 512)
    nq = t // tq

    def body(r_ref, q_ref, k_ref, v_ref, o_ref, l_ref, st0, st1):
        hp, i = pl.program_id(0), pl.program_id(1)
        key = lax.broadcasted_iota(jnp.int32, (tq, tq), 0)
        qry = lax.broadcasted_iota(jnp.int32, (tq, tq), 1)

        def tile_of(pos):
            return jnp.where(pos == 0, i, jnp.minimum(pos - 1, jnp.maximum(i - 1, 0))), pos <= i

        def scores(pos, buf, masked):
            j, _ = tile_of(pos)
            off = pl.multiple_of(j * tq, tq)
            for a in range(2):
                st = _mm(k_ref[a, pl.ds(off, tq), :], q_ref[a])
                buf[a] = jnp.where(qry >= key, st, NEG) if masked else st

        def consume(pos, buf, carry):
            j, real = tile_of(pos)
            off = pl.multiple_of(j * tq, tq)
            out = []
            for a in range(2):
                m, acc = carry[a]
                st = buf[a]
                d = jnp.where(real, r_ref[2 * hp + a, i] - r_ref[2 * hp + a, j], NEG)
                m_new = jnp.maximum(m, jnp.max(st, axis=0, keepdims=True) + d)
                pt = jnp.exp(st - (m_new - d))
                acc = jnp.exp(m - m_new) * acc + _mm(v_ref[a, :, pl.ds(off, tq)], pt.astype(MXU_DT))
                out.append((m_new, acc))
            return tuple(out)

        def trip(p, carry):
            scores(2 * p + 1, st1, False)
            carry = consume(2 * p, st0, carry)
            scores(2 * p + 2, st0, False)
            return consume(2 * p + 1, st1, carry)

        scores(0, st0, True)
        init = tuple((jnp.full((1, tq), NEG, F32), jnp.zeros((LANES, tq), F32)) for _ in range(2))
        trips = (i + 1) // 2
        carry = lax.fori_loop(0, trips, trip, init)
        (ma, acca), (mb, accb) = consume(2 * trips, st0, carry)
        la, lb = acca[DH:DH + 1, :], accb[0:1, :]
        l_ref[0] = ma + jnp.log(la)
        l_ref[1] = mb + jnp.log(lb)
        sub = lax.broadcasted_iota(jnp.int32, (LANES, tq), 0)
        o_ref[...] = jnp.where(sub < DH, acca / la, accb / lb).T

    return pl.pallas_call(
        body, name=name, grid=(NH // 2, nq),
        in_specs=[pl.BlockSpec(memory_space=pltpu.SMEM),
                  pl.BlockSpec((2, LANES, tq), lambda p, i: (p, 0, i)),
                  pl.BlockSpec((2, t, LANES), lambda p, i: (p, 0, 0)),
                  pl.BlockSpec((2, LANES, t), lambda p, i: (p, 0, 0))],
        out_specs=(pl.BlockSpec((tq, LANES), lambda p, i: (i, p)),
                   pl.BlockSpec((2, 1, tq), lambda p, i: (p, 0, i))),
        out_shape=(jax.ShapeDtypeStruct((t, DA), F32), jax.ShapeDtypeStruct((NH, 1, t), F32)),
        scratch_shapes=[pltpu.VMEM((2, tq, tq), F32), pltpu.VMEM((2, tq, tq), F32)],
        compiler_params=_params(48, 2),
    )(r, qat, ka, vta)


def _conv_parts(bch):
    return bch[:, 0:DCV], bch[:, DCV:2 * DCV], bch[:, 2 * DCV:3 * DCV]


def _mix_post_fwd(o, bch, conv_w, g_attn, g_conv, xh1, g1, b1, w_mo, name):
    t = o.shape[0]
    tm = _tile(t, 512)
    hb = tm // 8

    def body(o_ref, bch_ref, halo_ref, cw_ref, ga_ref, gc_ref, x_ref, g_ref, b_ref, w_ref,
             mg_ref, xh_ref, rs_ref, ext):
        i = pl.program_id(0)
        an, _ = _rms_fwd(o_ref[...])
        mg_ref[:, 0:DA] = (an * ga_ref[...]).astype(mg_ref.dtype)
        bb, cc, hh = _conv_parts(bch_ref[...])
        _, hc, hh_h = _conv_parts(halo_ref[...])
        u = cc * hh
        ext[0:8, :] = jnp.where(i > 0, hc * hh_h, 0.0)
        ext[8:8 + tm, :] = u
        raw = cw_ref[0:1, :] * ext[6:6 + tm, :] + cw_ref[1:2, :] * ext[7:7 + tm, :] + cw_ref[2:3, :] * u
        cn, _ = _rms_fwd(bb * raw)
        mg_ref[:, DA:D] = (cn * gc_ref[...]).astype(mg_ref.dtype)
        x1 = x_ref[...] * g_ref[...] + b_ref[...]
        xh, rstd = _ln_fwd(ALPHA * x1 + _mm(mg_ref[...], w_ref[...]))
        xh_ref[...] = xh
        rs_ref[...] = rstd

    row = lambda w: pl.BlockSpec((tm, w), lambda i: (i, 0))
    full = lambda a: pl.BlockSpec(a.shape, lambda i: (0, 0))
    return pl.pallas_call(
        body, name=name, grid=(t // tm,),
        in_specs=[row(DA), row(3 * DCV),
                  pl.BlockSpec((8, 3 * DCV), lambda i: (jnp.maximum(i * hb - 1, 0), 0)),
                  full(conv_w), full(g_attn), full(g_conv), row(D), full(g1), full(b1), full(w_mo)],
        out_specs=(row(D), row(D), pl.BlockSpec((tm, 1), lambda i: (i, 0))),
        out_shape=(jax.ShapeDtypeStruct((t, D), MXU_DT), jax.ShapeDtypeStruct((t, D), F32),
                   jax.ShapeDtypeStruct((t, 1), F32)),
        scratch_shapes=[pltpu.VMEM((tm + 8, DCV), F32)],
        compiler_params=_params(48),
    )(o, bch, bch, conv_w, g_attn, g_conv, xh1, g1, b1, w_mo)


def _tail(xh3, rs3, g3, b3, p, w_g, w_ple, bg, g4, b4, target, name):
    t = xh3.shape[0]
    tm = _tile(t, 512)

    def body(x_ref, rs_ref, g3_ref, b3_ref, p_ref, wg_ref, wp_ref, bg_ref, g4_ref, b4_ref, t_ref,
             dr_ref, dz_ref, de_ref, st_ref):
        @pl.when(pl.program_id(0) == 0)
        def _():
            st_ref[...] = jnp.zeros_like(st_ref)

        xh3v = x_ref[...]
        x3 = xh3v * g3_ref[...] + b3_ref[...]
        gate = _sigmoid(_mm(x3.astype(MXU_DT), wg_ref[...]) + bg_ref[...])
        e = _mm(p_ref[...].astype(MXU_DT), wp_ref[...])
        xh4, rstd4 = _ln_fwd(ALPHA * x3 + gate * e)
        diff = xh4 * g4_ref[...] + b4_ref[...] - t_ref[...]
        dy = diff * (1.0 / D)
        st_ref[5:6, :] += _colsum(diff * diff)
        st_ref[0:1, :] += _colsum(dy * xh4)
        st_ref[1:2, :] += _colsum(dy)
        dr4 = _ln_bwd(dy * g4_ref[...], xh4, rstd4)
        de_ref[...] = (dr4 * gate).astype(de_ref.dtype)
        dz = dr4 * e * gate * (1.0 - gate)
        st_ref[2:3, :] += _colsum(dz)
        dzb = dz.astype(MXU_DT)
        dz_ref[...] = dzb
        dx3 = ALPHA * dr4 + _mm_nt(dzb, wg_ref[...])
        st_ref[3:4, :] += _colsum(dx3 * xh3v)
        st_ref[4:5, :] += _colsum(dx3)
        dr_ref[...] = _ln_bwd(dx3 * g3_ref[...], xh3v, rs_ref[...])

    row = lambda w: pl.BlockSpec((tm, w), lambda i: (i, 0))
    full = lambda a: pl.BlockSpec(a.shape, lambda i: (0, 0))
    return pl.pallas_call(
        body, name=name, grid=(t // tm,),
        in_specs=[row(D), row(1), full(g3), full(b3), row(PLE), full(w_g), full(w_ple), full(bg), full(g4),
                  full(b4), row(D)],
        out_specs=(row(D), row(D), row(D), pl.BlockSpec((8, D), lambda i: (0, 0))),
        out_shape=(jax.ShapeDtypeStruct((t, D), F32), jax.ShapeDtypeStruct((t, D), MXU_DT),
                   jax.ShapeDtypeStruct((t, D), MXU_DT), jax.ShapeDtypeStruct((8, D), F32)),
        compiler_params=_params(48),
    )(xh3, rs3, g3, b3, p, w_g, w_ple, bg, g4, b4, target)


def _ffn_bwd(dr, gact, uact, xin, rsin, gin, w_in, w_out, prev_ln, name, exchange=()):
    t = dr.shape[0]
    tm = _tile(t, 256)
    nt = t // tm
    chunks = _f_chunks()
    ne = len(exchange)

    def body(*refs):
        dr_ref, g_ref, u_ref, x_ref, rs_ref, gi_ref, win_hbm, wout_hbm = refs[:8]
        df_ref, dg_ref, du_ref, dx_ref, st_ref = refs[8 + ne:13 + ne]
        win_v, wout_v, acc_ref = refs[13 + 2 * ne:16 + 2 * ne]
        if ne:
            e_start, e_finish = _exchange_phases(refs[8:8 + ne], refs[13 + ne:13 + 2 * ne], *refs[16 + 2 * ne:])

        @pl.when(pl.program_id(0) == 0)
        def _():
            if ne:
                e_start()
            pltpu.sync_copy(win_hbm, win_v)
            pltpu.sync_copy(wout_hbm, wout_v)
            st_ref[...] = jnp.zeros_like(st_ref)

        if ne:
            pl.when(pl.program_id(0) == nt - 1)(e_finish)

        drv = dr_ref[...]
        dfb = (0.5 * drv).astype(MXU_DT)
        df_ref[...] = dfb
        for ci, (c0, fc) in enumerate(chunks):
            dh = _mm_nt(dfb, wout_v[c0:c0 + fc, :])
            g = g_ref[:, c0:c0 + fc].astype(F32)
            u = u_ref[:, c0:c0 + fc].astype(F32)
            sg = _sigmoid(g)
            dgb = (dh * u * (sg * (1.0 + g * (1.0 - sg)))).astype(MXU_DT)
            dub = (dh * (g * sg)).astype(MXU_DT)
            dg_ref[:, c0:c0 + fc] = dgb
            du_ref[:, c0:c0 + fc] = dub
            part = _mm_nt(dgb, win_v[:, c0:c0 + fc]) + _mm_nt(dub, win_v[:, F + c0:F + c0 + fc])
            if ci == 0:
                acc_ref[...] = part
            else:
                acc_ref[...] += part
        dx = ALPHA * drv + acc_ref[...]
        if prev_ln:
            xh = x_ref[...]
            st_ref[0:1, :] += _colsum(dx * xh)
            st_ref[1:2, :] += _colsum(dx)
            dx_ref[...] = _ln_bwd(dx * gi_ref[...], xh, rs_ref[...])
        else:
            dx_ref[...] = dx

    row = pl.BlockSpec((tm, D), lambda i: (i, 0))
    vec = pl.BlockSpec((1, D), lambda i: (0, 0))
    act = pl.BlockSpec((tm, F), lambda i: (i, 0))
    return pl.pallas_call(
        body, name=name, grid=(nt,),
        in_specs=[row, act, act, row, pl.BlockSpec((tm, 1), lambda i: (i, 0)), vec, ANY, ANY] + [ANY] * ne,
        out_specs=(row, act, act, row, pl.BlockSpec((8, D), lambda i: (0, 0))) + (ANY,) * ne,
        out_shape=(jax.ShapeDtypeStruct((t, D), MXU_DT), jax.ShapeDtypeStruct((t, F), MXU_DT),
                   jax.ShapeDtypeStruct((t, F), MXU_DT), jax.ShapeDtypeStruct((t, D), F32),
                   jax.ShapeDtypeStruct((8, D), F32))
        + tuple(jax.ShapeDtypeStruct(a.shape, a.dtype) for a in exchange),
        scratch_shapes=[pltpu.VMEM((D, 2 * F), MXU_DT), pltpu.VMEM((F, D), MXU_DT), pltpu.VMEM((tm, D), F32)]
        + (_exchange_sems(ne) if ne else []),
        compiler_params=_params(52),
    )(dr, gact, uact, xin, rsin, gin, w_in, w_out, *exchange)


def _mix_post_bwd(dr2, o, bch, conv_w, g_attn, g_conv, w_mo, name):
    t = dr2.shape[0]
    tm = _tile(t, 512)
    hb = tm // 8

    def body(dr_ref, o_ref, bch_ref, halo_ref, cw_ref, ga_ref, gc_ref, w_ref,
             dm_ref, do_ref, dl_ref, dy_ref, st_ref, ext):
        i = pl.program_id(0)

        @pl.when(i == 0)
        def _():
            st_ref[...] = jnp.zeros_like(st_ref)

        dmb = dr_ref[...].astype(MXU_DT)
        dm_ref[...] = dmb
        dmg = _mm_nt(dmb, w_ref[...])
        ov = o_ref[...]
        an, ra = _rms_fwd(ov)
        da = dmg[:, 0:DA]
        st_ref[0:1, :] += _colsum(da * an)
        dxa = _rms_bwd(da * ga_ref[...], an, ra)
        dor = dxa.astype(MXU_DT).astype(F32)
        dot = dor.T
        for h in range(NH):
            do_ref[h, 0:DH, :] = dot[DH * h:DH * (h + 1)].astype(do_ref.dtype)
            do_ref[h, DH:LANES, :] = jnp.zeros((LANES - DH, tm), do_ref.dtype)
        srow = lax.broadcasted_iota(jnp.int32, (8, DA), 0)
        scol = lax.broadcasted_iota(jnp.int32, (8, DA), 1)
        sel = jnp.where((scol // DH) == srow, 1.0, 0.0).astype(MXU_DT)
        hi, mid, lo = _split3(dor * ov)
        delta = _mm_nt(sel, hi) + _mm_nt(sel, mid) + _mm_nt(sel, lo)
        for h in range(NH):
            dl_ref[h] = delta[h:h + 1, :]
        bb, cc, hh = _conv_parts(bch_ref[...])
        _, hc, hh_h = _conv_parts(halo_ref[...])
        u = cc * hh
        ext[0:8, :] = jnp.where(i > 0, hc * hh_h, 0.0)
        ext[8:8 + tm, :] = u
        raw = cw_ref[0:1, :] * ext[6:6 + tm, :] + cw_ref[1:2, :] * ext[7:7 + tm, :] + cw_ref[2:3, :] * u
        cn, rc = _rms_fwd(bb * raw)
        dcn = dmg[:, DA:D]
        st_ref[1:2, :] += _colsum(dcn * cn)
        dy_ref[...] = _rms_bwd(dcn * gc_ref[...], cn, rc)

    row = lambda w: pl.BlockSpec((tm, w), lambda i: (i, 0))
    full = lambda a: pl.BlockSpec(a.shape, lambda i: (0, 0))
    return pl.pallas_call(
        body, name=name, grid=(t // tm,),
        in_specs=[row(D), row(DA), row(3 * DCV),
                  pl.BlockSpec((8, 3 * DCV), lambda i: (jnp.maximum(i * hb - 1, 0), 0)),
                  full(conv_w), full(g_attn), full(g_conv), full(w_mo)],
        out_specs=(row(D), pl.BlockSpec((NH, LANES, tm), lambda i: (0, 0, i)),
                   pl.BlockSpec((NH, 1, tm), lambda i: (0, 0, i)), row(DCV),
                   pl.BlockSpec((8, DA), lambda i: (0, 0))),
        out_shape=(jax.ShapeDtypeStruct((t, D), MXU_DT), jax.ShapeDtypeStruct((NH, LANES, t), MXU_DT),
                   jax.ShapeDtypeStruct((NH, 1, t), F32), jax.ShapeDtypeStruct((t, DCV), F32),
                   jax.ShapeDtypeStruct((8, DA), F32)),
        scratch_shapes=[pltpu.VMEM((tm + 8, DCV), F32)],
        compiler_params=_params(48),
    )(dr2, o, bch, bch, conv_w, g_attn, g_conv, w_mo)


def _conv_bwd(dy, bch, conv_w, name):
    t = dy.shape[0]
    tm = _tile(t, 512)
    hb = tm // 8
    nt = t // tm

    def body(dy_ref, dyn_ref, bch_ref, prev_ref, next_ref, cw_ref, out_ref, st_ref, ext_u, ext_d):
        i = pl.program_id(0)

        @pl.when(i == 0)
        def _():
            st_ref[...] = jnp.zeros_like(st_ref)

        bb, cc, hh = _conv_parts(bch_ref[...])
        _, pc, ph = _conv_parts(prev_ref[...])
        nb, _, _ = _conv_parts(next_ref[...])
        u = cc * hh
        ext_u[0:8, :] = jnp.where(i > 0, pc * ph, 0.0)
        ext_u[8:8 + tm, :] = u
        u1 = ext_u[7:7 + tm, :]
        u2 = ext_u[6:6 + tm, :]
        w0, w1, w2 = cw_ref[0:1, :], cw_ref[1:2, :], cw_ref[2:3, :]
        dyv = dy_ref[...]
        out_ref[:, 0:DCV] = (dyv * (w0 * u2 + w1 * u1 + w2 * u)).astype(out_ref.dtype)
        dcr = dyv * bb
        ext_d[0:tm, :] = dcr
        ext_d[tm:tm + 8, :] = jnp.where(i < nt - 1, dyn_ref[...] * nb, 0.0)
        du = w2 * dcr + w1 * ext_d[1:1 + tm, :] + w0 * ext_d[2:2 + tm, :]
        out_ref[:, DCV:2 * DCV] = (du * hh).astype(out_ref.dtype)
        out_ref[:, 2 * DCV:3 * DCV] = (du * cc).astype(out_ref.dtype)
        st_ref[0:1, :] += _colsum(dcr * u2)
        st_ref[1:2, :] += _colsum(dcr * u1)
        st_ref[2:3, :] += _colsum(dcr * u)

    row = lambda w: pl.BlockSpec((tm, w), lambda i: (i, 0))
    prev = lambda w: pl.BlockSpec((8, w), lambda i: (jnp.maximum(i * hb - 1, 0), 0))
    nxt = lambda w: pl.BlockSpec((8, w), lambda i: (jnp.minimum((i + 1) * hb, nt * hb - 1), 0))
    return pl.pallas_call(
        body, name=name, grid=(nt,),
        in_specs=[row(DCV), nxt(DCV), row(3 * DCV), prev(3 * DCV), nxt(3 * DCV),
                  pl.BlockSpec(conv_w.shape, lambda i: (0, 0))],
        out_specs=(row(3 * DCV), pl.BlockSpec((8, DCV), lambda i: (0, 0))),
        out_shape=(jax.ShapeDtypeStruct((t, 3 * DCV), MXU_DT), jax.ShapeDtypeStruct((8, DCV), F32)),
        scratch_shapes=[pltpu.VMEM((tm + 8, DCV), F32), pltpu.VMEM((tm + 8, DCV), F32)],
        compiler_params=_params(48),
    )(dy, dy, bch, bch, bch, conv_w)


def _attn_bwd(ka, kat, va, qat, dot, lrow, drow, r, name):
    t = ka.shape[1]
    tq = _tile(t, 512)
    nq = t // tq

    def body(r_ref, ka_ref, kat_ref, va_ref, l_ref, dl_ref, qat_hbm, dot_hbm,
             dk_ref, dv_ref, dck_ref, dqt_hbm, dcq_hbm, qat_v, dot_v, dq_acc):
        hp, j = pl.program_id(0), pl.program_id(1)

        @pl.when(j == 0)
        def _():
            pltpu.sync_copy(qat_hbm.at[pl.ds(2 * hp, 2)], qat_v)
            pltpu.sync_copy(dot_hbm.at[pl.ds(2 * hp, 2)], dot_v)
            dq_acc[...] = jnp.zeros_like(dq_acc)

        key = lax.broadcasted_iota(jnp.int32, (tq, tq), 0)
        qry = lax.broadcasted_iota(jnp.int32, (tq, tq), 1)

        def step(i, carry, masked):
            off = pl.multiple_of(i * tq, tq)
            out = []
            for a in range(2):
                dk, dv = carry[a]
                st = _mm(ka_ref[a], qat_v[a, :, pl.ds(off, tq)])
                dpt = _mm(va_ref[a], dot_v[a, :, pl.ds(off, tq)])
                if masked:
                    st = jnp.where(qry >= key, st, NEG)
                d = r_ref[2 * hp + a, i] - r_ref[2 * hp + a, j]
                pt = jnp.exp(st - (l_ref[a, :, pl.ds(off, tq)] - d))
                dsb = (pt * (dpt - dl_ref[a, :, pl.ds(off, tq)])).astype(MXU_DT)
                dv = dv + _mm_nt(dot_v[a, 0:DH, pl.ds(off, tq)], pt.astype(MXU_DT))
                dk = dk + _mm_nt(qat_v[a, 0:QROWS, pl.ds(off, tq)], dsb)
                dq_acc[a, :, pl.ds(off, tq)] += _mm(kat_ref[a], dsb)
                out.append((dk, dv))
            return tuple(out)

        init = tuple((jnp.zeros((QROWS, tq), F32), jnp.zeros((DH, tq), F32)) for _ in range(2))
        carry = step(j, init, True)
        (dka, dva), (dkb, dvb) = lax.fori_loop(j + 1, nq, lambda i, cr: step(i, cr, False), carry)
        dk_ref[...] = jnp.concatenate([dka[0:DH], dkb[0:DH]], axis=0).T.astype(dk_ref.dtype)
        dv_ref[...] = jnp.concatenate([dva, dvb], axis=0).T.astype(dv_ref.dtype)
        dck_ref[0] = -dka[DH + 3:DH + 4, :]
        dck_ref[1] = -dkb[DH + 3:DH + 4, :]

        @pl.when(j == nq - 1)
        def _():
            pltpu.sync_copy(dq_acc, dqt_hbm.at[pl.ds(2 * hp, 2)])
            pltpu.sync_copy(dq_acc.at[:, DH:DH + 1, :], dcq_hbm.at[pl.ds(2 * hp, 2)])

    pair = lambda rows, cols: pl.BlockSpec((2, rows, cols), lambda p, j: (p, 0, 0))
    return pl.pallas_call(
        body, name=name, grid=(NH // 2, nq),
        in_specs=[pl.BlockSpec(memory_space=pltpu.SMEM),
                  pl.BlockSpec((2, tq, LANES), lambda p, j: (p, j, 0)),
                  pl.BlockSpec((2, QROWS, tq), lambda p, j: (p, 0, j)),
                  pl.BlockSpec((2, tq, LANES), lambda p, j: (p, j, 0)),
                  pair(1, t), pair(1, t), ANY, ANY],
        out_specs=(pl.BlockSpec((tq, LANES), lambda p, j: (j, p)),
                   pl.BlockSpec((tq, LANES), lambda p, j: (j, p)),
                   pl.BlockSpec((2, 1, tq), lambda p, j: (p, 0, j)), ANY, ANY),
        out_shape=(jax.ShapeDtypeStruct((t, DA), MXU_DT), jax.ShapeDtypeStruct((t, DA), MXU_DT),
                   jax.ShapeDtypeStruct((NH, 1, t), F32), jax.ShapeDtypeStruct((NH, QROWS, t), F32),
                   jax.ShapeDtypeStruct((NH, 1, t), F32)),
        scratch_shapes=[pltpu.VMEM((2, LANES, t), MXU_DT), pltpu.VMEM((2, LANES, t), MXU_DT),
                        pltpu.VMEM((2, QROWS, t), F32)],
        compiler_params=_params(52, 2),
    )(r, ka, kat, va, lrow, drow, qat, dot)


def _mix_proj_bwd(dr2, dqt, dk, dv, dbch, dc, z, xh1, rs1, g1, w_qkv, w_bch, w_f, name):
    t = dr2.shape[0]
    tm = _tile(t, 512)
    nt = t // tm

    def body(dr_ref, dqt_ref, dk_ref, dv_ref, db_ref, dc_ref, z_ref, x_ref, rs_ref, g_ref,
             wq_ref, wb_ref, wf_ref, out_ref, df_ref, dq_ref, st_ref, carry):
        @pl.when(pl.program_id(0) == 0)
        def _():
            carry[...] = jnp.zeros_like(carry)
            st_ref[...] = jnp.zeros_like(st_ref)

        dq_ref[...] = (jnp.concatenate([dqt_ref[h, 0:DH, :] for h in range(NH)], axis=0).T * 0.125
                       ).astype(dq_ref.dtype)

        row = lax.broadcasted_iota(jnp.int32, (tm, tm), 0)
        col = lax.broadcasted_iota(jnp.int32, (tm, tm), 1)
        triu = jnp.where(col >= row, 1.0, 0.0).astype(MXU_DT)
        dlogf = carry[...] + _mm_sel(triu, dc_ref[...])
        carry[...] = dlogf[0:1, :]
        dz = dlogf / (1.0 + jnp.exp(z_ref[...]))
        st_ref[2:3, 0:LANES] += _colsum(dz)
        dfb = dz.astype(MXU_DT)
        df_ref[...] = dfb
        dx = (ALPHA * dr_ref[...]
              + _mm_nt(dq_ref[...], wq_ref[:, 0:DA])
              + _mm_nt(dk_ref[...], wq_ref[:, DA:2 * DA])
              + _mm_nt(dv_ref[...], wq_ref[:, 2 * DA:3 * DA])
              + _mm_nt(db_ref[...], wb_ref[...])
              + _mm_nt(dfb, wf_ref[...]))
        xh = x_ref[...]
        st_ref[0:1, :] += _colsum(dx * xh)
        st_ref[1:2, :] += _colsum(dx)
        out_ref[...] = _ln_bwd(dx * g_ref[...], xh, rs_ref[...])

    row = lambda w: pl.BlockSpec((tm, w), lambda i: (nt - 1 - i, 0))
    full = lambda a: pl.BlockSpec(a.shape, lambda i: (0, 0))
    return pl.pallas_call(
        body, name=name, grid=(nt,),
        in_specs=[row(D), pl.BlockSpec((NH, QROWS, tm), lambda i: (0, 0, nt - 1 - i)), row(DA), row(DA),
                  row(3 * DCV), row(LANES), row(LANES), row(D), row(1),
                  full(g1), full(w_qkv), full(w_bch), full(w_f)],
        out_specs=(row(D), row(LANES), row(DA), pl.BlockSpec((8, D), lambda i: (0, 0))),
        out_shape=(jax.ShapeDtypeStruct((t, D), F32), jax.ShapeDtypeStruct((t, LANES), MXU_DT),
                   jax.ShapeDtypeStruct((t, DA), MXU_DT), jax.ShapeDtypeStruct((8, D), F32)),
        scratch_shapes=[pltpu.VMEM((1, LANES), F32)],
        compiler_params=_params(48),
    )(dr2, dqt, dk, dv, dbch, dc, z, xh1, rs1, g1, w_qkv, w_bch, w_f)


def _dw(mode, a_parts, b, m, n, name, tmm=None, tn=None, exchange=()):
    t = b.shape[0]
    tmm = tmm or m
    tn = tn or n
    tt = _tile(t, 1024)
    na, ne = len(a_parts), len(exchange)
    grid = (m // tmm, n // tn, t // tt)

    def body(*refs):
        a_refs, b_ref, o_ref = refs[:na], refs[na], refs[na + 1 + ne]
        if ne:
            e_start, e_finish = _exchange_phases(refs[na + 1:na + 1 + ne], refs[na + 2 + ne:na + 2 + 2 * ne],
                                                 *refs[na + 2 + 2 * ne:])
            at = lambda steps: functools.reduce(jnp.logical_and, [pl.program_id(d) == s for d, s in enumerate(steps)])
            pl.when(at((0, 0, 0)))(e_start)

        @pl.when(pl.program_id(2) == 0)
        def _():
            o_ref[...] = jnp.zeros_like(o_ref)

        if mode == "plain":
            a = a_refs[0][...].astype(MXU_DT)
        elif mode == "affine":
            a = (a_refs[0][...] * a_refs[1][...] + a_refs[2][...]).astype(MXU_DT)
        else:
            g = a_refs[0][...].astype(F32)
            a = (g * _sigmoid(g) * a_refs[1][...].astype(F32)).astype(MXU_DT)
        o_ref[...] += _mm_tn(a, b_ref[...].astype(MXU_DT))
        if ne:
            pl.when(at(tuple(g - 1 for g in grid)))(e_finish)

    a_tile = pl.BlockSpec((tt, tmm), lambda i, j, k: (k, i))
    a_vec = pl.BlockSpec((1, tmm), lambda i, j, k: (0, i))
    a_specs = {"plain": [a_tile], "affine": [a_tile, a_vec, a_vec], "swiglu": [a_tile, a_tile]}[mode]
    res = pl.pallas_call(
        body, name=name, grid=grid,
        in_specs=a_specs + [pl.BlockSpec((tt, tn), lambda i, j, k: (k, j))] + [ANY] * ne,
        out_specs=(pl.BlockSpec((tmm, tn), lambda i, j, k: (i, j)),) + (ANY,) * ne,
        out_shape=(jax.ShapeDtypeStruct((m, n), F32),)
        + tuple(jax.ShapeDtypeStruct(a.shape, a.dtype) for a in exchange),
        scratch_shapes=_exchange_sems(ne) if ne else [],
        compiler_params=_params(48, 3),
    )(*a_parts, b, *exchange)
    return res if ne else res[0]


def _adamw(w, g, m, v):
    m = ADAM_B1 * m + (1.0 - ADAM_B1) * g
    v = ADAM_B2 * v + (1.0 - ADAM_B2) * (g * g)
    m_hat = m / (1.0 - ADAM_B1 ** ADAM_STEP)
    v_hat = v / (1.0 - ADAM_B2 ** ADAM_STEP)
    delta = -ADAM_LR * (m_hat / (jnp.sqrt(v_hat) + ADAM_EPS) + ADAM_WD * w)
    return delta, m, v


def _reduce_adamw(landed, own, w, m, v, name):
    r, c = own.shape
    tr = _tile(r, 128)

    def body(l_ref, o_ref, w_ref, m_ref, v_ref, g_out, d_out, m_out, v_out):
        me = 4 * lax.axis_index("x") + 2 * lax.axis_index("y") + lax.axis_index("c")
        g = None
        for j in range(NDEV):
            term = jnp.where(me == j, o_ref[...], l_ref[j].astype(F32))
            g = term if g is None else g + term
        g_out[...] = g
        d_out[...], m_out[...], v_out[...] = _adamw(w_ref[...], g, m_ref[...], v_ref[...])

    blk = pl.BlockSpec((tr, c), lambda i: (i, 0))
    sds = jax.ShapeDtypeStruct((r, c), F32)
    return pl.pallas_call(
        body, name=name, grid=(r // tr,),
        in_specs=[pl.BlockSpec((NDEV, tr, c), lambda i: (0, i, 0)), blk, blk, blk, blk],
        out_specs=(blk, blk, blk, blk), out_shape=(sds, sds, sds, sds),
        compiler_params=_params(40),
    )(landed, own, w, m, v)


def _sum_small(gathered, name):
    _, r, c = gathered.shape

    def body(g_ref, o_ref):
        acc = g_ref[0]
        for j in range(1, NDEV):
            acc = acc + g_ref[j]
        o_ref[...] = acc

    return pl.pallas_call(body, name=name, out_shape=jax.ShapeDtypeStruct((r, c), F32))(gathered)


def _adamw_small(g, w, m, v, name):
    def body(g_ref, w_ref, m_ref, v_ref, d_out, m_out, v_out):
        d_out[...], m_out[...], v_out[...] = _adamw(w_ref[...], g_ref[...], m_ref[...], v_ref[...])

    sds = jax.ShapeDtypeStruct(g.shape, F32)
    return pl.pallas_call(body, name=name, out_shape=(sds, sds, sds))(g, w, m, v)


def _cols_from_stack(s):
    return jnp.transpose(s, (1, 0, 2)).reshape(s.shape[1], NDEV * s.shape[2])


def _cols_to_stack(w):
    r, c = w.shape
    return jnp.transpose(w.reshape(r, NDEV, c // NDEV), (1, 0, 2))


def _rows_from_stack(s):
    return s.reshape(NDEV * s.shape[1], s.shape[2])


def _rows_to_stack(w):
    r, c = w.shape
    return w.reshape(NDEV, r // NDEV, c)


SMALL_ROWS = 16
SMALL_SLOTS = {
    "ln1_g": (0, 0, D), "ln1_b": (1, 0, D), "ln2_g": (2, 0, D), "ln2_b": (3, 0, D), "ln3_g": (4, 0, D),
    "ln3_b": (5, 0, D), "b_ple_gate": (6, 0, D), "ln4_g": (7, 0, D), "ln4_b": (8, 0, D),
    "g_attn": (9, 0, DA), "g_conv": (9, DA, DCV), "b_forget": (10, 0, NH),
}
CONVW_ROW = 11
LOSS_SLOT = (10, LANES)


def _pack_small(vals, conv_rows, loss=None):
    out = jnp.zeros((SMALL_ROWS, D), F32)
    for nm, (r, off, wd) in SMALL_SLOTS.items():
        out = out.at[r:r + 1, off:off + wd].set(vals[nm].reshape(1, wd).astype(F32))
    out = out.at[CONVW_ROW:CONVW_ROW + 3, 0:conv_rows.shape[1]].set(conv_rows.astype(F32))
    if loss is not None:
        out = out.at[LOSS_SLOT[0], LOSS_SLOT[1]].set(loss)
    return out


def _unpack_small(packed, name):
    r, off, wd = SMALL_SLOTS[name]
    return packed[r:r + 1, off:off + wd]


def kernel(x, p, ffn1_w_in, ffn1_w_out, ln1_g, ln1_b, w_mix_in, b_forget, conv_w, g_attn, g_conv, w_mix_out, ln2_g, ln2_b, ffn2_w_in, ffn2_w_out, ln3_g, ln3_b, w_ple, w_ple_gate, b_ple_gate, ln4_g, ln4_b, loss_target, m_ffn1_w_in, m_ffn1_w_out, m_ln1_g, m_ln1_b, m_w_mix_in, m_b_forget, m_conv_w, m_g_attn, m_g_conv, m_w_mix_out, m_ln2_g, m_ln2_b, m_ffn2_w_in, m_ffn2_w_out, m_ln3_g, m_ln3_b, m_w_ple, m_w_ple_gate, m_b_ple_gate, m_ln4_g, m_ln4_b, v_ffn1_w_in, v_ffn1_w_out, v_ln1_g, v_ln1_b, v_w_mix_in, v_b_forget, v_conv_w, v_g_attn, v_g_conv, v_w_mix_out, v_ln2_g, v_ln2_b, v_ffn2_w_in, v_ffn2_w_out, v_ln3_g, v_ln3_b, v_w_ple, v_w_ple_gate, v_b_ple_gate, v_ln4_g, v_ln4_b):
    args = dict(locals())
    t = x.shape[1]
    me = 4 * lax.axis_index("x") + 2 * lax.axis_index("y") + lax.axis_index("c")
    x0 = x.reshape(t, D)
    p0 = p.reshape(t, PLE)
    tgt = loss_target.reshape(t, D)

    big = ["ffn1_w_in", "ffn1_w_out", "w_mix_in", "w_mix_out", "ffn2_w_in", "ffn2_w_out", "w_ple", "w_ple_gate"]
    col_sharded = {"ffn1_w_in", "w_mix_in", "ffn2_w_in", "w_ple"}
    shard = {nm: args[nm][0] for nm in big}

    unstack = lambda nm, g: (_cols_from_stack(g) if nm in col_sharded else _rows_from_stack(g)).astype(MXU_DT)
    stack = lambda nm, g: _cols_to_stack(g) if nm in col_sharded else _rows_to_stack(g)
    wire = lambda names: [shard[nm].astype(WIRE_DT) for nm in names]
    first, later = big[:2], big[2:]

    full = {nm: unstack(nm, g) for nm, g in zip(first, _allgather(wire(first), "ag_ffn1"))}
    ffn1_out, gathered = _ffn1_fwd(x0, full, wire(later) + [conv_w[0]])
    full.update({nm: unstack(nm, g) for nm, g in zip(later, gathered)})
    cw = _cols_from_stack(gathered[len(later)])

    dr1, gw, small, loss_part = _mid_step(p0, tgt, full, cw, {nm: args[nm] for nm in SMALL_SLOTS}, ffn1_out)
    small_part = _pack_small({nm: small[nm] for nm in SMALL_SLOTS},
                             jnp.pad(small["conv_w"], ((0, 0), (0, D - DCV))), loss_part)
    stacks = {nm: stack(nm, gw[nm]) for nm in later}
    gx, gw1, landed_later, landed_w_in = _ffn1_bwd(
        x0, dr1, ffn1_out, full, [stacks[nm].astype(WIRE_DT) for nm in later],
        lambda g: stack("ffn1_w_in", g).astype(WIRE_DT))
    stacks.update({nm: stack(nm, gw1[nm]) for nm in first})
    (landed_w_out,), (small_all,) = _exchange_and_gather([stacks["ffn1_w_out"].astype(WIRE_DT)], [small_part],
                                                         "rs_ffn1_out")
    landed = dict(zip(later, landed_later), ffn1_w_in=landed_w_in, ffn1_w_out=landed_w_out)
    small_g = _sum_small(small_all, "sum_small")
    loss = small_g[LOSS_SLOT[0], LOSS_SLOT[1]]

    outs = {"loss": loss, "grad_x": gx.reshape(1, t, D)}
    for nm in big:
        own = lax.dynamic_index_in_dim(stacks[nm], me, axis=0, keepdims=False)
        g, dl, mn, vn = _reduce_adamw(landed[nm], own, shard[nm], args["m_" + nm][0], args["v_" + nm][0],
                                      "adamw_" + nm)
        outs["grad_" + nm], outs["delta_" + nm], outs["new_m_" + nm], outs["new_v_" + nm] = (
            g[None], dl[None], mn[None], vn[None])
    small_names = list(SMALL_SLOTS)
    cshard = lax.dynamic_slice_in_dim(small_g[CONVW_ROW:CONVW_ROW + 3, 0:DCV], me * (DCV // NDEV), DCV // NDEV, axis=1)
    g_pack = _pack_small({nm: _unpack_small(small_g, nm) for nm in small_names}, cshard)
    packs = [_pack_small({nm: args[pre + nm] for nm in small_names}, args[pre + "conv_w"][0])
             for pre in ("", "m_", "v_")]
    d_pack, m_pack, v_pack = _adamw_small(g_pack, packs[0], packs[1], packs[2], "adamw_small")
    for key, pk in (("grad_", g_pack), ("delta_", d_pack), ("new_m_", m_pack), ("new_v_", v_pack)):
        for nm in small_names:
            outs[key + nm] = _unpack_small(pk, nm)
        outs[key + "conv_w"] = pk[CONVW_ROW:CONVW_ROW + 3, 0:DCV // NDEV][None]

    wnames = ["ffn1_w_in", "ffn1_w_out", "ln1_g", "ln1_b", "w_mix_in", "b_forget", "conv_w", "g_attn", "g_conv",
              "w_mix_out", "ln2_g", "ln2_b", "ffn2_w_in", "ffn2_w_out", "ln3_g", "ln3_b", "w_ple", "w_ple_gate",
              "b_ple_gate", "ln4_g", "ln4_b"]
    return (outs["loss"], outs["grad_x"], *[outs[pre + nm] for pre in ("grad_", "delta_", "new_m_", "new_v_")
                                            for nm in wnames])


def _ffn1_fwd(x0, full, gather=()):
    res = _ffn_fwd(x0, jnp.ones((1, D), F32), jnp.zeros((1, D), F32), full["ffn1_w_in"], full["ffn1_w_out"],
                   "ffn1_fwd", gather)
    return res[:4], res[4:]


def _ffn1_bwd(x0, dr1, ffn1_out, full, exchange=(), w_in_slots=None):
    g1a, u1a, _, rs1 = ffn1_out
    ones, zeros = jnp.ones((1, D), F32), jnp.zeros((1, D), F32)
    res = _ffn_bwd(dr1, g1a, u1a, x0, rs1, ones, full["ffn1_w_in"], full["ffn1_w_out"], False, "ffn1_bwd",
                   exchange)
    df1, dg1, du1, gx = res[:4]
    gw_in = jnp.concatenate(
        [_dw("affine", (x0, ones, zeros), dg1, D, F, "dw_ffn1_in_g", tn=F // 2),
         _dw("affine", (x0, ones, zeros), du1, D, F, "dw_ffn1_in_u", tn=F // 2)], axis=1)
    side = () if w_in_slots is None else (w_in_slots(gw_in),)
    out = _dw("swiglu", (g1a, u1a), df1, F, D, "dw_ffn1_out", tmm=F // 2, exchange=side)
    gw_out, landed_in = (out, None) if w_in_slots is None else (out[0], out[1])
    return gx, {"ffn1_w_in": gw_in, "ffn1_w_out": gw_out}, res[5:], landed_in


def _mid_step(p0, tgt, full, cw, sp, ffn1_out):
    g1a, u1a, xh1, rs1 = ffn1_out
    t = xh1.shape[0]
    ln1_g, ln1_b, ln2_g, ln2_b, ln3_g, ln3_b = (sp[k] for k in ("ln1_g", "ln1_b", "ln2_g", "ln2_b", "ln3_g", "ln3_b"))
    ln4_g, ln4_b, g_attn, g_conv, b_ple_gate = (sp[k] for k in ("ln4_g", "ln4_b", "g_attn", "g_conv", "b_ple_gate"))
    wmi = full["w_mix_in"]
    w_qkv = wmi[:, 0:3 * DA]
    w_f = jnp.pad(wmi[:, 3 * DA:3 * DA + NH], ((0, 0), (0, LANES - NH)))
    w_bch = wmi[:, 3 * DA + NH:]
    bf_pad = jnp.pad(sp["b_forget"], ((0, 0), (0, LANES - NH)))

    ka, va, qat, kat, vta, bch, z, rt = _mix_proj_fwd(xh1, ln1_g, ln1_b, w_qkv[:, DA:], jnp.transpose(w_qkv),
                                                      w_bch, w_f, bf_pad, "mix_proj_fwd")
    rtile = jnp.transpose(rt[:, 0, 0:NH])
    o, lse = _attn_fwd(qat, ka, vta, rtile, "attn_fwd")
    merged, xh2, rs2 = _mix_post_fwd(o, bch, cw, g_attn, g_conv, xh1, ln1_g, ln1_b, full["w_mix_out"],
                                     "mix_post_fwd")
    g2a, u2a, xh3, rs3 = _ffn_fwd(xh2, ln2_g, ln2_b, full["ffn2_w_in"], full["ffn2_w_out"], "ffn2_fwd")

    dr3, dz, de, st_tail = _tail(xh3, rs3, ln3_g, ln3_b, p0, full["w_ple_gate"], full["w_ple"], b_ple_gate,
                                 ln4_g, ln4_b, tgt, "tail")
    df2, dg2, du2, dr2, st_f2 = _ffn_bwd(dr3, g2a, u2a, xh2, rs2, ln2_g, full["ffn2_w_in"], full["ffn2_w_out"],
                                         True, "ffn2_bwd")
    dmix, dot, drow, dyc, st_post = _mix_post_bwd(dr2, o, bch, cw, g_attn, g_conv, full["w_mix_out"],
                                                  "mix_post_bwd")
    dbch, st_conv = _conv_bwd(dyc, bch, cw, "conv_bwd")
    dk, dv, dck, dqt, dcq = _attn_bwd(ka, kat, va, qat, dot, lse, drow, rtile, "attn_bwd")
    dc_pad = jnp.pad(jnp.transpose((dcq + dck).reshape(NH, t)), ((0, 0), (0, LANES - NH)))
    dr1, dfl, dq, st_proj = _mix_proj_bwd(dr2, dqt, dk, dv, dbch, dc_pad, z, xh1, rs1, ln1_g, w_qkv, w_bch, w_f,
                                          "mix_proj_bwd")

    x1p, x2p, x3p = (xh1, ln1_g, ln1_b), (xh2, ln2_g, ln2_b), (xh3, ln3_g, ln3_b)
    gw = {}
    gw["ffn2_w_in"] = jnp.concatenate(
        [_dw("affine", x2p, dg2, D, F, "dw_ffn2_in_g", tn=F // 2),
         _dw("affine", x2p, du2, D, F, "dw_ffn2_in_u", tn=F // 2)], axis=1)
    gw["ffn2_w_out"] = _dw("swiglu", (g2a, u2a), df2, F, D, "dw_ffn2_out", tmm=F // 2)
    gw["w_mix_out"] = _dw("plain", (merged,), dmix, D, D, "dw_mix_out")
    gw["w_mix_in"] = jnp.concatenate(
        [_dw("affine", x1p, dq, D, DA, "dw_mix_in_q"), _dw("affine", x1p, dk, D, DA, "dw_mix_in_k"),
         _dw("affine", x1p, dv, D, DA, "dw_mix_in_v"),
         _dw("affine", x1p, dfl, D, LANES, "dw_mix_in_f")[:, 0:NH],
         _dw("affine", x1p, dbch, D, 3 * DCV, "dw_mix_in_bch")], axis=1)
    gw["w_ple_gate"] = _dw("affine", x3p, dz, D, D, "dw_ple_gate")
    gw["w_ple"] = _dw("plain", (p0,), de, PLE, D, "dw_ple")

    loss_part = (0.5 / D) * jnp.sum(st_tail[5:6, :])
    small = {"ln1_g": st_proj[0:1], "ln1_b": st_proj[1:2], "ln2_g": st_f2[0:1], "ln2_b": st_f2[1:2],
             "ln3_g": st_tail[3:4], "ln3_b": st_tail[4:5], "b_ple_gate": st_tail[2:3], "ln4_g": st_tail[0:1],
             "ln4_b": st_tail[1:2], "g_attn": st_post[0:1], "g_conv": st_post[1:2],
             "b_forget": st_proj[2:3, 0:NH], "conv_w": st_conv[0:3]}
    return dr1, gw, small, loss_part
```

```python
import functools

import jax
import jax.numpy as jnp
from jax import lax
from jax.experimental import pallas as pl
from jax.experimental.pallas import tpu as pltpu

D = 1024
F = 2816
NH = 8
DH = 64
DA = NH * DH
DCV = D - DA
PLE = 256
LN_EPS = 1e-5
RMS_EPS = 1e-6
NEG = -1e30
ALPHA = 2.0 ** 0.25
NDEV = 8
LANES = 128

ADAM_LR, ADAM_B1, ADAM_B2, ADAM_EPS, ADAM_WD, ADAM_STEP = 0.001, 0.9, 0.999, 1e-08, 0.01, 10

F32 = jnp.float32
MXU_DT = jnp.bfloat16
WIRE_DT = jnp.bfloat16

MESH_ID = pl.DeviceIdType.MESH
ANY = pl.BlockSpec(memory_space=pl.ANY)


def _params(vmem_mb, n_axes=1):
    return pltpu.CompilerParams(dimension_semantics=("arbitrary",) * n_axes,
                                vmem_limit_bytes=int(vmem_mb) << 20)


def _mm(a, b):
    return jnp.dot(a, b, preferred_element_type=F32)


def _mm_nt(a, b):
    return lax.dot_general(a, b, (((1,), (1,)), ((), ())), preferred_element_type=F32)


def _mm_tn(a, b):
    return lax.dot_general(a, b, (((0,), (0,)), ((), ())), preferred_element_type=F32)


def _split3(x):
    hi = x.astype(MXU_DT)
    r1 = x - hi.astype(F32)
    mid = r1.astype(MXU_DT)
    lo = (r1 - mid.astype(F32)).astype(MXU_DT)
    return hi, mid, lo


def _mm_sel(sel, x):
    hi, mid, lo = _split3(x)
    return _mm(sel, hi) + _mm(sel, mid) + _mm(sel, lo)


def _sigmoid(x):
    return 1.0 / (1.0 + jnp.exp(-x))


def _ln_fwd(r):
    mu = jnp.mean(r, axis=-1, keepdims=True)
    xc = r - mu
    var = jnp.mean(xc * xc, axis=-1, keepdims=True)
    rstd = lax.rsqrt(var + LN_EPS)
    return xc * rstd, rstd


def _ln_bwd(dxhat, xhat, rstd):
    m1 = jnp.mean(dxhat, axis=-1, keepdims=True)
    m2 = jnp.mean(dxhat * xhat, axis=-1, keepdims=True)
    return rstd * (dxhat - m1 - xhat * m2)


def _rms_fwd(x):
    r = lax.rsqrt(jnp.mean(x * x, axis=-1, keepdims=True) + RMS_EPS)
    return x * r, r


def _rms_bwd(dyg, xn, r):
    return r * (dyg - xn * jnp.mean(dyg * xn, axis=-1, keepdims=True))


def _colsum(x):
    return jnp.sum(x, axis=0, keepdims=True)


def _f_chunks():
    out, c0 = [], 0
    while c0 < F:
        fc = min(512, F - c0)
        out.append((c0, fc))
        c0 += fc
    return out


def _tile(t, want):
    return want if t % want == 0 and t >= want else t


def _exchange_sems(n):
    return [pltpu.SemaphoreType.DMA((n * (NDEV - 1),)), pltpu.SemaphoreType.DMA((n * (NDEV - 1),))]


def _exchange_phases(ins, outs, send_sems, recv_sems):
    n = len(ins)

    def peers():
        x, y, c = lax.axis_index("x"), lax.axis_index("y"), lax.axis_index("c")
        out = []
        for k in range(1, NDEV):
            px = 1 - x if (k >> 2) & 1 else x
            py = 1 - y if (k >> 1) & 1 else y
            pc = 1 - c if k & 1 else c
            out.append(((px, py, pc), 4 * px + 2 * py + pc))
        return 4 * x + 2 * y + c, out

    def remote(w, k, to, slot_src, slot_dst):
        return pltpu.make_async_remote_copy(
            src_ref=ins[w].at[slot_src], dst_ref=outs[w].at[slot_dst],
            send_sem=send_sems.at[w * (NDEV - 1) + k], recv_sem=recv_sems.at[w * (NDEV - 1) + k],
            device_id=to, device_id_type=MESH_ID)

    def start():
        me, prs = peers()
        for k, (to, pid) in enumerate(prs):
            for w in range(n):
                remote(w, k, to, pid, me).start()

    def finish():
        me, prs = peers()
        for k, (to, pid) in enumerate(prs):
            for w in range(n):
                remote(w, k, to, me, pid).wait_recv()
        for k, (to, pid) in enumerate(prs):
            for w in range(n):
                remote(w, k, to, pid, me).wait_send()

    return start, finish


def _gather_sems(n):
    return [pltpu.SemaphoreType.DMA((n * (NDEV - 1),)), pltpu.SemaphoreType.DMA((n * (NDEV - 1),)),
            pltpu.SemaphoreType.DMA((n,))]


def _gather_phases(ins, outs, send_sems, recv_sems, loc_sems):
    n = len(ins)
    per = NDEV - 1

    def place():
        x, y, c = lax.axis_index("x"), lax.axis_index("y"), lax.axis_index("c")
        return (x, y, c), (x, y, 1 - c), [(1 - x, y), (x, 1 - y), (1 - x, 1 - y)]

    def copy(w, k, block, to, src=None):
        dst = outs[w].at[4 * block[0] + 2 * block[1] + block[2]]
        return pltpu.make_async_remote_copy(
            src_ref=dst if src is None else src, dst_ref=dst,
            send_sem=send_sems.at[w * per + k], recv_sem=recv_sems.at[w * per + k],
            device_id=to, device_id_type=MESH_ID)

    def local(w, me):
        return pltpu.make_async_copy(ins[w], outs[w].at[4 * me[0] + 2 * me[1] + me[2]], loc_sems.at[w])

    def first(me, sib, chips):
        out = []
        for j, chip in enumerate(chips):
            out += [copy(w, 1 + j, me, (*chip, me[2]), src=ins[w]) for w in range(n)]
        return out + [copy(w, 0, me, sib, src=ins[w]) for w in range(n)]

    def start():
        me, sib, chips = place()
        for w in range(n):
            local(w, me).start()
        for cp in first(me, sib, chips):
            cp.start()

    def forward():
        me, sib, chips = place()
        for j, chip in enumerate(chips):
            for w in range(n):
                copy(w, 1 + j, (*chip, me[2]), me).wait_recv()
                copy(w, 4 + j, (*chip, me[2]), sib).start()

    def finish():
        me, sib, chips = place()
        for w in range(n):
            copy(w, 0, sib, me).wait_recv()
        for j, chip in enumerate(chips):
            for w in range(n):
                copy(w, 4 + j, (*chip, 1 - me[2]), me).wait_recv()
        for cp in first(me, sib, chips):
            cp.wait_send()
        for j, chip in enumerate(chips):
            for w in range(n):
                copy(w, 4 + j, (*chip, me[2]), sib).wait_send()
        for w in range(n):
            local(w, me).wait()

    return start, forward, finish


def _allgather(arrs, name):
    n = len(arrs)

    def body(*refs):
        start, forward, finish = _gather_phases(refs[:n], refs[n:2 * n], *refs[2 * n:])
        start()
        forward()
        finish()

    return pl.pallas_call(
        body, name=name, out_shape=tuple(jax.ShapeDtypeStruct((NDEV,) + a.shape, a.dtype) for a in arrs),
        in_specs=[ANY] * n, out_specs=tuple([ANY] * n), scratch_shapes=_gather_sems(n),
    )(*arrs)


def _exchange_and_gather(ex, ga, name):
    ne, ng = len(ex), len(ga)

    def body(*refs):
        ins, outs, sems = refs[:ne + ng], refs[ne + ng:2 * (ne + ng)], refs[2 * (ne + ng):]
        e_start, e_finish = _exchange_phases(ins[:ne], outs[:ne], *sems[:2])
        g_start, g_forward, g_finish = _gather_phases(ins[ne:], outs[ne:], *sems[2:])
        e_start()
        g_start()
        g_forward()
        g_finish()
        e_finish()

    res = pl.pallas_call(
        body, name=name,
        out_shape=tuple(jax.ShapeDtypeStruct(a.shape, a.dtype) for a in ex)
        + tuple(jax.ShapeDtypeStruct((NDEV,) + a.shape, a.dtype) for a in ga),
        in_specs=[ANY] * (ne + ng), out_specs=tuple([ANY] * (ne + ng)),
        scratch_shapes=_exchange_sems(ne) + _gather_sems(ng),
    )(*ex, *ga)
    return res[:ne], res[ne:]


def _ffn_fwd(xin, gin, bin_, w_in, w_out, name, gather=()):
    t = xin.shape[0]
    tm = _tile(t, 512)
    nt = t // tm
    chunks = _f_chunks()
    ng = len(gather)

    def body(*refs):
        x_ref, gi_ref, bi_ref, win_hbm, wout_hbm = refs[:5]
        g_ref, u_ref, xh_ref, rs_ref = refs[5 + ng:9 + ng]
        win_v, wout_v, acc_ref = refs[9 + 2 * ng:12 + 2 * ng]
        if ng:
            g_start, g_forward, g_finish = _gather_phases(refs[5:5 + ng], refs[9 + ng:9 + 2 * ng],
                                                          *refs[12 + 2 * ng:])

        @pl.when(pl.program_id(0) == 0)
        def _():
            if ng:
                g_start()
            pltpu.sync_copy(win_hbm, win_v)
            pltpu.sync_copy(wout_hbm, wout_v)

        if ng:
            pl.when(pl.program_id(0) == nt // 2)(g_forward)
            pl.when(pl.program_id(0) == nt - 1)(g_finish)

        x = x_ref[...] * gi_ref[...] + bi_ref[...]
        xb = x.astype(MXU_DT)
        for ci, (c0, fc) in enumerate(chunks):
            gc = _mm(xb, win_v[:, c0:c0 + fc])
            uc = _mm(xb, win_v[:, F + c0:F + c0 + fc])
            g_ref[:, c0:c0 + fc] = gc.astype(g_ref.dtype)
            u_ref[:, c0:c0 + fc] = uc.astype(u_ref.dtype)
            hc = (gc * _sigmoid(gc) * uc).astype(MXU_DT)
            part = _mm(hc, wout_v[c0:c0 + fc, :])
            if ci == 0:
                acc_ref[...] = part
            else:
                acc_ref[...] += part
        xh, rstd = _ln_fwd(ALPHA * x + 0.5 * acc_ref[...])
        xh_ref[...] = xh
        rs_ref[...] = rstd

    row = pl.BlockSpec((tm, D), lambda i: (i, 0))
    vec = pl.BlockSpec((1, D), lambda i: (0, 0))
    act = pl.BlockSpec((tm, F), lambda i: (i, 0))
    return pl.pallas_call(
        body, name=name, grid=(nt,),
        in_specs=[row, vec, vec, ANY, ANY] + [ANY] * ng,
        out_specs=(act, act, row, pl.BlockSpec((tm, 1), lambda i: (i, 0))) + (ANY,) * ng,
        out_shape=(jax.ShapeDtypeStruct((t, F), MXU_DT), jax.ShapeDtypeStruct((t, F), MXU_DT),
                   jax.ShapeDtypeStruct((t, D), F32), jax.ShapeDtypeStruct((t, 1), F32))
        + tuple(jax.ShapeDtypeStruct((NDEV,) + a.shape, a.dtype) for a in gather),
        scratch_shapes=[pltpu.VMEM((D, 2 * F), MXU_DT), pltpu.VMEM((F, D), MXU_DT), pltpu.VMEM((tm, D), F32)]
        + (_gather_sems(ng) if ng else []),
        compiler_params=_params(52),
    )(xin, gin, bin_, w_in, w_out, *gather)


QROWS = 80
BIAS_AT = DH


def _place_matrices():
    import numpy as np
    pk = np.zeros((NH, LANES, LANES), np.float32)
    pqt = np.zeros((NH, LANES, LANES), np.float32)
    for h in range(NH):
        for piece in range(3):
            pk[h, 8 * piece + h, BIAS_AT + 3 + piece] = -1.0
            pqt[h, BIAS_AT + piece, 8 * piece + h] = 1.0
    pkt = np.transpose(pk, (0, 2, 1))
    return tuple(jnp.asarray(m, MXU_DT) for m in (pk, pqt, pkt))


def _mix_proj_fwd(xh1, g1, b1, w_kv, w_qkv_t, w_bch, w_f, bf_pad, name):
    t = xh1.shape[0]
    tm = _tile(t, 512)
    pk, pqt, pkt = _place_matrices()

    def body(x_ref, g_ref, b_ref, wkv_ref, wt_ref, wb_ref, wf_ref, bf_ref, pk_ref, pqt_ref, pkt_ref,
             ka_ref, va_ref, qat_ref, kat_ref, vta_ref, bch_ref, z_ref, r_ref, carry):
        @pl.when(pl.program_id(0) == 0)
        def _():
            carry[...] = jnp.zeros_like(carry)

        xb = (x_ref[...] * g_ref[...] + b_ref[...]).astype(MXU_DT)
        kv = _mm(xb, wkv_ref[...])
        qkvt = _mm_nt(wt_ref[...], xb)
        bch_ref[...] = _mm(xb, wb_ref[...])
        z = _mm(xb, wf_ref[...]) + bf_ref[...]
        z_ref[...] = z
        logf = jnp.minimum(z, 0.0) - jnp.log(1.0 + jnp.exp(-jnp.abs(z)))
        row = lax.broadcasted_iota(jnp.int32, (tm, tm), 0)
        col = lax.broadcasted_iota(jnp.int32, (tm, tm), 1)
        tri = jnp.where(row >= col, 1.0, 0.0).astype(MXU_DT)
        c = carry[...] + _mm_sel(tri, logf)
        carry[...] = c[tm - 1:tm, :]
        r_ref[0] = c[0:1, :]
        lane = lax.broadcasted_iota(jnp.int32, (1, LANES), 1)
        hi, mid, lo = _split3(jnp.where(lane < NH, c - c[0:1, :], 0.0))
        pieces = (hi.astype(F32) + pltpu.roll(mid.astype(F32), 8, 1) + pltpu.roll(lo.astype(F32), 16, 1)
                  ).astype(MXU_DT)
        sub = lax.broadcasted_iota(jnp.int32, (DH, 1), 0)
        ones_k_lanes = jnp.where((lane >= BIAS_AT) & (lane < BIAS_AT + 3), 1.0, 0.0)
        ones_q_rows = jnp.where((sub >= 3) & (sub < 6), 1.0, 0.0)
        ones_k_rows = jnp.where(sub[0:QROWS - DH] < 3, 1.0, 0.0)
        first_row = jnp.where(sub == 0, 1.0, 0.0) + jnp.zeros((DH, tm), F32)
        for h in range(NH):
            pair, odd = divmod(h, 2)
            k2 = kv[:, LANES * pair:LANES * (pair + 1)]
            v2 = kv[:, DA + LANES * pair:DA + LANES * (pair + 1)]
            if odd:
                k2, v2 = pltpu.roll(k2, DH, 1), pltpu.roll(v2, DH, 1)
            ka_ref[h] = jnp.where(lane < DH, k2, _mm(pieces, pk_ref[h]) + ones_k_lanes).astype(ka_ref.dtype)
            va_ref[h] = jnp.where(lane < DH, v2, 0.0).astype(va_ref.dtype)
            qat_ref[h, 0:DH, :] = (qkvt[DH * h:DH * (h + 1)] * 0.125).astype(qat_ref.dtype)
            qat_ref[h, DH:LANES, :] = (_mm_nt(pqt_ref[h], pieces)[DH:LANES] + ones_q_rows).astype(qat_ref.dtype)
            kat_ref[h, 0:DH, :] = qkvt[DA + DH * h:DA + DH * (h + 1)].astype(kat_ref.dtype)
            kat_ref[h, DH:QROWS, :] = (_mm_nt(pkt_ref[h], pieces)[DH:QROWS] + ones_k_rows).astype(kat_ref.dtype)
            vt = qkvt[2 * DA + DH * h:2 * DA + DH * (h + 1)]
            vta_ref[h, 0:DH, :] = (first_row if odd else vt).astype(vta_ref.dtype)
            vta_ref[h, DH:LANES, :] = (vt if odd else first_row).astype(vta_ref.dtype)

    row = lambda w: pl.BlockSpec((tm, w), lambda i: (i, 0))
    full = lambda a: pl.BlockSpec(a.shape, lambda i: (0,) * a.ndim)
    nat = pl.BlockSpec((NH, tm, LANES), lambda i: (0, i, 0))
    fmaj = lambda rows: pl.BlockSpec((NH, rows, tm), lambda i: (0, 0, i))
    return pl.pallas_call(
        body, name=name, grid=(t // tm,),
        in_specs=[row(D), full(g1), full(b1), full(w_kv), full(w_qkv_t), full(w_bch), full(w_f), full(bf_pad),
                  full(pk), full(pqt), full(pkt)],
        out_specs=(nat, nat, fmaj(LANES), fmaj(QROWS), fmaj(LANES), row(3 * DCV), row(LANES),
                   pl.BlockSpec((1, 1, LANES), lambda i: (i, 0, 0))),
        out_shape=(jax.ShapeDtypeStruct((NH, t, LANES), MXU_DT), jax.ShapeDtypeStruct((NH, t, LANES), MXU_DT),
                   jax.ShapeDtypeStruct((NH, LANES, t), MXU_DT), jax.ShapeDtypeStruct((NH, QROWS, t), MXU_DT),
                   jax.ShapeDtypeStruct((NH, LANES, t), MXU_DT), jax.ShapeDtypeStruct((t, 3 * DCV), F32),
                   jax.ShapeDtypeStruct((t, LANES), F32), jax.ShapeDtypeStruct((t // tm, 1, LANES), F32)),
        scratch_shapes=[pltpu.VMEM((1, LANES), F32)],
        compiler_params=_params(56),
    )(xh1, g1, b1, w_kv, w_qkv_t, w_bch, w_f, bf_pad, pk, pqt, pkt)


def _attn_fwd(qat, ka, vta, r, name):
    t = ka.shape[1]
    tq = _tile(t, 512)
    nq = t // tq

    def body(r_ref, q_ref, k_ref, v_ref, o_ref, l_ref, st0, st1):
        hp, i = pl.program_id(0), pl.program_id(1)
        key = lax.broadcasted_iota(jnp.int32, (tq, tq), 0)
        qry = lax.broadcasted_iota(jnp.int32, (tq, tq), 1)

        def tile_of(pos):
            return jnp.where(pos == 0, i, pos - 1)

        def scores(pos, buf, masked):
            off = pl.multiple_of(tile_of(pos) * tq, tq)
            for a in range(2):
                st = _mm(k_ref[a, pl.ds(off, tq), :], q_ref[a])
                buf[a] = jnp.where(qry >= key, st, NEG) if masked else st

        def consume(pos, buf, carry):
            j = tile_of(pos)
            off = pl.multiple_of(j * tq, tq)
            out = []
            for a in range(2):
                m, acc = carry[a]
                st = buf[a]
                d = r_ref[2 * hp + a, i] - r_ref[2 * hp + a, j]
                m_new = jnp.maximum(m, jnp.max(st, axis=0, keepdims=True) + d)
                pt = jnp.exp(st - (m_new - d))
                acc = jnp.exp(m - m_new) * acc + _mm(v_ref[a, :, pl.ds(off, tq)], pt.astype(MXU_DT))
                out.append((m_new, acc))
            return tuple(out)

        def trip(p, carry):
            scores(2 * p + 1, st1, False)
            carry = consume(2 * p, st0, carry)
            scores(2 * p + 2, st0, False)
            return consume(2 * p + 1, st1, carry)

        scores(0, st0, True)
        init = tuple((jnp.full((1, tq), NEG, F32), jnp.zeros((LANES, tq), F32)) for _ in range(2))
        trips = i // 2
        carry = lax.fori_loop(0, trips, trip, init)

        def last_two(cr):
            scores(2 * trips + 1, st1, False)
            return consume(2 * trips + 1, st1, consume(2 * trips, st0, cr))

        (ma, acca), (mb, accb) = lax.cond(i % 2 == 1, last_two, lambda cr: consume(2 * trips, st0, cr), carry)
        la, lb = acca[DH:DH + 1, :], accb[0:1, :]
        l_ref[0] = ma + jnp.log(la)
        l_ref[1] = mb + jnp.log(lb)
        sub = lax.broadcasted_iota(jnp.int32, (LANES, tq), 0)
        o_ref[...] = jnp.where(sub < DH, acca / la, accb / lb).T

    return pl.pallas_call(
        body, name=name, grid=(NH // 2, nq),
        in_specs=[pl.BlockSpec(memory_space=pltpu.SMEM),
                  pl.BlockSpec((2, LANES, tq), lambda p, i: (p, 0, i)),
                  pl.BlockSpec((2, t, LANES), lambda p, i: (p, 0, 0)),
                  pl.BlockSpec((2, LANES, t), lambda p, i: (p, 0, 0))],
        out_specs=(pl.BlockSpec((tq, LANES), lambda p, i: (i, p)),
                   pl.BlockSpec((2, 1, tq), lambda p, i: (p, 0, i))),
        out_shape=(jax.ShapeDtypeStruct((t, DA), F32), jax.ShapeDtypeStruct((NH, 1, t), F32)),
        scratch_shapes=[pltpu.VMEM((2, tq, tq), F32), pltpu.VMEM((2, tq, tq), F32)],
        compiler_params=_params(48, 2),
    )(r, qat, ka, vta)


def _conv_parts(bch):
    return bch[:, 0:DCV], bch[:, DCV:2 * DCV], bch[:, 2 * DCV:3 * DCV]


def _mix_post_fwd(o, bch, conv_w, g_attn, g_conv, xh1, g1, b1, w_mo, name):
    t = o.shape[0]
    tm = _tile(t, 512)
    hb = tm // 8

    def body(o_ref, bch_ref, halo_ref, cw_ref, ga_ref, gc_ref, x_ref, g_ref, b_ref, w_ref,
             mg_ref, xh_ref, rs_ref, ext):
        i = pl.program_id(0)
        an, _ = _rms_fwd(o_ref[...])
        mg_ref[:, 0:DA] = (an * ga_ref[...]).astype(mg_ref.dtype)
        bb, cc, hh = _conv_parts(bch_ref[...])
        _, hc, hh_h = _conv_parts(halo_ref[...])
        u = cc * hh
        ext[0:8, :] = jnp.where(i > 0, hc * hh_h, 0.0)
        ext[8:8 + tm, :] = u
        raw = cw_ref[0:1, :] * ext[6:6 + tm, :] + cw_ref[1:2, :] * ext[7:7 + tm, :] + cw_ref[2:3, :] * u
        cn, _ = _rms_fwd(bb * raw)
        mg_ref[:, DA:D] = (cn * gc_ref[...]).astype(mg_ref.dtype)
        x1 = x_ref[...] * g_ref[...] + b_ref[...]
        xh, rstd = _ln_fwd(ALPHA * x1 + _mm(mg_ref[...], w_ref[...]))
        xh_ref[...] = xh
        rs_ref[...] = rstd

    row = lambda w: pl.BlockSpec((tm, w), lambda i: (i, 0))
    full = lambda a: pl.BlockSpec(a.shape, lambda i: (0, 0))
    return pl.pallas_call(
        body, name=name, grid=(t // tm,),
        in_specs=[row(DA), row(3 * DCV),
                  pl.BlockSpec((8, 3 * DCV), lambda i: (jnp.maximum(i * hb - 1, 0), 0)),
                  full(conv_w), full(g_attn), full(g_conv), row(D), full(g1), full(b1), full(w_mo)],
        out_specs=(row(D), row(D), pl.BlockSpec((tm, 1), lambda i: (i, 0))),
        out_shape=(jax.ShapeDtypeStruct((t, D), MXU_DT), jax.ShapeDtypeStruct((t, D), F32),
                   jax.ShapeDtypeStruct((t, 1), F32)),
        scratch_shapes=[pltpu.VMEM((tm + 8, DCV), F32)],
        compiler_params=_params(48),
    )(o, bch, bch, conv_w, g_attn, g_conv, xh1, g1, b1, w_mo)


def _tail(xh3, rs3, g3, b3, p, w_g, w_ple, bg, g4, b4, target, name):
    t = xh3.shape[0]
    tm = _tile(t, 512)

    def body(x_ref, rs_ref, g3_ref, b3_ref, p_ref, wg_ref, wp_ref, bg_ref, g4_ref, b4_ref, t_ref,
             dr_ref, dz_ref, de_ref, st_ref):
        @pl.when(pl.program_id(0) == 0)
        def _():
            st_ref[...] = jnp.zeros_like(st_ref)

        xh3v = x_ref[...]
        x3 = xh3v * g3_ref[...] + b3_ref[...]
        gate = _sigmoid(_mm(x3.astype(MXU_DT), wg_ref[...]) + bg_ref[...])
        e = _mm(p_ref[...].astype(MXU_DT), wp_ref[...])
        xh4, rstd4 = _ln_fwd(ALPHA * x3 + gate * e)
        diff = xh4 * g4_ref[...] + b4_ref[...] - t_ref[...]
        dy = diff * (1.0 / D)
        st_ref[5:6, :] += _colsum(diff * diff)
        st_ref[0:1, :] += _colsum(dy * xh4)
        st_ref[1:2, :] += _colsum(dy)
        dr4 = _ln_bwd(dy * g4_ref[...], xh4, rstd4)
        de_ref[...] = (dr4 * gate).astype(de_ref.dtype)
        dz = dr4 * e * gate * (1.0 - gate)
        st_ref[2:3, :] += _colsum(dz)
        dzb = dz.astype(MXU_DT)
        dz_ref[...] = dzb
        dx3 = ALPHA * dr4 + _mm_nt(dzb, wg_ref[...])
        st_ref[3:4, :] += _colsum(dx3 * xh3v)
        st_ref[4:5, :] += _colsum(dx3)
        dr_ref[...] = _ln_bwd(dx3 * g3_ref[...], xh3v, rs_ref[...])

    row = lambda w: pl.BlockSpec((tm, w), lambda i: (i, 0))
    full = lambda a: pl.BlockSpec(a.shape, lambda i: (0, 0))
    return pl.pallas_call(
        body, name=name, grid=(t // tm,),
        in_specs=[row(D), row(1), full(g3), full(b3), row(PLE), full(w_g), full(w_ple), full(bg), full(g4),
                  full(b4), row(D)],
        out_specs=(row(D), row(D), row(D), pl.BlockSpec((8, D), lambda i: (0, 0))),
        out_shape=(jax.ShapeDtypeStruct((t, D), F32), jax.ShapeDtypeStruct((t, D), MXU_DT),
                   jax.ShapeDtypeStruct((t, D), MXU_DT), jax.ShapeDtypeStruct((8, D), F32)),
        compiler_params=_params(48),
    )(xh3, rs3, g3, b3, p, w_g, w_ple, bg, g4, b4, target)


def _ffn_bwd(dr, gact, uact, xin, rsin, gin, w_in, w_out, prev_ln, name, exchange=()):
    t = dr.shape[0]
    tm = _tile(t, 256)
    nt = t // tm
    chunks = _f_chunks()
    ne = len(exchange)

    def body(*refs):
        dr_ref, g_ref, u_ref, x_ref, rs_ref, gi_ref, win_hbm, wout_hbm = refs[:8]
        df_ref, dg_ref, du_ref, dx_ref, st_ref = refs[8 + ne:13 + ne]
        win_v, wout_v, acc_ref = refs[13 + 2 * ne:16 + 2 * ne]
        if ne:
            e_start, e_finish = _exchange_phases(refs[8:8 + ne], refs[13 + ne:13 + 2 * ne], *refs[16 + 2 * ne:])

        @pl.when(pl.program_id(0) == 0)
        def _():
            if ne:
                e_start()
            pltpu.sync_copy(win_hbm, win_v)
            pltpu.sync_copy(wout_hbm, wout_v)
            st_ref[...] = jnp.zeros_like(st_ref)

        if ne:
            pl.when(pl.program_id(0) == nt - 1)(e_finish)

        drv = dr_ref[...]
        dfb = (0.5 * drv).astype(MXU_DT)
        df_ref[...] = dfb
        for ci, (c0, fc) in enumerate(chunks):
            dh = _mm_nt(dfb, wout_v[c0:c0 + fc, :])
            g = g_ref[:, c0:c0 + fc].astype(F32)
            u = u_ref[:, c0:c0 + fc].astype(F32)
            sg = _sigmoid(g)
            dgb = (dh * u * (sg * (1.0 + g * (1.0 - sg)))).astype(MXU_DT)
            dub = (dh * (g * sg)).astype(MXU_DT)
            dg_ref[:, c0:c0 + fc] = dgb
            du_ref[:, c0:c0 + fc] = dub
            part = _mm_nt(dgb, win_v[:, c0:c0 + fc]) + _mm_nt(dub, win_v[:, F + c0:F + c0 + fc])
            if ci == 0:
                acc_ref[...] = part
            else:
                acc_ref[...] += part
        dx = ALPHA * drv + acc_ref[...]
        if prev_ln:
            xh = x_ref[...]
            st_ref[0:1, :] += _colsum(dx * xh)
            st_ref[1:2, :] += _colsum(dx)
            dx_ref[...] = _ln_bwd(dx * gi_ref[...], xh, rs_ref[...])
        else:
            dx_ref[...] = dx

    row = pl.BlockSpec((tm, D), lambda i: (i, 0))
    vec = pl.BlockSpec((1, D), lambda i: (0, 0))
    act = pl.BlockSpec((tm, F), lambda i: (i, 0))
    return pl.pallas_call(
        body, name=name, grid=(nt,),
        in_specs=[row, act, act, row, pl.BlockSpec((tm, 1), lambda i: (i, 0)), vec, ANY, ANY] + [ANY] * ne,
        out_specs=(row, act, act, row, pl.BlockSpec((8, D), lambda i: (0, 0))) + (ANY,) * ne,
        out_shape=(jax.ShapeDtypeStruct((t, D), MXU_DT), jax.ShapeDtypeStruct((t, F), MXU_DT),
                   jax.ShapeDtypeStruct((t, F), MXU_DT), jax.ShapeDtypeStruct((t, D), F32),
                   jax.ShapeDtypeStruct((8, D), F32))
        + tuple(jax.ShapeDtypeStruct(a.shape, a.dtype) for a in exchange),
        scratch_shapes=[pltpu.VMEM((D, 2 * F), MXU_DT), pltpu.VMEM((F, D), MXU_DT), pltpu.VMEM((tm, D), F32)]
        + (_exchange_sems(ne) if ne else []),
        compiler_params=_params(52),
    )(dr, gact, uact, xin, rsin, gin, w_in, w_out, *exchange)


def _mix_post_bwd(dr2, o, bch, conv_w, g_attn, g_conv, w_mo, name):
    t = dr2.shape[0]
    tm = _tile(t, 512)
    hb = tm // 8

    def body(dr_ref, o_ref, bch_ref, halo_ref, cw_ref, ga_ref, gc_ref, w_ref,
             dm_ref, do_ref, dl_ref, dy_ref, st_ref, ext):
        i = pl.program_id(0)

        @pl.when(i == 0)
        def _():
            st_ref[...] = jnp.zeros_like(st_ref)

        dmb = dr_ref[...].astype(MXU_DT)
        dm_ref[...] = dmb
        dmg = _mm_nt(dmb, w_ref[...])
        ov = o_ref[...]
        an, ra = _rms_fwd(ov)
        da = dmg[:, 0:DA]
        st_ref[0:1, :] += _colsum(da * an)
        dxa = _rms_bwd(da * ga_ref[...], an, ra)
        dor = dxa.astype(MXU_DT).astype(F32)
        dot = dor.T
        for h in range(NH):
            do_ref[h, 0:DH, :] = dot[DH * h:DH * (h + 1)].astype(do_ref.dtype)
            do_ref[h, DH:LANES, :] = jnp.zeros((LANES - DH, tm), do_ref.dtype)
        srow = lax.broadcasted_iota(jnp.int32, (8, DA), 0)
        scol = lax.broadcasted_iota(jnp.int32, (8, DA), 1)
        sel = jnp.where((scol // DH) == srow, 1.0, 0.0).astype(MXU_DT)
        hi, mid, lo = _split3(dor * ov)
        delta = _mm_nt(sel, hi) + _mm_nt(sel, mid) + _mm_nt(sel, lo)
        for h in range(NH):
            dl_ref[h] = delta[h:h + 1, :]
        bb, cc, hh = _conv_parts(bch_ref[...])
        _, hc, hh_h = _conv_parts(halo_ref[...])
        u = cc * hh
        ext[0:8, :] = jnp.where(i > 0, hc * hh_h, 0.0)
        ext[8:8 + tm, :] = u
        raw = cw_ref[0:1, :] * ext[6:6 + tm, :] + cw_ref[1:2, :] * ext[7:7 + tm, :] + cw_ref[2:3, :] * u
        cn, rc = _rms_fwd(bb * raw)
        dcn = dmg[:, DA:D]
        st_ref[1:2, :] += _colsum(dcn * cn)
        dy_ref[...] = _rms_bwd(dcn * gc_ref[...], cn, rc)

    row = lambda w: pl.BlockSpec((tm, w), lambda i: (i, 0))
    full = lambda a: pl.BlockSpec(a.shape, lambda i: (0, 0))
    return pl.pallas_call(
        body, name=name, grid=(t // tm,),
        in_specs=[row(D), row(DA), row(3 * DCV),
                  pl.BlockSpec((8, 3 * DCV), lambda i: (jnp.maximum(i * hb - 1, 0), 0)),
                  full(conv_w), full(g_attn), full(g_conv), full(w_mo)],
        out_specs=(row(D), pl.BlockSpec((NH, LANES, tm), lambda i: (0, 0, i)),
                   pl.BlockSpec((NH, 1, tm), lambda i: (0, 0, i)), row(DCV),
                   pl.BlockSpec((8, DA), lambda i: (0, 0))),
        out_shape=(jax.ShapeDtypeStruct((t, D), MXU_DT), jax.ShapeDtypeStruct((NH, LANES, t), MXU_DT),
                   jax.ShapeDtypeStruct((NH, 1, t), F32), jax.ShapeDtypeStruct((t, DCV), F32),
                   jax.ShapeDtypeStruct((8, DA), F32)),
        scratch_shapes=[pltpu.VMEM((tm + 8, DCV), F32)],
        compiler_params=_params(48),
    )(dr2, o, bch, bch, conv_w, g_attn, g_conv, w_mo)


def _conv_bwd(dy, bch, conv_w, name):
    t = dy.shape[0]
    tm = _tile(t, 512)
    hb = tm // 8
    nt = t // tm

    def body(dy_ref, dyn_ref, bch_ref, prev_ref, next_ref, cw_ref, out_ref, st_ref, ext_u, ext_d):
        i = pl.program_id(0)

        @pl.when(i == 0)
        def _():
            st_ref[...] = jnp.zeros_like(st_ref)

        bb, cc, hh = _conv_parts(bch_ref[...])
        _, pc, ph = _conv_parts(prev_ref[...])
        nb, _, _ = _conv_parts(next_ref[...])
        u = cc * hh
        ext_u[0:8, :] = jnp.where(i > 0, pc * ph, 0.0)
        ext_u[8:8 + tm, :] = u
        u1 = ext_u[7:7 + tm, :]
        u2 = ext_u[6:6 + tm, :]
        w0, w1, w2 = cw_ref[0:1, :], cw_ref[1:2, :], cw_ref[2:3, :]
        dyv = dy_ref[...]
        out_ref[:, 0:DCV] = (dyv * (w0 * u2 + w1 * u1 + w2 * u)).astype(out_ref.dtype)
        dcr = dyv * bb
        ext_d[0:tm, :] = dcr
        ext_d[tm:tm + 8, :] = jnp.where(i < nt - 1, dyn_ref[...] * nb, 0.0)
        du = w2 * dcr + w1 * ext_d[1:1 + tm, :] + w0 * ext_d[2:2 + tm, :]
        out_ref[:, DCV:2 * DCV] = (du * hh).astype(out_ref.dtype)
        out_ref[:, 2 * DCV:3 * DCV] = (du * cc).astype(out_ref.dtype)
        st_ref[0:1, :] += _colsum(dcr * u2)
        st_ref[1:2, :] += _colsum(dcr * u1)
        st_ref[2:3, :] += _colsum(dcr * u)

    row = lambda w: pl.BlockSpec((tm, w), lambda i: (i, 0))
    prev = lambda w: pl.BlockSpec((8, w), lambda i: (jnp.maximum(i * hb - 1, 0), 0))
    nxt = lambda w: pl.BlockSpec((8, w), lambda i: (jnp.minimum((i + 1) * hb, nt * hb - 1), 0))
    return pl.pallas_call(
        body, name=name, grid=(nt,),
        in_specs=[row(DCV), nxt(DCV), row(3 * DCV), prev(3 * DCV), nxt(3 * DCV),
                  pl.BlockSpec(conv_w.shape, lambda i: (0, 0))],
        out_specs=(row(3 * DCV), pl.BlockSpec((8, DCV), lambda i: (0, 0))),
        out_shape=(jax.ShapeDtypeStruct((t, 3 * DCV), MXU_DT), jax.ShapeDtypeStruct((8, DCV), F32)),
        scratch_shapes=[pltpu.VMEM((tm + 8, DCV), F32), pltpu.VMEM((tm + 8, DCV), F32)],
        compiler_params=_params(48),
    )(dy, dy, bch, bch, bch, conv_w)


def _attn_bwd(ka, kat, va, qat, dot, lrow, drow, r, name):
    t = ka.shape[1]
    tq = _tile(t, 512)
    nq = t // tq

    def body(r_ref, ka_ref, kat_ref, va_ref, l_ref, dl_ref, qat_hbm, dot_hbm,
             dk_ref, dv_ref, dck_ref, dqt_hbm, dcq_hbm, qat_v, dot_v, dq_acc):
        hp, j = pl.program_id(0), pl.program_id(1)

        @pl.when(j == 0)
        def _():
            pltpu.sync_copy(qat_hbm.at[pl.ds(2 * hp, 2)], qat_v)
            pltpu.sync_copy(dot_hbm.at[pl.ds(2 * hp, 2)], dot_v)
            dq_acc[...] = jnp.zeros_like(dq_acc)

        key = lax.broadcasted_iota(jnp.int32, (tq, tq), 0)
        qry = lax.broadcasted_iota(jnp.int32, (tq, tq), 1)

        def step(i, carry, masked):
            off = pl.multiple_of(i * tq, tq)
            out = []
            for a in range(2):
                dk, dv = carry[a]
                st = _mm(ka_ref[a], qat_v[a, :, pl.ds(off, tq)])
                dpt = _mm(va_ref[a], dot_v[a, :, pl.ds(off, tq)])
                if masked:
                    st = jnp.where(qry >= key, st, NEG)
                d = r_ref[2 * hp + a, i] - r_ref[2 * hp + a, j]
                pt = jnp.exp(st - (l_ref[a, :, pl.ds(off, tq)] - d))
                dsb = (pt * (dpt - dl_ref[a, :, pl.ds(off, tq)])).astype(MXU_DT)
                dv = dv + _mm_nt(dot_v[a, 0:DH, pl.ds(off, tq)], pt.astype(MXU_DT))
                dk = dk + _mm_nt(qat_v[a, 0:QROWS, pl.ds(off, tq)], dsb)
                dq_acc[a, :, pl.ds(off, tq)] += _mm(kat_ref[a], dsb)
                out.append((dk, dv))
            return tuple(out)

        init = tuple((jnp.zeros((QROWS, tq), F32), jnp.zeros((DH, tq), F32)) for _ in range(2))
        carry = step(j, init, True)
        (dka, dva), (dkb, dvb) = lax.fori_loop(j + 1, nq, lambda i, cr: step(i, cr, False), carry)
        dk_ref[...] = jnp.concatenate([dka[0:DH], dkb[0:DH]], axis=0).T.astype(dk_ref.dtype)
        dv_ref[...] = jnp.concatenate([dva, dvb], axis=0).T.astype(dv_ref.dtype)
        dck_ref[0] = -dka[DH + 3:DH + 4, :]
        dck_ref[1] = -dkb[DH + 3:DH + 4, :]

        @pl.when(j == nq - 1)
        def _():
            pltpu.sync_copy(dq_acc, dqt_hbm.at[pl.ds(2 * hp, 2)])
            pltpu.sync_copy(dq_acc.at[:, DH:DH + 1, :], dcq_hbm.at[pl.ds(2 * hp, 2)])

    pair = lambda rows, cols: pl.BlockSpec((2, rows, cols), lambda p, j: (p, 0, 0))
    return pl.pallas_call(
        body, name=name, grid=(NH // 2, nq),
        in_specs=[pl.BlockSpec(memory_space=pltpu.SMEM),
                  pl.BlockSpec((2, tq, LANES), lambda p, j: (p, j, 0)),
                  pl.BlockSpec((2, QROWS, tq), lambda p, j: (p, 0, j)),
                  pl.BlockSpec((2, tq, LANES), lambda p, j: (p, j, 0)),
                  pair(1, t), pair(1, t), ANY, ANY],
        out_specs=(pl.BlockSpec((tq, LANES), lambda p, j: (j, p)),
                   pl.BlockSpec((tq, LANES), lambda p, j: (j, p)),
                   pl.BlockSpec((2, 1, tq), lambda p, j: (p, 0, j)), ANY, ANY),
        out_shape=(jax.ShapeDtypeStruct((t, DA), MXU_DT), jax.ShapeDtypeStruct((t, DA), MXU_DT),
                   jax.ShapeDtypeStruct((NH, 1, t), F32), jax.ShapeDtypeStruct((NH, QROWS, t), F32),
                   jax.ShapeDtypeStruct((NH, 1, t), F32)),
        scratch_shapes=[pltpu.VMEM((2, LANES, t), MXU_DT), pltpu.VMEM((2, LANES, t), MXU_DT),
                        pltpu.VMEM((2, QROWS, t), F32)],
        compiler_params=_params(52, 2),
    )(r, ka, kat, va, lrow, drow, qat, dot)


def _mix_proj_bwd(dr2, dqt, dk, dv, dbch, dc, z, xh1, rs1, g1, w_qkv, w_bch, w_f, name):
    t = dr2.shape[0]
    tm = _tile(t, 512)
    nt = t // tm

    def body(dr_ref, dqt_ref, dk_ref, dv_ref, db_ref, dc_ref, z_ref, x_ref, rs_ref, g_ref,
             wq_ref, wb_ref, wf_ref, out_ref, df_ref, dq_ref, st_ref, carry):
        @pl.when(pl.program_id(0) == 0)
        def _():
            carry[...] = jnp.zeros_like(carry)
            st_ref[...] = jnp.zeros_like(st_ref)

        dq_ref[...] = (jnp.concatenate([dqt_ref[h, 0:DH, :] for h in range(NH)], axis=0).T * 0.125
                       ).astype(dq_ref.dtype)

        row = lax.broadcasted_iota(jnp.int32, (tm, tm), 0)
        col = lax.broadcasted_iota(jnp.int32, (tm, tm), 1)
        triu = jnp.where(col >= row, 1.0, 0.0).astype(MXU_DT)
        dlogf = carry[...] + _mm_sel(triu, dc_ref[...])
        carry[...] = dlogf[0:1, :]
        dz = dlogf / (1.0 + jnp.exp(z_ref[...]))
        st_ref[2:3, 0:LANES] += _colsum(dz)
        dfb = dz.astype(MXU_DT)
        df_ref[...] = dfb
        dx = (ALPHA * dr_ref[...]
              + _mm_nt(dq_ref[...], wq_ref[:, 0:DA])
              + _mm_nt(dk_ref[...], wq_ref[:, DA:2 * DA])
              + _mm_nt(dv_ref[...], wq_ref[:, 2 * DA:3 * DA])
              + _mm_nt(db_ref[...], wb_ref[...])
              + _mm_nt(dfb, wf_ref[...]))
        xh = x_ref[...]
        st_ref[0:1, :] += _colsum(dx * xh)
        st_ref[1:2, :] += _colsum(dx)
        out_ref[...] = _ln_bwd(dx * g_ref[...], xh, rs_ref[...])

    row = lambda w: pl.BlockSpec((tm, w), lambda i: (nt - 1 - i, 0))
    full = lambda a: pl.BlockSpec(a.shape, lambda i: (0, 0))
    return pl.pallas_call(
        body, name=name, grid=(nt,),
        in_specs=[row(D), pl.BlockSpec((NH, QROWS, tm), lambda i: (0, 0, nt - 1 - i)), row(DA), row(DA),
                  row(3 * DCV), row(LANES), row(LANES), row(D), row(1),
                  full(g1), full(w_qkv), full(w_bch), full(w_f)],
        out_specs=(row(D), row(LANES), row(DA), pl.BlockSpec((8, D), lambda i: (0, 0))),
        out_shape=(jax.ShapeDtypeStruct((t, D), F32), jax.ShapeDtypeStruct((t, LANES), MXU_DT),
                   jax.ShapeDtypeStruct((t, DA), MXU_DT), jax.ShapeDtypeStruct((8, D), F32)),
        scratch_shapes=[pltpu.VMEM((1, LANES), F32)],
        compiler_params=_params(48),
    )(dr2, dqt, dk, dv, dbch, dc, z, xh1, rs1, g1, w_qkv, w_bch, w_f)


def _dw(mode, a_parts, b, m, n, name, tmm=None, tn=None, exchange=()):
    t = b.shape[0]
    tmm = tmm or m
    tn = tn or n
    tt = _tile(t, 2048)
    na, ne = len(a_parts), len(exchange)
    grid = (m // tmm, n // tn, t // tt)

    def body(*refs):
        a_refs, b_ref, o_ref = refs[:na], refs[na], refs[na + 1 + ne]
        if ne:
            e_start, e_finish = _exchange_phases(refs[na + 1:na + 1 + ne], refs[na + 2 + ne:na + 2 + 2 * ne],
                                                 *refs[na + 2 + 2 * ne:])
            at = lambda steps: functools.reduce(jnp.logical_and, [pl.program_id(d) == s for d, s in enumerate(steps)])
            pl.when(at((0, 0, 0)))(e_start)

        @pl.when(pl.program_id(2) == 0)
        def _():
            o_ref[...] = jnp.zeros_like(o_ref)

        if mode == "plain":
            a = a_refs[0][...].astype(MXU_DT)
        elif mode == "affine":
            a = (a_refs[0][...] * a_refs[1][...] + a_refs[2][...]).astype(MXU_DT)
        else:
            g = a_refs[0][...].astype(F32)
            a = (g * _sigmoid(g) * a_refs[1][...].astype(F32)).astype(MXU_DT)
        o_ref[...] += _mm_tn(a, b_ref[...].astype(MXU_DT))
        if ne:
            pl.when(at(tuple(g - 1 for g in grid)))(e_finish)

    a_tile = pl.BlockSpec((tt, tmm), lambda i, j, k: (k, i))
    a_vec = pl.BlockSpec((1, tmm), lambda i, j, k: (0, i))
    a_specs = {"plain": [a_tile], "affine": [a_tile, a_vec, a_vec], "swiglu": [a_tile, a_tile]}[mode]
    res = pl.pallas_call(
        body, name=name, grid=grid,
        in_specs=a_specs + [pl.BlockSpec((tt, tn), lambda i, j, k: (k, j))] + [ANY] * ne,
        out_specs=(pl.BlockSpec((tmm, tn), lambda i, j, k: (i, j)),) + (ANY,) * ne,
        out_shape=(jax.ShapeDtypeStruct((m, n), F32),)
        + tuple(jax.ShapeDtypeStruct(a.shape, a.dtype) for a in exchange),
        scratch_shapes=_exchange_sems(ne) if ne else [],
        compiler_params=_params(52, 3),
    )(*a_parts, b, *exchange)
    return res if ne else res[0]


def _adamw(w, g, m, v):
    m = ADAM_B1 * m + (1.0 - ADAM_B1) * g
    v = ADAM_B2 * v + (1.0 - ADAM_B2) * (g * g)
    m_hat = m / (1.0 - ADAM_B1 ** ADAM_STEP)
    v_hat = v / (1.0 - ADAM_B2 ** ADAM_STEP)
    delta = -ADAM_LR * (m_hat / (jnp.sqrt(v_hat) + ADAM_EPS) + ADAM_WD * w)
    return delta, m, v


def _reduce_adamw(landed, own, w, m, v, name):
    r, c = own.shape
    tr = _tile(r, 128)

    def body(l_ref, o_ref, w_ref, m_ref, v_ref, g_out, d_out, m_out, v_out):
        me = 4 * lax.axis_index("x") + 2 * lax.axis_index("y") + lax.axis_index("c")
        g = None
        for j in range(NDEV):
            term = jnp.where(me == j, o_ref[...], l_ref[j].astype(F32))
            g = term if g is None else g + term
        g_out[...] = g
        d_out[...], m_out[...], v_out[...] = _adamw(w_ref[...], g, m_ref[...], v_ref[...])

    blk = pl.BlockSpec((tr, c), lambda i: (i, 0))
    sds = jax.ShapeDtypeStruct((r, c), F32)
    return pl.pallas_call(
        body, name=name, grid=(r // tr,),
        in_specs=[pl.BlockSpec((NDEV, tr, c), lambda i: (0, i, 0)), blk, blk, blk, blk],
        out_specs=(blk, blk, blk, blk), out_shape=(sds, sds, sds, sds),
        compiler_params=_params(40),
    )(landed, own, w, m, v)


def _sum_small(gathered, name):
    _, r, c = gathered.shape

    def body(g_ref, o_ref):
        acc = g_ref[0]
        for j in range(1, NDEV):
            acc = acc + g_ref[j]
        o_ref[...] = acc

    return pl.pallas_call(body, name=name, out_shape=jax.ShapeDtypeStruct((r, c), F32))(gathered)


def _adamw_small(g, w, m, v, name):
    def body(g_ref, w_ref, m_ref, v_ref, d_out, m_out, v_out):
        d_out[...], m_out[...], v_out[...] = _adamw(w_ref[...], g_ref[...], m_ref[...], v_ref[...])

    sds = jax.ShapeDtypeStruct(g.shape, F32)
    return pl.pallas_call(body, name=name, out_shape=(sds, sds, sds))(g, w, m, v)


def _cols_from_stack(s):
    return jnp.transpose(s, (1, 0, 2)).reshape(s.shape[1], NDEV * s.shape[2])


def _cols_to_stack(w):
    r, c = w.shape
    return jnp.transpose(w.reshape(r, NDEV, c // NDEV), (1, 0, 2))


def _rows_from_stack(s):
    return s.reshape(NDEV * s.shape[1], s.shape[2])


def _rows_to_stack(w):
    r, c = w.shape
    return w.reshape(NDEV, r // NDEV, c)


SMALL_ROWS = 16
SMALL_SLOTS = {
    "ln1_g": (0, 0, D), "ln1_b": (1, 0, D), "ln2_g": (2, 0, D), "ln2_b": (3, 0, D), "ln3_g": (4, 0, D),
    "ln3_b": (5, 0, D), "b_ple_gate": (6, 0, D), "ln4_g": (7, 0, D), "ln4_b": (8, 0, D),
    "g_attn": (9, 0, DA), "g_conv": (9, DA, DCV), "b_forget": (10, 0, NH),
}
CONVW_ROW = 11
LOSS_SLOT = (10, LANES)


def _pack_small(vals, conv_rows, loss=None):
    out = jnp.zeros((SMALL_ROWS, D), F32)
    for nm, (r, off, wd) in SMALL_SLOTS.items():
        out = out.at[r:r + 1, off:off + wd].set(vals[nm].reshape(1, wd).astype(F32))
    out = out.at[CONVW_ROW:CONVW_ROW + 3, 0:conv_rows.shape[1]].set(conv_rows.astype(F32))
    if loss is not None:
        out = out.at[LOSS_SLOT[0], LOSS_SLOT[1]].set(loss)
    return out


def _unpack_small(packed, name):
    r, off, wd = SMALL_SLOTS[name]
    return packed[r:r + 1, off:off + wd]


def kernel(x, p, ffn1_w_in, ffn1_w_out, ln1_g, ln1_b, w_mix_in, b_forget, conv_w, g_attn, g_conv, w_mix_out, ln2_g, ln2_b, ffn2_w_in, ffn2_w_out, ln3_g, ln3_b, w_ple, w_ple_gate, b_ple_gate, ln4_g, ln4_b, loss_target, m_ffn1_w_in, m_ffn1_w_out, m_ln1_g, m_ln1_b, m_w_mix_in, m_b_forget, m_conv_w, m_g_attn, m_g_conv, m_w_mix_out, m_ln2_g, m_ln2_b, m_ffn2_w_in, m_ffn2_w_out, m_ln3_g, m_ln3_b, m_w_ple, m_w_ple_gate, m_b_ple_gate, m_ln4_g, m_ln4_b, v_ffn1_w_in, v_ffn1_w_out, v_ln1_g, v_ln1_b, v_w_mix_in, v_b_forget, v_conv_w, v_g_attn, v_g_conv, v_w_mix_out, v_ln2_g, v_ln2_b, v_ffn2_w_in, v_ffn2_w_out, v_ln3_g, v_ln3_b, v_w_ple, v_w_ple_gate, v_b_ple_gate, v_ln4_g, v_ln4_b):
    args = dict(locals())
    t = x.shape[1]
    me = 4 * lax.axis_index("x") + 2 * lax.axis_index("y") + lax.axis_index("c")
    x0 = x.reshape(t, D)
    p0 = p.reshape(t, PLE)
    tgt = loss_target.reshape(t, D)

    big = ["ffn1_w_in", "ffn1_w_out", "w_mix_in", "w_mix_out", "ffn2_w_in", "ffn2_w_out", "w_ple", "w_ple_gate"]
    col_sharded = {"ffn1_w_in", "w_mix_in", "ffn2_w_in", "w_ple"}
    shard = {nm: args[nm][0] for nm in big}

    unstack = lambda nm, g: (_cols_from_stack(g) if nm in col_sharded else _rows_from_stack(g)).astype(MXU_DT)
    stack = lambda nm, g: _cols_to_stack(g) if nm in col_sharded else _rows_to_stack(g)
    wire = lambda names: [shard[nm].astype(WIRE_DT) for nm in names]
    first, later = big[:2], big[2:]

    full = {nm: unstack(nm, g) for nm, g in zip(first, _allgather(wire(first), "ag_ffn1"))}
    ffn1_out, gathered = _ffn1_fwd(x0, full, wire(later) + [conv_w[0]])
    full.update({nm: unstack(nm, g) for nm, g in zip(later, gathered)})
    cw = _cols_from_stack(gathered[len(later)])

    dr1, gw, small, loss_part = _mid_step(p0, tgt, full, cw, {nm: args[nm] for nm in SMALL_SLOTS}, ffn1_out)
    small_part = _pack_small({nm: small[nm] for nm in SMALL_SLOTS},
                             jnp.pad(small["conv_w"], ((0, 0), (0, D - DCV))), loss_part)
    stacks = {nm: stack(nm, gw[nm]) for nm in later}
    gx, gw1, landed_later, landed_w_in = _ffn1_bwd(
        x0, dr1, ffn1_out, full, [stacks[nm].astype(WIRE_DT) for nm in later],
        lambda g: stack("ffn1_w_in", g).astype(WIRE_DT))
    stacks.update({nm: stack(nm, gw1[nm]) for nm in first})
    (landed_w_out,), (small_all,) = _exchange_and_gather([stacks["ffn1_w_out"].astype(WIRE_DT)], [small_part],
                                                         "rs_ffn1_out")
    landed = dict(zip(later, landed_later), ffn1_w_in=landed_w_in, ffn1_w_out=landed_w_out)
    small_g = _sum_small(small_all, "sum_small")
    loss = small_g[LOSS_SLOT[0], LOSS_SLOT[1]]

    outs = {"loss": loss, "grad_x": gx.reshape(1, t, D)}
    for nm in big:
        own = lax.dynamic_index_in_dim(stacks[nm], me, axis=0, keepdims=False)
        g, dl, mn, vn = _reduce_adamw(landed[nm], own, shard[nm], args["m_" + nm][0], args["v_" + nm][0],
                                      "adamw_" + nm)
        outs["grad_" + nm], outs["delta_" + nm], outs["new_m_" + nm], outs["new_v_" + nm] = (
            g[None], dl[None], mn[None], vn[None])
    small_names = list(SMALL_SLOTS)
    cshard = lax.dynamic_slice_in_dim(small_g[CONVW_ROW:CONVW_ROW + 3, 0:DCV], me * (DCV // NDEV), DCV // NDEV, axis=1)
    g_pack = _pack_small({nm: _unpack_small(small_g, nm) for nm in small_names}, cshard)
    packs = [_pack_small({nm: args[pre + nm] for nm in small_names}, args[pre + "conv_w"][0])
             for pre in ("", "m_", "v_")]
    d_pack, m_pack, v_pack = _adamw_small(g_pack, packs[0], packs[1], packs[2], "adamw_small")
    for key, pk in (("grad_", g_pack), ("delta_", d_pack), ("new_m_", m_pack), ("new_v_", v_pack)):
        for nm in small_names:
            outs[key + nm] = _unpack_small(pk, nm)
        outs[key + "conv_w"] = pk[CONVW_ROW:CONVW_ROW + 3, 0:DCV // NDEV][None]

    wnames = ["ffn1_w_in", "ffn1_w_out", "ln1_g", "ln1_b", "w_mix_in", "b_forget", "conv_w", "g_attn", "g_conv",
              "w_mix_out", "ln2_g", "ln2_b", "ffn2_w_in", "ffn2_w_out", "ln3_g", "ln3_b", "w_ple", "w_ple_gate",
              "b_ple_gate", "ln4_g", "ln4_b"]
    return (outs["loss"], outs["grad_x"], *[outs[pre + nm] for pre in ("grad_", "delta_", "new_m_", "new_v_")
                                            for nm in wnames])


def _ffn1_fwd(x0, full, gather=()):
    res = _ffn_fwd(x0, jnp.ones((1, D), F32), jnp.zeros((1, D), F32), full["ffn1_w_in"], full["ffn1_w_out"],
                   "ffn1_fwd", gather)
    return res[:4], res[4:]


def _ffn1_bwd(x0, dr1, ffn1_out, full, exchange=(), w_in_slots=None):
    g1a, u1a, _, rs1 = ffn1_out
    ones, zeros = jnp.ones((1, D), F32), jnp.zeros((1, D), F32)
    res = _ffn_bwd(dr1, g1a, u1a, x0, rs1, ones, full["ffn1_w_in"], full["ffn1_w_out"], False, "ffn1_bwd",
                   exchange)
    df1, dg1, du1, gx = res[:4]
    gw_in = jnp.concatenate(
        [_dw("affine", (x0, ones, zeros), dg1, D, F, "dw_ffn1_in_g", tn=F // 2),
         _dw("affine", (x0, ones, zeros), du1, D, F, "dw_ffn1_in_u", tn=F // 2)], axis=1)
    side = () if w_in_slots is None else (w_in_slots(gw_in),)
    out = _dw("swiglu", (g1a, u1a), df1, F, D, "dw_ffn1_out", tmm=F // 2, exchange=side)
    gw_out, landed_in = (out, None) if w_in_slots is None else (out[0], out[1])
    return gx, {"ffn1_w_in": gw_in, "ffn1_w_out": gw_out}, res[5:], landed_in


def _mid_step(p0, tgt, full, cw, sp, ffn1_out):
    g1a, u1a, xh1, rs1 = ffn1_out
    t = xh1.shape[0]
    ln1_g, ln1_b, ln2_g, ln2_b, ln3_g, ln3_b = (sp[k] for k in ("ln1_g", "ln1_b", "ln2_g", "ln2_b", "ln3_g", "ln3_b"))
    ln4_g, ln4_b, g_attn, g_conv, b_ple_gate = (sp[k] for k in ("ln4_g", "ln4_b", "g_attn", "g_conv", "b_ple_gate"))
    wmi = full["w_mix_in"]
    w_qkv = wmi[:, 0:3 * DA]
    w_f = jnp.pad(wmi[:, 3 * DA:3 * DA + NH], ((0, 0), (0, LANES - NH)))
    w_bch = wmi[:, 3 * DA + NH:]
    bf_pad = jnp.pad(sp["b_forget"], ((0, 0), (0, LANES - NH)))

    ka, va, qat, kat, vta, bch, z, rt = _mix_proj_fwd(xh1, ln1_g, ln1_b, w_qkv[:, DA:], jnp.transpose(w_qkv),
                                                      w_bch, w_f, bf_pad, "mix_proj_fwd")
    rtile = jnp.transpose(rt[:, 0, 0:NH])
    o, lse = _attn_fwd(qat, ka, vta, rtile, "attn_fwd")
    merged, xh2, rs2 = _mix_post_fwd(o, bch, cw, g_attn, g_conv, xh1, ln1_g, ln1_b, full["w_mix_out"],
                                     "mix_post_fwd")
    g2a, u2a, xh3, rs3 = _ffn_fwd(xh2, ln2_g, ln2_b, full["ffn2_w_in"], full["ffn2_w_out"], "ffn2_fwd")

    dr3, dz, de, st_tail = _tail(xh3, rs3, ln3_g, ln3_b, p0, full["w_ple_gate"], full["w_ple"], b_ple_gate,
                                 ln4_g, ln4_b, tgt, "tail")
    df2, dg2, du2, dr2, st_f2 = _ffn_bwd(dr3, g2a, u2a, xh2, rs2, ln2_g, full["ffn2_w_in"], full["ffn2_w_out"],
                                         True, "ffn2_bwd")
    dmix, dot, drow, dyc, st_post = _mix_post_bwd(dr2, o, bch, cw, g_attn, g_conv, full["w_mix_out"],
                                                  "mix_post_bwd")
    dbch, st_conv = _conv_bwd(dyc, bch, cw, "conv_bwd")
    dk, dv, dck, dqt, dcq = _attn_bwd(ka, kat, va, qat, dot, lse, drow, rtile, "attn_bwd")
    dc_pad = jnp.pad(jnp.transpose((dcq + dck).reshape(NH, t)), ((0, 0), (0, LANES - NH)))
    dr1, dfl, dq, st_proj = _mix_proj_bwd(dr2, dqt, dk, dv, dbch, dc_pad, z, xh1, rs1, ln1_g, w_qkv, w_bch, w_f,
                                          "mix_proj_bwd")

    x1p, x2p, x3p = (xh1, ln1_g, ln1_b), (xh2, ln2_g, ln2_b), (xh3, ln3_g, ln3_b)
    gw = {}
    gw["ffn2_w_in"] = jnp.concatenate(
        [_dw("affine", x2p, dg2, D, F, "dw_ffn2_in_g", tn=F // 2),
         _dw("affine", x2p, du2, D, F, "dw_ffn2_in_u", tn=F // 2)], axis=1)
    gw["ffn2_w_out"] = _dw("swiglu", (g2a, u2a), df2, F, D, "dw_ffn2_out", tmm=F // 2)
    gw["w_mix_out"] = _dw("plain", (merged,), dmix, D, D, "dw_mix_out")
    gw["w_mix_in"] = jnp.concatenate(
        [_dw("affine", x1p, dq, D, DA, "dw_mix_in_q"), _dw("affine", x1p, dk, D, DA, "dw_mix_in_k"),
         _dw("affine", x1p, dv, D, DA, "dw_mix_in_v"),
         _dw("affine", x1p, dfl, D, LANES, "dw_mix_in_f")[:, 0:NH],
         _dw("affine", x1p, dbch, D, 3 * DCV, "dw_mix_in_bch")], axis=1)
    gw["w_ple_gate"] = _dw("affine", x3p, dz, D, D, "dw_ple_gate")
    gw["w_ple"] = _dw("plain", (p0,), de, PLE, D, "dw_ple")

    loss_part = (0.5 / D) * jnp.sum(st_tail[5:6, :])
    small = {"ln1_g": st_proj[0:1], "ln1_b": st_proj[1:2], "ln2_g": st_f2[0:1], "ln2_b": st_f2[1:2],
             "ln3_g": st_tail[3:4], "ln3_b": st_tail[4:5], "b_ple_gate": st_tail[2:3], "ln4_g": st_tail[0:1],
             "ln4_b": st_tail[1:2], "g_attn": st_post[0:1], "g_conv": st_post[1:2],
             "b_forget": st_proj[2:3, 0:NH], "conv_w": st_conv[0:3]}
    return dr1, gw, small, loss_part
```

```python
import functools

import jax
import jax.numpy as jnp
from jax import lax
from jax.experimental import pallas as pl
from jax.experimental.pallas import tpu as pltpu

D = 1024
F = 2816
NH = 8
DH = 64
DA = NH * DH
DCV = D - DA
PLE = 256
LN_EPS = 1e-5
RMS_EPS = 1e-6
NEG = -1e30
ALPHA = 2.0 ** 0.25
NDEV = 8
LANES = 128

ADAM_LR, ADAM_B1, ADAM_B2, ADAM_EPS, ADAM_WD, ADAM_STEP = 0.001, 0.9, 0.999, 1e-08, 0.01, 10

F32 = jnp.float32
MXU_DT = jnp.bfloat16
WIRE_DT = jnp.bfloat16

MESH_ID = pl.DeviceIdType.MESH
ANY = pl.BlockSpec(memory_space=pl.ANY)


def _params(vmem_mb, n_axes=1):
    return pltpu.CompilerParams(dimension_semantics=("arbitrary",) * n_axes,
                                vmem_limit_bytes=int(vmem_mb) << 20)


def _mm(a, b):
    return jnp.dot(a, b, preferred_element_type=F32)


def _mm_nt(a, b):
    return lax.dot_general(a, b, (((1,), (1,)), ((), ())), preferred_element_type=F32)


def _mm_tn(a, b):
    return lax.dot_general(a, b, (((0,), (0,)), ((), ())), preferred_element_type=F32)


def _split3(x):
    hi = x.astype(MXU_DT)
    r1 = x - hi.astype(F32)
    mid = r1.astype(MXU_DT)
    lo = (r1 - mid.astype(F32)).astype(MXU_DT)
    return hi, mid, lo


def _mm_sel(sel, x):
    hi, mid, lo = _split3(x)
    return _mm(sel, hi) + _mm(sel, mid) + _mm(sel, lo)


def _sigmoid(x):
    return 1.0 / (1.0 + jnp.exp(-x))


def _ln_fwd(r):
    mu = jnp.mean(r, axis=-1, keepdims=True)
    xc = r - mu
    var = jnp.mean(xc * xc, axis=-1, keepdims=True)
    rstd = lax.rsqrt(var + LN_EPS)
    return xc * rstd, rstd


def _ln_bwd(dxhat, xhat, rstd):
    m1 = jnp.mean(dxhat, axis=-1, keepdims=True)
    m2 = jnp.mean(dxhat * xhat, axis=-1, keepdims=True)
    return rstd * (dxhat - m1 - xhat * m2)


def _rms_fwd(x):
    r = lax.rsqrt(jnp.mean(x * x, axis=-1, keepdims=True) + RMS_EPS)
    return x * r, r


def _rms_bwd(dyg, xn, r):
    return r * (dyg - xn * jnp.mean(dyg * xn, axis=-1, keepdims=True))


def _colsum(x):
    return jnp.sum(x, axis=0, keepdims=True)


def _f_chunks():
    out, c0 = [], 0
    while c0 < F:
        fc = min(512, F - c0)
        out.append((c0, fc))
        c0 += fc
    return out


def _tile(t, want):
    return want if t % want == 0 and t >= want else t


def _exchange_sems(n):
    return [pltpu.SemaphoreType.DMA((n * (NDEV - 1),)), pltpu.SemaphoreType.DMA((n * (NDEV - 1),))]


def _exchange_phases(ins, outs, send_sems, recv_sems):
    n = len(ins)

    def peers():
        x, y, c = lax.axis_index("x"), lax.axis_index("y"), lax.axis_index("c")
        out = []
        for k in range(1, NDEV):
            px = 1 - x if (k >> 2) & 1 else x
            py = 1 - y if (k >> 1) & 1 else y
            pc = 1 - c if k & 1 else c
            out.append(((px, py, pc), 4 * px + 2 * py + pc))
        return 4 * x + 2 * y + c, out

    def remote(w, k, to, slot_src, slot_dst):
        return pltpu.make_async_remote_copy(
            src_ref=ins[w].at[slot_src], dst_ref=outs[w].at[slot_dst],
            send_sem=send_sems.at[w * (NDEV - 1) + k], recv_sem=recv_sems.at[w * (NDEV - 1) + k],
            device_id=to, device_id_type=MESH_ID)

    def start():
        me, prs = peers()
        for k, (to, pid) in enumerate(prs):
            for w in range(n):
                remote(w, k, to, pid, me).start()

    def finish():
        me, prs = peers()
        for k, (to, pid) in enumerate(prs):
            for w in range(n):
                remote(w, k, to, me, pid).wait_recv()
        for k, (to, pid) in enumerate(prs):
            for w in range(n):
                remote(w, k, to, pid, me).wait_send()

    return start, finish


def _gather_sems(n):
    return [pltpu.SemaphoreType.DMA((n * (NDEV - 1),)), pltpu.SemaphoreType.DMA((n * (NDEV - 1),)),
            pltpu.SemaphoreType.DMA((n,))]


def _gather_phases(ins, outs, send_sems, recv_sems, loc_sems):
    n = len(ins)
    per = NDEV - 1

    def place():
        x, y, c = lax.axis_index("x"), lax.axis_index("y"), lax.axis_index("c")
        return (x, y, c), (x, y, 1 - c), [(1 - x, y), (x, 1 - y), (1 - x, 1 - y)]

    def copy(w, k, block, to, src=None):
        dst = outs[w].at[4 * block[0] + 2 * block[1] + block[2]]
        return pltpu.make_async_remote_copy(
            src_ref=dst if src is None else src, dst_ref=dst,
            send_sem=send_sems.at[w * per + k], recv_sem=recv_sems.at[w * per + k],
            device_id=to, device_id_type=MESH_ID)

    def local(w, me):
        return pltpu.make_async_copy(ins[w], outs[w].at[4 * me[0] + 2 * me[1] + me[2]], loc_sems.at[w])

    def first(me, sib, chips):
        out = []
        for j, chip in enumerate(chips):
            out += [copy(w, 1 + j, me, (*chip, me[2]), src=ins[w]) for w in range(n)]
        return out + [copy(w, 0, me, sib, src=ins[w]) for w in range(n)]

    def start():
        me, sib, chips = place()
        for w in range(n):
            local(w, me).start()
        for cp in first(me, sib, chips):
            cp.start()

    def forward():
        me, sib, chips = place()
        for j, chip in enumerate(chips):
            for w in range(n):
                copy(w, 1 + j, (*chip, me[2]), me).wait_recv()
                copy(w, 4 + j, (*chip, me[2]), sib).start()

    def finish():
        me, sib, chips = place()
        for w in range(n):
            copy(w, 0, sib, me).wait_recv()
        for j, chip in enumerate(chips):
            for w in range(n):
                copy(w, 4 + j, (*chip, 1 - me[2]), me).wait_recv()
        for cp in first(me, sib, chips):
            cp.wait_send()
        for j, chip in enumerate(chips):
            for w in range(n):
                copy(w, 4 + j, (*chip, me[2]), sib).wait_send()
        for w in range(n):
            local(w, me).wait()

    return start, forward, finish


def _allgather(arrs, name):
    n = len(arrs)

    def body(*refs):
        start, forward, finish = _gather_phases(refs[:n], refs[n:2 * n], *refs[2 * n:])
        start()
        forward()
        finish()

    return pl.pallas_call(
        body, name=name, out_shape=tuple(jax.ShapeDtypeStruct((NDEV,) + a.shape, a.dtype) for a in arrs),
        in_specs=[ANY] * n, out_specs=tuple([ANY] * n), scratch_shapes=_gather_sems(n),
    )(*arrs)


def _exchange_and_gather(ex, ga, name):
    ne, ng = len(ex), len(ga)

    def body(*refs):
        ins, outs, sems = refs[:ne + ng], refs[ne + ng:2 * (ne + ng)], refs[2 * (ne + ng):]
        e_start, e_finish = _exchange_phases(ins[:ne], outs[:ne], *sems[:2])
        g_start, g_forward, g_finish = _gather_phases(ins[ne:], outs[ne:], *sems[2:])
        e_start()
        g_start()
        g_forward()
        g_finish()
        e_finish()

    res = pl.pallas_call(
        body, name=name,
        out_shape=tuple(jax.ShapeDtypeStruct(a.shape, a.dtype) for a in ex)
        + tuple(jax.ShapeDtypeStruct((NDEV,) + a.shape, a.dtype) for a in ga),
        in_specs=[ANY] * (ne + ng), out_specs=tuple([ANY] * (ne + ng)),
        scratch_shapes=_exchange_sems(ne) + _gather_sems(ng),
    )(*ex, *ga)
    return res[:ne], res[ne:]


def _ffn_fwd(xin, gin, bin_, w_in, w_out, name, gather=()):
    t = xin.shape[0]
    tm = _tile(t, 512)
    nt = t // tm
    chunks = _f_chunks()
    ng = len(gather)

    def body(*refs):
        x_ref, gi_ref, bi_ref, win_hbm, wout_hbm = refs[:5]
        g_ref, u_ref, xh_ref, rs_ref = refs[5 + ng:9 + ng]
        win_v, wout_v, acc_ref = refs[9 + 2 * ng:12 + 2 * ng]
        if ng:
            g_start, g_forward, g_finish = _gather_phases(refs[5:5 + ng], refs[9 + ng:9 + 2 * ng],
                                                          *refs[12 + 2 * ng:])

        @pl.when(pl.program_id(0) == 0)
        def _():
            if ng:
                g_start()
            pltpu.sync_copy(win_hbm, win_v)
            pltpu.sync_copy(wout_hbm, wout_v)

        if ng:
            pl.when(pl.program_id(0) == nt // 2)(g_forward)
            pl.when(pl.program_id(0) == nt - 1)(g_finish)

        x = x_ref[...] * gi_ref[...] + bi_ref[...]
        xb = x.astype(MXU_DT)
        for ci, (c0, fc) in enumerate(chunks):
            gc = _mm(xb, win_v[:, c0:c0 + fc])
            uc = _mm(xb, win_v[:, F + c0:F + c0 + fc])
            g_ref[:, c0:c0 + fc] = gc.astype(g_ref.dtype)
            u_ref[:, c0:c0 + fc] = uc.astype(u_ref.dtype)
            hc = (gc * _sigmoid(gc) * uc).astype(MXU_DT)
            part = _mm(hc, wout_v[c0:c0 + fc, :])
            if ci == 0:
                acc_ref[...] = part
            else:
                acc_ref[...] += part
        xh, rstd = _ln_fwd(ALPHA * x + 0.5 * acc_ref[...])
        xh_ref[...] = xh
        rs_ref[...] = rstd

    row = pl.BlockSpec((tm, D), lambda i: (i, 0))
    vec = pl.BlockSpec((1, D), lambda i: (0, 0))
    act = pl.BlockSpec((tm, F), lambda i: (i, 0))
    return pl.pallas_call(
        body, name=name, grid=(nt,),
        in_specs=[row, vec, vec, ANY, ANY] + [ANY] * ng,
        out_specs=(act, act, row, pl.BlockSpec((tm, 1), lambda i: (i, 0))) + (ANY,) * ng,
        out_shape=(jax.ShapeDtypeStruct((t, F), MXU_DT), jax.ShapeDtypeStruct((t, F), MXU_DT),
                   jax.ShapeDtypeStruct((t, D), F32), jax.ShapeDtypeStruct((t, 1), F32))
        + tuple(jax.ShapeDtypeStruct((NDEV,) + a.shape, a.dtype) for a in gather),
        scratch_shapes=[pltpu.VMEM((D, 2 * F), MXU_DT), pltpu.VMEM((F, D), MXU_DT), pltpu.VMEM((tm, D), F32)]
        + (_gather_sems(ng) if ng else []),
        compiler_params=_params(52),
    )(xin, gin, bin_, w_in, w_out, *gather)


QROWS = 80
BIAS_AT = DH


def _place_matrices():
    import numpy as np
    pk = np.zeros((NH, LANES, LANES), np.float32)
    pqt = np.zeros((NH, LANES, LANES), np.float32)
    for h in range(NH):
        for piece in range(3):
            pk[h, 8 * piece + h, BIAS_AT + 3 + piece] = -1.0
            pqt[h, BIAS_AT + piece, 8 * piece + h] = 1.0
    pkt = np.transpose(pk, (0, 2, 1))
    return tuple(jnp.asarray(m, MXU_DT) for m in (pk, pqt, pkt))


def _mix_proj_fwd(xh1, g1, b1, w_kv, w_qkv_t, w_bch, w_f, bf_pad, name):
    t = xh1.shape[0]
    tm = _tile(t, 512)
    pk, pqt, pkt = _place_matrices()

    def body(x_ref, g_ref, b_ref, wkv_ref, wt_ref, wb_ref, wf_ref, bf_ref, pk_ref, pqt_ref, pkt_ref,
             ka_ref, va_ref, qat_ref, kat_ref, vta_ref, bch_ref, z_ref, r_ref, carry):
        @pl.when(pl.program_id(0) == 0)
        def _():
            carry[...] = jnp.zeros_like(carry)

        xb = (x_ref[...] * g_ref[...] + b_ref[...]).astype(MXU_DT)
        kv = _mm(xb, wkv_ref[...])
        qkvt = _mm_nt(wt_ref[...], xb)
        bch_ref[...] = _mm(xb, wb_ref[...])
        z = _mm(xb, wf_ref[...]) + bf_ref[...]
        z_ref[...] = z
        logf = jnp.minimum(z, 0.0) - jnp.log(1.0 + jnp.exp(-jnp.abs(z)))
        row = lax.broadcasted_iota(jnp.int32, (tm, tm), 0)
        col = lax.broadcasted_iota(jnp.int32, (tm, tm), 1)
        tri = jnp.where(row >= col, 1.0, 0.0).astype(MXU_DT)
        c = carry[...] + _mm_sel(tri, logf)
        carry[...] = c[tm - 1:tm, :]
        r_ref[0] = c[0:1, :]
        lane = lax.broadcasted_iota(jnp.int32, (1, LANES), 1)
        hi, mid, lo = _split3(jnp.where(lane < NH, c - c[0:1, :], 0.0))
        pieces = (hi.astype(F32) + pltpu.roll(mid.astype(F32), 8, 1) + pltpu.roll(lo.astype(F32), 16, 1)
                  ).astype(MXU_DT)
        sub = lax.broadcasted_iota(jnp.int32, (DH, 1), 0)
        ones_k_lanes = jnp.where((lane >= BIAS_AT) & (lane < BIAS_AT + 3), 1.0, 0.0)
        ones_q_rows = jnp.where((sub >= 3) & (sub < 6), 1.0, 0.0)
        ones_k_rows = jnp.where(sub[0:QROWS - DH] < 3, 1.0, 0.0)
        first_row = jnp.where(sub == 0, 1.0, 0.0) + jnp.zeros((DH, tm), F32)
        for h in range(NH):
            pair, odd = divmod(h, 2)
            k2 = kv[:, LANES * pair:LANES * (pair + 1)]
            v2 = kv[:, DA + LANES * pair:DA + LANES * (pair + 1)]
            if odd:
                k2, v2 = pltpu.roll(k2, DH, 1), pltpu.roll(v2, DH, 1)
            ka_ref[h] = jnp.where(lane < DH, k2, _mm(pieces, pk_ref[h]) + ones_k_lanes).astype(ka_ref.dtype)
            va_ref[h] = jnp.where(lane < DH, v2, 0.0).astype(va_ref.dtype)
            qat_ref[h, 0:DH, :] = (qkvt[DH * h:DH * (h + 1)] * 0.125).astype(qat_ref.dtype)
            qat_ref[h, DH:LANES, :] = (_mm_nt(pqt_ref[h], pieces)[DH:LANES] + ones_q_rows).astype(qat_ref.dtype)
            kat_ref[h, 0:DH, :] = qkvt[DA + DH * h:DA + DH * (h + 1)].astype(kat_ref.dtype)
            kat_ref[h, DH:QROWS, :] = (_mm_nt(pkt_ref[h], pieces)[DH:QROWS] + ones_k_rows).astype(kat_ref.dtype)
            vt = qkvt[2 * DA + DH * h:2 * DA + DH * (h + 1)]
            vta_ref[h, 0:DH, :] = (first_row if odd else vt).astype(vta_ref.dtype)
            vta_ref[h, DH:LANES, :] = (vt if odd else first_row).astype(vta_ref.dtype)

    row = lambda w: pl.BlockSpec((tm, w), lambda i: (i, 0))
    full = lambda a: pl.BlockSpec(a.shape, lambda i: (0,) * a.ndim)
    nat = pl.BlockSpec((NH, tm, LANES), lambda i: (0, i, 0))
    fmaj = lambda rows: pl.BlockSpec((NH, rows, tm), lambda i: (0, 0, i))
    return pl.pallas_call(
        body, name=name, grid=(t // tm,),
        in_specs=[row(D), full(g1), full(b1), full(w_kv), full(w_qkv_t), full(w_bch), full(w_f), full(bf_pad),
                  full(pk), full(pqt), full(pkt)],
        out_specs=(nat, nat, fmaj(LANES), fmaj(QROWS), fmaj(LANES), row(3 * DCV), row(LANES),
                   pl.BlockSpec((1, 1, LANES), lambda i: (i, 0, 0))),
        out_shape=(jax.ShapeDtypeStruct((NH, t, LANES), MXU_DT), jax.ShapeDtypeStruct((NH, t, LANES), MXU_DT),
                   jax.ShapeDtypeStruct((NH, LANES, t), MXU_DT), jax.ShapeDtypeStruct((NH, QROWS, t), MXU_DT),
                   jax.ShapeDtypeStruct((NH, LANES, t), MXU_DT), jax.ShapeDtypeStruct((t, 3 * DCV), F32),
                   jax.ShapeDtypeStruct((t, LANES), F32), jax.ShapeDtypeStruct((t // tm, 1, LANES), F32)),
        scratch_shapes=[pltpu.VMEM((1, LANES), F32)],
        compiler_params=_params(56),
    )(xh1, g1, b1, w_kv, w_qkv_t, w_bch, w_f, bf_pad, pk, pqt, pkt)


def _attn_fwd(qat, ka, vta, r, name):
    t = ka.shape[1]
    tq = _tile(t, 512)
    nq = t // tq

    def body(r_ref, q_ref, k_ref, v_ref, o_ref, l_ref, st0, st1):
        hp, i = pl.program_id(0), pl.program_id(1)
        key = lax.broadcasted_iota(jnp.int32, (tq, tq), 0)
        qry = lax.broadcasted_iota(jnp.int32, (tq, tq), 1)

        def tile_of(pos):
            return jnp.where(pos == 0, i, pos - 1)

        def scores(pos, buf, masked):
            off = pl.multiple_of(tile_of(pos) * tq, tq)
            for a in range(2):
                st = _mm(k_ref[a, pl.ds(off, tq), :], q_ref[a])
                buf[a] = jnp.where(qry >= key, st, NEG) if masked else st

        def consume(pos, buf, carry):
            j = tile_of(pos)
            off = pl.multiple_of(j * tq, tq)
            out = []
            for a in range(2):
                m, acc = carry[a]
                st = buf[a]
                d = r_ref[2 * hp + a, i] - r_ref[2 * hp + a, j]
                m_new = jnp.maximum(m, jnp.max(st, axis=0, keepdims=True) + d)
                pt = jnp.exp(st - (m_new - d))
                acc = jnp.exp(m - m_new) * acc + _mm(v_ref[a, :, pl.ds(off, tq)], pt.astype(MXU_DT))
                out.append((m_new, acc))
            return tuple(out)

        def trip(p, carry):
            scores(2 * p + 1, st1, False)
            carry = consume(2 * p, st0, carry)
            scores(2 * p + 2, st0, False)
            return consume(2 * p + 1, st1, carry)

        scores(0, st0, True)
        init = tuple((jnp.full((1, tq), NEG, F32), jnp.zeros((LANES, tq), F32)) for _ in range(2))
        trips = i // 2
        carry = lax.fori_loop(0, trips, trip, init)

        def last_two(cr):
            scores(2 * trips + 1, st1, False)
            return consume(2 * trips + 1, st1, consume(2 * trips, st0, cr))

        (ma, acca), (mb, accb) = lax.cond(i % 2 == 1, last_two, lambda cr: consume(2 * trips, st0, cr), carry)
        la, lb = acca[DH:DH + 1, :], accb[0:1, :]
        l_ref[0] = ma + jnp.log(la)
        l_ref[1] = mb + jnp.log(lb)
        sub = lax.broadcasted_iota(jnp.int32, (LANES, tq), 0)
        o_ref[...] = jnp.where(sub < DH, acca / la, accb / lb).T

    return pl.pallas_call(
        body, name=name, grid=(NH // 2, nq),
        in_specs=[pl.BlockSpec(memory_space=pltpu.SMEM),
                  pl.BlockSpec((2, LANES, tq), lambda p, i: (p, 0, i)),
                  pl.BlockSpec((2, t, LANES), lambda p, i: (p, 0, 0)),
                  pl.BlockSpec((2, LANES, t), lambda p, i: (p, 0, 0))],
        out_specs=(pl.BlockSpec((tq, LANES), lambda p, i: (i, p)),
                   pl.BlockSpec((2, 1, tq), lambda p, i: (p, 0, i))),
        out_shape=(jax.ShapeDtypeStruct((t, DA), F32), jax.ShapeDtypeStruct((NH, 1, t), F32)),
        scratch_shapes=[pltpu.VMEM((2, tq, tq), F32), pltpu.VMEM((2, tq, tq), F32)],
        compiler_params=_params(48, 2),
    )(r, qat, ka, vta)


def _conv_parts(bch):
    return bch[:, 0:DCV], bch[:, DCV:2 * DCV], bch[:, 2 * DCV:3 * DCV]


def _mix_post_fwd(o, bch, conv_w, g_attn, g_conv, xh1, g1, b1, w_mo, name):
    t = o.shape[0]
    tm = _tile(t, 512)
    hb = tm // 8

    def body(o_ref, bch_ref, halo_ref, cw_ref, ga_ref, gc_ref, x_ref, g_ref, b_ref, w_ref,
             mg_ref, xh_ref, rs_ref, ext):
        i = pl.program_id(0)
        an, _ = _rms_fwd(o_ref[...])
        mg_ref[:, 0:DA] = (an * ga_ref[...]).astype(mg_ref.dtype)
        bb, cc, hh = _conv_parts(bch_ref[...])
        _, hc, hh_h = _conv_parts(halo_ref[...])
        u = cc * hh
        ext[0:8, :] = jnp.where(i > 0, hc * hh_h, 0.0)
        ext[8:8 + tm, :] = u
        raw = cw_ref[0:1, :] * ext[6:6 + tm, :] + cw_ref[1:2, :] * ext[7:7 + tm, :] + cw_ref[2:3, :] * u
        cn, _ = _rms_fwd(bb * raw)
        mg_ref[:, DA:D] = (cn * gc_ref[...]).astype(mg_ref.dtype)
        x1 = x_ref[...] * g_ref[...] + b_ref[...]
        xh, rstd = _ln_fwd(ALPHA * x1 + _mm(mg_ref[...], w_ref[...]))
        xh_ref[...] = xh
        rs_ref[...] = rstd

    row = lambda w: pl.BlockSpec((tm, w), lambda i: (i, 0))
    full = lambda a: pl.BlockSpec(a.shape, lambda i: (0, 0))
    return pl.pallas_call(
        body, name=name, grid=(t // tm,),
        in_specs=[row(DA), row(3 * DCV),
                  pl.BlockSpec((8, 3 * DCV), lambda i: (jnp.maximum(i * hb - 1, 0), 0)),
                  full(conv_w), full(g_attn), full(g_conv), row(D), full(g1), full(b1), full(w_mo)],
        out_specs=(row(D), row(D), pl.BlockSpec((tm, 1), lambda i: (i, 0))),
        out_shape=(jax.ShapeDtypeStruct((t, D), MXU_DT), jax.ShapeDtypeStruct((t, D), F32),
                   jax.ShapeDtypeStruct((t, 1), F32)),
        scratch_shapes=[pltpu.VMEM((tm + 8, DCV), F32)],
        compiler_params=_params(48),
    )(o, bch, bch, conv_w, g_attn, g_conv, xh1, g1, b1, w_mo)


def _tail(xh3, rs3, g3, b3, p, w_g, w_ple, bg, g4, b4, target, name):
    t = xh3.shape[0]
    tm = _tile(t, 512)

    def body(x_ref, rs_ref, g3_ref, b3_ref, p_ref, wg_ref, wp_ref, bg_ref, g4_ref, b4_ref, t_ref,
             dr_ref, dz_ref, de_ref, st_ref):
        @pl.when(pl.program_id(0) == 0)
        def _():
            st_ref[...] = jnp.zeros_like(st_ref)

        xh3v = x_ref[...]
        x3 = xh3v * g3_ref[...] + b3_ref[...]
        gate = _sigmoid(_mm(x3.astype(MXU_DT), wg_ref[...]) + bg_ref[...])
        e = _mm(p_ref[...].astype(MXU_DT), wp_ref[...])
        xh4, rstd4 = _ln_fwd(ALPHA * x3 + gate * e)
        diff = xh4 * g4_ref[...] + b4_ref[...] - t_ref[...]
        dy = diff * (1.0 / D)
        st_ref[5:6, :] += _colsum(diff * diff)
        st_ref[0:1, :] += _colsum(dy * xh4)
        st_ref[1:2, :] += _colsum(dy)
        dr4 = _ln_bwd(dy * g4_ref[...], xh4, rstd4)
        de_ref[...] = (dr4 * gate).astype(de_ref.dtype)
        dz = dr4 * e * gate * (1.0 - gate)
        st_ref[2:3, :] += _colsum(dz)
        dzb = dz.astype(MXU_DT)
        dz_ref[...] = dzb
        dx3 = ALPHA * dr4 + _mm_nt(dzb, wg_ref[...])
        st_ref[3:4, :] += _colsum(dx3 * xh3v)
        st_ref[4:5, :] += _colsum(dx3)
        dr_ref[...] = _ln_bwd(dx3 * g3_ref[...], xh3v, rs_ref[...])

    row = lambda w: pl.BlockSpec((tm, w), lambda i: (i, 0))
    full = lambda a: pl.BlockSpec(a.shape, lambda i: (0, 0))
    return pl.pallas_call(
        body, name=name, grid=(t // tm,),
        in_specs=[row(D), row(1), full(g3), full(b3), row(PLE), full(w_g), full(w_ple), full(bg), full(g4),
                  full(b4), row(D)],
        out_specs=(row(D), row(D), row(D), pl.BlockSpec((8, D), lambda i: (0, 0))),
        out_shape=(jax.ShapeDtypeStruct((t, D), F32), jax.ShapeDtypeStruct((t, D), MXU_DT),
                   jax.ShapeDtypeStruct((t, D), MXU_DT), jax.ShapeDtypeStruct((8, D), F32)),
        compiler_params=_params(48),
    )(xh3, rs3, g3, b3, p, w_g, w_ple, bg, g4, b4, target)


def _ffn_bwd(dr, gact, uact, xin, rsin, gin, w_in, w_out, prev_ln, name, exchange=()):
    t = dr.shape[0]
    tm = _tile(t, 512)
    nt = t // tm
    chunks = _f_chunks()
    ne = len(exchange)

    def body(*refs):
        dr_ref, g_ref, u_ref, x_ref, rs_ref, gi_ref, win_hbm, wout_hbm = refs[:8]
        df_ref, dg_ref, du_ref, dx_ref, st_ref = refs[8 + ne:13 + ne]
        win_v, wout_v = refs[13 + 2 * ne:15 + 2 * ne]
        acc_ref = dx_ref
        if ne:
            e_start, e_finish = _exchange_phases(refs[8:8 + ne], refs[13 + ne:13 + 2 * ne], *refs[15 + 2 * ne:])

        @pl.when(pl.program_id(0) == 0)
        def _():
            if ne:
                e_start()
            pltpu.sync_copy(win_hbm, win_v)
            pltpu.sync_copy(wout_hbm, wout_v)
            st_ref[...] = jnp.zeros_like(st_ref)

        if ne:
            pl.when(pl.program_id(0) == nt - 1)(e_finish)

        drv = dr_ref[...]
        dfb = (0.5 * drv).astype(MXU_DT)
        df_ref[...] = dfb
        for ci, (c0, fc) in enumerate(chunks):
            dh = _mm_nt(dfb, wout_v[c0:c0 + fc, :])
            g = g_ref[:, c0:c0 + fc].astype(F32)
            u = u_ref[:, c0:c0 + fc].astype(F32)
            sg = _sigmoid(g)
            dgb = (dh * u * (sg * (1.0 + g * (1.0 - sg)))).astype(MXU_DT)
            dub = (dh * (g * sg)).astype(MXU_DT)
            dg_ref[:, c0:c0 + fc] = dgb
            du_ref[:, c0:c0 + fc] = dub
            part = _mm_nt(dgb, win_v[:, c0:c0 + fc]) + _mm_nt(dub, win_v[:, F + c0:F + c0 + fc])
            if ci == 0:
                acc_ref[...] = part
            else:
                acc_ref[...] += part
        dx = ALPHA * drv + acc_ref[...]
        if prev_ln:
            xh = x_ref[...]
            st_ref[0:1, :] += _colsum(dx * xh)
            st_ref[1:2, :] += _colsum(dx)
            dx_ref[...] = _ln_bwd(dx * gi_ref[...], xh, rs_ref[...])
        else:
            dx_ref[...] = dx

    row = pl.BlockSpec((tm, D), lambda i: (i, 0))
    vec = pl.BlockSpec((1, D), lambda i: (0, 0))
    act = pl.BlockSpec((tm, F), lambda i: (i, 0))
    return pl.pallas_call(
        body, name=name, grid=(nt,),
        in_specs=[row, act, act, row, pl.BlockSpec((tm, 1), lambda i: (i, 0)), vec, ANY, ANY] + [ANY] * ne,
        out_specs=(row, act, act, row, pl.BlockSpec((8, D), lambda i: (0, 0))) + (ANY,) * ne,
        out_shape=(jax.ShapeDtypeStruct((t, D), MXU_DT), jax.ShapeDtypeStruct((t, F), MXU_DT),
                   jax.ShapeDtypeStruct((t, F), MXU_DT), jax.ShapeDtypeStruct((t, D), F32),
                   jax.ShapeDtypeStruct((8, D), F32))
        + tuple(jax.ShapeDtypeStruct(a.shape, a.dtype) for a in exchange),
        scratch_shapes=[pltpu.VMEM((D, 2 * F), MXU_DT), pltpu.VMEM((F, D), MXU_DT)]
        + (_exchange_sems(ne) if ne else []),
        compiler_params=_params(60),
    )(dr, gact, uact, xin, rsin, gin, w_in, w_out, *exchange)


def _mix_post_bwd(dr2, o, bch, conv_w, g_attn, g_conv, w_mo, name):
    t = dr2.shape[0]
    tm = _tile(t, 512)
    hb = tm // 8

    def body(dr_ref, o_ref, bch_ref, halo_ref, cw_ref, ga_ref, gc_ref, w_ref,
             dm_ref, do_ref, dl_ref, dy_ref, st_ref, ext):
        i = pl.program_id(0)

        @pl.when(i == 0)
        def _():
            st_ref[...] = jnp.zeros_like(st_ref)

        dmb = dr_ref[...].astype(MXU_DT)
        dm_ref[...] = dmb
        dmg = _mm_nt(dmb, w_ref[...])
        ov = o_ref[...]
        an, ra = _rms_fwd(ov)
        da = dmg[:, 0:DA]
        st_ref[0:1, :] += _colsum(da * an)
        dxa = _rms_bwd(da * ga_ref[...], an, ra)
        dor = dxa.astype(MXU_DT).astype(F32)
        dot = dor.T
        for h in range(NH):
            do_ref[h, 0:DH, :] = dot[DH * h:DH * (h + 1)].astype(do_ref.dtype)
            do_ref[h, DH:LANES, :] = jnp.zeros((LANES - DH, tm), do_ref.dtype)
        srow = lax.broadcasted_iota(jnp.int32, (8, DA), 0)
        scol = lax.broadcasted_iota(jnp.int32, (8, DA), 1)
        sel = jnp.where((scol // DH) == srow, 1.0, 0.0).astype(MXU_DT)
        hi, mid, lo = _split3(dor * ov)
        delta = _mm_nt(sel, hi) + _mm_nt(sel, mid) + _mm_nt(sel, lo)
        for h in range(NH):
            dl_ref[h] = delta[h:h + 1, :]
        bb, cc, hh = _conv_parts(bch_ref[...])
        _, hc, hh_h = _conv_parts(halo_ref[...])
        u = cc * hh
        ext[0:8, :] = jnp.where(i > 0, hc * hh_h, 0.0)
        ext[8:8 + tm, :] = u
        raw = cw_ref[0:1, :] * ext[6:6 + tm, :] + cw_ref[1:2, :] * ext[7:7 + tm, :] + cw_ref[2:3, :] * u
        cn, rc = _rms_fwd(bb * raw)
        dcn = dmg[:, DA:D]
        st_ref[1:2, :] += _colsum(dcn * cn)
        dy_ref[...] = _rms_bwd(dcn * gc_ref[...], cn, rc)

    row = lambda w: pl.BlockSpec((tm, w), lambda i: (i, 0))
    full = lambda a: pl.BlockSpec(a.shape, lambda i: (0, 0))
    return pl.pallas_call(
        body, name=name, grid=(t // tm,),
        in_specs=[row(D), row(DA), row(3 * DCV),
                  pl.BlockSpec((8, 3 * DCV), lambda i: (jnp.maximum(i * hb - 1, 0), 0)),
                  full(conv_w), full(g_attn), full(g_conv), full(w_mo)],
        out_specs=(row(D), pl.BlockSpec((NH, LANES, tm), lambda i: (0, 0, i)),
                   pl.BlockSpec((NH, 1, tm), lambda i: (0, 0, i)), row(DCV),
                   pl.BlockSpec((8, DA), lambda i: (0, 0))),
        out_shape=(jax.ShapeDtypeStruct((t, D), MXU_DT), jax.ShapeDtypeStruct((NH, LANES, t), MXU_DT),
                   jax.ShapeDtypeStruct((NH, 1, t), F32), jax.ShapeDtypeStruct((t, DCV), F32),
                   jax.ShapeDtypeStruct((8, DA), F32)),
        scratch_shapes=[pltpu.VMEM((tm + 8, DCV), F32)],
        compiler_params=_params(48),
    )(dr2, o, bch, bch, conv_w, g_attn, g_conv, w_mo)


def _conv_bwd(dy, bch, conv_w, name):
    t = dy.shape[0]
    tm = _tile(t, 512)
    hb = tm // 8
    nt = t // tm

    def body(dy_ref, dyn_ref, bch_ref, prev_ref, next_ref, cw_ref, out_ref, st_ref, ext_u, ext_d):
        i = pl.program_id(0)

        @pl.when(i == 0)
        def _():
            st_ref[...] = jnp.zeros_like(st_ref)

        bb, cc, hh = _conv_parts(bch_ref[...])
        _, pc, ph = _conv_parts(prev_ref[...])
        nb, _, _ = _conv_parts(next_ref[...])
        u = cc * hh
        ext_u[0:8, :] = jnp.where(i > 0, pc * ph, 0.0)
        ext_u[8:8 + tm, :] = u
        u1 = ext_u[7:7 + tm, :]
        u2 = ext_u[6:6 + tm, :]
        w0, w1, w2 = cw_ref[0:1, :], cw_ref[1:2, :], cw_ref[2:3, :]
        dyv = dy_ref[...]
        out_ref[:, 0:DCV] = (dyv * (w0 * u2 + w1 * u1 + w2 * u)).astype(out_ref.dtype)
        dcr = dyv * bb
        ext_d[0:tm, :] = dcr
        ext_d[tm:tm + 8, :] = jnp.where(i < nt - 1, dyn_ref[...] * nb, 0.0)
        du = w2 * dcr + w1 * ext_d[1:1 + tm, :] + w0 * ext_d[2:2 + tm, :]
        out_ref[:, DCV:2 * DCV] = (du * hh).astype(out_ref.dtype)
        out_ref[:, 2 * DCV:3 * DCV] = (du * cc).astype(out_ref.dtype)
        st_ref[0:1, :] += _colsum(dcr * u2)
        st_ref[1:2, :] += _colsum(dcr * u1)
        st_ref[2:3, :] += _colsum(dcr * u)

    row = lambda w: pl.BlockSpec((tm, w), lambda i: (i, 0))
    prev = lambda w: pl.BlockSpec((8, w), lambda i: (jnp.maximum(i * hb - 1, 0), 0))
    nxt = lambda w: pl.BlockSpec((8, w), lambda i: (jnp.minimum((i + 1) * hb, nt * hb - 1), 0))
    return pl.pallas_call(
        body, name=name, grid=(nt,),
        in_specs=[row(DCV), nxt(DCV), row(3 * DCV), prev(3 * DCV), nxt(3 * DCV),
                  pl.BlockSpec(conv_w.shape, lambda i: (0, 0))],
        out_specs=(row(3 * DCV), pl.BlockSpec((8, DCV), lambda i: (0, 0))),
        out_shape=(jax.ShapeDtypeStruct((t, 3 * DCV), MXU_DT), jax.ShapeDtypeStruct((8, DCV), F32)),
        scratch_shapes=[pltpu.VMEM((tm + 8, DCV), F32), pltpu.VMEM((tm + 8, DCV), F32)],
        compiler_params=_params(48),
    )(dy, dy, bch, bch, bch, conv_w)


def _attn_bwd(ka, kat, va, qat, dot, lrow, drow, r, name):
    t = ka.shape[1]
    tq = _tile(t, 512)
    nq = t // tq

    def body(r_ref, ka_ref, kat_ref, va_ref, l_ref, dl_ref, qat_hbm, dot_hbm,
             dk_ref, dv_ref, dck_ref, dqt_hbm, dcq_hbm, qat_v, dot_v, dq_acc):
        hp, j = pl.program_id(0), pl.program_id(1)

        @pl.when(j == 0)
        def _():
            pltpu.sync_copy(qat_hbm.at[pl.ds(2 * hp, 2)], qat_v)
            pltpu.sync_copy(dot_hbm.at[pl.ds(2 * hp, 2)], dot_v)
            dq_acc[...] = jnp.zeros_like(dq_acc)

        key = lax.broadcasted_iota(jnp.int32, (tq, tq), 0)
        qry = lax.broadcasted_iota(jnp.int32, (tq, tq), 1)

        def step(i, carry, masked):
            off = pl.multiple_of(i * tq, tq)
            out = []
            for a in range(2):
                dk, dv = carry[a]
                st = _mm(ka_ref[a], qat_v[a, :, pl.ds(off, tq)])
                dpt = _mm(va_ref[a], dot_v[a, :, pl.ds(off, tq)])
                if masked:
                    st = jnp.where(qry >= key, st, NEG)
                d = r_ref[2 * hp + a, i] - r_ref[2 * hp + a, j]
                pt = jnp.exp(st - (l_ref[a, :, pl.ds(off, tq)] - d))
                dsb = (pt * (dpt - dl_ref[a, :, pl.ds(off, tq)])).astype(MXU_DT)
                dv = dv + _mm_nt(dot_v[a, 0:DH, pl.ds(off, tq)], pt.astype(MXU_DT))
                dk = dk + _mm_nt(qat_v[a, 0:QROWS, pl.ds(off, tq)], dsb)
                dq_acc[a, :, pl.ds(off, tq)] += _mm(kat_ref[a], dsb)
                out.append((dk, dv))
            return tuple(out)

        init = tuple((jnp.zeros((QROWS, tq), F32), jnp.zeros((DH, tq), F32)) for _ in range(2))
        carry = step(j, init, True)
        (dka, dva), (dkb, dvb) = lax.fori_loop(j + 1, nq, lambda i, cr: step(i, cr, False), carry)
        dk_ref[...] = jnp.concatenate([dka[0:DH], dkb[0:DH]], axis=0).T.astype(dk_ref.dtype)
        dv_ref[...] = jnp.concatenate([dva, dvb], axis=0).T.astype(dv_ref.dtype)
        dck_ref[0] = -dka[DH + 3:DH + 4, :]
        dck_ref[1] = -dkb[DH + 3:DH + 4, :]

        @pl.when(j == nq - 1)
        def _():
            pltpu.sync_copy(dq_acc, dqt_hbm.at[pl.ds(2 * hp, 2)])
            pltpu.sync_copy(dq_acc.at[:, DH:DH + 1, :], dcq_hbm.at[pl.ds(2 * hp, 2)])

    pair = lambda rows, cols: pl.BlockSpec((2, rows, cols), lambda p, j: (p, 0, 0))
    return pl.pallas_call(
        body, name=name, grid=(NH // 2, nq),
        in_specs=[pl.BlockSpec(memory_space=pltpu.SMEM),
                  pl.BlockSpec((2, tq, LANES), lambda p, j: (p, j, 0)),
                  pl.BlockSpec((2, QROWS, tq), lambda p, j: (p, 0, j)),
                  pl.BlockSpec((2, tq, LANES), lambda p, j: (p, j, 0)),
                  pair(1, t), pair(1, t), ANY, ANY],
        out_specs=(pl.BlockSpec((tq, LANES), lambda p, j: (j, p)),
                   pl.BlockSpec((tq, LANES), lambda p, j: (j, p)),
                   pl.BlockSpec((2, 1, tq), lambda p, j: (p, 0, j)), ANY, ANY),
        out_shape=(jax.ShapeDtypeStruct((t, DA), MXU_DT), jax.ShapeDtypeStruct((t, DA), MXU_DT),
                   jax.ShapeDtypeStruct((NH, 1, t), F32), jax.ShapeDtypeStruct((NH, QROWS, t), F32),
                   jax.ShapeDtypeStruct((NH, 1, t), F32)),
        scratch_shapes=[pltpu.VMEM((2, LANES, t), MXU_DT), pltpu.VMEM((2, LANES, t), MXU_DT),
                        pltpu.VMEM((2, QROWS, t), F32)],
        compiler_params=_params(52, 2),
    )(r, ka, kat, va, lrow, drow, qat, dot)


def _mix_proj_bwd(dr2, dqt, dk, dv, dbch, dc, z, xh1, rs1, g1, w_qkv, w_bch, w_f, name):
    t = dr2.shape[0]
    tm = _tile(t, 512)
    nt = t // tm

    def body(dr_ref, dqt_ref, dk_ref, dv_ref, db_ref, dc_ref, z_ref, x_ref, rs_ref, g_ref,
             wq_ref, wb_ref, wf_ref, out_ref, df_ref, dq_ref, st_ref, carry):
        @pl.when(pl.program_id(0) == 0)
        def _():
            carry[...] = jnp.zeros_like(carry)
            st_ref[...] = jnp.zeros_like(st_ref)

        dq_ref[...] = (jnp.concatenate([dqt_ref[h, 0:DH, :] for h in range(NH)], axis=0).T * 0.125
                       ).astype(dq_ref.dtype)

        row = lax.broadcasted_iota(jnp.int32, (tm, tm), 0)
        col = lax.broadcasted_iota(jnp.int32, (tm, tm), 1)
        triu = jnp.where(col >= row, 1.0, 0.0).astype(MXU_DT)
        dlogf = carry[...] + _mm_sel(triu, dc_ref[...])
        carry[...] = dlogf[0:1, :]
        dz = dlogf / (1.0 + jnp.exp(z_ref[...]))
        st_ref[2:3, 0:LANES] += _colsum(dz)
        dfb = dz.astype(MXU_DT)
        df_ref[...] = dfb
        dx = (ALPHA * dr_ref[...]
              + _mm_nt(dq_ref[...], wq_ref[:, 0:DA])
              + _mm_nt(dk_ref[...], wq_ref[:, DA:2 * DA])
              + _mm_nt(dv_ref[...], wq_ref[:, 2 * DA:3 * DA])
              + _mm_nt(db_ref[...], wb_ref[...])
              + _mm_nt(dfb, wf_ref[...]))
        xh = x_ref[...]
        st_ref[0:1, :] += _colsum(dx * xh)
        st_ref[1:2, :] += _colsum(dx)
        out_ref[...] = _ln_bwd(dx * g_ref[...], xh, rs_ref[...])

    row = lambda w: pl.BlockSpec((tm, w), lambda i: (nt - 1 - i, 0))
    full = lambda a: pl.BlockSpec(a.shape, lambda i: (0, 0))
    return pl.pallas_call(
        body, name=name, grid=(nt,),
        in_specs=[row(D), pl.BlockSpec((NH, QROWS, tm), lambda i: (0, 0, nt - 1 - i)), row(DA), row(DA),
                  row(3 * DCV), row(LANES), row(LANES), row(D), row(1),
                  full(g1), full(w_qkv), full(w_bch), full(w_f)],
        out_specs=(row(D), row(LANES), row(DA), pl.BlockSpec((8, D), lambda i: (0, 0))),
        out_shape=(jax.ShapeDtypeStruct((t, D), F32), jax.ShapeDtypeStruct((t, LANES), MXU_DT),
                   jax.ShapeDtypeStruct((t, DA), MXU_DT), jax.ShapeDtypeStruct((8, D), F32)),
        scratch_shapes=[pltpu.VMEM((1, LANES), F32)],
        compiler_params=_params(48),
    )(dr2, dqt, dk, dv, dbch, dc, z, xh1, rs1, g1, w_qkv, w_bch, w_f)


def _dw(mode, a_parts, b, m, n, name, tmm=None, tn=None, exchange=()):
    t = b.shape[0]
    tmm = tmm or m
    tn = tn or n
    tt = _tile(t, 2048)
    na, ne = len(a_parts), len(exchange)
    grid = (m // tmm, n // tn, t // tt)

    def body(*refs):
        a_refs, b_ref, o_ref = refs[:na], refs[na], refs[na + 1 + ne]
        if ne:
            e_start, e_finish = _exchange_phases(refs[na + 1:na + 1 + ne], refs[na + 2 + ne:na + 2 + 2 * ne],
                                                 *refs[na + 2 + 2 * ne:])
            at = lambda steps: functools.reduce(jnp.logical_and, [pl.program_id(d) == s for d, s in enumerate(steps)])
            pl.when(at((0, 0, 0)))(e_start)

        @pl.when(pl.program_id(2) == 0)
        def _():
            o_ref[...] = jnp.zeros_like(o_ref)

        if mode == "plain":
            a = a_refs[0][...].astype(MXU_DT)
        elif mode == "affine":
            a = (a_refs[0][...] * a_refs[1][...] + a_refs[2][...]).astype(MXU_DT)
        else:
            g = a_refs[0][...].astype(F32)
            a = (g * _sigmoid(g) * a_refs[1][...].astype(F32)).astype(MXU_DT)
        o_ref[...] += _mm_tn(a, b_ref[...].astype(MXU_DT))
        if ne:
            pl.when(at(tuple(g - 1 for g in grid)))(e_finish)

    a_tile = pl.BlockSpec((tt, tmm), lambda i, j, k: (k, i))
    a_vec = pl.BlockSpec((1, tmm), lambda i, j, k: (0, i))
    a_specs = {"plain": [a_tile], "affine": [a_tile, a_vec, a_vec], "swiglu": [a_tile, a_tile]}[mode]
    res = pl.pallas_call(
        body, name=name, grid=grid,
        in_specs=a_specs + [pl.BlockSpec((tt, tn), lambda i, j, k: (k, j))] + [ANY] * ne,
        out_specs=(pl.BlockSpec((tmm, tn), lambda i, j, k: (i, j)),) + (ANY,) * ne,
        out_shape=(jax.ShapeDtypeStruct((m, n), F32),)
        + tuple(jax.ShapeDtypeStruct(a.shape, a.dtype) for a in exchange),
        scratch_shapes=_exchange_sems(ne) if ne else [],
        compiler_params=_params(52, 3),
    )(*a_parts, b, *exchange)
    return res if ne else res[0]


def _adamw(w, g, m, v):
    m = ADAM_B1 * m + (1.0 - ADAM_B1) * g
    v = ADAM_B2 * v + (1.0 - ADAM_B2) * (g * g)
    m_hat = m / (1.0 - ADAM_B1 ** ADAM_STEP)
    v_hat = v / (1.0 - ADAM_B2 ** ADAM_STEP)
    delta = -ADAM_LR * (m_hat / (jnp.sqrt(v_hat) + ADAM_EPS) + ADAM_WD * w)
    return delta, m, v


def _reduce_adamw(landed, own, w, m, v, name):
    r, c = own.shape
    tr = _tile(r, 128)

    def body(l_ref, o_ref, w_ref, m_ref, v_ref, g_out, d_out, m_out, v_out):
        me = 4 * lax.axis_index("x") + 2 * lax.axis_index("y") + lax.axis_index("c")
        g = None
        for j in range(NDEV):
            term = jnp.where(me == j, o_ref[...], l_ref[j].astype(F32))
            g = term if g is None else g + term
        g_out[...] = g
        d_out[...], m_out[...], v_out[...] = _adamw(w_ref[...], g, m_ref[...], v_ref[...])

    blk = pl.BlockSpec((tr, c), lambda i: (i, 0))
    sds = jax.ShapeDtypeStruct((r, c), F32)
    return pl.pallas_call(
        body, name=name, grid=(r // tr,),
        in_specs=[pl.BlockSpec((NDEV, tr, c), lambda i: (0, i, 0)), blk, blk, blk, blk],
        out_specs=(blk, blk, blk, blk), out_shape=(sds, sds, sds, sds),
        compiler_params=_params(40),
    )(landed, own, w, m, v)


def _sum_small(gathered, name):
    _, r, c = gathered.shape

    def body(g_ref, o_ref):
        acc = g_ref[0]
        for j in range(1, NDEV):
            acc = acc + g_ref[j]
        o_ref[...] = acc

    return pl.pallas_call(body, name=name, out_shape=jax.ShapeDtypeStruct((r, c), F32))(gathered)


def _adamw_small(g, w, m, v, name):
    def body(g_ref, w_ref, m_ref, v_ref, d_out, m_out, v_out):
        d_out[...], m_out[...], v_out[...] = _adamw(w_ref[...], g_ref[...], m_ref[...], v_ref[...])

    sds = jax.ShapeDtypeStruct(g.shape, F32)
    return pl.pallas_call(body, name=name, out_shape=(sds, sds, sds))(g, w, m, v)


def _cols_from_stack(s):
    return jnp.transpose(s, (1, 0, 2)).reshape(s.shape[1], NDEV * s.shape[2])


def _cols_to_stack(w):
    r, c = w.shape
    return jnp.transpose(w.reshape(r, NDEV, c // NDEV), (1, 0, 2))


def _rows_from_stack(s):
    return s.reshape(NDEV * s.shape[1], s.shape[2])


def _rows_to_stack(w):
    r, c = w.shape
    return w.reshape(NDEV, r // NDEV, c)


SMALL_ROWS = 16
SMALL_SLOTS = {
    "ln1_g": (0, 0, D), "ln1_b": (1, 0, D), "ln2_g": (2, 0, D), "ln2_b": (3, 0, D), "ln3_g": (4, 0, D),
    "ln3_b": (5, 0, D), "b_ple_gate": (6, 0, D), "ln4_g": (7, 0, D), "ln4_b": (8, 0, D),
    "g_attn": (9, 0, DA), "g_conv": (9, DA, DCV), "b_forget": (10, 0, NH),
}
CONVW_ROW = 11
LOSS_SLOT = (10, LANES)


def _pack_small(vals, conv_rows, loss=None):
    out = jnp.zeros((SMALL_ROWS, D), F32)
    for nm, (r, off, wd) in SMALL_SLOTS.items():
        out = out.at[r:r + 1, off:off + wd].set(vals[nm].reshape(1, wd).astype(F32))
    out = out.at[CONVW_ROW:CONVW_ROW + 3, 0:conv_rows.shape[1]].set(conv_rows.astype(F32))
    if loss is not None:
        out = out.at[LOSS_SLOT[0], LOSS_SLOT[1]].set(loss)
    return out


def _unpack_small(packed, name):
    r, off, wd = SMALL_SLOTS[name]
    return packed[r:r + 1, off:off + wd]


def kernel(x, p, ffn1_w_in, ffn1_w_out, ln1_g, ln1_b, w_mix_in, b_forget, conv_w, g_attn, g_conv, w_mix_out, ln2_g, ln2_b, ffn2_w_in, ffn2_w_out, ln3_g, ln3_b, w_ple, w_ple_gate, b_ple_gate, ln4_g, ln4_b, loss_target, m_ffn1_w_in, m_ffn1_w_out, m_ln1_g, m_ln1_b, m_w_mix_in, m_b_forget, m_conv_w, m_g_attn, m_g_conv, m_w_mix_out, m_ln2_g, m_ln2_b, m_ffn2_w_in, m_ffn2_w_out, m_ln3_g, m_ln3_b, m_w_ple, m_w_ple_gate, m_b_ple_gate, m_ln4_g, m_ln4_b, v_ffn1_w_in, v_ffn1_w_out, v_ln1_g, v_ln1_b, v_w_mix_in, v_b_forget, v_conv_w, v_g_attn, v_g_conv, v_w_mix_out, v_ln2_g, v_ln2_b, v_ffn2_w_in, v_ffn2_w_out, v_ln3_g, v_ln3_b, v_w_ple, v_w_ple_gate, v_b_ple_gate, v_ln4_g, v_ln4_b):
    args = dict(locals())
    t = x.shape[1]
    me = 4 * lax.axis_index("x") + 2 * lax.axis_index("y") + lax.axis_index("c")
    x0 = x.reshape(t, D)
    p0 = p.reshape(t, PLE)
    tgt = loss_target.reshape(t, D)

    big = ["ffn1_w_in", "ffn1_w_out", "w_mix_in", "w_mix_out", "ffn2_w_in", "ffn2_w_out", "w_ple", "w_ple_gate"]
    col_sharded = {"ffn1_w_in", "w_mix_in", "ffn2_w_in", "w_ple"}
    shard = {nm: args[nm][0] for nm in big}

    unstack = lambda nm, g: (_cols_from_stack(g) if nm in col_sharded else _rows_from_stack(g)).astype(MXU_DT)
    stack = lambda nm, g: _cols_to_stack(g) if nm in col_sharded else _rows_to_stack(g)
    wire = lambda names: [shard[nm].astype(WIRE_DT) for nm in names]
    first, later = big[:2], big[2:]

    full = {nm: unstack(nm, g) for nm, g in zip(first, _allgather(wire(first), "ag_ffn1"))}
    ffn1_out, gathered = _ffn1_fwd(x0, full, wire(later) + [conv_w[0]])
    full.update({nm: unstack(nm, g) for nm, g in zip(later, gathered)})
    cw = _cols_from_stack(gathered[len(later)])

    dr1, gw, small, loss_part = _mid_step(p0, tgt, full, cw, {nm: args[nm] for nm in SMALL_SLOTS}, ffn1_out)
    small_part = _pack_small({nm: small[nm] for nm in SMALL_SLOTS},
                             jnp.pad(small["conv_w"], ((0, 0), (0, D - DCV))), loss_part)
    stacks = {nm: stack(nm, gw[nm]) for nm in later}
    gx, gw1, landed_later, landed_w_in = _ffn1_bwd(
        x0, dr1, ffn1_out, full, [stacks[nm].astype(WIRE_DT) for nm in later],
        lambda g: stack("ffn1_w_in", g).astype(WIRE_DT))
    stacks.update({nm: stack(nm, gw1[nm]) for nm in first})
    (landed_w_out,), (small_all,) = _exchange_and_gather([stacks["ffn1_w_out"].astype(WIRE_DT)], [small_part],
                                                         "rs_ffn1_out")
    landed = dict(zip(later, landed_later), ffn1_w_in=landed_w_in, ffn1_w_out=landed_w_out)
    small_g = _sum_small(small_all, "sum_small")
    loss = small_g[LOSS_SLOT[0], LOSS_SLOT[1]]

    outs = {"loss": loss, "grad_x": gx.reshape(1, t, D)}
    for nm in big:
        own = lax.dynamic_index_in_dim(stacks[nm], me, axis=0, keepdims=False)
        g, dl, mn, vn = _reduce_adamw(landed[nm], own, shard[nm], args["m_" + nm][0], args["v_" + nm][0],
                                      "adamw_" + nm)
        outs["grad_" + nm], outs["delta_" + nm], outs["new_m_" + nm], outs["new_v_" + nm] = (
            g[None], dl[None], mn[None], vn[None])
    small_names = list(SMALL_SLOTS)
    cshard = lax.dynamic_slice_in_dim(small_g[CONVW_ROW:CONVW_ROW + 3, 0:DCV], me * (DCV // NDEV), DCV // NDEV, axis=1)
    g_pack = _pack_small({nm: _unpack_small(small_g, nm) for nm in small_names}, cshard)
    packs = [_pack_small({nm: args[pre + nm] for nm in small_names}, args[pre + "conv_w"][0])
             for pre in ("", "m_", "v_")]
    d_pack, m_pack, v_pack = _adamw_small(g_pack, packs[0], packs[1], packs[2], "adamw_small")
    for key, pk in (("grad_", g_pack), ("delta_", d_pack), ("new_m_", m_pack), ("new_v_", v_pack)):
        for nm in small_names:
            outs[key + nm] = _unpack_small(pk, nm)
        outs[key + "conv_w"] = pk[CONVW_ROW:CONVW_ROW + 3, 0:DCV // NDEV][None]

    wnames = ["ffn1_w_in", "ffn1_w_out", "ln1_g", "ln1_b", "w_mix_in", "b_forget", "conv_w", "g_attn", "g_conv",
              "w_mix_out", "ln2_g", "ln2_b", "ffn2_w_in", "ffn2_w_out", "ln3_g", "ln3_b", "w_ple", "w_ple_gate",
              "b_ple_gate", "ln4_g", "ln4_b"]
    return (outs["loss"], outs["grad_x"], *[outs[pre + nm] for pre in ("grad_", "delta_", "new_m_", "new_v_")
                                            for nm in wnames])


def _ffn1_fwd(x0, full, gather=()):
    res = _ffn_fwd(x0, jnp.ones((1, D), F32), jnp.zeros((1, D), F32), full["ffn1_w_in"], full["ffn1_w_out"],
                   "ffn1_fwd", gather)
    return res[:4], res[4:]


def _ffn1_bwd(x0, dr1, ffn1_out, full, exchange=(), w_in_slots=None):
    g1a, u1a, _, rs1 = ffn1_out
    ones, zeros = jnp.ones((1, D), F32), jnp.zeros((1, D), F32)
    res = _ffn_bwd(dr1, g1a, u1a, x0, rs1, ones, full["ffn1_w_in"], full["ffn1_w_out"], False, "ffn1_bwd",
                   exchange)
    df1, dg1, du1, gx = res[:4]
    gw_in = jnp.concatenate(
        [_dw("affine", (x0, ones, zeros), dg1, D, F, "dw_ffn1_in_g", tn=F // 2),
         _dw("affine", (x0, ones, zeros), du1, D, F, "dw_ffn1_in_u", tn=F // 2)], axis=1)
    side = () if w_in_slots is None else (w_in_slots(gw_in),)
    out = _dw("swiglu", (g1a, u1a), df1, F, D, "dw_ffn1_out", tmm=F // 2, exchange=side)
    gw_out, landed_in = (out, None) if w_in_slots is None else (out[0], out[1])
    return gx, {"ffn1_w_in": gw_in, "ffn1_w_out": gw_out}, res[5:], landed_in


def _mid_step(p0, tgt, full, cw, sp, ffn1_out):
    g1a, u1a, xh1, rs1 = ffn1_out
    t = xh1.shape[0]
    ln1_g, ln1_b, ln2_g, ln2_b, ln3_g, ln3_b = (sp[k] for k in ("ln1_g", "ln1_b", "ln2_g", "ln2_b", "ln3_g", "ln3_b"))
    ln4_g, ln4_b, g_attn, g_conv, b_ple_gate = (sp[k] for k in ("ln4_g", "ln4_b", "g_attn", "g_conv", "b_ple_gate"))
    wmi = full["w_mix_in"]
    w_qkv = wmi[:, 0:3 * DA]
    w_f = jnp.pad(wmi[:, 3 * DA:3 * DA + NH], ((0, 0), (0, LANES - NH)))
    w_bch = wmi[:, 3 * DA + NH:]
    bf_pad = jnp.pad(sp["b_forget"], ((0, 0), (0, LANES - NH)))

    ka, va, qat, kat, vta, bch, z, rt = _mix_proj_fwd(xh1, ln1_g, ln1_b, w_qkv[:, DA:], jnp.transpose(w_qkv),
                                                      w_bch, w_f, bf_pad, "mix_proj_fwd")
    rtile = jnp.transpose(rt[:, 0, 0:NH])
    o, lse = _attn_fwd(qat, ka, vta, rtile, "attn_fwd")
    merged, xh2, rs2 = _mix_post_fwd(o, bch, cw, g_attn, g_conv, xh1, ln1_g, ln1_b, full["w_mix_out"],
                                     "mix_post_fwd")
    g2a, u2a, xh3, rs3 = _ffn_fwd(xh2, ln2_g, ln2_b, full["ffn2_w_in"], full["ffn2_w_out"], "ffn2_fwd")

    dr3, dz, de, st_tail = _tail(xh3, rs3, ln3_g, ln3_b, p0, full["w_ple_gate"], full["w_ple"], b_ple_gate,
                                 ln4_g, ln4_b, tgt, "tail")
    df2, dg2, du2, dr2, st_f2 = _ffn_bwd(dr3, g2a, u2a, xh2, rs2, ln2_g, full["ffn2_w_in"], full["ffn2_w_out"],
                                         True, "ffn2_bwd")
    dmix, dot, drow, dyc, st_post = _mix_post_bwd(dr2, o, bch, cw, g_attn, g_conv, full["w_mix_out"],
                                                  "mix_post_bwd")
    dbch, st_conv = _conv_bwd(dyc, bch, cw, "conv_bwd")
    dk, dv, dck, dqt, dcq = _attn_bwd(ka, kat, va, qat, dot, lse, drow, rtile, "attn_bwd")
    dc_pad = jnp.pad(jnp.transpose((dcq + dck).reshape(NH, t)), ((0, 0), (0, LANES - NH)))
    dr1, dfl, dq, st_proj = _mix_proj_bwd(dr2, dqt, dk, dv, dbch, dc_pad, z, xh1, rs1, ln1_g, w_qkv, w_bch, w_f,
                                          "mix_proj_bwd")

    x1p, x2p, x3p = (xh1, ln1_g, ln1_b), (xh2, ln2_g, ln2_b), (xh3, ln3_g, ln3_b)
    gw = {}
    gw["ffn2_w_in"] = jnp.concatenate(
        [_dw("affine", x2p, dg2, D, F, "dw_ffn2_in_g", tn=F // 2),
         _dw("affine", x2p, du2, D, F, "dw_ffn2_in_u", tn=F // 2)], axis=1)
    gw["ffn2_w_out"] = _dw("swiglu", (g2a, u2a), df2, F, D, "dw_ffn2_out", tmm=F // 2)
    gw["w_mix_out"] = _dw("plain", (merged,), dmix, D, D, "dw_mix_out")
    gw["w_mix_in"] = jnp.concatenate(
        [_dw("affine", x1p, dq, D, DA, "dw_mix_in_q"), _dw("affine", x1p, dk, D, DA, "dw_mix_in_k"),
         _dw("affine", x1p, dv, D, DA, "dw_mix_in_v"),
         _dw("affine", x1p, dfl, D, LANES, "dw_mix_in_f")[:, 0:NH],
         _dw("affine", x1p, dbch, D, 3 * DCV, "dw_mix_in_bch")], axis=1)
    gw["w_ple_gate"] = _dw("affine", x3p, dz, D, D, "dw_ple_gate")
    gw["w_ple"] = _dw("plain", (p0,), de, PLE, D, "dw_ple")

    loss_part = (0.5 / D) * jnp.sum(st_tail[5:6, :])
    small = {"ln1_g": st_proj[0:1], "ln1_b": st_proj[1:2], "ln2_g": st_f2[0:1], "ln2_b": st_f2[1:2],
             "ln3_g": st_tail[3:4], "ln3_b": st_tail[4:5], "b_ple_gate": st_tail[2:3], "ln4_g": st_tail[0:1],
             "ln4_b": st_tail[1:2], "g_attn": st_post[0:1], "g_conv": st_post[1:2],
             "b_forget": st_proj[2:3, 0:NH], "conv_w": st_conv[0:3]}
    return dr1, gw, small, loss_part
```

```python
import functools

import jax
import jax.numpy as jnp
from jax import lax
from jax.experimental import pallas as pl
from jax.experimental.pallas import tpu as pltpu

D = 1024
F = 2816
NH = 8
DH = 64
DA = NH * DH
DCV = D - DA
PLE = 256
LN_EPS = 1e-5
RMS_EPS = 1e-6
NEG = -1e30
ALPHA = 2.0 ** 0.25
NDEV = 8
LANES = 128

ADAM_LR, ADAM_B1, ADAM_B2, ADAM_EPS, ADAM_WD, ADAM_STEP = 0.001, 0.9, 0.999, 1e-08, 0.01, 10

F32 = jnp.float32
MXU_DT = jnp.bfloat16
WIRE_DT = jnp.bfloat16

MESH_ID = pl.DeviceIdType.MESH
ANY = pl.BlockSpec(memory_space=pl.ANY)


def _params(vmem_mb, n_axes=1):
    return pltpu.CompilerParams(dimension_semantics=("arbitrary",) * n_axes,
                                vmem_limit_bytes=int(vmem_mb) << 20)


def _mm(a, b):
    return jnp.dot(a, b, preferred_element_type=F32)


def _mm_nt(a, b):
    return lax.dot_general(a, b, (((1,), (1,)), ((), ())), preferred_element_type=F32)


def _mm_tn(a, b):
    return lax.dot_general(a, b, (((0,), (0,)), ((), ())), preferred_element_type=F32)


def _split3(x):
    hi = x.astype(MXU_DT)
    r1 = x - hi.astype(F32)
    mid = r1.astype(MXU_DT)
    lo = (r1 - mid.astype(F32)).astype(MXU_DT)
    return hi, mid, lo


def _mm_sel(sel, x):
    hi, mid, lo = _split3(x)
    return _mm(sel, hi) + _mm(sel, mid) + _mm(sel, lo)


def _sigmoid(x):
    return 1.0 / (1.0 + jnp.exp(-x))


def _ln_fwd(r):
    mu = jnp.mean(r, axis=-1, keepdims=True)
    xc = r - mu
    var = jnp.mean(xc * xc, axis=-1, keepdims=True)
    rstd = lax.rsqrt(var + LN_EPS)
    return xc * rstd, rstd


def _ln_bwd(dxhat, xhat, rstd):
    m1 = jnp.mean(dxhat, axis=-1, keepdims=True)
    m2 = jnp.mean(dxhat * xhat, axis=-1, keepdims=True)
    return rstd * (dxhat - m1 - xhat * m2)


def _rms_fwd(x):
    r = lax.rsqrt(jnp.mean(x * x, axis=-1, keepdims=True) + RMS_EPS)
    return x * r, r


def _rms_bwd(dyg, xn, r):
    return r * (dyg - xn * jnp.mean(dyg * xn, axis=-1, keepdims=True))


def _colsum(x):
    return jnp.sum(x, axis=0, keepdims=True)


def _f_chunks():
    out, c0 = [], 0
    while c0 < F:
        fc = min(512, F - c0)
        out.append((c0, fc))
        c0 += fc
    return out


def _tile(t, want):
    return want if t % want == 0 and t >= want else t


def _exchange_sems(n):
    return [pltpu.SemaphoreType.DMA((n * (NDEV - 1),)), pltpu.SemaphoreType.DMA((n * (NDEV - 1),))]


def _exchange_phases(ins, outs, send_sems, recv_sems):
    n = len(ins)

    def peers():
        x, y, c = lax.axis_index("x"), lax.axis_index("y"), lax.axis_index("c")
        out = []
        for k in range(1, NDEV):
            px = 1 - x if (k >> 2) & 1 else x
            py = 1 - y if (k >> 1) & 1 else y
            pc = 1 - c if k & 1 else c
            out.append(((px, py, pc), 4 * px + 2 * py + pc))
        return 4 * x + 2 * y + c, out

    def remote(w, k, to, slot_src, slot_dst):
        return pltpu.make_async_remote_copy(
            src_ref=ins[w].at[slot_src], dst_ref=outs[w].at[slot_dst],
            send_sem=send_sems.at[w * (NDEV - 1) + k], recv_sem=recv_sems.at[w * (NDEV - 1) + k],
            device_id=to, device_id_type=MESH_ID)

    def start():
        me, prs = peers()
        for k, (to, pid) in enumerate(prs):
            for w in range(n):
                remote(w, k, to, pid, me).start()

    def finish():
        me, prs = peers()
        for k, (to, pid) in enumerate(prs):
            for w in range(n):
                remote(w, k, to, me, pid).wait_recv()
        for k, (to, pid) in enumerate(prs):
            for w in range(n):
                remote(w, k, to, pid, me).wait_send()

    return start, finish


def _gather_sems(n):
    return [pltpu.SemaphoreType.DMA((n * (NDEV - 1),)), pltpu.SemaphoreType.DMA((n * (NDEV - 1),)),
            pltpu.SemaphoreType.DMA((n,))]


def _gather_phases(ins, outs, send_sems, recv_sems, loc_sems):
    n = len(ins)
    per = NDEV - 1

    def place():
        x, y, c = lax.axis_index("x"), lax.axis_index("y"), lax.axis_index("c")
        return (x, y, c), (x, y, 1 - c), [(1 - x, y), (x, 1 - y), (1 - x, 1 - y)]

    def copy(w, k, block, to, src=None):
        dst = outs[w].at[4 * block[0] + 2 * block[1] + block[2]]
        return pltpu.make_async_remote_copy(
            src_ref=dst if src is None else src, dst_ref=dst,
            send_sem=send_sems.at[w * per + k], recv_sem=recv_sems.at[w * per + k],
            device_id=to, device_id_type=MESH_ID)

    def local(w, me):
        return pltpu.make_async_copy(ins[w], outs[w].at[4 * me[0] + 2 * me[1] + me[2]], loc_sems.at[w])

    def first(me, sib, chips):
        out = []
        for j, chip in enumerate(chips):
            out += [copy(w, 1 + j, me, (*chip, me[2]), src=ins[w]) for w in range(n)]
        return out + [copy(w, 0, me, sib, src=ins[w]) for w in range(n)]

    def start():
        me, sib, chips = place()
        for w in range(n):
            local(w, me).start()
        for cp in first(me, sib, chips):
            cp.start()

    def forward():
        me, sib, chips = place()
        for j, chip in enumerate(chips):
            for w in range(n):
                copy(w, 1 + j, (*chip, me[2]), me).wait_recv()
                copy(w, 4 + j, (*chip, me[2]), sib).start()

    def finish():
        me, sib, chips = place()
        for w in range(n):
            copy(w, 0, sib, me).wait_recv()
        for j, chip in enumerate(chips):
            for w in range(n):
                copy(w, 4 + j, (*chip, 1 - me[2]), me).wait_recv()
        for cp in first(me, sib, chips):
            cp.wait_send()
        for j, chip in enumerate(chips):
            for w in range(n):
                copy(w, 4 + j, (*chip, me[2]), sib).wait_send()
        for w in range(n):
            local(w, me).wait()

    return start, forward, finish


def _allgather(arrs, name):
    n = len(arrs)

    def body(*refs):
        start, forward, finish = _gather_phases(refs[:n], refs[n:2 * n], *refs[2 * n:])
        start()
        forward()
        finish()

    return pl.pallas_call(
        body, name=name, out_shape=tuple(jax.ShapeDtypeStruct((NDEV,) + a.shape, a.dtype) for a in arrs),
        in_specs=[ANY] * n, out_specs=tuple([ANY] * n), scratch_shapes=_gather_sems(n),
    )(*arrs)


def _exchange_and_gather(ex, ga, name):
    ne, ng = len(ex), len(ga)

    def body(*refs):
        ins, outs, sems = refs[:ne + ng], refs[ne + ng:2 * (ne + ng)], refs[2 * (ne + ng):]
        e_start, e_finish = _exchange_phases(ins[:ne], outs[:ne], *sems[:2])
        g_start, g_forward, g_finish = _gather_phases(ins[ne:], outs[ne:], *sems[2:])
        e_start()
        g_start()
        g_forward()
        g_finish()
        e_finish()

    res = pl.pallas_call(
        body, name=name,
        out_shape=tuple(jax.ShapeDtypeStruct(a.shape, a.dtype) for a in ex)
        + tuple(jax.ShapeDtypeStruct((NDEV,) + a.shape, a.dtype) for a in ga),
        in_specs=[ANY] * (ne + ng), out_specs=tuple([ANY] * (ne + ng)),
        scratch_shapes=_exchange_sems(ne) + _gather_sems(ng),
    )(*ex, *ga)
    return res[:ne], res[ne:]


def _ffn_fwd(xin, gin, bin_, w_in, w_out, name, gather=()):
    t = xin.shape[0]
    tm = _tile(t, 512)
    nt = t // tm
    chunks = _f_chunks()
    ng = len(gather)

    def body(*refs):
        x_ref, gi_ref, bi_ref, win_hbm, wout_hbm = refs[:5]
        g_ref, u_ref, xh_ref, rs_ref = refs[5 + ng:9 + ng]
        win_v, wout_v, acc_ref = refs[9 + 2 * ng:12 + 2 * ng]
        if ng:
            g_start, g_forward, g_finish = _gather_phases(refs[5:5 + ng], refs[9 + ng:9 + 2 * ng],
                                                          *refs[12 + 2 * ng:])

        @pl.when(pl.program_id(0) == 0)
        def _():
            if ng:
                g_start()
            pltpu.sync_copy(win_hbm, win_v)
            pltpu.sync_copy(wout_hbm, wout_v)

        if ng:
            pl.when(pl.program_id(0) == nt // 2)(g_forward)
            pl.when(pl.program_id(0) == nt - 1)(g_finish)

        x = x_ref[...] * gi_ref[...] + bi_ref[...]
        xb = x.astype(MXU_DT)
        for ci, (c0, fc) in enumerate(chunks):
            gc = _mm(xb, win_v[:, c0:c0 + fc])
            uc = _mm(xb, win_v[:, F + c0:F + c0 + fc])
            g_ref[:, c0:c0 + fc] = gc.astype(g_ref.dtype)
            u_ref[:, c0:c0 + fc] = uc.astype(u_ref.dtype)
            hc = (gc * _sigmoid(gc) * uc).astype(MXU_DT)
            part = _mm(hc, wout_v[c0:c0 + fc, :])
            if ci == 0:
                acc_ref[...] = part
            else:
                acc_ref[...] += part
        xh, rstd = _ln_fwd(ALPHA * x + 0.5 * acc_ref[...])
        xh_ref[...] = xh
        rs_ref[...] = rstd

    row = pl.BlockSpec((tm, D), lambda i: (i, 0))
    vec = pl.BlockSpec((1, D), lambda i: (0, 0))
    act = pl.BlockSpec((tm, F), lambda i: (i, 0))
    return pl.pallas_call(
        body, name=name, grid=(nt,),
        in_specs=[row, vec, vec, ANY, ANY] + [ANY] * ng,
        out_specs=(act, act, row, pl.BlockSpec((tm, 1), lambda i: (i, 0))) + (ANY,) * ng,
        out_shape=(jax.ShapeDtypeStruct((t, F), MXU_DT), jax.ShapeDtypeStruct((t, F), MXU_DT),
                   jax.ShapeDtypeStruct((t, D), F32), jax.ShapeDtypeStruct((t, 1), F32))
        + tuple(jax.ShapeDtypeStruct((NDEV,) + a.shape, a.dtype) for a in gather),
        scratch_shapes=[pltpu.VMEM((D, 2 * F), MXU_DT), pltpu.VMEM((F, D), MXU_DT), pltpu.VMEM((tm, D), F32)]
        + (_gather_sems(ng) if ng else []),
        compiler_params=_params(52),
    )(xin, gin, bin_, w_in, w_out, *gather)


QROWS = 80
BIAS_AT = DH


def _place_matrices():
    import numpy as np
    pk = np.zeros((NH, LANES, LANES), np.float32)
    pqt = np.zeros((NH, LANES, LANES), np.float32)
    for h in range(NH):
        for piece in range(3):
            pk[h, 8 * piece + h, BIAS_AT + 3 + piece] = -1.0
            pqt[h, BIAS_AT + piece, 8 * piece + h] = 1.0
    pkt = np.transpose(pk, (0, 2, 1))
    return tuple(jnp.asarray(m, MXU_DT) for m in (pk, pqt, pkt))


def _mix_proj_fwd(xh1, g1, b1, w_kv, w_qkv_t, w_bch, w_f, bf_pad, name):
    t = xh1.shape[0]
    tm = _tile(t, 512)
    pk, pqt, pkt = _place_matrices()

    def body(x_ref, g_ref, b_ref, wkv_ref, wt_ref, wb_ref, wf_ref, bf_ref, pk_ref, pqt_ref, pkt_ref,
             ka_ref, va_ref, qat_ref, kat_ref, vta_ref, bch_ref, z_ref, r_ref, carry):
        @pl.when(pl.program_id(0) == 0)
        def _():
            carry[...] = jnp.zeros_like(carry)

        xb = (x_ref[...] * g_ref[...] + b_ref[...]).astype(MXU_DT)
        kv = _mm(xb, wkv_ref[...])
        qkvt = _mm_nt(wt_ref[...], xb)
        bch_ref[...] = _mm(xb, wb_ref[...])
        z = _mm(xb, wf_ref[...]) + bf_ref[...]
        z_ref[...] = z
        logf = jnp.minimum(z, 0.0) - jnp.log(1.0 + jnp.exp(-jnp.abs(z)))
        row = lax.broadcasted_iota(jnp.int32, (tm, tm), 0)
        col = lax.broadcasted_iota(jnp.int32, (tm, tm), 1)
        tri = jnp.where(row >= col, 1.0, 0.0).astype(MXU_DT)
        c = carry[...] + _mm_sel(tri, logf)
        carry[...] = c[tm - 1:tm, :]
        r_ref[0] = c[0:1, :]
        lane = lax.broadcasted_iota(jnp.int32, (1, LANES), 1)
        hi, mid, lo = _split3(jnp.where(lane < NH, c - c[0:1, :], 0.0))
        pieces = (hi.astype(F32) + pltpu.roll(mid.astype(F32), 8, 1) + pltpu.roll(lo.astype(F32), 16, 1)
                  ).astype(MXU_DT)
        sub = lax.broadcasted_iota(jnp.int32, (DH, 1), 0)
        ones_k_lanes = jnp.where((lane >= BIAS_AT) & (lane < BIAS_AT + 3), 1.0, 0.0)
        ones_q_rows = jnp.where((sub >= 3) & (sub < 6), 1.0, 0.0)
        ones_k_rows = jnp.where(sub[0:QROWS - DH] < 3, 1.0, 0.0)
        first_row = jnp.where(sub == 0, 1.0, 0.0) + jnp.zeros((DH, tm), F32)
        for h in range(NH):
            pair, odd = divmod(h, 2)
            k2 = kv[:, LANES * pair:LANES * (pair + 1)]
            v2 = kv[:, DA + LANES * pair:DA + LANES * (pair + 1)]
            if odd:
                k2, v2 = pltpu.roll(k2, DH, 1), pltpu.roll(v2, DH, 1)
            ka_ref[h] = jnp.where(lane < DH, k2, _mm(pieces, pk_ref[h]) + ones_k_lanes).astype(ka_ref.dtype)
            va_ref[h] = jnp.where(lane < DH, v2, 0.0).astype(va_ref.dtype)
            qat_ref[h, 0:DH, :] = (qkvt[DH * h:DH * (h + 1)] * 0.125).astype(qat_ref.dtype)
            qat_ref[h, DH:LANES, :] = (_mm_nt(pqt_ref[h], pieces)[DH:LANES] + ones_q_rows).astype(qat_ref.dtype)
            kat_ref[h, 0:DH, :] = qkvt[DA + DH * h:DA + DH * (h + 1)].astype(kat_ref.dtype)
            kat_ref[h, DH:QROWS, :] = (_mm_nt(pkt_ref[h], pieces)[DH:QROWS] + ones_k_rows).astype(kat_ref.dtype)
            vt = qkvt[2 * DA + DH * h:2 * DA + DH * (h + 1)]
            vta_ref[h, 0:DH, :] = (first_row if odd else vt).astype(vta_ref.dtype)
            vta_ref[h, DH:LANES, :] = (vt if odd else first_row).astype(vta_ref.dtype)

    row = lambda w: pl.BlockSpec((tm, w), lambda i: (i, 0))
    full = lambda a: pl.BlockSpec(a.shape, lambda i: (0,) * a.ndim)
    nat = pl.BlockSpec((NH, tm, LANES), lambda i: (0, i, 0))
    fmaj = lambda rows: pl.BlockSpec((NH, rows, tm), lambda i: (0, 0, i))
    return pl.pallas_call(
        body, name=name, grid=(t // tm,),
        in_specs=[row(D), full(g1), full(b1), full(w_kv), full(w_qkv_t), full(w_bch), full(w_f), full(bf_pad),
                  full(pk), full(pqt), full(pkt)],
        out_specs=(nat, nat, fmaj(LANES), fmaj(QROWS), fmaj(LANES), row(3 * DCV), row(LANES),
                   pl.BlockSpec((1, 1, LANES), lambda i: (i, 0, 0))),
        out_shape=(jax.ShapeDtypeStruct((NH, t, LANES), MXU_DT), jax.ShapeDtypeStruct((NH, t, LANES), MXU_DT),
                   jax.ShapeDtypeStruct((NH, LANES, t), MXU_DT), jax.ShapeDtypeStruct((NH, QROWS, t), MXU_DT),
                   jax.ShapeDtypeStruct((NH, LANES, t), MXU_DT), jax.ShapeDtypeStruct((t, 3 * DCV), F32),
                   jax.ShapeDtypeStruct((t, LANES), F32), jax.ShapeDtypeStruct((t // tm, 1, LANES), F32)),
        scratch_shapes=[pltpu.VMEM((1, LANES), F32)],
        compiler_params=_params(56),
    )(xh1, g1, b1, w_kv, w_qkv_t, w_bch, w_f, bf_pad, pk, pqt, pkt)


def _attn_fwd(qat, ka, vta, r, name):
    t = ka.shape[1]
    tq = _tile(t, 512)
    nq = t // tq

    def body(r_ref, q_ref, k_ref, v_ref, o_ref, l_ref, st0, st1):
        hp, i = pl.program_id(0), pl.program_id(1)
        key = lax.broadcasted_iota(jnp.int32, (tq, tq), 0)
        qry = lax.broadcasted_iota(jnp.int32, (tq, tq), 1)

        def tile_of(pos):
            return jnp.where(pos == 0, i, pos - 1)

        def scores(pos, buf, masked):
            off = pl.multiple_of(tile_of(pos) * tq, tq)
            for a in range(2):
                st = _mm(k_ref[a, pl.ds(off, tq), :], q_ref[a])
                buf[a] = jnp.where(qry >= key, st, NEG) if masked else st

        def consume(pos, buf, carry):
            j = tile_of(pos)
            off = pl.multiple_of(j * tq, tq)
            out = []
            for a in range(2):
                m, acc = carry[a]
                st = buf[a]
                d = r_ref[2 * hp + a, i] - r_ref[2 * hp + a, j]
                m_new = jnp.maximum(m, jnp.max(st, axis=0, keepdims=True) + d)
                pt = jnp.exp(st - (m_new - d))
                acc = jnp.exp(m - m_new) * acc + _mm(v_ref[a, :, pl.ds(off, tq)], pt.astype(MXU_DT))
                out.append((m_new, acc))
            return tuple(out)

        def trip(p, carry):
            scores(2 * p + 1, st1, False)
            carry = consume(2 * p, st0, carry)
            scores(2 * p + 2, st0, False)
            return consume(2 * p + 1, st1, carry)

        scores(0, st0, True)
        init = tuple((jnp.full((1, tq), NEG, F32), jnp.zeros((LANES, tq), F32)) for _ in range(2))
        trips = i // 2
        carry = lax.fori_loop(0, trips, trip, init)

        def last_two(cr):
            scores(2 * trips + 1, st1, False)
            return consume(2 * trips + 1, st1, consume(2 * trips, st0, cr))

        (ma, acca), (mb, accb) = lax.cond(i % 2 == 1, last_two, lambda cr: consume(2 * trips, st0, cr), carry)
        la, lb = acca[DH:DH + 1, :], accb[0:1, :]
        l_ref[0] = ma + jnp.log(la)
        l_ref[1] = mb + jnp.log(lb)
        sub = lax.broadcasted_iota(jnp.int32, (LANES, tq), 0)
        o_ref[...] = jnp.where(sub < DH, acca / la, accb / lb).T

    return pl.pallas_call(
        body, name=name, grid=(NH // 2, nq),
        in_specs=[pl.BlockSpec(memory_space=pltpu.SMEM),
                  pl.BlockSpec((2, LANES, tq), lambda p, i: (p, 0, i)),
                  pl.BlockSpec((2, t, LANES), lambda p, i: (p, 0, 0)),
                  pl.BlockSpec((2, LANES, t), lambda p, i: (p, 0, 0))],
        out_specs=(pl.BlockSpec((tq, LANES), lambda p, i: (i, p)),
                   pl.BlockSpec((2, 1, tq), lambda p, i: (p, 0, i))),
        out_shape=(jax.ShapeDtypeStruct((t, DA), F32), jax.ShapeDtypeStruct((NH, 1, t), F32)),
        scratch_shapes=[pltpu.VMEM((2, tq, tq), F32), pltpu.VMEM((2, tq, tq), F32)],
        compiler_params=_params(48, 2),
    )(r, qat, ka, vta)


def _conv_parts(bch):
    return bch[:, 0:DCV], bch[:, DCV:2 * DCV], bch[:, 2 * DCV:3 * DCV]


def _mix_post_fwd(o, bch, conv_w, g_attn, g_conv, xh1, g1, b1, w_mo, name):
    t = o.shape[0]
    tm = _tile(t, 512)
    hb = tm // 8

    def body(o_ref, bch_ref, halo_ref, cw_ref, ga_ref, gc_ref, x_ref, g_ref, b_ref, w_ref,
             mg_ref, xh_ref, rs_ref, ext):
        i = pl.program_id(0)
        an, _ = _rms_fwd(o_ref[...])
        mg_ref[:, 0:DA] = (an * ga_ref[...]).astype(mg_ref.dtype)
        bb, cc, hh = _conv_parts(bch_ref[...])
        _, hc, hh_h = _conv_parts(halo_ref[...])
        u = cc * hh
        ext[0:8, :] = jnp.where(i > 0, hc * hh_h, 0.0)
        ext[8:8 + tm, :] = u
        raw = cw_ref[0:1, :] * ext[6:6 + tm, :] + cw_ref[1:2, :] * ext[7:7 + tm, :] + cw_ref[2:3, :] * u
        cn, _ = _rms_fwd(bb * raw)
        mg_ref[:, DA:D] = (cn * gc_ref[...]).astype(mg_ref.dtype)
        x1 = x_ref[...] * g_ref[...] + b_ref[...]
        xh, rstd = _ln_fwd(ALPHA * x1 + _mm(mg_ref[...], w_ref[...]))
        xh_ref[...] = xh
        rs_ref[...] = rstd

    row = lambda w: pl.BlockSpec((tm, w), lambda i: (i, 0))
    full = lambda a: pl.BlockSpec(a.shape, lambda i: (0, 0))
    return pl.pallas_call(
        body, name=name, grid=(t // tm,),
        in_specs=[row(DA), row(3 * DCV),
                  pl.BlockSpec((8, 3 * DCV), lambda i: (jnp.maximum(i * hb - 1, 0), 0)),
                  full(conv_w), full(g_attn), full(g_conv), row(D), full(g1), full(b1), full(w_mo)],
        out_specs=(row(D), row(D), pl.BlockSpec((tm, 1), lambda i: (i, 0))),
        out_shape=(jax.ShapeDtypeStruct((t, D), MXU_DT), jax.ShapeDtypeStruct((t, D), F32),
                   jax.ShapeDtypeStruct((t, 1), F32)),
        scratch_shapes=[pltpu.VMEM((tm + 8, DCV), F32)],
        compiler_params=_params(48),
    )(o, bch, bch, conv_w, g_attn, g_conv, xh1, g1, b1, w_mo)


def _tail(xh3, rs3, g3, b3, p, w_g, w_ple, bg, g4, b4, target, name):
    t = xh3.shape[0]
    tm = _tile(t, 512)

    def body(x_ref, rs_ref, g3_ref, b3_ref, p_ref, wg_ref, wp_ref, bg_ref, g4_ref, b4_ref, t_ref,
             dr_ref, dz_ref, de_ref, st_ref):
        @pl.when(pl.program_id(0) == 0)
        def _():
            st_ref[...] = jnp.zeros_like(st_ref)

        xh3v = x_ref[...]
        x3 = xh3v * g3_ref[...] + b3_ref[...]
        gate = _sigmoid(_mm(x3.astype(MXU_DT), wg_ref[...]) + bg_ref[...])
        e = _mm(p_ref[...].astype(MXU_DT), wp_ref[...])
        xh4, rstd4 = _ln_fwd(ALPHA * x3 + gate * e)
        diff = xh4 * g4_ref[...] + b4_ref[...] - t_ref[...]
        dy = diff * (1.0 / D)
        st_ref[5:6, :] += _colsum(diff * diff)
        st_ref[0:1, :] += _colsum(dy * xh4)
        st_ref[1:2, :] += _colsum(dy)
        dr4 = _ln_bwd(dy * g4_ref[...], xh4, rstd4)
        de_ref[...] = (dr4 * gate).astype(de_ref.dtype)
        dz = dr4 * e * gate * (1.0 - gate)
        st_ref[2:3, :] += _colsum(dz)
        dzb = dz.astype(MXU_DT)
        dz_ref[...] = dzb
        dx3 = ALPHA * dr4 + _mm_nt(dzb, wg_ref[...])
        st_ref[3:4, :] += _colsum(dx3 * xh3v)
        st_ref[4:5, :] += _colsum(dx3)
        dr_ref[...] = _ln_bwd(dx3 * g3_ref[...], xh3v, rs_ref[...])

    row = lambda w: pl.BlockSpec((tm, w), lambda i: (i, 0))
    full = lambda a: pl.BlockSpec(a.shape, lambda i: (0, 0))
    return pl.pallas_call(
        body, name=name, grid=(t // tm,),
        in_specs=[row(D), row(1), full(g3), full(b3), row(PLE), full(w_g), full(w_ple), full(bg), full(g4),
                  full(b4), row(D)],
        out_specs=(row(D), row(D), row(D), pl.BlockSpec((8, D), lambda i: (0, 0))),
        out_shape=(jax.ShapeDtypeStruct((t, D), F32), jax.ShapeDtypeStruct((t, D), MXU_DT),
                   jax.ShapeDtypeStruct((t, D), MXU_DT), jax.ShapeDtypeStruct((8, D), F32)),
        compiler_params=_params(48),
    )(xh3, rs3, g3, b3, p, w_g, w_ple, bg, g4, b4, target)


def _ffn_bwd(dr, gact, uact, xin, rsin, gin, w_in, w_out, prev_ln, name, exchange=()):
    t = dr.shape[0]
    tm = _tile(t, 512)
    nt = t // tm
    chunks = _f_chunks()
    ne = len(exchange)

    def body(*refs):
        dr_ref, g_ref, u_ref, x_ref, rs_ref, gi_ref, win_hbm, wout_hbm = refs[:8]
        df_ref, dg_ref, du_ref, dx_ref, st_ref = refs[8 + ne:13 + ne]
        win_v, wout_v = refs[13 + 2 * ne:15 + 2 * ne]
        acc_ref = dx_ref
        if ne:
            e_start, e_finish = _exchange_phases(refs[8:8 + ne], refs[13 + ne:13 + 2 * ne], *refs[15 + 2 * ne:])

        @pl.when(pl.program_id(0) == 0)
        def _():
            if ne:
                e_start()
            pltpu.sync_copy(win_hbm, win_v)
            pltpu.sync_copy(wout_hbm, wout_v)
            st_ref[...] = jnp.zeros_like(st_ref)

        if ne:
            pl.when(pl.program_id(0) == nt - 1)(e_finish)

        drv = dr_ref[...]
        dfb = (0.5 * drv).astype(MXU_DT)
        df_ref[...] = dfb
        for ci, (c0, fc) in enumerate(chunks):
            dh = _mm_nt(dfb, wout_v[c0:c0 + fc, :])
            g = g_ref[:, c0:c0 + fc].astype(F32)
            u = u_ref[:, c0:c0 + fc].astype(F32)
            sg = _sigmoid(g)
            dgb = (dh * u * (sg * (1.0 + g * (1.0 - sg)))).astype(MXU_DT)
            dub = (dh * (g * sg)).astype(MXU_DT)
            dg_ref[:, c0:c0 + fc] = dgb
            du_ref[:, c0:c0 + fc] = dub
            part = _mm_nt(dgb, win_v[:, c0:c0 + fc]) + _mm_nt(dub, win_v[:, F + c0:F + c0 + fc])
            if ci == 0:
                acc_ref[...] = part
            else:
                acc_ref[...] += part
        dx = ALPHA * drv + acc_ref[...]
        if prev_ln:
            xh = x_ref[...]
            st_ref[0:1, :] += _colsum(dx * xh)
            st_ref[1:2, :] += _colsum(dx)
            dx_ref[...] = _ln_bwd(dx * gi_ref[...], xh, rs_ref[...])
        else:
            dx_ref[...] = dx

    row = pl.BlockSpec((tm, D), lambda i: (i, 0))
    vec = pl.BlockSpec((1, D), lambda i: (0, 0))
    act = pl.BlockSpec((tm, F), lambda i: (i, 0))
    return pl.pallas_call(
        body, name=name, grid=(nt,),
        in_specs=[row, act, act, row, pl.BlockSpec((tm, 1), lambda i: (i, 0)), vec, ANY, ANY] + [ANY] * ne,
        out_specs=(row, act, act, row, pl.BlockSpec((8, D), lambda i: (0, 0))) + (ANY,) * ne,
        out_shape=(jax.ShapeDtypeStruct((t, D), MXU_DT), jax.ShapeDtypeStruct((t, F), MXU_DT),
                   jax.ShapeDtypeStruct((t, F), MXU_DT), jax.ShapeDtypeStruct((t, D), F32),
                   jax.ShapeDtypeStruct((8, D), F32))
        + tuple(jax.ShapeDtypeStruct(a.shape, a.dtype) for a in exchange),
        scratch_shapes=[pltpu.VMEM((D, 2 * F), MXU_DT), pltpu.VMEM((F, D), MXU_DT)]
        + (_exchange_sems(ne) if ne else []),
        compiler_params=_params(60),
    )(dr, gact, uact, xin, rsin, gin, w_in, w_out, *exchange)


def _mix_post_bwd(dr2, o, bch, conv_w, g_attn, g_conv, w_mo, name):
    t = dr2.shape[0]
    tm = _tile(t, 512)
    hb = tm // 8

    def body(dr_ref, o_ref, bch_ref, halo_ref, cw_ref, ga_ref, gc_ref, w_ref,
             dm_ref, do_ref, dl_ref, dy_ref, st_ref, ext):
        i = pl.program_id(0)

        @pl.when(i == 0)
        def _():
            st_ref[...] = jnp.zeros_like(st_ref)

        dmb = dr_ref[...].astype(MXU_DT)
        dm_ref[...] = dmb
        dmg = _mm_nt(dmb, w_ref[...])
        ov = o_ref[...]
        an, ra = _rms_fwd(ov)
        da = dmg[:, 0:DA]
        st_ref[0:1, :] += _colsum(da * an)
        dxa = _rms_bwd(da * ga_ref[...], an, ra)
        dor = dxa.astype(MXU_DT).astype(F32)
        dot = dor.T
        for h in range(NH):
            do_ref[h, 0:DH, :] = dot[DH * h:DH * (h + 1)].astype(do_ref.dtype)
            do_ref[h, DH:LANES, :] = jnp.zeros((LANES - DH, tm), do_ref.dtype)
        srow = lax.broadcasted_iota(jnp.int32, (8, DA), 0)
        scol = lax.broadcasted_iota(jnp.int32, (8, DA), 1)
        sel = jnp.where((scol // DH) == srow, 1.0, 0.0).astype(MXU_DT)
        hi, mid, lo = _split3(dor * ov)
        delta = _mm_nt(sel, hi) + _mm_nt(sel, mid) + _mm_nt(sel, lo)
        for h in range(NH):
            dl_ref[h] = delta[h:h + 1, :]
        bb, cc, hh = _conv_parts(bch_ref[...])
        _, hc, hh_h = _conv_parts(halo_ref[...])
        u = cc * hh
        ext[0:8, :] = jnp.where(i > 0, hc * hh_h, 0.0)
        ext[8:8 + tm, :] = u
        raw = cw_ref[0:1, :] * ext[6:6 + tm, :] + cw_ref[1:2, :] * ext[7:7 + tm, :] + cw_ref[2:3, :] * u
        cn, rc = _rms_fwd(bb * raw)
        dcn = dmg[:, DA:D]
        st_ref[1:2, :] += _colsum(dcn * cn)
        dy_ref[...] = _rms_bwd(dcn * gc_ref[...], cn, rc)

    row = lambda w: pl.BlockSpec((tm, w), lambda i: (i, 0))
    full = lambda a: pl.BlockSpec(a.shape, lambda i: (0, 0))
    return pl.pallas_call(
        body, name=name, grid=(t // tm,),
        in_specs=[row(D), row(DA), row(3 * DCV),
                  pl.BlockSpec((8, 3 * DCV), lambda i: (jnp.maximum(i * hb - 1, 0), 0)),
                  full(conv_w), full(g_attn), full(g_conv), full(w_mo)],
        out_specs=(row(D), pl.BlockSpec((NH, LANES, tm), lambda i: (0, 0, i)),
                   pl.BlockSpec((NH, 1, tm), lambda i: (0, 0, i)), row(DCV),
                   pl.BlockSpec((8, DA), lambda i: (0, 0))),
        out_shape=(jax.ShapeDtypeStruct((t, D), MXU_DT), jax.ShapeDtypeStruct((NH, LANES, t), MXU_DT),
                   jax.ShapeDtypeStruct((NH, 1, t), F32), jax.ShapeDtypeStruct((t, DCV), F32),
                   jax.ShapeDtypeStruct((8, DA), F32)),
        scratch_shapes=[pltpu.VMEM((tm + 8, DCV), F32)],
        compiler_params=_params(48),
    )(dr2, o, bch, bch, conv_w, g_attn, g_conv, w_mo)


def _conv_bwd(dy, bch, conv_w, name):
    t = dy.shape[0]
    tm = _tile(t, 512)
    hb = tm // 8
    nt = t // tm

    def body(dy_ref, dyn_ref, bch_ref, prev_ref, next_ref, cw_ref, out_ref, st_ref, ext_u, ext_d):
        i = pl.program_id(0)

        @pl.when(i == 0)
        def _():
            st_ref[...] = jnp.zeros_like(st_ref)

        bb, cc, hh = _conv_parts(bch_ref[...])
        _, pc, ph = _conv_parts(prev_ref[...])
        nb, _, _ = _conv_parts(next_ref[...])
        u = cc * hh
        ext_u[0:8, :] = jnp.where(i > 0, pc * ph, 0.0)
        ext_u[8:8 + tm, :] = u
        u1 = ext_u[7:7 + tm, :]
        u2 = ext_u[6:6 + tm, :]
        w0, w1, w2 = cw_ref[0:1, :], cw_ref[1:2, :], cw_ref[2:3, :]
        dyv = dy_ref[...]
        out_ref[:, 0:DCV] = (dyv * (w0 * u2 + w1 * u1 + w2 * u)).astype(out_ref.dtype)
        dcr = dyv * bb
        ext_d[0:tm, :] = dcr
        ext_d[tm:tm + 8, :] = jnp.where(i < nt - 1, dyn_ref[...] * nb, 0.0)
        du = w2 * dcr + w1 * ext_d[1:1 + tm, :] + w0 * ext_d[2:2 + tm, :]
        out_ref[:, DCV:2 * DCV] = (du * hh).astype(out_ref.dtype)
        out_ref[:, 2 * DCV:3 * DCV] = (du * cc).astype(out_ref.dtype)
        st_ref[0:1, :] += _colsum(dcr * u2)
        st_ref[1:2, :] += _colsum(dcr * u1)
        st_ref[2:3, :] += _colsum(dcr * u)

    row = lambda w: pl.BlockSpec((tm, w), lambda i: (i, 0))
    prev = lambda w: pl.BlockSpec((8, w), lambda i: (jnp.maximum(i * hb - 1, 0), 0))
    nxt = lambda w: pl.BlockSpec((8, w), lambda i: (jnp.minimum((i + 1) * hb, nt * hb - 1), 0))
    return pl.pallas_call(
        body, name=name, grid=(nt,),
        in_specs=[row(DCV), nxt(DCV), row(3 * DCV), prev(3 * DCV), nxt(3 * DCV),
                  pl.BlockSpec(conv_w.shape, lambda i: (0, 0))],
        out_specs=(row(3 * DCV), pl.BlockSpec((8, DCV), lambda i: (0, 0))),
        out_shape=(jax.ShapeDtypeStruct((t, 3 * DCV), MXU_DT), jax.ShapeDtypeStruct((8, DCV), F32)),
        scratch_shapes=[pltpu.VMEM((tm + 8, DCV), F32), pltpu.VMEM((tm + 8, DCV), F32)],
        compiler_params=_params(48),
    )(dy, dy, bch, bch, bch, conv_w)


def _attn_bwd(ka, kat, va, qat, dot, lrow, drow, r, name):
    t = ka.shape[1]
    tq = _tile(t, 512)
    nq = t // tq

    def body(r_ref, ka_ref, kat_ref, va_ref, l_ref, dl_ref, qat_hbm, dot_hbm,
             dk_ref, dv_ref, dck_ref, dqt_hbm, dcq_hbm, qat_v, dot_v, dq_acc):
        hp, j = pl.program_id(0), pl.program_id(1)

        @pl.when(j == 0)
        def _():
            pltpu.sync_copy(qat_hbm.at[pl.ds(2 * hp, 2)], qat_v)
            pltpu.sync_copy(dot_hbm.at[pl.ds(2 * hp, 2)], dot_v)
            dq_acc[...] = jnp.zeros_like(dq_acc)

        key = lax.broadcasted_iota(jnp.int32, (tq, tq), 0)
        qry = lax.broadcasted_iota(jnp.int32, (tq, tq), 1)

        def step(i, carry, masked):
            off = pl.multiple_of(i * tq, tq)
            out = []
            for a in range(2):
                dk, dv = carry[a]
                st = _mm(ka_ref[a], qat_v[a, :, pl.ds(off, tq)])
                dpt = _mm(va_ref[a], dot_v[a, :, pl.ds(off, tq)])
                if masked:
                    st = jnp.where(qry >= key, st, NEG)
                d = r_ref[2 * hp + a, i] - r_ref[2 * hp + a, j]
                pt = jnp.exp(st - (l_ref[a, :, pl.ds(off, tq)] - d))
                dsb = (pt * (dpt - dl_ref[a, :, pl.ds(off, tq)])).astype(MXU_DT)
                dv = dv + _mm_nt(dot_v[a, 0:DH, pl.ds(off, tq)], pt.astype(MXU_DT))
                dk = dk + _mm_nt(qat_v[a, 0:QROWS, pl.ds(off, tq)], dsb)
                dq_acc[a, :, pl.ds(off, tq)] += _mm(kat_ref[a], dsb)
                out.append((dk, dv))
            return tuple(out)

        init = tuple((jnp.zeros((QROWS, tq), F32), jnp.zeros((DH, tq), F32)) for _ in range(2))
        carry = step(j, init, True)
        (dka, dva), (dkb, dvb) = lax.fori_loop(j + 1, nq, lambda i, cr: step(i, cr, False), carry)
        dk_ref[...] = jnp.concatenate([dka[0:DH], dkb[0:DH]], axis=0).T.astype(dk_ref.dtype)
        dv_ref[...] = jnp.concatenate([dva, dvb], axis=0).T.astype(dv_ref.dtype)
        dck_ref[0] = -dka[DH + 3:DH + 4, :]
        dck_ref[1] = -dkb[DH + 3:DH + 4, :]

        @pl.when(j == nq - 1)
        def _():
            pltpu.sync_copy(dq_acc, dqt_hbm.at[pl.ds(2 * hp, 2)])
            pltpu.sync_copy(dq_acc.at[:, DH:DH + 1, :], dcq_hbm.at[pl.ds(2 * hp, 2)])

    pair = lambda rows, cols: pl.BlockSpec((2, rows, cols), lambda p, j: (p, 0, 0))
    return pl.pallas_call(
        body, name=name, grid=(NH // 2, nq),
        in_specs=[pl.BlockSpec(memory_space=pltpu.SMEM),
                  pl.BlockSpec((2, tq, LANES), lambda p, j: (p, j, 0)),
                  pl.BlockSpec((2, QROWS, tq), lambda p, j: (p, 0, j)),
                  pl.BlockSpec((2, tq, LANES), lambda p, j: (p, j, 0)),
                  pair(1, t), pair(1, t), ANY, ANY],
        out_specs=(pl.BlockSpec((tq, LANES), lambda p, j: (j, p)),
                   pl.BlockSpec((tq, LANES), lambda p, j: (j, p)),
                   pl.BlockSpec((2, 1, tq), lambda p, j: (p, 0, j)), ANY, ANY),
        out_shape=(jax.ShapeDtypeStruct((t, DA), MXU_DT), jax.ShapeDtypeStruct((t, DA), MXU_DT),
                   jax.ShapeDtypeStruct((NH, 1, t), F32), jax.ShapeDtypeStruct((NH, QROWS, t), F32),
                   jax.ShapeDtypeStruct((NH, 1, t), F32)),
        scratch_shapes=[pltpu.VMEM((2, LANES, t), MXU_DT), pltpu.VMEM((2, LANES, t), MXU_DT),
                        pltpu.VMEM((2, QROWS, t), F32)],
        compiler_params=_params(52, 2),
    )(r, ka, kat, va, lrow, drow, qat, dot)


def _mix_proj_bwd(dr2, dqt, dk, dv, dbch, dc, z, xh1, rs1, g1, w_qkv, w_bch, w_f, name):
    t = dr2.shape[0]
    tm = _tile(t, 512)
    nt = t // tm

    def body(dr_ref, dqt_ref, dk_ref, dv_ref, db_ref, dc_ref, z_ref, x_ref, rs_ref, g_ref,
             wq_ref, wb_ref, wf_ref, out_ref, df_ref, dq_ref, st_ref, carry):
        @pl.when(pl.program_id(0) == 0)
        def _():
            carry[...] = jnp.zeros_like(carry)
            st_ref[...] = jnp.zeros_like(st_ref)

        dq_ref[...] = (jnp.concatenate([dqt_ref[h, 0:DH, :] for h in range(NH)], axis=0).T * 0.125
                       ).astype(dq_ref.dtype)

        row = lax.broadcasted_iota(jnp.int32, (tm, tm), 0)
        col = lax.broadcasted_iota(jnp.int32, (tm, tm), 1)
        triu = jnp.where(col >= row, 1.0, 0.0).astype(MXU_DT)
        dlogf = carry[...] + _mm_sel(triu, dc_ref[...])
        carry[...] = dlogf[0:1, :]
        dz = dlogf / (1.0 + jnp.exp(z_ref[...]))
        st_ref[2:3, 0:LANES] += _colsum(dz)
        dfb = dz.astype(MXU_DT)
        df_ref[...] = dfb
        dx = (ALPHA * dr_ref[...]
              + _mm_nt(dq_ref[...], wq_ref[:, 0:DA])
              + _mm_nt(dk_ref[...], wq_ref[:, DA:2 * DA])
              + _mm_nt(dv_ref[...], wq_ref[:, 2 * DA:3 * DA])
              + _mm_nt(db_ref[...], wb_ref[...])
              + _mm_nt(dfb, wf_ref[...]))
        xh = x_ref[...]
        st_ref[0:1, :] += _colsum(dx * xh)
        st_ref[1:2, :] += _colsum(dx)
        out_ref[...] = _ln_bwd(dx * g_ref[...], xh, rs_ref[...])

    row = lambda w: pl.BlockSpec((tm, w), lambda i: (nt - 1 - i, 0))
    full = lambda a: pl.BlockSpec(a.shape, lambda i: (0, 0))
    return pl.pallas_call(
        body, name=name, grid=(nt,),
        in_specs=[row(D), pl.BlockSpec((NH, QROWS, tm), lambda i: (0, 0, nt - 1 - i)), row(DA), row(DA),
                  row(3 * DCV), row(LANES), row(LANES), row(D), row(1),
                  full(g1), full(w_qkv), full(w_bch), full(w_f)],
        out_specs=(row(D), row(LANES), row(DA), pl.BlockSpec((8, D), lambda i: (0, 0))),
        out_shape=(jax.ShapeDtypeStruct((t, D), F32), jax.ShapeDtypeStruct((t, LANES), MXU_DT),
                   jax.ShapeDtypeStruct((t, DA), MXU_DT), jax.ShapeDtypeStruct((8, D), F32)),
        scratch_shapes=[pltpu.VMEM((1, LANES), F32)],
        compiler_params=_params(48),
    )(dr2, dqt, dk, dv, dbch, dc, z, xh1, rs1, g1, w_qkv, w_bch, w_f)


def _dw(mode, a_parts, b, m, n, name, tmm=None, tn=None, exchange=()):
    t = b.shape[0]
    tmm = tmm or m
    tn = tn or n
    tt = _tile(t, 2048)
    na, ne = len(a_parts), len(exchange)
    grid = (m // tmm, n // tn, t // tt)

    def body(*refs):
        a_refs, b_ref, o_ref = refs[:na], refs[na], refs[na + 1 + ne]
        if ne:
            e_start, e_finish = _exchange_phases(refs[na + 1:na + 1 + ne], refs[na + 2 + ne:na + 2 + 2 * ne],
                                                 *refs[na + 2 + 2 * ne:])
            at = lambda steps: functools.reduce(jnp.logical_and, [pl.program_id(d) == s for d, s in enumerate(steps)])
            pl.when(at((0, 0, 0)))(e_start)

        @pl.when(pl.program_id(2) == 0)
        def _():
            o_ref[...] = jnp.zeros_like(o_ref)

        if mode == "plain":
            a = a_refs[0][...].astype(MXU_DT)
        elif mode == "affine":
            a = (a_refs[0][...] * a_refs[1][...] + a_refs[2][...]).astype(MXU_DT)
        else:
            g = a_refs[0][...].astype(F32)
            a = (g * _sigmoid(g) * a_refs[1][...].astype(F32)).astype(MXU_DT)
        o_ref[...] += _mm_tn(a, b_ref[...].astype(MXU_DT))
        if ne:
            pl.when(at(tuple(g - 1 for g in grid)))(e_finish)

    a_tile = pl.BlockSpec((tt, tmm), lambda i, j, k: (k, i))
    a_vec = pl.BlockSpec((1, tmm), lambda i, j, k: (0, i))
    a_specs = {"plain": [a_tile], "affine": [a_tile, a_vec, a_vec], "swiglu": [a_tile, a_tile]}[mode]
    res = pl.pallas_call(
        body, name=name, grid=grid,
        in_specs=a_specs + [pl.BlockSpec((tt, tn), lambda i, j, k: (k, j))] + [ANY] * ne,
        out_specs=(pl.BlockSpec((tmm, tn), lambda i, j, k: (i, j)),) + (ANY,) * ne,
        out_shape=(jax.ShapeDtypeStruct((m, n), F32),)
        + tuple(jax.ShapeDtypeStruct(a.shape, a.dtype) for a in exchange),
        scratch_shapes=_exchange_sems(ne) if ne else [],
        compiler_params=_params(52, 3),
    )(*a_parts, b, *exchange)
    return res if ne else res[0]


def _adamw(w, g, m, v):
    m = ADAM_B1 * m + (1.0 - ADAM_B1) * g
    v = ADAM_B2 * v + (1.0 - ADAM_B2) * (g * g)
    m_hat = m / (1.0 - ADAM_B1 ** ADAM_STEP)
    v_hat = v / (1.0 - ADAM_B2 ** ADAM_STEP)
    delta = -ADAM_LR * (m_hat / (jnp.sqrt(v_hat) + ADAM_EPS) + ADAM_WD * w)
    return delta, m, v


def _reduce_adamw(landed, own, w, m, v, name):
    r, c = own.shape
    tr = _tile(r, 128)

    def body(l_ref, o_ref, w_ref, m_ref, v_ref, g_out, d_out, m_out, v_out):
        me = 4 * lax.axis_index("x") + 2 * lax.axis_index("y") + lax.axis_index("c")
        g = None
        for j in range(NDEV):
            term = jnp.where(me == j, o_ref[...], l_ref[j].astype(F32))
            g = term if g is None else g + term
        g_out[...] = g
        d_out[...], m_out[...], v_out[...] = _adamw(w_ref[...], g, m_ref[...], v_ref[...])

    blk = pl.BlockSpec((tr, c), lambda i: (i, 0))
    sds = jax.ShapeDtypeStruct((r, c), F32)
    return pl.pallas_call(
        body, name=name, grid=(r // tr,),
        in_specs=[pl.BlockSpec((NDEV, tr, c), lambda i: (0, i, 0)), blk, blk, blk, blk],
        out_specs=(blk, blk, blk, blk), out_shape=(sds, sds, sds, sds),
        compiler_params=_params(40),
    )(landed, own, w, m, v)


def _sum_small(gathered, name):
    _, r, c = gathered.shape

    def body(g_ref, o_ref):
        acc = g_ref[0]
        for j in range(1, NDEV):
            acc = acc + g_ref[j]
        o_ref[...] = acc

    return pl.pallas_call(body, name=name, out_shape=jax.ShapeDtypeStruct((r, c), F32))(gathered)


def _adamw_small(g, w, m, v, name):
    def body(g_ref, w_ref, m_ref, v_ref, d_out, m_out, v_out):
        d_out[...], m_out[...], v_out[...] = _adamw(w_ref[...], g_ref[...], m_ref[...], v_ref[...])

    sds = jax.ShapeDtypeStruct(g.shape, F32)
    return pl.pallas_call(body, name=name, out_shape=(sds, sds, sds))(g, w, m, v)


def _cols_from_stack(s):
    return jnp.transpose(s, (1, 0, 2)).reshape(s.shape[1], NDEV * s.shape[2])


def _cols_to_stack(w):
    r, c = w.shape
    return jnp.transpose(w.reshape(r, NDEV, c // NDEV), (1, 0, 2))


def _rows_from_stack(s):
    return s.reshape(NDEV * s.shape[1], s.shape[2])


def _rows_to_stack(w):
    r, c = w.shape
    return w.reshape(NDEV, r // NDEV, c)


SMALL_ROWS = 16
SMALL_SLOTS = {
    "ln1_g": (0, 0, D), "ln1_b": (1, 0, D), "ln2_g": (2, 0, D), "ln2_b": (3, 0, D), "ln3_g": (4, 0, D),
    "ln3_b": (5, 0, D), "b_ple_gate": (6, 0, D), "ln4_g": (7, 0, D), "ln4_b": (8, 0, D),
    "g_attn": (9, 0, DA), "g_conv": (9, DA, DCV), "b_forget": (10, 0, NH),
}
CONVW_ROW = 11
LOSS_SLOT = (10, LANES)


def _pack_small(vals, conv_rows, loss=None):
    out = jnp.zeros((SMALL_ROWS, D), F32)
    for nm, (r, off, wd) in SMALL_SLOTS.items():
        out = out.at[r:r + 1, off:off + wd].set(vals[nm].reshape(1, wd).astype(F32))
    out = out.at[CONVW_ROW:CONVW_ROW + 3, 0:conv_rows.shape[1]].set(conv_rows.astype(F32))
    if loss is not None:
        out = out.at[LOSS_SLOT[0], LOSS_SLOT[1]].set(loss)
    return out


def _unpack_small(packed, name):
    r, off, wd = SMALL_SLOTS[name]
    return packed[r:r + 1, off:off + wd]


def kernel(x, p, ffn1_w_in, ffn1_w_out, ln1_g, ln1_b, w_mix_in, b_forget, conv_w, g_attn, g_conv, w_mix_out, ln2_g, ln2_b, ffn2_w_in, ffn2_w_out, ln3_g, ln3_b, w_ple, w_ple_gate, b_ple_gate, ln4_g, ln4_b, loss_target, m_ffn1_w_in, m_ffn1_w_out, m_ln1_g, m_ln1_b, m_w_mix_in, m_b_forget, m_conv_w, m_g_attn, m_g_conv, m_w_mix_out, m_ln2_g, m_ln2_b, m_ffn2_w_in, m_ffn2_w_out, m_ln3_g, m_ln3_b, m_w_ple, m_w_ple_gate, m_b_ple_gate, m_ln4_g, m_ln4_b, v_ffn1_w_in, v_ffn1_w_out, v_ln1_g, v_ln1_b, v_w_mix_in, v_b_forget, v_conv_w, v_g_attn, v_g_conv, v_w_mix_out, v_ln2_g, v_ln2_b, v_ffn2_w_in, v_ffn2_w_out, v_ln3_g, v_ln3_b, v_w_ple, v_w_ple_gate, v_b_ple_gate, v_ln4_g, v_ln4_b):
    args = dict(locals())
    t = x.shape[1]
    me = 4 * lax.axis_index("x") + 2 * lax.axis_index("y") + lax.axis_index("c")
    x0 = x.reshape(t, D)
    p0 = p.reshape(t, PLE)
    tgt = loss_target.reshape(t, D)

    big = ["ffn1_w_in", "ffn1_w_out", "w_mix_in", "w_mix_out", "ffn2_w_in", "ffn2_w_out", "w_ple", "w_ple_gate"]
    col_sharded = {"ffn1_w_in", "w_mix_in", "ffn2_w_in", "w_ple"}
    shard = {nm: args[nm][0] for nm in big}

    unstack = lambda nm, g: (_cols_from_stack(g) if nm in col_sharded else _rows_from_stack(g)).astype(MXU_DT)
    stack = lambda nm, g: _cols_to_stack(g) if nm in col_sharded else _rows_to_stack(g)
    wire = lambda names: [shard[nm].astype(WIRE_DT) for nm in names]
    first, later = big[:2], big[2:]

    full = {nm: unstack(nm, g) for nm, g in zip(first, _allgather(wire(first), "ag_ffn1"))}
    ffn1_out, gathered = _ffn1_fwd(x0, full, wire(later) + [conv_w[0]])
    full.update({nm: unstack(nm, g) for nm, g in zip(later, gathered)})
    cw = _cols_from_stack(gathered[len(later)])

    dr1, gw, small, loss_part = _mid_step(p0, tgt, full, cw, {nm: args[nm] for nm in SMALL_SLOTS}, ffn1_out)
    small_part = _pack_small({nm: small[nm] for nm in SMALL_SLOTS},
                             jnp.pad(small["conv_w"], ((0, 0), (0, D - DCV))), loss_part)
    stacks = {nm: stack(nm, gw[nm]) for nm in later}
    beside_bwd, beside_dw = later[1:], later[:1]
    gx, gw1, landed_bwd, landed_dw, landed_w_in = _ffn1_bwd(
        x0, dr1, ffn1_out, full, [stacks[nm].astype(WIRE_DT) for nm in beside_bwd],
        [stacks[nm].astype(WIRE_DT) for nm in beside_dw], lambda g: stack("ffn1_w_in", g).astype(WIRE_DT))
    stacks.update({nm: stack(nm, gw1[nm]) for nm in first})
    (landed_w_out,), (small_all,) = _exchange_and_gather([stacks["ffn1_w_out"].astype(WIRE_DT)], [small_part],
                                                         "rs_ffn1_out")
    landed = dict(zip(beside_bwd + beside_dw, list(landed_bwd) + list(landed_dw)),
                  ffn1_w_in=landed_w_in, ffn1_w_out=landed_w_out)
    small_g = _sum_small(small_all, "sum_small")
    loss = small_g[LOSS_SLOT[0], LOSS_SLOT[1]]

    outs = {"loss": loss, "grad_x": gx.reshape(1, t, D)}
    for nm in big:
        own = lax.dynamic_index_in_dim(stacks[nm], me, axis=0, keepdims=False)
        g, dl, mn, vn = _reduce_adamw(landed[nm], own, shard[nm], args["m_" + nm][0], args["v_" + nm][0],
                                      "adamw_" + nm)
        outs["grad_" + nm], outs["delta_" + nm], outs["new_m_" + nm], outs["new_v_" + nm] = (
            g[None], dl[None], mn[None], vn[None])
    small_names = list(SMALL_SLOTS)
    cshard = lax.dynamic_slice_in_dim(small_g[CONVW_ROW:CONVW_ROW + 3, 0:DCV], me * (DCV // NDEV), DCV // NDEV, axis=1)
    g_pack = _pack_small({nm: _unpack_small(small_g, nm) for nm in small_names}, cshard)
    packs = [_pack_small({nm: args[pre + nm] for nm in small_names}, args[pre + "conv_w"][0])
             for pre in ("", "m_", "v_")]
    d_pack, m_pack, v_pack = _adamw_small(g_pack, packs[0], packs[1], packs[2], "adamw_small")
    for key, pk in (("grad_", g_pack), ("delta_", d_pack), ("new_m_", m_pack), ("new_v_", v_pack)):
        for nm in small_names:
            outs[key + nm] = _unpack_small(pk, nm)
        outs[key + "conv_w"] = pk[CONVW_ROW:CONVW_ROW + 3, 0:DCV // NDEV][None]

    wnames = ["ffn1_w_in", "ffn1_w_out", "ln1_g", "ln1_b", "w_mix_in", "b_forget", "conv_w", "g_attn", "g_conv",
              "w_mix_out", "ln2_g", "ln2_b", "ffn2_w_in", "ffn2_w_out", "ln3_g", "ln3_b", "w_ple", "w_ple_gate",
              "b_ple_gate", "ln4_g", "ln4_b"]
    return (outs["loss"], outs["grad_x"], *[outs[pre + nm] for pre in ("grad_", "delta_", "new_m_", "new_v_")
                                            for nm in wnames])


def _ffn1_fwd(x0, full, gather=()):
    res = _ffn_fwd(x0, jnp.ones((1, D), F32), jnp.zeros((1, D), F32), full["ffn1_w_in"], full["ffn1_w_out"],
                   "ffn1_fwd", gather)
    return res[:4], res[4:]


def _ffn1_bwd(x0, dr1, ffn1_out, full, exchange=(), exchange_late=(), w_in_slots=None):
    g1a, u1a, _, rs1 = ffn1_out
    ones, zeros = jnp.ones((1, D), F32), jnp.zeros((1, D), F32)
    res = _ffn_bwd(dr1, g1a, u1a, x0, rs1, ones, full["ffn1_w_in"], full["ffn1_w_out"], False, "ffn1_bwd",
                   exchange)
    df1, dg1, du1, gx = res[:4]
    dw_g = _dw("affine", (x0, ones, zeros), dg1, D, F, "dw_ffn1_in_g", tn=F // 2, exchange=exchange_late)
    dw_g, landed_late = (dw_g[0], dw_g[1:]) if exchange_late else (dw_g, ())
    gw_in = jnp.concatenate([dw_g, _dw("affine", (x0, ones, zeros), du1, D, F, "dw_ffn1_in_u", tn=F // 2)], axis=1)
    side = () if w_in_slots is None else (w_in_slots(gw_in),)
    out = _dw("swiglu", (g1a, u1a), df1, F, D, "dw_ffn1_out", tmm=F // 2, exchange=side)
    gw_out, landed_in = (out, None) if w_in_slots is None else (out[0], out[1])
    return gx, {"ffn1_w_in": gw_in, "ffn1_w_out": gw_out}, res[5:], landed_late, landed_in


def _mid_step(p0, tgt, full, cw, sp, ffn1_out):
    g1a, u1a, xh1, rs1 = ffn1_out
    t = xh1.shape[0]
    ln1_g, ln1_b, ln2_g, ln2_b, ln3_g, ln3_b = (sp[k] for k in ("ln1_g", "ln1_b", "ln2_g", "ln2_b", "ln3_g", "ln3_b"))
    ln4_g, ln4_b, g_attn, g_conv, b_ple_gate = (sp[k] for k in ("ln4_g", "ln4_b", "g_attn", "g_conv", "b_ple_gate"))
    wmi = full["w_mix_in"]
    w_qkv = wmi[:, 0:3 * DA]
    w_f = jnp.pad(wmi[:, 3 * DA:3 * DA + NH], ((0, 0), (0, LANES - NH)))
    w_bch = wmi[:, 3 * DA + NH:]
    bf_pad = jnp.pad(sp["b_forget"], ((0, 0), (0, LANES - NH)))

    ka, va, qat, kat, vta, bch, z, rt = _mix_proj_fwd(xh1, ln1_g, ln1_b, w_qkv[:, DA:], jnp.transpose(w_qkv),
                                                      w_bch, w_f, bf_pad, "mix_proj_fwd")
    rtile = jnp.transpose(rt[:, 0, 0:NH])
    o, lse = _attn_fwd(qat, ka, vta, rtile, "attn_fwd")
    merged, xh2, rs2 = _mix_post_fwd(o, bch, cw, g_attn, g_conv, xh1, ln1_g, ln1_b, full["w_mix_out"],
                                     "mix_post_fwd")
    g2a, u2a, xh3, rs3 = _ffn_fwd(xh2, ln2_g, ln2_b, full["ffn2_w_in"], full["ffn2_w_out"], "ffn2_fwd")

    dr3, dz, de, st_tail = _tail(xh3, rs3, ln3_g, ln3_b, p0, full["w_ple_gate"], full["w_ple"], b_ple_gate,
                                 ln4_g, ln4_b, tgt, "tail")
    df2, dg2, du2, dr2, st_f2 = _ffn_bwd(dr3, g2a, u2a, xh2, rs2, ln2_g, full["ffn2_w_in"], full["ffn2_w_out"],
                                         True, "ffn2_bwd")
    dmix, dot, drow, dyc, st_post = _mix_post_bwd(dr2, o, bch, cw, g_attn, g_conv, full["w_mix_out"],
                                                  "mix_post_bwd")
    dbch, st_conv = _conv_bwd(dyc, bch, cw, "conv_bwd")
    dk, dv, dck, dqt, dcq = _attn_bwd(ka, kat, va, qat, dot, lse, drow, rtile, "attn_bwd")
    dc_pad = jnp.pad(jnp.transpose((dcq + dck).reshape(NH, t)), ((0, 0), (0, LANES - NH)))
    dr1, dfl, dq, st_proj = _mix_proj_bwd(dr2, dqt, dk, dv, dbch, dc_pad, z, xh1, rs1, ln1_g, w_qkv, w_bch, w_f,
                                          "mix_proj_bwd")

    x1p, x2p, x3p = (xh1, ln1_g, ln1_b), (xh2, ln2_g, ln2_b), (xh3, ln3_g, ln3_b)
    gw = {}
    gw["ffn2_w_in"] = jnp.concatenate(
        [_dw("affine", x2p, dg2, D, F, "dw_ffn2_in_g", tn=F // 2),
         _dw("affine", x2p, du2, D, F, "dw_ffn2_in_u", tn=F // 2)], axis=1)
    gw["ffn2_w_out"] = _dw("swiglu", (g2a, u2a), df2, F, D, "dw_ffn2_out", tmm=F // 2)
    gw["w_mix_out"] = _dw("plain", (merged,), dmix, D, D, "dw_mix_out")
    gw["w_mix_in"] = jnp.concatenate(
        [_dw("affine", x1p, dq, D, DA, "dw_mix_in_q"), _dw("affine", x1p, dk, D, DA, "dw_mix_in_k"),
         _dw("affine", x1p, dv, D, DA, "dw_mix_in_v"),
         _dw("affine", x1p, dfl, D, LANES, "dw_mix_in_f")[:, 0:NH],
         _dw("affine", x1p, dbch, D, 3 * DCV, "dw_mix_in_bch")], axis=1)
    gw["w_ple_gate"] = _dw("affine", x3p, dz, D, D, "dw_ple_gate")
    gw["w_ple"] = _dw("plain", (p0,), de, PLE, D, "dw_ple")

    loss_part = (0.5 / D) * jnp.sum(st_tail[5:6, :])
    small = {"ln1_g": st_proj[0:1], "ln1_b": st_proj[1:2], "ln2_g": st_f2[0:1], "ln2_b": st_f2[1:2],
             "ln3_g": st_tail[3:4], "ln3_b": st_tail[4:5], "b_ple_gate": st_tail[2:3], "ln4_g": st_tail[0:1],
             "ln4_b": st_tail[1:2], "g_attn": st_post[0:1], "g_conv": st_post[1:2],
             "b_forget": st_proj[2:3, 0:NH], "conv_w": st_conv[0:3]}
    return dr1, gw, small, loss_part
```

```python
import functools

import jax
import jax.numpy as jnp
from jax import lax
from jax.experimental import pallas as pl
from jax.experimental.pallas import tpu as pltpu

D = 1024
F = 2816
NH = 8
DH = 64
DA = NH * DH
DCV = D - DA
PLE = 256
LN_EPS = 1e-5
RMS_EPS = 1e-6
NEG = -1e30
ALPHA = 2.0 ** 0.25
NDEV = 8
LANES = 128

ADAM_LR, ADAM_B1, ADAM_B2, ADAM_EPS, ADAM_WD, ADAM_STEP = 0.001, 0.9, 0.999, 1e-08, 0.01, 10

F32 = jnp.float32
MXU_DT = jnp.bfloat16
WIRE_DT = jnp.bfloat16

MESH_ID = pl.DeviceIdType.MESH
ANY = pl.BlockSpec(memory_space=pl.ANY)


def _params(vmem_mb, n_axes=1):
    return pltpu.CompilerParams(dimension_semantics=("arbitrary",) * n_axes,
                                vmem_limit_bytes=int(vmem_mb) << 20)


def _mm(a, b):
    return jnp.dot(a, b, preferred_element_type=F32)


def _mm_nt(a, b):
    return lax.dot_general(a, b, (((1,), (1,)), ((), ())), preferred_element_type=F32)


def _mm_tn(a, b):
    return lax.dot_general(a, b, (((0,), (0,)), ((), ())), preferred_element_type=F32)


def _split3(x):
    hi = x.astype(MXU_DT)
    r1 = x - hi.astype(F32)
    mid = r1.astype(MXU_DT)
    lo = (r1 - mid.astype(F32)).astype(MXU_DT)
    return hi, mid, lo


def _mm_sel(sel, x):
    hi, mid, lo = _split3(x)
    return _mm(sel, hi) + _mm(sel, mid) + _mm(sel, lo)


def _sigmoid(x):
    return 1.0 / (1.0 + jnp.exp(-x))


def _ln_fwd(r):
    mu = jnp.mean(r, axis=-1, keepdims=True)
    xc = r - mu
    var = jnp.mean(xc * xc, axis=-1, keepdims=True)
    rstd = lax.rsqrt(var + LN_EPS)
    return xc * rstd, rstd


def _ln_bwd(dxhat, xhat, rstd):
    m1 = jnp.mean(dxhat, axis=-1, keepdims=True)
    m2 = jnp.mean(dxhat * xhat, axis=-1, keepdims=True)
    return rstd * (dxhat - m1 - xhat * m2)


def _rms_fwd(x):
    r = lax.rsqrt(jnp.mean(x * x, axis=-1, keepdims=True) + RMS_EPS)
    return x * r, r


def _rms_bwd(dyg, xn, r):
    return r * (dyg - xn * jnp.mean(dyg * xn, axis=-1, keepdims=True))


def _colsum(x):
    return jnp.sum(x, axis=0, keepdims=True)


def _f_chunks():
    out, c0 = [], 0
    while c0 < F:
        fc = min(512, F - c0)
        out.append((c0, fc))
        c0 += fc
    return out


def _tile(t, want):
    return want if t % want == 0 and t >= want else t


def _exchange_sems(n):
    return [pltpu.SemaphoreType.DMA((n * (NDEV - 1),)), pltpu.SemaphoreType.DMA((n * (NDEV - 1),))]


def _exchange_phases(ins, outs, send_sems, recv_sems):
    n = len(ins)

    def peers():
        x, y, c = lax.axis_index("x"), lax.axis_index("y"), lax.axis_index("c")
        out = []
        for k in range(1, NDEV):
            px = 1 - x if (k >> 2) & 1 else x
            py = 1 - y if (k >> 1) & 1 else y
            pc = 1 - c if k & 1 else c
            out.append(((px, py, pc), 4 * px + 2 * py + pc))
        return 4 * x + 2 * y + c, out

    def remote(w, k, to, slot_src, slot_dst):
        return pltpu.make_async_remote_copy(
            src_ref=ins[w].at[slot_src], dst_ref=outs[w].at[slot_dst],
            send_sem=send_sems.at[w * (NDEV - 1) + k], recv_sem=recv_sems.at[w * (NDEV - 1) + k],
            device_id=to, device_id_type=MESH_ID)

    def start():
        me, prs = peers()
        for k, (to, pid) in enumerate(prs):
            for w in range(n):
                remote(w, k, to, pid, me).start()

    def finish():
        me, prs = peers()
        for k, (to, pid) in enumerate(prs):
            for w in range(n):
                remote(w, k, to, me, pid).wait_recv()
        for k, (to, pid) in enumerate(prs):
            for w in range(n):
                remote(w, k, to, pid, me).wait_send()

    return start, finish


def _gather_sems(n):
    return [pltpu.SemaphoreType.DMA((n * (NDEV - 1),)), pltpu.SemaphoreType.DMA((n * (NDEV - 1),)),
            pltpu.SemaphoreType.DMA((n,))]


def _gather_phases(ins, outs, send_sems, recv_sems, loc_sems):
    n = len(ins)
    per = NDEV - 1

    def place():
        x, y, c = lax.axis_index("x"), lax.axis_index("y"), lax.axis_index("c")
        return (x, y, c), (x, y, 1 - c), [(1 - x, y), (x, 1 - y), (1 - x, 1 - y)]

    def copy(w, k, block, to, src=None):
        dst = outs[w].at[4 * block[0] + 2 * block[1] + block[2]]
        return pltpu.make_async_remote_copy(
            src_ref=dst if src is None else src, dst_ref=dst,
            send_sem=send_sems.at[w * per + k], recv_sem=recv_sems.at[w * per + k],
            device_id=to, device_id_type=MESH_ID)

    def local(w, me):
        return pltpu.make_async_copy(ins[w], outs[w].at[4 * me[0] + 2 * me[1] + me[2]], loc_sems.at[w])

    def first(me, sib, chips):
        out = []
        for j, chip in enumerate(chips):
            out += [copy(w, 1 + j, me, (*chip, me[2]), src=ins[w]) for w in range(n)]
        return out + [copy(w, 0, me, sib, src=ins[w]) for w in range(n)]

    def start():
        me, sib, chips = place()
        for w in range(n):
            local(w, me).start()
        for cp in first(me, sib, chips):
            cp.start()

    def forward():
        me, sib, chips = place()
        for j, chip in enumerate(chips):
            for w in range(n):
                copy(w, 1 + j, (*chip, me[2]), me).wait_recv()
                copy(w, 4 + j, (*chip, me[2]), sib).start()

    def finish():
        me, sib, chips = place()
        for w in range(n):
            copy(w, 0, sib, me).wait_recv()
        for j, chip in enumerate(chips):
            for w in range(n):
                copy(w, 4 + j, (*chip, 1 - me[2]), me).wait_recv()
        for cp in first(me, sib, chips):
            cp.wait_send()
        for j, chip in enumerate(chips):
            for w in range(n):
                copy(w, 4 + j, (*chip, me[2]), sib).wait_send()
        for w in range(n):
            local(w, me).wait()

    return start, forward, finish


def _allgather(arrs, name):
    n = len(arrs)

    def body(*refs):
        start, forward, finish = _gather_phases(refs[:n], refs[n:2 * n], *refs[2 * n:])
        start()
        forward()
        finish()

    return pl.pallas_call(
        body, name=name, out_shape=tuple(jax.ShapeDtypeStruct((NDEV,) + a.shape, a.dtype) for a in arrs),
        in_specs=[ANY] * n, out_specs=tuple([ANY] * n), scratch_shapes=_gather_sems(n),
    )(*arrs)


def _exchange_and_gather(ex, ga, name):
    ne, ng = len(ex), len(ga)

    def body(*refs):
        ins, outs, sems = refs[:ne + ng], refs[ne + ng:2 * (ne + ng)], refs[2 * (ne + ng):]
        e_start, e_finish = _exchange_phases(ins[:ne], outs[:ne], *sems[:2])
        g_start, g_forward, g_finish = _gather_phases(ins[ne:], outs[ne:], *sems[2:])
        e_start()
        g_start()
        g_forward()
        g_finish()
        e_finish()

    res = pl.pallas_call(
        body, name=name,
        out_shape=tuple(jax.ShapeDtypeStruct(a.shape, a.dtype) for a in ex)
        + tuple(jax.ShapeDtypeStruct((NDEV,) + a.shape, a.dtype) for a in ga),
        in_specs=[ANY] * (ne + ng), out_specs=tuple([ANY] * (ne + ng)),
        scratch_shapes=_exchange_sems(ne) + _gather_sems(ng),
    )(*ex, *ga)
    return res[:ne], res[ne:]


def _ffn_fwd(xin, gin, bin_, w_in, w_out, name, gather=()):
    t = xin.shape[0]
    tm = _tile(t, 512)
    nt = t // tm
    chunks = _f_chunks()
    ng = len(gather)

    def body(*refs):
        x_ref, gi_ref, bi_ref, win_hbm, wout_hbm = refs[:5]
        g_ref, u_ref, xh_ref, rs_ref = refs[5 + ng:9 + ng]
        win_v, wout_v, acc_ref = refs[9 + 2 * ng:12 + 2 * ng]
        if ng:
            g_start, g_forward, g_finish = _gather_phases(refs[5:5 + ng], refs[9 + ng:9 + 2 * ng],
                                                          *refs[12 + 2 * ng:])

        @pl.when(pl.program_id(0) == 0)
        def _():
            if ng:
                g_start()
            pltpu.sync_copy(win_hbm, win_v)
            pltpu.sync_copy(wout_hbm, wout_v)

        if ng:
            pl.when(pl.program_id(0) == nt // 2)(g_forward)
            pl.when(pl.program_id(0) == nt - 1)(g_finish)

        x = x_ref[...] * gi_ref[...] + bi_ref[...]
        xb = x.astype(MXU_DT)
        for ci, (c0, fc) in enumerate(chunks):
            gc = _mm(xb, win_v[:, c0:c0 + fc])
            uc = _mm(xb, win_v[:, F + c0:F + c0 + fc])
            g_ref[:, c0:c0 + fc] = gc.astype(g_ref.dtype)
            u_ref[:, c0:c0 + fc] = uc.astype(u_ref.dtype)
            hc = (gc * _sigmoid(gc) * uc).astype(MXU_DT)
            part = _mm(hc, wout_v[c0:c0 + fc, :])
            if ci == 0:
                acc_ref[...] = part
            else:
                acc_ref[...] += part
        xh, rstd = _ln_fwd(ALPHA * x + 0.5 * acc_ref[...])
        xh_ref[...] = xh
        rs_ref[...] = rstd

    row = pl.BlockSpec((tm, D), lambda i: (i, 0))
    vec = pl.BlockSpec((1, D), lambda i: (0, 0))
    act = pl.BlockSpec((tm, F), lambda i: (i, 0))
    return pl.pallas_call(
        body, name=name, grid=(nt,),
        in_specs=[row, vec, vec, ANY, ANY] + [ANY] * ng,
        out_specs=(act, act, row, pl.BlockSpec((tm, 1), lambda i: (i, 0))) + (ANY,) * ng,
        out_shape=(jax.ShapeDtypeStruct((t, F), MXU_DT), jax.ShapeDtypeStruct((t, F), MXU_DT),
                   jax.ShapeDtypeStruct((t, D), F32), jax.ShapeDtypeStruct((t, 1), F32))
        + tuple(jax.ShapeDtypeStruct((NDEV,) + a.shape, a.dtype) for a in gather),
        scratch_shapes=[pltpu.VMEM((D, 2 * F), MXU_DT), pltpu.VMEM((F, D), MXU_DT), pltpu.VMEM((tm, D), F32)]
        + (_gather_sems(ng) if ng else []),
        compiler_params=_params(52),
    )(xin, gin, bin_, w_in, w_out, *gather)


QROWS = 80
BIAS_AT = DH


def _place_matrices():
    import numpy as np
    pk = np.zeros((NH, LANES, LANES), np.float32)
    pqt = np.zeros((NH, LANES, LANES), np.float32)
    for h in range(NH):
        for piece in range(3):
            pk[h, 8 * piece + h, BIAS_AT + 3 + piece] = -1.0
            pqt[h, BIAS_AT + piece, 8 * piece + h] = 1.0
    pkt = np.transpose(pk, (0, 2, 1))
    return tuple(jnp.asarray(m, MXU_DT) for m in (pk, pqt, pkt))


def _mix_proj_fwd(xh1, g1, b1, w_kv, w_qkv_t, w_bch, w_f, bf_pad, name):
    t = xh1.shape[0]
    tm = _tile(t, 512)
    pk, pqt, pkt = _place_matrices()

    def body(x_ref, g_ref, b_ref, wkv_ref, wt_ref, wb_ref, wf_ref, bf_ref, pk_ref, pqt_ref, pkt_ref,
             ka_ref, va_ref, qat_ref, kat_ref, vta_ref, bch_ref, z_ref, r_ref, carry):
        @pl.when(pl.program_id(0) == 0)
        def _():
            carry[...] = jnp.zeros_like(carry)

        xb = (x_ref[...] * g_ref[...] + b_ref[...]).astype(MXU_DT)
        kv = _mm(xb, wkv_ref[...])
        qkvt = _mm_nt(wt_ref[...], xb)
        bch_ref[...] = _mm(xb, wb_ref[...])
        z = _mm(xb, wf_ref[...]) + bf_ref[...]
        z_ref[...] = z
        logf = jnp.minimum(z, 0.0) - jnp.log(1.0 + jnp.exp(-jnp.abs(z)))
        row = lax.broadcasted_iota(jnp.int32, (tm, tm), 0)
        col = lax.broadcasted_iota(jnp.int32, (tm, tm), 1)
        tri = jnp.where(row >= col, 1.0, 0.0).astype(MXU_DT)
        c = carry[...] + _mm_sel(tri, logf)
        carry[...] = c[tm - 1:tm, :]
        r_ref[0] = c[0:1, :]
        lane = lax.broadcasted_iota(jnp.int32, (1, LANES), 1)
        hi, mid, lo = _split3(jnp.where(lane < NH, c - c[0:1, :], 0.0))
        pieces = (hi.astype(F32) + pltpu.roll(mid.astype(F32), 8, 1) + pltpu.roll(lo.astype(F32), 16, 1)
                  ).astype(MXU_DT)
        sub = lax.broadcasted_iota(jnp.int32, (DH, 1), 0)
        ones_k_lanes = jnp.where((lane >= BIAS_AT) & (lane < BIAS_AT + 3), 1.0, 0.0)
        ones_q_rows = jnp.where((sub >= 3) & (sub < 6), 1.0, 0.0)
        ones_k_rows = jnp.where(sub[0:QROWS - DH] < 3, 1.0, 0.0)
        first_row = jnp.where(sub == 0, 1.0, 0.0) + jnp.zeros((DH, tm), F32)
        for h in range(NH):
            pair, odd = divmod(h, 2)
            k2 = kv[:, LANES * pair:LANES * (pair + 1)]
            v2 = kv[:, DA + LANES * pair:DA + LANES * (pair + 1)]
            if odd:
                k2, v2 = pltpu.roll(k2, DH, 1), pltpu.roll(v2, DH, 1)
            ka_ref[h] = jnp.where(lane < DH, k2, _mm(pieces, pk_ref[h]) + ones_k_lanes).astype(ka_ref.dtype)
            va_ref[h] = jnp.where(lane < DH, v2, 0.0).astype(va_ref.dtype)
            qat_ref[h, 0:DH, :] = (qkvt[DH * h:DH * (h + 1)] * 0.125).astype(qat_ref.dtype)
            qat_ref[h, DH:LANES, :] = (_mm_nt(pqt_ref[h], pieces)[DH:LANES] + ones_q_rows).astype(qat_ref.dtype)
            kat_ref[h, 0:DH, :] = qkvt[DA + DH * h:DA + DH * (h + 1)].astype(kat_ref.dtype)
            kat_ref[h, DH:QROWS, :] = (_mm_nt(pkt_ref[h], pieces)[DH:QROWS] + ones_k_rows).astype(kat_ref.dtype)
            vt = qkvt[2 * DA + DH * h:2 * DA + DH * (h + 1)]
            vta_ref[h, 0:DH, :] = (first_row if odd else vt).astype(vta_ref.dtype)
            vta_ref[h, DH:LANES, :] = (vt if odd else first_row).astype(vta_ref.dtype)

    row = lambda w: pl.BlockSpec((tm, w), lambda i: (i, 0))
    full = lambda a: pl.BlockSpec(a.shape, lambda i: (0,) * a.ndim)
    nat = pl.BlockSpec((NH, tm, LANES), lambda i: (0, i, 0))
    fmaj = lambda rows: pl.BlockSpec((NH, rows, tm), lambda i: (0, 0, i))
    return pl.pallas_call(
        body, name=name, grid=(t // tm,),
        in_specs=[row(D), full(g1), full(b1), full(w_kv), full(w_qkv_t), full(w_bch), full(w_f), full(bf_pad),
                  full(pk), full(pqt), full(pkt)],
        out_specs=(nat, nat, fmaj(LANES), fmaj(QROWS), fmaj(LANES), row(3 * DCV), row(LANES),
                   pl.BlockSpec((1, 1, LANES), lambda i: (i, 0, 0))),
        out_shape=(jax.ShapeDtypeStruct((NH, t, LANES), MXU_DT), jax.ShapeDtypeStruct((NH, t, LANES), MXU_DT),
                   jax.ShapeDtypeStruct((NH, LANES, t), MXU_DT), jax.ShapeDtypeStruct((NH, QROWS, t), MXU_DT),
                   jax.ShapeDtypeStruct((NH, LANES, t), MXU_DT), jax.ShapeDtypeStruct((t, 3 * DCV), F32),
                   jax.ShapeDtypeStruct((t, LANES), F32), jax.ShapeDtypeStruct((t // tm, 1, LANES), F32)),
        scratch_shapes=[pltpu.VMEM((1, LANES), F32)],
        compiler_params=_params(56),
    )(xh1, g1, b1, w_kv, w_qkv_t, w_bch, w_f, bf_pad, pk, pqt, pkt)


def _attn_fwd(qat, ka, vta, r, name):
    t = ka.shape[1]
    tq = _tile(t, 512)
    nq = t // tq

    def body(r_ref, q_ref, k_ref, v_ref, o_ref, l_ref, st0, st1):
        hp, i = pl.program_id(0), pl.program_id(1)
        key = lax.broadcasted_iota(jnp.int32, (tq, tq), 0)
        qry = lax.broadcasted_iota(jnp.int32, (tq, tq), 1)

        def tile_of(pos):
            return jnp.where(pos == 0, i, pos - 1)

        def scores(pos, buf, masked):
            off = pl.multiple_of(tile_of(pos) * tq, tq)
            for a in range(2):
                st = _mm(k_ref[a, pl.ds(off, tq), :], q_ref[a])
                buf[a] = jnp.where(qry >= key, st, NEG) if masked else st

        def consume(pos, buf, carry):
            j = tile_of(pos)
            off = pl.multiple_of(j * tq, tq)
            out = []
            for a in range(2):
                m, acc = carry[a]
                st = buf[a]
                d = r_ref[2 * hp + a, i] - r_ref[2 * hp + a, j]
                m_new = jnp.maximum(m, jnp.max(st, axis=0, keepdims=True) + d)
                pt = jnp.exp(st - (m_new - d))
                acc = jnp.exp(m - m_new) * acc + _mm(v_ref[a, :, pl.ds(off, tq)], pt.astype(MXU_DT))
                out.append((m_new, acc))
            return tuple(out)

        def trip(p, carry):
            scores(2 * p + 1, st1, False)
            carry = consume(2 * p, st0, carry)
            scores(2 * p + 2, st0, False)
            return consume(2 * p + 1, st1, carry)

        scores(0, st0, True)
        init = tuple((jnp.full((1, tq), NEG, F32), jnp.zeros((LANES, tq), F32)) for _ in range(2))
        trips = i // 2
        carry = lax.fori_loop(0, trips, trip, init)

        def last_two(cr):
            scores(2 * trips + 1, st1, False)
            return consume(2 * trips + 1, st1, consume(2 * trips, st0, cr))

        (ma, acca), (mb, accb) = lax.cond(i % 2 == 1, last_two, lambda cr: consume(2 * trips, st0, cr), carry)
        la, lb = acca[DH:DH + 1, :], accb[0:1, :]
        l_ref[0] = ma + jnp.log(la)
        l_ref[1] = mb + jnp.log(lb)
        sub = lax.broadcasted_iota(jnp.int32, (LANES, tq), 0)
        o_ref[...] = jnp.where(sub < DH, acca / la, accb / lb).T

    return pl.pallas_call(
        body, name=name, grid=(NH // 2, nq),
        in_specs=[pl.BlockSpec(memory_space=pltpu.SMEM),
                  pl.BlockSpec((2, LANES, tq), lambda p, i: (p, 0, i)),
                  pl.BlockSpec((2, t, LANES), lambda p, i: (p, 0, 0)),
                  pl.BlockSpec((2, LANES, t), lambda p, i: (p, 0, 0))],
        out_specs=(pl.BlockSpec((tq, LANES), lambda p, i: (i, p)),
                   pl.BlockSpec((2, 1, tq), lambda p, i: (p, 0, i))),
        out_shape=(jax.ShapeDtypeStruct((t, DA), F32), jax.ShapeDtypeStruct((NH, 1, t), F32)),
        scratch_shapes=[pltpu.VMEM((2, tq, tq), F32), pltpu.VMEM((2, tq, tq), F32)],
        compiler_params=_params(48, 2),
    )(r, qat, ka, vta)


def _conv_parts(bch):
    return bch[:, 0:DCV], bch[:, DCV:2 * DCV], bch[:, 2 * DCV:3 * DCV]


def _mix_post_fwd(o, bch, conv_w, g_attn, g_conv, xh1, g1, b1, w_mo, name):
    t = o.shape[0]
    tm = _tile(t, 512)
    hb = tm // 8

    def body(o_ref, bch_ref, halo_ref, cw_ref, ga_ref, gc_ref, x_ref, g_ref, b_ref, w_ref,
             mg_ref, xh_ref, rs_ref, ext):
        i = pl.program_id(0)
        an, _ = _rms_fwd(o_ref[...])
        mg_ref[:, 0:DA] = (an * ga_ref[...]).astype(mg_ref.dtype)
        bb, cc, hh = _conv_parts(bch_ref[...])
        _, hc, hh_h = _conv_parts(halo_ref[...])
        u = cc * hh
        ext[0:8, :] = jnp.where(i > 0, hc * hh_h, 0.0)
        ext[8:8 + tm, :] = u
        raw = cw_ref[0:1, :] * ext[6:6 + tm, :] + cw_ref[1:2, :] * ext[7:7 + tm, :] + cw_ref[2:3, :] * u
        cn, _ = _rms_fwd(bb * raw)
        mg_ref[:, DA:D] = (cn * gc_ref[...]).astype(mg_ref.dtype)
        x1 = x_ref[...] * g_ref[...] + b_ref[...]
        xh, rstd = _ln_fwd(ALPHA * x1 + _mm(mg_ref[...], w_ref[...]))
        xh_ref[...] = xh
        rs_ref[...] = rstd

    row = lambda w: pl.BlockSpec((tm, w), lambda i: (i, 0))
    full = lambda a: pl.BlockSpec(a.shape, lambda i: (0, 0))
    return pl.pallas_call(
        body, name=name, grid=(t // tm,),
        in_specs=[row(DA), row(3 * DCV),
                  pl.BlockSpec((8, 3 * DCV), lambda i: (jnp.maximum(i * hb - 1, 0), 0)),
                  full(conv_w), full(g_attn), full(g_conv), row(D), full(g1), full(b1), full(w_mo)],
        out_specs=(row(D), row(D), pl.BlockSpec((tm, 1), lambda i: (i, 0))),
        out_shape=(jax.ShapeDtypeStruct((t, D), MXU_DT), jax.ShapeDtypeStruct((t, D), F32),
                   jax.ShapeDtypeStruct((t, 1), F32)),
        scratch_shapes=[pltpu.VMEM((tm + 8, DCV), F32)],
        compiler_params=_params(48),
    )(o, bch, bch, conv_w, g_attn, g_conv, xh1, g1, b1, w_mo)


def _tail(xh3, rs3, g3, b3, p, w_g, w_ple, bg, g4, b4, target, name):
    t = xh3.shape[0]
    tm = _tile(t, 512)

    def body(x_ref, rs_ref, g3_ref, b3_ref, p_ref, wg_ref, wp_ref, bg_ref, g4_ref, b4_ref, t_ref,
             dr_ref, dz_ref, de_ref, st_ref):
        @pl.when(pl.program_id(0) == 0)
        def _():
            st_ref[...] = jnp.zeros_like(st_ref)

        xh3v = x_ref[...]
        x3 = xh3v * g3_ref[...] + b3_ref[...]
        gate = _sigmoid(_mm(x3.astype(MXU_DT), wg_ref[...]) + bg_ref[...])
        e = _mm(p_ref[...].astype(MXU_DT), wp_ref[...])
        xh4, rstd4 = _ln_fwd(ALPHA * x3 + gate * e)
        diff = xh4 * g4_ref[...] + b4_ref[...] - t_ref[...]
        dy = diff * (1.0 / D)
        st_ref[5:6, :] += _colsum(diff * diff)
        st_ref[0:1, :] += _colsum(dy * xh4)
        st_ref[1:2, :] += _colsum(dy)
        dr4 = _ln_bwd(dy * g4_ref[...], xh4, rstd4)
        de_ref[...] = (dr4 * gate).astype(de_ref.dtype)
        dz = dr4 * e * gate * (1.0 - gate)
        st_ref[2:3, :] += _colsum(dz)
        dzb = dz.astype(MXU_DT)
        dz_ref[...] = dzb
        dx3 = ALPHA * dr4 + _mm_nt(dzb, wg_ref[...])
        st_ref[3:4, :] += _colsum(dx3 * xh3v)
        st_ref[4:5, :] += _colsum(dx3)
        dr_ref[...] = _ln_bwd(dx3 * g3_ref[...], xh3v, rs_ref[...])

    row = lambda w: pl.BlockSpec((tm, w), lambda i: (i, 0))
    full = lambda a: pl.BlockSpec(a.shape, lambda i: (0, 0))
    return pl.pallas_call(
        body, name=name, grid=(t // tm,),
        in_specs=[row(D), row(1), full(g3), full(b3), row(PLE), full(w_g), full(w_ple), full(bg), full(g4),
                  full(b4), row(D)],
        out_specs=(row(D), row(D), row(D), pl.BlockSpec((8, D), lambda i: (0, 0))),
        out_shape=(jax.ShapeDtypeStruct((t, D), F32), jax.ShapeDtypeStruct((t, D), MXU_DT),
                   jax.ShapeDtypeStruct((t, D), MXU_DT), jax.ShapeDtypeStruct((8, D), F32)),
        compiler_params=_params(48),
    )(xh3, rs3, g3, b3, p, w_g, w_ple, bg, g4, b4, target)


def _ffn_bwd(dr, gact, uact, xin, rsin, gin, w_in, w_out, prev_ln, name, exchange=()):
    t = dr.shape[0]
    tm = _tile(t, 512)
    nt = t // tm
    chunks = _f_chunks()
    ne = len(exchange)

    def body(*refs):
        dr_ref, g_ref, u_ref, x_ref, rs_ref, gi_ref, win_hbm, wout_hbm = refs[:8]
        df_ref, dg_ref, du_ref, dx_ref, st_ref = refs[8 + ne:13 + ne]
        win_v, wout_v = refs[13 + 2 * ne:15 + 2 * ne]
        acc_ref = dx_ref
        if ne:
            e_start, e_finish = _exchange_phases(refs[8:8 + ne], refs[13 + ne:13 + 2 * ne], *refs[15 + 2 * ne:])

        @pl.when(pl.program_id(0) == 0)
        def _():
            if ne:
                e_start()
            pltpu.sync_copy(win_hbm, win_v)
            pltpu.sync_copy(wout_hbm, wout_v)
            st_ref[...] = jnp.zeros_like(st_ref)

        if ne:
            pl.when(pl.program_id(0) == nt - 1)(e_finish)

        drv = dr_ref[...]
        dfb = (0.5 * drv).astype(MXU_DT)
        df_ref[...] = dfb
        for ci, (c0, fc) in enumerate(chunks):
            dh = _mm_nt(dfb, wout_v[c0:c0 + fc, :])
            g = g_ref[:, c0:c0 + fc].astype(F32)
            u = u_ref[:, c0:c0 + fc].astype(F32)
            sg = _sigmoid(g)
            dgb = (dh * u * (sg * (1.0 + g * (1.0 - sg)))).astype(MXU_DT)
            dub = (dh * (g * sg)).astype(MXU_DT)
            dg_ref[:, c0:c0 + fc] = dgb
            du_ref[:, c0:c0 + fc] = dub
            part = _mm_nt(dgb, win_v[:, c0:c0 + fc]) + _mm_nt(dub, win_v[:, F + c0:F + c0 + fc])
            if ci == 0:
                acc_ref[...] = part
            else:
                acc_ref[...] += part
        dx = ALPHA * drv + acc_ref[...]
        if prev_ln:
            xh = x_ref[...]
            st_ref[0:1, :] += _colsum(dx * xh)
            st_ref[1:2, :] += _colsum(dx)
            dx_ref[...] = _ln_bwd(dx * gi_ref[...], xh, rs_ref[...])
        else:
            dx_ref[...] = dx

    row = pl.BlockSpec((tm, D), lambda i: (i, 0))
    vec = pl.BlockSpec((1, D), lambda i: (0, 0))
    act = pl.BlockSpec((tm, F), lambda i: (i, 0))
    return pl.pallas_call(
        body, name=name, grid=(nt,),
        in_specs=[row, act, act, row, pl.BlockSpec((tm, 1), lambda i: (i, 0)), vec, ANY, ANY] + [ANY] * ne,
        out_specs=(row, act, act, row, pl.BlockSpec((8, D), lambda i: (0, 0))) + (ANY,) * ne,
        out_shape=(jax.ShapeDtypeStruct((t, D), MXU_DT), jax.ShapeDtypeStruct((t, F), MXU_DT),
                   jax.ShapeDtypeStruct((t, F), MXU_DT), jax.ShapeDtypeStruct((t, D), F32),
                   jax.ShapeDtypeStruct((8, D), F32))
        + tuple(jax.ShapeDtypeStruct(a.shape, a.dtype) for a in exchange),
        scratch_shapes=[pltpu.VMEM((D, 2 * F), MXU_DT), pltpu.VMEM((F, D), MXU_DT)]
        + (_exchange_sems(ne) if ne else []),
        compiler_params=_params(60),
    )(dr, gact, uact, xin, rsin, gin, w_in, w_out, *exchange)


def _mix_post_bwd(dr2, o, bch, conv_w, g_attn, g_conv, w_mo, name):
    t = dr2.shape[0]
    tm = _tile(t, 512)
    hb = tm // 8

    def body(dr_ref, o_ref, bch_ref, halo_ref, cw_ref, ga_ref, gc_ref, w_ref,
             dm_ref, do_ref, dl_ref, dy_ref, st_ref, ext):
        i = pl.program_id(0)

        @pl.when(i == 0)
        def _():
            st_ref[...] = jnp.zeros_like(st_ref)

        dmb = dr_ref[...].astype(MXU_DT)
        dm_ref[...] = dmb
        dmg = _mm_nt(dmb, w_ref[...])
        ov = o_ref[...]
        an, ra = _rms_fwd(ov)
        da = dmg[:, 0:DA]
        st_ref[0:1, :] += _colsum(da * an)
        dxa = _rms_bwd(da * ga_ref[...], an, ra)
        dor = dxa.astype(MXU_DT).astype(F32)
        dot = dor.T
        for h in range(NH):
            do_ref[h, 0:DH, :] = dot[DH * h:DH * (h + 1)].astype(do_ref.dtype)
            do_ref[h, DH:LANES, :] = jnp.zeros((LANES - DH, tm), do_ref.dtype)
        srow = lax.broadcasted_iota(jnp.int32, (8, DA), 0)
        scol = lax.broadcasted_iota(jnp.int32, (8, DA), 1)
        sel = jnp.where((scol // DH) == srow, 1.0, 0.0).astype(MXU_DT)
        hi, mid, lo = _split3(dor * ov)
        delta = _mm_nt(sel, hi) + _mm_nt(sel, mid) + _mm_nt(sel, lo)
        for h in range(NH):
            dl_ref[h] = delta[h:h + 1, :]
        bb, cc, hh = _conv_parts(bch_ref[...])
        _, hc, hh_h = _conv_parts(halo_ref[...])
        u = cc * hh
        ext[0:8, :] = jnp.where(i > 0, hc * hh_h, 0.0)
        ext[8:8 + tm, :] = u
        raw = cw_ref[0:1, :] * ext[6:6 + tm, :] + cw_ref[1:2, :] * ext[7:7 + tm, :] + cw_ref[2:3, :] * u
        cn, rc = _rms_fwd(bb * raw)
        dcn = dmg[:, DA:D]
        st_ref[1:2, :] += _colsum(dcn * cn)
        dy_ref[...] = _rms_bwd(dcn * gc_ref[...], cn, rc)

    row = lambda w: pl.BlockSpec((tm, w), lambda i: (i, 0))
    full = lambda a: pl.BlockSpec(a.shape, lambda i: (0, 0))
    return pl.pallas_call(
        body, name=name, grid=(t // tm,),
        in_specs=[row(D), row(DA), row(3 * DCV),
                  pl.BlockSpec((8, 3 * DCV), lambda i: (jnp.maximum(i * hb - 1, 0), 0)),
                  full(conv_w), full(g_attn), full(g_conv), full(w_mo)],
        out_specs=(row(D), pl.BlockSpec((NH, LANES, tm), lambda i: (0, 0, i)),
                   pl.BlockSpec((NH, 1, tm), lambda i: (0, 0, i)), row(DCV),
                   pl.BlockSpec((8, DA), lambda i: (0, 0))),
        out_shape=(jax.ShapeDtypeStruct((t, D), MXU_DT), jax.ShapeDtypeStruct((NH, LANES, t), MXU_DT),
                   jax.ShapeDtypeStruct((NH, 1, t), F32), jax.ShapeDtypeStruct((t, DCV), F32),
                   jax.ShapeDtypeStruct((8, DA), F32)),
        scratch_shapes=[pltpu.VMEM((tm + 8, DCV), F32)],
        compiler_params=_params(48),
    )(dr2, o, bch, bch, conv_w, g_attn, g_conv, w_mo)


def _conv_bwd(dy, bch, conv_w, name):
    t = dy.shape[0]
    tm = _tile(t, 512)
    hb = tm // 8
    nt = t // tm

    def body(dy_ref, dyn_ref, bch_ref, prev_ref, next_ref, cw_ref, out_ref, st_ref, ext_u, ext_d):
        i = pl.program_id(0)

        @pl.when(i == 0)
        def _():
            st_ref[...] = jnp.zeros_like(st_ref)

        bb, cc, hh = _conv_parts(bch_ref[...])
        _, pc, ph = _conv_parts(prev_ref[...])
        nb, _, _ = _conv_parts(next_ref[...])
        u = cc * hh
        ext_u[0:8, :] = jnp.where(i > 0, pc * ph, 0.0)
        ext_u[8:8 + tm, :] = u
        u1 = ext_u[7:7 + tm, :]
        u2 = ext_u[6:6 + tm, :]
        w0, w1, w2 = cw_ref[0:1, :], cw_ref[1:2, :], cw_ref[2:3, :]
        dyv = dy_ref[...]
        out_ref[:, 0:DCV] = (dyv * (w0 * u2 + w1 * u1 + w2 * u)).astype(out_ref.dtype)
        dcr = dyv * bb
        ext_d[0:tm, :] = dcr
        ext_d[tm:tm + 8, :] = jnp.where(i < nt - 1, dyn_ref[...] * nb, 0.0)
        du = w2 * dcr + w1 * ext_d[1:1 + tm, :] + w0 * ext_d[2:2 + tm, :]
        out_ref[:, DCV:2 * DCV] = (du * hh).astype(out_ref.dtype)
        out_ref[:, 2 * DCV:3 * DCV] = (du * cc).astype(out_ref.dtype)
        st_ref[0:1, :] += _colsum(dcr * u2)
        st_ref[1:2, :] += _colsum(dcr * u1)
        st_ref[2:3, :] += _colsum(dcr * u)

    row = lambda w: pl.BlockSpec((tm, w), lambda i: (i, 0))
    prev = lambda w: pl.BlockSpec((8, w), lambda i: (jnp.maximum(i * hb - 1, 0), 0))
    nxt = lambda w: pl.BlockSpec((8, w), lambda i: (jnp.minimum((i + 1) * hb, nt * hb - 1), 0))
    return pl.pallas_call(
        body, name=name, grid=(nt,),
        in_specs=[row(DCV), nxt(DCV), row(3 * DCV), prev(3 * DCV), nxt(3 * DCV),
                  pl.BlockSpec(conv_w.shape, lambda i: (0, 0))],
        out_specs=(row(3 * DCV), pl.BlockSpec((8, DCV), lambda i: (0, 0))),
        out_shape=(jax.ShapeDtypeStruct((t, 3 * DCV), MXU_DT), jax.ShapeDtypeStruct((8, DCV), F32)),
        scratch_shapes=[pltpu.VMEM((tm + 8, DCV), F32), pltpu.VMEM((tm + 8, DCV), F32)],
        compiler_params=_params(48),
    )(dy, dy, bch, bch, bch, conv_w)


def _attn_bwd(ka, kat, va, qat, dot, lrow, drow, r, name):
    t = ka.shape[1]
    tq = _tile(t, 512)
    nq = t // tq

    def body(r_ref, ka_ref, kat_ref, va_ref, l_ref, dl_ref, qat_hbm, dot_hbm,
             dk_ref, dv_ref, dck_ref, dqt_hbm, dcq_hbm, qat_v, dot_v, dq_acc):
        hp, j = pl.program_id(0), pl.program_id(1)

        @pl.when(j == 0)
        def _():
            pltpu.sync_copy(qat_hbm.at[pl.ds(2 * hp, 2)], qat_v)
            pltpu.sync_copy(dot_hbm.at[pl.ds(2 * hp, 2)], dot_v)
            dq_acc[...] = jnp.zeros_like(dq_acc)

        key = lax.broadcasted_iota(jnp.int32, (tq, tq), 0)
        qry = lax.broadcasted_iota(jnp.int32, (tq, tq), 1)

        def step(i, carry, masked):
            off = pl.multiple_of(i * tq, tq)
            out = []
            for a in range(2):
                dk, dv = carry[a]
                st = _mm(ka_ref[a], qat_v[a, :, pl.ds(off, tq)])
                dpt = _mm(va_ref[a], dot_v[a, :, pl.ds(off, tq)])
                if masked:
                    st = jnp.where(qry >= key, st, NEG)
                d = r_ref[2 * hp + a, i] - r_ref[2 * hp + a, j]
                pt = jnp.exp(st - (l_ref[a, :, pl.ds(off, tq)] - d))
                dsb = (pt * (dpt - dl_ref[a, :, pl.ds(off, tq)])).astype(MXU_DT)
                dv = dv + _mm_nt(dot_v[a, 0:DH, pl.ds(off, tq)], pt.astype(MXU_DT))
                dk = dk + _mm_nt(qat_v[a, 0:QROWS, pl.ds(off, tq)], dsb)
                dq_acc[a, :, pl.ds(off, tq)] += _mm(kat_ref[a], dsb)
                out.append((dk, dv))
            return tuple(out)

        init = tuple((jnp.zeros((QROWS, tq), F32), jnp.zeros((DH, tq), F32)) for _ in range(2))
        carry = step(j, init, True)
        (dka, dva), (dkb, dvb) = lax.fori_loop(j + 1, nq, lambda i, cr: step(i, cr, False), carry)
        dk_ref[...] = jnp.concatenate([dka[0:DH], dkb[0:DH]], axis=0).T.astype(dk_ref.dtype)
        dv_ref[...] = jnp.concatenate([dva, dvb], axis=0).T.astype(dv_ref.dtype)
        dck_ref[0] = -dka[DH + 3:DH + 4, :]
        dck_ref[1] = -dkb[DH + 3:DH + 4, :]

        @pl.when(j == nq - 1)
        def _():
            pltpu.sync_copy(dq_acc, dqt_hbm.at[pl.ds(2 * hp, 2)])
            pltpu.sync_copy(dq_acc.at[:, DH:DH + 1, :], dcq_hbm.at[pl.ds(2 * hp, 2)])

    pair = lambda rows, cols: pl.BlockSpec((2, rows, cols), lambda p, j: (p, 0, 0))
    return pl.pallas_call(
        body, name=name, grid=(NH // 2, nq),
        in_specs=[pl.BlockSpec(memory_space=pltpu.SMEM),
                  pl.BlockSpec((2, tq, LANES), lambda p, j: (p, j, 0)),
                  pl.BlockSpec((2, QROWS, tq), lambda p, j: (p, 0, j)),
                  pl.BlockSpec((2, tq, LANES), lambda p, j: (p, j, 0)),
                  pair(1, t), pair(1, t), ANY, ANY],
        out_specs=(pl.BlockSpec((tq, LANES), lambda p, j: (j, p)),
                   pl.BlockSpec((tq, LANES), lambda p, j: (j, p)),
                   pl.BlockSpec((2, 1, tq), lambda p, j: (p, 0, j)), ANY, ANY),
        out_shape=(jax.ShapeDtypeStruct((t, DA), MXU_DT), jax.ShapeDtypeStruct((t, DA), MXU_DT),
                   jax.ShapeDtypeStruct((NH, 1, t), F32), jax.ShapeDtypeStruct((NH, QROWS, t), F32),
                   jax.ShapeDtypeStruct((NH, 1, t), F32)),
        scratch_shapes=[pltpu.VMEM((2, LANES, t), MXU_DT), pltpu.VMEM((2, LANES, t), MXU_DT),
                        pltpu.VMEM((2, QROWS, t), F32)],
        compiler_params=_params(52, 2),
    )(r, ka, kat, va, lrow, drow, qat, dot)


def _mix_proj_bwd(dr2, dqt, dk, dv, dbch, dc, z, xh1, rs1, g1, w_qkv, w_bch, w_f, name):
    t = dr2.shape[0]
    tm = _tile(t, 512)
    nt = t // tm

    def body(dr_ref, dqt_ref, dk_ref, dv_ref, db_ref, dc_ref, z_ref, x_ref, rs_ref, g_ref,
             wq_ref, wb_ref, wf_ref, out_ref, df_ref, dq_ref, st_ref, carry):
        @pl.when(pl.program_id(0) == 0)
        def _():
            carry[...] = jnp.zeros_like(carry)
            st_ref[...] = jnp.zeros_like(st_ref)

        dq_ref[...] = (jnp.concatenate([dqt_ref[h, 0:DH, :] for h in range(NH)], axis=0).T * 0.125
                       ).astype(dq_ref.dtype)

        row = lax.broadcasted_iota(jnp.int32, (tm, tm), 0)
        col = lax.broadcasted_iota(jnp.int32, (tm, tm), 1)
        triu = jnp.where(col >= row, 1.0, 0.0).astype(MXU_DT)
        dlogf = carry[...] + _mm_sel(triu, dc_ref[...])
        carry[...] = dlogf[0:1, :]
        dz = dlogf / (1.0 + jnp.exp(z_ref[...]))
        st_ref[2:3, 0:LANES] += _colsum(dz)
        dfb = dz.astype(MXU_DT)
        df_ref[...] = dfb
        dx = (ALPHA * dr_ref[...]
              + _mm_nt(dq_ref[...], wq_ref[:, 0:DA])
              + _mm_nt(dk_ref[...], wq_ref[:, DA:2 * DA])
              + _mm_nt(dv_ref[...], wq_ref[:, 2 * DA:3 * DA])
              + _mm_nt(db_ref[...], wb_ref[...])
              + _mm_nt(dfb, wf_ref[...]))
        xh = x_ref[...]
        st_ref[0:1, :] += _colsum(dx * xh)
        st_ref[1:2, :] += _colsum(dx)
        out_ref[...] = _ln_bwd(dx * g_ref[...], xh, rs_ref[...])

    row = lambda w: pl.BlockSpec((tm, w), lambda i: (nt - 1 - i, 0))
    full = lambda a: pl.BlockSpec(a.shape, lambda i: (0, 0))
    return pl.pallas_call(
        body, name=name, grid=(nt,),
        in_specs=[row(D), pl.BlockSpec((NH, QROWS, tm), lambda i: (0, 0, nt - 1 - i)), row(DA), row(DA),
                  row(3 * DCV), row(LANES), row(LANES), row(D), row(1),
                  full(g1), full(w_qkv), full(w_bch), full(w_f)],
        out_specs=(row(D), row(LANES), row(DA), pl.BlockSpec((8, D), lambda i: (0, 0))),
        out_shape=(jax.ShapeDtypeStruct((t, D), F32), jax.ShapeDtypeStruct((t, LANES), MXU_DT),
                   jax.ShapeDtypeStruct((t, DA), MXU_DT), jax.ShapeDtypeStruct((8, D), F32)),
        scratch_shapes=[pltpu.VMEM((1, LANES), F32)],
        compiler_params=_params(48),
    )(dr2, dqt, dk, dv, dbch, dc, z, xh1, rs1, g1, w_qkv, w_bch, w_f)


def _dw(mode, a_parts, b, m, n, name, tmm=None, tn=None, exchange=()):
    t = b.shape[0]
    tmm = tmm or m
    tn = tn or n
    tt = _tile(t, 2048)
    na, ne = len(a_parts), len(exchange)
    grid = (m // tmm, n // tn, t // tt)

    def body(*refs):
        a_refs, b_ref, o_ref = refs[:na], refs[na], refs[na + 1 + ne]
        if ne:
            e_start, e_finish = _exchange_phases(refs[na + 1:na + 1 + ne], refs[na + 2 + ne:na + 2 + 2 * ne],
                                                 *refs[na + 2 + 2 * ne:])
            at = lambda steps: functools.reduce(jnp.logical_and, [pl.program_id(d) == s for d, s in enumerate(steps)])
            pl.when(at((0, 0, 0)))(e_start)

        @pl.when(pl.program_id(2) == 0)
        def _():
            o_ref[...] = jnp.zeros_like(o_ref)

        if mode == "plain":
            a = a_refs[0][...].astype(MXU_DT)
        elif mode == "affine":
            a = (a_refs[0][...] * a_refs[1][...] + a_refs[2][...]).astype(MXU_DT)
        else:
            g = a_refs[0][...].astype(F32)
            a = (g * _sigmoid(g) * a_refs[1][...].astype(F32)).astype(MXU_DT)
        o_ref[...] += _mm_tn(a, b_ref[...].astype(MXU_DT))
        if ne:
            pl.when(at(tuple(g - 1 for g in grid)))(e_finish)

    a_tile = pl.BlockSpec((tt, tmm), lambda i, j, k: (k, i))
    a_vec = pl.BlockSpec((1, tmm), lambda i, j, k: (0, i))
    a_specs = {"plain": [a_tile], "affine": [a_tile, a_vec, a_vec], "swiglu": [a_tile, a_tile]}[mode]
    res = pl.pallas_call(
        body, name=name, grid=grid,
        in_specs=a_specs + [pl.BlockSpec((tt, tn), lambda i, j, k: (k, j))] + [ANY] * ne,
        out_specs=(pl.BlockSpec((tmm, tn), lambda i, j, k: (i, j)),) + (ANY,) * ne,
        out_shape=(jax.ShapeDtypeStruct((m, n), F32),)
        + tuple(jax.ShapeDtypeStruct(a.shape, a.dtype) for a in exchange),
        scratch_shapes=_exchange_sems(ne) if ne else [],
        compiler_params=_params(52, 3),
    )(*a_parts, b, *exchange)
    return res if ne else res[0]


def _dw_shared(a_parts, bs, name):
    xh, g, b = a_parts
    t, m = xh.shape
    tt = _tile(t, 1024)
    nb = len(bs)

    def body(*refs):
        x_ref, g_ref, b_ref = refs[:3]
        b_refs, o_refs = refs[3:3 + nb], refs[3 + nb:]

        @pl.when(pl.program_id(0) == 0)
        def _():
            for o_ref in o_refs:
                o_ref[...] = jnp.zeros_like(o_ref)

        at = (x_ref[...] * g_ref[...] + b_ref[...]).T.astype(MXU_DT)
        for b_ref, o_ref in zip(b_refs, o_refs):
            o_ref[...] += _mm(at, b_ref[...].astype(MXU_DT))

    vec = pl.BlockSpec((1, m), lambda k: (0, 0))
    return pl.pallas_call(
        body, name=name, grid=(t // tt,),
        in_specs=[pl.BlockSpec((tt, m), lambda k: (k, 0)), vec, vec]
        + [pl.BlockSpec((tt, x.shape[1]), lambda k: (k, 0)) for x in bs],
        out_specs=tuple(pl.BlockSpec((m, x.shape[1]), lambda k: (0, 0)) for x in bs),
        out_shape=tuple(jax.ShapeDtypeStruct((m, x.shape[1]), F32) for x in bs),
        compiler_params=_params(56),
    )(xh, g, b, *bs)


def _adamw(w, g, m, v):
    m = ADAM_B1 * m + (1.0 - ADAM_B1) * g
    v = ADAM_B2 * v + (1.0 - ADAM_B2) * (g * g)
    m_hat = m / (1.0 - ADAM_B1 ** ADAM_STEP)
    v_hat = v / (1.0 - ADAM_B2 ** ADAM_STEP)
    delta = -ADAM_LR * (m_hat / (jnp.sqrt(v_hat) + ADAM_EPS) + ADAM_WD * w)
    return delta, m, v


def _reduce_adamw(landed, own, w, m, v, name):
    r, c = own.shape
    tr = _tile(r, 128)

    def body(l_ref, o_ref, w_ref, m_ref, v_ref, g_out, d_out, m_out, v_out):
        me = 4 * lax.axis_index("x") + 2 * lax.axis_index("y") + lax.axis_index("c")
        g = None
        for j in range(NDEV):
            term = jnp.where(me == j, o_ref[...], l_ref[j].astype(F32))
            g = term if g is None else g + term
        g_out[...] = g
        d_out[...], m_out[...], v_out[...] = _adamw(w_ref[...], g, m_ref[...], v_ref[...])

    blk = pl.BlockSpec((tr, c), lambda i: (i, 0))
    sds = jax.ShapeDtypeStruct((r, c), F32)
    return pl.pallas_call(
        body, name=name, grid=(r // tr,),
        in_specs=[pl.BlockSpec((NDEV, tr, c), lambda i: (0, i, 0)), blk, blk, blk, blk],
        out_specs=(blk, blk, blk, blk), out_shape=(sds, sds, sds, sds),
        compiler_params=_params(40),
    )(landed, own, w, m, v)


def _sum_small(gathered, name):
    _, r, c = gathered.shape

    def body(g_ref, o_ref):
        acc = g_ref[0]
        for j in range(1, NDEV):
            acc = acc + g_ref[j]
        o_ref[...] = acc

    return pl.pallas_call(body, name=name, out_shape=jax.ShapeDtypeStruct((r, c), F32))(gathered)


def _adamw_small(g, w, m, v, name):
    def body(g_ref, w_ref, m_ref, v_ref, d_out, m_out, v_out):
        d_out[...], m_out[...], v_out[...] = _adamw(w_ref[...], g_ref[...], m_ref[...], v_ref[...])

    sds = jax.ShapeDtypeStruct(g.shape, F32)
    return pl.pallas_call(body, name=name, out_shape=(sds, sds, sds))(g, w, m, v)


def _cols_from_stack(s):
    return jnp.transpose(s, (1, 0, 2)).reshape(s.shape[1], NDEV * s.shape[2])


def _cols_to_stack(w):
    r, c = w.shape
    return jnp.transpose(w.reshape(r, NDEV, c // NDEV), (1, 0, 2))


def _rows_from_stack(s):
    return s.reshape(NDEV * s.shape[1], s.shape[2])


def _rows_to_stack(w):
    r, c = w.shape
    return w.reshape(NDEV, r // NDEV, c)


SMALL_ROWS = 16
SMALL_SLOTS = {
    "ln1_g": (0, 0, D), "ln1_b": (1, 0, D), "ln2_g": (2, 0, D), "ln2_b": (3, 0, D), "ln3_g": (4, 0, D),
    "ln3_b": (5, 0, D), "b_ple_gate": (6, 0, D), "ln4_g": (7, 0, D), "ln4_b": (8, 0, D),
    "g_attn": (9, 0, DA), "g_conv": (9, DA, DCV), "b_forget": (10, 0, NH),
}
CONVW_ROW = 11
LOSS_SLOT = (10, LANES)


def _pack_small(vals, conv_rows, loss=None):
    out = jnp.zeros((SMALL_ROWS, D), F32)
    for nm, (r, off, wd) in SMALL_SLOTS.items():
        out = out.at[r:r + 1, off:off + wd].set(vals[nm].reshape(1, wd).astype(F32))
    out = out.at[CONVW_ROW:CONVW_ROW + 3, 0:conv_rows.shape[1]].set(conv_rows.astype(F32))
    if loss is not None:
        out = out.at[LOSS_SLOT[0], LOSS_SLOT[1]].set(loss)
    return out


def _unpack_small(packed, name):
    r, off, wd = SMALL_SLOTS[name]
    return packed[r:r + 1, off:off + wd]


def kernel(x, p, ffn1_w_in, ffn1_w_out, ln1_g, ln1_b, w_mix_in, b_forget, conv_w, g_attn, g_conv, w_mix_out, ln2_g, ln2_b, ffn2_w_in, ffn2_w_out, ln3_g, ln3_b, w_ple, w_ple_gate, b_ple_gate, ln4_g, ln4_b, loss_target, m_ffn1_w_in, m_ffn1_w_out, m_ln1_g, m_ln1_b, m_w_mix_in, m_b_forget, m_conv_w, m_g_attn, m_g_conv, m_w_mix_out, m_ln2_g, m_ln2_b, m_ffn2_w_in, m_ffn2_w_out, m_ln3_g, m_ln3_b, m_w_ple, m_w_ple_gate, m_b_ple_gate, m_ln4_g, m_ln4_b, v_ffn1_w_in, v_ffn1_w_out, v_ln1_g, v_ln1_b, v_w_mix_in, v_b_forget, v_conv_w, v_g_attn, v_g_conv, v_w_mix_out, v_ln2_g, v_ln2_b, v_ffn2_w_in, v_ffn2_w_out, v_ln3_g, v_ln3_b, v_w_ple, v_w_ple_gate, v_b_ple_gate, v_ln4_g, v_ln4_b):
    args = dict(locals())
    t = x.shape[1]
    me = 4 * lax.axis_index("x") + 2 * lax.axis_index("y") + lax.axis_index("c")
    x0 = x.reshape(t, D)
    p0 = p.reshape(t, PLE)
    tgt = loss_target.reshape(t, D)

    big = ["ffn1_w_in", "ffn1_w_out", "w_mix_in", "w_mix_out", "ffn2_w_in", "ffn2_w_out", "w_ple", "w_ple_gate"]
    col_sharded = {"ffn1_w_in", "w_mix_in", "ffn2_w_in", "w_ple"}
    shard = {nm: args[nm][0] for nm in big}

    unstack = lambda nm, g: (_cols_from_stack(g) if nm in col_sharded else _rows_from_stack(g)).astype(MXU_DT)
    stack = lambda nm, g: _cols_to_stack(g) if nm in col_sharded else _rows_to_stack(g)
    wire = lambda names: [shard[nm].astype(WIRE_DT) for nm in names]
    first, later = big[:2], big[2:]

    full = {nm: unstack(nm, g) for nm, g in zip(first, _allgather(wire(first), "ag_ffn1"))}
    ffn1_out, gathered = _ffn1_fwd(x0, full, wire(later) + [conv_w[0]])
    full.update({nm: unstack(nm, g) for nm, g in zip(later, gathered)})
    cw = _cols_from_stack(gathered[len(later)])

    dr1, gw, small, loss_part = _mid_step(p0, tgt, full, cw, {nm: args[nm] for nm in SMALL_SLOTS}, ffn1_out)
    small_part = _pack_small({nm: small[nm] for nm in SMALL_SLOTS},
                             jnp.pad(small["conv_w"], ((0, 0), (0, D - DCV))), loss_part)
    stacks = {nm: stack(nm, gw[nm]) for nm in later}
    beside_bwd, beside_dw = later[1:], later[:1]
    gx, gw1, landed_bwd, landed_dw, landed_w_in = _ffn1_bwd(
        x0, dr1, ffn1_out, full, [stacks[nm].astype(WIRE_DT) for nm in beside_bwd],
        [stacks[nm].astype(WIRE_DT) for nm in beside_dw], lambda g: stack("ffn1_w_in", g).astype(WIRE_DT))
    stacks.update({nm: stack(nm, gw1[nm]) for nm in first})
    (landed_w_out,), (small_all,) = _exchange_and_gather([stacks["ffn1_w_out"].astype(WIRE_DT)], [small_part],
                                                         "rs_ffn1_out")
    landed = dict(zip(beside_bwd + beside_dw, list(landed_bwd) + list(landed_dw)),
                  ffn1_w_in=landed_w_in, ffn1_w_out=landed_w_out)
    small_g = _sum_small(small_all, "sum_small")
    loss = small_g[LOSS_SLOT[0], LOSS_SLOT[1]]

    outs = {"loss": loss, "grad_x": gx.reshape(1, t, D)}
    for nm in big:
        own = lax.dynamic_index_in_dim(stacks[nm], me, axis=0, keepdims=False)
        g, dl, mn, vn = _reduce_adamw(landed[nm], own, shard[nm], args["m_" + nm][0], args["v_" + nm][0],
                                      "adamw_" + nm)
        outs["grad_" + nm], outs["delta_" + nm], outs["new_m_" + nm], outs["new_v_" + nm] = (
            g[None], dl[None], mn[None], vn[None])
    small_names = list(SMALL_SLOTS)
    cshard = lax.dynamic_slice_in_dim(small_g[CONVW_ROW:CONVW_ROW + 3, 0:DCV], me * (DCV // NDEV), DCV // NDEV, axis=1)
    g_pack = _pack_small({nm: _unpack_small(small_g, nm) for nm in small_names}, cshard)
    packs = [_pack_small({nm: args[pre + nm] for nm in small_names}, args[pre + "conv_w"][0])
             for pre in ("", "m_", "v_")]
    d_pack, m_pack, v_pack = _adamw_small(g_pack, packs[0], packs[1], packs[2], "adamw_small")
    for key, pk in (("grad_", g_pack), ("delta_", d_pack), ("new_m_", m_pack), ("new_v_", v_pack)):
        for nm in small_names:
            outs[key + nm] = _unpack_small(pk, nm)
        outs[key + "conv_w"] = pk[CONVW_ROW:CONVW_ROW + 3, 0:DCV // NDEV][None]

    wnames = ["ffn1_w_in", "ffn1_w_out", "ln1_g", "ln1_b", "w_mix_in", "b_forget", "conv_w", "g_attn", "g_conv",
              "w_mix_out", "ln2_g", "ln2_b", "ffn2_w_in", "ffn2_w_out", "ln3_g", "ln3_b", "w_ple", "w_ple_gate",
              "b_ple_gate", "ln4_g", "ln4_b"]
    return (outs["loss"], outs["grad_x"], *[outs[pre + nm] for pre in ("grad_", "delta_", "new_m_", "new_v_")
                                            for nm in wnames])


def _ffn1_fwd(x0, full, gather=()):
    res = _ffn_fwd(x0, jnp.ones((1, D), F32), jnp.zeros((1, D), F32), full["ffn1_w_in"], full["ffn1_w_out"],
                   "ffn1_fwd", gather)
    return res[:4], res[4:]


def _ffn1_bwd(x0, dr1, ffn1_out, full, exchange=(), exchange_late=(), w_in_slots=None):
    g1a, u1a, _, rs1 = ffn1_out
    ones, zeros = jnp.ones((1, D), F32), jnp.zeros((1, D), F32)
    res = _ffn_bwd(dr1, g1a, u1a, x0, rs1, ones, full["ffn1_w_in"], full["ffn1_w_out"], False, "ffn1_bwd",
                   exchange)
    df1, dg1, du1, gx = res[:4]
    dw_g = _dw("affine", (x0, ones, zeros), dg1, D, F, "dw_ffn1_in_g", tn=F // 2, exchange=exchange_late)
    dw_g, landed_late = (dw_g[0], dw_g[1:]) if exchange_late else (dw_g, ())
    gw_in = jnp.concatenate([dw_g, _dw("affine", (x0, ones, zeros), du1, D, F, "dw_ffn1_in_u", tn=F // 2)], axis=1)
    side = () if w_in_slots is None else (w_in_slots(gw_in),)
    out = _dw("swiglu", (g1a, u1a), df1, F, D, "dw_ffn1_out", tmm=F // 2, exchange=side)
    gw_out, landed_in = (out, None) if w_in_slots is None else (out[0], out[1])
    return gx, {"ffn1_w_in": gw_in, "ffn1_w_out": gw_out}, res[5:], landed_late, landed_in


def _mid_step(p0, tgt, full, cw, sp, ffn1_out):
    g1a, u1a, xh1, rs1 = ffn1_out
    t = xh1.shape[0]
    ln1_g, ln1_b, ln2_g, ln2_b, ln3_g, ln3_b = (sp[k] for k in ("ln1_g", "ln1_b", "ln2_g", "ln2_b", "ln3_g", "ln3_b"))
    ln4_g, ln4_b, g_attn, g_conv, b_ple_gate = (sp[k] for k in ("ln4_g", "ln4_b", "g_attn", "g_conv", "b_ple_gate"))
    wmi = full["w_mix_in"]
    w_qkv = wmi[:, 0:3 * DA]
    w_f = jnp.pad(wmi[:, 3 * DA:3 * DA + NH], ((0, 0), (0, LANES - NH)))
    w_bch = wmi[:, 3 * DA + NH:]
    bf_pad = jnp.pad(sp["b_forget"], ((0, 0), (0, LANES - NH)))

    ka, va, qat, kat, vta, bch, z, rt = _mix_proj_fwd(xh1, ln1_g, ln1_b, w_qkv[:, DA:], jnp.transpose(w_qkv),
                                                      w_bch, w_f, bf_pad, "mix_proj_fwd")
    rtile = jnp.transpose(rt[:, 0, 0:NH])
    o, lse = _attn_fwd(qat, ka, vta, rtile, "attn_fwd")
    merged, xh2, rs2 = _mix_post_fwd(o, bch, cw, g_attn, g_conv, xh1, ln1_g, ln1_b, full["w_mix_out"],
                                     "mix_post_fwd")
    g2a, u2a, xh3, rs3 = _ffn_fwd(xh2, ln2_g, ln2_b, full["ffn2_w_in"], full["ffn2_w_out"], "ffn2_fwd")

    dr3, dz, de, st_tail = _tail(xh3, rs3, ln3_g, ln3_b, p0, full["w_ple_gate"], full["w_ple"], b_ple_gate,
                                 ln4_g, ln4_b, tgt, "tail")
    df2, dg2, du2, dr2, st_f2 = _ffn_bwd(dr3, g2a, u2a, xh2, rs2, ln2_g, full["ffn2_w_in"], full["ffn2_w_out"],
                                         True, "ffn2_bwd")
    dmix, dot, drow, dyc, st_post = _mix_post_bwd(dr2, o, bch, cw, g_attn, g_conv, full["w_mix_out"],
                                                  "mix_post_bwd")
    dbch, st_conv = _conv_bwd(dyc, bch, cw, "conv_bwd")
    dk, dv, dck, dqt, dcq = _attn_bwd(ka, kat, va, qat, dot, lse, drow, rtile, "attn_bwd")
    dc_pad = jnp.pad(jnp.transpose((dcq + dck).reshape(NH, t)), ((0, 0), (0, LANES - NH)))
    dr1, dfl, dq, st_proj = _mix_proj_bwd(dr2, dqt, dk, dv, dbch, dc_pad, z, xh1, rs1, ln1_g, w_qkv, w_bch, w_f,
                                          "mix_proj_bwd")

    x1p, x2p, x3p = (xh1, ln1_g, ln1_b), (xh2, ln2_g, ln2_b), (xh3, ln3_g, ln3_b)
    gw = {}
    gw["ffn2_w_in"] = jnp.concatenate(
        [_dw("affine", x2p, dg2, D, F, "dw_ffn2_in_g", tn=F // 2),
         _dw("affine", x2p, du2, D, F, "dw_ffn2_in_u", tn=F // 2)], axis=1)
    gw["ffn2_w_out"] = _dw("swiglu", (g2a, u2a), df2, F, D, "dw_ffn2_out", tmm=F // 2)
    gw["w_mix_out"] = _dw("plain", (merged,), dmix, D, D, "dw_mix_out")
    gq, gk, gv, gf, gbch = _dw_shared(x1p, (dq, dk, dv, dfl, dbch), "dw_mix_in")
    gw["w_mix_in"] = jnp.concatenate([gq, gk, gv, gf[:, 0:NH], gbch], axis=1)
    gw["w_ple_gate"] = _dw("affine", x3p, dz, D, D, "dw_ple_gate")
    gw["w_ple"] = _dw("plain", (p0,), de, PLE, D, "dw_ple")

    loss_part = (0.5 / D) * jnp.sum(st_tail[5:6, :])
    small = {"ln1_g": st_proj[0:1], "ln1_b": st_proj[1:2], "ln2_g": st_f2[0:1], "ln2_b": st_f2[1:2],
             "ln3_g": st_tail[3:4], "ln3_b": st_tail[4:5], "b_ple_gate": st_tail[2:3], "ln4_g": st_tail[0:1],
             "ln4_b": st_tail[1:2], "g_attn": st_post[0:1], "g_conv": st_post[1:2],
             "b_forget": st_proj[2:3, 0:NH], "conv_w": st_conv[0:3]}
    return dr1, gw, small, loss_part
```

```python
import functools

import jax
import jax.numpy as jnp
from jax import lax
from jax.experimental import pallas as pl
from jax.experimental.pallas import tpu as pltpu

D = 1024
F = 2816
NH = 8
DH = 64
DA = NH * DH
DCV = D - DA
PLE = 256
LN_EPS = 1e-5
RMS_EPS = 1e-6
NEG = -1e30
ALPHA = 2.0 ** 0.25
NDEV = 8
LANES = 128

ADAM_LR, ADAM_B1, ADAM_B2, ADAM_EPS, ADAM_WD, ADAM_STEP = 0.001, 0.9, 0.999, 1e-08, 0.01, 10

F32 = jnp.float32
MXU_DT = jnp.bfloat16
WIRE_DT = jnp.bfloat16

MESH_ID = pl.DeviceIdType.MESH
ANY = pl.BlockSpec(memory_space=pl.ANY)


def _params(vmem_mb, n_axes=1):
    return pltpu.CompilerParams(dimension_semantics=("arbitrary",) * n_axes,
                                vmem_limit_bytes=int(vmem_mb) << 20)


def _mm(a, b):
    return jnp.dot(a, b, preferred_element_type=F32)


def _mm_nt(a, b):
    return lax.dot_general(a, b, (((1,), (1,)), ((), ())), preferred_element_type=F32)


def _mm_tn(a, b):
    return lax.dot_general(a, b, (((0,), (0,)), ((), ())), preferred_element_type=F32)


def _split3(x):
    hi = x.astype(MXU_DT)
    r1 = x - hi.astype(F32)
    mid = r1.astype(MXU_DT)
    lo = (r1 - mid.astype(F32)).astype(MXU_DT)
    return hi, mid, lo


def _mm_sel(sel, x):
    hi, mid, lo = _split3(x)
    return _mm(sel, hi) + _mm(sel, mid) + _mm(sel, lo)


def _sigmoid(x):
    return 1.0 / (1.0 + jnp.exp(-x))


def _ln_fwd(r):
    mu = jnp.mean(r, axis=-1, keepdims=True)
    xc = r - mu
    var = jnp.mean(xc * xc, axis=-1, keepdims=True)
    rstd = lax.rsqrt(var + LN_EPS)
    return xc * rstd, rstd


def _ln_bwd(dxhat, xhat, rstd):
    m1 = jnp.mean(dxhat, axis=-1, keepdims=True)
    m2 = jnp.mean(dxhat * xhat, axis=-1, keepdims=True)
    return rstd * (dxhat - m1 - xhat * m2)


def _rms_fwd(x):
    r = lax.rsqrt(jnp.mean(x * x, axis=-1, keepdims=True) + RMS_EPS)
    return x * r, r


def _rms_bwd(dyg, xn, r):
    return r * (dyg - xn * jnp.mean(dyg * xn, axis=-1, keepdims=True))


def _colsum(x):
    return jnp.sum(x, axis=0, keepdims=True)


def _f_chunks():
    out, c0 = [], 0
    while c0 < F:
        fc = min(512, F - c0)
        out.append((c0, fc))
        c0 += fc
    return out


def _tile(t, want):
    return want if t % want == 0 and t >= want else t


def _exchange_sems(n):
    return [pltpu.SemaphoreType.DMA((n * (NDEV - 1),)), pltpu.SemaphoreType.DMA((n * (NDEV - 1),))]


def _exchange_phases(ins, outs, send_sems, recv_sems):
    n = len(ins)

    def peers():
        x, y, c = lax.axis_index("x"), lax.axis_index("y"), lax.axis_index("c")
        out = []
        for k in range(1, NDEV):
            px = 1 - x if (k >> 2) & 1 else x
            py = 1 - y if (k >> 1) & 1 else y
            pc = 1 - c if k & 1 else c
            out.append(((px, py, pc), 4 * px + 2 * py + pc))
        return 4 * x + 2 * y + c, out

    def remote(w, k, to, slot_src, slot_dst):
        return pltpu.make_async_remote_copy(
            src_ref=ins[w].at[slot_src], dst_ref=outs[w].at[slot_dst],
            send_sem=send_sems.at[w * (NDEV - 1) + k], recv_sem=recv_sems.at[w * (NDEV - 1) + k],
            device_id=to, device_id_type=MESH_ID)

    def start():
        me, prs = peers()
        for k, (to, pid) in enumerate(prs):
            for w in range(n):
                remote(w, k, to, pid, me).start()

    def finish():
        me, prs = peers()
        for k, (to, pid) in enumerate(prs):
            for w in range(n):
                remote(w, k, to, me, pid).wait_recv()
        for k, (to, pid) in enumerate(prs):
            for w in range(n):
                remote(w, k, to, pid, me).wait_send()

    return start, finish


def _gather_sems(n):
    return [pltpu.SemaphoreType.DMA((n * (NDEV - 1),)), pltpu.SemaphoreType.DMA((n * (NDEV - 1),)),
            pltpu.SemaphoreType.DMA((n,))]


def _gather_phases(ins, outs, send_sems, recv_sems, loc_sems):
    n = len(ins)
    per = NDEV - 1

    def place():
        x, y, c = lax.axis_index("x"), lax.axis_index("y"), lax.axis_index("c")
        return (x, y, c), (x, y, 1 - c), [(1 - x, y), (x, 1 - y), (1 - x, 1 - y)]

    def copy(w, k, block, to, src=None):
        dst = outs[w].at[4 * block[0] + 2 * block[1] + block[2]]
        return pltpu.make_async_remote_copy(
            src_ref=dst if src is None else src, dst_ref=dst,
            send_sem=send_sems.at[w * per + k], recv_sem=recv_sems.at[w * per + k],
            device_id=to, device_id_type=MESH_ID)

    def local(w, me):
        return pltpu.make_async_copy(ins[w], outs[w].at[4 * me[0] + 2 * me[1] + me[2]], loc_sems.at[w])

    def first(me, sib, chips):
        out = []
        for j, chip in enumerate(chips):
            out += [copy(w, 1 + j, me, (*chip, me[2]), src=ins[w]) for w in range(n)]
        return out + [copy(w, 0, me, sib, src=ins[w]) for w in range(n)]

    def start():
        me, sib, chips = place()
        for w in range(n):
            local(w, me).start()
        for cp in first(me, sib, chips):
            cp.start()

    def forward():
        me, sib, chips = place()
        for j, chip in enumerate(chips):
            for w in range(n):
                copy(w, 1 + j, (*chip, me[2]), me).wait_recv()
                copy(w, 4 + j, (*chip, me[2]), sib).start()

    def finish():
        me, sib, chips = place()
        for w in range(n):
            copy(w, 0, sib, me).wait_recv()
        for j, chip in enumerate(chips):
            for w in range(n):
                copy(w, 4 + j, (*chip, 1 - me[2]), me).wait_recv()
        for cp in first(me, sib, chips):
            cp.wait_send()
        for j, chip in enumerate(chips):
            for w in range(n):
                copy(w, 4 + j, (*chip, me[2]), sib).wait_send()
        for w in range(n):
            local(w, me).wait()

    return start, forward, finish


def _allgather(arrs, name):
    n = len(arrs)

    def body(*refs):
        start, forward, finish = _gather_phases(refs[:n], refs[n:2 * n], *refs[2 * n:])
        start()
        forward()
        finish()

    return pl.pallas_call(
        body, name=name, out_shape=tuple(jax.ShapeDtypeStruct((NDEV,) + a.shape, a.dtype) for a in arrs),
        in_specs=[ANY] * n, out_specs=tuple([ANY] * n), scratch_shapes=_gather_sems(n),
    )(*arrs)


def _exchange_and_gather(ex, ga, name):
    ne, ng = len(ex), len(ga)

    def body(*refs):
        ins, outs, sems = refs[:ne + ng], refs[ne + ng:2 * (ne + ng)], refs[2 * (ne + ng):]
        e_start, e_finish = _exchange_phases(ins[:ne], outs[:ne], *sems[:2])
        g_start, g_forward, g_finish = _gather_phases(ins[ne:], outs[ne:], *sems[2:])
        e_start()
        g_start()
        g_forward()
        g_finish()
        e_finish()

    res = pl.pallas_call(
        body, name=name,
        out_shape=tuple(jax.ShapeDtypeStruct(a.shape, a.dtype) for a in ex)
        + tuple(jax.ShapeDtypeStruct((NDEV,) + a.shape, a.dtype) for a in ga),
        in_specs=[ANY] * (ne + ng), out_specs=tuple([ANY] * (ne + ng)),
        scratch_shapes=_exchange_sems(ne) + _gather_sems(ng),
    )(*ex, *ga)
    return res[:ne], res[ne:]


def _ffn_fwd(xin, gin, bin_, w_in, w_out, name, gather=()):
    t = xin.shape[0]
    tm = _tile(t, 512)
    nt = t // tm
    chunks = _f_chunks()
    ng = len(gather)

    def body(*refs):
        x_ref, gi_ref, bi_ref, win_hbm, wout_hbm = refs[:5]
        g_ref, u_ref, xh_ref, rs_ref = refs[5 + ng:9 + ng]
        win_v, wout_v, acc_ref = refs[9 + 2 * ng:12 + 2 * ng]
        if ng:
            g_start, g_forward, g_finish = _gather_phases(refs[5:5 + ng], refs[9 + ng:9 + 2 * ng],
                                                          *refs[12 + 2 * ng:])

        @pl.when(pl.program_id(0) == 0)
        def _():
            if ng:
                g_start()
            pltpu.sync_copy(win_hbm, win_v)
            pltpu.sync_copy(wout_hbm, wout_v)

        if ng:
            pl.when(pl.program_id(0) == nt // 2)(g_forward)
            pl.when(pl.program_id(0) == nt - 1)(g_finish)

        x = x_ref[...] * gi_ref[...] + bi_ref[...]
        xb = x.astype(MXU_DT)
        for ci, (c0, fc) in enumerate(chunks):
            gc = _mm(xb, win_v[:, c0:c0 + fc])
            uc = _mm(xb, win_v[:, F + c0:F + c0 + fc])
            g_ref[:, c0:c0 + fc] = gc.astype(g_ref.dtype)
            u_ref[:, c0:c0 + fc] = uc.astype(u_ref.dtype)
            hc = (gc * _sigmoid(gc) * uc).astype(MXU_DT)
            part = _mm(hc, wout_v[c0:c0 + fc, :])
            if ci == 0:
                acc_ref[...] = part
            else:
                acc_ref[...] += part
        xh, rstd = _ln_fwd(ALPHA * x + 0.5 * acc_ref[...])
        xh_ref[...] = xh
        rs_ref[...] = rstd

    row = pl.BlockSpec((tm, D), lambda i: (i, 0))
    vec = pl.BlockSpec((1, D), lambda i: (0, 0))
    act = pl.BlockSpec((tm, F), lambda i: (i, 0))
    return pl.pallas_call(
        body, name=name, grid=(nt,),
        in_specs=[row, vec, vec, ANY, ANY] + [ANY] * ng,
        out_specs=(act, act, row, pl.BlockSpec((tm, 1), lambda i: (i, 0))) + (ANY,) * ng,
        out_shape=(jax.ShapeDtypeStruct((t, F), MXU_DT), jax.ShapeDtypeStruct((t, F), MXU_DT),
                   jax.ShapeDtypeStruct((t, D), F32), jax.ShapeDtypeStruct((t, 1), F32))
        + tuple(jax.ShapeDtypeStruct((NDEV,) + a.shape, a.dtype) for a in gather),
        scratch_shapes=[pltpu.VMEM((D, 2 * F), MXU_DT), pltpu.VMEM((F, D), MXU_DT), pltpu.VMEM((tm, D), F32)]
        + (_gather_sems(ng) if ng else []),
        compiler_params=_params(52),
    )(xin, gin, bin_, w_in, w_out, *gather)


QROWS = 80
BIAS_AT = DH


def _place_matrices():
    import numpy as np
    pk = np.zeros((NH, LANES, LANES), np.float32)
    pqt = np.zeros((NH, LANES, LANES), np.float32)
    for h in range(NH):
        for piece in range(3):
            pk[h, 8 * piece + h, BIAS_AT + 3 + piece] = -1.0
            pqt[h, BIAS_AT + piece, 8 * piece + h] = 1.0
    pkt = np.transpose(pk, (0, 2, 1))
    return tuple(jnp.asarray(m, MXU_DT) for m in (pk, pqt, pkt))


def _mix_proj_fwd(xh1, g1, b1, w_kv, w_qkv_t, w_bch, w_f, bf_pad, name):
    t = xh1.shape[0]
    tm = _tile(t, 512)
    pk, pqt, pkt = _place_matrices()

    def body(x_ref, g_ref, b_ref, wkv_ref, wt_ref, wb_ref, wf_ref, bf_ref, pk_ref, pqt_ref, pkt_ref,
             ka_ref, va_ref, qat_ref, kat_ref, vta_ref, bch_ref, z_ref, r_ref, carry):
        @pl.when(pl.program_id(0) == 0)
        def _():
            carry[...] = jnp.zeros_like(carry)

        xb = (x_ref[...] * g_ref[...] + b_ref[...]).astype(MXU_DT)
        kv = _mm(xb, wkv_ref[...])
        qkvt = _mm_nt(wt_ref[...], xb)
        bch_ref[...] = _mm(xb, wb_ref[...])
        z = _mm(xb, wf_ref[...]) + bf_ref[...]
        z_ref[...] = z
        logf = jnp.minimum(z, 0.0) - jnp.log(1.0 + jnp.exp(-jnp.abs(z)))
        row = lax.broadcasted_iota(jnp.int32, (tm, tm), 0)
        col = lax.broadcasted_iota(jnp.int32, (tm, tm), 1)
        tri = jnp.where(row >= col, 1.0, 0.0).astype(MXU_DT)
        c = carry[...] + _mm_sel(tri, logf)
        carry[...] = c[tm - 1:tm, :]
        r_ref[0] = c[0:1, :]
        lane = lax.broadcasted_iota(jnp.int32, (1, LANES), 1)
        hi, mid, lo = _split3(jnp.where(lane < NH, c - c[0:1, :], 0.0))
        pieces = (hi.astype(F32) + pltpu.roll(mid.astype(F32), 8, 1) + pltpu.roll(lo.astype(F32), 16, 1)
                  ).astype(MXU_DT)
        sub = lax.broadcasted_iota(jnp.int32, (DH, 1), 0)
        ones_k_lanes = jnp.where((lane >= BIAS_AT) & (lane < BIAS_AT + 3), 1.0, 0.0)
        ones_q_rows = jnp.where((sub >= 3) & (sub < 6), 1.0, 0.0)
        ones_k_rows = jnp.where(sub[0:QROWS - DH] < 3, 1.0, 0.0)
        first_row = jnp.where(sub == 0, 1.0, 0.0) + jnp.zeros((DH, tm), F32)
        for h in range(NH):
            pair, odd = divmod(h, 2)
            k2 = kv[:, LANES * pair:LANES * (pair + 1)]
            v2 = kv[:, DA + LANES * pair:DA + LANES * (pair + 1)]
            if odd:
                k2, v2 = pltpu.roll(k2, DH, 1), pltpu.roll(v2, DH, 1)
            ka_ref[h] = jnp.where(lane < DH, k2, _mm(pieces, pk_ref[h]) + ones_k_lanes).astype(ka_ref.dtype)
            va_ref[h] = jnp.where(lane < DH, v2, 0.0).astype(va_ref.dtype)
            qat_ref[h, 0:DH, :] = (qkvt[DH * h:DH * (h + 1)] * 0.125).astype(qat_ref.dtype)
            qat_ref[h, DH:LANES, :] = (_mm_nt(pqt_ref[h], pieces)[DH:LANES] + ones_q_rows).astype(qat_ref.dtype)
            kat_ref[h, 0:DH, :] = qkvt[DA + DH * h:DA + DH * (h + 1)].astype(kat_ref.dtype)
            kat_ref[h, DH:QROWS, :] = (_mm_nt(pkt_ref[h], pieces)[DH:QROWS] + ones_k_rows).astype(kat_ref.dtype)
            vt = qkvt[2 * DA + DH * h:2 * DA + DH * (h + 1)]
            vta_ref[h, 0:DH, :] = (first_row if odd else vt).astype(vta_ref.dtype)
            vta_ref[h, DH:LANES, :] = (vt if odd else first_row).astype(vta_ref.dtype)

    row = lambda w: pl.BlockSpec((tm, w), lambda i: (i, 0))
    full = lambda a: pl.BlockSpec(a.shape, lambda i: (0,) * a.ndim)
    nat = pl.BlockSpec((NH, tm, LANES), lambda i: (0, i, 0))
    fmaj = lambda rows: pl.BlockSpec((NH, rows, tm), lambda i: (0, 0, i))
    return pl.pallas_call(
        body, name=name, grid=(t // tm,),
        in_specs=[row(D), full(g1), full(b1), full(w_kv), full(w_qkv_t), full(w_bch), full(w_f), full(bf_pad),
                  full(pk), full(pqt), full(pkt)],
        out_specs=(nat, nat, fmaj(LANES), fmaj(QROWS), fmaj(LANES), row(3 * DCV), row(LANES),
                   pl.BlockSpec((1, 1, LANES), lambda i: (i, 0, 0))),
        out_shape=(jax.ShapeDtypeStruct((NH, t, LANES), MXU_DT), jax.ShapeDtypeStruct((NH, t, LANES), MXU_DT),
                   jax.ShapeDtypeStruct((NH, LANES, t), MXU_DT), jax.ShapeDtypeStruct((NH, QROWS, t), MXU_DT),
                   jax.ShapeDtypeStruct((NH, LANES, t), MXU_DT), jax.ShapeDtypeStruct((t, 3 * DCV), F32),
                   jax.ShapeDtypeStruct((t, LANES), F32), jax.ShapeDtypeStruct((t // tm, 1, LANES), F32)),
        scratch_shapes=[pltpu.VMEM((1, LANES), F32)],
        compiler_params=_params(56),
    )(xh1, g1, b1, w_kv, w_qkv_t, w_bch, w_f, bf_pad, pk, pqt, pkt)


def _attn_fwd(qat, ka, vta, r, name):
    t = ka.shape[1]
    tq = _tile(t, 512)
    nq = t // tq

    def body(r_ref, q_ref, k_ref, v_ref, o_ref, l_ref, st0, st1):
        hp, i = pl.program_id(0), pl.program_id(1)
        key = lax.broadcasted_iota(jnp.int32, (tq, tq), 0)
        qry = lax.broadcasted_iota(jnp.int32, (tq, tq), 1)

        def tile_of(pos):
            return jnp.where(pos == 0, i, pos - 1)

        def scores(pos, buf, masked):
            off = pl.multiple_of(tile_of(pos) * tq, tq)
            for a in range(2):
                st = _mm(k_ref[a, pl.ds(off, tq), :], q_ref[a])
                buf[a] = jnp.where(qry >= key, st, NEG) if masked else st

        def consume(pos, buf, carry):
            j = tile_of(pos)
            off = pl.multiple_of(j * tq, tq)
            out = []
            for a in range(2):
                m, acc = carry[a]
                st = buf[a]
                d = r_ref[2 * hp + a, i] - r_ref[2 * hp + a, j]
                m_new = jnp.maximum(m, jnp.max(st, axis=0, keepdims=True) + d)
                pt = jnp.exp(st - (m_new - d))
                acc = jnp.exp(m - m_new) * acc + _mm(v_ref[a, :, pl.ds(off, tq)], pt.astype(MXU_DT))
                out.append((m_new, acc))
            return tuple(out)

        def trip(p, carry):
            scores(2 * p + 1, st1, False)
            carry = consume(2 * p, st0, carry)
            scores(2 * p + 2, st0, False)
            return consume(2 * p + 1, st1, carry)

        scores(0, st0, True)
        init = tuple((jnp.full((1, tq), NEG, F32), jnp.zeros((LANES, tq), F32)) for _ in range(2))
        trips = i // 2
        carry = lax.fori_loop(0, trips, trip, init)

        def last_two(cr):
            scores(2 * trips + 1, st1, False)
            return consume(2 * trips + 1, st1, consume(2 * trips, st0, cr))

        (ma, acca), (mb, accb) = lax.cond(i % 2 == 1, last_two, lambda cr: consume(2 * trips, st0, cr), carry)
        la, lb = acca[DH:DH + 1, :], accb[0:1, :]
        l_ref[0] = ma + jnp.log(la)
        l_ref[1] = mb + jnp.log(lb)
        sub = lax.broadcasted_iota(jnp.int32, (LANES, tq), 0)
        o_ref[...] = jnp.where(sub < DH, acca / la, accb / lb).T

    return pl.pallas_call(
        body, name=name, grid=(NH // 2, nq),
        in_specs=[pl.BlockSpec(memory_space=pltpu.SMEM),
                  pl.BlockSpec((2, LANES, tq), lambda p, i: (p, 0, i)),
                  pl.BlockSpec((2, t, LANES), lambda p, i: (p, 0, 0)),
                  pl.BlockSpec((2, LANES, t), lambda p, i: (p, 0, 0))],
        out_specs=(pl.BlockSpec((tq, LANES), lambda p, i: (i, p)),
                   pl.BlockSpec((2, 1, tq), lambda p, i: (p, 0, i))),
        out_shape=(jax.ShapeDtypeStruct((t, DA), F32), jax.ShapeDtypeStruct((NH, 1, t), F32)),
        scratch_shapes=[pltpu.VMEM((2, tq, tq), F32), pltpu.VMEM((2, tq, tq), F32)],
        compiler_params=_params(48, 2),
    )(r, qat, ka, vta)


def _conv_parts(bch):
    return bch[:, 0:DCV], bch[:, DCV:2 * DCV], bch[:, 2 * DCV:3 * DCV]


def _mix_post_fwd(o, bch, conv_w, g_attn, g_conv, xh1, g1, b1, w_mo, name):
    t = o.shape[0]
    tm = _tile(t, 512)
    hb = tm // 8

    def body(o_ref, bch_ref, halo_ref, cw_ref, ga_ref, gc_ref, x_ref, g_ref, b_ref, w_ref,
             mg_ref, xh_ref, rs_ref, ext):
        i = pl.program_id(0)
        an, _ = _rms_fwd(o_ref[...])
        mg_ref[:, 0:DA] = (an * ga_ref[...]).astype(mg_ref.dtype)
        bb, cc, hh = _conv_parts(bch_ref[...])
        _, hc, hh_h = _conv_parts(halo_ref[...])
        u = cc * hh
        ext[0:8, :] = jnp.where(i > 0, hc * hh_h, 0.0)
        ext[8:8 + tm, :] = u
        raw = cw_ref[0:1, :] * ext[6:6 + tm, :] + cw_ref[1:2, :] * ext[7:7 + tm, :] + cw_ref[2:3, :] * u
        cn, _ = _rms_fwd(bb * raw)
        mg_ref[:, DA:D] = (cn * gc_ref[...]).astype(mg_ref.dtype)
        x1 = x_ref[...] * g_ref[...] + b_ref[...]
        xh, rstd = _ln_fwd(ALPHA * x1 + _mm(mg_ref[...], w_ref[...]))
        xh_ref[...] = xh
        rs_ref[...] = rstd

    row = lambda w: pl.BlockSpec((tm, w), lambda i: (i, 0))
    full = lambda a: pl.BlockSpec(a.shape, lambda i: (0, 0))
    return pl.pallas_call(
        body, name=name, grid=(t // tm,),
        in_specs=[row(DA), row(3 * DCV),
                  pl.BlockSpec((8, 3 * DCV), lambda i: (jnp.maximum(i * hb - 1, 0), 0)),
                  full(conv_w), full(g_attn), full(g_conv), row(D), full(g1), full(b1), full(w_mo)],
        out_specs=(row(D), row(D), pl.BlockSpec((tm, 1), lambda i: (i, 0))),
        out_shape=(jax.ShapeDtypeStruct((t, D), MXU_DT), jax.ShapeDtypeStruct((t, D), F32),
                   jax.ShapeDtypeStruct((t, 1), F32)),
        scratch_shapes=[pltpu.VMEM((tm + 8, DCV), F32)],
        compiler_params=_params(48),
    )(o, bch, bch, conv_w, g_attn, g_conv, xh1, g1, b1, w_mo)


def _tail(xh3, rs3, g3, b3, p, w_g, w_ple, bg, g4, b4, target, name):
    t = xh3.shape[0]
    tm = _tile(t, 512)

    def body(x_ref, rs_ref, g3_ref, b3_ref, p_ref, wg_ref, wp_ref, bg_ref, g4_ref, b4_ref, t_ref,
             dr_ref, dz_ref, de_ref, st_ref):
        @pl.when(pl.program_id(0) == 0)
        def _():
            st_ref[...] = jnp.zeros_like(st_ref)

        xh3v = x_ref[...]
        x3 = xh3v * g3_ref[...] + b3_ref[...]
        gate = _sigmoid(_mm(x3.astype(MXU_DT), wg_ref[...]) + bg_ref[...])
        e = _mm(p_ref[...].astype(MXU_DT), wp_ref[...])
        xh4, rstd4 = _ln_fwd(ALPHA * x3 + gate * e)
        diff = xh4 * g4_ref[...] + b4_ref[...] - t_ref[...]
        dy = diff * (1.0 / D)
        st_ref[5:6, :] += _colsum(diff * diff)
        st_ref[0:1, :] += _colsum(dy * xh4)
        st_ref[1:2, :] += _colsum(dy)
        dr4 = _ln_bwd(dy * g4_ref[...], xh4, rstd4)
        de_ref[...] = (dr4 * gate).astype(de_ref.dtype)
        dz = dr4 * e * gate * (1.0 - gate)
        st_ref[2:3, :] += _colsum(dz)
        dzb = dz.astype(MXU_DT)
        dz_ref[...] = dzb
        dx3 = ALPHA * dr4 + _mm_nt(dzb, wg_ref[...])
        st_ref[3:4, :] += _colsum(dx3 * xh3v)
        st_ref[4:5, :] += _colsum(dx3)
        dr_ref[...] = _ln_bwd(dx3 * g3_ref[...], xh3v, rs_ref[...])

    row = lambda w: pl.BlockSpec((tm, w), lambda i: (i, 0))
    full = lambda a: pl.BlockSpec(a.shape, lambda i: (0, 0))
    return pl.pallas_call(
        body, name=name, grid=(t // tm,),
        in_specs=[row(D), row(1), full(g3), full(b3), row(PLE), full(w_g), full(w_ple), full(bg), full(g4),
                  full(b4), row(D)],
        out_specs=(row(D), row(D), row(D), pl.BlockSpec((8, D), lambda i: (0, 0))),
        out_shape=(jax.ShapeDtypeStruct((t, D), F32), jax.ShapeDtypeStruct((t, D), MXU_DT),
                   jax.ShapeDtypeStruct((t, D), MXU_DT), jax.ShapeDtypeStruct((8, D), F32)),
        compiler_params=_params(48),
    )(xh3, rs3, g3, b3, p, w_g, w_ple, bg, g4, b4, target)


def _ffn_bwd(dr, gact, uact, xin, rsin, gin, w_in, w_out, prev_ln, name, exchange=()):
    t = dr.shape[0]
    tm = _tile(t, 512)
    nt = t // tm
    chunks = _f_chunks()
    ne = len(exchange)

    def body(*refs):
        dr_ref, g_ref, u_ref, x_ref, rs_ref, gi_ref, win_hbm, wout_hbm = refs[:8]
        df_ref, dg_ref, du_ref, dx_ref, st_ref = refs[8 + ne:13 + ne]
        win_v, wout_v = refs[13 + 2 * ne:15 + 2 * ne]
        acc_ref = dx_ref
        if ne:
            e_start, e_finish = _exchange_phases(refs[8:8 + ne], refs[13 + ne:13 + 2 * ne], *refs[15 + 2 * ne:])

        @pl.when(pl.program_id(0) == 0)
        def _():
            if ne:
                e_start()
            pltpu.sync_copy(win_hbm, win_v)
            pltpu.sync_copy(wout_hbm, wout_v)
            st_ref[...] = jnp.zeros_like(st_ref)

        if ne:
            pl.when(pl.program_id(0) == nt - 1)(e_finish)

        drv = dr_ref[...]
        dfb = (0.5 * drv).astype(MXU_DT)
        df_ref[...] = dfb
        for ci, (c0, fc) in enumerate(chunks):
            dh = _mm_nt(dfb, wout_v[c0:c0 + fc, :])
            g = g_ref[:, c0:c0 + fc].astype(F32)
            u = u_ref[:, c0:c0 + fc].astype(F32)
            sg = _sigmoid(g)
            dgb = (dh * u * (sg * (1.0 + g * (1.0 - sg)))).astype(MXU_DT)
            dub = (dh * (g * sg)).astype(MXU_DT)
            dg_ref[:, c0:c0 + fc] = dgb
            du_ref[:, c0:c0 + fc] = dub
            part = _mm_nt(dgb, win_v[:, c0:c0 + fc]) + _mm_nt(dub, win_v[:, F + c0:F + c0 + fc])
            if ci == 0:
                acc_ref[...] = part
            else:
                acc_ref[...] += part
        dx = ALPHA * drv + acc_ref[...]
        if prev_ln:
            xh = x_ref[...]
            st_ref[0:1, :] += _colsum(dx * xh)
            st_ref[1:2, :] += _colsum(dx)
            dx_ref[...] = _ln_bwd(dx * gi_ref[...], xh, rs_ref[...])
        else:
            dx_ref[...] = dx

    row = pl.BlockSpec((tm, D), lambda i: (i, 0))
    vec = pl.BlockSpec((1, D), lambda i: (0, 0))
    act = pl.BlockSpec((tm, F), lambda i: (i, 0))
    return pl.pallas_call(
        body, name=name, grid=(nt,),
        in_specs=[row, act, act, row, pl.BlockSpec((tm, 1), lambda i: (i, 0)), vec, ANY, ANY] + [ANY] * ne,
        out_specs=(row, act, act, row, pl.BlockSpec((8, D), lambda i: (0, 0))) + (ANY,) * ne,
        out_shape=(jax.ShapeDtypeStruct((t, D), MXU_DT), jax.ShapeDtypeStruct((t, F), MXU_DT),
                   jax.ShapeDtypeStruct((t, F), MXU_DT), jax.ShapeDtypeStruct((t, D), F32),
                   jax.ShapeDtypeStruct((8, D), F32))
        + tuple(jax.ShapeDtypeStruct(a.shape, a.dtype) for a in exchange),
        scratch_shapes=[pltpu.VMEM((D, 2 * F), MXU_DT), pltpu.VMEM((F, D), MXU_DT)]
        + (_exchange_sems(ne) if ne else []),
        compiler_params=_params(60),
    )(dr, gact, uact, xin, rsin, gin, w_in, w_out, *exchange)


def _mix_post_bwd(dr2, o, bch, conv_w, g_attn, g_conv, w_mo, name):
    t = dr2.shape[0]
    tm = _tile(t, 512)
    hb = tm // 8

    def body(dr_ref, o_ref, bch_ref, halo_ref, cw_ref, ga_ref, gc_ref, w_ref,
             dm_ref, do_ref, dl_ref, dy_ref, st_ref, ext):
        i = pl.program_id(0)

        @pl.when(i == 0)
        def _():
            st_ref[...] = jnp.zeros_like(st_ref)

        dmb = dr_ref[...].astype(MXU_DT)
        dm_ref[...] = dmb
        dmg = _mm_nt(dmb, w_ref[...])
        ov = o_ref[...]
        an, ra = _rms_fwd(ov)
        da = dmg[:, 0:DA]
        st_ref[0:1, :] += _colsum(da * an)
        dxa = _rms_bwd(da * ga_ref[...], an, ra)
        dor = dxa.astype(MXU_DT).astype(F32)
        dot = dor.T
        for h in range(NH):
            do_ref[h, 0:DH, :] = dot[DH * h:DH * (h + 1)].astype(do_ref.dtype)
            do_ref[h, DH:LANES, :] = jnp.zeros((LANES - DH, tm), do_ref.dtype)
        srow = lax.broadcasted_iota(jnp.int32, (8, DA), 0)
        scol = lax.broadcasted_iota(jnp.int32, (8, DA), 1)
        sel = jnp.where((scol // DH) == srow, 1.0, 0.0).astype(MXU_DT)
        hi, mid, lo = _split3(dor * ov)
        delta = _mm_nt(sel, hi) + _mm_nt(sel, mid) + _mm_nt(sel, lo)
        for h in range(NH):
            dl_ref[h] = delta[h:h + 1, :]
        bb, cc, hh = _conv_parts(bch_ref[...])
        _, hc, hh_h = _conv_parts(halo_ref[...])
        u = cc * hh
        ext[0:8, :] = jnp.where(i > 0, hc * hh_h, 0.0)
        ext[8:8 + tm, :] = u
        raw = cw_ref[0:1, :] * ext[6:6 + tm, :] + cw_ref[1:2, :] * ext[7:7 + tm, :] + cw_ref[2:3, :] * u
        cn, rc = _rms_fwd(bb * raw)
        dcn = dmg[:, DA:D]
        st_ref[1:2, :] += _colsum(dcn * cn)
        dy_ref[...] = _rms_bwd(dcn * gc_ref[...], cn, rc)

    row = lambda w: pl.BlockSpec((tm, w), lambda i: (i, 0))
    full = lambda a: pl.BlockSpec(a.shape, lambda i: (0, 0))
    return pl.pallas_call(
        body, name=name, grid=(t // tm,),
        in_specs=[row(D), row(DA), row(3 * DCV),
                  pl.BlockSpec((8, 3 * DCV), lambda i: (jnp.maximum(i * hb - 1, 0), 0)),
                  full(conv_w), full(g_attn), full(g_conv), full(w_mo)],
        out_specs=(row(D), pl.BlockSpec((NH, LANES, tm), lambda i: (0, 0, i)),
                   pl.BlockSpec((NH, 1, tm), lambda i: (0, 0, i)), row(DCV),
                   pl.BlockSpec((8, DA), lambda i: (0, 0))),
        out_shape=(jax.ShapeDtypeStruct((t, D), MXU_DT), jax.ShapeDtypeStruct((NH, LANES, t), MXU_DT),
                   jax.ShapeDtypeStruct((NH, 1, t), F32), jax.ShapeDtypeStruct((t, DCV), F32),
                   jax.ShapeDtypeStruct((8, DA), F32)),
        scratch_shapes=[pltpu.VMEM((tm + 8, DCV), F32)],
        compiler_params=_params(48),
    )(dr2, o, bch, bch, conv_w, g_attn, g_conv, w_mo)


def _conv_bwd(dy, bch, conv_w, name):
    t = dy.shape[0]
    tm = _tile(t, 512)
    hb = tm // 8
    nt = t // tm

    def body(dy_ref, dyn_ref, bch_ref, prev_ref, next_ref, cw_ref, out_ref, st_ref, ext_u, ext_d):
        i = pl.program_id(0)

        @pl.when(i == 0)
        def _():
            st_ref[...] = jnp.zeros_like(st_ref)

        bb, cc, hh = _conv_parts(bch_ref[...])
        _, pc, ph = _conv_parts(prev_ref[...])
        nb, _, _ = _conv_parts(next_ref[...])
        u = cc * hh
        ext_u[0:8, :] = jnp.where(i > 0, pc * ph, 0.0)
        ext_u[8:8 + tm, :] = u
        u1 = ext_u[7:7 + tm, :]
        u2 = ext_u[6:6 + tm, :]
        w0, w1, w2 = cw_ref[0:1, :], cw_ref[1:2, :], cw_ref[2:3, :]
        dyv = dy_ref[...]
        out_ref[:, 0:DCV] = (dyv * (w0 * u2 + w1 * u1 + w2 * u)).astype(out_ref.dtype)
        dcr = dyv * bb
        ext_d[0:tm, :] = dcr
        ext_d[tm:tm + 8, :] = jnp.where(i < nt - 1, dyn_ref[...] * nb, 0.0)
        du = w2 * dcr + w1 * ext_d[1:1 + tm, :] + w0 * ext_d[2:2 + tm, :]
        out_ref[:, DCV:2 * DCV] = (du * hh).astype(out_ref.dtype)
        out_ref[:, 2 * DCV:3 * DCV] = (du * cc).astype(out_ref.dtype)
        st_ref[0:1, :] += _colsum(dcr * u2)
        st_ref[1:2, :] += _colsum(dcr * u1)
        st_ref[2:3, :] += _colsum(dcr * u)

    row = lambda w: pl.BlockSpec((tm, w), lambda i: (i, 0))
    prev = lambda w: pl.BlockSpec((8, w), lambda i: (jnp.maximum(i * hb - 1, 0), 0))
    nxt = lambda w: pl.BlockSpec((8, w), lambda i: (jnp.minimum((i + 1) * hb, nt * hb - 1), 0))
    return pl.pallas_call(
        body, name=name, grid=(nt,),
        in_specs=[row(DCV), nxt(DCV), row(3 * DCV), prev(3 * DCV), nxt(3 * DCV),
                  pl.BlockSpec(conv_w.shape, lambda i: (0, 0))],
        out_specs=(row(3 * DCV), pl.BlockSpec((8, DCV), lambda i: (0, 0))),
        out_shape=(jax.ShapeDtypeStruct((t, 3 * DCV), MXU_DT), jax.ShapeDtypeStruct((8, DCV), F32)),
        scratch_shapes=[pltpu.VMEM((tm + 8, DCV), F32), pltpu.VMEM((tm + 8, DCV), F32)],
        compiler_params=_params(48),
    )(dy, dy, bch, bch, bch, conv_w)


def _attn_bwd(ka, kat, va, qat, dot, lrow, drow, r, name):
    t = ka.shape[1]
    tq = _tile(t, 512)
    nq = t // tq

    def body(r_ref, ka_ref, kat_ref, va_ref, l_ref, dl_ref, qat_hbm, dot_hbm,
             dk_ref, dv_ref, dck_ref, dqt_hbm, dcq_hbm, qat_v, dot_v, dq_acc):
        hp, j = pl.program_id(0), pl.program_id(1)

        @pl.when(j == 0)
        def _():
            pltpu.sync_copy(qat_hbm.at[pl.ds(2 * hp, 2)], qat_v)
            pltpu.sync_copy(dot_hbm.at[pl.ds(2 * hp, 2)], dot_v)
            dq_acc[...] = jnp.zeros_like(dq_acc)

        key = lax.broadcasted_iota(jnp.int32, (tq, tq), 0)
        qry = lax.broadcasted_iota(jnp.int32, (tq, tq), 1)

        def step(i, carry, masked):
            off = pl.multiple_of(i * tq, tq)
            out = []
            for a in range(2):
                dk, dv = carry[a]
                st = _mm(ka_ref[a], qat_v[a, :, pl.ds(off, tq)])
                dpt = _mm(va_ref[a], dot_v[a, :, pl.ds(off, tq)])
                if masked:
                    st = jnp.where(qry >= key, st, NEG)
                d = r_ref[2 * hp + a, i] - r_ref[2 * hp + a, j]
                pt = jnp.exp(st - (l_ref[a, :, pl.ds(off, tq)] - d))
                dsb = (pt * (dpt - dl_ref[a, :, pl.ds(off, tq)])).astype(MXU_DT)
                dv = dv + _mm_nt(dot_v[a, 0:DH, pl.ds(off, tq)], pt.astype(MXU_DT))
                dk = dk + _mm_nt(qat_v[a, 0:QROWS, pl.ds(off, tq)], dsb)
                dq_acc[a, :, pl.ds(off, tq)] += _mm(kat_ref[a], dsb)
                out.append((dk, dv))
            return tuple(out)

        init = tuple((jnp.zeros((QROWS, tq), F32), jnp.zeros((DH, tq), F32)) for _ in range(2))
        carry = step(j, init, True)
        (dka, dva), (dkb, dvb) = lax.fori_loop(j + 1, nq, lambda i, cr: step(i, cr, False), carry)
        dk_ref[0] = dka[0:DH].astype(dk_ref.dtype)
        dk_ref[1] = dkb[0:DH].astype(dk_ref.dtype)
        dv_ref[0] = dva.astype(dv_ref.dtype)
        dv_ref[1] = dvb.astype(dv_ref.dtype)
        dck_ref[0] = -dka[DH + 3:DH + 4, :]
        dck_ref[1] = -dkb[DH + 3:DH + 4, :]

        @pl.when(j == nq - 1)
        def _():
            pltpu.sync_copy(dq_acc, dqt_hbm.at[pl.ds(2 * hp, 2)])
            pltpu.sync_copy(dq_acc.at[:, DH:DH + 1, :], dcq_hbm.at[pl.ds(2 * hp, 2)])

    pair = lambda rows, cols: pl.BlockSpec((2, rows, cols), lambda p, j: (p, 0, 0))
    return pl.pallas_call(
        body, name=name, grid=(NH // 2, nq),
        in_specs=[pl.BlockSpec(memory_space=pltpu.SMEM),
                  pl.BlockSpec((2, tq, LANES), lambda p, j: (p, j, 0)),
                  pl.BlockSpec((2, QROWS, tq), lambda p, j: (p, 0, j)),
                  pl.BlockSpec((2, tq, LANES), lambda p, j: (p, j, 0)),
                  pair(1, t), pair(1, t), ANY, ANY],
        out_specs=(pl.BlockSpec((2, DH, tq), lambda p, j: (p, 0, j)),
                   pl.BlockSpec((2, DH, tq), lambda p, j: (p, 0, j)),
                   pl.BlockSpec((2, 1, tq), lambda p, j: (p, 0, j)), ANY, ANY),
        out_shape=(jax.ShapeDtypeStruct((NH, DH, t), MXU_DT), jax.ShapeDtypeStruct((NH, DH, t), MXU_DT),
                   jax.ShapeDtypeStruct((NH, 1, t), F32), jax.ShapeDtypeStruct((NH, QROWS, t), F32),
                   jax.ShapeDtypeStruct((NH, 1, t), F32)),
        scratch_shapes=[pltpu.VMEM((2, LANES, t), MXU_DT), pltpu.VMEM((2, LANES, t), MXU_DT),
                        pltpu.VMEM((2, QROWS, t), F32)],
        compiler_params=_params(52, 2),
    )(r, ka, kat, va, lrow, drow, qat, dot)


def _mix_proj_bwd(dr2, dqt, dkt, dvt, dbch, dc, z, xh1, rs1, g1, w_qkv, w_bch, w_f, name):
    t = dr2.shape[0]
    tm = _tile(t, 512)
    nt = t // tm

    def body(dr_ref, dqt_ref, dkt_ref, dvt_ref, db_ref, dc_ref, z_ref, x_ref, rs_ref, g_ref,
             wq_ref, wb_ref, wf_ref, out_ref, df_ref, dq_ref, dk_ref, dv_ref, st_ref, carry):
        @pl.when(pl.program_id(0) == 0)
        def _():
            carry[...] = jnp.zeros_like(carry)
            st_ref[...] = jnp.zeros_like(st_ref)

        tokens_major = lambda parts: jnp.concatenate(parts, axis=0).T
        dq_ref[...] = (tokens_major([dqt_ref[h, 0:DH, :] for h in range(NH)]) * 0.125).astype(dq_ref.dtype)
        dk_ref[...] = tokens_major([dkt_ref[h].astype(F32) for h in range(NH)]).astype(dk_ref.dtype)
        dv_ref[...] = tokens_major([dvt_ref[h].astype(F32) for h in range(NH)]).astype(dv_ref.dtype)

        row = lax.broadcasted_iota(jnp.int32, (tm, tm), 0)
        col = lax.broadcasted_iota(jnp.int32, (tm, tm), 1)
        triu = jnp.where(col >= row, 1.0, 0.0).astype(MXU_DT)
        dlogf = carry[...] + _mm_sel(triu, dc_ref[...])
        carry[...] = dlogf[0:1, :]
        dz = dlogf / (1.0 + jnp.exp(z_ref[...]))
        st_ref[2:3, 0:LANES] += _colsum(dz)
        dfb = dz.astype(MXU_DT)
        df_ref[...] = dfb
        dx = (ALPHA * dr_ref[...]
              + _mm_nt(dq_ref[...], wq_ref[:, 0:DA])
              + _mm_nt(dk_ref[...], wq_ref[:, DA:2 * DA])
              + _mm_nt(dv_ref[...], wq_ref[:, 2 * DA:3 * DA])
              + _mm_nt(db_ref[...], wb_ref[...])
              + _mm_nt(dfb, wf_ref[...]))
        xh = x_ref[...]
        st_ref[0:1, :] += _colsum(dx * xh)
        st_ref[1:2, :] += _colsum(dx)
        out_ref[...] = _ln_bwd(dx * g_ref[...], xh, rs_ref[...])

    row = lambda w: pl.BlockSpec((tm, w), lambda i: (nt - 1 - i, 0))
    fmaj = lambda rows: pl.BlockSpec((NH, rows, tm), lambda i: (0, 0, nt - 1 - i))
    full = lambda a: pl.BlockSpec(a.shape, lambda i: (0, 0))
    return pl.pallas_call(
        body, name=name, grid=(nt,),
        in_specs=[row(D), fmaj(QROWS), fmaj(DH), fmaj(DH),
                  row(3 * DCV), row(LANES), row(LANES), row(D), row(1),
                  full(g1), full(w_qkv), full(w_bch), full(w_f)],
        out_specs=(row(D), row(LANES), row(DA), row(DA), row(DA), pl.BlockSpec((8, D), lambda i: (0, 0))),
        out_shape=(jax.ShapeDtypeStruct((t, D), F32), jax.ShapeDtypeStruct((t, LANES), MXU_DT))
        + (jax.ShapeDtypeStruct((t, DA), MXU_DT),) * 3 + (jax.ShapeDtypeStruct((8, D), F32),),
        scratch_shapes=[pltpu.VMEM((1, LANES), F32)],
        compiler_params=_params(48),
    )(dr2, dqt, dkt, dvt, dbch, dc, z, xh1, rs1, g1, w_qkv, w_bch, w_f)


def _dw(mode, a_parts, b, m, n, name, tmm=None, tn=None, exchange=()):
    t = b.shape[0]
    tmm = tmm or m
    tn = tn or n
    tt = _tile(t, 2048)
    na, ne = len(a_parts), len(exchange)
    grid = (m // tmm, n // tn, t // tt)

    def body(*refs):
        a_refs, b_ref, o_ref = refs[:na], refs[na], refs[na + 1 + ne]
        if ne:
            e_start, e_finish = _exchange_phases(refs[na + 1:na + 1 + ne], refs[na + 2 + ne:na + 2 + 2 * ne],
                                                 *refs[na + 2 + 2 * ne:])
            at = lambda steps: functools.reduce(jnp.logical_and, [pl.program_id(d) == s for d, s in enumerate(steps)])
            pl.when(at((0, 0, 0)))(e_start)

        @pl.when(pl.program_id(2) == 0)
        def _():
            o_ref[...] = jnp.zeros_like(o_ref)

        if mode == "plain":
            a = a_refs[0][...].astype(MXU_DT)
        elif mode == "affine":
            a = (a_refs[0][...] * a_refs[1][...] + a_refs[2][...]).astype(MXU_DT)
        else:
            g = a_refs[0][...].astype(F32)
            a = (g * _sigmoid(g) * a_refs[1][...].astype(F32)).astype(MXU_DT)
        o_ref[...] += _mm_tn(a, b_ref[...].astype(MXU_DT))
        if ne:
            pl.when(at(tuple(g - 1 for g in grid)))(e_finish)

    a_tile = pl.BlockSpec((tt, tmm), lambda i, j, k: (k, i))
    a_vec = pl.BlockSpec((1, tmm), lambda i, j, k: (0, i))
    a_specs = {"plain": [a_tile], "affine": [a_tile, a_vec, a_vec], "swiglu": [a_tile, a_tile]}[mode]
    res = pl.pallas_call(
        body, name=name, grid=grid,
        in_specs=a_specs + [pl.BlockSpec((tt, tn), lambda i, j, k: (k, j))] + [ANY] * ne,
        out_specs=(pl.BlockSpec((tmm, tn), lambda i, j, k: (i, j)),) + (ANY,) * ne,
        out_shape=(jax.ShapeDtypeStruct((m, n), F32),)
        + tuple(jax.ShapeDtypeStruct(a.shape, a.dtype) for a in exchange),
        scratch_shapes=_exchange_sems(ne) if ne else [],
        compiler_params=_params(52, 3),
    )(*a_parts, b, *exchange)
    return res if ne else res[0]


def _dw_shared(a_parts, bs, name):
    xh, g, b = a_parts
    t, m = xh.shape
    tt = _tile(t, 1024)
    nb = len(bs)

    def body(*refs):
        x_ref, g_ref, b_ref = refs[:3]
        b_refs, o_refs = refs[3:3 + nb], refs[3 + nb:]

        @pl.when(pl.program_id(0) == 0)
        def _():
            for o_ref in o_refs:
                o_ref[...] = jnp.zeros_like(o_ref)

        at = (x_ref[...] * g_ref[...] + b_ref[...]).T.astype(MXU_DT)
        for b_ref, o_ref in zip(b_refs, o_refs):
            o_ref[...] += _mm(at, b_ref[...].astype(MXU_DT))

    vec = pl.BlockSpec((1, m), lambda k: (0, 0))
    return pl.pallas_call(
        body, name=name, grid=(t // tt,),
        in_specs=[pl.BlockSpec((tt, m), lambda k: (k, 0)), vec, vec]
        + [pl.BlockSpec((tt, x.shape[1]), lambda k: (k, 0)) for x in bs],
        out_specs=tuple(pl.BlockSpec((m, x.shape[1]), lambda k: (0, 0)) for x in bs),
        out_shape=tuple(jax.ShapeDtypeStruct((m, x.shape[1]), F32) for x in bs),
        compiler_params=_params(56),
    )(xh, g, b, *bs)


def _adamw(w, g, m, v):
    m = ADAM_B1 * m + (1.0 - ADAM_B1) * g
    v = ADAM_B2 * v + (1.0 - ADAM_B2) * (g * g)
    m_hat = m / (1.0 - ADAM_B1 ** ADAM_STEP)
    v_hat = v / (1.0 - ADAM_B2 ** ADAM_STEP)
    delta = -ADAM_LR * (m_hat / (jnp.sqrt(v_hat) + ADAM_EPS) + ADAM_WD * w)
    return delta, m, v


def _reduce_adamw(landed, own, w, m, v, name):
    r, c = own.shape
    tr = _tile(r, 128)

    def body(l_ref, o_ref, w_ref, m_ref, v_ref, g_out, d_out, m_out, v_out):
        me = 4 * lax.axis_index("x") + 2 * lax.axis_index("y") + lax.axis_index("c")
        g = None
        for j in range(NDEV):
            term = jnp.where(me == j, o_ref[...], l_ref[j].astype(F32))
            g = term if g is None else g + term
        g_out[...] = g
        d_out[...], m_out[...], v_out[...] = _adamw(w_ref[...], g, m_ref[...], v_ref[...])

    blk = pl.BlockSpec((tr, c), lambda i: (i, 0))
    sds = jax.ShapeDtypeStruct((r, c), F32)
    return pl.pallas_call(
        body, name=name, grid=(r // tr,),
        in_specs=[pl.BlockSpec((NDEV, tr, c), lambda i: (0, i, 0)), blk, blk, blk, blk],
        out_specs=(blk, blk, blk, blk), out_shape=(sds, sds, sds, sds),
        compiler_params=_params(40),
    )(landed, own, w, m, v)


def _sum_small(gathered, name):
    _, r, c = gathered.shape

    def body(g_ref, o_ref):
        acc = g_ref[0]
        for j in range(1, NDEV):
            acc = acc + g_ref[j]
        o_ref[...] = acc

    return pl.pallas_call(body, name=name, out_shape=jax.ShapeDtypeStruct((r, c), F32))(gathered)


def _adamw_small(g, w, m, v, name):
    def body(g_ref, w_ref, m_ref, v_ref, d_out, m_out, v_out):
        d_out[...], m_out[...], v_out[...] = _adamw(w_ref[...], g_ref[...], m_ref[...], v_ref[...])

    sds = jax.ShapeDtypeStruct(g.shape, F32)
    return pl.pallas_call(body, name=name, out_shape=(sds, sds, sds))(g, w, m, v)


def _cols_from_stack(s):
    return jnp.transpose(s, (1, 0, 2)).reshape(s.shape[1], NDEV * s.shape[2])


def _cols_to_stack(w):
    r, c = w.shape
    return jnp.transpose(w.reshape(r, NDEV, c // NDEV), (1, 0, 2))


def _rows_from_stack(s):
    return s.reshape(NDEV * s.shape[1], s.shape[2])


def _rows_to_stack(w):
    r, c = w.shape
    return w.reshape(NDEV, r // NDEV, c)


SMALL_ROWS = 16
SMALL_SLOTS = {
    "ln1_g": (0, 0, D), "ln1_b": (1, 0, D), "ln2_g": (2, 0, D), "ln2_b": (3, 0, D), "ln3_g": (4, 0, D),
    "ln3_b": (5, 0, D), "b_ple_gate": (6, 0, D), "ln4_g": (7, 0, D), "ln4_b": (8, 0, D),
    "g_attn": (9, 0, DA), "g_conv": (9, DA, DCV), "b_forget": (10, 0, NH),
}
CONVW_ROW = 11
LOSS_SLOT = (10, LANES)


def _pack_small(vals, conv_rows, loss=None):
    out = jnp.zeros((SMALL_ROWS, D), F32)
    for nm, (r, off, wd) in SMALL_SLOTS.items():
        out = out.at[r:r + 1, off:off + wd].set(vals[nm].reshape(1, wd).astype(F32))
    out = out.at[CONVW_ROW:CONVW_ROW + 3, 0:conv_rows.shape[1]].set(conv_rows.astype(F32))
    if loss is not None:
        out = out.at[LOSS_SLOT[0], LOSS_SLOT[1]].set(loss)
    return out


def _unpack_small(packed, name):
    r, off, wd = SMALL_SLOTS[name]
    return packed[r:r + 1, off:off + wd]


def kernel(x, p, ffn1_w_in, ffn1_w_out, ln1_g, ln1_b, w_mix_in, b_forget, conv_w, g_attn, g_conv, w_mix_out, ln2_g, ln2_b, ffn2_w_in, ffn2_w_out, ln3_g, ln3_b, w_ple, w_ple_gate, b_ple_gate, ln4_g, ln4_b, loss_target, m_ffn1_w_in, m_ffn1_w_out, m_ln1_g, m_ln1_b, m_w_mix_in, m_b_forget, m_conv_w, m_g_attn, m_g_conv, m_w_mix_out, m_ln2_g, m_ln2_b, m_ffn2_w_in, m_ffn2_w_out, m_ln3_g, m_ln3_b, m_w_ple, m_w_ple_gate, m_b_ple_gate, m_ln4_g, m_ln4_b, v_ffn1_w_in, v_ffn1_w_out, v_ln1_g, v_ln1_b, v_w_mix_in, v_b_forget, v_conv_w, v_g_attn, v_g_conv, v_w_mix_out, v_ln2_g, v_ln2_b, v_ffn2_w_in, v_ffn2_w_out, v_ln3_g, v_ln3_b, v_w_ple, v_w_ple_gate, v_b_ple_gate, v_ln4_g, v_ln4_b):
    args = dict(locals())
    t = x.shape[1]
    me = 4 * lax.axis_index("x") + 2 * lax.axis_index("y") + lax.axis_index("c")
    x0 = x.reshape(t, D)
    p0 = p.reshape(t, PLE)
    tgt = loss_target.reshape(t, D)

    big = ["ffn1_w_in", "ffn1_w_out", "w_mix_in", "w_mix_out", "ffn2_w_in", "ffn2_w_out", "w_ple", "w_ple_gate"]
    col_sharded = {"ffn1_w_in", "w_mix_in", "ffn2_w_in", "w_ple"}
    shard = {nm: args[nm][0] for nm in big}

    unstack = lambda nm, g: (_cols_from_stack(g) if nm in col_sharded else _rows_from_stack(g)).astype(MXU_DT)
    stack = lambda nm, g: _cols_to_stack(g) if nm in col_sharded else _rows_to_stack(g)
    wire = lambda names: [shard[nm].astype(WIRE_DT) for nm in names]
    first, later = big[:2], big[2:]

    full = {nm: unstack(nm, g) for nm, g in zip(first, _allgather(wire(first), "ag_ffn1"))}
    ffn1_out, gathered = _ffn1_fwd(x0, full, wire(later) + [conv_w[0]])
    full.update({nm: unstack(nm, g) for nm, g in zip(later, gathered)})
    cw = _cols_from_stack(gathered[len(later)])

    dr1, gw, small, loss_part = _mid_step(p0, tgt, full, cw, {nm: args[nm] for nm in SMALL_SLOTS}, ffn1_out)
    small_part = _pack_small({nm: small[nm] for nm in SMALL_SLOTS},
                             jnp.pad(small["conv_w"], ((0, 0), (0, D - DCV))), loss_part)
    stacks = {nm: stack(nm, gw[nm]) for nm in later}
    beside_bwd, beside_dw = later[1:], later[:1]
    gx, gw1, landed_bwd, landed_dw, landed_w_in = _ffn1_bwd(
        x0, dr1, ffn1_out, full, [stacks[nm].astype(WIRE_DT) for nm in beside_bwd],
        [stacks[nm].astype(WIRE_DT) for nm in beside_dw], lambda g: stack("ffn1_w_in", g).astype(WIRE_DT))
    stacks.update({nm: stack(nm, gw1[nm]) for nm in first})
    (landed_w_out,), (small_all,) = _exchange_and_gather([stacks["ffn1_w_out"].astype(WIRE_DT)], [small_part],
                                                         "rs_ffn1_out")
    landed = dict(zip(beside_bwd + beside_dw, list(landed_bwd) + list(landed_dw)),
                  ffn1_w_in=landed_w_in, ffn1_w_out=landed_w_out)
    small_g = _sum_small(small_all, "sum_small")
    loss = small_g[LOSS_SLOT[0], LOSS_SLOT[1]]

    outs = {"loss": loss, "grad_x": gx.reshape(1, t, D)}
    for nm in big:
        own = lax.dynamic_index_in_dim(stacks[nm], me, axis=0, keepdims=False)
        g, dl, mn, vn = _reduce_adamw(landed[nm], own, shard[nm], args["m_" + nm][0], args["v_" + nm][0],
                                      "adamw_" + nm)
        outs["grad_" + nm], outs["delta_" + nm], outs["new_m_" + nm], outs["new_v_" + nm] = (
            g[None], dl[None], mn[None], vn[None])
    small_names = list(SMALL_SLOTS)
    cshard = lax.dynamic_slice_in_dim(small_g[CONVW_ROW:CONVW_ROW + 3, 0:DCV], me * (DCV // NDEV), DCV // NDEV, axis=1)
    g_pack = _pack_small({nm: _unpack_small(small_g, nm) for nm in small_names}, cshard)
    packs = [_pack_small({nm: args[pre + nm] for nm in small_names}, args[pre + "conv_w"][0])
             for pre in ("", "m_", "v_")]
    d_pack, m_pack, v_pack = _adamw_small(g_pack, packs[0], packs[1], packs[2], "adamw_small")
    for key, pk in (("grad_", g_pack), ("delta_", d_pack), ("new_m_", m_pack), ("new_v_", v_pack)):
        for nm in small_names:
            outs[key + nm] = _unpack_small(pk, nm)
        outs[key + "conv_w"] = pk[CONVW_ROW:CONVW_ROW + 3, 0:DCV // NDEV][None]

    wnames = ["ffn1_w_in", "ffn1_w_out", "ln1_g", "ln1_b", "w_mix_in", "b_forget", "conv_w", "g_attn", "g_conv",
              "w_mix_out", "ln2_g", "ln2_b", "ffn2_w_in", "ffn2_w_out", "ln3_g", "ln3_b", "w_ple", "w_ple_gate",
              "b_ple_gate", "ln4_g", "ln4_b"]
    return (outs["loss"], outs["grad_x"], *[outs[pre + nm] for pre in ("grad_", "delta_", "new_m_", "new_v_")
                                            for nm in wnames])


def _ffn1_fwd(x0, full, gather=()):
    res = _ffn_fwd(x0, jnp.ones((1, D), F32), jnp.zeros((1, D), F32), full["ffn1_w_in"], full["ffn1_w_out"],
                   "ffn1_fwd", gather)
    return res[:4], res[4:]


def _ffn1_bwd(x0, dr1, ffn1_out, full, exchange=(), exchange_late=(), w_in_slots=None):
    g1a, u1a, _, rs1 = ffn1_out
    ones, zeros = jnp.ones((1, D), F32), jnp.zeros((1, D), F32)
    res = _ffn_bwd(dr1, g1a, u1a, x0, rs1, ones, full["ffn1_w_in"], full["ffn1_w_out"], False, "ffn1_bwd",
                   exchange)
    df1, dg1, du1, gx = res[:4]
    dw_g = _dw("affine", (x0, ones, zeros), dg1, D, F, "dw_ffn1_in_g", tn=F // 2, exchange=exchange_late)
    dw_g, landed_late = (dw_g[0], dw_g[1:]) if exchange_late else (dw_g, ())
    gw_in = jnp.concatenate([dw_g, _dw("affine", (x0, ones, zeros), du1, D, F, "dw_ffn1_in_u", tn=F // 2)], axis=1)
    side = () if w_in_slots is None else (w_in_slots(gw_in),)
    out = _dw("swiglu", (g1a, u1a), df1, F, D, "dw_ffn1_out", tmm=F // 2, exchange=side)
    gw_out, landed_in = (out, None) if w_in_slots is None else (out[0], out[1])
    return gx, {"ffn1_w_in": gw_in, "ffn1_w_out": gw_out}, res[5:], landed_late, landed_in


def _mid_step(p0, tgt, full, cw, sp, ffn1_out):
    g1a, u1a, xh1, rs1 = ffn1_out
    t = xh1.shape[0]
    ln1_g, ln1_b, ln2_g, ln2_b, ln3_g, ln3_b = (sp[k] for k in ("ln1_g", "ln1_b", "ln2_g", "ln2_b", "ln3_g", "ln3_b"))
    ln4_g, ln4_b, g_attn, g_conv, b_ple_gate = (sp[k] for k in ("ln4_g", "ln4_b", "g_attn", "g_conv", "b_ple_gate"))
    wmi = full["w_mix_in"]
    w_qkv = wmi[:, 0:3 * DA]
    w_f = jnp.pad(wmi[:, 3 * DA:3 * DA + NH], ((0, 0), (0, LANES - NH)))
    w_bch = wmi[:, 3 * DA + NH:]
    bf_pad = jnp.pad(sp["b_forget"], ((0, 0), (0, LANES - NH)))

    ka, va, qat, kat, vta, bch, z, rt = _mix_proj_fwd(xh1, ln1_g, ln1_b, w_qkv[:, DA:], jnp.transpose(w_qkv),
                                                      w_bch, w_f, bf_pad, "mix_proj_fwd")
    rtile = jnp.transpose(rt[:, 0, 0:NH])
    o, lse = _attn_fwd(qat, ka, vta, rtile, "attn_fwd")
    merged, xh2, rs2 = _mix_post_fwd(o, bch, cw, g_attn, g_conv, xh1, ln1_g, ln1_b, full["w_mix_out"],
                                     "mix_post_fwd")
    g2a, u2a, xh3, rs3 = _ffn_fwd(xh2, ln2_g, ln2_b, full["ffn2_w_in"], full["ffn2_w_out"], "ffn2_fwd")

    dr3, dz, de, st_tail = _tail(xh3, rs3, ln3_g, ln3_b, p0, full["w_ple_gate"], full["w_ple"], b_ple_gate,
                                 ln4_g, ln4_b, tgt, "tail")
    df2, dg2, du2, dr2, st_f2 = _ffn_bwd(dr3, g2a, u2a, xh2, rs2, ln2_g, full["ffn2_w_in"], full["ffn2_w_out"],
                                         True, "ffn2_bwd")
    dmix, dot, drow, dyc, st_post = _mix_post_bwd(dr2, o, bch, cw, g_attn, g_conv, full["w_mix_out"],
                                                  "mix_post_bwd")
    dbch, st_conv = _conv_bwd(dyc, bch, cw, "conv_bwd")
    dkt, dvt, dck, dqt, dcq = _attn_bwd(ka, kat, va, qat, dot, lse, drow, rtile, "attn_bwd")
    dc_pad = jnp.pad(jnp.transpose((dcq + dck).reshape(NH, t)), ((0, 0), (0, LANES - NH)))
    dr1, dfl, dq, dk, dv, st_proj = _mix_proj_bwd(dr2, dqt, dkt, dvt, dbch, dc_pad, z, xh1, rs1, ln1_g,
                                                  w_qkv, w_bch, w_f, "mix_proj_bwd")

    x1p, x2p, x3p = (xh1, ln1_g, ln1_b), (xh2, ln2_g, ln2_b), (xh3, ln3_g, ln3_b)
    gw = {}
    gw["ffn2_w_in"] = jnp.concatenate(
        [_dw("affine", x2p, dg2, D, F, "dw_ffn2_in_g", tn=F // 2),
         _dw("affine", x2p, du2, D, F, "dw_ffn2_in_u", tn=F // 2)], axis=1)
    gw["ffn2_w_out"] = _dw("swiglu", (g2a, u2a), df2, F, D, "dw_ffn2_out", tmm=F // 2)
    gw["w_mix_out"] = _dw("plain", (merged,), dmix, D, D, "dw_mix_out")
    gq, gk, gv, gf, gbch = _dw_shared(x1p, (dq, dk, dv, dfl, dbch), "dw_mix_in")
    gw["w_mix_in"] = jnp.concatenate([gq, gk, gv, gf[:, 0:NH], gbch], axis=1)
    gw["w_ple_gate"] = _dw("affine", x3p, dz, D, D, "dw_ple_gate")
    gw["w_ple"] = _dw("plain", (p0,), de, PLE, D, "dw_ple")

    loss_part = (0.5 / D) * jnp.sum(st_tail[5:6, :])
    small = {"ln1_g": st_proj[0:1], "ln1_b": st_proj[1:2], "ln2_g": st_f2[0:1], "ln2_b": st_f2[1:2],
             "ln3_g": st_tail[3:4], "ln3_b": st_tail[4:5], "b_ple_gate": st_tail[2:3], "ln4_g": st_tail[0:1],
             "ln4_b": st_tail[1:2], "g_attn": st_post[0:1], "g_conv": st_post[1:2],
             "b_forget": st_proj[2:3, 0:NH], "conv_w": st_conv[0:3]}
    return dr1, gw, small, loss_part
```

```python
import functools

import jax
import jax.numpy as jnp
from jax import lax
from jax.experimental import pallas as pl
from jax.experimental.pallas import tpu as pltpu

D = 1024
F = 2816
NH = 8
DH = 64
DA = NH * DH
DCV = D - DA
PLE = 256
LN_EPS = 1e-5
RMS_EPS = 1e-6
NEG = -1e30
ALPHA = 2.0 ** 0.25
NDEV = 8
LANES = 128

ADAM_LR, ADAM_B1, ADAM_B2, ADAM_EPS, ADAM_WD, ADAM_STEP = 0.001, 0.9, 0.999, 1e-08, 0.01, 10

F32 = jnp.float32
MXU_DT = jnp.bfloat16
WIRE_DT = jnp.bfloat16

MESH_ID = pl.DeviceIdType.MESH
ANY = pl.BlockSpec(memory_space=pl.ANY)


def _params(vmem_mb, n_axes=1):
    return pltpu.CompilerParams(dimension_semantics=("arbitrary",) * n_axes,
                                vmem_limit_bytes=int(vmem_mb) << 20)


def _mm(a, b):
    return jnp.dot(a, b, preferred_element_type=F32)


def _mm_nt(a, b):
    return lax.dot_general(a, b, (((1,), (1,)), ((), ())), preferred_element_type=F32)


def _mm_tn(a, b):
    return lax.dot_general(a, b, (((0,), (0,)), ((), ())), preferred_element_type=F32)


def _split3(x):
    hi = x.astype(MXU_DT)
    r1 = x - hi.astype(F32)
    mid = r1.astype(MXU_DT)
    lo = (r1 - mid.astype(F32)).astype(MXU_DT)
    return hi, mid, lo


def _mm_sel(sel, x):
    hi, mid, lo = _split3(x)
    return _mm(sel, hi) + _mm(sel, mid) + _mm(sel, lo)


def _sigmoid(x):
    return 1.0 / (1.0 + jnp.exp(-x))


def _ln_fwd(r):
    mu = jnp.mean(r, axis=-1, keepdims=True)
    xc = r - mu
    var = jnp.mean(xc * xc, axis=-1, keepdims=True)
    rstd = lax.rsqrt(var + LN_EPS)
    return xc * rstd, rstd


def _ln_bwd(dxhat, xhat, rstd):
    m1 = jnp.mean(dxhat, axis=-1, keepdims=True)
    m2 = jnp.mean(dxhat * xhat, axis=-1, keepdims=True)
    return rstd * (dxhat - m1 - xhat * m2)


def _rms_fwd(x):
    r = lax.rsqrt(jnp.mean(x * x, axis=-1, keepdims=True) + RMS_EPS)
    return x * r, r


def _rms_bwd(dyg, xn, r):
    return r * (dyg - xn * jnp.mean(dyg * xn, axis=-1, keepdims=True))


def _colsum(x):
    return jnp.sum(x, axis=0, keepdims=True)


def _f_chunks():
    out, c0 = [], 0
    while c0 < F:
        fc = min(512, F - c0)
        out.append((c0, fc))
        c0 += fc
    return out


def _tile(t, want):
    return want if t % want == 0 and t >= want else t


def _exchange_sems(n):
    return [pltpu.SemaphoreType.DMA((n * (NDEV - 1),)), pltpu.SemaphoreType.DMA((n * (NDEV - 1),))]


def _exchange_phases(ins, outs, send_sems, recv_sems):
    n = len(ins)

    def peers():
        x, y, c = lax.axis_index("x"), lax.axis_index("y"), lax.axis_index("c")
        out = []
        for k in range(1, NDEV):
            px = 1 - x if (k >> 2) & 1 else x
            py = 1 - y if (k >> 1) & 1 else y
            pc = 1 - c if k & 1 else c
            out.append(((px, py, pc), 4 * px + 2 * py + pc))
        return 4 * x + 2 * y + c, out

    def remote(w, k, to, slot_src, slot_dst):
        return pltpu.make_async_remote_copy(
            src_ref=ins[w].at[slot_src], dst_ref=outs[w].at[slot_dst],
            send_sem=send_sems.at[w * (NDEV - 1) + k], recv_sem=recv_sems.at[w * (NDEV - 1) + k],
            device_id=to, device_id_type=MESH_ID)

    def start():
        me, prs = peers()
        for k, (to, pid) in enumerate(prs):
            for w in range(n):
                remote(w, k, to, pid, me).start()

    def finish():
        me, prs = peers()
        for k, (to, pid) in enumerate(prs):
            for w in range(n):
                remote(w, k, to, me, pid).wait_recv()
        for k, (to, pid) in enumerate(prs):
            for w in range(n):
                remote(w, k, to, pid, me).wait_send()

    return start, finish


def _gather_sems(n):
    return [pltpu.SemaphoreType.DMA((n * (NDEV - 1),)), pltpu.SemaphoreType.DMA((n * (NDEV - 1),)),
            pltpu.SemaphoreType.DMA((n,))]


def _gather_phases(ins, outs, send_sems, recv_sems, loc_sems):
    n = len(ins)
    per = NDEV - 1

    def place():
        x, y, c = lax.axis_index("x"), lax.axis_index("y"), lax.axis_index("c")
        return (x, y, c), (x, y, 1 - c), [(1 - x, y), (x, 1 - y), (1 - x, 1 - y)]

    def copy(w, k, block, to, src=None):
        dst = outs[w].at[4 * block[0] + 2 * block[1] + block[2]]
        return pltpu.make_async_remote_copy(
            src_ref=dst if src is None else src, dst_ref=dst,
            send_sem=send_sems.at[w * per + k], recv_sem=recv_sems.at[w * per + k],
            device_id=to, device_id_type=MESH_ID)

    def local(w, me):
        return pltpu.make_async_copy(ins[w], outs[w].at[4 * me[0] + 2 * me[1] + me[2]], loc_sems.at[w])

    def first(me, sib, chips):
        out = []
        for j, chip in enumerate(chips):
            out += [copy(w, 1 + j, me, (*chip, me[2]), src=ins[w]) for w in range(n)]
        return out + [copy(w, 0, me, sib, src=ins[w]) for w in range(n)]

    def start():
        me, sib, chips = place()
        for w in range(n):
            local(w, me).start()
        for cp in first(me, sib, chips):
            cp.start()

    def forward():
        me, sib, chips = place()
        for j, chip in enumerate(chips):
            for w in range(n):
                copy(w, 1 + j, (*chip, me[2]), me).wait_recv()
                copy(w, 4 + j, (*chip, me[2]), sib).start()

    def finish():
        me, sib, chips = place()
        for w in range(n):
            copy(w, 0, sib, me).wait_recv()
        for j, chip in enumerate(chips):
            for w in range(n):
                copy(w, 4 + j, (*chip, 1 - me[2]), me).wait_recv()
        for cp in first(me, sib, chips):
            cp.wait_send()
        for j, chip in enumerate(chips):
            for w in range(n):
                copy(w, 4 + j, (*chip, me[2]), sib).wait_send()
        for w in range(n):
            local(w, me).wait()

    return start, forward, finish


def _allgather(arrs, name):
    n = len(arrs)

    def body(*refs):
        start, forward, finish = _gather_phases(refs[:n], refs[n:2 * n], *refs[2 * n:])
        start()
        forward()
        finish()

    return pl.pallas_call(
        body, name=name, out_shape=tuple(jax.ShapeDtypeStruct((NDEV,) + a.shape, a.dtype) for a in arrs),
        in_specs=[ANY] * n, out_specs=tuple([ANY] * n), scratch_shapes=_gather_sems(n),
    )(*arrs)


def _exchange_and_gather(ex, ga, name):
    ne, ng = len(ex), len(ga)

    def body(*refs):
        ins, outs, sems = refs[:ne + ng], refs[ne + ng:2 * (ne + ng)], refs[2 * (ne + ng):]
        e_start, e_finish = _exchange_phases(ins[:ne], outs[:ne], *sems[:2])
        g_start, g_forward, g_finish = _gather_phases(ins[ne:], outs[ne:], *sems[2:])
        e_start()
        g_start()
        g_forward()
        g_finish()
        e_finish()

    res = pl.pallas_call(
        body, name=name,
        out_shape=tuple(jax.ShapeDtypeStruct(a.shape, a.dtype) for a in ex)
        + tuple(jax.ShapeDtypeStruct((NDEV,) + a.shape, a.dtype) for a in ga),
        in_specs=[ANY] * (ne + ng), out_specs=tuple([ANY] * (ne + ng)),
        scratch_shapes=_exchange_sems(ne) + _gather_sems(ng),
    )(*ex, *ga)
    return res[:ne], res[ne:]


def _ffn_fwd(xin, gin, bin_, w_in, w_out, name, gather=()):
    t = xin.shape[0]
    tm = _tile(t, 512)
    nt = t // tm
    chunks = _f_chunks()
    ng = len(gather)

    def body(*refs):
        x_ref, gi_ref, bi_ref, win_hbm, wout_hbm = refs[:5]
        g_ref, u_ref, xh_ref, rs_ref = refs[5 + ng:9 + ng]
        win_v, wout_v, acc_ref = refs[9 + 2 * ng:12 + 2 * ng]
        if ng:
            g_start, g_forward, g_finish = _gather_phases(refs[5:5 + ng], refs[9 + ng:9 + 2 * ng],
                                                          *refs[12 + 2 * ng:])

        @pl.when(pl.program_id(0) == 0)
        def _():
            if ng:
                g_start()
            pltpu.sync_copy(win_hbm, win_v)
            pltpu.sync_copy(wout_hbm, wout_v)

        if ng:
            pl.when(pl.program_id(0) == nt // 2)(g_forward)
            pl.when(pl.program_id(0) == nt - 1)(g_finish)

        x = x_ref[...] * gi_ref[...] + bi_ref[...]
        xb = x.astype(MXU_DT)
        for ci, (c0, fc) in enumerate(chunks):
            gc = _mm(xb, win_v[:, c0:c0 + fc])
            uc = _mm(xb, win_v[:, F + c0:F + c0 + fc])
            g_ref[:, c0:c0 + fc] = gc.astype(g_ref.dtype)
            u_ref[:, c0:c0 + fc] = uc.astype(u_ref.dtype)
            hc = (gc * _sigmoid(gc) * uc).astype(MXU_DT)
            part = _mm(hc, wout_v[c0:c0 + fc, :])
            if ci == 0:
                acc_ref[...] = part
            else:
                acc_ref[...] += part
        xh, rstd = _ln_fwd(ALPHA * x + 0.5 * acc_ref[...])
        xh_ref[...] = xh
        rs_ref[...] = rstd

    row = pl.BlockSpec((tm, D), lambda i: (i, 0))
    vec = pl.BlockSpec((1, D), lambda i: (0, 0))
    act = pl.BlockSpec((tm, F), lambda i: (i, 0))
    return pl.pallas_call(
        body, name=name, grid=(nt,),
        in_specs=[row, vec, vec, ANY, ANY] + [ANY] * ng,
        out_specs=(act, act, row, pl.BlockSpec((tm, 1), lambda i: (i, 0))) + (ANY,) * ng,
        out_shape=(jax.ShapeDtypeStruct((t, F), MXU_DT), jax.ShapeDtypeStruct((t, F), MXU_DT),
                   jax.ShapeDtypeStruct((t, D), F32), jax.ShapeDtypeStruct((t, 1), F32))
        + tuple(jax.ShapeDtypeStruct((NDEV,) + a.shape, a.dtype) for a in gather),
        scratch_shapes=[pltpu.VMEM((D, 2 * F), MXU_DT), pltpu.VMEM((F, D), MXU_DT), pltpu.VMEM((tm, D), F32)]
        + (_gather_sems(ng) if ng else []),
        compiler_params=_params(52),
    )(xin, gin, bin_, w_in, w_out, *gather)


QROWS = 80
BIAS_AT = DH


def _place_matrices():
    import numpy as np
    pk = np.zeros((NH, LANES, LANES), np.float32)
    pqt = np.zeros((NH, LANES, LANES), np.float32)
    for h in range(NH):
        for piece in range(3):
            pk[h, 8 * piece + h, BIAS_AT + 3 + piece] = -1.0
            pqt[h, BIAS_AT + piece, 8 * piece + h] = 1.0
    pkt = np.transpose(pk, (0, 2, 1))
    return tuple(jnp.asarray(m, MXU_DT) for m in (pk, pqt, pkt))


def _mix_proj_fwd(xh1, g1, b1, w_kv, w_qkv_t, w_bch, w_f, bf_pad, name):
    t = xh1.shape[0]
    tm = _tile(t, 512)
    pk, pqt, pkt = _place_matrices()

    def body(x_ref, g_ref, b_ref, wkv_ref, wt_ref, wb_ref, wf_ref, bf_ref, pk_ref, pqt_ref, pkt_ref,
             ka_ref, va_ref, qat_ref, kat_ref, vta_ref, bch_ref, z_ref, r_ref, carry):
        @pl.when(pl.program_id(0) == 0)
        def _():
            carry[...] = jnp.zeros_like(carry)

        xb = (x_ref[...] * g_ref[...] + b_ref[...]).astype(MXU_DT)
        kv = _mm(xb, wkv_ref[...])
        qkvt = _mm_nt(wt_ref[...], xb)
        bch_ref[...] = _mm(xb, wb_ref[...])
        z = _mm(xb, wf_ref[...]) + bf_ref[...]
        z_ref[...] = z
        logf = jnp.minimum(z, 0.0) - jnp.log(1.0 + jnp.exp(-jnp.abs(z)))
        row = lax.broadcasted_iota(jnp.int32, (tm, tm), 0)
        col = lax.broadcasted_iota(jnp.int32, (tm, tm), 1)
        tri = jnp.where(row >= col, 1.0, 0.0).astype(MXU_DT)
        c = carry[...] + _mm_sel(tri, logf)
        carry[...] = c[tm - 1:tm, :]
        r_ref[0] = c[0:1, :]
        lane = lax.broadcasted_iota(jnp.int32, (1, LANES), 1)
        hi, mid, lo = _split3(jnp.where(lane < NH, c - c[0:1, :], 0.0))
        pieces = (hi.astype(F32) + pltpu.roll(mid.astype(F32), 8, 1) + pltpu.roll(lo.astype(F32), 16, 1)
                  ).astype(MXU_DT)
        sub = lax.broadcasted_iota(jnp.int32, (DH, 1), 0)
        ones_k_lanes = jnp.where((lane >= BIAS_AT) & (lane < BIAS_AT + 3), 1.0, 0.0)
        ones_q_rows = jnp.where((sub >= 3) & (sub < 6), 1.0, 0.0)
        ones_k_rows = jnp.where(sub[0:QROWS - DH] < 3, 1.0, 0.0)
        first_row = jnp.where(sub == 0, 1.0, 0.0) + jnp.zeros((DH, tm), F32)
        for h in range(NH):
            pair, odd = divmod(h, 2)
            k2 = kv[:, LANES * pair:LANES * (pair + 1)]
            v2 = kv[:, DA + LANES * pair:DA + LANES * (pair + 1)]
            if odd:
                k2, v2 = pltpu.roll(k2, DH, 1), pltpu.roll(v2, DH, 1)
            ka_ref[h] = jnp.where(lane < DH, k2, _mm(pieces, pk_ref[h]) + ones_k_lanes).astype(ka_ref.dtype)
            va_ref[h] = jnp.where(lane < DH, v2, 0.0).astype(va_ref.dtype)
            qat_ref[h, 0:DH, :] = (qkvt[DH * h:DH * (h + 1)] * 0.125).astype(qat_ref.dtype)
            qat_ref[h, DH:LANES, :] = (_mm_nt(pqt_ref[h], pieces)[DH:LANES] + ones_q_rows).astype(qat_ref.dtype)
            kat_ref[h, 0:DH, :] = qkvt[DA + DH * h:DA + DH * (h + 1)].astype(kat_ref.dtype)
            kat_ref[h, DH:QROWS, :] = (_mm_nt(pkt_ref[h], pieces)[DH:QROWS] + ones_k_rows).astype(kat_ref.dtype)
            vt = qkvt[2 * DA + DH * h:2 * DA + DH * (h + 1)]
            vta_ref[h, 0:DH, :] = (first_row if odd else vt).astype(vta_ref.dtype)
            vta_ref[h, DH:LANES, :] = (vt if odd else first_row).astype(vta_ref.dtype)

    row = lambda w: pl.BlockSpec((tm, w), lambda i: (i, 0))
    full = lambda a: pl.BlockSpec(a.shape, lambda i: (0,) * a.ndim)
    nat = pl.BlockSpec((NH, tm, LANES), lambda i: (0, i, 0))
    fmaj = lambda rows: pl.BlockSpec((NH, rows, tm), lambda i: (0, 0, i))
    return pl.pallas_call(
        body, name=name, grid=(t // tm,),
        in_specs=[row(D), full(g1), full(b1), full(w_kv), full(w_qkv_t), full(w_bch), full(w_f), full(bf_pad),
                  full(pk), full(pqt), full(pkt)],
        out_specs=(nat, nat, fmaj(LANES), fmaj(QROWS), fmaj(LANES), row(3 * DCV), row(LANES),
                   pl.BlockSpec((1, 1, LANES), lambda i: (i, 0, 0))),
        out_shape=(jax.ShapeDtypeStruct((NH, t, LANES), MXU_DT), jax.ShapeDtypeStruct((NH, t, LANES), MXU_DT),
                   jax.ShapeDtypeStruct((NH, LANES, t), MXU_DT), jax.ShapeDtypeStruct((NH, QROWS, t), MXU_DT),
                   jax.ShapeDtypeStruct((NH, LANES, t), MXU_DT), jax.ShapeDtypeStruct((t, 3 * DCV), F32),
                   jax.ShapeDtypeStruct((t, LANES), F32), jax.ShapeDtypeStruct((t // tm, 1, LANES), F32)),
        scratch_shapes=[pltpu.VMEM((1, LANES), F32)],
        compiler_params=_params(56),
    )(xh1, g1, b1, w_kv, w_qkv_t, w_bch, w_f, bf_pad, pk, pqt, pkt)


def _attn_fwd(qat, ka, vta, r, name):
    t = ka.shape[1]
    tq = _tile(t, 512)
    nq = t // tq

    def body(r_ref, q_ref, k_ref, v_ref, o_ref, l_ref, st0, st1):
        hp, i = pl.program_id(0), pl.program_id(1)
        key = lax.broadcasted_iota(jnp.int32, (tq, tq), 0)
        qry = lax.broadcasted_iota(jnp.int32, (tq, tq), 1)

        def tile_of(pos):
            return jnp.where(pos == 0, i, pos - 1)

        def scores(pos, buf, masked):
            off = pl.multiple_of(tile_of(pos) * tq, tq)
            for a in range(2):
                st = _mm(k_ref[a, pl.ds(off, tq), :], q_ref[a])
                buf[a] = jnp.where(qry >= key, st, NEG) if masked else st

        def consume(pos, buf, carry):
            j = tile_of(pos)
            off = pl.multiple_of(j * tq, tq)
            out = []
            for a in range(2):
                m, acc = carry[a]
                st = buf[a]
                d = r_ref[2 * hp + a, i] - r_ref[2 * hp + a, j]
                m_new = jnp.maximum(m, jnp.max(st, axis=0, keepdims=True) + d)
                pt = jnp.exp(st - (m_new - d))
                acc = jnp.exp(m - m_new) * acc + _mm(v_ref[a, :, pl.ds(off, tq)], pt.astype(MXU_DT))
                out.append((m_new, acc))
            return tuple(out)

        def trip(p, carry):
            scores(2 * p + 1, st1, False)
            carry = consume(2 * p, st0, carry)
            scores(2 * p + 2, st0, False)
            return consume(2 * p + 1, st1, carry)

        scores(0, st0, True)
        init = tuple((jnp.full((1, tq), NEG, F32), jnp.zeros((LANES, tq), F32)) for _ in range(2))
        trips = i // 2
        carry = lax.fori_loop(0, trips, trip, init)

        def last_two(cr):
            scores(2 * trips + 1, st1, False)
            return consume(2 * trips + 1, st1, consume(2 * trips, st0, cr))

        (ma, acca), (mb, accb) = lax.cond(i % 2 == 1, last_two, lambda cr: consume(2 * trips, st0, cr), carry)
        la, lb = acca[DH:DH + 1, :], accb[0:1, :]
        l_ref[0] = ma + jnp.log(la)
        l_ref[1] = mb + jnp.log(lb)
        sub = lax.broadcasted_iota(jnp.int32, (LANES, tq), 0)
        o_ref[...] = jnp.where(sub < DH, acca / la, accb / lb).T

    return pl.pallas_call(
        body, name=name, grid=(NH // 2, nq),
        in_specs=[pl.BlockSpec(memory_space=pltpu.SMEM),
                  pl.BlockSpec((2, LANES, tq), lambda p, i: (p, 0, i)),
                  pl.BlockSpec((2, t, LANES), lambda p, i: (p, 0, 0)),
                  pl.BlockSpec((2, LANES, t), lambda p, i: (p, 0, 0))],
        out_specs=(pl.BlockSpec((tq, LANES), lambda p, i: (i, p)),
                   pl.BlockSpec((2, 1, tq), lambda p, i: (p, 0, i))),
        out_shape=(jax.ShapeDtypeStruct((t, DA), F32), jax.ShapeDtypeStruct((NH, 1, t), F32)),
        scratch_shapes=[pltpu.VMEM((2, tq, tq), F32), pltpu.VMEM((2, tq, tq), F32)],
        compiler_params=_params(48, 2),
    )(r, qat, ka, vta)


def _conv_parts(bch):
    return bch[:, 0:DCV], bch[:, DCV:2 * DCV], bch[:, 2 * DCV:3 * DCV]


def _mix_post_fwd(o, bch, conv_w, g_attn, g_conv, xh1, g1, b1, w_mo, name):
    t = o.shape[0]
    tm = _tile(t, 512)
    hb = tm // 8

    def body(o_ref, bch_ref, halo_ref, cw_ref, ga_ref, gc_ref, x_ref, g_ref, b_ref, w_ref,
             mg_ref, xh_ref, rs_ref, ext):
        i = pl.program_id(0)
        an, _ = _rms_fwd(o_ref[...])
        mg_ref[:, 0:DA] = (an * ga_ref[...]).astype(mg_ref.dtype)
        bb, cc, hh = _conv_parts(bch_ref[...])
        _, hc, hh_h = _conv_parts(halo_ref[...])
        u = cc * hh
        ext[0:8, :] = jnp.where(i > 0, hc * hh_h, 0.0)
        ext[8:8 + tm, :] = u
        raw = cw_ref[0:1, :] * ext[6:6 + tm, :] + cw_ref[1:2, :] * ext[7:7 + tm, :] + cw_ref[2:3, :] * u
        cn, _ = _rms_fwd(bb * raw)
        mg_ref[:, DA:D] = (cn * gc_ref[...]).astype(mg_ref.dtype)
        x1 = x_ref[...] * g_ref[...] + b_ref[...]
        xh, rstd = _ln_fwd(ALPHA * x1 + _mm(mg_ref[...], w_ref[...]))
        xh_ref[...] = xh
        rs_ref[...] = rstd

    row = lambda w: pl.BlockSpec((tm, w), lambda i: (i, 0))
    full = lambda a: pl.BlockSpec(a.shape, lambda i: (0, 0))
    return pl.pallas_call(
        body, name=name, grid=(t // tm,),
        in_specs=[row(DA), row(3 * DCV),
                  pl.BlockSpec((8, 3 * DCV), lambda i: (jnp.maximum(i * hb - 1, 0), 0)),
                  full(conv_w), full(g_attn), full(g_conv), row(D), full(g1), full(b1), full(w_mo)],
        out_specs=(row(D), row(D), pl.BlockSpec((tm, 1), lambda i: (i, 0))),
        out_shape=(jax.ShapeDtypeStruct((t, D), MXU_DT), jax.ShapeDtypeStruct((t, D), F32),
                   jax.ShapeDtypeStruct((t, 1), F32)),
        scratch_shapes=[pltpu.VMEM((tm + 8, DCV), F32)],
        compiler_params=_params(48),
    )(o, bch, bch, conv_w, g_attn, g_conv, xh1, g1, b1, w_mo)


def _tail(xh3, rs3, g3, b3, p, w_g, w_ple, bg, g4, b4, target, name):
    t = xh3.shape[0]
    tm = _tile(t, 512)

    def body(x_ref, rs_ref, g3_ref, b3_ref, p_ref, wg_ref, wp_ref, bg_ref, g4_ref, b4_ref, t_ref,
             dr_ref, dz_ref, de_ref, st_ref):
        @pl.when(pl.program_id(0) == 0)
        def _():
            st_ref[...] = jnp.zeros_like(st_ref)

        xh3v = x_ref[...]
        x3 = xh3v * g3_ref[...] + b3_ref[...]
        gate = _sigmoid(_mm(x3.astype(MXU_DT), wg_ref[...]) + bg_ref[...])
        e = _mm(p_ref[...].astype(MXU_DT), wp_ref[...])
        xh4, rstd4 = _ln_fwd(ALPHA * x3 + gate * e)
        diff = xh4 * g4_ref[...] + b4_ref[...] - t_ref[...]
        dy = diff * (1.0 / D)
        st_ref[5:6, :] += _colsum(diff * diff)
        st_ref[0:1, :] += _colsum(dy * xh4)
        st_ref[1:2, :] += _colsum(dy)
        dr4 = _ln_bwd(dy * g4_ref[...], xh4, rstd4)
        de_ref[...] = (dr4 * gate).astype(de_ref.dtype)
        dz = dr4 * e * gate * (1.0 - gate)
        st_ref[2:3, :] += _colsum(dz)
        dzb = dz.astype(MXU_DT)
        dz_ref[...] = dzb
        dx3 = ALPHA * dr4 + _mm_nt(dzb, wg_ref[...])
        st_ref[3:4, :] += _colsum(dx3 * xh3v)
        st_ref[4:5, :] += _colsum(dx3)
        dr_ref[...] = _ln_bwd(dx3 * g3_ref[...], xh3v, rs_ref[...])

    row = lambda w: pl.BlockSpec((tm, w), lambda i: (i, 0))
    full = lambda a: pl.BlockSpec(a.shape, lambda i: (0, 0))
    return pl.pallas_call(
        body, name=name, grid=(t // tm,),
        in_specs=[row(D), row(1), full(g3), full(b3), row(PLE), full(w_g), full(w_ple), full(bg), full(g4),
                  full(b4), row(D)],
        out_specs=(row(D), row(D), row(D), pl.BlockSpec((8, D), lambda i: (0, 0))),
        out_shape=(jax.ShapeDtypeStruct((t, D), F32), jax.ShapeDtypeStruct((t, D), MXU_DT),
                   jax.ShapeDtypeStruct((t, D), MXU_DT), jax.ShapeDtypeStruct((8, D), F32)),
        compiler_params=_params(48),
    )(xh3, rs3, g3, b3, p, w_g, w_ple, bg, g4, b4, target)


def _ffn_bwd(dr, gact, uact, xin, rsin, gin, w_in, w_out, prev_ln, name, exchange=()):
    t = dr.shape[0]
    tm = _tile(t, 512)
    nt = t // tm
    chunks = _f_chunks()
    ne = len(exchange)

    def body(*refs):
        dr_ref, g_ref, u_ref, x_ref, rs_ref, gi_ref, win_hbm, wout_hbm = refs[:8]
        df_ref, dg_ref, du_ref, dx_ref, st_ref = refs[8 + ne:13 + ne]
        win_v, wout_v = refs[13 + 2 * ne:15 + 2 * ne]
        acc_ref = dx_ref
        if ne:
            e_start, e_finish = _exchange_phases(refs[8:8 + ne], refs[13 + ne:13 + 2 * ne], *refs[15 + 2 * ne:])

        @pl.when(pl.program_id(0) == 0)
        def _():
            if ne:
                e_start()
            pltpu.sync_copy(win_hbm, win_v)
            pltpu.sync_copy(wout_hbm, wout_v)
            st_ref[...] = jnp.zeros_like(st_ref)

        if ne:
            pl.when(pl.program_id(0) == nt - 1)(e_finish)

        drv = dr_ref[...]
        dfb = (0.5 * drv).astype(MXU_DT)
        df_ref[...] = dfb
        for ci, (c0, fc) in enumerate(chunks):
            dh = _mm_nt(dfb, wout_v[c0:c0 + fc, :])
            g = g_ref[:, c0:c0 + fc].astype(F32)
            u = u_ref[:, c0:c0 + fc].astype(F32)
            sg = _sigmoid(g)
            dgb = (dh * u * (sg * (1.0 + g * (1.0 - sg)))).astype(MXU_DT)
            dub = (dh * (g * sg)).astype(MXU_DT)
            dg_ref[:, c0:c0 + fc] = dgb
            du_ref[:, c0:c0 + fc] = dub
            part = _mm_nt(dgb, win_v[:, c0:c0 + fc]) + _mm_nt(dub, win_v[:, F + c0:F + c0 + fc])
            if ci == 0:
                acc_ref[...] = part
            else:
                acc_ref[...] += part
        dx = ALPHA * drv + acc_ref[...]
        if prev_ln:
            xh = x_ref[...]
            st_ref[0:1, :] += _colsum(dx * xh)
            st_ref[1:2, :] += _colsum(dx)
            dx_ref[...] = _ln_bwd(dx * gi_ref[...], xh, rs_ref[...])
        else:
            dx_ref[...] = dx

    row = pl.BlockSpec((tm, D), lambda i: (i, 0))
    vec = pl.BlockSpec((1, D), lambda i: (0, 0))
    act = pl.BlockSpec((tm, F), lambda i: (i, 0))
    return pl.pallas_call(
        body, name=name, grid=(nt,),
        in_specs=[row, act, act, row, pl.BlockSpec((tm, 1), lambda i: (i, 0)), vec, ANY, ANY] + [ANY] * ne,
        out_specs=(row, act, act, row, pl.BlockSpec((8, D), lambda i: (0, 0))) + (ANY,) * ne,
        out_shape=(jax.ShapeDtypeStruct((t, D), MXU_DT), jax.ShapeDtypeStruct((t, F), MXU_DT),
                   jax.ShapeDtypeStruct((t, F), MXU_DT), jax.ShapeDtypeStruct((t, D), F32),
                   jax.ShapeDtypeStruct((8, D), F32))
        + tuple(jax.ShapeDtypeStruct(a.shape, a.dtype) for a in exchange),
        scratch_shapes=[pltpu.VMEM((D, 2 * F), MXU_DT), pltpu.VMEM((F, D), MXU_DT)]
        + (_exchange_sems(ne) if ne else []),
        compiler_params=_params(60),
    )(dr, gact, uact, xin, rsin, gin, w_in, w_out, *exchange)


def _mix_post_bwd(dr2, o, bch, conv_w, g_attn, g_conv, w_mo, name):
    t = dr2.shape[0]
    tm = _tile(t, 512)
    hb = tm // 8

    def body(dr_ref, o_ref, bch_ref, halo_ref, cw_ref, ga_ref, gc_ref, w_ref,
             dm_ref, do_ref, dl_ref, dy_ref, st_ref, ext):
        i = pl.program_id(0)

        @pl.when(i == 0)
        def _():
            st_ref[...] = jnp.zeros_like(st_ref)

        dmb = dr_ref[...].astype(MXU_DT)
        dm_ref[...] = dmb
        dmg = _mm_nt(dmb, w_ref[...])
        ov = o_ref[...]
        an, ra = _rms_fwd(ov)
        da = dmg[:, 0:DA]
        st_ref[0:1, :] += _colsum(da * an)
        dxa = _rms_bwd(da * ga_ref[...], an, ra)
        dor = dxa.astype(MXU_DT).astype(F32)
        dot = dor.T
        for h in range(NH):
            do_ref[h, 0:DH, :] = dot[DH * h:DH * (h + 1)].astype(do_ref.dtype)
            do_ref[h, DH:LANES, :] = jnp.zeros((LANES - DH, tm), do_ref.dtype)
        srow = lax.broadcasted_iota(jnp.int32, (8, DA), 0)
        scol = lax.broadcasted_iota(jnp.int32, (8, DA), 1)
        sel = jnp.where((scol // DH) == srow, 1.0, 0.0).astype(MXU_DT)
        hi, mid, lo = _split3(dor * ov)
        delta = _mm_nt(sel, hi) + _mm_nt(sel, mid) + _mm_nt(sel, lo)
        for h in range(NH):
            dl_ref[h] = delta[h:h + 1, :]
        bb, cc, hh = _conv_parts(bch_ref[...])
        _, hc, hh_h = _conv_parts(halo_ref[...])
        u = cc * hh
        ext[0:8, :] = jnp.where(i > 0, hc * hh_h, 0.0)
        ext[8:8 + tm, :] = u
        raw = cw_ref[0:1, :] * ext[6:6 + tm, :] + cw_ref[1:2, :] * ext[7:7 + tm, :] + cw_ref[2:3, :] * u
        cn, rc = _rms_fwd(bb * raw)
        dcn = dmg[:, DA:D]
        st_ref[1:2, :] += _colsum(dcn * cn)
        dy_ref[...] = _rms_bwd(dcn * gc_ref[...], cn, rc)

    row = lambda w: pl.BlockSpec((tm, w), lambda i: (i, 0))
    full = lambda a: pl.BlockSpec(a.shape, lambda i: (0, 0))
    return pl.pallas_call(
        body, name=name, grid=(t // tm,),
        in_specs=[row(D), row(DA), row(3 * DCV),
                  pl.BlockSpec((8, 3 * DCV), lambda i: (jnp.maximum(i * hb - 1, 0), 0)),
                  full(conv_w), full(g_attn), full(g_conv), full(w_mo)],
        out_specs=(row(D), pl.BlockSpec((NH, LANES, tm), lambda i: (0, 0, i)),
                   pl.BlockSpec((NH, 1, tm), lambda i: (0, 0, i)), row(DCV),
                   pl.BlockSpec((8, DA), lambda i: (0, 0))),
        out_shape=(jax.ShapeDtypeStruct((t, D), MXU_DT), jax.ShapeDtypeStruct((NH, LANES, t), MXU_DT),
                   jax.ShapeDtypeStruct((NH, 1, t), F32), jax.ShapeDtypeStruct((t, DCV), F32),
                   jax.ShapeDtypeStruct((8, DA), F32)),
        scratch_shapes=[pltpu.VMEM((tm + 8, DCV), F32)],
        compiler_params=_params(48),
    )(dr2, o, bch, bch, conv_w, g_attn, g_conv, w_mo)


def _conv_bwd(dy, bch, conv_w, name):
    t = dy.shape[0]
    tm = _tile(t, 512)
    hb = tm // 8
    nt = t // tm

    def body(dy_ref, dyn_ref, bch_ref, prev_ref, next_ref, cw_ref, out_ref, st_ref, ext_u, ext_d):
        i = pl.program_id(0)

        @pl.when(i == 0)
        def _():
            st_ref[...] = jnp.zeros_like(st_ref)

        bb, cc, hh = _conv_parts(bch_ref[...])
        _, pc, ph = _conv_parts(prev_ref[...])
        nb, _, _ = _conv_parts(next_ref[...])
        u = cc * hh
        ext_u[0:8, :] = jnp.where(i > 0, pc * ph, 0.0)
        ext_u[8:8 + tm, :] = u
        u1 = ext_u[7:7 + tm, :]
        u2 = ext_u[6:6 + tm, :]
        w0, w1, w2 = cw_ref[0:1, :], cw_ref[1:2, :], cw_ref[2:3, :]
        dyv = dy_ref[...]
        out_ref[:, 0:DCV] = (dyv * (w0 * u2 + w1 * u1 + w2 * u)).astype(out_ref.dtype)
        dcr = dyv * bb
        ext_d[0:tm, :] = dcr
        ext_d[tm:tm + 8, :] = jnp.where(i < nt - 1, dyn_ref[...] * nb, 0.0)
        du = w2 * dcr + w1 * ext_d[1:1 + tm, :] + w0 * ext_d[2:2 + tm, :]
        out_ref[:, DCV:2 * DCV] = (du * hh).astype(out_ref.dtype)
        out_ref[:, 2 * DCV:3 * DCV] = (du * cc).astype(out_ref.dtype)
        st_ref[0:1, :] += _colsum(dcr * u2)
        st_ref[1:2, :] += _colsum(dcr * u1)
        st_ref[2:3, :] += _colsum(dcr * u)

    row = lambda w: pl.BlockSpec((tm, w), lambda i: (i, 0))
    prev = lambda w: pl.BlockSpec((8, w), lambda i: (jnp.maximum(i * hb - 1, 0), 0))
    nxt = lambda w: pl.BlockSpec((8, w), lambda i: (jnp.minimum((i + 1) * hb, nt * hb - 1), 0))
    return pl.pallas_call(
        body, name=name, grid=(nt,),
        in_specs=[row(DCV), nxt(DCV), row(3 * DCV), prev(3 * DCV), nxt(3 * DCV),
                  pl.BlockSpec(conv_w.shape, lambda i: (0, 0))],
        out_specs=(row(3 * DCV), pl.BlockSpec((8, DCV), lambda i: (0, 0))),
        out_shape=(jax.ShapeDtypeStruct((t, 3 * DCV), MXU_DT), jax.ShapeDtypeStruct((8, DCV), F32)),
        scratch_shapes=[pltpu.VMEM((tm + 8, DCV), F32), pltpu.VMEM((tm + 8, DCV), F32)],
        compiler_params=_params(48),
    )(dy, dy, bch, bch, bch, conv_w)


def _attn_bwd(ka, kat, va, qat, dot, lrow, drow, r, name):
    t = ka.shape[1]
    tq = _tile(t, 512)
    nq = t // tq

    def body(r_ref, ka_ref, kat_ref, va_ref, l_ref, dl_ref, qat_v, dot_v,
             dk_ref, dv_ref, dck_ref, dqt_hbm, dcq_hbm, dq_acc):
        hp, j = pl.program_id(0), pl.program_id(1)

        @pl.when(j == 0)
        def _():
            dq_acc[...] = jnp.zeros_like(dq_acc)

        key = lax.broadcasted_iota(jnp.int32, (tq, tq), 0)
        qry = lax.broadcasted_iota(jnp.int32, (tq, tq), 1)

        def step(i, carry, masked):
            off = pl.multiple_of(i * tq, tq)
            out = []
            for a in range(2):
                dk, dv = carry[a]
                st = _mm(ka_ref[a], qat_v[a, :, pl.ds(off, tq)])
                dpt = _mm(va_ref[a], dot_v[a, :, pl.ds(off, tq)])
                if masked:
                    st = jnp.where(qry >= key, st, NEG)
                d = r_ref[2 * hp + a, i] - r_ref[2 * hp + a, j]
                pt = jnp.exp(st - (l_ref[a, :, pl.ds(off, tq)] - d))
                dsb = (pt * (dpt - dl_ref[a, :, pl.ds(off, tq)])).astype(MXU_DT)
                dv = dv + _mm_nt(dot_v[a, 0:DH, pl.ds(off, tq)], pt.astype(MXU_DT))
                dk = dk + _mm_nt(qat_v[a, 0:QROWS, pl.ds(off, tq)], dsb)
                dq_acc[a, :, pl.ds(off, tq)] += _mm(kat_ref[a], dsb)
                out.append((dk, dv))
            return tuple(out)

        init = tuple((jnp.zeros((QROWS, tq), F32), jnp.zeros((DH, tq), F32)) for _ in range(2))
        carry = step(j, init, True)
        (dka, dva), (dkb, dvb) = lax.fori_loop(j + 1, nq, lambda i, cr: step(i, cr, False), carry)
        dk_ref[...] = jnp.concatenate([dka[0:DH], dkb[0:DH]], axis=0).T.astype(dk_ref.dtype)
        dv_ref[...] = jnp.concatenate([dva, dvb], axis=0).T.astype(dv_ref.dtype)
        dck_ref[0] = -dka[DH + 3:DH + 4, :]
        dck_ref[1] = -dkb[DH + 3:DH + 4, :]

        @pl.when(j == nq - 1)
        def _():
            pltpu.sync_copy(dq_acc, dqt_hbm.at[pl.ds(2 * hp, 2)])
            pltpu.sync_copy(dq_acc.at[:, DH:DH + 1, :], dcq_hbm.at[pl.ds(2 * hp, 2)])

    pair = lambda rows, cols: pl.BlockSpec((2, rows, cols), lambda p, j: (p, 0, 0))
    return pl.pallas_call(
        body, name=name, grid=(NH // 2, nq),
        in_specs=[pl.BlockSpec(memory_space=pltpu.SMEM),
                  pl.BlockSpec((2, tq, LANES), lambda p, j: (p, j, 0)),
                  pl.BlockSpec((2, QROWS, tq), lambda p, j: (p, 0, j)),
                  pl.BlockSpec((2, tq, LANES), lambda p, j: (p, j, 0)),
                  pair(1, t), pair(1, t), pair(LANES, t), pair(LANES, t)],
        out_specs=(pl.BlockSpec((tq, LANES), lambda p, j: (j, p)),
                   pl.BlockSpec((tq, LANES), lambda p, j: (j, p)),
                   pl.BlockSpec((2, 1, tq), lambda p, j: (p, 0, j)), ANY, ANY),
        out_shape=(jax.ShapeDtypeStruct((t, DA), MXU_DT), jax.ShapeDtypeStruct((t, DA), MXU_DT),
                   jax.ShapeDtypeStruct((NH, 1, t), F32), jax.ShapeDtypeStruct((NH, QROWS, t), F32),
                   jax.ShapeDtypeStruct((NH, 1, t), F32)),
        scratch_shapes=[pltpu.VMEM((2, QROWS, t), F32)],
        compiler_params=_params(52, 2),
    )(r, ka, kat, va, lrow, drow, qat, dot)


def _mix_proj_bwd(dr2, dqt, dk, dv, dbch, dc, z, xh1, rs1, g1, w_qkv, w_bch, w_f, name):
    t = dr2.shape[0]
    tm = _tile(t, 512)
    nt = t // tm

    def body(dr_ref, dqt_ref, dk_ref, dv_ref, db_ref, dc_ref, z_ref, x_ref, rs_ref, g_ref,
             wq_ref, wb_ref, wf_ref, out_ref, df_ref, dq_ref, st_ref, carry):
        @pl.when(pl.program_id(0) == 0)
        def _():
            carry[...] = jnp.zeros_like(carry)
            st_ref[...] = jnp.zeros_like(st_ref)

        dq_ref[...] = (jnp.concatenate([dqt_ref[h, 0:DH, :] for h in range(NH)], axis=0).T * 0.125
                       ).astype(dq_ref.dtype)

        row = lax.broadcasted_iota(jnp.int32, (tm, tm), 0)
        col = lax.broadcasted_iota(jnp.int32, (tm, tm), 1)
        triu = jnp.where(col >= row, 1.0, 0.0).astype(MXU_DT)
        dlogf = carry[...] + _mm_sel(triu, dc_ref[...])
        carry[...] = dlogf[0:1, :]
        dz = dlogf / (1.0 + jnp.exp(z_ref[...]))
        st_ref[2:3, 0:LANES] += _colsum(dz)
        dfb = dz.astype(MXU_DT)
        df_ref[...] = dfb
        dx = (ALPHA * dr_ref[...]
              + _mm_nt(dq_ref[...], wq_ref[:, 0:DA])
              + _mm_nt(dk_ref[...], wq_ref[:, DA:2 * DA])
              + _mm_nt(dv_ref[...], wq_ref[:, 2 * DA:3 * DA])
              + _mm_nt(db_ref[...], wb_ref[...])
              + _mm_nt(dfb, wf_ref[...]))
        xh = x_ref[...]
        st_ref[0:1, :] += _colsum(dx * xh)
        st_ref[1:2, :] += _colsum(dx)
        out_ref[...] = _ln_bwd(dx * g_ref[...], xh, rs_ref[...])

    row = lambda w: pl.BlockSpec((tm, w), lambda i: (nt - 1 - i, 0))
    full = lambda a: pl.BlockSpec(a.shape, lambda i: (0, 0))
    return pl.pallas_call(
        body, name=name, grid=(nt,),
        in_specs=[row(D), pl.BlockSpec((NH, QROWS, tm), lambda i: (0, 0, nt - 1 - i)), row(DA), row(DA),
                  row(3 * DCV), row(LANES), row(LANES), row(D), row(1),
                  full(g1), full(w_qkv), full(w_bch), full(w_f)],
        out_specs=(row(D), row(LANES), row(DA), pl.BlockSpec((8, D), lambda i: (0, 0))),
        out_shape=(jax.ShapeDtypeStruct((t, D), F32), jax.ShapeDtypeStruct((t, LANES), MXU_DT),
                   jax.ShapeDtypeStruct((t, DA), MXU_DT), jax.ShapeDtypeStruct((8, D), F32)),
        scratch_shapes=[pltpu.VMEM((1, LANES), F32)],
        compiler_params=_params(48),
    )(dr2, dqt, dk, dv, dbch, dc, z, xh1, rs1, g1, w_qkv, w_bch, w_f)


def _dw(mode, a_parts, b, m, n, name, tmm=None, tn=None, exchange=()):
    t = b.shape[0]
    tmm = tmm or m
    tn = tn or n
    tt = _tile(t, 2048)
    na, ne = len(a_parts), len(exchange)
    grid = (m // tmm, n // tn, t // tt)

    def body(*refs):
        a_refs, b_ref, o_ref = refs[:na], refs[na], refs[na + 1 + ne]
        if ne:
            e_start, e_finish = _exchange_phases(refs[na + 1:na + 1 + ne], refs[na + 2 + ne:na + 2 + 2 * ne],
                                                 *refs[na + 2 + 2 * ne:])
            at = lambda steps: functools.reduce(jnp.logical_and, [pl.program_id(d) == s for d, s in enumerate(steps)])
            pl.when(at((0, 0, 0)))(e_start)

        @pl.when(pl.program_id(2) == 0)
        def _():
            o_ref[...] = jnp.zeros_like(o_ref)

        if mode == "plain":
            a = a_refs[0][...].astype(MXU_DT)
        elif mode == "affine":
            a = (a_refs[0][...] * a_refs[1][...] + a_refs[2][...]).astype(MXU_DT)
        else:
            g = a_refs[0][...].astype(F32)
            a = (g * _sigmoid(g) * a_refs[1][...].astype(F32)).astype(MXU_DT)
        o_ref[...] += _mm_tn(a, b_ref[...].astype(MXU_DT))
        if ne:
            pl.when(at(tuple(g - 1 for g in grid)))(e_finish)

    a_tile = pl.BlockSpec((tt, tmm), lambda i, j, k: (k, i))
    a_vec = pl.BlockSpec((1, tmm), lambda i, j, k: (0, i))
    a_specs = {"plain": [a_tile], "affine": [a_tile, a_vec, a_vec], "swiglu": [a_tile, a_tile]}[mode]
    res = pl.pallas_call(
        body, name=name, grid=grid,
        in_specs=a_specs + [pl.BlockSpec((tt, tn), lambda i, j, k: (k, j))] + [ANY] * ne,
        out_specs=(pl.BlockSpec((tmm, tn), lambda i, j, k: (i, j)),) + (ANY,) * ne,
        out_shape=(jax.ShapeDtypeStruct((m, n), F32),)
        + tuple(jax.ShapeDtypeStruct(a.shape, a.dtype) for a in exchange),
        scratch_shapes=_exchange_sems(ne) if ne else [],
        compiler_params=_params(52, 3),
    )(*a_parts, b, *exchange)
    return res if ne else res[0]


def _dw_shared(a_parts, bs, name):
    xh, g, b = a_parts
    t, m = xh.shape
    tt = _tile(t, 1024)
    nb = len(bs)

    def body(*refs):
        x_ref, g_ref, b_ref = refs[:3]
        b_refs, o_refs = refs[3:3 + nb], refs[3 + nb:]

        @pl.when(pl.program_id(0) == 0)
        def _():
            for o_ref in o_refs:
                o_ref[...] = jnp.zeros_like(o_ref)

        at = (x_ref[...] * g_ref[...] + b_ref[...]).T.astype(MXU_DT)
        for b_ref, o_ref in zip(b_refs, o_refs):
            o_ref[...] += _mm(at, b_ref[...].astype(MXU_DT))

    vec = pl.BlockSpec((1, m), lambda k: (0, 0))
    return pl.pallas_call(
        body, name=name, grid=(t // tt,),
        in_specs=[pl.BlockSpec((tt, m), lambda k: (k, 0)), vec, vec]
        + [pl.BlockSpec((tt, x.shape[1]), lambda k: (k, 0)) for x in bs],
        out_specs=tuple(pl.BlockSpec((m, x.shape[1]), lambda k: (0, 0)) for x in bs),
        out_shape=tuple(jax.ShapeDtypeStruct((m, x.shape[1]), F32) for x in bs),
        compiler_params=_params(56),
    )(xh, g, b, *bs)


def _adamw(w, g, m, v):
    m = ADAM_B1 * m + (1.0 - ADAM_B1) * g
    v = ADAM_B2 * v + (1.0 - ADAM_B2) * (g * g)
    m_hat = m / (1.0 - ADAM_B1 ** ADAM_STEP)
    v_hat = v / (1.0 - ADAM_B2 ** ADAM_STEP)
    delta = -ADAM_LR * (m_hat / (jnp.sqrt(v_hat) + ADAM_EPS) + ADAM_WD * w)
    return delta, m, v


def _reduce_adamw(landed, own, w, m, v, name):
    r, c = own.shape
    tr = _tile(r, 128)

    def body(l_ref, o_ref, w_ref, m_ref, v_ref, g_out, d_out, m_out, v_out):
        me = 4 * lax.axis_index("x") + 2 * lax.axis_index("y") + lax.axis_index("c")
        g = None
        for j in range(NDEV):
            term = jnp.where(me == j, o_ref[...], l_ref[j].astype(F32))
            g = term if g is None else g + term
        g_out[...] = g
        d_out[...], m_out[...], v_out[...] = _adamw(w_ref[...], g, m_ref[...], v_ref[...])

    blk = pl.BlockSpec((tr, c), lambda i: (i, 0))
    sds = jax.ShapeDtypeStruct((r, c), F32)
    return pl.pallas_call(
        body, name=name, grid=(r // tr,),
        in_specs=[pl.BlockSpec((NDEV, tr, c), lambda i: (0, i, 0)), blk, blk, blk, blk],
        out_specs=(blk, blk, blk, blk), out_shape=(sds, sds, sds, sds),
        compiler_params=_params(40),
    )(landed, own, w, m, v)


def _sum_small(gathered, name):
    _, r, c = gathered.shape

    def body(g_ref, o_ref):
        acc = g_ref[0]
        for j in range(1, NDEV):
            acc = acc + g_ref[j]
        o_ref[...] = acc

    return pl.pallas_call(body, name=name, out_shape=jax.ShapeDtypeStruct((r, c), F32))(gathered)


def _adamw_small(g, w, m, v, name):
    def body(g_ref, w_ref, m_ref, v_ref, d_out, m_out, v_out):
        d_out[...], m_out[...], v_out[...] = _adamw(w_ref[...], g_ref[...], m_ref[...], v_ref[...])

    sds = jax.ShapeDtypeStruct(g.shape, F32)
    return pl.pallas_call(body, name=name, out_shape=(sds, sds, sds))(g, w, m, v)


def _cols_from_stack(s):
    return jnp.transpose(s, (1, 0, 2)).reshape(s.shape[1], NDEV * s.shape[2])


def _cols_to_stack(w):
    r, c = w.shape
    return jnp.transpose(w.reshape(r, NDEV, c // NDEV), (1, 0, 2))


def _rows_from_stack(s):
    return s.reshape(NDEV * s.shape[1], s.shape[2])


def _rows_to_stack(w):
    r, c = w.shape
    return w.reshape(NDEV, r // NDEV, c)


SMALL_ROWS = 16
SMALL_SLOTS = {
    "ln1_g": (0, 0, D), "ln1_b": (1, 0, D), "ln2_g": (2, 0, D), "ln2_b": (3, 0, D), "ln3_g": (4, 0, D),
    "ln3_b": (5, 0, D), "b_ple_gate": (6, 0, D), "ln4_g": (7, 0, D), "ln4_b": (8, 0, D),
    "g_attn": (9, 0, DA), "g_conv": (9, DA, DCV), "b_forget": (10, 0, NH),
}
CONVW_ROW = 11
LOSS_SLOT = (10, LANES)


def _pack_small(vals, conv_rows, loss=None):
    out = jnp.zeros((SMALL_ROWS, D), F32)
    for nm, (r, off, wd) in SMALL_SLOTS.items():
        out = out.at[r:r + 1, off:off + wd].set(vals[nm].reshape(1, wd).astype(F32))
    out = out.at[CONVW_ROW:CONVW_ROW + 3, 0:conv_rows.shape[1]].set(conv_rows.astype(F32))
    if loss is not None:
        out = out.at[LOSS_SLOT[0], LOSS_SLOT[1]].set(loss)
    return out


def _unpack_small(packed, name):
    r, off, wd = SMALL_SLOTS[name]
    return packed[r:r + 1, off:off + wd]


def kernel(x, p, ffn1_w_in, ffn1_w_out, ln1_g, ln1_b, w_mix_in, b_forget, conv_w, g_attn, g_conv, w_mix_out, ln2_g, ln2_b, ffn2_w_in, ffn2_w_out, ln3_g, ln3_b, w_ple, w_ple_gate, b_ple_gate, ln4_g, ln4_b, loss_target, m_ffn1_w_in, m_ffn1_w_out, m_ln1_g, m_ln1_b, m_w_mix_in, m_b_forget, m_conv_w, m_g_attn, m_g_conv, m_w_mix_out, m_ln2_g, m_ln2_b, m_ffn2_w_in, m_ffn2_w_out, m_ln3_g, m_ln3_b, m_w_ple, m_w_ple_gate, m_b_ple_gate, m_ln4_g, m_ln4_b, v_ffn1_w_in, v_ffn1_w_out, v_ln1_g, v_ln1_b, v_w_mix_in, v_b_forget, v_conv_w, v_g_attn, v_g_conv, v_w_mix_out, v_ln2_g, v_ln2_b, v_ffn2_w_in, v_ffn2_w_out, v_ln3_g, v_ln3_b, v_w_ple, v_w_ple_gate, v_b_ple_gate, v_ln4_g, v_ln4_b):
    args = dict(locals())
    t = x.shape[1]
    me = 4 * lax.axis_index("x") + 2 * lax.axis_index("y") + lax.axis_index("c")
    x0 = x.reshape(t, D)
    p0 = p.reshape(t, PLE)
    tgt = loss_target.reshape(t, D)

    big = ["ffn1_w_in", "ffn1_w_out", "w_mix_in", "w_mix_out", "ffn2_w_in", "ffn2_w_out", "w_ple", "w_ple_gate"]
    col_sharded = {"ffn1_w_in", "w_mix_in", "ffn2_w_in", "w_ple"}
    shard = {nm: args[nm][0] for nm in big}

    unstack = lambda nm, g: (_cols_from_stack(g) if nm in col_sharded else _rows_from_stack(g)).astype(MXU_DT)
    stack = lambda nm, g: _cols_to_stack(g) if nm in col_sharded else _rows_to_stack(g)
    wire = lambda names: [shard[nm].astype(WIRE_DT) for nm in names]
    first, later = big[:2], big[2:]

    full = {nm: unstack(nm, g) for nm, g in zip(first, _allgather(wire(first), "ag_ffn1"))}
    ffn1_out, gathered = _ffn1_fwd(x0, full, wire(later) + [conv_w[0]])
    full.update({nm: unstack(nm, g) for nm, g in zip(later, gathered)})
    cw = _cols_from_stack(gathered[len(later)])

    dr1, gw, small, loss_part = _mid_step(p0, tgt, full, cw, {nm: args[nm] for nm in SMALL_SLOTS}, ffn1_out)
    small_part = _pack_small({nm: small[nm] for nm in SMALL_SLOTS},
                             jnp.pad(small["conv_w"], ((0, 0), (0, D - DCV))), loss_part)
    stacks = {nm: stack(nm, gw[nm]) for nm in later}
    beside_bwd, beside_dw = later[1:], later[:1]
    gx, gw1, landed_bwd, landed_dw, landed_w_in = _ffn1_bwd(
        x0, dr1, ffn1_out, full, [stacks[nm].astype(WIRE_DT) for nm in beside_bwd],
        [stacks[nm].astype(WIRE_DT) for nm in beside_dw], lambda g: stack("ffn1_w_in", g).astype(WIRE_DT))
    stacks.update({nm: stack(nm, gw1[nm]) for nm in first})
    (landed_w_out,), (small_all,) = _exchange_and_gather([stacks["ffn1_w_out"].astype(WIRE_DT)], [small_part],
                                                         "rs_ffn1_out")
    landed = dict(zip(beside_bwd + beside_dw, list(landed_bwd) + list(landed_dw)),
                  ffn1_w_in=landed_w_in, ffn1_w_out=landed_w_out)
    small_g = _sum_small(small_all, "sum_small")
    loss = small_g[LOSS_SLOT[0], LOSS_SLOT[1]]

    outs = {"loss": loss, "grad_x": gx.reshape(1, t, D)}
    for nm in big:
        own = lax.dynamic_index_in_dim(stacks[nm], me, axis=0, keepdims=False)
        g, dl, mn, vn = _reduce_adamw(landed[nm], own, shard[nm], args["m_" + nm][0], args["v_" + nm][0],
                                      "adamw_" + nm)
        outs["grad_" + nm], outs["delta_" + nm], outs["new_m_" + nm], outs["new_v_" + nm] = (
            g[None], dl[None], mn[None], vn[None])
    small_names = list(SMALL_SLOTS)
    cshard = lax.dynamic_slice_in_dim(small_g[CONVW_ROW:CONVW_ROW + 3, 0:DCV], me * (DCV // NDEV), DCV // NDEV, axis=1)
    g_pack = _pack_small({nm: _unpack_small(small_g, nm) for nm in small_names}, cshard)
    packs = [_pack_small({nm: args[pre + nm] for nm in small_names}, args[pre + "conv_w"][0])
             for pre in ("", "m_", "v_")]
    d_pack, m_pack, v_pack = _adamw_small(g_pack, packs[0], packs[1], packs[2], "adamw_small")
    for key, pk in (("grad_", g_pack), ("delta_", d_pack), ("new_m_", m_pack), ("new_v_", v_pack)):
        for nm in small_names:
            outs[key + nm] = _unpack_small(pk, nm)
        outs[key + "conv_w"] = pk[CONVW_ROW:CONVW_ROW + 3, 0:DCV // NDEV][None]

    wnames = ["ffn1_w_in", "ffn1_w_out", "ln1_g", "ln1_b", "w_mix_in", "b_forget", "conv_w", "g_attn", "g_conv",
              "w_mix_out", "ln2_g", "ln2_b", "ffn2_w_in", "ffn2_w_out", "ln3_g", "ln3_b", "w_ple", "w_ple_gate",
              "b_ple_gate", "ln4_g", "ln4_b"]
    return (outs["loss"], outs["grad_x"], *[outs[pre + nm] for pre in ("grad_", "delta_", "new_m_", "new_v_")
                                            for nm in wnames])


def _ffn1_fwd(x0, full, gather=()):
    res = _ffn_fwd(x0, jnp.ones((1, D), F32), jnp.zeros((1, D), F32), full["ffn1_w_in"], full["ffn1_w_out"],
                   "ffn1_fwd", gather)
    return res[:4], res[4:]


def _ffn1_bwd(x0, dr1, ffn1_out, full, exchange=(), exchange_late=(), w_in_slots=None):
    g1a, u1a, _, rs1 = ffn1_out
    ones, zeros = jnp.ones((1, D), F32), jnp.zeros((1, D), F32)
    res = _ffn_bwd(dr1, g1a, u1a, x0, rs1, ones, full["ffn1_w_in"], full["ffn1_w_out"], False, "ffn1_bwd",
                   exchange)
    df1, dg1, du1, gx = res[:4]
    dw_g = _dw("affine", (x0, ones, zeros), dg1, D, F, "dw_ffn1_in_g", tn=F // 2, exchange=exchange_late)
    dw_g, landed_late = (dw_g[0], dw_g[1:]) if exchange_late else (dw_g, ())
    gw_in = jnp.concatenate([dw_g, _dw("affine", (x0, ones, zeros), du1, D, F, "dw_ffn1_in_u", tn=F // 2)], axis=1)
    side = () if w_in_slots is None else (w_in_slots(gw_in),)
    out = _dw("swiglu", (g1a, u1a), df1, F, D, "dw_ffn1_out", tmm=F // 2, exchange=side)
    gw_out, landed_in = (out, None) if w_in_slots is None else (out[0], out[1])
    return gx, {"ffn1_w_in": gw_in, "ffn1_w_out": gw_out}, res[5:], landed_late, landed_in


def _mid_step(p0, tgt, full, cw, sp, ffn1_out):
    g1a, u1a, xh1, rs1 = ffn1_out
    t = xh1.shape[0]
    ln1_g, ln1_b, ln2_g, ln2_b, ln3_g, ln3_b = (sp[k] for k in ("ln1_g", "ln1_b", "ln2_g", "ln2_b", "ln3_g", "ln3_b"))
    ln4_g, ln4_b, g_attn, g_conv, b_ple_gate = (sp[k] for k in ("ln4_g", "ln4_b", "g_attn", "g_conv", "b_ple_gate"))
    wmi = full["w_mix_in"]
    w_qkv = wmi[:, 0:3 * DA]
    w_f = jnp.pad(wmi[:, 3 * DA:3 * DA + NH], ((0, 0), (0, LANES - NH)))
    w_bch = wmi[:, 3 * DA + NH:]
    bf_pad = jnp.pad(sp["b_forget"], ((0, 0), (0, LANES - NH)))

    ka, va, qat, kat, vta, bch, z, rt = _mix_proj_fwd(xh1, ln1_g, ln1_b, w_qkv[:, DA:], jnp.transpose(w_qkv),
                                                      w_bch, w_f, bf_pad, "mix_proj_fwd")
    rtile = jnp.transpose(rt[:, 0, 0:NH])
    o, lse = _attn_fwd(qat, ka, vta, rtile, "attn_fwd")
    merged, xh2, rs2 = _mix_post_fwd(o, bch, cw, g_attn, g_conv, xh1, ln1_g, ln1_b, full["w_mix_out"],
                                     "mix_post_fwd")
    g2a, u2a, xh3, rs3 = _ffn_fwd(xh2, ln2_g, ln2_b, full["ffn2_w_in"], full["ffn2_w_out"], "ffn2_fwd")

    dr3, dz, de, st_tail = _tail(xh3, rs3, ln3_g, ln3_b, p0, full["w_ple_gate"], full["w_ple"], b_ple_gate,
                                 ln4_g, ln4_b, tgt, "tail")
    df2, dg2, du2, dr2, st_f2 = _ffn_bwd(dr3, g2a, u2a, xh2, rs2, ln2_g, full["ffn2_w_in"], full["ffn2_w_out"],
                                         True, "ffn2_bwd")
    dmix, dot, drow, dyc, st_post = _mix_post_bwd(dr2, o, bch, cw, g_attn, g_conv, full["w_mix_out"],
                                                  "mix_post_bwd")
    dbch, st_conv = _conv_bwd(dyc, bch, cw, "conv_bwd")
    dk, dv, dck, dqt, dcq = _attn_bwd(ka, kat, va, qat, dot, lse, drow, rtile, "attn_bwd")
    dc_pad = jnp.pad(jnp.transpose((dcq + dck).reshape(NH, t)), ((0, 0), (0, LANES - NH)))
    dr1, dfl, dq, st_proj = _mix_proj_bwd(dr2, dqt, dk, dv, dbch, dc_pad, z, xh1, rs1, ln1_g, w_qkv, w_bch, w_f,
                                          "mix_proj_bwd")

    x1p, x2p, x3p = (xh1, ln1_g, ln1_b), (xh2, ln2_g, ln2_b), (xh3, ln3_g, ln3_b)
    gw = {}
    gw["ffn2_w_in"] = jnp.concatenate(
        [_dw("affine", x2p, dg2, D, F, "dw_ffn2_in_g", tn=F // 2),
         _dw("affine", x2p, du2, D, F, "dw_ffn2_in_u", tn=F // 2)], axis=1)
    gw["ffn2_w_out"] = _dw("swiglu", (g2a, u2a), df2, F, D, "dw_ffn2_out", tmm=F // 2)
    gw["w_mix_out"] = _dw("plain", (merged,), dmix, D, D, "dw_mix_out")
    gq, gk, gv, gf, gbch = _dw_shared(x1p, (dq, dk, dv, dfl, dbch), "dw_mix_in")
    gw["w_mix_in"] = jnp.concatenate([gq, gk, gv, gf[:, 0:NH], gbch], axis=1)
    gw["w_ple_gate"] = _dw("affine", x3p, dz, D, D, "dw_ple_gate")
    gw["w_ple"] = _dw("plain", (p0,), de, PLE, D, "dw_ple")

    loss_part = (0.5 / D) * jnp.sum(st_tail[5:6, :])
    small = {"ln1_g": st_proj[0:1], "ln1_b": st_proj[1:2], "ln2_g": st_f2[0:1], "ln2_b": st_f2[1:2],
             "ln3_g": st_tail[3:4], "ln3_b": st_tail[4:5], "b_ple_gate": st_tail[2:3], "ln4_g": st_tail[0:1],
             "ln4_b": st_tail[1:2], "g_attn": st_post[0:1], "g_conv": st_post[1:2],
             "b_forget": st_proj[2:3, 0:NH], "conv_w": st_conv[0:3]}
    return dr1, gw, small, loss_part
```

```python
import functools

import jax
import jax.numpy as jnp
from jax import lax
from jax.experimental import pallas as pl
from jax.experimental.pallas import tpu as pltpu

D = 1024
F = 2816
NH = 8
DH = 64
DA = NH * DH
DCV = D - DA
PLE = 256
LN_EPS = 1e-5
RMS_EPS = 1e-6
NEG = -1e30
ALPHA = 2.0 ** 0.25
NDEV = 8
LANES = 128

ADAM_LR, ADAM_B1, ADAM_B2, ADAM_EPS, ADAM_WD, ADAM_STEP = 0.001, 0.9, 0.999, 1e-08, 0.01, 10

F32 = jnp.float32
MXU_DT = jnp.bfloat16
WIRE_DT = jnp.bfloat16

MESH_ID = pl.DeviceIdType.MESH
ANY = pl.BlockSpec(memory_space=pl.ANY)


def _params(vmem_mb, n_axes=1):
    return pltpu.CompilerParams(dimension_semantics=("arbitrary",) * n_axes,
                                vmem_limit_bytes=int(vmem_mb) << 20)


def _mm(a, b):
    return jnp.dot(a, b, preferred_element_type=F32)


def _mm_nt(a, b):
    return lax.dot_general(a, b, (((1,), (1,)), ((), ())), preferred_element_type=F32)


def _mm_tn(a, b):
    return lax.dot_general(a, b, (((0,), (0,)), ((), ())), preferred_element_type=F32)


def _split3(x):
    hi = x.astype(MXU_DT)
    r1 = x - hi.astype(F32)
    mid = r1.astype(MXU_DT)
    lo = (r1 - mid.astype(F32)).astype(MXU_DT)
    return hi, mid, lo


def _mm_sel(sel, x):
    hi, mid, lo = _split3(x)
    return _mm(sel, hi) + _mm(sel, mid) + _mm(sel, lo)


def _sigmoid(x):
    return 1.0 / (1.0 + jnp.exp(-x))


def _ln_fwd(r):
    mu = jnp.mean(r, axis=-1, keepdims=True)
    xc = r - mu
    var = jnp.mean(xc * xc, axis=-1, keepdims=True)
    rstd = lax.rsqrt(var + LN_EPS)
    return xc * rstd, rstd


def _ln_bwd(dxhat, xhat, rstd):
    m1 = jnp.mean(dxhat, axis=-1, keepdims=True)
    m2 = jnp.mean(dxhat * xhat, axis=-1, keepdims=True)
    return rstd * (dxhat - m1 - xhat * m2)


def _rms_fwd(x):
    r = lax.rsqrt(jnp.mean(x * x, axis=-1, keepdims=True) + RMS_EPS)
    return x * r, r


def _rms_bwd(dyg, xn, r):
    return r * (dyg - xn * jnp.mean(dyg * xn, axis=-1, keepdims=True))


def _colsum(x):
    return jnp.sum(x, axis=0, keepdims=True)


def _f_chunks():
    out, c0 = [], 0
    while c0 < F:
        fc = min(512, F - c0)
        out.append((c0, fc))
        c0 += fc
    return out


def _tile(t, want):
    return want if t % want == 0 and t >= want else t


def _exchange_sems(n):
    return [pltpu.SemaphoreType.DMA((n * (NDEV - 1),)), pltpu.SemaphoreType.DMA((n * (NDEV - 1),))]


def _exchange_phases(ins, outs, send_sems, recv_sems):
    n = len(ins)

    def peers():
        x, y, c = lax.axis_index("x"), lax.axis_index("y"), lax.axis_index("c")
        out = []
        for k in range(1, NDEV):
            px = 1 - x if (k >> 2) & 1 else x
            py = 1 - y if (k >> 1) & 1 else y
            pc = 1 - c if k & 1 else c
            out.append(((px, py, pc), 4 * px + 2 * py + pc))
        return 4 * x + 2 * y + c, out

    def remote(w, k, to, slot_src, slot_dst):
        return pltpu.make_async_remote_copy(
            src_ref=ins[w].at[slot_src], dst_ref=outs[w].at[slot_dst],
            send_sem=send_sems.at[w * (NDEV - 1) + k], recv_sem=recv_sems.at[w * (NDEV - 1) + k],
            device_id=to, device_id_type=MESH_ID)

    def start():
        me, prs = peers()
        for k, (to, pid) in enumerate(prs):
            for w in range(n):
                remote(w, k, to, pid, me).start()

    def finish():
        me, prs = peers()
        for k, (to, pid) in enumerate(prs):
            for w in range(n):
                remote(w, k, to, me, pid).wait_recv()
        for k, (to, pid) in enumerate(prs):
            for w in range(n):
                remote(w, k, to, pid, me).wait_send()

    return start, finish


def _gather_sems(n):
    return [pltpu.SemaphoreType.DMA((n * (NDEV - 1),)), pltpu.SemaphoreType.DMA((n * (NDEV - 1),)),
            pltpu.SemaphoreType.DMA((n,))]


def _gather_phases(ins, outs, send_sems, recv_sems, loc_sems):
    n = len(ins)
    per = NDEV - 1

    def place():
        x, y, c = lax.axis_index("x"), lax.axis_index("y"), lax.axis_index("c")
        return (x, y, c), (x, y, 1 - c), [(1 - x, y), (x, 1 - y), (1 - x, 1 - y)]

    def copy(w, k, block, to, src=None):
        dst = outs[w].at[4 * block[0] + 2 * block[1] + block[2]]
        return pltpu.make_async_remote_copy(
            src_ref=dst if src is None else src, dst_ref=dst,
            send_sem=send_sems.at[w * per + k], recv_sem=recv_sems.at[w * per + k],
            device_id=to, device_id_type=MESH_ID)

    def local(w, me):
        return pltpu.make_async_copy(ins[w], outs[w].at[4 * me[0] + 2 * me[1] + me[2]], loc_sems.at[w])

    def first(me, sib, chips):
        out = []
        for j, chip in enumerate(chips):
            out += [copy(w, 1 + j, me, (*chip, me[2]), src=ins[w]) for w in range(n)]
        return out + [copy(w, 0, me, sib, src=ins[w]) for w in range(n)]

    def start():
        me, sib, chips = place()
        for w in range(n):
            local(w, me).start()
        for cp in first(me, sib, chips):
            cp.start()

    def forward():
        me, sib, chips = place()
        for j, chip in enumerate(chips):
            for w in range(n):
                copy(w, 1 + j, (*chip, me[2]), me).wait_recv()
                copy(w, 4 + j, (*chip, me[2]), sib).start()

    def finish():
        me, sib, chips = place()
        for w in range(n):
            copy(w, 0, sib, me).wait_recv()
        for j, chip in enumerate(chips):
            for w in range(n):
                copy(w, 4 + j, (*chip, 1 - me[2]), me).wait_recv()
        for cp in first(me, sib, chips):
            cp.wait_send()
        for j, chip in enumerate(chips):
            for w in range(n):
                copy(w, 4 + j, (*chip, me[2]), sib).wait_send()
        for w in range(n):
            local(w, me).wait()

    return start, forward, finish


def _allgather(arrs, name):
    n = len(arrs)

    def body(*refs):
        start, forward, finish = _gather_phases(refs[:n], refs[n:2 * n], *refs[2 * n:])
        start()
        forward()
        finish()

    return pl.pallas_call(
        body, name=name, out_shape=tuple(jax.ShapeDtypeStruct((NDEV,) + a.shape, a.dtype) for a in arrs),
        in_specs=[ANY] * n, out_specs=tuple([ANY] * n), scratch_shapes=_gather_sems(n),
    )(*arrs)


def _exchange_and_gather(ex, ga, name):
    ne, ng = len(ex), len(ga)

    def body(*refs):
        ins, outs, sems = refs[:ne + ng], refs[ne + ng:2 * (ne + ng)], refs[2 * (ne + ng):]
        e_start, e_finish = _exchange_phases(ins[:ne], outs[:ne], *sems[:2])
        g_start, g_forward, g_finish = _gather_phases(ins[ne:], outs[ne:], *sems[2:])
        e_start()
        g_start()
        g_forward()
        g_finish()
        e_finish()

    res = pl.pallas_call(
        body, name=name,
        out_shape=tuple(jax.ShapeDtypeStruct(a.shape, a.dtype) for a in ex)
        + tuple(jax.ShapeDtypeStruct((NDEV,) + a.shape, a.dtype) for a in ga),
        in_specs=[ANY] * (ne + ng), out_specs=tuple([ANY] * (ne + ng)),
        scratch_shapes=_exchange_sems(ne) + _gather_sems(ng),
    )(*ex, *ga)
    return res[:ne], res[ne:]


def _ffn_fwd(xin, gin, bin_, w_in, w_out, name, gather=()):
    t = xin.shape[0]
    tm = _tile(t, 512)
    nt = t // tm
    chunks = _f_chunks()
    ng = len(gather)

    def body(*refs):
        x_ref, gi_ref, bi_ref, win_hbm, wout_hbm = refs[:5]
        g_ref, u_ref, xh_ref, rs_ref = refs[5 + ng:9 + ng]
        win_v, wout_v, acc_ref = refs[9 + 2 * ng:12 + 2 * ng]
        if ng:
            g_start, g_forward, g_finish = _gather_phases(refs[5:5 + ng], refs[9 + ng:9 + 2 * ng],
                                                          *refs[12 + 2 * ng:])

        @pl.when(pl.program_id(0) == 0)
        def _():
            if ng:
                g_start()
            pltpu.sync_copy(win_hbm, win_v)
            pltpu.sync_copy(wout_hbm, wout_v)

        if ng:
            pl.when(pl.program_id(0) == nt // 2)(g_forward)
            pl.when(pl.program_id(0) == nt - 1)(g_finish)

        x = x_ref[...] * gi_ref[...] + bi_ref[...]
        xb = x.astype(MXU_DT)
        for ci, (c0, fc) in enumerate(chunks):
            gc = _mm(xb, win_v[:, c0:c0 + fc])
            uc = _mm(xb, win_v[:, F + c0:F + c0 + fc])
            g_ref[:, c0:c0 + fc] = gc.astype(g_ref.dtype)
            u_ref[:, c0:c0 + fc] = uc.astype(u_ref.dtype)
            hc = (gc * _sigmoid(gc) * uc).astype(MXU_DT)
            part = _mm(hc, wout_v[c0:c0 + fc, :])
            if ci == 0:
                acc_ref[...] = part
            else:
                acc_ref[...] += part
        xh, rstd = _ln_fwd(ALPHA * x + 0.5 * acc_ref[...])
        xh_ref[...] = xh
        rs_ref[...] = rstd

    row = pl.BlockSpec((tm, D), lambda i: (i, 0))
    vec = pl.BlockSpec((1, D), lambda i: (0, 0))
    act = pl.BlockSpec((tm, F), lambda i: (i, 0))
    return pl.pallas_call(
        body, name=name, grid=(nt,),
        in_specs=[row, vec, vec, ANY, ANY] + [ANY] * ng,
        out_specs=(act, act, row, pl.BlockSpec((tm, 1), lambda i: (i, 0))) + (ANY,) * ng,
        out_shape=(jax.ShapeDtypeStruct((t, F), MXU_DT), jax.ShapeDtypeStruct((t, F), MXU_DT),
                   jax.ShapeDtypeStruct((t, D), F32), jax.ShapeDtypeStruct((t, 1), F32))
        + tuple(jax.ShapeDtypeStruct((NDEV,) + a.shape, a.dtype) for a in gather),
        scratch_shapes=[pltpu.VMEM((D, 2 * F), MXU_DT), pltpu.VMEM((F, D), MXU_DT), pltpu.VMEM((tm, D), F32)]
        + (_gather_sems(ng) if ng else []),
        compiler_params=_params(52),
    )(xin, gin, bin_, w_in, w_out, *gather)


QROWS = 80
BIAS_AT = DH


def _place_matrices():
    import numpy as np
    pk = np.zeros((NH, LANES, LANES), np.float32)
    pqt = np.zeros((NH, LANES, LANES), np.float32)
    for h in range(NH):
        for piece in range(3):
            pk[h, 8 * piece + h, BIAS_AT + 3 + piece] = -1.0
            pqt[h, BIAS_AT + piece, 8 * piece + h] = 1.0
    pkt = np.transpose(pk, (0, 2, 1))
    return tuple(jnp.asarray(m, MXU_DT) for m in (pk, pqt, pkt))


def _mix_proj_fwd(xh1, g1, b1, w_kv, w_qkv_t, w_bch, w_f, bf_pad, name):
    t = xh1.shape[0]
    tm = _tile(t, 512)
    pk, pqt, pkt = _place_matrices()

    def body(x_ref, g_ref, b_ref, wkv_ref, wt_ref, wb_ref, wf_ref, bf_ref, pk_ref, pqt_ref, pkt_ref,
             ka_ref, va_ref, qat_ref, kat_ref, vta_ref, bch_ref, z_ref, r_ref, carry):
        @pl.when(pl.program_id(0) == 0)
        def _():
            carry[...] = jnp.zeros_like(carry)

        xb = (x_ref[...] * g_ref[...] + b_ref[...]).astype(MXU_DT)
        kv = _mm(xb, wkv_ref[...])
        qkvt = _mm_nt(wt_ref[...], xb)
        bch_ref[...] = _mm(xb, wb_ref[...])
        z = _mm(xb, wf_ref[...]) + bf_ref[...]
        z_ref[...] = z
        logf = jnp.minimum(z, 0.0) - jnp.log(1.0 + jnp.exp(-jnp.abs(z)))
        row = lax.broadcasted_iota(jnp.int32, (tm, tm), 0)
        col = lax.broadcasted_iota(jnp.int32, (tm, tm), 1)
        tri = jnp.where(row >= col, 1.0, 0.0).astype(MXU_DT)
        c = carry[...] + _mm_sel(tri, logf)
        carry[...] = c[tm - 1:tm, :]
        r_ref[0] = c[0:1, :]
        lane = lax.broadcasted_iota(jnp.int32, (1, LANES), 1)
        hi, mid, lo = _split3(jnp.where(lane < NH, c - c[0:1, :], 0.0))
        pieces = (hi.astype(F32) + pltpu.roll(mid.astype(F32), 8, 1) + pltpu.roll(lo.astype(F32), 16, 1)
                  ).astype(MXU_DT)
        sub = lax.broadcasted_iota(jnp.int32, (DH, 1), 0)
        ones_k_lanes = jnp.where((lane >= BIAS_AT) & (lane < BIAS_AT + 3), 1.0, 0.0)
        ones_q_rows = jnp.where((sub >= 3) & (sub < 6), 1.0, 0.0)
        ones_k_rows = jnp.where(sub[0:QROWS - DH] < 3, 1.0, 0.0)
        first_row = jnp.where(sub == 0, 1.0, 0.0) + jnp.zeros((DH, tm), F32)
        for h in range(NH):
            pair, odd = divmod(h, 2)
            k2 = kv[:, LANES * pair:LANES * (pair + 1)]
            v2 = kv[:, DA + LANES * pair:DA + LANES * (pair + 1)]
            if odd:
                k2, v2 = pltpu.roll(k2, DH, 1), pltpu.roll(v2, DH, 1)
            ka_ref[h] = jnp.where(lane < DH, k2, _mm(pieces, pk_ref[h]) + ones_k_lanes).astype(ka_ref.dtype)
            va_ref[h] = jnp.where(lane < DH, v2, 0.0).astype(va_ref.dtype)
            qat_ref[h, 0:DH, :] = (qkvt[DH * h:DH * (h + 1)] * 0.125).astype(qat_ref.dtype)
            qat_ref[h, DH:LANES, :] = (_mm_nt(pqt_ref[h], pieces)[DH:LANES] + ones_q_rows).astype(qat_ref.dtype)
            kat_ref[h, 0:DH, :] = qkvt[DA + DH * h:DA + DH * (h + 1)].astype(kat_ref.dtype)
            kat_ref[h, DH:QROWS, :] = (_mm_nt(pkt_ref[h], pieces)[DH:QROWS] + ones_k_rows).astype(kat_ref.dtype)
            vt = qkvt[2 * DA + DH * h:2 * DA + DH * (h + 1)]
            vta_ref[h, 0:DH, :] = (first_row if odd else vt).astype(vta_ref.dtype)
            vta_ref[h, DH:LANES, :] = (vt if odd else first_row).astype(vta_ref.dtype)

    row = lambda w: pl.BlockSpec((tm, w), lambda i: (i, 0))
    full = lambda a: pl.BlockSpec(a.shape, lambda i: (0,) * a.ndim)
    nat = pl.BlockSpec((NH, tm, LANES), lambda i: (0, i, 0))
    fmaj = lambda rows: pl.BlockSpec((NH, rows, tm), lambda i: (0, 0, i))
    return pl.pallas_call(
        body, name=name, grid=(t // tm,),
        in_specs=[row(D), full(g1), full(b1), full(w_kv), full(w_qkv_t), full(w_bch), full(w_f), full(bf_pad),
                  full(pk), full(pqt), full(pkt)],
        out_specs=(nat, nat, fmaj(LANES), fmaj(QROWS), fmaj(LANES), row(3 * DCV), row(LANES),
                   pl.BlockSpec((1, 1, LANES), lambda i: (i, 0, 0))),
        out_shape=(jax.ShapeDtypeStruct((NH, t, LANES), MXU_DT), jax.ShapeDtypeStruct((NH, t, LANES), MXU_DT),
                   jax.ShapeDtypeStruct((NH, LANES, t), MXU_DT), jax.ShapeDtypeStruct((NH, QROWS, t), MXU_DT),
                   jax.ShapeDtypeStruct((NH, LANES, t), MXU_DT), jax.ShapeDtypeStruct((t, 3 * DCV), F32),
                   jax.ShapeDtypeStruct((t, LANES), F32), jax.ShapeDtypeStruct((t // tm, 1, LANES), F32)),
        scratch_shapes=[pltpu.VMEM((1, LANES), F32)],
        compiler_params=_params(56),
    )(xh1, g1, b1, w_kv, w_qkv_t, w_bch, w_f, bf_pad, pk, pqt, pkt)


def _attn_fwd(qat, ka, vta, r, name):
    t = ka.shape[1]
    tq = _tile(t, 512)
    nq = t // tq

    def body(r_ref, q_ref, k_ref, v_ref, o_ref, l_ref, st0, st1):
        hp, i = pl.program_id(0), pl.program_id(1)
        key = lax.broadcasted_iota(jnp.int32, (tq, tq), 0)
        qry = lax.broadcasted_iota(jnp.int32, (tq, tq), 1)

        def tile_of(pos):
            return jnp.where(pos == 0, i, pos - 1)

        def scores(pos, buf, masked):
            off = pl.multiple_of(tile_of(pos) * tq, tq)
            for a in range(2):
                st = _mm(k_ref[a, pl.ds(off, tq), :], q_ref[a])
                buf[a] = jnp.where(qry >= key, st, NEG) if masked else st

        def consume(pos, buf, carry):
            j = tile_of(pos)
            off = pl.multiple_of(j * tq, tq)
            out = []
            for a in range(2):
                m, acc = carry[a]
                st = buf[a]
                d = r_ref[2 * hp + a, i] - r_ref[2 * hp + a, j]
                m_new = jnp.maximum(m, jnp.max(st, axis=0, keepdims=True) + d)
                pt = jnp.exp(st - (m_new - d))
                acc = jnp.exp(m - m_new) * acc + _mm(v_ref[a, :, pl.ds(off, tq)], pt.astype(MXU_DT))
                out.append((m_new, acc))
            return tuple(out)

        def trip(p, carry):
            scores(2 * p + 1, st1, False)
            carry = consume(2 * p, st0, carry)
            scores(2 * p + 2, st0, False)
            return consume(2 * p + 1, st1, carry)

        scores(0, st0, True)
        init = tuple((jnp.full((1, tq), NEG, F32), jnp.zeros((LANES, tq), F32)) for _ in range(2))
        trips = i // 2
        carry = lax.fori_loop(0, trips, trip, init)

        def last_two(cr):
            scores(2 * trips + 1, st1, False)
            return consume(2 * trips + 1, st1, consume(2 * trips, st0, cr))

        (ma, acca), (mb, accb) = lax.cond(i % 2 == 1, last_two, lambda cr: consume(2 * trips, st0, cr), carry)
        la, lb = acca[DH:DH + 1, :], accb[0:1, :]
        l_ref[0] = ma + jnp.log(la)
        l_ref[1] = mb + jnp.log(lb)
        sub = lax.broadcasted_iota(jnp.int32, (LANES, tq), 0)
        o_ref[...] = jnp.where(sub < DH, acca / la, accb / lb).T

    return pl.pallas_call(
        body, name=name, grid=(NH // 2, nq),
        in_specs=[pl.BlockSpec(memory_space=pltpu.SMEM),
                  pl.BlockSpec((2, LANES, tq), lambda p, i: (p, 0, i)),
                  pl.BlockSpec((2, t, LANES), lambda p, i: (p, 0, 0)),
                  pl.BlockSpec((2, LANES, t), lambda p, i: (p, 0, 0))],
        out_specs=(pl.BlockSpec((tq, LANES), lambda p, i: (i, p)),
                   pl.BlockSpec((2, 1, tq), lambda p, i: (p, 0, i))),
        out_shape=(jax.ShapeDtypeStruct((t, DA), F32), jax.ShapeDtypeStruct((NH, 1, t), F32)),
        scratch_shapes=[pltpu.VMEM((2, tq, tq), F32), pltpu.VMEM((2, tq, tq), F32)],
        compiler_params=_params(48, 2),
    )(r, qat, ka, vta)


def _conv_parts(bch):
    return bch[:, 0:DCV], bch[:, DCV:2 * DCV], bch[:, 2 * DCV:3 * DCV]


def _mix_post_fwd(o, bch, conv_w, g_attn, g_conv, xh1, g1, b1, w_mo, name):
    t = o.shape[0]
    tm = _tile(t, 512)
    hb = tm // 8

    def body(o_ref, bch_ref, halo_ref, cw_ref, ga_ref, gc_ref, x_ref, g_ref, b_ref, w_ref,
             mg_ref, xh_ref, rs_ref, ext):
        i = pl.program_id(0)
        an, _ = _rms_fwd(o_ref[...])
        mg_ref[:, 0:DA] = (an * ga_ref[...]).astype(mg_ref.dtype)
        bb, cc, hh = _conv_parts(bch_ref[...])
        _, hc, hh_h = _conv_parts(halo_ref[...])
        u = cc * hh
        ext[0:8, :] = jnp.where(i > 0, hc * hh_h, 0.0)
        ext[8:8 + tm, :] = u
        raw = cw_ref[0:1, :] * ext[6:6 + tm, :] + cw_ref[1:2, :] * ext[7:7 + tm, :] + cw_ref[2:3, :] * u
        cn, _ = _rms_fwd(bb * raw)
        mg_ref[:, DA:D] = (cn * gc_ref[...]).astype(mg_ref.dtype)
        x1 = x_ref[...] * g_ref[...] + b_ref[...]
        xh, rstd = _ln_fwd(ALPHA * x1 + _mm(mg_ref[...], w_ref[...]))
        xh_ref[...] = xh
        rs_ref[...] = rstd

    row = lambda w: pl.BlockSpec((tm, w), lambda i: (i, 0))
    full = lambda a: pl.BlockSpec(a.shape, lambda i: (0, 0))
    return pl.pallas_call(
        body, name=name, grid=(t // tm,),
        in_specs=[row(DA), row(3 * DCV),
                  pl.BlockSpec((8, 3 * DCV), lambda i: (jnp.maximum(i * hb - 1, 0), 0)),
                  full(conv_w), full(g_attn), full(g_conv), row(D), full(g1), full(b1), full(w_mo)],
        out_specs=(row(D), row(D), pl.BlockSpec((tm, 1), lambda i: (i, 0))),
        out_shape=(jax.ShapeDtypeStruct((t, D), MXU_DT), jax.ShapeDtypeStruct((t, D), F32),
                   jax.ShapeDtypeStruct((t, 1), F32)),
        scratch_shapes=[pltpu.VMEM((tm + 8, DCV), F32)],
        compiler_params=_params(48),
    )(o, bch, bch, conv_w, g_attn, g_conv, xh1, g1, b1, w_mo)


def _tail(xh3, rs3, g3, b3, p, w_g, w_ple, bg, g4, b4, target, name):
    t = xh3.shape[0]
    tm = _tile(t, 512)

    def body(x_ref, rs_ref, g3_ref, b3_ref, p_ref, wg_ref, wp_ref, bg_ref, g4_ref, b4_ref, t_ref,
             dr_ref, dz_ref, de_ref, st_ref):
        @pl.when(pl.program_id(0) == 0)
        def _():
            st_ref[...] = jnp.zeros_like(st_ref)

        xh3v = x_ref[...]
        x3 = xh3v * g3_ref[...] + b3_ref[...]
        gate = _sigmoid(_mm(x3.astype(MXU_DT), wg_ref[...]) + bg_ref[...])
        e = _mm(p_ref[...].astype(MXU_DT), wp_ref[...])
        xh4, rstd4 = _ln_fwd(ALPHA * x3 + gate * e)
        diff = xh4 * g4_ref[...] + b4_ref[...] - t_ref[...]
        dy = diff * (1.0 / D)
        st_ref[5:6, :] += _colsum(diff * diff)
        st_ref[0:1, :] += _colsum(dy * xh4)
        st_ref[1:2, :] += _colsum(dy)
        dr4 = _ln_bwd(dy * g4_ref[...], xh4, rstd4)
        de_ref[...] = (dr4 * gate).astype(de_ref.dtype)
        dz = dr4 * e * gate * (1.0 - gate)
        st_ref[2:3, :] += _colsum(dz)
        dzb = dz.astype(MXU_DT)
        dz_ref[...] = dzb
        dx3 = ALPHA * dr4 + _mm_nt(dzb, wg_ref[...])
        st_ref[3:4, :] += _colsum(dx3 * xh3v)
        st_ref[4:5, :] += _colsum(dx3)
        dr_ref[...] = _ln_bwd(dx3 * g3_ref[...], xh3v, rs_ref[...])

    row = lambda w: pl.BlockSpec((tm, w), lambda i: (i, 0))
    full = lambda a: pl.BlockSpec(a.shape, lambda i: (0, 0))
    return pl.pallas_call(
        body, name=name, grid=(t // tm,),
        in_specs=[row(D), row(1), full(g3), full(b3), row(PLE), full(w_g), full(w_ple), full(bg), full(g4),
                  full(b4), row(D)],
        out_specs=(row(D), row(D), row(D), pl.BlockSpec((8, D), lambda i: (0, 0))),
        out_shape=(jax.ShapeDtypeStruct((t, D), F32), jax.ShapeDtypeStruct((t, D), MXU_DT),
                   jax.ShapeDtypeStruct((t, D), MXU_DT), jax.ShapeDtypeStruct((8, D), F32)),
        compiler_params=_params(48),
    )(xh3, rs3, g3, b3, p, w_g, w_ple, bg, g4, b4, target)


def _ffn_bwd(dr, gact, uact, xin, rsin, gin, w_in, w_out, prev_ln, name, exchange=()):
    t = dr.shape[0]
    tm = _tile(t, 512)
    nt = t // tm
    chunks = _f_chunks()
    ne = len(exchange)

    def body(*refs):
        dr_ref, g_ref, u_ref, x_ref, rs_ref, gi_ref, win_hbm, wout_hbm = refs[:8]
        df_ref, dg_ref, du_ref, dx_ref, st_ref = refs[8 + ne:13 + ne]
        win_v, wout_v = refs[13 + 2 * ne:15 + 2 * ne]
        acc_ref = dx_ref
        if ne:
            e_start, e_finish = _exchange_phases(refs[8:8 + ne], refs[13 + ne:13 + 2 * ne], *refs[15 + 2 * ne:])

        @pl.when(pl.program_id(0) == 0)
        def _():
            if ne:
                e_start()
            pltpu.sync_copy(win_hbm, win_v)
            pltpu.sync_copy(wout_hbm, wout_v)
            st_ref[...] = jnp.zeros_like(st_ref)

        if ne:
            pl.when(pl.program_id(0) == nt - 1)(e_finish)

        drv = dr_ref[...]
        dfb = (0.5 * drv).astype(MXU_DT)
        df_ref[...] = dfb
        for ci, (c0, fc) in enumerate(chunks):
            dh = _mm_nt(dfb, wout_v[c0:c0 + fc, :])
            g = g_ref[:, c0:c0 + fc].astype(F32)
            u = u_ref[:, c0:c0 + fc].astype(F32)
            sg = _sigmoid(g)
            dgb = (dh * u * (sg * (1.0 + g * (1.0 - sg)))).astype(MXU_DT)
            dub = (dh * (g * sg)).astype(MXU_DT)
            dg_ref[:, c0:c0 + fc] = dgb
            du_ref[:, c0:c0 + fc] = dub
            part = _mm_nt(dgb, win_v[:, c0:c0 + fc]) + _mm_nt(dub, win_v[:, F + c0:F + c0 + fc])
            if ci == 0:
                acc_ref[...] = part
            else:
                acc_ref[...] += part
        dx = ALPHA * drv + acc_ref[...]
        if prev_ln:
            xh = x_ref[...]
            st_ref[0:1, :] += _colsum(dx * xh)
            st_ref[1:2, :] += _colsum(dx)
            dx_ref[...] = _ln_bwd(dx * gi_ref[...], xh, rs_ref[...])
        else:
            dx_ref[...] = dx

    row = pl.BlockSpec((tm, D), lambda i: (i, 0))
    vec = pl.BlockSpec((1, D), lambda i: (0, 0))
    act = pl.BlockSpec((tm, F), lambda i: (i, 0))
    return pl.pallas_call(
        body, name=name, grid=(nt,),
        in_specs=[row, act, act, row, pl.BlockSpec((tm, 1), lambda i: (i, 0)), vec, ANY, ANY] + [ANY] * ne,
        out_specs=(row, act, act, row, pl.BlockSpec((8, D), lambda i: (0, 0))) + (ANY,) * ne,
        out_shape=(jax.ShapeDtypeStruct((t, D), MXU_DT), jax.ShapeDtypeStruct((t, F), MXU_DT),
                   jax.ShapeDtypeStruct((t, F), MXU_DT), jax.ShapeDtypeStruct((t, D), F32),
                   jax.ShapeDtypeStruct((8, D), F32))
        + tuple(jax.ShapeDtypeStruct(a.shape, a.dtype) for a in exchange),
        scratch_shapes=[pltpu.VMEM((D, 2 * F), MXU_DT), pltpu.VMEM((F, D), MXU_DT)]
        + (_exchange_sems(ne) if ne else []),
        compiler_params=_params(60),
    )(dr, gact, uact, xin, rsin, gin, w_in, w_out, *exchange)


def _mix_post_bwd(dr2, o, bch, conv_w, g_attn, g_conv, w_mo, name):
    t = dr2.shape[0]
    tm = _tile(t, 512)
    hb = tm // 8

    def body(dr_ref, o_ref, bch_ref, halo_ref, cw_ref, ga_ref, gc_ref, w_ref,
             dm_ref, do_ref, dl_ref, dy_ref, st_ref, ext):
        i = pl.program_id(0)

        @pl.when(i == 0)
        def _():
            st_ref[...] = jnp.zeros_like(st_ref)

        dmb = dr_ref[...].astype(MXU_DT)
        dm_ref[...] = dmb
        dmg = _mm_nt(dmb, w_ref[...])
        ov = o_ref[...]
        an, ra = _rms_fwd(ov)
        da = dmg[:, 0:DA]
        st_ref[0:1, :] += _colsum(da * an)
        dxa = _rms_bwd(da * ga_ref[...], an, ra)
        dor = dxa.astype(MXU_DT).astype(F32)
        dot = dor.T
        for h in range(NH):
            do_ref[h, 0:DH, :] = dot[DH * h:DH * (h + 1)].astype(do_ref.dtype)
            do_ref[h, DH:LANES, :] = jnp.zeros((LANES - DH, tm), do_ref.dtype)
        srow = lax.broadcasted_iota(jnp.int32, (8, DA), 0)
        scol = lax.broadcasted_iota(jnp.int32, (8, DA), 1)
        sel = jnp.where((scol // DH) == srow, 1.0, 0.0).astype(MXU_DT)
        hi, mid, lo = _split3(dor * ov)
        delta = _mm_nt(sel, hi) + _mm_nt(sel, mid) + _mm_nt(sel, lo)
        for h in range(NH):
            dl_ref[h] = delta[h:h + 1, :]
        bb, cc, hh = _conv_parts(bch_ref[...])
        _, hc, hh_h = _conv_parts(halo_ref[...])
        u = cc * hh
        ext[0:8, :] = jnp.where(i > 0, hc * hh_h, 0.0)
        ext[8:8 + tm, :] = u
        raw = cw_ref[0:1, :] * ext[6:6 + tm, :] + cw_ref[1:2, :] * ext[7:7 + tm, :] + cw_ref[2:3, :] * u
        cn, rc = _rms_fwd(bb * raw)
        dcn = dmg[:, DA:D]
        st_ref[1:2, :] += _colsum(dcn * cn)
        dy_ref[...] = _rms_bwd(dcn * gc_ref[...], cn, rc)

    row = lambda w: pl.BlockSpec((tm, w), lambda i: (i, 0))
    full = lambda a: pl.BlockSpec(a.shape, lambda i: (0, 0))
    return pl.pallas_call(
        body, name=name, grid=(t // tm,),
        in_specs=[row(D), row(DA), row(3 * DCV),
                  pl.BlockSpec((8, 3 * DCV), lambda i: (jnp.maximum(i * hb - 1, 0), 0)),
                  full(conv_w), full(g_attn), full(g_conv), full(w_mo)],
        out_specs=(row(D), pl.BlockSpec((NH, LANES, tm), lambda i: (0, 0, i)),
                   pl.BlockSpec((NH, 1, tm), lambda i: (0, 0, i)), row(DCV),
                   pl.BlockSpec((8, DA), lambda i: (0, 0))),
        out_shape=(jax.ShapeDtypeStruct((t, D), MXU_DT), jax.ShapeDtypeStruct((NH, LANES, t), MXU_DT),
                   jax.ShapeDtypeStruct((NH, 1, t), F32), jax.ShapeDtypeStruct((t, DCV), F32),
                   jax.ShapeDtypeStruct((8, DA), F32)),
        scratch_shapes=[pltpu.VMEM((tm + 8, DCV), F32)],
        compiler_params=_params(48),
    )(dr2, o, bch, bch, conv_w, g_attn, g_conv, w_mo)


def _conv_bwd(dy, bch, conv_w, name):
    t = dy.shape[0]
    tm = _tile(t, 512)
    hb = tm // 8
    nt = t // tm

    def body(dy_ref, dyn_ref, bch_ref, prev_ref, next_ref, cw_ref, out_ref, st_ref, ext_u, ext_d):
        i = pl.program_id(0)

        @pl.when(i == 0)
        def _():
            st_ref[...] = jnp.zeros_like(st_ref)

        bb, cc, hh = _conv_parts(bch_ref[...])
        _, pc, ph = _conv_parts(prev_ref[...])
        nb, _, _ = _conv_parts(next_ref[...])
        u = cc * hh
        ext_u[0:8, :] = jnp.where(i > 0, pc * ph, 0.0)
        ext_u[8:8 + tm, :] = u
        u1 = ext_u[7:7 + tm, :]
        u2 = ext_u[6:6 + tm, :]
        w0, w1, w2 = cw_ref[0:1, :], cw_ref[1:2, :], cw_ref[2:3, :]
        dyv = dy_ref[...]
        out_ref[:, 0:DCV] = (dyv * (w0 * u2 + w1 * u1 + w2 * u)).astype(out_ref.dtype)
        dcr = dyv * bb
        ext_d[0:tm, :] = dcr
        ext_d[tm:tm + 8, :] = jnp.where(i < nt - 1, dyn_ref[...] * nb, 0.0)
        du = w2 * dcr + w1 * ext_d[1:1 + tm, :] + w0 * ext_d[2:2 + tm, :]
        out_ref[:, DCV:2 * DCV] = (du * hh).astype(out_ref.dtype)
        out_ref[:, 2 * DCV:3 * DCV] = (du * cc).astype(out_ref.dtype)
        st_ref[0:1, :] += _colsum(dcr * u2)
        st_ref[1:2, :] += _colsum(dcr * u1)
        st_ref[2:3, :] += _colsum(dcr * u)

    row = lambda w: pl.BlockSpec((tm, w), lambda i: (i, 0))
    prev = lambda w: pl.BlockSpec((8, w), lambda i: (jnp.maximum(i * hb - 1, 0), 0))
    nxt = lambda w: pl.BlockSpec((8, w), lambda i: (jnp.minimum((i + 1) * hb, nt * hb - 1), 0))
    return pl.pallas_call(
        body, name=name, grid=(nt,),
        in_specs=[row(DCV), nxt(DCV), row(3 * DCV), prev(3 * DCV), nxt(3 * DCV),
                  pl.BlockSpec(conv_w.shape, lambda i: (0, 0))],
        out_specs=(row(3 * DCV), pl.BlockSpec((8, DCV), lambda i: (0, 0))),
        out_shape=(jax.ShapeDtypeStruct((t, 3 * DCV), MXU_DT), jax.ShapeDtypeStruct((8, DCV), F32)),
        scratch_shapes=[pltpu.VMEM((tm + 8, DCV), F32), pltpu.VMEM((tm + 8, DCV), F32)],
        compiler_params=_params(48),
    )(dy, dy, bch, bch, bch, conv_w)


def _attn_bwd(ka, kat, va, qat, dot, lrow, drow, r, name):
    t = ka.shape[1]
    tq = _tile(t, 512)
    nq = t // tq

    def body(r_ref, ka_ref, kat_ref, va_ref, l_ref, dl_ref, qat_v, dot_v,
             dk_ref, dv_ref, dck_ref, dqt_hbm, dcq_hbm, dq_acc):
        hp, j = pl.program_id(0), pl.program_id(1)

        @pl.when(j == 0)
        def _():
            dq_acc[...] = jnp.zeros_like(dq_acc)

        key = lax.broadcasted_iota(jnp.int32, (tq, tq), 0)
        qry = lax.broadcasted_iota(jnp.int32, (tq, tq), 1)

        def step(i, carry, masked):
            off = pl.multiple_of(i * tq, tq)
            out = []
            for a in range(2):
                dk, dv = carry[a]
                st = _mm(ka_ref[a], qat_v[a, :, pl.ds(off, tq)])
                dpt = _mm(va_ref[a], dot_v[a, :, pl.ds(off, tq)])
                if masked:
                    st = jnp.where(qry >= key, st, NEG)
                d = r_ref[2 * hp + a, i] - r_ref[2 * hp + a, j]
                pt = jnp.exp(st - (l_ref[a, :, pl.ds(off, tq)] - d))
                dsb = (pt * (dpt - dl_ref[a, :, pl.ds(off, tq)])).astype(MXU_DT)
                dv = dv + _mm_nt(dot_v[a, 0:DH, pl.ds(off, tq)], pt.astype(MXU_DT))
                dk = dk + _mm_nt(qat_v[a, 0:QROWS, pl.ds(off, tq)], dsb)
                dq_acc[a, :, pl.ds(off, tq)] += _mm(kat_ref[a], dsb)
                out.append((dk, dv))
            return tuple(out)

        init = tuple((jnp.zeros((QROWS, tq), F32), jnp.zeros((DH, tq), F32)) for _ in range(2))
        carry = step(j, init, True)
        (dka, dva), (dkb, dvb) = lax.fori_loop(j + 1, nq, lambda i, cr: step(i, cr, False), carry)
        dk_ref[...] = jnp.concatenate([dka[0:DH], dkb[0:DH]], axis=0).T.astype(dk_ref.dtype)
        dv_ref[...] = jnp.concatenate([dva, dvb], axis=0).T.astype(dv_ref.dtype)
        dck_ref[0] = -dka[DH + 3:DH + 4, :]
        dck_ref[1] = -dkb[DH + 3:DH + 4, :]

        @pl.when(j == nq - 1)
        def _():
            pltpu.sync_copy(dq_acc, dqt_hbm.at[pl.ds(2 * hp, 2)])
            pltpu.sync_copy(dq_acc.at[:, DH:DH + 1, :], dcq_hbm.at[pl.ds(2 * hp, 2)])

    pair = lambda rows, cols: pl.BlockSpec((2, rows, cols), lambda p, j: (p, 0, 0))
    return pl.pallas_call(
        body, name=name, grid=(NH // 2, nq),
        in_specs=[pl.BlockSpec(memory_space=pltpu.SMEM),
                  pl.BlockSpec((2, tq, LANES), lambda p, j: (p, j, 0)),
                  pl.BlockSpec((2, QROWS, tq), lambda p, j: (p, 0, j)),
                  pl.BlockSpec((2, tq, LANES), lambda p, j: (p, j, 0)),
                  pair(1, t), pair(1, t), pair(LANES, t), pair(LANES, t)],
        out_specs=(pl.BlockSpec((tq, LANES), lambda p, j: (j, p)),
                   pl.BlockSpec((tq, LANES), lambda p, j: (j, p)),
                   pl.BlockSpec((2, 1, tq), lambda p, j: (p, 0, j)), ANY, ANY),
        out_shape=(jax.ShapeDtypeStruct((t, DA), MXU_DT), jax.ShapeDtypeStruct((t, DA), MXU_DT),
                   jax.ShapeDtypeStruct((NH, 1, t), F32), jax.ShapeDtypeStruct((NH, QROWS, t), F32),
                   jax.ShapeDtypeStruct((NH, 1, t), F32)),
        scratch_shapes=[pltpu.VMEM((2, QROWS, t), F32)],
        compiler_params=_params(52, 2),
    )(r, ka, kat, va, lrow, drow, qat, dot)


def _mix_proj_bwd(dr2, dqt, dk, dv, dbch, dc, z, xh1, rs1, g1, w_qkv, w_bch, w_f, name, exchange=()):
    t = dr2.shape[0]
    tm = _tile(t, 512)
    nt = t // tm
    ne = len(exchange)

    def body(*refs):
        (dr_ref, dqt_ref, dk_ref, dv_ref, db_ref, dc_ref, z_ref, x_ref, rs_ref, g_ref,
         wq_ref, wb_ref, wf_ref) = refs[:13]
        out_ref, df_ref, dq_ref, st_ref = refs[13 + ne:17 + ne]
        carry = refs[17 + 2 * ne]
        if ne:
            e_start, e_finish = _exchange_phases(refs[13:13 + ne], refs[17 + ne:17 + 2 * ne], *refs[18 + 2 * ne:])
            pl.when(pl.program_id(0) == 0)(e_start)

        @pl.when(pl.program_id(0) == 0)
        def _():
            carry[...] = jnp.zeros_like(carry)
            st_ref[...] = jnp.zeros_like(st_ref)

        dq_ref[...] = (jnp.concatenate([dqt_ref[h, 0:DH, :] for h in range(NH)], axis=0).T * 0.125
                       ).astype(dq_ref.dtype)

        row = lax.broadcasted_iota(jnp.int32, (tm, tm), 0)
        col = lax.broadcasted_iota(jnp.int32, (tm, tm), 1)
        triu = jnp.where(col >= row, 1.0, 0.0).astype(MXU_DT)
        dlogf = carry[...] + _mm_sel(triu, dc_ref[...])
        carry[...] = dlogf[0:1, :]
        dz = dlogf / (1.0 + jnp.exp(z_ref[...]))
        st_ref[2:3, 0:LANES] += _colsum(dz)
        dfb = dz.astype(MXU_DT)
        df_ref[...] = dfb
        dx = (ALPHA * dr_ref[...]
              + _mm_nt(dq_ref[...], wq_ref[:, 0:DA])
              + _mm_nt(dk_ref[...], wq_ref[:, DA:2 * DA])
              + _mm_nt(dv_ref[...], wq_ref[:, 2 * DA:3 * DA])
              + _mm_nt(db_ref[...], wb_ref[...])
              + _mm_nt(dfb, wf_ref[...]))
        xh = x_ref[...]
        st_ref[0:1, :] += _colsum(dx * xh)
        st_ref[1:2, :] += _colsum(dx)
        out_ref[...] = _ln_bwd(dx * g_ref[...], xh, rs_ref[...])
        if ne:
            pl.when(pl.program_id(0) == nt - 1)(e_finish)

    row = lambda w: pl.BlockSpec((tm, w), lambda i: (nt - 1 - i, 0))
    full = lambda a: pl.BlockSpec(a.shape, lambda i: (0, 0))
    res = pl.pallas_call(
        body, name=name, grid=(nt,),
        in_specs=[row(D), pl.BlockSpec((NH, QROWS, tm), lambda i: (0, 0, nt - 1 - i)), row(DA), row(DA),
                  row(3 * DCV), row(LANES), row(LANES), row(D), row(1),
                  full(g1), full(w_qkv), full(w_bch), full(w_f)] + [ANY] * ne,
        out_specs=(row(D), row(LANES), row(DA), pl.BlockSpec((8, D), lambda i: (0, 0))) + (ANY,) * ne,
        out_shape=(jax.ShapeDtypeStruct((t, D), F32), jax.ShapeDtypeStruct((t, LANES), MXU_DT),
                   jax.ShapeDtypeStruct((t, DA), MXU_DT), jax.ShapeDtypeStruct((8, D), F32))
        + tuple(jax.ShapeDtypeStruct(a.shape, a.dtype) for a in exchange),
        scratch_shapes=[pltpu.VMEM((1, LANES), F32)] + (_exchange_sems(ne) if ne else []),
        compiler_params=_params(48),
    )(dr2, dqt, dk, dv, dbch, dc, z, xh1, rs1, g1, w_qkv, w_bch, w_f, *exchange)
    return res[:4] + (res[4:],)


def _dw(mode, a_parts, b, m, n, name, tmm=None, tn=None, exchange=()):
    t = b.shape[0]
    tmm = tmm or m
    tn = tn or n
    tt = _tile(t, 2048)
    na, ne = len(a_parts), len(exchange)
    grid = (m // tmm, n // tn, t // tt)

    def body(*refs):
        a_refs, b_ref, o_ref = refs[:na], refs[na], refs[na + 1 + ne]
        if ne:
            e_start, e_finish = _exchange_phases(refs[na + 1:na + 1 + ne], refs[na + 2 + ne:na + 2 + 2 * ne],
                                                 *refs[na + 2 + 2 * ne:])
            at = lambda steps: functools.reduce(jnp.logical_and, [pl.program_id(d) == s for d, s in enumerate(steps)])
            pl.when(at((0, 0, 0)))(e_start)

        @pl.when(pl.program_id(2) == 0)
        def _():
            o_ref[...] = jnp.zeros_like(o_ref)

        if mode == "plain":
            a = a_refs[0][...].astype(MXU_DT)
        elif mode == "affine":
            a = (a_refs[0][...] * a_refs[1][...] + a_refs[2][...]).astype(MXU_DT)
        else:
            g = a_refs[0][...].astype(F32)
            a = (g * _sigmoid(g) * a_refs[1][...].astype(F32)).astype(MXU_DT)
        o_ref[...] += _mm_tn(a, b_ref[...].astype(MXU_DT))
        if ne:
            pl.when(at(tuple(g - 1 for g in grid)))(e_finish)

    a_tile = pl.BlockSpec((tt, tmm), lambda i, j, k: (k, i))
    a_vec = pl.BlockSpec((1, tmm), lambda i, j, k: (0, i))
    a_specs = {"plain": [a_tile], "affine": [a_tile, a_vec, a_vec], "swiglu": [a_tile, a_tile]}[mode]
    res = pl.pallas_call(
        body, name=name, grid=grid,
        in_specs=a_specs + [pl.BlockSpec((tt, tn), lambda i, j, k: (k, j))] + [ANY] * ne,
        out_specs=(pl.BlockSpec((tmm, tn), lambda i, j, k: (i, j)),) + (ANY,) * ne,
        out_shape=(jax.ShapeDtypeStruct((m, n), F32),)
        + tuple(jax.ShapeDtypeStruct(a.shape, a.dtype) for a in exchange),
        scratch_shapes=_exchange_sems(ne) if ne else [],
        compiler_params=_params(52, 3),
    )(*a_parts, b, *exchange)
    return res if ne else res[0]


def _dw_shared(a_parts, bs, name, exchange=()):
    xh, g, b = a_parts
    t, m = xh.shape
    tt = _tile(t, 1024)
    nk = t // tt
    nb, ne = len(bs), len(exchange)

    def body(*refs):
        x_ref, g_ref, b_ref = refs[:3]
        b_refs, o_refs = refs[3:3 + nb], refs[3 + nb + ne:3 + 2 * nb + ne]
        if ne:
            e_start, e_finish = _exchange_phases(refs[3 + nb:3 + nb + ne], refs[3 + 2 * nb + ne:3 + 2 * nb + 2 * ne],
                                                 *refs[3 + 2 * nb + 2 * ne:])
            pl.when(pl.program_id(0) == 0)(e_start)

        @pl.when(pl.program_id(0) == 0)
        def _():
            for o_ref in o_refs:
                o_ref[...] = jnp.zeros_like(o_ref)

        at = (x_ref[...] * g_ref[...] + b_ref[...]).T.astype(MXU_DT)
        for b_ref, o_ref in zip(b_refs, o_refs):
            o_ref[...] += _mm(at, b_ref[...].astype(MXU_DT))
        if ne:
            pl.when(pl.program_id(0) == nk - 1)(e_finish)

    vec = pl.BlockSpec((1, m), lambda k: (0, 0))
    res = pl.pallas_call(
        body, name=name, grid=(nk,),
        in_specs=[pl.BlockSpec((tt, m), lambda k: (k, 0)), vec, vec]
        + [pl.BlockSpec((tt, x.shape[1]), lambda k: (k, 0)) for x in bs] + [ANY] * ne,
        out_specs=tuple(pl.BlockSpec((m, x.shape[1]), lambda k: (0, 0)) for x in bs) + (ANY,) * ne,
        out_shape=tuple(jax.ShapeDtypeStruct((m, x.shape[1]), F32) for x in bs)
        + tuple(jax.ShapeDtypeStruct(a.shape, a.dtype) for a in exchange),
        scratch_shapes=_exchange_sems(ne) if ne else [],
        compiler_params=_params(56),
    )(xh, g, b, *bs, *exchange)
    return res[:nb], res[nb:]


def _adamw(w, g, m, v):
    m = ADAM_B1 * m + (1.0 - ADAM_B1) * g
    v = ADAM_B2 * v + (1.0 - ADAM_B2) * (g * g)
    m_hat = m / (1.0 - ADAM_B1 ** ADAM_STEP)
    v_hat = v / (1.0 - ADAM_B2 ** ADAM_STEP)
    delta = -ADAM_LR * (m_hat / (jnp.sqrt(v_hat) + ADAM_EPS) + ADAM_WD * w)
    return delta, m, v


def _reduce_adamw(landed, own, w, m, v, name):
    r, c = own.shape
    tr = _tile(r, 128)

    def body(l_ref, o_ref, w_ref, m_ref, v_ref, g_out, d_out, m_out, v_out):
        me = 4 * lax.axis_index("x") + 2 * lax.axis_index("y") + lax.axis_index("c")
        g = None
        for j in range(NDEV):
            term = jnp.where(me == j, o_ref[...], l_ref[j].astype(F32))
            g = term if g is None else g + term
        g_out[...] = g
        d_out[...], m_out[...], v_out[...] = _adamw(w_ref[...], g, m_ref[...], v_ref[...])

    blk = pl.BlockSpec((tr, c), lambda i: (i, 0))
    sds = jax.ShapeDtypeStruct((r, c), F32)
    return pl.pallas_call(
        body, name=name, grid=(r // tr,),
        in_specs=[pl.BlockSpec((NDEV, tr, c), lambda i: (0, i, 0)), blk, blk, blk, blk],
        out_specs=(blk, blk, blk, blk), out_shape=(sds, sds, sds, sds),
        compiler_params=_params(40),
    )(landed, own, w, m, v)


def _sum_small(gathered, name):
    _, r, c = gathered.shape

    def body(g_ref, o_ref):
        acc = g_ref[0]
        for j in range(1, NDEV):
            acc = acc + g_ref[j]
        o_ref[...] = acc

    return pl.pallas_call(body, name=name, out_shape=jax.ShapeDtypeStruct((r, c), F32))(gathered)


def _adamw_small(g, w, m, v, name):
    def body(g_ref, w_ref, m_ref, v_ref, d_out, m_out, v_out):
        d_out[...], m_out[...], v_out[...] = _adamw(w_ref[...], g_ref[...], m_ref[...], v_ref[...])

    sds = jax.ShapeDtypeStruct(g.shape, F32)
    return pl.pallas_call(body, name=name, out_shape=(sds, sds, sds))(g, w, m, v)


def _cols_from_stack(s):
    return jnp.transpose(s, (1, 0, 2)).reshape(s.shape[1], NDEV * s.shape[2])


def _cols_to_stack(w):
    r, c = w.shape
    return jnp.transpose(w.reshape(r, NDEV, c // NDEV), (1, 0, 2))


def _rows_from_stack(s):
    return s.reshape(NDEV * s.shape[1], s.shape[2])


def _rows_to_stack(w):
    r, c = w.shape
    return w.reshape(NDEV, r // NDEV, c)


SMALL_ROWS = 16
SMALL_SLOTS = {
    "ln1_g": (0, 0, D), "ln1_b": (1, 0, D), "ln2_g": (2, 0, D), "ln2_b": (3, 0, D), "ln3_g": (4, 0, D),
    "ln3_b": (5, 0, D), "b_ple_gate": (6, 0, D), "ln4_g": (7, 0, D), "ln4_b": (8, 0, D),
    "g_attn": (9, 0, DA), "g_conv": (9, DA, DCV), "b_forget": (10, 0, NH),
}
CONVW_ROW = 11
LOSS_SLOT = (10, LANES)


def _pack_small(vals, conv_rows, loss=None):
    out = jnp.zeros((SMALL_ROWS, D), F32)
    for nm, (r, off, wd) in SMALL_SLOTS.items():
        out = out.at[r:r + 1, off:off + wd].set(vals[nm].reshape(1, wd).astype(F32))
    out = out.at[CONVW_ROW:CONVW_ROW + 3, 0:conv_rows.shape[1]].set(conv_rows.astype(F32))
    if loss is not None:
        out = out.at[LOSS_SLOT[0], LOSS_SLOT[1]].set(loss)
    return out


def _unpack_small(packed, name):
    r, off, wd = SMALL_SLOTS[name]
    return packed[r:r + 1, off:off + wd]


def kernel(x, p, ffn1_w_in, ffn1_w_out, ln1_g, ln1_b, w_mix_in, b_forget, conv_w, g_attn, g_conv, w_mix_out, ln2_g, ln2_b, ffn2_w_in, ffn2_w_out, ln3_g, ln3_b, w_ple, w_ple_gate, b_ple_gate, ln4_g, ln4_b, loss_target, m_ffn1_w_in, m_ffn1_w_out, m_ln1_g, m_ln1_b, m_w_mix_in, m_b_forget, m_conv_w, m_g_attn, m_g_conv, m_w_mix_out, m_ln2_g, m_ln2_b, m_ffn2_w_in, m_ffn2_w_out, m_ln3_g, m_ln3_b, m_w_ple, m_w_ple_gate, m_b_ple_gate, m_ln4_g, m_ln4_b, v_ffn1_w_in, v_ffn1_w_out, v_ln1_g, v_ln1_b, v_w_mix_in, v_b_forget, v_conv_w, v_g_attn, v_g_conv, v_w_mix_out, v_ln2_g, v_ln2_b, v_ffn2_w_in, v_ffn2_w_out, v_ln3_g, v_ln3_b, v_w_ple, v_w_ple_gate, v_b_ple_gate, v_ln4_g, v_ln4_b):
    args = dict(locals())
    t = x.shape[1]
    me = 4 * lax.axis_index("x") + 2 * lax.axis_index("y") + lax.axis_index("c")
    x0 = x.reshape(t, D)
    p0 = p.reshape(t, PLE)
    tgt = loss_target.reshape(t, D)

    big = ["ffn1_w_in", "ffn1_w_out", "w_mix_in", "w_mix_out", "ffn2_w_in", "ffn2_w_out", "w_ple", "w_ple_gate"]
    col_sharded = {"ffn1_w_in", "w_mix_in", "ffn2_w_in", "w_ple"}
    shard = {nm: args[nm][0] for nm in big}

    unstack = lambda nm, g: (_cols_from_stack(g) if nm in col_sharded else _rows_from_stack(g)).astype(MXU_DT)
    stack = lambda nm, g: _cols_to_stack(g) if nm in col_sharded else _rows_to_stack(g)
    wire = lambda names: [shard[nm].astype(WIRE_DT) for nm in names]
    first, later = big[:2], big[2:]

    full = {nm: unstack(nm, g) for nm, g in zip(first, _allgather(wire(first), "ag_ffn1"))}
    ffn1_out, gathered = _ffn1_fwd(x0, full, wire(later) + [conv_w[0]])
    full.update({nm: unstack(nm, g) for nm, g in zip(later, gathered)})
    cw = _cols_from_stack(gathered[len(later)])

    slots = lambda nm, g: stack(nm, g).astype(WIRE_DT)
    dr1, gw, small, loss_part, landed = _mid_step(p0, tgt, full, cw, {nm: args[nm] for nm in SMALL_SLOTS},
                                                  ffn1_out, slots)
    small_part = _pack_small({nm: small[nm] for nm in SMALL_SLOTS},
                             jnp.pad(small["conv_w"], ((0, 0), (0, D - DCV))), loss_part)
    gx, gw1, _, (landed["w_mix_in"],), landed["ffn1_w_in"] = _ffn1_bwd(
        x0, dr1, ffn1_out, full, (), [slots("w_mix_in", gw["w_mix_in"])], lambda g: slots("ffn1_w_in", g))
    stacks = {nm: stack(nm, g) for nm, g in {**gw, **gw1}.items()}
    (landed["ffn1_w_out"],), (small_all,) = _exchange_and_gather([stacks["ffn1_w_out"].astype(WIRE_DT)],
                                                                 [small_part], "rs_ffn1_out")
    small_g = _sum_small(small_all, "sum_small")
    loss = small_g[LOSS_SLOT[0], LOSS_SLOT[1]]

    outs = {"loss": loss, "grad_x": gx.reshape(1, t, D)}
    for nm in big:
        own = lax.dynamic_index_in_dim(stacks[nm], me, axis=0, keepdims=False)
        g, dl, mn, vn = _reduce_adamw(landed[nm], own, shard[nm], args["m_" + nm][0], args["v_" + nm][0],
                                      "adamw_" + nm)
        outs["grad_" + nm], outs["delta_" + nm], outs["new_m_" + nm], outs["new_v_" + nm] = (
            g[None], dl[None], mn[None], vn[None])
    small_names = list(SMALL_SLOTS)
    cshard = lax.dynamic_slice_in_dim(small_g[CONVW_ROW:CONVW_ROW + 3, 0:DCV], me * (DCV // NDEV), DCV // NDEV, axis=1)
    g_pack = _pack_small({nm: _unpack_small(small_g, nm) for nm in small_names}, cshard)
    packs = [_pack_small({nm: args[pre + nm] for nm in small_names}, args[pre + "conv_w"][0])
             for pre in ("", "m_", "v_")]
    d_pack, m_pack, v_pack = _adamw_small(g_pack, packs[0], packs[1], packs[2], "adamw_small")
    for key, pk in (("grad_", g_pack), ("delta_", d_pack), ("new_m_", m_pack), ("new_v_", v_pack)):
        for nm in small_names:
            outs[key + nm] = _unpack_small(pk, nm)
        outs[key + "conv_w"] = pk[CONVW_ROW:CONVW_ROW + 3, 0:DCV // NDEV][None]

    wnames = ["ffn1_w_in", "ffn1_w_out", "ln1_g", "ln1_b", "w_mix_in", "b_forget", "conv_w", "g_attn", "g_conv",
              "w_mix_out", "ln2_g", "ln2_b", "ffn2_w_in", "ffn2_w_out", "ln3_g", "ln3_b", "w_ple", "w_ple_gate",
              "b_ple_gate", "ln4_g", "ln4_b"]
    return (outs["loss"], outs["grad_x"], *[outs[pre + nm] for pre in ("grad_", "delta_", "new_m_", "new_v_")
                                            for nm in wnames])


def _ffn1_fwd(x0, full, gather=()):
    res = _ffn_fwd(x0, jnp.ones((1, D), F32), jnp.zeros((1, D), F32), full["ffn1_w_in"], full["ffn1_w_out"],
                   "ffn1_fwd", gather)
    return res[:4], res[4:]


def _ffn1_bwd(x0, dr1, ffn1_out, full, exchange=(), exchange_late=(), w_in_slots=None):
    g1a, u1a, _, rs1 = ffn1_out
    ones, zeros = jnp.ones((1, D), F32), jnp.zeros((1, D), F32)
    res = _ffn_bwd(dr1, g1a, u1a, x0, rs1, ones, full["ffn1_w_in"], full["ffn1_w_out"], False, "ffn1_bwd",
                   exchange)
    df1, dg1, du1, gx = res[:4]
    dw_g = _dw("affine", (x0, ones, zeros), dg1, D, F, "dw_ffn1_in_g", tn=F // 2, exchange=exchange_late)
    dw_g, landed_late = (dw_g[0], dw_g[1:]) if exchange_late else (dw_g, ())
    gw_in = jnp.concatenate([dw_g, _dw("affine", (x0, ones, zeros), du1, D, F, "dw_ffn1_in_u", tn=F // 2)], axis=1)
    side = () if w_in_slots is None else (w_in_slots(gw_in),)
    out = _dw("swiglu", (g1a, u1a), df1, F, D, "dw_ffn1_out", tmm=F // 2, exchange=side)
    gw_out, landed_in = (out, None) if w_in_slots is None else (out[0], out[1])
    return gx, {"ffn1_w_in": gw_in, "ffn1_w_out": gw_out}, res[5:], landed_late, landed_in


def _mid_step(p0, tgt, full, cw, sp, ffn1_out, slots):
    g1a, u1a, xh1, rs1 = ffn1_out
    t = xh1.shape[0]
    ln1_g, ln1_b, ln2_g, ln2_b, ln3_g, ln3_b = (sp[k] for k in ("ln1_g", "ln1_b", "ln2_g", "ln2_b", "ln3_g", "ln3_b"))
    ln4_g, ln4_b, g_attn, g_conv, b_ple_gate = (sp[k] for k in ("ln4_g", "ln4_b", "g_attn", "g_conv", "b_ple_gate"))
    wmi = full["w_mix_in"]
    w_qkv = wmi[:, 0:3 * DA]
    w_f = jnp.pad(wmi[:, 3 * DA:3 * DA + NH], ((0, 0), (0, LANES - NH)))
    w_bch = wmi[:, 3 * DA + NH:]
    bf_pad = jnp.pad(sp["b_forget"], ((0, 0), (0, LANES - NH)))

    ka, va, qat, kat, vta, bch, z, rt = _mix_proj_fwd(xh1, ln1_g, ln1_b, w_qkv[:, DA:], jnp.transpose(w_qkv),
                                                      w_bch, w_f, bf_pad, "mix_proj_fwd")
    rtile = jnp.transpose(rt[:, 0, 0:NH])
    o, lse = _attn_fwd(qat, ka, vta, rtile, "attn_fwd")
    merged, xh2, rs2 = _mix_post_fwd(o, bch, cw, g_attn, g_conv, xh1, ln1_g, ln1_b, full["w_mix_out"],
                                     "mix_post_fwd")
    g2a, u2a, xh3, rs3 = _ffn_fwd(xh2, ln2_g, ln2_b, full["ffn2_w_in"], full["ffn2_w_out"], "ffn2_fwd")

    dr3, dz, de, st_tail = _tail(xh3, rs3, ln3_g, ln3_b, p0, full["w_ple_gate"], full["w_ple"], b_ple_gate,
                                 ln4_g, ln4_b, tgt, "tail")
    df2, dg2, du2, dr2, st_f2 = _ffn_bwd(dr3, g2a, u2a, xh2, rs2, ln2_g, full["ffn2_w_in"], full["ffn2_w_out"],
                                         True, "ffn2_bwd")
    dmix, dot, drow, dyc, st_post = _mix_post_bwd(dr2, o, bch, cw, g_attn, g_conv, full["w_mix_out"],
                                                  "mix_post_bwd")
    dbch, st_conv = _conv_bwd(dyc, bch, cw, "conv_bwd")
    dk, dv, dck, dqt, dcq = _attn_bwd(ka, kat, va, qat, dot, lse, drow, rtile, "attn_bwd")
    dc_pad = jnp.pad(jnp.transpose((dcq + dck).reshape(NH, t)), ((0, 0), (0, LANES - NH)))

    x1p, x2p, x3p = (xh1, ln1_g, ln1_b), (xh2, ln2_g, ln2_b), (xh3, ln3_g, ln3_b)
    gw = {}
    gw["ffn2_w_in"] = jnp.concatenate(
        [_dw("affine", x2p, dg2, D, F, "dw_ffn2_in_g", tn=F // 2),
         _dw("affine", x2p, du2, D, F, "dw_ffn2_in_u", tn=F // 2)], axis=1)
    gw["ffn2_w_out"] = _dw("swiglu", (g2a, u2a), df2, F, D, "dw_ffn2_out", tmm=F // 2)
    gw["w_mix_out"] = _dw("plain", (merged,), dmix, D, D, "dw_mix_out")
    gw["w_ple_gate"] = _dw("affine", x3p, dz, D, D, "dw_ple_gate")
    gw["w_ple"] = _dw("plain", (p0,), de, PLE, D, "dw_ple")
    early_a, early_b = ("ffn2_w_in", "w_ple_gate", "w_ple"), ("ffn2_w_out", "w_mix_out")
    travel = lambda names: [slots(nm, gw[nm]) for nm in names] if slots else []
    dr1, dfl, dq, st_proj, landed_a = _mix_proj_bwd(dr2, dqt, dk, dv, dbch, dc_pad, z, xh1, rs1, ln1_g, w_qkv, w_bch,
                                                    w_f, "mix_proj_bwd", travel(early_a))
    (gq, gk, gv, gf, gbch), landed_b = _dw_shared(x1p, (dq, dk, dv, dfl, dbch), "dw_mix_in", travel(early_b))
    gw["w_mix_in"] = jnp.concatenate([gq, gk, gv, gf[:, 0:NH], gbch], axis=1)
    landed = dict(zip(early_a + early_b, tuple(landed_a) + tuple(landed_b)))

    loss_part = (0.5 / D) * jnp.sum(st_tail[5:6, :])
    small = {"ln1_g": st_proj[0:1], "ln1_b": st_proj[1:2], "ln2_g": st_f2[0:1], "ln2_b": st_f2[1:2],
             "ln3_g": st_tail[3:4], "ln3_b": st_tail[4:5], "b_ple_gate": st_tail[2:3], "ln4_g": st_tail[0:1],
             "ln4_b": st_tail[1:2], "g_attn": st_post[0:1], "g_conv": st_post[1:2],
             "b_forget": st_proj[2:3, 0:NH], "conv_w": st_conv[0:3]}
    return dr1, gw, small, loss_part, landed
```

```python
import functools

import jax
import jax.numpy as jnp
from jax import lax
from jax.experimental import pallas as pl
from jax.experimental.pallas import tpu as pltpu

D = 1024
F = 2816
NH = 8
DH = 64
DA = NH * DH
DCV = D - DA
PLE = 256
LN_EPS = 1e-5
RMS_EPS = 1e-6
NEG = -1e30
ALPHA = 2.0 ** 0.25
NDEV = 8
LANES = 128

ADAM_LR, ADAM_B1, ADAM_B2, ADAM_EPS, ADAM_WD, ADAM_STEP = 0.001, 0.9, 0.999, 1e-08, 0.01, 10

F32 = jnp.float32
MXU_DT = jnp.bfloat16
WIRE_DT = jnp.bfloat16

MESH_ID = pl.DeviceIdType.MESH
ANY = pl.BlockSpec(memory_space=pl.ANY)


def _params(vmem_mb, n_axes=1):
    return pltpu.CompilerParams(dimension_semantics=("arbitrary",) * n_axes,
                                vmem_limit_bytes=int(vmem_mb) << 20)


def _mm(a, b):
    return jnp.dot(a, b, preferred_element_type=F32)


def _mm_nt(a, b):
    return lax.dot_general(a, b, (((1,), (1,)), ((), ())), preferred_element_type=F32)


def _mm_tn(a, b):
    return lax.dot_general(a, b, (((0,), (0,)), ((), ())), preferred_element_type=F32)


def _split3(x):
    hi = x.astype(MXU_DT)
    r1 = x - hi.astype(F32)
    mid = r1.astype(MXU_DT)
    lo = (r1 - mid.astype(F32)).astype(MXU_DT)
    return hi, mid, lo


def _mm_sel(sel, x):
    hi, mid, lo = _split3(x)
    return _mm(sel, hi) + _mm(sel, mid) + _mm(sel, lo)


def _sigmoid(x):
    return 1.0 / (1.0 + jnp.exp(-x))


def _ln_fwd(r):
    mu = jnp.mean(r, axis=-1, keepdims=True)
    xc = r - mu
    var = jnp.mean(xc * xc, axis=-1, keepdims=True)
    rstd = lax.rsqrt(var + LN_EPS)
    return xc * rstd, rstd


def _ln_bwd(dxhat, xhat, rstd):
    m1 = jnp.mean(dxhat, axis=-1, keepdims=True)
    m2 = jnp.mean(dxhat * xhat, axis=-1, keepdims=True)
    return rstd * (dxhat - m1 - xhat * m2)


def _rms_fwd(x):
    r = lax.rsqrt(jnp.mean(x * x, axis=-1, keepdims=True) + RMS_EPS)
    return x * r, r


def _rms_bwd(dyg, xn, r):
    return r * (dyg - xn * jnp.mean(dyg * xn, axis=-1, keepdims=True))


def _colsum(x):
    return jnp.sum(x, axis=0, keepdims=True)


def _f_chunks():
    out, c0 = [], 0
    while c0 < F:
        fc = min(512, F - c0)
        out.append((c0, fc))
        c0 += fc
    return out


def _tile(t, want):
    return want if t % want == 0 and t >= want else t


def _exchange_sems(n):
    return [pltpu.SemaphoreType.DMA((n * (NDEV - 1),)), pltpu.SemaphoreType.DMA((n * (NDEV - 1),))]


def _exchange_phases(ins, outs, send_sems, recv_sems):
    n = len(ins)

    def peers():
        x, y, c = lax.axis_index("x"), lax.axis_index("y"), lax.axis_index("c")
        out = []
        for k in range(1, NDEV):
            px = 1 - x if (k >> 2) & 1 else x
            py = 1 - y if (k >> 1) & 1 else y
            pc = 1 - c if k & 1 else c
            out.append(((px, py, pc), 4 * px + 2 * py + pc))
        return 4 * x + 2 * y + c, out

    def remote(w, k, to, slot_src, slot_dst):
        return pltpu.make_async_remote_copy(
            src_ref=ins[w].at[slot_src], dst_ref=outs[w].at[slot_dst],
            send_sem=send_sems.at[w * (NDEV - 1) + k], recv_sem=recv_sems.at[w * (NDEV - 1) + k],
            device_id=to, device_id_type=MESH_ID)

    def start():
        me, prs = peers()
        for k, (to, pid) in enumerate(prs):
            for w in range(n):
                remote(w, k, to, pid, me).start()

    def finish():
        me, prs = peers()
        for k, (to, pid) in enumerate(prs):
            for w in range(n):
                remote(w, k, to, me, pid).wait_recv()
        for k, (to, pid) in enumerate(prs):
            for w in range(n):
                remote(w, k, to, pid, me).wait_send()

    return start, finish


def _gather_sems(n):
    return [pltpu.SemaphoreType.DMA((n * (NDEV - 1),)), pltpu.SemaphoreType.DMA((n * (NDEV - 1),)),
            pltpu.SemaphoreType.DMA((n,))]


def _gather_phases(ins, outs, send_sems, recv_sems, loc_sems):
    n = len(ins)
    per = NDEV - 1

    def place():
        x, y, c = lax.axis_index("x"), lax.axis_index("y"), lax.axis_index("c")
        return (x, y, c), (x, y, 1 - c), [(1 - x, y), (x, 1 - y), (1 - x, 1 - y)]

    def copy(w, k, block, to, src=None):
        dst = outs[w].at[4 * block[0] + 2 * block[1] + block[2]]
        return pltpu.make_async_remote_copy(
            src_ref=dst if src is None else src, dst_ref=dst,
            send_sem=send_sems.at[w * per + k], recv_sem=recv_sems.at[w * per + k],
            device_id=to, device_id_type=MESH_ID)

    def local(w, me):
        return pltpu.make_async_copy(ins[w], outs[w].at[4 * me[0] + 2 * me[1] + me[2]], loc_sems.at[w])

    def first(me, sib, chips):
        out = []
        for j, chip in enumerate(chips):
            out += [copy(w, 1 + j, me, (*chip, me[2]), src=ins[w]) for w in range(n)]
        return out + [copy(w, 0, me, sib, src=ins[w]) for w in range(n)]

    def start():
        me, sib, chips = place()
        for w in range(n):
            local(w, me).start()
        for cp in first(me, sib, chips):
            cp.start()

    def forward():
        me, sib, chips = place()
        for j, chip in enumerate(chips):
            for w in range(n):
                copy(w, 1 + j, (*chip, me[2]), me).wait_recv()
                copy(w, 4 + j, (*chip, me[2]), sib).start()

    def finish():
        me, sib, chips = place()
        for w in range(n):
            copy(w, 0, sib, me).wait_recv()
        for j, chip in enumerate(chips):
            for w in range(n):
                copy(w, 4 + j, (*chip, 1 - me[2]), me).wait_recv()
        for cp in first(me, sib, chips):
            cp.wait_send()
        for j, chip in enumerate(chips):
            for w in range(n):
                copy(w, 4 + j, (*chip, me[2]), sib).wait_send()
        for w in range(n):
            local(w, me).wait()

    return start, forward, finish


def _allgather(arrs, name):
    n = len(arrs)

    def body(*refs):
        start, forward, finish = _gather_phases(refs[:n], refs[n:2 * n], *refs[2 * n:])
        start()
        forward()
        finish()

    return pl.pallas_call(
        body, name=name, out_shape=tuple(jax.ShapeDtypeStruct((NDEV,) + a.shape, a.dtype) for a in arrs),
        in_specs=[ANY] * n, out_specs=tuple([ANY] * n), scratch_shapes=_gather_sems(n),
    )(*arrs)


def _exchange_and_gather(ex, ga, name):
    ne, ng = len(ex), len(ga)

    def body(*refs):
        ins, outs, sems = refs[:ne + ng], refs[ne + ng:2 * (ne + ng)], refs[2 * (ne + ng):]
        e_start, e_finish = _exchange_phases(ins[:ne], outs[:ne], *sems[:2])
        g_start, g_forward, g_finish = _gather_phases(ins[ne:], outs[ne:], *sems[2:])
        e_start()
        g_start()
        g_forward()
        g_finish()
        e_finish()

    res = pl.pallas_call(
        body, name=name,
        out_shape=tuple(jax.ShapeDtypeStruct(a.shape, a.dtype) for a in ex)
        + tuple(jax.ShapeDtypeStruct((NDEV,) + a.shape, a.dtype) for a in ga),
        in_specs=[ANY] * (ne + ng), out_specs=tuple([ANY] * (ne + ng)),
        scratch_shapes=_exchange_sems(ne) + _gather_sems(ng),
    )(*ex, *ga)
    return res[:ne], res[ne:]


def _ffn_fwd(xin, gin, bin_, w_in, w_out, name, gather=()):
    t = xin.shape[0]
    tm = _tile(t, 512)
    nt = t // tm
    chunks = _f_chunks()
    ng = len(gather)

    def body(*refs):
        x_ref, gi_ref, bi_ref, win_hbm, wout_hbm = refs[:5]
        g_ref, u_ref, xh_ref, rs_ref = refs[5 + ng:9 + ng]
        win_v, wout_v, acc_ref = refs[9 + 2 * ng:12 + 2 * ng]
        if ng:
            g_start, g_forward, g_finish = _gather_phases(refs[5:5 + ng], refs[9 + ng:9 + 2 * ng],
                                                          *refs[12 + 2 * ng:])

        @pl.when(pl.program_id(0) == 0)
        def _():
            if ng:
                g_start()
            pltpu.sync_copy(win_hbm, win_v)
            pltpu.sync_copy(wout_hbm, wout_v)

        if ng:
            pl.when(pl.program_id(0) == nt // 2)(g_forward)
            pl.when(pl.program_id(0) == nt - 1)(g_finish)

        x = x_ref[...] * gi_ref[...] + bi_ref[...]
        xb = x.astype(MXU_DT)
        for ci, (c0, fc) in enumerate(chunks):
            gc = _mm(xb, win_v[:, c0:c0 + fc])
            uc = _mm(xb, win_v[:, F + c0:F + c0 + fc])
            g_ref[:, c0:c0 + fc] = gc.astype(g_ref.dtype)
            u_ref[:, c0:c0 + fc] = uc.astype(u_ref.dtype)
            hc = (gc * _sigmoid(gc) * uc).astype(MXU_DT)
            part = _mm(hc, wout_v[c0:c0 + fc, :])
            if ci == 0:
                acc_ref[...] = part
            else:
                acc_ref[...] += part
        xh, rstd = _ln_fwd(ALPHA * x + 0.5 * acc_ref[...])
        xh_ref[...] = xh
        rs_ref[...] = rstd

    row = pl.BlockSpec((tm, D), lambda i: (i, 0))
    vec = pl.BlockSpec((1, D), lambda i: (0, 0))
    act = pl.BlockSpec((tm, F), lambda i: (i, 0))
    return pl.pallas_call(
        body, name=name, grid=(nt,),
        in_specs=[row, vec, vec, ANY, ANY] + [ANY] * ng,
        out_specs=(act, act, row, pl.BlockSpec((tm, 1), lambda i: (i, 0))) + (ANY,) * ng,
        out_shape=(jax.ShapeDtypeStruct((t, F), MXU_DT), jax.ShapeDtypeStruct((t, F), MXU_DT),
                   jax.ShapeDtypeStruct((t, D), F32), jax.ShapeDtypeStruct((t, 1), F32))
        + tuple(jax.ShapeDtypeStruct((NDEV,) + a.shape, a.dtype) for a in gather),
        scratch_shapes=[pltpu.VMEM((D, 2 * F), MXU_DT), pltpu.VMEM((F, D), MXU_DT), pltpu.VMEM((tm, D), F32)]
        + (_gather_sems(ng) if ng else []),
        compiler_params=_params(52),
    )(xin, gin, bin_, w_in, w_out, *gather)


QROWS = 80
BIAS_AT = DH


def _place_matrices():
    import numpy as np
    pk = np.zeros((NH, LANES, LANES), np.float32)
    pqt = np.zeros((NH, LANES, LANES), np.float32)
    for h in range(NH):
        for piece in range(3):
            pk[h, 8 * piece + h, BIAS_AT + 3 + piece] = -1.0
            pqt[h, BIAS_AT + piece, 8 * piece + h] = 1.0
    pkt = np.transpose(pk, (0, 2, 1))
    return tuple(jnp.asarray(m, MXU_DT) for m in (pk, pqt, pkt))


def _mix_proj_fwd(xh1, g1, b1, w_kv, w_qkv_t, w_bch, w_f, bf_pad, name):
    t = xh1.shape[0]
    tm = _tile(t, 512)
    pk, pqt, pkt = _place_matrices()

    def body(x_ref, g_ref, b_ref, wkv_ref, wt_ref, wb_ref, wf_ref, bf_ref, pk_ref, pqt_ref, pkt_ref,
             ka_ref, va_ref, qat_ref, kat_ref, vta_ref, bch_ref, z_ref, r_ref, carry):
        @pl.when(pl.program_id(0) == 0)
        def _():
            carry[...] = jnp.zeros_like(carry)

        xb = (x_ref[...] * g_ref[...] + b_ref[...]).astype(MXU_DT)
        kv = _mm(xb, wkv_ref[...])
        qkvt = _mm_nt(wt_ref[...], xb)
        bch_ref[...] = _mm(xb, wb_ref[...])
        z = _mm(xb, wf_ref[...]) + bf_ref[...]
        z_ref[...] = z
        logf = jnp.minimum(z, 0.0) - jnp.log(1.0 + jnp.exp(-jnp.abs(z)))
        row = lax.broadcasted_iota(jnp.int32, (tm, tm), 0)
        col = lax.broadcasted_iota(jnp.int32, (tm, tm), 1)
        tri = jnp.where(row >= col, 1.0, 0.0).astype(MXU_DT)
        c = carry[...] + _mm_sel(tri, logf)
        carry[...] = c[tm - 1:tm, :]
        r_ref[0] = c[0:1, :]
        lane = lax.broadcasted_iota(jnp.int32, (1, LANES), 1)
        hi, mid, lo = _split3(jnp.where(lane < NH, c - c[0:1, :], 0.0))
        pieces = (hi.astype(F32) + pltpu.roll(mid.astype(F32), 8, 1) + pltpu.roll(lo.astype(F32), 16, 1)
                  ).astype(MXU_DT)
        sub = lax.broadcasted_iota(jnp.int32, (DH, 1), 0)
        ones_k_lanes = jnp.where((lane >= BIAS_AT) & (lane < BIAS_AT + 3), 1.0, 0.0)
        ones_q_rows = jnp.where((sub >= 3) & (sub < 6), 1.0, 0.0)
        ones_k_rows = jnp.where(sub[0:QROWS - DH] < 3, 1.0, 0.0)
        first_row = jnp.where(sub == 0, 1.0, 0.0) + jnp.zeros((DH, tm), F32)
        for h in range(NH):
            pair, odd = divmod(h, 2)
            k2 = kv[:, LANES * pair:LANES * (pair + 1)]
            v2 = kv[:, DA + LANES * pair:DA + LANES * (pair + 1)]
            if odd:
                k2, v2 = pltpu.roll(k2, DH, 1), pltpu.roll(v2, DH, 1)
            ka_ref[h] = jnp.where(lane < DH, k2, _mm(pieces, pk_ref[h]) + ones_k_lanes).astype(ka_ref.dtype)
            va_ref[h] = jnp.where(lane < DH, v2, 0.0).astype(va_ref.dtype)
            qat_ref[h, 0:DH, :] = (qkvt[DH * h:DH * (h + 1)] * 0.125).astype(qat_ref.dtype)
            qat_ref[h, DH:LANES, :] = (_mm_nt(pqt_ref[h], pieces)[DH:LANES] + ones_q_rows).astype(qat_ref.dtype)
            kat_ref[h, 0:DH, :] = qkvt[DA + DH * h:DA + DH * (h + 1)].astype(kat_ref.dtype)
            kat_ref[h, DH:QROWS, :] = (_mm_nt(pkt_ref[h], pieces)[DH:QROWS] + ones_k_rows).astype(kat_ref.dtype)
            vt = qkvt[2 * DA + DH * h:2 * DA + DH * (h + 1)]
            vta_ref[h, 0:DH, :] = (first_row if odd else vt).astype(vta_ref.dtype)
            vta_ref[h, DH:LANES, :] = (vt if odd else first_row).astype(vta_ref.dtype)

    row = lambda w: pl.BlockSpec((tm, w), lambda i: (i, 0))
    full = lambda a: pl.BlockSpec(a.shape, lambda i: (0,) * a.ndim)
    nat = pl.BlockSpec((NH, tm, LANES), lambda i: (0, i, 0))
    fmaj = lambda rows: pl.BlockSpec((NH, rows, tm), lambda i: (0, 0, i))
    return pl.pallas_call(
        body, name=name, grid=(t // tm,),
        in_specs=[row(D), full(g1), full(b1), full(w_kv), full(w_qkv_t), full(w_bch), full(w_f), full(bf_pad),
                  full(pk), full(pqt), full(pkt)],
        out_specs=(nat, nat, fmaj(LANES), fmaj(QROWS), fmaj(LANES), row(3 * DCV), row(LANES),
                   pl.BlockSpec((1, 1, LANES), lambda i: (i, 0, 0))),
        out_shape=(jax.ShapeDtypeStruct((NH, t, LANES), MXU_DT), jax.ShapeDtypeStruct((NH, t, LANES), MXU_DT),
                   jax.ShapeDtypeStruct((NH, LANES, t), MXU_DT), jax.ShapeDtypeStruct((NH, QROWS, t), MXU_DT),
                   jax.ShapeDtypeStruct((NH, LANES, t), MXU_DT), jax.ShapeDtypeStruct((t, 3 * DCV), F32),
                   jax.ShapeDtypeStruct((t, LANES), F32), jax.ShapeDtypeStruct((t // tm, 1, LANES), F32)),
        scratch_shapes=[pltpu.VMEM((1, LANES), F32)],
        compiler_params=_params(56),
    )(xh1, g1, b1, w_kv, w_qkv_t, w_bch, w_f, bf_pad, pk, pqt, pkt)


def _attn_fwd(qat, ka, vta, r, name):
    t = ka.shape[1]
    tq = _tile(t, 512)
    nq = t // tq

    def body(r_ref, q_ref, k_ref, v_ref, o_ref, l_ref, st0, st1):
        hp, i = pl.program_id(0), pl.program_id(1)
        key = lax.broadcasted_iota(jnp.int32, (tq, tq), 0)
        qry = lax.broadcasted_iota(jnp.int32, (tq, tq), 1)

        def tile_of(pos):
            return jnp.where(pos == 0, i, pos - 1)

        def scores(pos, buf, masked):
            off = pl.multiple_of(tile_of(pos) * tq, tq)
            for a in range(2):
                st = _mm(k_ref[a, pl.ds(off, tq), :], q_ref[a])
                buf[a] = jnp.where(qry >= key, st, NEG) if masked else st

        def consume(pos, buf, carry):
            j = tile_of(pos)
            off = pl.multiple_of(j * tq, tq)
            out = []
            for a in range(2):
                m, acc = carry[a]
                st = buf[a]
                d = r_ref[2 * hp + a, i] - r_ref[2 * hp + a, j]
                m_new = jnp.maximum(m, jnp.max(st, axis=0, keepdims=True) + d)
                pt = jnp.exp(st - (m_new - d))
                acc = jnp.exp(m - m_new) * acc + _mm(v_ref[a, :, pl.ds(off, tq)], pt.astype(MXU_DT))
                out.append((m_new, acc))
            return tuple(out)

        def trip(p, carry):
            scores(2 * p + 1, st1, False)
            carry = consume(2 * p, st0, carry)
            scores(2 * p + 2, st0, False)
            return consume(2 * p + 1, st1, carry)

        scores(0, st0, True)
        init = tuple((jnp.full((1, tq), NEG, F32), jnp.zeros((LANES, tq), F32)) for _ in range(2))
        trips = i // 2
        carry = lax.fori_loop(0, trips, trip, init)

        def last_two(cr):
            scores(2 * trips + 1, st1, False)
            return consume(2 * trips + 1, st1, consume(2 * trips, st0, cr))

        (ma, acca), (mb, accb) = lax.cond(i % 2 == 1, last_two, lambda cr: consume(2 * trips, st0, cr), carry)
        la, lb = acca[DH:DH + 1, :], accb[0:1, :]
        l_ref[0] = ma + jnp.log(la)
        l_ref[1] = mb + jnp.log(lb)
        sub = lax.broadcasted_iota(jnp.int32, (LANES, tq), 0)
        o_ref[...] = jnp.where(sub < DH, acca / la, accb / lb).T

    return pl.pallas_call(
        body, name=name, grid=(NH // 2, nq),
        in_specs=[pl.BlockSpec(memory_space=pltpu.SMEM),
                  pl.BlockSpec((2, LANES, tq), lambda p, i: (p, 0, i)),
                  pl.BlockSpec((2, t, LANES), lambda p, i: (p, 0, 0)),
                  pl.BlockSpec((2, LANES, t), lambda p, i: (p, 0, 0))],
        out_specs=(pl.BlockSpec((tq, LANES), lambda p, i: (i, p)),
                   pl.BlockSpec((2, 1, tq), lambda p, i: (p, 0, i))),
        out_shape=(jax.ShapeDtypeStruct((t, DA), F32), jax.ShapeDtypeStruct((NH, 1, t), F32)),
        scratch_shapes=[pltpu.VMEM((2, tq, tq), F32), pltpu.VMEM((2, tq, tq), F32)],
        compiler_params=_params(48, 2),
    )(r, qat, ka, vta)


def _conv_parts(bch):
    return bch[:, 0:DCV], bch[:, DCV:2 * DCV], bch[:, 2 * DCV:3 * DCV]


def _mix_post_fwd(o, bch, conv_w, g_attn, g_conv, xh1, g1, b1, w_mo, name):
    t = o.shape[0]
    tm = _tile(t, 512)
    hb = tm // 8

    def body(o_ref, bch_ref, halo_ref, cw_ref, ga_ref, gc_ref, x_ref, g_ref, b_ref, w_ref,
             mg_ref, xh_ref, rs_ref, ext):
        i = pl.program_id(0)
        an, _ = _rms_fwd(o_ref[...])
        mg_ref[:, 0:DA] = (an * ga_ref[...]).astype(mg_ref.dtype)
        bb, cc, hh = _conv_parts(bch_ref[...])
        _, hc, hh_h = _conv_parts(halo_ref[...])
        u = cc * hh
        ext[0:8, :] = jnp.where(i > 0, hc * hh_h, 0.0)
        ext[8:8 + tm, :] = u
        raw = cw_ref[0:1, :] * ext[6:6 + tm, :] + cw_ref[1:2, :] * ext[7:7 + tm, :] + cw_ref[2:3, :] * u
        cn, _ = _rms_fwd(bb * raw)
        mg_ref[:, DA:D] = (cn * gc_ref[...]).astype(mg_ref.dtype)
        x1 = x_ref[...] * g_ref[...] + b_ref[...]
        xh, rstd = _ln_fwd(ALPHA * x1 + _mm(mg_ref[...], w_ref[...]))
        xh_ref[...] = xh
        rs_ref[...] = rstd

    row = lambda w: pl.BlockSpec((tm, w), lambda i: (i, 0))
    full = lambda a: pl.BlockSpec(a.shape, lambda i: (0, 0))
    return pl.pallas_call(
        body, name=name, grid=(t // tm,),
        in_specs=[row(DA), row(3 * DCV),
                  pl.BlockSpec((8, 3 * DCV), lambda i: (jnp.maximum(i * hb - 1, 0), 0)),
                  full(conv_w), full(g_attn), full(g_conv), row(D), full(g1), full(b1), full(w_mo)],
        out_specs=(row(D), row(D), pl.BlockSpec((tm, 1), lambda i: (i, 0))),
        out_shape=(jax.ShapeDtypeStruct((t, D), MXU_DT), jax.ShapeDtypeStruct((t, D), F32),
                   jax.ShapeDtypeStruct((t, 1), F32)),
        scratch_shapes=[pltpu.VMEM((tm + 8, DCV), F32)],
        compiler_params=_params(48),
    )(o, bch, bch, conv_w, g_attn, g_conv, xh1, g1, b1, w_mo)


def _tail(xh3, rs3, g3, b3, p, w_g, w_ple, bg, g4, b4, target, name):
    t = xh3.shape[0]
    tm = _tile(t, 512)

    def body(x_ref, rs_ref, g3_ref, b3_ref, p_ref, wg_ref, wp_ref, bg_ref, g4_ref, b4_ref, t_ref,
             dr_ref, dz_ref, de_ref, st_ref):
        @pl.when(pl.program_id(0) == 0)
        def _():
            st_ref[...] = jnp.zeros_like(st_ref)

        xh3v = x_ref[...]
        x3 = xh3v * g3_ref[...] + b3_ref[...]
        gate = _sigmoid(_mm(x3.astype(MXU_DT), wg_ref[...]) + bg_ref[...])
        e = _mm(p_ref[...].astype(MXU_DT), wp_ref[...])
        xh4, rstd4 = _ln_fwd(ALPHA * x3 + gate * e)
        diff = xh4 * g4_ref[...] + b4_ref[...] - t_ref[...]
        dy = diff * (1.0 / D)
        st_ref[5:6, :] += _colsum(diff * diff)
        st_ref[0:1, :] += _colsum(dy * xh4)
        st_ref[1:2, :] += _colsum(dy)
        dr4 = _ln_bwd(dy * g4_ref[...], xh4, rstd4)
        de_ref[...] = (dr4 * gate).astype(de_ref.dtype)
        dz = dr4 * e * gate * (1.0 - gate)
        st_ref[2:3, :] += _colsum(dz)
        dzb = dz.astype(MXU_DT)
        dz_ref[...] = dzb
        dx3 = ALPHA * dr4 + _mm_nt(dzb, wg_ref[...])
        st_ref[3:4, :] += _colsum(dx3 * xh3v)
        st_ref[4:5, :] += _colsum(dx3)
        dr_ref[...] = _ln_bwd(dx3 * g3_ref[...], xh3v, rs_ref[...])

    row = lambda w: pl.BlockSpec((tm, w), lambda i: (i, 0))
    full = lambda a: pl.BlockSpec(a.shape, lambda i: (0, 0))
    return pl.pallas_call(
        body, name=name, grid=(t // tm,),
        in_specs=[row(D), row(1), full(g3), full(b3), row(PLE), full(w_g), full(w_ple), full(bg), full(g4),
                  full(b4), row(D)],
        out_specs=(row(D), row(D), row(D), pl.BlockSpec((8, D), lambda i: (0, 0))),
        out_shape=(jax.ShapeDtypeStruct((t, D), F32), jax.ShapeDtypeStruct((t, D), MXU_DT),
                   jax.ShapeDtypeStruct((t, D), MXU_DT), jax.ShapeDtypeStruct((8, D), F32)),
        compiler_params=_params(48),
    )(xh3, rs3, g3, b3, p, w_g, w_ple, bg, g4, b4, target)


def _ffn_bwd(dr, gact, uact, xin, rsin, gin, w_in, w_out, prev_ln, name, exchange=()):
    t = dr.shape[0]
    tm = _tile(t, 512)
    nt = t // tm
    chunks = _f_chunks()
    ne = len(exchange)

    def body(*refs):
        dr_ref, g_ref, u_ref, x_ref, rs_ref, gi_ref, win_hbm, wout_hbm = refs[:8]
        df_ref, dg_ref, du_ref, dx_ref, st_ref = refs[8 + ne:13 + ne]
        win_v, wout_v = refs[13 + 2 * ne:15 + 2 * ne]
        acc_ref = dx_ref
        if ne:
            e_start, e_finish = _exchange_phases(refs[8:8 + ne], refs[13 + ne:13 + 2 * ne], *refs[15 + 2 * ne:])

        @pl.when(pl.program_id(0) == 0)
        def _():
            if ne:
                e_start()
            pltpu.sync_copy(win_hbm, win_v)
            pltpu.sync_copy(wout_hbm, wout_v)
            st_ref[...] = jnp.zeros_like(st_ref)

        if ne:
            pl.when(pl.program_id(0) == nt - 1)(e_finish)

        drv = dr_ref[...]
        dfb = (0.5 * drv).astype(MXU_DT)
        df_ref[...] = dfb
        for ci, (c0, fc) in enumerate(chunks):
            dh = _mm_nt(dfb, wout_v[c0:c0 + fc, :])
            g = g_ref[:, c0:c0 + fc].astype(F32)
            u = u_ref[:, c0:c0 + fc].astype(F32)
            sg = _sigmoid(g)
            dgb = (dh * u * (sg * (1.0 + g * (1.0 - sg)))).astype(MXU_DT)
            dub = (dh * (g * sg)).astype(MXU_DT)
            dg_ref[:, c0:c0 + fc] = dgb
            du_ref[:, c0:c0 + fc] = dub
            part = _mm_nt(dgb, win_v[:, c0:c0 + fc]) + _mm_nt(dub, win_v[:, F + c0:F + c0 + fc])
            if ci == 0:
                acc_ref[...] = part
            else:
                acc_ref[...] += part
        dx = ALPHA * drv + acc_ref[...]
        if prev_ln:
            xh = x_ref[...]
            st_ref[0:1, :] += _colsum(dx * xh)
            st_ref[1:2, :] += _colsum(dx)
            dx_ref[...] = _ln_bwd(dx * gi_ref[...], xh, rs_ref[...])
        else:
            dx_ref[...] = dx

    row = pl.BlockSpec((tm, D), lambda i: (i, 0))
    vec = pl.BlockSpec((1, D), lambda i: (0, 0))
    act = pl.BlockSpec((tm, F), lambda i: (i, 0))
    return pl.pallas_call(
        body, name=name, grid=(nt,),
        in_specs=[row, act, act, row, pl.BlockSpec((tm, 1), lambda i: (i, 0)), vec, ANY, ANY] + [ANY] * ne,
        out_specs=(row, act, act, row, pl.BlockSpec((8, D), lambda i: (0, 0))) + (ANY,) * ne,
        out_shape=(jax.ShapeDtypeStruct((t, D), MXU_DT), jax.ShapeDtypeStruct((t, F), MXU_DT),
                   jax.ShapeDtypeStruct((t, F), MXU_DT), jax.ShapeDtypeStruct((t, D), F32),
                   jax.ShapeDtypeStruct((8, D), F32))
        + tuple(jax.ShapeDtypeStruct(a.shape, a.dtype) for a in exchange),
        scratch_shapes=[pltpu.VMEM((D, 2 * F), MXU_DT), pltpu.VMEM((F, D), MXU_DT)]
        + (_exchange_sems(ne) if ne else []),
        compiler_params=_params(60),
    )(dr, gact, uact, xin, rsin, gin, w_in, w_out, *exchange)


def _mix_post_bwd(dr2, o, bch, conv_w, g_attn, g_conv, w_mo, name):
    t = dr2.shape[0]
    tm = _tile(t, 512)
    hb = tm // 8

    def body(dr_ref, o_ref, bch_ref, halo_ref, cw_ref, ga_ref, gc_ref, w_ref,
             dm_ref, do_ref, dl_ref, dy_ref, st_ref, ext):
        i = pl.program_id(0)

        @pl.when(i == 0)
        def _():
            st_ref[...] = jnp.zeros_like(st_ref)

        dmb = dr_ref[...].astype(MXU_DT)
        dm_ref[...] = dmb
        dmg = _mm_nt(dmb, w_ref[...])
        ov = o_ref[...]
        an, ra = _rms_fwd(ov)
        da = dmg[:, 0:DA]
        st_ref[0:1, :] += _colsum(da * an)
        dxa = _rms_bwd(da * ga_ref[...], an, ra)
        dor = dxa.astype(MXU_DT).astype(F32)
        dot = dor.T
        for h in range(NH):
            do_ref[h, 0:DH, :] = dot[DH * h:DH * (h + 1)].astype(do_ref.dtype)
            do_ref[h, DH:LANES, :] = jnp.zeros((LANES - DH, tm), do_ref.dtype)
        srow = lax.broadcasted_iota(jnp.int32, (8, DA), 0)
        scol = lax.broadcasted_iota(jnp.int32, (8, DA), 1)
        sel = jnp.where((scol // DH) == srow, 1.0, 0.0).astype(MXU_DT)
        hi, mid, lo = _split3(dor * ov)
        delta = _mm_nt(sel, hi) + _mm_nt(sel, mid) + _mm_nt(sel, lo)
        for h in range(NH):
            dl_ref[h] = delta[h:h + 1, :]
        bb, cc, hh = _conv_parts(bch_ref[...])
        _, hc, hh_h = _conv_parts(halo_ref[...])
        u = cc * hh
        ext[0:8, :] = jnp.where(i > 0, hc * hh_h, 0.0)
        ext[8:8 + tm, :] = u
        raw = cw_ref[0:1, :] * ext[6:6 + tm, :] + cw_ref[1:2, :] * ext[7:7 + tm, :] + cw_ref[2:3, :] * u
        cn, rc = _rms_fwd(bb * raw)
        dcn = dmg[:, DA:D]
        st_ref[1:2, :] += _colsum(dcn * cn)
        dy_ref[...] = _rms_bwd(dcn * gc_ref[...], cn, rc)

    row = lambda w: pl.BlockSpec((tm, w), lambda i: (i, 0))
    full = lambda a: pl.BlockSpec(a.shape, lambda i: (0, 0))
    return pl.pallas_call(
        body, name=name, grid=(t // tm,),
        in_specs=[row(D), row(DA), row(3 * DCV),
                  pl.BlockSpec((8, 3 * DCV), lambda i: (jnp.maximum(i * hb - 1, 0), 0)),
                  full(conv_w), full(g_attn), full(g_conv), full(w_mo)],
        out_specs=(row(D), pl.BlockSpec((NH, LANES, tm), lambda i: (0, 0, i)),
                   pl.BlockSpec((NH, 1, tm), lambda i: (0, 0, i)), row(DCV),
                   pl.BlockSpec((8, DA), lambda i: (0, 0))),
        out_shape=(jax.ShapeDtypeStruct((t, D), MXU_DT), jax.ShapeDtypeStruct((NH, LANES, t), MXU_DT),
                   jax.ShapeDtypeStruct((NH, 1, t), F32), jax.ShapeDtypeStruct((t, DCV), F32),
                   jax.ShapeDtypeStruct((8, DA), F32)),
        scratch_shapes=[pltpu.VMEM((tm + 8, DCV), F32)],
        compiler_params=_params(48),
    )(dr2, o, bch, bch, conv_w, g_attn, g_conv, w_mo)


def _conv_bwd(dy, bch, conv_w, name):
    t = dy.shape[0]
    tm = _tile(t, 512)
    hb = tm // 8
    nt = t // tm

    def body(dy_ref, dyn_ref, bch_ref, prev_ref, next_ref, cw_ref, out_ref, st_ref, ext_u, ext_d):
        i = pl.program_id(0)

        @pl.when(i == 0)
        def _():
            st_ref[...] = jnp.zeros_like(st_ref)

        bb, cc, hh = _conv_parts(bch_ref[...])
        _, pc, ph = _conv_parts(prev_ref[...])
        nb, _, _ = _conv_parts(next_ref[...])
        u = cc * hh
        ext_u[0:8, :] = jnp.where(i > 0, pc * ph, 0.0)
        ext_u[8:8 + tm, :] = u
        u1 = ext_u[7:7 + tm, :]
        u2 = ext_u[6:6 + tm, :]
        w0, w1, w2 = cw_ref[0:1, :], cw_ref[1:2, :], cw_ref[2:3, :]
        dyv = dy_ref[...]
        out_ref[:, 0:DCV] = (dyv * (w0 * u2 + w1 * u1 + w2 * u)).astype(out_ref.dtype)
        dcr = dyv * bb
        ext_d[0:tm, :] = dcr
        ext_d[tm:tm + 8, :] = jnp.where(i < nt - 1, dyn_ref[...] * nb, 0.0)
        du = w2 * dcr + w1 * ext_d[1:1 + tm, :] + w0 * ext_d[2:2 + tm, :]
        out_ref[:, DCV:2 * DCV] = (du * hh).astype(out_ref.dtype)
        out_ref[:, 2 * DCV:3 * DCV] = (du * cc).astype(out_ref.dtype)
        st_ref[0:1, :] += _colsum(dcr * u2)
        st_ref[1:2, :] += _colsum(dcr * u1)
        st_ref[2:3, :] += _colsum(dcr * u)

    row = lambda w: pl.BlockSpec((tm, w), lambda i: (i, 0))
    prev = lambda w: pl.BlockSpec((8, w), lambda i: (jnp.maximum(i * hb - 1, 0), 0))
    nxt = lambda w: pl.BlockSpec((8, w), lambda i: (jnp.minimum((i + 1) * hb, nt * hb - 1), 0))
    return pl.pallas_call(
        body, name=name, grid=(nt,),
        in_specs=[row(DCV), nxt(DCV), row(3 * DCV), prev(3 * DCV), nxt(3 * DCV),
                  pl.BlockSpec(conv_w.shape, lambda i: (0, 0))],
        out_specs=(row(3 * DCV), pl.BlockSpec((8, DCV), lambda i: (0, 0))),
        out_shape=(jax.ShapeDtypeStruct((t, 3 * DCV), MXU_DT), jax.ShapeDtypeStruct((8, DCV), F32)),
        scratch_shapes=[pltpu.VMEM((tm + 8, DCV), F32), pltpu.VMEM((tm + 8, DCV), F32)],
        compiler_params=_params(48),
    )(dy, dy, bch, bch, bch, conv_w)


def _attn_bwd(ka, kat, va, qat, dot, lrow, drow, r, name):
    t = ka.shape[1]
    tq = _tile(t, 512)
    nq = t // tq

    def body(r_ref, ka_ref, kat_ref, va_ref, l_ref, dl_ref, qat_v, dot_v,
             dk_ref, dv_ref, dck_ref, dqt_hbm, dcq_hbm, dq_acc):
        hp, j = pl.program_id(0), pl.program_id(1)

        @pl.when(j == 0)
        def _():
            dq_acc[...] = jnp.zeros_like(dq_acc)

        key = lax.broadcasted_iota(jnp.int32, (tq, tq), 0)
        qry = lax.broadcasted_iota(jnp.int32, (tq, tq), 1)

        def step(i, carry, masked):
            off = pl.multiple_of(i * tq, tq)
            out = []
            for a in range(2):
                dk, dv = carry[a]
                st = _mm(ka_ref[a], qat_v[a, :, pl.ds(off, tq)])
                dpt = _mm(va_ref[a], dot_v[a, :, pl.ds(off, tq)])
                if masked:
                    st = jnp.where(qry >= key, st, NEG)
                d = r_ref[2 * hp + a, i] - r_ref[2 * hp + a, j]
                pt = jnp.exp(st - (l_ref[a, :, pl.ds(off, tq)] - d))
                dsb = (pt * (dpt - dl_ref[a, :, pl.ds(off, tq)])).astype(MXU_DT)
                dv = dv + _mm_nt(dot_v[a, 0:DH, pl.ds(off, tq)], pt.astype(MXU_DT))
                dk = dk + _mm_nt(qat_v[a, 0:QROWS, pl.ds(off, tq)], dsb)
                dq_acc[a, :, pl.ds(off, tq)] += _mm(kat_ref[a], dsb)
                out.append((dk, dv))
            return tuple(out)

        init = tuple((jnp.zeros((QROWS, tq), F32), jnp.zeros((DH, tq), F32)) for _ in range(2))
        carry = step(j, init, True)
        (dka, dva), (dkb, dvb) = lax.fori_loop(j + 1, nq, lambda i, cr: step(i, cr, False), carry)
        dk_ref[...] = jnp.concatenate([dka[0:DH], dkb[0:DH]], axis=0).T.astype(dk_ref.dtype)
        dv_ref[...] = jnp.concatenate([dva, dvb], axis=0).T.astype(dv_ref.dtype)
        dck_ref[0] = -dka[DH + 3:DH + 4, :]
        dck_ref[1] = -dkb[DH + 3:DH + 4, :]

        @pl.when(j == nq - 1)
        def _():
            pltpu.sync_copy(dq_acc, dqt_hbm.at[pl.ds(2 * hp, 2)])
            pltpu.sync_copy(dq_acc.at[:, DH:DH + 1, :], dcq_hbm.at[pl.ds(2 * hp, 2)])

    pair = lambda rows, cols: pl.BlockSpec((2, rows, cols), lambda p, j: (p, 0, 0))
    return pl.pallas_call(
        body, name=name, grid=(NH // 2, nq),
        in_specs=[pl.BlockSpec(memory_space=pltpu.SMEM),
                  pl.BlockSpec((2, tq, LANES), lambda p, j: (p, j, 0)),
                  pl.BlockSpec((2, QROWS, tq), lambda p, j: (p, 0, j)),
                  pl.BlockSpec((2, tq, LANES), lambda p, j: (p, j, 0)),
                  pair(1, t), pair(1, t), pair(LANES, t), pair(LANES, t)],
        out_specs=(pl.BlockSpec((tq, LANES), lambda p, j: (j, p)),
                   pl.BlockSpec((tq, LANES), lambda p, j: (j, p)),
                   pl.BlockSpec((2, 1, tq), lambda p, j: (p, 0, j)), ANY, ANY),
        out_shape=(jax.ShapeDtypeStruct((t, DA), MXU_DT), jax.ShapeDtypeStruct((t, DA), MXU_DT),
                   jax.ShapeDtypeStruct((NH, 1, t), F32), jax.ShapeDtypeStruct((NH, QROWS, t), F32),
                   jax.ShapeDtypeStruct((NH, 1, t), F32)),
        scratch_shapes=[pltpu.VMEM((2, QROWS, t), F32)],
        compiler_params=_params(52, 2),
    )(r, ka, kat, va, lrow, drow, qat, dot)


def _mix_proj_bwd(dr2, dqt, dk, dv, dbch, dc, z, xh1, rs1, g1, w_qkv, w_bch, w_f, name, exchange=()):
    t = dr2.shape[0]
    tm = _tile(t, 512)
    nt = t // tm
    ne = len(exchange)

    def body(*refs):
        (dr_ref, dqt_ref, dk_ref, dv_ref, db_ref, dc_ref, z_ref, x_ref, rs_ref, g_ref,
         wq_ref, wb_ref, wf_ref) = refs[:13]
        out_ref, df_ref, dq_ref, st_ref = refs[13 + ne:17 + ne]
        carry = refs[17 + 2 * ne]
        if ne:
            e_start, e_finish = _exchange_phases(refs[13:13 + ne], refs[17 + ne:17 + 2 * ne], *refs[18 + 2 * ne:])
            pl.when(pl.program_id(0) == 0)(e_start)

        @pl.when(pl.program_id(0) == 0)
        def _():
            carry[...] = jnp.zeros_like(carry)
            st_ref[...] = jnp.zeros_like(st_ref)

        dq_ref[...] = (jnp.concatenate([dqt_ref[h, 0:DH, :] for h in range(NH)], axis=0).T * 0.125
                       ).astype(dq_ref.dtype)

        row = lax.broadcasted_iota(jnp.int32, (tm, tm), 0)
        col = lax.broadcasted_iota(jnp.int32, (tm, tm), 1)
        triu = jnp.where(col >= row, 1.0, 0.0).astype(MXU_DT)
        dlogf = carry[...] + _mm_sel(triu, dc_ref[...])
        carry[...] = dlogf[0:1, :]
        dz = dlogf / (1.0 + jnp.exp(z_ref[...]))
        st_ref[2:3, 0:LANES] += _colsum(dz)
        dfb = dz.astype(MXU_DT)
        df_ref[...] = dfb
        dx = (ALPHA * dr_ref[...]
              + _mm_nt(dq_ref[...], wq_ref[:, 0:DA])
              + _mm_nt(dk_ref[...], wq_ref[:, DA:2 * DA])
              + _mm_nt(dv_ref[...], wq_ref[:, 2 * DA:3 * DA])
              + _mm_nt(db_ref[...], wb_ref[...])
              + _mm_nt(dfb, wf_ref[...]))
        xh = x_ref[...]
        st_ref[0:1, :] += _colsum(dx * xh)
        st_ref[1:2, :] += _colsum(dx)
        out_ref[...] = _ln_bwd(dx * g_ref[...], xh, rs_ref[...])
        if ne:
            pl.when(pl.program_id(0) == nt - 1)(e_finish)

    row = lambda w: pl.BlockSpec((tm, w), lambda i: (nt - 1 - i, 0))
    full = lambda a: pl.BlockSpec(a.shape, lambda i: (0, 0))
    res = pl.pallas_call(
        body, name=name, grid=(nt,),
        in_specs=[row(D), pl.BlockSpec((NH, QROWS, tm), lambda i: (0, 0, nt - 1 - i)), row(DA), row(DA),
                  row(3 * DCV), row(LANES), row(LANES), row(D), row(1),
                  full(g1), full(w_qkv), full(w_bch), full(w_f)] + [ANY] * ne,
        out_specs=(row(D), row(LANES), row(DA), pl.BlockSpec((8, D), lambda i: (0, 0))) + (ANY,) * ne,
        out_shape=(jax.ShapeDtypeStruct((t, D), F32), jax.ShapeDtypeStruct((t, LANES), MXU_DT),
                   jax.ShapeDtypeStruct((t, DA), MXU_DT), jax.ShapeDtypeStruct((8, D), F32))
        + tuple(jax.ShapeDtypeStruct(a.shape, a.dtype) for a in exchange),
        scratch_shapes=[pltpu.VMEM((1, LANES), F32)] + (_exchange_sems(ne) if ne else []),
        compiler_params=_params(48),
    )(dr2, dqt, dk, dv, dbch, dc, z, xh1, rs1, g1, w_qkv, w_bch, w_f, *exchange)
    return res[:4] + (res[4:],)


def _dw(mode, a_parts, b, m, n, name, tmm=None, tn=None, exchange=()):
    t = b.shape[0]
    tmm = tmm or m
    tn = tn or n
    tt = _tile(t, 2048)
    na, ne = len(a_parts), len(exchange)
    grid = (m // tmm, n // tn, t // tt)

    def body(*refs):
        a_refs, b_ref, o_ref = refs[:na], refs[na], refs[na + 1 + ne]
        if ne:
            e_start, e_finish = _exchange_phases(refs[na + 1:na + 1 + ne], refs[na + 2 + ne:na + 2 + 2 * ne],
                                                 *refs[na + 2 + 2 * ne:])
            at = lambda steps: functools.reduce(jnp.logical_and, [pl.program_id(d) == s for d, s in enumerate(steps)])
            pl.when(at((0, 0, 0)))(e_start)

        @pl.when(pl.program_id(2) == 0)
        def _():
            o_ref[...] = jnp.zeros_like(o_ref)

        if mode == "plain":
            a = a_refs[0][...].astype(MXU_DT)
        elif mode == "affine":
            a = (a_refs[0][...] * a_refs[1][...] + a_refs[2][...]).astype(MXU_DT)
        else:
            g = a_refs[0][...].astype(F32)
            a = (g * _sigmoid(g) * a_refs[1][...].astype(F32)).astype(MXU_DT)
        o_ref[...] += _mm_tn(a, b_ref[...].astype(MXU_DT))
        if ne:
            pl.when(at(tuple(g - 1 for g in grid)))(e_finish)

    a_tile = pl.BlockSpec((tt, tmm), lambda i, j, k: (k, i))
    a_vec = pl.BlockSpec((1, tmm), lambda i, j, k: (0, i))
    a_specs = {"plain": [a_tile], "affine": [a_tile, a_vec, a_vec], "swiglu": [a_tile, a_tile]}[mode]
    res = pl.pallas_call(
        body, name=name, grid=grid,
        in_specs=a_specs + [pl.BlockSpec((tt, tn), lambda i, j, k: (k, j))] + [ANY] * ne,
        out_specs=(pl.BlockSpec((tmm, tn), lambda i, j, k: (i, j)),) + (ANY,) * ne,
        out_shape=(jax.ShapeDtypeStruct((m, n), F32),)
        + tuple(jax.ShapeDtypeStruct(a.shape, a.dtype) for a in exchange),
        scratch_shapes=_exchange_sems(ne) if ne else [],
        compiler_params=_params(52, 3),
    )(*a_parts, b, *exchange)
    return res if ne else res[0]


def _dw_shared(a_parts, bs, name, exchange=()):
    xh, g, b = a_parts
    t, m = xh.shape
    tt = _tile(t, 1024)
    nk = t // tt
    nb, ne = len(bs), len(exchange)

    def body(*refs):
        x_ref, g_ref, b_ref = refs[:3]
        b_refs, o_refs = refs[3:3 + nb], refs[3 + nb + ne:3 + 2 * nb + ne]
        if ne:
            e_start, e_finish = _exchange_phases(refs[3 + nb:3 + nb + ne], refs[3 + 2 * nb + ne:3 + 2 * nb + 2 * ne],
                                                 *refs[3 + 2 * nb + 2 * ne:])
            pl.when(pl.program_id(0) == 0)(e_start)

        @pl.when(pl.program_id(0) == 0)
        def _():
            for o_ref in o_refs:
                o_ref[...] = jnp.zeros_like(o_ref)

        at = (x_ref[...] * g_ref[...] + b_ref[...]).T.astype(MXU_DT)
        for b_ref, o_ref in zip(b_refs, o_refs):
            o_ref[...] += _mm(at, b_ref[...].astype(MXU_DT))
        if ne:
            pl.when(pl.program_id(0) == nk - 1)(e_finish)

    vec = pl.BlockSpec((1, m), lambda k: (0, 0))
    res = pl.pallas_call(
        body, name=name, grid=(nk,),
        in_specs=[pl.BlockSpec((tt, m), lambda k: (k, 0)), vec, vec]
        + [pl.BlockSpec((tt, x.shape[1]), lambda k: (k, 0)) for x in bs] + [ANY] * ne,
        out_specs=tuple(pl.BlockSpec((m, x.shape[1]), lambda k: (0, 0)) for x in bs) + (ANY,) * ne,
        out_shape=tuple(jax.ShapeDtypeStruct((m, x.shape[1]), F32) for x in bs)
        + tuple(jax.ShapeDtypeStruct(a.shape, a.dtype) for a in exchange),
        scratch_shapes=_exchange_sems(ne) if ne else [],
        compiler_params=_params(56),
    )(xh, g, b, *bs, *exchange)
    return res[:nb], res[nb:]


def _adamw(w, g, m, v):
    m = ADAM_B1 * m + (1.0 - ADAM_B1) * g
    v = ADAM_B2 * v + (1.0 - ADAM_B2) * (g * g)
    m_hat = m / (1.0 - ADAM_B1 ** ADAM_STEP)
    v_hat = v / (1.0 - ADAM_B2 ** ADAM_STEP)
    delta = -ADAM_LR * (m_hat / (jnp.sqrt(v_hat) + ADAM_EPS) + ADAM_WD * w)
    return delta, m, v


def _reduce_adamw(landed, own, w, m, v, name):
    r, c = own.shape
    tr = _tile(r, 128)

    def body(l_ref, o_ref, w_ref, m_ref, v_ref, g_out, d_out, m_out, v_out):
        me = 4 * lax.axis_index("x") + 2 * lax.axis_index("y") + lax.axis_index("c")
        g = None
        for j in range(NDEV):
            term = jnp.where(me == j, o_ref[...], l_ref[j].astype(F32))
            g = term if g is None else g + term
        g_out[...] = g
        d_out[...], m_out[...], v_out[...] = _adamw(w_ref[...], g, m_ref[...], v_ref[...])

    blk = pl.BlockSpec((tr, c), lambda i: (i, 0))
    sds = jax.ShapeDtypeStruct((r, c), F32)
    return pl.pallas_call(
        body, name=name, grid=(r // tr,),
        in_specs=[pl.BlockSpec((NDEV, tr, c), lambda i: (0, i, 0)), blk, blk, blk, blk],
        out_specs=(blk, blk, blk, blk), out_shape=(sds, sds, sds, sds),
        compiler_params=_params(40),
    )(landed, own, w, m, v)


def _sum_small(gathered, name):
    _, r, c = gathered.shape

    def body(g_ref, o_ref):
        acc = g_ref[0]
        for j in range(1, NDEV):
            acc = acc + g_ref[j]
        o_ref[...] = acc

    return pl.pallas_call(body, name=name, out_shape=jax.ShapeDtypeStruct((r, c), F32))(gathered)


def _adamw_small(small_g, conv_g, params, name):
    names = list(params)
    flat = [a for nm in names for a in params[nm]]

    def body(*refs):
        g_ref, c_ref = refs[0], refs[1]
        ins, outs = refs[2:2 + len(flat)], refs[2 + len(flat):]
        for k, nm in enumerate(names):
            w_ref, m_ref, v_ref = ins[3 * k:3 * k + 3]
            if nm == "conv_w":
                g = c_ref[...]
            else:
                r, off, wd = SMALL_SLOTS[nm]
                g = g_ref[r:r + 1, off:off + wd]
            g_out, d_out, m_out, v_out = outs[4 * k:4 * k + 4]
            g_out[...] = g
            d_out[...], m_out[...], v_out[...] = _adamw(w_ref[...], g, m_ref[...], v_ref[...])

    res = pl.pallas_call(
        body, name=name,
        out_shape=tuple(jax.ShapeDtypeStruct(params[nm][0].shape, F32) for nm in names for _ in range(4)),
    )(small_g, conv_g, *flat)
    return {nm: res[4 * k:4 * k + 4] for k, nm in enumerate(names)}


def _cols_from_stack(s):
    return jnp.transpose(s, (1, 0, 2)).reshape(s.shape[1], NDEV * s.shape[2])


def _cols_to_stack(w):
    r, c = w.shape
    return jnp.transpose(w.reshape(r, NDEV, c // NDEV), (1, 0, 2))


def _rows_from_stack(s):
    return s.reshape(NDEV * s.shape[1], s.shape[2])


def _rows_to_stack(w):
    r, c = w.shape
    return w.reshape(NDEV, r // NDEV, c)


SMALL_ROWS = 16
SMALL_SLOTS = {
    "ln1_g": (0, 0, D), "ln1_b": (1, 0, D), "ln2_g": (2, 0, D), "ln2_b": (3, 0, D), "ln3_g": (4, 0, D),
    "ln3_b": (5, 0, D), "b_ple_gate": (6, 0, D), "ln4_g": (7, 0, D), "ln4_b": (8, 0, D),
    "g_attn": (9, 0, DA), "g_conv": (9, DA, DCV), "b_forget": (10, 0, NH),
}
CONVW_ROW = 11
LOSS_SLOT = (10, LANES)


def _pack_small(vals, conv_rows, loss=None):
    out = jnp.zeros((SMALL_ROWS, D), F32)
    for nm, (r, off, wd) in SMALL_SLOTS.items():
        out = out.at[r:r + 1, off:off + wd].set(vals[nm].reshape(1, wd).astype(F32))
    out = out.at[CONVW_ROW:CONVW_ROW + 3, 0:conv_rows.shape[1]].set(conv_rows.astype(F32))
    if loss is not None:
        out = out.at[LOSS_SLOT[0], LOSS_SLOT[1]].set(loss)
    return out


def kernel(x, p, ffn1_w_in, ffn1_w_out, ln1_g, ln1_b, w_mix_in, b_forget, conv_w, g_attn, g_conv, w_mix_out, ln2_g, ln2_b, ffn2_w_in, ffn2_w_out, ln3_g, ln3_b, w_ple, w_ple_gate, b_ple_gate, ln4_g, ln4_b, loss_target, m_ffn1_w_in, m_ffn1_w_out, m_ln1_g, m_ln1_b, m_w_mix_in, m_b_forget, m_conv_w, m_g_attn, m_g_conv, m_w_mix_out, m_ln2_g, m_ln2_b, m_ffn2_w_in, m_ffn2_w_out, m_ln3_g, m_ln3_b, m_w_ple, m_w_ple_gate, m_b_ple_gate, m_ln4_g, m_ln4_b, v_ffn1_w_in, v_ffn1_w_out, v_ln1_g, v_ln1_b, v_w_mix_in, v_b_forget, v_conv_w, v_g_attn, v_g_conv, v_w_mix_out, v_ln2_g, v_ln2_b, v_ffn2_w_in, v_ffn2_w_out, v_ln3_g, v_ln3_b, v_w_ple, v_w_ple_gate, v_b_ple_gate, v_ln4_g, v_ln4_b):
    args = dict(locals())
    t = x.shape[1]
    me = 4 * lax.axis_index("x") + 2 * lax.axis_index("y") + lax.axis_index("c")
    x0 = x.reshape(t, D)
    p0 = p.reshape(t, PLE)
    tgt = loss_target.reshape(t, D)

    big = ["ffn1_w_in", "ffn1_w_out", "w_mix_in", "w_mix_out", "ffn2_w_in", "ffn2_w_out", "w_ple", "w_ple_gate"]
    col_sharded = {"ffn1_w_in", "w_mix_in", "ffn2_w_in", "w_ple"}
    shard = {nm: args[nm][0] for nm in big}

    unstack = lambda nm, g: (_cols_from_stack(g) if nm in col_sharded else _rows_from_stack(g)).astype(MXU_DT)
    stack = lambda nm, g: _cols_to_stack(g) if nm in col_sharded else _rows_to_stack(g)
    wire = lambda names: [shard[nm].astype(WIRE_DT) for nm in names]
    first, later = big[:2], big[2:]

    full = {nm: unstack(nm, g) for nm, g in zip(first, _allgather(wire(first), "ag_ffn1"))}
    ffn1_out, gathered = _ffn1_fwd(x0, full, wire(later) + [conv_w[0]])
    full.update({nm: unstack(nm, g) for nm, g in zip(later, gathered)})
    cw = _cols_from_stack(gathered[len(later)])

    slots = lambda nm, g: stack(nm, g).astype(WIRE_DT)
    dr1, gw, small, loss_part, landed = _mid_step(p0, tgt, full, cw, {nm: args[nm] for nm in SMALL_SLOTS},
                                                  ffn1_out, slots)
    small_part = _pack_small({nm: small[nm] for nm in SMALL_SLOTS},
                             jnp.pad(small["conv_w"], ((0, 0), (0, D - DCV))), loss_part)
    gx, gw1, _, (landed["w_mix_in"],), landed["ffn1_w_in"] = _ffn1_bwd(
        x0, dr1, ffn1_out, full, (), [slots("w_mix_in", gw["w_mix_in"])], lambda g: slots("ffn1_w_in", g))
    stacks = {nm: stack(nm, g) for nm, g in {**gw, **gw1}.items()}
    (landed["ffn1_w_out"],), (small_all,) = _exchange_and_gather([stacks["ffn1_w_out"].astype(WIRE_DT)],
                                                                 [small_part], "rs_ffn1_out")
    small_g = _sum_small(small_all, "sum_small")
    loss = small_g[LOSS_SLOT[0], LOSS_SLOT[1]]

    outs = {"loss": loss, "grad_x": gx.reshape(1, t, D)}
    for nm in big:
        own = lax.dynamic_index_in_dim(stacks[nm], me, axis=0, keepdims=False)
        g, dl, mn, vn = _reduce_adamw(landed[nm], own, shard[nm], args["m_" + nm][0], args["v_" + nm][0],
                                      "adamw_" + nm)
        outs["grad_" + nm], outs["delta_" + nm], outs["new_m_" + nm], outs["new_v_" + nm] = (
            g[None], dl[None], mn[None], vn[None])
    cshard = lax.dynamic_slice_in_dim(small_g[CONVW_ROW:CONVW_ROW + 3, 0:DCV], me * (DCV // NDEV), DCV // NDEV, axis=1)
    params = {nm: tuple(args[pre + nm] for pre in ("", "m_", "v_")) for nm in SMALL_SLOTS}
    params["conv_w"] = tuple(args[pre + "conv_w"][0] for pre in ("", "m_", "v_"))
    for nm, res in _adamw_small(small_g, cshard, params, "adamw_small").items():
        for key, val in zip(("grad_", "delta_", "new_m_", "new_v_"), res):
            outs[key + nm] = val[None] if nm == "conv_w" else val

    wnames = ["ffn1_w_in", "ffn1_w_out", "ln1_g", "ln1_b", "w_mix_in", "b_forget", "conv_w", "g_attn", "g_conv",
              "w_mix_out", "ln2_g", "ln2_b", "ffn2_w_in", "ffn2_w_out", "ln3_g", "ln3_b", "w_ple", "w_ple_gate",
              "b_ple_gate", "ln4_g", "ln4_b"]
    return (outs["loss"], outs["grad_x"], *[outs[pre + nm] for pre in ("grad_", "delta_", "new_m_", "new_v_")
                                            for nm in wnames])


def _ffn1_fwd(x0, full, gather=()):
    res = _ffn_fwd(x0, jnp.ones((1, D), F32), jnp.zeros((1, D), F32), full["ffn1_w_in"], full["ffn1_w_out"],
                   "ffn1_fwd", gather)
    return res[:4], res[4:]


def _ffn1_bwd(x0, dr1, ffn1_out, full, exchange=(), exchange_late=(), w_in_slots=None):
    g1a, u1a, _, rs1 = ffn1_out
    ones, zeros = jnp.ones((1, D), F32), jnp.zeros((1, D), F32)
    res = _ffn_bwd(dr1, g1a, u1a, x0, rs1, ones, full["ffn1_w_in"], full["ffn1_w_out"], False, "ffn1_bwd",
                   exchange)
    df1, dg1, du1, gx = res[:4]
    dw_g = _dw("affine", (x0, ones, zeros), dg1, D, F, "dw_ffn1_in_g", tn=F // 2, exchange=exchange_late)
    dw_g, landed_late = (dw_g[0], dw_g[1:]) if exchange_late else (dw_g, ())
    gw_in = jnp.concatenate([dw_g, _dw("affine", (x0, ones, zeros), du1, D, F, "dw_ffn1_in_u", tn=F // 2)], axis=1)
    side = () if w_in_slots is None else (w_in_slots(gw_in),)
    out = _dw("swiglu", (g1a, u1a), df1, F, D, "dw_ffn1_out", tmm=F // 2, exchange=side)
    gw_out, landed_in = (out, None) if w_in_slots is None else (out[0], out[1])
    return gx, {"ffn1_w_in": gw_in, "ffn1_w_out": gw_out}, res[5:], landed_late, landed_in


def _mid_step(p0, tgt, full, cw, sp, ffn1_out, slots):
    g1a, u1a, xh1, rs1 = ffn1_out
    t = xh1.shape[0]
    ln1_g, ln1_b, ln2_g, ln2_b, ln3_g, ln3_b = (sp[k] for k in ("ln1_g", "ln1_b", "ln2_g", "ln2_b", "ln3_g", "ln3_b"))
    ln4_g, ln4_b, g_attn, g_conv, b_ple_gate = (sp[k] for k in ("ln4_g", "ln4_b", "g_attn", "g_conv", "b_ple_gate"))
    wmi = full["w_mix_in"]
    w_qkv = wmi[:, 0:3 * DA]
    w_f = jnp.pad(wmi[:, 3 * DA:3 * DA + NH], ((0, 0), (0, LANES - NH)))
    w_bch = wmi[:, 3 * DA + NH:]
    bf_pad = jnp.pad(sp["b_forget"], ((0, 0), (0, LANES - NH)))

    ka, va, qat, kat, vta, bch, z, rt = _mix_proj_fwd(xh1, ln1_g, ln1_b, w_qkv[:, DA:], jnp.transpose(w_qkv),
                                                      w_bch, w_f, bf_pad, "mix_proj_fwd")
    rtile = jnp.transpose(rt[:, 0, 0:NH])
    o, lse = _attn_fwd(qat, ka, vta, rtile, "attn_fwd")
    merged, xh2, rs2 = _mix_post_fwd(o, bch, cw, g_attn, g_conv, xh1, ln1_g, ln1_b, full["w_mix_out"],
                                     "mix_post_fwd")
    g2a, u2a, xh3, rs3 = _ffn_fwd(xh2, ln2_g, ln2_b, full["ffn2_w_in"], full["ffn2_w_out"], "ffn2_fwd")

    dr3, dz, de, st_tail = _tail(xh3, rs3, ln3_g, ln3_b, p0, full["w_ple_gate"], full["w_ple"], b_ple_gate,
                                 ln4_g, ln4_b, tgt, "tail")
    df2, dg2, du2, dr2, st_f2 = _ffn_bwd(dr3, g2a, u2a, xh2, rs2, ln2_g, full["ffn2_w_in"], full["ffn2_w_out"],
                                         True, "ffn2_bwd")
    dmix, dot, drow, dyc, st_post = _mix_post_bwd(dr2, o, bch, cw, g_attn, g_conv, full["w_mix_out"],
                                                  "mix_post_bwd")
    dbch, st_conv = _conv_bwd(dyc, bch, cw, "conv_bwd")
    dk, dv, dck, dqt, dcq = _attn_bwd(ka, kat, va, qat, dot, lse, drow, rtile, "attn_bwd")
    dc_pad = jnp.pad(jnp.transpose((dcq + dck).reshape(NH, t)), ((0, 0), (0, LANES - NH)))

    x1p, x2p, x3p = (xh1, ln1_g, ln1_b), (xh2, ln2_g, ln2_b), (xh3, ln3_g, ln3_b)
    gw = {}
    gw["ffn2_w_in"] = jnp.concatenate(
        [_dw("affine", x2p, dg2, D, F, "dw_ffn2_in_g", tn=F // 2),
         _dw("affine", x2p, du2, D, F, "dw_ffn2_in_u", tn=F // 2)], axis=1)
    gw["ffn2_w_out"] = _dw("swiglu", (g2a, u2a), df2, F, D, "dw_ffn2_out", tmm=F // 2)
    gw["w_mix_out"] = _dw("plain", (merged,), dmix, D, D, "dw_mix_out")
    gw["w_ple_gate"] = _dw("affine", x3p, dz, D, D, "dw_ple_gate")
    gw["w_ple"] = _dw("plain", (p0,), de, PLE, D, "dw_ple")
    early_a, early_b = ("ffn2_w_in", "w_ple_gate", "w_ple"), ("ffn2_w_out", "w_mix_out")
    travel = lambda names: [slots(nm, gw[nm]) for nm in names] if slots else []
    dr1, dfl, dq, st_proj, landed_a = _mix_proj_bwd(dr2, dqt, dk, dv, dbch, dc_pad, z, xh1, rs1, ln1_g, w_qkv, w_bch,
                                                    w_f, "mix_proj_bwd", travel(early_a))
    (gq, gk, gv, gf, gbch), landed_b = _dw_shared(x1p, (dq, dk, dv, dfl, dbch), "dw_mix_in", travel(early_b))
    gw["w_mix_in"] = jnp.concatenate([gq, gk, gv, gf[:, 0:NH], gbch], axis=1)
    landed = dict(zip(early_a + early_b, tuple(landed_a) + tuple(landed_b)))

    loss_part = (0.5 / D) * jnp.sum(st_tail[5:6, :])
    small = {"ln1_g": st_proj[0:1], "ln1_b": st_proj[1:2], "ln2_g": st_f2[0:1], "ln2_b": st_f2[1:2],
             "ln3_g": st_tail[3:4], "ln3_b": st_tail[4:5], "b_ple_gate": st_tail[2:3], "ln4_g": st_tail[0:1],
             "ln4_b": st_tail[1:2], "g_attn": st_post[0:1], "g_conv": st_post[1:2],
             "b_forget": st_proj[2:3, 0:NH], "conv_w": st_conv[0:3]}
    return dr1, gw, small, loss_part, landed
```

```python
import functools

import jax
import jax.numpy as jnp
from jax import lax
from jax.experimental import pallas as pl
from jax.experimental.pallas import tpu as pltpu

D = 1024
F = 2816
NH = 8
DH = 64
DA = NH * DH
DCV = D - DA
PLE = 256
LN_EPS = 1e-5
RMS_EPS = 1e-6
NEG = -1e30
ALPHA = 2.0 ** 0.25
NDEV = 8
LANES = 128

ADAM_LR, ADAM_B1, ADAM_B2, ADAM_EPS, ADAM_WD, ADAM_STEP = 0.001, 0.9, 0.999, 1e-08, 0.01, 10

F32 = jnp.float32
MXU_DT = jnp.bfloat16
WIRE_DT = jnp.bfloat16

MESH_ID = pl.DeviceIdType.MESH
ANY = pl.BlockSpec(memory_space=pl.ANY)


def _params(vmem_mb, n_axes=1):
    return pltpu.CompilerParams(dimension_semantics=("arbitrary",) * n_axes,
                                vmem_limit_bytes=int(vmem_mb) << 20)


def _mm(a, b):
    return jnp.dot(a, b, preferred_element_type=F32)


def _mm_nt(a, b):
    return lax.dot_general(a, b, (((1,), (1,)), ((), ())), preferred_element_type=F32)


def _mm_tn(a, b):
    return lax.dot_general(a, b, (((0,), (0,)), ((), ())), preferred_element_type=F32)


def _split3(x):
    hi = x.astype(MXU_DT)
    r1 = x - hi.astype(F32)
    mid = r1.astype(MXU_DT)
    lo = (r1 - mid.astype(F32)).astype(MXU_DT)
    return hi, mid, lo


def _mm_sel(sel, x):
    hi, mid, lo = _split3(x)
    return _mm(sel, hi) + _mm(sel, mid) + _mm(sel, lo)


def _sigmoid(x):
    return 1.0 / (1.0 + jnp.exp(-x))


def _ln_fwd(r):
    mu = jnp.mean(r, axis=-1, keepdims=True)
    xc = r - mu
    var = jnp.mean(xc * xc, axis=-1, keepdims=True)
    rstd = lax.rsqrt(var + LN_EPS)
    return xc * rstd, rstd


def _ln_bwd(dxhat, xhat, rstd):
    m1 = jnp.mean(dxhat, axis=-1, keepdims=True)
    m2 = jnp.mean(dxhat * xhat, axis=-1, keepdims=True)
    return rstd * (dxhat - m1 - xhat * m2)


def _rms_fwd(x):
    r = lax.rsqrt(jnp.mean(x * x, axis=-1, keepdims=True) + RMS_EPS)
    return x * r, r


def _rms_bwd(dyg, xn, r):
    return r * (dyg - xn * jnp.mean(dyg * xn, axis=-1, keepdims=True))


def _colsum(x):
    return jnp.sum(x, axis=0, keepdims=True)


def _f_chunks():
    out, c0 = [], 0
    while c0 < F:
        fc = min(512, F - c0)
        out.append((c0, fc))
        c0 += fc
    return out


def _tile(t, want):
    return want if t % want == 0 and t >= want else t


def _exchange_sems(n):
    return [pltpu.SemaphoreType.DMA((n * (NDEV - 1),)), pltpu.SemaphoreType.DMA((n * (NDEV - 1),))]


def _exchange_phases(ins, outs, send_sems, recv_sems):
    n = len(ins)

    def peers():
        x, y, c = lax.axis_index("x"), lax.axis_index("y"), lax.axis_index("c")
        out = []
        for k in range(1, NDEV):
            px = 1 - x if (k >> 2) & 1 else x
            py = 1 - y if (k >> 1) & 1 else y
            pc = 1 - c if k & 1 else c
            out.append(((px, py, pc), 4 * px + 2 * py + pc))
        return 4 * x + 2 * y + c, out

    def remote(w, k, to, slot_src, slot_dst):
        return pltpu.make_async_remote_copy(
            src_ref=ins[w].at[slot_src], dst_ref=outs[w].at[slot_dst],
            send_sem=send_sems.at[w * (NDEV - 1) + k], recv_sem=recv_sems.at[w * (NDEV - 1) + k],
            device_id=to, device_id_type=MESH_ID)

    def start():
        me, prs = peers()
        for k, (to, pid) in enumerate(prs):
            for w in range(n):
                remote(w, k, to, pid, me).start()

    def finish():
        me, prs = peers()
        for k, (to, pid) in enumerate(prs):
            for w in range(n):
                remote(w, k, to, me, pid).wait_recv()
        for k, (to, pid) in enumerate(prs):
            for w in range(n):
                remote(w, k, to, pid, me).wait_send()

    return start, finish


def _gather_sems(n):
    return [pltpu.SemaphoreType.DMA((n * (NDEV - 1),)), pltpu.SemaphoreType.DMA((n * (NDEV - 1),)),
            pltpu.SemaphoreType.DMA((n,))]


def _gather_phases(ins, outs, send_sems, recv_sems, loc_sems):
    n = len(ins)
    per = NDEV - 1

    def place():
        x, y, c = lax.axis_index("x"), lax.axis_index("y"), lax.axis_index("c")
        return (x, y, c), (x, y, 1 - c), [(1 - x, y), (x, 1 - y), (1 - x, 1 - y)]

    def copy(w, k, block, to, src=None):
        dst = outs[w].at[4 * block[0] + 2 * block[1] + block[2]]
        return pltpu.make_async_remote_copy(
            src_ref=dst if src is None else src, dst_ref=dst,
            send_sem=send_sems.at[w * per + k], recv_sem=recv_sems.at[w * per + k],
            device_id=to, device_id_type=MESH_ID)

    def local(w, me):
        return pltpu.make_async_copy(ins[w], outs[w].at[4 * me[0] + 2 * me[1] + me[2]], loc_sems.at[w])

    def first(me, sib, chips):
        out = []
        for j, chip in enumerate(chips):
            out += [copy(w, 1 + j, me, (*chip, me[2]), src=ins[w]) for w in range(n)]
        return out + [copy(w, 0, me, sib, src=ins[w]) for w in range(n)]

    def start():
        me, sib, chips = place()
        for w in range(n):
            local(w, me).start()
        for cp in first(me, sib, chips):
            cp.start()

    def forward():
        me, sib, chips = place()
        for j, chip in enumerate(chips):
            for w in range(n):
                copy(w, 1 + j, (*chip, me[2]), me).wait_recv()
                copy(w, 4 + j, (*chip, me[2]), sib).start()

    def finish():
        me, sib, chips = place()
        for w in range(n):
            copy(w, 0, sib, me).wait_recv()
        for j, chip in enumerate(chips):
            for w in range(n):
                copy(w, 4 + j, (*chip, 1 - me[2]), me).wait_recv()
        for cp in first(me, sib, chips):
            cp.wait_send()
        for j, chip in enumerate(chips):
            for w in range(n):
                copy(w, 4 + j, (*chip, me[2]), sib).wait_send()
        for w in range(n):
            local(w, me).wait()

    return start, forward, finish


def _allgather(arrs, name):
    n = len(arrs)

    def body(*refs):
        start, forward, finish = _gather_phases(refs[:n], refs[n:2 * n], *refs[2 * n:])
        start()
        forward()
        finish()

    return pl.pallas_call(
        body, name=name, out_shape=tuple(jax.ShapeDtypeStruct((NDEV,) + a.shape, a.dtype) for a in arrs),
        in_specs=[ANY] * n, out_specs=tuple([ANY] * n), scratch_shapes=_gather_sems(n),
    )(*arrs)


def _exchange_and_gather(ex, ga, name):
    ne, ng = len(ex), len(ga)

    def body(*refs):
        ins, outs, sems = refs[:ne + ng], refs[ne + ng:2 * (ne + ng)], refs[2 * (ne + ng):]
        e_start, e_finish = _exchange_phases(ins[:ne], outs[:ne], *sems[:2])
        g_start, g_forward, g_finish = _gather_phases(ins[ne:], outs[ne:], *sems[2:])
        e_start()
        g_start()
        g_forward()
        g_finish()
        e_finish()

    res = pl.pallas_call(
        body, name=name,
        out_shape=tuple(jax.ShapeDtypeStruct(a.shape, a.dtype) for a in ex)
        + tuple(jax.ShapeDtypeStruct((NDEV,) + a.shape, a.dtype) for a in ga),
        in_specs=[ANY] * (ne + ng), out_specs=tuple([ANY] * (ne + ng)),
        scratch_shapes=_exchange_sems(ne) + _gather_sems(ng),
    )(*ex, *ga)
    return res[:ne], res[ne:]


def _ffn_fwd(xin, gin, bin_, w_in, w_out, name, gather=()):
    t = xin.shape[0]
    tm = _tile(t, 512)
    nt = t // tm
    chunks = _f_chunks()
    ng = len(gather)

    def body(*refs):
        x_ref, gi_ref, bi_ref, win_hbm, wout_hbm = refs[:5]
        g_ref, u_ref, xh_ref, rs_ref = refs[5 + ng:9 + ng]
        win_v, wout_v, acc_ref = refs[9 + 2 * ng:12 + 2 * ng]
        if ng:
            g_start, g_forward, g_finish = _gather_phases(refs[5:5 + ng], refs[9 + ng:9 + 2 * ng],
                                                          *refs[12 + 2 * ng:])

        @pl.when(pl.program_id(0) == 0)
        def _():
            if ng:
                g_start()
            pltpu.sync_copy(win_hbm, win_v)
            pltpu.sync_copy(wout_hbm, wout_v)

        if ng:
            pl.when(pl.program_id(0) == nt // 2)(g_forward)
            pl.when(pl.program_id(0) == nt - 1)(g_finish)

        x = x_ref[...] * gi_ref[...] + bi_ref[...]
        xb = x.astype(MXU_DT)
        for ci, (c0, fc) in enumerate(chunks):
            gc = _mm(xb, win_v[:, c0:c0 + fc])
            uc = _mm(xb, win_v[:, F + c0:F + c0 + fc])
            g_ref[:, c0:c0 + fc] = gc.astype(g_ref.dtype)
            u_ref[:, c0:c0 + fc] = uc.astype(u_ref.dtype)
            hc = (gc * _sigmoid(gc) * uc).astype(MXU_DT)
            part = _mm(hc, wout_v[c0:c0 + fc, :])
            if ci == 0:
                acc_ref[...] = part
            else:
                acc_ref[...] += part
        xh, rstd = _ln_fwd(ALPHA * x + 0.5 * acc_ref[...])
        xh_ref[...] = xh
        rs_ref[...] = rstd

    row = pl.BlockSpec((tm, D), lambda i: (i, 0))
    vec = pl.BlockSpec((1, D), lambda i: (0, 0))
    act = pl.BlockSpec((tm, F), lambda i: (i, 0))
    return pl.pallas_call(
        body, name=name, grid=(nt,),
        in_specs=[row, vec, vec, ANY, ANY] + [ANY] * ng,
        out_specs=(act, act, row, pl.BlockSpec((tm, 1), lambda i: (i, 0))) + (ANY,) * ng,
        out_shape=(jax.ShapeDtypeStruct((t, F), MXU_DT), jax.ShapeDtypeStruct((t, F), MXU_DT),
                   jax.ShapeDtypeStruct((t, D), F32), jax.ShapeDtypeStruct((t, 1), F32))
        + tuple(jax.ShapeDtypeStruct((NDEV,) + a.shape, a.dtype) for a in gather),
        scratch_shapes=[pltpu.VMEM((D, 2 * F), MXU_DT), pltpu.VMEM((F, D), MXU_DT), pltpu.VMEM((tm, D), F32)]
        + (_gather_sems(ng) if ng else []),
        compiler_params=_params(52),
    )(xin, gin, bin_, w_in, w_out, *gather)


QROWS = 80
BIAS_AT = DH


def _place_matrices():
    import numpy as np
    pk = np.zeros((NH, LANES, LANES), np.float32)
    pqt = np.zeros((NH, LANES, LANES), np.float32)
    for h in range(NH):
        for piece in range(3):
            pk[h, 8 * piece + h, BIAS_AT + 3 + piece] = -1.0
            pqt[h, BIAS_AT + piece, 8 * piece + h] = 1.0
    pkt = np.transpose(pk, (0, 2, 1))
    return tuple(jnp.asarray(m, MXU_DT) for m in (pk, pqt, pkt))


def _mix_proj_fwd(xh1, g1, b1, w_kv, w_qkv_t, w_bch, w_f, bf_pad, name):
    t = xh1.shape[0]
    tm = _tile(t, 512)
    pk, pqt, pkt = _place_matrices()

    def body(x_ref, g_ref, b_ref, wkv_ref, wt_ref, wb_ref, wf_ref, bf_ref, pk_ref, pqt_ref, pkt_ref,
             ka_ref, va_ref, qat_ref, kat_ref, vta_ref, bch_ref, z_ref, r_ref, carry):
        @pl.when(pl.program_id(0) == 0)
        def _():
            carry[...] = jnp.zeros_like(carry)

        xb = (x_ref[...] * g_ref[...] + b_ref[...]).astype(MXU_DT)
        kv = _mm(xb, wkv_ref[...])
        qkvt = _mm_nt(wt_ref[...], xb)
        bch_ref[...] = _mm(xb, wb_ref[...])
        z = _mm(xb, wf_ref[...]) + bf_ref[...]
        z_ref[...] = z
        logf = jnp.minimum(z, 0.0) - jnp.log(1.0 + jnp.exp(-jnp.abs(z)))
        row = lax.broadcasted_iota(jnp.int32, (tm, tm), 0)
        col = lax.broadcasted_iota(jnp.int32, (tm, tm), 1)
        tri = jnp.where(row >= col, 1.0, 0.0).astype(MXU_DT)
        c = carry[...] + _mm_sel(tri, logf)
        carry[...] = c[tm - 1:tm, :]
        r_ref[0] = c[0:1, :]
        lane = lax.broadcasted_iota(jnp.int32, (1, LANES), 1)
        hi, mid, lo = _split3(jnp.where(lane < NH, c - c[0:1, :], 0.0))
        pieces = (hi.astype(F32) + pltpu.roll(mid.astype(F32), 8, 1) + pltpu.roll(lo.astype(F32), 16, 1)
                  ).astype(MXU_DT)
        sub = lax.broadcasted_iota(jnp.int32, (DH, 1), 0)
        ones_k_lanes = jnp.where((lane >= BIAS_AT) & (lane < BIAS_AT + 3), 1.0, 0.0)
        ones_q_rows = jnp.where((sub >= 3) & (sub < 6), 1.0, 0.0)
        ones_k_rows = jnp.where(sub[0:QROWS - DH] < 3, 1.0, 0.0)
        first_row = jnp.where(sub == 0, 1.0, 0.0) + jnp.zeros((DH, tm), F32)
        for h in range(NH):
            pair, odd = divmod(h, 2)
            k2 = kv[:, LANES * pair:LANES * (pair + 1)]
            v2 = kv[:, DA + LANES * pair:DA + LANES * (pair + 1)]
            if odd:
                k2, v2 = pltpu.roll(k2, DH, 1), pltpu.roll(v2, DH, 1)
            ka_ref[h] = jnp.where(lane < DH, k2, _mm(pieces, pk_ref[h]) + ones_k_lanes).astype(ka_ref.dtype)
            va_ref[h] = jnp.where(lane < DH, v2, 0.0).astype(va_ref.dtype)
            qat_ref[h, 0:DH, :] = (qkvt[DH * h:DH * (h + 1)] * 0.125).astype(qat_ref.dtype)
            qat_ref[h, DH:LANES, :] = (_mm_nt(pqt_ref[h], pieces)[DH:LANES] + ones_q_rows).astype(qat_ref.dtype)
            kat_ref[h, 0:DH, :] = qkvt[DA + DH * h:DA + DH * (h + 1)].astype(kat_ref.dtype)
            kat_ref[h, DH:QROWS, :] = (_mm_nt(pkt_ref[h], pieces)[DH:QROWS] + ones_k_rows).astype(kat_ref.dtype)
            vt = qkvt[2 * DA + DH * h:2 * DA + DH * (h + 1)]
            vta_ref[h, 0:DH, :] = (first_row if odd else vt).astype(vta_ref.dtype)
            vta_ref[h, DH:LANES, :] = (vt if odd else first_row).astype(vta_ref.dtype)

    row = lambda w: pl.BlockSpec((tm, w), lambda i: (i, 0))
    full = lambda a: pl.BlockSpec(a.shape, lambda i: (0,) * a.ndim)
    nat = pl.BlockSpec((NH, tm, LANES), lambda i: (0, i, 0))
    fmaj = lambda rows: pl.BlockSpec((NH, rows, tm), lambda i: (0, 0, i))
    return pl.pallas_call(
        body, name=name, grid=(t // tm,),
        in_specs=[row(D), full(g1), full(b1), full(w_kv), full(w_qkv_t), full(w_bch), full(w_f), full(bf_pad),
                  full(pk), full(pqt), full(pkt)],
        out_specs=(nat, nat, fmaj(LANES), fmaj(QROWS), fmaj(LANES), row(3 * DCV), row(LANES),
                   pl.BlockSpec((1, 1, LANES), lambda i: (i, 0, 0))),
        out_shape=(jax.ShapeDtypeStruct((NH, t, LANES), MXU_DT), jax.ShapeDtypeStruct((NH, t, LANES), MXU_DT),
                   jax.ShapeDtypeStruct((NH, LANES, t), MXU_DT), jax.ShapeDtypeStruct((NH, QROWS, t), MXU_DT),
                   jax.ShapeDtypeStruct((NH, LANES, t), MXU_DT), jax.ShapeDtypeStruct((t, 3 * DCV), F32),
                   jax.ShapeDtypeStruct((t, LANES), F32), jax.ShapeDtypeStruct((t // tm, 1, LANES), F32)),
        scratch_shapes=[pltpu.VMEM((1, LANES), F32)],
        compiler_params=_params(56),
    )(xh1, g1, b1, w_kv, w_qkv_t, w_bch, w_f, bf_pad, pk, pqt, pkt)


def _attn_fwd(qat, ka, vta, r, name):
    t = ka.shape[1]
    tq = _tile(t, 512)
    nq = t // tq

    def body(r_ref, q_ref, k_ref, v_ref, o_ref, l_ref, st0, st1):
        hp, i = pl.program_id(0), pl.program_id(1)
        key = lax.broadcasted_iota(jnp.int32, (tq, tq), 0)
        qry = lax.broadcasted_iota(jnp.int32, (tq, tq), 1)

        def tile_of(pos):
            return jnp.where(pos == 0, i, pos - 1)

        def scores(pos, buf, masked):
            off = pl.multiple_of(tile_of(pos) * tq, tq)
            for a in range(2):
                st = _mm(k_ref[a, pl.ds(off, tq), :], q_ref[a])
                buf[a] = jnp.where(qry >= key, st, NEG) if masked else st

        def consume(pos, buf, carry):
            j = tile_of(pos)
            off = pl.multiple_of(j * tq, tq)
            out = []
            for a in range(2):
                m, acc = carry[a]
                st = buf[a]
                d = r_ref[2 * hp + a, i] - r_ref[2 * hp + a, j]
                m_new = jnp.maximum(m, jnp.max(st, axis=0, keepdims=True) + d)
                pt = jnp.exp(st - (m_new - d))
                acc = jnp.exp(m - m_new) * acc + _mm(v_ref[a, :, pl.ds(off, tq)], pt.astype(MXU_DT))
                out.append((m_new, acc))
            return tuple(out)

        def trip(p, carry):
            scores(2 * p + 1, st1, False)
            carry = consume(2 * p, st0, carry)
            scores(2 * p + 2, st0, False)
            return consume(2 * p + 1, st1, carry)

        scores(0, st0, True)
        init = tuple((jnp.full((1, tq), NEG, F32), jnp.zeros((LANES, tq), F32)) for _ in range(2))
        trips = i // 2
        carry = lax.fori_loop(0, trips, trip, init)

        def last_two(cr):
            scores(2 * trips + 1, st1, False)
            return consume(2 * trips + 1, st1, consume(2 * trips, st0, cr))

        (ma, acca), (mb, accb) = lax.cond(i % 2 == 1, last_two, lambda cr: consume(2 * trips, st0, cr), carry)
        la, lb = acca[DH:DH + 1, :], accb[0:1, :]
        l_ref[0] = ma + jnp.log(la)
        l_ref[1] = mb + jnp.log(lb)
        sub = lax.broadcasted_iota(jnp.int32, (LANES, tq), 0)
        o_ref[...] = jnp.where(sub < DH, acca / la, accb / lb).T

    return pl.pallas_call(
        body, name=name, grid=(NH // 2, nq),
        in_specs=[pl.BlockSpec(memory_space=pltpu.SMEM),
                  pl.BlockSpec((2, LANES, tq), lambda p, i: (p, 0, i)),
                  pl.BlockSpec((2, t, LANES), lambda p, i: (p, 0, 0)),
                  pl.BlockSpec((2, LANES, t), lambda p, i: (p, 0, 0))],
        out_specs=(pl.BlockSpec((tq, LANES), lambda p, i: (i, p)),
                   pl.BlockSpec((2, 1, tq), lambda p, i: (p, 0, i))),
        out_shape=(jax.ShapeDtypeStruct((t, DA), F32), jax.ShapeDtypeStruct((NH, 1, t), F32)),
        scratch_shapes=[pltpu.VMEM((2, tq, tq), F32), pltpu.VMEM((2, tq, tq), F32)],
        compiler_params=_params(48, 2),
    )(r, qat, ka, vta)


def _conv_parts(bch):
    return bch[:, 0:DCV], bch[:, DCV:2 * DCV], bch[:, 2 * DCV:3 * DCV]


def _mix_post_fwd(o, bch, conv_w, g_attn, g_conv, xh1, g1, b1, w_mo, name):
    t = o.shape[0]
    tm = _tile(t, 512)
    hb = tm // 8

    def body(o_ref, bch_ref, halo_ref, cw_ref, ga_ref, gc_ref, x_ref, g_ref, b_ref, w_ref,
             mg_ref, xh_ref, rs_ref, ext):
        i = pl.program_id(0)
        an, _ = _rms_fwd(o_ref[...])
        mg_ref[:, 0:DA] = (an * ga_ref[...]).astype(mg_ref.dtype)
        bb, cc, hh = _conv_parts(bch_ref[...])
        _, hc, hh_h = _conv_parts(halo_ref[...])
        u = cc * hh
        ext[0:8, :] = jnp.where(i > 0, hc * hh_h, 0.0)
        ext[8:8 + tm, :] = u
        raw = cw_ref[0:1, :] * ext[6:6 + tm, :] + cw_ref[1:2, :] * ext[7:7 + tm, :] + cw_ref[2:3, :] * u
        cn, _ = _rms_fwd(bb * raw)
        mg_ref[:, DA:D] = (cn * gc_ref[...]).astype(mg_ref.dtype)
        x1 = x_ref[...] * g_ref[...] + b_ref[...]
        xh, rstd = _ln_fwd(ALPHA * x1 + _mm(mg_ref[...], w_ref[...]))
        xh_ref[...] = xh
        rs_ref[...] = rstd

    row = lambda w: pl.BlockSpec((tm, w), lambda i: (i, 0))
    full = lambda a: pl.BlockSpec(a.shape, lambda i: (0, 0))
    return pl.pallas_call(
        body, name=name, grid=(t // tm,),
        in_specs=[row(DA), row(3 * DCV),
                  pl.BlockSpec((8, 3 * DCV), lambda i: (jnp.maximum(i * hb - 1, 0), 0)),
                  full(conv_w), full(g_attn), full(g_conv), row(D), full(g1), full(b1), full(w_mo)],
        out_specs=(row(D), row(D), pl.BlockSpec((tm, 1), lambda i: (i, 0))),
        out_shape=(jax.ShapeDtypeStruct((t, D), MXU_DT), jax.ShapeDtypeStruct((t, D), F32),
                   jax.ShapeDtypeStruct((t, 1), F32)),
        scratch_shapes=[pltpu.VMEM((tm + 8, DCV), F32)],
        compiler_params=_params(48),
    )(o, bch, bch, conv_w, g_attn, g_conv, xh1, g1, b1, w_mo)


def _tail(xh3, rs3, g3, b3, p, w_g, w_ple, bg, g4, b4, target, name):
    t = xh3.shape[0]
    tm = _tile(t, 512)

    def body(x_ref, rs_ref, g3_ref, b3_ref, p_ref, wg_ref, wp_ref, bg_ref, g4_ref, b4_ref, t_ref,
             dr_ref, dz_ref, de_ref, st_ref):
        @pl.when(pl.program_id(0) == 0)
        def _():
            st_ref[...] = jnp.zeros_like(st_ref)

        xh3v = x_ref[...]
        x3 = xh3v * g3_ref[...] + b3_ref[...]
        gate = _sigmoid(_mm(x3.astype(MXU_DT), wg_ref[...]) + bg_ref[...])
        e = _mm(p_ref[...].astype(MXU_DT), wp_ref[...])
        xh4, rstd4 = _ln_fwd(ALPHA * x3 + gate * e)
        diff = xh4 * g4_ref[...] + b4_ref[...] - t_ref[...]
        dy = diff * (1.0 / D)
        st_ref[5:6, :] += _colsum(diff * diff)
        st_ref[0:1, :] += _colsum(dy * xh4)
        st_ref[1:2, :] += _colsum(dy)
        dr4 = _ln_bwd(dy * g4_ref[...], xh4, rstd4)
        de_ref[...] = (dr4 * gate).astype(de_ref.dtype)
        dz = dr4 * e * gate * (1.0 - gate)
        st_ref[2:3, :] += _colsum(dz)
        dzb = dz.astype(MXU_DT)
        dz_ref[...] = dzb
        dx3 = ALPHA * dr4 + _mm_nt(dzb, wg_ref[...])
        st_ref[3:4, :] += _colsum(dx3 * xh3v)
        st_ref[4:5, :] += _colsum(dx3)
        dr_ref[...] = _ln_bwd(dx3 * g3_ref[...], xh3v, rs_ref[...])

    row = lambda w: pl.BlockSpec((tm, w), lambda i: (i, 0))
    full = lambda a: pl.BlockSpec(a.shape, lambda i: (0, 0))
    return pl.pallas_call(
        body, name=name, grid=(t // tm,),
        in_specs=[row(D), row(1), full(g3), full(b3), row(PLE), full(w_g), full(w_ple), full(bg), full(g4),
                  full(b4), row(D)],
        out_specs=(row(D), row(D), row(D), pl.BlockSpec((8, D), lambda i: (0, 0))),
        out_shape=(jax.ShapeDtypeStruct((t, D), F32), jax.ShapeDtypeStruct((t, D), MXU_DT),
                   jax.ShapeDtypeStruct((t, D), MXU_DT), jax.ShapeDtypeStruct((8, D), F32)),
        compiler_params=_params(48),
    )(xh3, rs3, g3, b3, p, w_g, w_ple, bg, g4, b4, target)


def _ffn_bwd(dr, gact, uact, xin, rsin, gin, w_in, w_out, prev_ln, name, exchange=()):
    t = dr.shape[0]
    tm = _tile(t, 512)
    nt = t // tm
    chunks = _f_chunks()
    ne = len(exchange)

    def body(*refs):
        dr_ref, g_ref, u_ref, x_ref, rs_ref, gi_ref, win_hbm, wout_hbm = refs[:8]
        df_ref, dg_ref, du_ref, dx_ref, st_ref = refs[8 + ne:13 + ne]
        win_v, wout_v = refs[13 + 2 * ne:15 + 2 * ne]
        acc_ref = dx_ref
        if ne:
            e_start, e_finish = _exchange_phases(refs[8:8 + ne], refs[13 + ne:13 + 2 * ne], *refs[15 + 2 * ne:])

        @pl.when(pl.program_id(0) == 0)
        def _():
            if ne:
                e_start()
            pltpu.sync_copy(win_hbm, win_v)
            pltpu.sync_copy(wout_hbm, wout_v)
            st_ref[...] = jnp.zeros_like(st_ref)

        if ne:
            pl.when(pl.program_id(0) == nt - 1)(e_finish)

        drv = dr_ref[...]
        dfb = (0.5 * drv).astype(MXU_DT)
        df_ref[...] = dfb
        for ci, (c0, fc) in enumerate(chunks):
            dh = _mm_nt(dfb, wout_v[c0:c0 + fc, :])
            g = g_ref[:, c0:c0 + fc].astype(F32)
            u = u_ref[:, c0:c0 + fc].astype(F32)
            sg = _sigmoid(g)
            dgb = (dh * u * (sg * (1.0 + g * (1.0 - sg)))).astype(MXU_DT)
            dub = (dh * (g * sg)).astype(MXU_DT)
            dg_ref[:, c0:c0 + fc] = dgb
            du_ref[:, c0:c0 + fc] = dub
            part = _mm_nt(dgb, win_v[:, c0:c0 + fc]) + _mm_nt(dub, win_v[:, F + c0:F + c0 + fc])
            if ci == 0:
                acc_ref[...] = part
            else:
                acc_ref[...] += part
        dx = ALPHA * drv + acc_ref[...]
        if prev_ln:
            xh = x_ref[...]
            st_ref[0:1, :] += _colsum(dx * xh)
            st_ref[1:2, :] += _colsum(dx)
            dx_ref[...] = _ln_bwd(dx * gi_ref[...], xh, rs_ref[...])
        else:
            dx_ref[...] = dx

    row = pl.BlockSpec((tm, D), lambda i: (i, 0))
    vec = pl.BlockSpec((1, D), lambda i: (0, 0))
    act = pl.BlockSpec((tm, F), lambda i: (i, 0))
    return pl.pallas_call(
        body, name=name, grid=(nt,),
        in_specs=[row, act, act, row, pl.BlockSpec((tm, 1), lambda i: (i, 0)), vec, ANY, ANY] + [ANY] * ne,
        out_specs=(row, act, act, row, pl.BlockSpec((8, D), lambda i: (0, 0))) + (ANY,) * ne,
        out_shape=(jax.ShapeDtypeStruct((t, D), MXU_DT), jax.ShapeDtypeStruct((t, F), MXU_DT),
                   jax.ShapeDtypeStruct((t, F), MXU_DT), jax.ShapeDtypeStruct((t, D), F32),
                   jax.ShapeDtypeStruct((8, D), F32))
        + tuple(jax.ShapeDtypeStruct(a.shape, a.dtype) for a in exchange),
        scratch_shapes=[pltpu.VMEM((D, 2 * F), MXU_DT), pltpu.VMEM((F, D), MXU_DT)]
        + (_exchange_sems(ne) if ne else []),
        compiler_params=_params(60),
    )(dr, gact, uact, xin, rsin, gin, w_in, w_out, *exchange)


def _mix_post_bwd(dr2, o, bch, conv_w, g_attn, g_conv, w_mo, name):
    t = dr2.shape[0]
    tm = _tile(t, 512)
    hb = tm // 8

    def body(dr_ref, o_ref, bch_ref, halo_ref, cw_ref, ga_ref, gc_ref, w_ref,
             dm_ref, do_ref, dl_ref, dy_ref, st_ref, ext):
        i = pl.program_id(0)

        @pl.when(i == 0)
        def _():
            st_ref[...] = jnp.zeros_like(st_ref)

        dmb = dr_ref[...].astype(MXU_DT)
        dm_ref[...] = dmb
        dmg = _mm_nt(dmb, w_ref[...])
        ov = o_ref[...]
        an, ra = _rms_fwd(ov)
        da = dmg[:, 0:DA]
        st_ref[0:1, :] += _colsum(da * an)
        dxa = _rms_bwd(da * ga_ref[...], an, ra)
        dor = dxa.astype(MXU_DT).astype(F32)
        dot = dor.T
        for h in range(NH):
            do_ref[h, 0:DH, :] = dot[DH * h:DH * (h + 1)].astype(do_ref.dtype)
            do_ref[h, DH:LANES, :] = jnp.zeros((LANES - DH, tm), do_ref.dtype)
        srow = lax.broadcasted_iota(jnp.int32, (8, DA), 0)
        scol = lax.broadcasted_iota(jnp.int32, (8, DA), 1)
        sel = jnp.where((scol // DH) == srow, 1.0, 0.0).astype(MXU_DT)
        hi, mid, lo = _split3(dor * ov)
        delta = _mm_nt(sel, hi) + _mm_nt(sel, mid) + _mm_nt(sel, lo)
        for h in range(NH):
            dl_ref[h] = delta[h:h + 1, :]
        bb, cc, hh = _conv_parts(bch_ref[...])
        _, hc, hh_h = _conv_parts(halo_ref[...])
        u = cc * hh
        ext[0:8, :] = jnp.where(i > 0, hc * hh_h, 0.0)
        ext[8:8 + tm, :] = u
        raw = cw_ref[0:1, :] * ext[6:6 + tm, :] + cw_ref[1:2, :] * ext[7:7 + tm, :] + cw_ref[2:3, :] * u
        cn, rc = _rms_fwd(bb * raw)
        dcn = dmg[:, DA:D]
        st_ref[1:2, :] += _colsum(dcn * cn)
        dy_ref[...] = _rms_bwd(dcn * gc_ref[...], cn, rc)

    row = lambda w: pl.BlockSpec((tm, w), lambda i: (i, 0))
    full = lambda a: pl.BlockSpec(a.shape, lambda i: (0, 0))
    return pl.pallas_call(
        body, name=name, grid=(t // tm,),
        in_specs=[row(D), row(DA), row(3 * DCV),
                  pl.BlockSpec((8, 3 * DCV), lambda i: (jnp.maximum(i * hb - 1, 0), 0)),
                  full(conv_w), full(g_attn), full(g_conv), full(w_mo)],
        out_specs=(row(D), pl.BlockSpec((NH, LANES, tm), lambda i: (0, 0, i)),
                   pl.BlockSpec((NH, 1, tm), lambda i: (0, 0, i)), row(DCV),
                   pl.BlockSpec((8, DA), lambda i: (0, 0))),
        out_shape=(jax.ShapeDtypeStruct((t, D), MXU_DT), jax.ShapeDtypeStruct((NH, LANES, t), MXU_DT),
                   jax.ShapeDtypeStruct((NH, 1, t), F32), jax.ShapeDtypeStruct((t, DCV), F32),
                   jax.ShapeDtypeStruct((8, DA), F32)),
        scratch_shapes=[pltpu.VMEM((tm + 8, DCV), F32)],
        compiler_params=_params(48),
    )(dr2, o, bch, bch, conv_w, g_attn, g_conv, w_mo)


def _conv_bwd(dy, bch, conv_w, name):
    t = dy.shape[0]
    tm = _tile(t, 512)
    hb = tm // 8
    nt = t // tm

    def body(dy_ref, dyn_ref, bch_ref, prev_ref, next_ref, cw_ref, out_ref, st_ref, ext_u, ext_d):
        i = pl.program_id(0)

        @pl.when(i == 0)
        def _():
            st_ref[...] = jnp.zeros_like(st_ref)

        bb, cc, hh = _conv_parts(bch_ref[...])
        _, pc, ph = _conv_parts(prev_ref[...])
        nb, _, _ = _conv_parts(next_ref[...])
        u = cc * hh
        ext_u[0:8, :] = jnp.where(i > 0, pc * ph, 0.0)
        ext_u[8:8 + tm, :] = u
        u1 = ext_u[7:7 + tm, :]
        u2 = ext_u[6:6 + tm, :]
        w0, w1, w2 = cw_ref[0:1, :], cw_ref[1:2, :], cw_ref[2:3, :]
        dyv = dy_ref[...]
        out_ref[:, 0:DCV] = (dyv * (w0 * u2 + w1 * u1 + w2 * u)).astype(out_ref.dtype)
        dcr = dyv * bb
        ext_d[0:tm, :] = dcr
        ext_d[tm:tm + 8, :] = jnp.where(i < nt - 1, dyn_ref[...] * nb, 0.0)
        du = w2 * dcr + w1 * ext_d[1:1 + tm, :] + w0 * ext_d[2:2 + tm, :]
        out_ref[:, DCV:2 * DCV] = (du * hh).astype(out_ref.dtype)
        out_ref[:, 2 * DCV:3 * DCV] = (du * cc).astype(out_ref.dtype)
        st_ref[0:1, :] += _colsum(dcr * u2)
        st_ref[1:2, :] += _colsum(dcr * u1)
        st_ref[2:3, :] += _colsum(dcr * u)

    row = lambda w: pl.BlockSpec((tm, w), lambda i: (i, 0))
    prev = lambda w: pl.BlockSpec((8, w), lambda i: (jnp.maximum(i * hb - 1, 0), 0))
    nxt = lambda w: pl.BlockSpec((8, w), lambda i: (jnp.minimum((i + 1) * hb, nt * hb - 1), 0))
    return pl.pallas_call(
        body, name=name, grid=(nt,),
        in_specs=[row(DCV), nxt(DCV), row(3 * DCV), prev(3 * DCV), nxt(3 * DCV),
                  pl.BlockSpec(conv_w.shape, lambda i: (0, 0))],
        out_specs=(row(3 * DCV), pl.BlockSpec((8, DCV), lambda i: (0, 0))),
        out_shape=(jax.ShapeDtypeStruct((t, 3 * DCV), MXU_DT), jax.ShapeDtypeStruct((8, DCV), F32)),
        scratch_shapes=[pltpu.VMEM((tm + 8, DCV), F32), pltpu.VMEM((tm + 8, DCV), F32)],
        compiler_params=_params(48),
    )(dy, dy, bch, bch, bch, conv_w)


def _attn_bwd(ka, kat, va, qat, dot, lrow, drow, r, name):
    t = ka.shape[1]
    tq = _tile(t, 512)
    nq = t // tq

    def body(r_ref, ka_ref, kat_ref, va_ref, l_ref, dl_ref, qat_v, dot_v,
             dk_ref, dv_ref, dck_ref, dqt_hbm, dcq_hbm, dq_acc):
        hp, j = pl.program_id(0), pl.program_id(1)

        @pl.when(j == 0)
        def _():
            dq_acc[...] = jnp.zeros_like(dq_acc)

        key = lax.broadcasted_iota(jnp.int32, (tq, tq), 0)
        qry = lax.broadcasted_iota(jnp.int32, (tq, tq), 1)

        def step(i, carry, masked):
            off = pl.multiple_of(i * tq, tq)
            out = []
            for a in range(2):
                dk, dv = carry[a]
                st = _mm(ka_ref[a], qat_v[a, :, pl.ds(off, tq)])
                dpt = _mm(va_ref[a], dot_v[a, :, pl.ds(off, tq)])
                if masked:
                    st = jnp.where(qry >= key, st, NEG)
                d = r_ref[2 * hp + a, i] - r_ref[2 * hp + a, j]
                pt = jnp.exp(st - (l_ref[a, :, pl.ds(off, tq)] - d))
                dsb = (pt * (dpt - dl_ref[a, :, pl.ds(off, tq)])).astype(MXU_DT)
                dv = dv + _mm_nt(dot_v[a, 0:DH, pl.ds(off, tq)], pt.astype(MXU_DT))
                dk = dk + _mm_nt(qat_v[a, 0:QROWS, pl.ds(off, tq)], dsb)
                dq_acc[a, :, pl.ds(off, tq)] += _mm(kat_ref[a], dsb)
                out.append((dk, dv))
            return tuple(out)

        init = tuple((jnp.zeros((QROWS, tq), F32), jnp.zeros((DH, tq), F32)) for _ in range(2))
        carry = step(j, init, True)
        (dka, dva), (dkb, dvb) = lax.fori_loop(j + 1, nq, lambda i, cr: step(i, cr, False), carry)
        dk_ref[...] = jnp.concatenate([dka[0:DH], dkb[0:DH]], axis=0).T.astype(dk_ref.dtype)
        dv_ref[...] = jnp.concatenate([dva, dvb], axis=0).T.astype(dv_ref.dtype)
        dck_ref[0] = -dka[DH + 3:DH + 4, :]
        dck_ref[1] = -dkb[DH + 3:DH + 4, :]

        @pl.when(j == nq - 1)
        def _():
            pltpu.sync_copy(dq_acc, dqt_hbm.at[pl.ds(2 * hp, 2)])
            pltpu.sync_copy(dq_acc.at[:, DH:DH + 1, :], dcq_hbm.at[pl.ds(2 * hp, 2)])

    pair = lambda rows, cols: pl.BlockSpec((2, rows, cols), lambda p, j: (p, 0, 0))
    return pl.pallas_call(
        body, name=name, grid=(NH // 2, nq),
        in_specs=[pl.BlockSpec(memory_space=pltpu.SMEM),
                  pl.BlockSpec((2, tq, LANES), lambda p, j: (p, j, 0)),
                  pl.BlockSpec((2, QROWS, tq), lambda p, j: (p, 0, j)),
                  pl.BlockSpec((2, tq, LANES), lambda p, j: (p, j, 0)),
                  pair(1, t), pair(1, t), pair(LANES, t), pair(LANES, t)],
        out_specs=(pl.BlockSpec((tq, LANES), lambda p, j: (j, p)),
                   pl.BlockSpec((tq, LANES), lambda p, j: (j, p)),
                   pl.BlockSpec((2, 1, tq), lambda p, j: (p, 0, j)), ANY, ANY),
        out_shape=(jax.ShapeDtypeStruct((t, DA), MXU_DT), jax.ShapeDtypeStruct((t, DA), MXU_DT),
                   jax.ShapeDtypeStruct((NH, 1, t), F32), jax.ShapeDtypeStruct((NH, QROWS, t), F32),
                   jax.ShapeDtypeStruct((NH, 1, t), F32)),
        scratch_shapes=[pltpu.VMEM((2, QROWS, t), F32)],
        compiler_params=_params(52, 2),
    )(r, ka, kat, va, lrow, drow, qat, dot)


def _mix_proj_bwd(dr2, dqt, dk, dv, dbch, dc, z, xh1, rs1, g1, w_qkv, w_bch, w_f, name, exchange=()):
    t = dr2.shape[0]
    tm = _tile(t, 512)
    nt = t // tm
    ne = len(exchange)

    def body(*refs):
        (dr_ref, dqt_ref, dk_ref, dv_ref, db_ref, dc_ref, z_ref, x_ref, rs_ref, g_ref,
         wq_ref, wb_ref, wf_ref) = refs[:13]
        out_ref, df_ref, dq_ref, st_ref = refs[13 + ne:17 + ne]
        carry = refs[17 + 2 * ne]
        if ne:
            e_start, e_finish = _exchange_phases(refs[13:13 + ne], refs[17 + ne:17 + 2 * ne], *refs[18 + 2 * ne:])
            pl.when(pl.program_id(0) == 0)(e_start)

        @pl.when(pl.program_id(0) == 0)
        def _():
            carry[...] = jnp.zeros_like(carry)
            st_ref[...] = jnp.zeros_like(st_ref)

        dq_ref[...] = (jnp.concatenate([dqt_ref[h, 0:DH, :] for h in range(NH)], axis=0).T * 0.125
                       ).astype(dq_ref.dtype)

        row = lax.broadcasted_iota(jnp.int32, (tm, tm), 0)
        col = lax.broadcasted_iota(jnp.int32, (tm, tm), 1)
        triu = jnp.where(col >= row, 1.0, 0.0).astype(MXU_DT)
        dlogf = carry[...] + _mm_sel(triu, dc_ref[...])
        carry[...] = dlogf[0:1, :]
        dz = dlogf / (1.0 + jnp.exp(z_ref[...]))
        st_ref[2:3, 0:LANES] += _colsum(dz)
        dfb = dz.astype(MXU_DT)
        df_ref[...] = dfb
        dx = (ALPHA * dr_ref[...]
              + _mm_nt(dq_ref[...], wq_ref[:, 0:DA])
              + _mm_nt(dk_ref[...], wq_ref[:, DA:2 * DA])
              + _mm_nt(dv_ref[...], wq_ref[:, 2 * DA:3 * DA])
              + _mm_nt(db_ref[...], wb_ref[...])
              + _mm_nt(dfb, wf_ref[...]))
        xh = x_ref[...]
        st_ref[0:1, :] += _colsum(dx * xh)
        st_ref[1:2, :] += _colsum(dx)
        out_ref[...] = _ln_bwd(dx * g_ref[...], xh, rs_ref[...])
        if ne:
            pl.when(pl.program_id(0) == nt - 1)(e_finish)

    row = lambda w: pl.BlockSpec((tm, w), lambda i: (nt - 1 - i, 0))
    full = lambda a: pl.BlockSpec(a.shape, lambda i: (0, 0))
    res = pl.pallas_call(
        body, name=name, grid=(nt,),
        in_specs=[row(D), pl.BlockSpec((NH, QROWS, tm), lambda i: (0, 0, nt - 1 - i)), row(DA), row(DA),
                  row(3 * DCV), row(LANES), row(LANES), row(D), row(1),
                  full(g1), full(w_qkv), full(w_bch), full(w_f)] + [ANY] * ne,
        out_specs=(row(D), row(LANES), row(DA), pl.BlockSpec((8, D), lambda i: (0, 0))) + (ANY,) * ne,
        out_shape=(jax.ShapeDtypeStruct((t, D), F32), jax.ShapeDtypeStruct((t, LANES), MXU_DT),
                   jax.ShapeDtypeStruct((t, DA), MXU_DT), jax.ShapeDtypeStruct((8, D), F32))
        + tuple(jax.ShapeDtypeStruct(a.shape, a.dtype) for a in exchange),
        scratch_shapes=[pltpu.VMEM((1, LANES), F32)] + (_exchange_sems(ne) if ne else []),
        compiler_params=_params(48),
    )(dr2, dqt, dk, dv, dbch, dc, z, xh1, rs1, g1, w_qkv, w_bch, w_f, *exchange)
    return res[:4] + (res[4:],)


def _dw(mode, a_parts, b, m, n, name, tmm=None, tn=None, exchange=()):
    t = b.shape[0]
    tmm = tmm or m
    tn = tn or n
    tt = _tile(t, 2048)
    na, ne = len(a_parts), len(exchange)
    grid = (m // tmm, n // tn, t // tt)

    def body(*refs):
        a_refs, b_ref, o_ref = refs[:na], refs[na], refs[na + 1 + ne]
        if ne:
            e_start, e_finish = _exchange_phases(refs[na + 1:na + 1 + ne], refs[na + 2 + ne:na + 2 + 2 * ne],
                                                 *refs[na + 2 + 2 * ne:])
            at = lambda steps: functools.reduce(jnp.logical_and, [pl.program_id(d) == s for d, s in enumerate(steps)])
            pl.when(at((0, 0, 0)))(e_start)

        @pl.when(pl.program_id(2) == 0)
        def _():
            o_ref[...] = jnp.zeros_like(o_ref)

        if mode == "plain":
            a = a_refs[0][...].astype(MXU_DT)
        elif mode == "affine":
            a = (a_refs[0][...] * a_refs[1][...] + a_refs[2][...]).astype(MXU_DT)
        else:
            g = a_refs[0][...].astype(F32)
            a = (g * _sigmoid(g) * a_refs[1][...].astype(F32)).astype(MXU_DT)
        o_ref[...] += _mm_tn(a, b_ref[...].astype(MXU_DT))
        if ne:
            pl.when(at(tuple(g - 1 for g in grid)))(e_finish)

    a_tile = pl.BlockSpec((tt, tmm), lambda i, j, k: (k, i))
    a_vec = pl.BlockSpec((1, tmm), lambda i, j, k: (0, i))
    a_specs = {"plain": [a_tile], "affine": [a_tile, a_vec, a_vec], "swiglu": [a_tile, a_tile]}[mode]
    res = pl.pallas_call(
        body, name=name, grid=grid,
        in_specs=a_specs + [pl.BlockSpec((tt, tn), lambda i, j, k: (k, j))] + [ANY] * ne,
        out_specs=(pl.BlockSpec((tmm, tn), lambda i, j, k: (i, j)),) + (ANY,) * ne,
        out_shape=(jax.ShapeDtypeStruct((m, n), F32),)
        + tuple(jax.ShapeDtypeStruct(a.shape, a.dtype) for a in exchange),
        scratch_shapes=_exchange_sems(ne) if ne else [],
        compiler_params=_params(52, 3),
    )(*a_parts, b, *exchange)
    return res if ne else res[0]


def _dw_shared(a_parts, bs, name, exchange=()):
    xh, g, b = a_parts
    t, m = xh.shape
    tt = _tile(t, 1024)
    nk = t // tt
    nb, ne = len(bs), len(exchange)

    def body(*refs):
        x_ref, g_ref, b_ref = refs[:3]
        b_refs, o_refs = refs[3:3 + nb], refs[3 + nb + ne:3 + 2 * nb + ne]
        if ne:
            e_start, e_finish = _exchange_phases(refs[3 + nb:3 + nb + ne], refs[3 + 2 * nb + ne:3 + 2 * nb + 2 * ne],
                                                 *refs[3 + 2 * nb + 2 * ne:])
            pl.when(pl.program_id(0) == 0)(e_start)

        @pl.when(pl.program_id(0) == 0)
        def _():
            for o_ref in o_refs:
                o_ref[...] = jnp.zeros_like(o_ref)

        at = (x_ref[...] * g_ref[...] + b_ref[...]).T.astype(MXU_DT)
        for b_ref, o_ref in zip(b_refs, o_refs):
            o_ref[...] += _mm(at, b_ref[...].astype(MXU_DT))
        if ne:
            pl.when(pl.program_id(0) == nk - 1)(e_finish)

    vec = pl.BlockSpec((1, m), lambda k: (0, 0))
    res = pl.pallas_call(
        body, name=name, grid=(nk,),
        in_specs=[pl.BlockSpec((tt, m), lambda k: (k, 0)), vec, vec]
        + [pl.BlockSpec((tt, x.shape[1]), lambda k: (k, 0)) for x in bs] + [ANY] * ne,
        out_specs=tuple(pl.BlockSpec((m, x.shape[1]), lambda k: (0, 0)) for x in bs) + (ANY,) * ne,
        out_shape=tuple(jax.ShapeDtypeStruct((m, x.shape[1]), F32) for x in bs)
        + tuple(jax.ShapeDtypeStruct(a.shape, a.dtype) for a in exchange),
        scratch_shapes=_exchange_sems(ne) if ne else [],
        compiler_params=_params(56),
    )(xh, g, b, *bs, *exchange)
    return res[:nb], res[nb:]


def _adamw(w, g, m, v):
    m = ADAM_B1 * m + (1.0 - ADAM_B1) * g
    v = ADAM_B2 * v + (1.0 - ADAM_B2) * (g * g)
    m_hat = m / (1.0 - ADAM_B1 ** ADAM_STEP)
    v_hat = v / (1.0 - ADAM_B2 ** ADAM_STEP)
    delta = -ADAM_LR * (m_hat / (jnp.sqrt(v_hat) + ADAM_EPS) + ADAM_WD * w)
    return delta, m, v


def _reduce_adamw(landed, own, w, m, v, name):
    r, c = own.shape
    tr = _tile(r, 128)

    def body(l_ref, o_ref, w_ref, m_ref, v_ref, g_out, d_out, m_out, v_out):
        me = 4 * lax.axis_index("x") + 2 * lax.axis_index("y") + lax.axis_index("c")
        g = None
        for j in range(NDEV):
            term = jnp.where(me == j, o_ref[...], l_ref[j].astype(F32))
            g = term if g is None else g + term
        g_out[...] = g
        d_out[...], m_out[...], v_out[...] = _adamw(w_ref[...], g, m_ref[...], v_ref[...])

    blk = pl.BlockSpec((tr, c), lambda i: (i, 0))
    sds = jax.ShapeDtypeStruct((r, c), F32)
    return pl.pallas_call(
        body, name=name, grid=(r // tr,),
        in_specs=[pl.BlockSpec((NDEV, tr, c), lambda i: (0, i, 0)), blk, blk, blk, blk],
        out_specs=(blk, blk, blk, blk), out_shape=(sds, sds, sds, sds),
        compiler_params=_params(40),
    )(landed, own, w, m, v)


def _sum_small(gathered, name):
    _, r, c = gathered.shape

    def body(g_ref, o_ref):
        acc = g_ref[0]
        for j in range(1, NDEV):
            acc = acc + g_ref[j]
        o_ref[...] = acc

    return pl.pallas_call(body, name=name, out_shape=jax.ShapeDtypeStruct((r, c), F32))(gathered)


def _adamw_small(small_g, conv_g, params, name):
    names = list(params)
    flat = [a for nm in names for a in params[nm]]

    def body(*refs):
        g_ref, c_ref = refs[0], refs[1]
        ins, outs = refs[2:2 + len(flat)], refs[2 + len(flat):]
        for k, nm in enumerate(names):
            w_ref, m_ref, v_ref = ins[3 * k:3 * k + 3]
            if nm == "conv_w":
                g = c_ref[...]
            else:
                r, off, wd = SMALL_SLOTS[nm]
                g = g_ref[r:r + 1, off:off + wd]
            g_out, d_out, m_out, v_out = outs[4 * k:4 * k + 4]
            g_out[...] = g
            d_out[...], m_out[...], v_out[...] = _adamw(w_ref[...], g, m_ref[...], v_ref[...])

    res = pl.pallas_call(
        body, name=name,
        out_shape=tuple(jax.ShapeDtypeStruct(params[nm][0].shape, F32) for nm in names for _ in range(4)),
    )(small_g, conv_g, *flat)
    return {nm: res[4 * k:4 * k + 4] for k, nm in enumerate(names)}


def _cols_from_stack(s):
    return jnp.transpose(s, (1, 0, 2)).reshape(s.shape[1], NDEV * s.shape[2])


def _cols_to_stack(w):
    r, c = w.shape
    return jnp.transpose(w.reshape(r, NDEV, c // NDEV), (1, 0, 2))


def _rows_from_stack(s):
    return s.reshape(NDEV * s.shape[1], s.shape[2])


def _rows_to_stack(w):
    r, c = w.shape
    return w.reshape(NDEV, r // NDEV, c)


SMALL_ROWS = 16
SMALL_SLOTS = {
    "ln1_g": (0, 0, D), "ln1_b": (1, 0, D), "ln2_g": (2, 0, D), "ln2_b": (3, 0, D), "ln3_g": (4, 0, D),
    "ln3_b": (5, 0, D), "b_ple_gate": (6, 0, D), "ln4_g": (7, 0, D), "ln4_b": (8, 0, D),
    "g_attn": (9, 0, DA), "g_conv": (9, DA, DCV), "b_forget": (10, 0, NH),
}
CONVW_ROW = 11
LOSS_SLOT = (10, LANES)


def _pack_small_grads(st_tail, st_proj, st_f2, st_post, st_conv, name):
    def body(t_ref, p_ref, f_ref, a_ref, c_ref, o_ref):
        rows = {"ln1_g": p_ref[0:1, :], "ln1_b": p_ref[1:2, :], "ln2_g": f_ref[0:1, :], "ln2_b": f_ref[1:2, :],
                "ln3_g": t_ref[3:4, :], "ln3_b": t_ref[4:5, :], "b_ple_gate": t_ref[2:3, :],
                "ln4_g": t_ref[0:1, :], "ln4_b": t_ref[1:2, :], "g_attn": a_ref[0:1, :], "g_conv": a_ref[1:2, :],
                "b_forget": p_ref[2:3, 0:NH]}
        o_ref[...] = jnp.zeros_like(o_ref)
        for nm, (r, off, wd) in SMALL_SLOTS.items():
            o_ref[r:r + 1, off:off + wd] = rows[nm]
        o_ref[CONVW_ROW:CONVW_ROW + 3, 0:DCV] = c_ref[0:3, :]
        loss = (0.5 / D) * jnp.sum(t_ref[5:6, :], axis=1, keepdims=True)
        lane = lax.broadcasted_iota(jnp.int32, (1, D), 1)
        r = LOSS_SLOT[0]
        o_ref[r:r + 1, :] = jnp.where(lane == LOSS_SLOT[1], loss, o_ref[r:r + 1, :])

    return pl.pallas_call(body, name=name, out_shape=jax.ShapeDtypeStruct((SMALL_ROWS, D), F32))(
        st_tail, st_proj, st_f2, st_post, st_conv)


def kernel(x, p, ffn1_w_in, ffn1_w_out, ln1_g, ln1_b, w_mix_in, b_forget, conv_w, g_attn, g_conv, w_mix_out, ln2_g, ln2_b, ffn2_w_in, ffn2_w_out, ln3_g, ln3_b, w_ple, w_ple_gate, b_ple_gate, ln4_g, ln4_b, loss_target, m_ffn1_w_in, m_ffn1_w_out, m_ln1_g, m_ln1_b, m_w_mix_in, m_b_forget, m_conv_w, m_g_attn, m_g_conv, m_w_mix_out, m_ln2_g, m_ln2_b, m_ffn2_w_in, m_ffn2_w_out, m_ln3_g, m_ln3_b, m_w_ple, m_w_ple_gate, m_b_ple_gate, m_ln4_g, m_ln4_b, v_ffn1_w_in, v_ffn1_w_out, v_ln1_g, v_ln1_b, v_w_mix_in, v_b_forget, v_conv_w, v_g_attn, v_g_conv, v_w_mix_out, v_ln2_g, v_ln2_b, v_ffn2_w_in, v_ffn2_w_out, v_ln3_g, v_ln3_b, v_w_ple, v_w_ple_gate, v_b_ple_gate, v_ln4_g, v_ln4_b):
    args = dict(locals())
    t = x.shape[1]
    me = 4 * lax.axis_index("x") + 2 * lax.axis_index("y") + lax.axis_index("c")
    x0 = x.reshape(t, D)
    p0 = p.reshape(t, PLE)
    tgt = loss_target.reshape(t, D)

    big = ["ffn1_w_in", "ffn1_w_out", "w_mix_in", "w_mix_out", "ffn2_w_in", "ffn2_w_out", "w_ple", "w_ple_gate"]
    col_sharded = {"ffn1_w_in", "w_mix_in", "ffn2_w_in", "w_ple"}
    shard = {nm: args[nm][0] for nm in big}

    unstack = lambda nm, g: (_cols_from_stack(g) if nm in col_sharded else _rows_from_stack(g)).astype(MXU_DT)
    stack = lambda nm, g: _cols_to_stack(g) if nm in col_sharded else _rows_to_stack(g)
    wire = lambda names: [shard[nm].astype(WIRE_DT) for nm in names]
    first, later = big[:2], big[2:]

    full = {nm: unstack(nm, g) for nm, g in zip(first, _allgather(wire(first), "ag_ffn1"))}
    ffn1_out, gathered = _ffn1_fwd(x0, full, wire(later) + [conv_w[0]])
    full.update({nm: unstack(nm, g) for nm, g in zip(later, gathered)})
    cw = _cols_from_stack(gathered[len(later)])

    slots = lambda nm, g: stack(nm, g).astype(WIRE_DT)
    dr1, gw, small_part, landed = _mid_step(p0, tgt, full, cw, {nm: args[nm] for nm in SMALL_SLOTS}, ffn1_out, slots)
    gx, gw1, _, (landed["w_mix_in"],), landed["ffn1_w_in"] = _ffn1_bwd(
        x0, dr1, ffn1_out, full, (), [slots("w_mix_in", gw["w_mix_in"])], lambda g: slots("ffn1_w_in", g))
    stacks = {nm: stack(nm, g) for nm, g in {**gw, **gw1}.items()}
    (landed["ffn1_w_out"],), (small_all,) = _exchange_and_gather([stacks["ffn1_w_out"].astype(WIRE_DT)],
                                                                 [small_part], "rs_ffn1_out")
    small_g = _sum_small(small_all, "sum_small")
    loss = small_g[LOSS_SLOT[0], LOSS_SLOT[1]]

    outs = {"loss": loss, "grad_x": gx.reshape(1, t, D)}
    for nm in big:
        own = lax.dynamic_index_in_dim(stacks[nm], me, axis=0, keepdims=False)
        g, dl, mn, vn = _reduce_adamw(landed[nm], own, shard[nm], args["m_" + nm][0], args["v_" + nm][0],
                                      "adamw_" + nm)
        outs["grad_" + nm], outs["delta_" + nm], outs["new_m_" + nm], outs["new_v_" + nm] = (
            g[None], dl[None], mn[None], vn[None])
    cshard = lax.dynamic_slice_in_dim(small_g[CONVW_ROW:CONVW_ROW + 3, 0:DCV], me * (DCV // NDEV), DCV // NDEV, axis=1)
    params = {nm: tuple(args[pre + nm] for pre in ("", "m_", "v_")) for nm in SMALL_SLOTS}
    params["conv_w"] = tuple(args[pre + "conv_w"][0] for pre in ("", "m_", "v_"))
    for nm, res in _adamw_small(small_g, cshard, params, "adamw_small").items():
        for key, val in zip(("grad_", "delta_", "new_m_", "new_v_"), res):
            outs[key + nm] = val[None] if nm == "conv_w" else val

    wnames = ["ffn1_w_in", "ffn1_w_out", "ln1_g", "ln1_b", "w_mix_in", "b_forget", "conv_w", "g_attn", "g_conv",
              "w_mix_out", "ln2_g", "ln2_b", "ffn2_w_in", "ffn2_w_out", "ln3_g", "ln3_b", "w_ple", "w_ple_gate",
              "b_ple_gate", "ln4_g", "ln4_b"]
    return (outs["loss"], outs["grad_x"], *[outs[pre + nm] for pre in ("grad_", "delta_", "new_m_", "new_v_")
                                            for nm in wnames])


def _ffn1_fwd(x0, full, gather=()):
    res = _ffn_fwd(x0, jnp.ones((1, D), F32), jnp.zeros((1, D), F32), full["ffn1_w_in"], full["ffn1_w_out"],
                   "ffn1_fwd", gather)
    return res[:4], res[4:]


def _ffn1_bwd(x0, dr1, ffn1_out, full, exchange=(), exchange_late=(), w_in_slots=None):
    g1a, u1a, _, rs1 = ffn1_out
    ones, zeros = jnp.ones((1, D), F32), jnp.zeros((1, D), F32)
    res = _ffn_bwd(dr1, g1a, u1a, x0, rs1, ones, full["ffn1_w_in"], full["ffn1_w_out"], False, "ffn1_bwd",
                   exchange)
    df1, dg1, du1, gx = res[:4]
    dw_g = _dw("affine", (x0, ones, zeros), dg1, D, F, "dw_ffn1_in_g", tn=F // 2, exchange=exchange_late)
    dw_g, landed_late = (dw_g[0], dw_g[1:]) if exchange_late else (dw_g, ())
    gw_in = jnp.concatenate([dw_g, _dw("affine", (x0, ones, zeros), du1, D, F, "dw_ffn1_in_u", tn=F // 2)], axis=1)
    side = () if w_in_slots is None else (w_in_slots(gw_in),)
    out = _dw("swiglu", (g1a, u1a), df1, F, D, "dw_ffn1_out", tmm=F // 2, exchange=side)
    gw_out, landed_in = (out, None) if w_in_slots is None else (out[0], out[1])
    return gx, {"ffn1_w_in": gw_in, "ffn1_w_out": gw_out}, res[5:], landed_late, landed_in


def _mid_step(p0, tgt, full, cw, sp, ffn1_out, slots):
    g1a, u1a, xh1, rs1 = ffn1_out
    t = xh1.shape[0]
    ln1_g, ln1_b, ln2_g, ln2_b, ln3_g, ln3_b = (sp[k] for k in ("ln1_g", "ln1_b", "ln2_g", "ln2_b", "ln3_g", "ln3_b"))
    ln4_g, ln4_b, g_attn, g_conv, b_ple_gate = (sp[k] for k in ("ln4_g", "ln4_b", "g_attn", "g_conv", "b_ple_gate"))
    wmi = full["w_mix_in"]
    w_qkv = wmi[:, 0:3 * DA]
    w_f = jnp.pad(wmi[:, 3 * DA:3 * DA + NH], ((0, 0), (0, LANES - NH)))
    w_bch = wmi[:, 3 * DA + NH:]
    bf_pad = jnp.pad(sp["b_forget"], ((0, 0), (0, LANES - NH)))

    ka, va, qat, kat, vta, bch, z, rt = _mix_proj_fwd(xh1, ln1_g, ln1_b, w_qkv[:, DA:], jnp.transpose(w_qkv),
                                                      w_bch, w_f, bf_pad, "mix_proj_fwd")
    rtile = jnp.transpose(rt[:, 0, 0:NH])
    o, lse = _attn_fwd(qat, ka, vta, rtile, "attn_fwd")
    merged, xh2, rs2 = _mix_post_fwd(o, bch, cw, g_attn, g_conv, xh1, ln1_g, ln1_b, full["w_mix_out"],
                                     "mix_post_fwd")
    g2a, u2a, xh3, rs3 = _ffn_fwd(xh2, ln2_g, ln2_b, full["ffn2_w_in"], full["ffn2_w_out"], "ffn2_fwd")

    dr3, dz, de, st_tail = _tail(xh3, rs3, ln3_g, ln3_b, p0, full["w_ple_gate"], full["w_ple"], b_ple_gate,
                                 ln4_g, ln4_b, tgt, "tail")
    df2, dg2, du2, dr2, st_f2 = _ffn_bwd(dr3, g2a, u2a, xh2, rs2, ln2_g, full["ffn2_w_in"], full["ffn2_w_out"],
                                         True, "ffn2_bwd")
    dmix, dot, drow, dyc, st_post = _mix_post_bwd(dr2, o, bch, cw, g_attn, g_conv, full["w_mix_out"],
                                                  "mix_post_bwd")
    dbch, st_conv = _conv_bwd(dyc, bch, cw, "conv_bwd")
    dk, dv, dck, dqt, dcq = _attn_bwd(ka, kat, va, qat, dot, lse, drow, rtile, "attn_bwd")
    dc_pad = jnp.pad(jnp.transpose((dcq + dck).reshape(NH, t)), ((0, 0), (0, LANES - NH)))

    x1p, x2p, x3p = (xh1, ln1_g, ln1_b), (xh2, ln2_g, ln2_b), (xh3, ln3_g, ln3_b)
    gw = {}
    gw["ffn2_w_in"] = jnp.concatenate(
        [_dw("affine", x2p, dg2, D, F, "dw_ffn2_in_g", tn=F // 2),
         _dw("affine", x2p, du2, D, F, "dw_ffn2_in_u", tn=F // 2)], axis=1)
    gw["ffn2_w_out"] = _dw("swiglu", (g2a, u2a), df2, F, D, "dw_ffn2_out", tmm=F // 2)
    gw["w_mix_out"] = _dw("plain", (merged,), dmix, D, D, "dw_mix_out")
    gw["w_ple_gate"] = _dw("affine", x3p, dz, D, D, "dw_ple_gate")
    gw["w_ple"] = _dw("plain", (p0,), de, PLE, D, "dw_ple")
    early_a, early_b = ("ffn2_w_in", "w_ple_gate", "w_ple"), ("ffn2_w_out", "w_mix_out")
    travel = lambda names: [slots(nm, gw[nm]) for nm in names] if slots else []
    dr1, dfl, dq, st_proj, landed_a = _mix_proj_bwd(dr2, dqt, dk, dv, dbch, dc_pad, z, xh1, rs1, ln1_g, w_qkv, w_bch,
                                                    w_f, "mix_proj_bwd", travel(early_a))
    (gq, gk, gv, gf, gbch), landed_b = _dw_shared(x1p, (dq, dk, dv, dfl, dbch), "dw_mix_in", travel(early_b))
    gw["w_mix_in"] = jnp.concatenate([gq, gk, gv, gf[:, 0:NH], gbch], axis=1)
    landed = dict(zip(early_a + early_b, tuple(landed_a) + tuple(landed_b)))

    small_part = _pack_small_grads(st_tail, st_proj, st_f2, st_post, st_conv, "pack_small")
    return dr1, gw, small_part, landed
```
